```python
import math
import numpy as np
import jax
import jax.numpy as jnp
from jax import lax

D_MODEL = 1024
BATCH = 16
SEQ = 2048
DEPTH = 1

NSA_HEADS = 8
NSA_KV_GROUPS = 2
NSA_REP = NSA_HEADS // NSA_KV_GROUPS
NSA_HD = 64
CMP_LEN = 32
CMP_STRIDE = 16
CMP_HIDDEN = 2 * NSA_HD
SEL_BLOCK = 64
SEL_TOPN = 8
SEL_FORCED_LOCAL = 2
WINDOW = 512
DIFF_HEADS = 8
DIFF_HD = 32
D_MIX = NSA_HEADS * NSA_HD + DIFF_HEADS * 2 * DIFF_HD
REL_BUCKETS = 32
REL_MAX_EXACT = 16
REL_MAX_DIST = 128
N_REL_HEADS = NSA_HEADS + DIFF_HEADS
Q_BLOCK = 128
SEL_Q_BLOCK = 64
MOE_GROUPS = 4
EXPERTS_PER_GROUP = 8
N_EXPERTS = MOE_GROUPS * EXPERTS_PER_GROUP
EXPERT_FF = 256
MOE_TOPK = 2
EPS = 1e-6
NEG = -1e30
NSA_Q = NSA_HEADS * NSA_HD
NSA_KV = NSA_KV_GROUPS * NSA_HD
NSA_GATE = NSA_HEADS * 3
DIFF_QK = DIFF_HEADS * 2 * DIFF_HD
DIFF_V = DIFF_HEADS * 2 * DIFF_HD
IN_SIZES = (NSA_Q, NSA_KV, NSA_KV, NSA_KV, NSA_KV, NSA_KV, NSA_KV, NSA_GATE, DIFF_QK, DIFF_QK, DIFF_V)
D_IN = sum(IN_SIZES)

kernel_name = 'hybrid_nsa_diffattn_hiermoe'


def rmsnorm(x, g):
    xf = x.astype(jnp.float32)
    y = xf * lax.rsqrt(jnp.mean(xf * xf, axis=-1, keepdims=True) + EPS)
    return (y * g.astype(jnp.float32)).astype(x.dtype)


def t5_bucket(dist):
    n = jnp.maximum(dist, 0)
    nf = jnp.maximum(n, 1).astype(jnp.float32)
    large = REL_MAX_EXACT + (jnp.log(nf / REL_MAX_EXACT) / math.log(REL_MAX_DIST / REL_MAX_EXACT)
                             * (REL_BUCKETS - REL_MAX_EXACT)).astype(jnp.int32)
    large = jnp.minimum(large, REL_BUCKETS - 1)
    return jnp.where(n < REL_MAX_EXACT, n, large)


def masked_softmax(s, mask):
    s = jnp.where(mask, s.astype(jnp.float32), NEG)
    p = jax.nn.softmax(s, axis=-1)
    return jnp.where(mask, p, 0.0)


def merge_blocks(out, axis):
    out = jnp.moveaxis(out, 0, axis)
    shp = out.shape
    return out.reshape(shp[:axis] + (shp[axis] * shp[axis + 1],) + shp[axis + 2:])


def nsa_mixer(q, kc, vc, ks, vs, kw, vw, gate_logits, tab_a, pos_k, pos_v, ck1, ck2, cv1, cv2):
    B, S, _ = q.shape
    G, R, hd = NSA_KV_GROUPS, NSA_REP, NSA_HD
    scale = hd ** -0.5
    tpos = jnp.arange(S, dtype=jnp.int32)
    tab = tab_a.reshape(REL_BUCKETS, G, R)
    q = q.reshape(B, S, G, R, hd).transpose(0, 2, 3, 1, 4)

    def kv_heads(t):
        return t.reshape(B, S, G, hd).transpose(0, 2, 1, 3)
    kc, vc, ks, vs, kw, vw = (kv_heads(t) for t in (kc, vc, ks, vs, kw, vw))

    n_cmp = (S - CMP_LEN) // CMP_STRIDE + 1
    starts = np.arange(n_cmp) * CMP_STRIDE
    blk_idx = starts[:, None] + np.arange(CMP_LEN)[None, :]

    def compress(t, pos, w1, w2):
        blocks = (t[:, :, blk_idx] + pos).reshape(B, G, n_cmp, CMP_LEN * hd)
        return jax.nn.gelu(blocks @ w1) @ w2
    k_cmp = compress(kc, pos_k, ck1, ck2)
    v_cmp = compress(vc, pos_v, cv1, cv2)
    cmp_end = jnp.asarray(starts + CMP_LEN - 1, dtype=jnp.int32)
    dist_c = tpos[:, None] - cmp_end[None, :]
    s_c = jnp.einsum('bgrtd,bgcd->bgrtc', q, k_cmp) * scale + tab[t5_bucket(dist_c)].transpose(2, 3, 0, 1)
    p_cmp = masked_softmax(s_c, dist_c >= 0)
    o_cmp = jnp.einsum('bgrtc,bgcd->bgrtd', p_cmp.astype(v_cmp.dtype), v_cmp)

    n_sel = S // SEL_BLOCK
    sel_start = np.arange(n_sel) * SEL_BLOCK
    overlap = ((starts[:, None] <= sel_start[None, :] + SEL_BLOCK - 1)
               & (starts[:, None] + CMP_LEN - 1 >= sel_start[None, :])).astype(np.float32)
    imp = jnp.einsum('bgrtc,cj->bgtj', p_cmp, jnp.asarray(overlap))
    blk = jnp.arange(n_sel, dtype=jnp.int32)[None, :]
    cur = (tpos // SEL_BLOCK)[:, None]
    valid = blk <= cur
    forced = (blk == 0) | ((cur - blk >= 0) & (cur - blk < SEL_FORCED_LOCAL))
    score = jnp.where(valid & forced, 1e9, jnp.where(valid, imp, -1e9))
    n_top = min(SEL_TOPN, n_sel)
    top_val, top_idx = lax.top_k(score, n_top)
    top_ok = top_val > -1e8

    ks_blk = ks.reshape(B, G, n_sel, SEL_BLOCK, hd)
    vs_blk = vs.reshape(B, G, n_sel, SEL_BLOCK, hd)
    bi = jnp.arange(B)[:, None, None, None]
    gi = jnp.arange(G)[None, :, None, None]
    tab_g = tab.transpose(1, 0, 2)

    def sel_block(c):
        t0 = c * SEL_Q_BLOCK
        qc = lax.dynamic_slice_in_dim(q, t0, SEL_Q_BLOCK, axis=3)
        ic = lax.dynamic_slice_in_dim(top_idx, t0, SEL_Q_BLOCK, axis=2)
        okc = lax.dynamic_slice_in_dim(top_ok, t0, SEL_Q_BLOCK, axis=2)
        kg = ks_blk[bi, gi, ic].reshape(B, G, SEL_Q_BLOCK, n_top * SEL_BLOCK, hd)
        vg = vs_blk[bi, gi, ic].reshape(B, G, SEL_Q_BLOCK, n_top * SEL_BLOCK, hd)
        kpos = (ic[..., None] * SEL_BLOCK + jnp.arange(SEL_BLOCK, dtype=jnp.int32)).reshape(
            B, G, SEL_Q_BLOCK, n_top * SEL_BLOCK)
        tq = t0 + jnp.arange(SEL_Q_BLOCK, dtype=jnp.int32)
        dist = tq[None, None, :, None] - kpos
        mask = jnp.repeat(okc, SEL_BLOCK, axis=-1) & (dist >= 0)
        bias = jnp.moveaxis(tab_g[gi, t5_bucket(dist)], -1, 2)
        s = jnp.einsum('bgrqd,bgqkd->bgrqk', qc, kg) * scale + bias
        p = masked_softmax(s, mask[:, :, None])
        return jnp.einsum('bgrqk,bgqkd->bgrqd', p.astype(vg.dtype), vg)
    o_slc = merge_blocks(lax.map(sel_block, jnp.arange(S // SEL_Q_BLOCK)), 3)

    pad = ((0, 0), (0, 0), (WINDOW, 0), (0, 0))
    kw_pad = jnp.pad(kw, pad)
    vw_pad = jnp.pad(vw, pad)
    span = Q_BLOCK + WINDOW

    def win_block(c):
        t0 = c * Q_BLOCK
        qc = lax.dynamic_slice_in_dim(q, t0, Q_BLOCK, axis=3)
        kb = lax.dynamic_slice_in_dim(kw_pad, t0, span, axis=2)
        vb = lax.dynamic_slice_in_dim(vw_pad, t0, span, axis=2)
        tq = t0 + jnp.arange(Q_BLOCK, dtype=jnp.int32)
        kpos = t0 - WINDOW + jnp.arange(span, dtype=jnp.int32)
        dist = tq[:, None] - kpos[None, :]
        mask = (kpos[None, :] >= 0) & (dist >= 0) & (dist < WINDOW)
        bias = tab[t5_bucket(dist)].transpose(2, 3, 0, 1)
        s = jnp.einsum('bgrqd,bgkd->bgrqk', qc, kb) * scale + bias
        p = masked_softmax(s, mask)
        return jnp.einsum('bgrqk,bgkd->bgrqd', p.astype(vb.dtype), vb)
    o_win = merge_blocks(lax.map(win_block, jnp.arange(S // Q_BLOCK)), 3)

    g = jax.nn.sigmoid(gate_logits.astype(jnp.float32)).astype(o_win.dtype)
    g = g.reshape(B, S, G, R, 3).transpose(0, 2, 3, 1, 4)
    o = g[..., 0:1] * o_cmp + g[..., 1:2] * o_slc + g[..., 2:3] * o_win
    return o.transpose(0, 3, 1, 2, 4).reshape(B, S, NSA_HEADS * hd)


def diff_mixer(q, k, v, tab_b, lq1, lk1, lq2, lk2, subln, lambda_init):
    B, S, _ = q.shape
    H, d = DIFF_HEADS, DIFF_HD
    scale = d ** -0.5
    q = q.reshape(B, S, H, 2, d).transpose(3, 0, 2, 1, 4)
    k = k.reshape(B, S, H, 2, d).transpose(3, 0, 2, 1, 4)
    v = v.reshape(B, S, H, 2 * d).transpose(0, 2, 1, 3)
    lam = (jnp.exp(jnp.sum((lq1 * lk1).astype(jnp.float32)))
           - jnp.exp(jnp.sum((lq2 * lk2).astype(jnp.float32))) + lambda_init)
    kpos = jnp.arange(S, dtype=jnp.int32)

    def blk(c):
        t0 = c * Q_BLOCK
        qc = lax.dynamic_slice_in_dim(q, t0, Q_BLOCK, axis=3)
        tq = t0 + jnp.arange(Q_BLOCK, dtype=jnp.int32)
        dist = tq[:, None] - kpos[None, :]
        bias = tab_b[t5_bucket(dist)].transpose(2, 0, 1)
        s = jnp.einsum('ibhqd,ibhkd->ibhqk', qc, k) * scale + bias
        p = masked_softmax(s, dist >= 0)
        a = p[0] - lam * p[1]
        return jnp.einsum('bhqk,bhkd->bhqd', a.astype(v.dtype), v)
    o = merge_blocks(lax.map(blk, jnp.arange(S // Q_BLOCK)), 2)
    o = rmsnorm(o, subln) * (1.0 - lambda_init)
    return o.transpose(0, 2, 1, 3).reshape(B, S, H * 2 * d)


def hier_moe(h, wg, bg, we, be, w_gate, w_up, w_down):
    B, S, D = h.shape
    t = h.reshape(B * S, D)
    T = t.shape[0]
    pg = jax.nn.softmax((t @ wg).astype(jnp.float32) + bg.astype(jnp.float32), axis=-1)
    g_prob, g_idx = lax.top_k(pg, 1)
    e_logit = ((t @ we).astype(jnp.float32) + be.astype(jnp.float32)).reshape(T, MOE_GROUPS, EXPERTS_PER_GROUP)
    e_in = jnp.take_along_axis(e_logit, g_idx[:, :, None], axis=1)[:, 0]
    pe = jax.nn.softmax(e_in, axis=-1)
    e_prob, e_idx = lax.top_k(pe, MOE_TOPK)
    w = g_prob * e_prob / jnp.sum(e_prob, axis=-1, keepdims=True)
    expert_id = g_idx * EXPERTS_PER_GROUP + e_idx
    combine = jnp.sum(jax.nn.one_hot(expert_id, N_EXPERTS, dtype=jnp.float32) * w[..., None], axis=1)
    combine = combine.astype(h.dtype)
    y = jnp.zeros_like(t)
    for gidx in range(MOE_GROUPS):
        sl = slice(gidx * EXPERTS_PER_GROUP, (gidx + 1) * EXPERTS_PER_GROUP)
        hg = jax.nn.silu(jnp.einsum('td,edf->tef', t, w_gate[sl])) * jnp.einsum('td,edf->tef', t, w_up[sl])
        y = y + jnp.einsum('tef,efd->td', hg * combine[:, sl, None], w_down[sl])
    return y.reshape(B, S, D)


def setup_inputs(seed: int = 0) -> dict:
    key = jax.random.key(seed)
    ks = jax.random.split(key, 32)

    def nrm(k, shape, scale):
        return jax.random.normal(k, shape, jnp.float32) * scale

    def gain(k, shape):
        return 1.0 + 0.01 * jax.random.normal(k, shape, jnp.float32)
    return {
        'x': nrm(ks[0], (BATCH, SEQ, D_MODEL), 1.0),
        'rel_bias': nrm(ks[1], (REL_BUCKETS, N_REL_HEADS), 0.5),
        'ln_mix': gain(ks[2], (DEPTH, D_MODEL)),
        'w_in': nrm(ks[3], (DEPTH, D_MODEL, D_IN), D_MODEL ** -0.5),
        'cmp_pos_k': nrm(ks[4], (DEPTH, CMP_LEN, NSA_HD), 0.1),
        'cmp_pos_v': nrm(ks[5], (DEPTH, CMP_LEN, NSA_HD), 0.1),
        'cmp_k_w1': nrm(ks[6], (DEPTH, CMP_LEN * NSA_HD, CMP_HIDDEN), (CMP_LEN * NSA_HD) ** -0.5),
        'cmp_k_w2': nrm(ks[7], (DEPTH, CMP_HIDDEN, NSA_HD), CMP_HIDDEN ** -0.5),
        'cmp_v_w1': nrm(ks[8], (DEPTH, CMP_LEN * NSA_HD, CMP_HIDDEN), (CMP_LEN * NSA_HD) ** -0.5),
        'cmp_v_w2': nrm(ks[9], (DEPTH, CMP_HIDDEN, NSA_HD), CMP_HIDDEN ** -0.5),
        'diff_lq1': nrm(ks[10], (DEPTH, DIFF_HD), 0.1),
        'diff_lk1': nrm(ks[11], (DEPTH, DIFF_HD), 0.1),
        'diff_lq2': nrm(ks[12], (DEPTH, DIFF_HD), 0.1),
        'diff_lk2': nrm(ks[13], (DEPTH, DIFF_HD), 0.1),
        'diff_subln': gain(ks[14], (DEPTH, 2 * DIFF_HD)),
        'w_out': nrm(ks[15], (DEPTH, D_MIX, D_MODEL), D_MIX ** -0.5),
        'ln_ffn': gain(ks[16], (DEPTH, D_MODEL)),
        'router_group_w': nrm(ks[17], (DEPTH, D_MODEL, MOE_GROUPS), D_MODEL ** -0.5),
        'router_group_b': nrm(ks[18], (DEPTH, MOE_GROUPS), 0.01),
        'router_expert_w': nrm(ks[19], (DEPTH, D_MODEL, N_EXPERTS), D_MODEL ** -0.5),
        'router_expert_b': nrm(ks[20], (DEPTH, N_EXPERTS), 0.01),
        'exp_w_gate': nrm(ks[21], (DEPTH, N_EXPERTS, D_MODEL, EXPERT_FF), D_MODEL ** -0.5),
        'exp_w_up': nrm(ks[22], (DEPTH, N_EXPERTS, D_MODEL, EXPERT_FF), D_MODEL ** -0.5),
        'exp_w_down': nrm(ks[23], (DEPTH, N_EXPERTS, EXPERT_FF, D_MODEL), EXPERT_FF ** -0.5),
        'ln_final': gain(ks[24], (D_MODEL,)),
    }


def reference(x, rel_bias, ln_mix, w_in, cmp_pos_k, cmp_pos_v, cmp_k_w1, cmp_k_w2, cmp_v_w1, cmp_v_w2,
              diff_lq1, diff_lk1, diff_lq2, diff_lk2, diff_subln, w_out, ln_ffn,
              router_group_w, router_group_b, router_expert_w, router_expert_b,
              exp_w_gate, exp_w_up, exp_w_down, ln_final):
    split_points = [int(v) for v in np.cumsum(IN_SIZES)[:-1]]
    tab_a = rel_bias[:, :NSA_HEADS]
    tab_b = rel_bias[:, NSA_HEADS:]
    h = x
    for l in range(DEPTH):
        u = rmsnorm(h, ln_mix[l]) @ w_in[l]
        q_a, kc, vc, ks, vs, kw, vw, gates, q_b, k_b, v_b = jnp.split(u, split_points, axis=-1)
        o_a = nsa_mixer(q_a, kc, vc, ks, vs, kw, vw, gates, tab_a, cmp_pos_k[l], cmp_pos_v[l],
                        cmp_k_w1[l], cmp_k_w2[l], cmp_v_w1[l], cmp_v_w2[l])
        lambda_init = 0.8 - 0.6 * math.exp(-0.3 * l)
        o_b = diff_mixer(q_b, k_b, v_b, tab_b, diff_lq1[l], diff_lk1[l], diff_lq2[l], diff_lk2[l],
                         diff_subln[l], lambda_init)
        h = h + jnp.concatenate([o_a, o_b], axis=-1) @ w_out[l]
        h = h + hier_moe(rmsnorm(h, ln_ffn[l]), router_group_w[l], router_group_b[l],
                         router_expert_w[l], router_expert_b[l], exp_w_gate[l], exp_w_up[l], exp_w_down[l])
    return rmsnorm(h, ln_final)
```

```python
import functools
import math

import numpy as np
import jax
import jax.numpy as jnp
from jax import lax
from jax.experimental import pallas as pl
from jax.experimental.pallas import tpu as pltpu

F32 = jnp.float32
BF16 = jnp.bfloat16
NEG = -1e30
EPS = 1e-6

D_MODEL = 1024
LANE = 128
NSA_HEADS, NSA_G, NSA_R, NSA_HD = 8, 2, 4, 64
CMP_LEN, CMP_STRIDE, CMP_HIDDEN = 32, 16, 128
SEL_BLOCK, SEL_TOPN, SEL_FORCED_LOCAL, WINDOW = 64, 8, 2, 512
DIFF_HEADS, DIFF_HD = 8, 32
REL_BUCKETS, REL_MAX_EXACT, REL_MAX_DIST = 32, 16, 128
N_REL_HEADS = NSA_HEADS + DIFF_HEADS
MOE_GROUPS, EPG, N_EXPERTS, EXPERT_FF = 4, 8, 32, 256
LAMBDA_INIT = 0.8 - 0.6 * math.exp(-0.3 * 0)

TQ = 256
TM_PROJ = 512
TM_MOE = 1024
VMEM_LIMIT = 48 * 1024 * 1024


def _cparams(sem):
    return pltpu.CompilerParams(dimension_semantics=sem, vmem_limit_bytes=VMEM_LIMIT)


def _dot(a, b):
    return jnp.dot(a, b, preferred_element_type=F32)


def _dot_nt(a, b):
    return lax.dot_general(a, b, (((1,), (1,)), ((), ())), preferred_element_type=F32)


def _bucket_thresholds():
    n = np.arange(0, REL_MAX_DIST + 1)
    nf = np.maximum(n, 1).astype(np.float32)
    large = REL_MAX_EXACT + (np.log(nf / np.float32(REL_MAX_EXACT)) / np.float32(math.log(REL_MAX_DIST / REL_MAX_EXACT))
                             * np.float32(REL_BUCKETS - REL_MAX_EXACT)).astype(np.int32)
    large = np.minimum(large, REL_BUCKETS - 1)
    bucket = np.where(n < REL_MAX_EXACT, n, large)
    return [int(np.argmax(bucket >= b)) for b in range(REL_BUCKETS)]


_THR = _bucket_thresholds()


def _inproj_kernel(x_ref, g_ref, w_ref, qa_ref, kc_ref, vc_ref, ks_ref, vs_ref, kw_ref, vw_ref,
                   qb_ref, kb_ref, vb_ref, gt_ref):
    x = x_ref[...]
    xn = (x * lax.rsqrt(jnp.mean(x * x, axis=-1, keepdims=True) + EPS) * g_ref[...]).astype(BF16)
    a = _dot(xn, w_ref[:, 0:512])
    for r in range(4):
        qa_ref[r] = a[:, r * LANE:(r + 1) * LANE].astype(BF16)
    a = _dot(xn, w_ref[:, 512:1280])
    kc_ref[...] = a[:, 0:128]
    vc_ref[...] = a[:, 128:256]
    for i, ref in enumerate((ks_ref, vs_ref, kw_ref, vw_ref)):
        ref[...] = a[:, 256 + i * LANE:384 + i * LANE].astype(BF16)
    for j, ref in enumerate((qb_ref, kb_ref, vb_ref)):
        a = _dot(xn, w_ref[:, 1280 + j * 512:1792 + j * 512])
        for r in range(4):
            ref[r] = a[:, r * LANE:(r + 1) * LANE].astype(BF16)
    gt_ref[...] = _dot(xn, w_ref[:, 2816:2944])


def _in_proj(x2, g, w):
    T = x2.shape[0]
    tm = TM_PROJ
    row = lambda i: (i, 0)
    row3 = lambda i: (0, i, 0)
    o128b = jax.ShapeDtypeStruct((T, LANE), BF16)
    o128f = jax.ShapeDtypeStruct((T, LANE), F32)
    o4 = jax.ShapeDtypeStruct((4, T, LANE), BF16)
    s128 = pl.BlockSpec((tm, LANE), row)
    s4 = pl.BlockSpec((4, tm, LANE), row3)
    return pl.pallas_call(
        _inproj_kernel,
        grid=(T // tm,),
        in_specs=[pl.BlockSpec((tm, D_MODEL), row),
                  pl.BlockSpec((1, D_MODEL), lambda i: (0, 0)),
                  pl.BlockSpec(w.shape, lambda i: (0, 0))],
        out_specs=[s4, s128, s128, s128, s128, s128, s128, s4, s4, s4, s128],
        out_shape=[o4, o128f, o128f, o128b, o128b, o128b, o128b, o4, o4, o4, o128f],
        compiler_params=_cparams(("parallel",)),
        name="in_proj",
    )(x2, g, w)


def _gelu_tanh(x):
    return 0.5 * x * (1.0 + jnp.tanh(math.sqrt(2.0 / math.pi) * (x + 0.044715 * (x * x * x))))


def _compress_kernel(kc_ref, vc_ref, posk_ref, posv_ref, w1k_ref, w1v_ref, w2k_ref, w2v_ref, ko_ref, vo_ref):
    nrow = kc_ref.shape[0] // CMP_STRIDE
    rid = lax.broadcasted_iota(jnp.int32, (nrow, 1), 0)
    for src, pos, w1, w2, out in ((kc_ref, posk_ref, w1k_ref, w2k_ref, ko_ref),
                                  (vc_ref, posv_ref, w1v_ref, w2v_ref, vo_ref)):
        hid_a = jnp.zeros((nrow, 2 * CMP_HIDDEN), F32)
        hid_b = jnp.zeros((nrow, 2 * CMP_HIDDEN), F32)
        for m in range(CMP_STRIDE):
            y = src[pl.ds(m, nrow, stride=CMP_STRIDE), :]
            hid_a = hid_a + _dot((y + pos[m:m + 1, :]).astype(BF16), w1[m])
            hid_b = hid_b + _dot((y + pos[CMP_STRIDE + m:CMP_STRIDE + m + 1, :]).astype(BF16), w1[CMP_STRIDE + m])
        hid = hid_a + pltpu.roll(hid_b, nrow - 1, 0)
        o = _dot(_gelu_tanh(hid).astype(BF16), w2[...])
        out[0] = jnp.where(rid < nrow - 1, o, 0.0).astype(BF16)


def _compress(kc, vc, posk, posv, w1k, w1v, w2k, w2v, B, S):
    nrow = S // CMP_STRIDE
    full = lambda a: pl.BlockSpec(a.shape, lambda b: (0,) * a.ndim)
    src = pl.BlockSpec((S, LANE), lambda b: (b, 0))
    osp = pl.BlockSpec((1, nrow, LANE), lambda b: (b, 0, 0))
    osh = jax.ShapeDtypeStruct((B, nrow, LANE), BF16)
    return pl.pallas_call(
        _compress_kernel,
        grid=(B,),
        in_specs=[src, src, full(posk), full(posv), full(w1k), full(w1v), full(w2k), full(w2v)],
        out_specs=[osp, osp],
        out_shape=[osh, osh],
        compiler_params=_cparams(("parallel",)),
        name="compress",
    )(kc, vc, posk, posv, w1k, w1v, w2k, w2v)


def _bias_from_dist(dist, tab_ref, h):
    val = jnp.full(dist.shape, tab_ref[h, 0], F32)
    for b in range(1, REL_BUCKETS):
        val = jnp.where(dist >= _THR[b], tab_ref[h, b], val)
    return val


def _bias_near_kernel(tab_ref, out_ref):
    h = pl.program_id(0)
    i = lax.broadcasted_iota(jnp.int32, (TQ, TQ), 0)
    j = lax.broadcasted_iota(jnp.int32, (TQ, TQ), 1)
    for d in range(2):
        out_ref[0, d] = _bias_from_dist(i - j + d * TQ, tab_ref, h) - tab_ref[h, REL_BUCKETS - 1]


def _bias_cmp_kernel(tab_ref, out_ref):
    h = pl.program_id(0)
    t = pl.program_id(1) * TQ + lax.broadcasted_iota(jnp.int32, (TQ, LANE), 0)
    c = lax.broadcasted_iota(jnp.int32, (TQ, LANE), 1)
    out_ref[0] = _bias_from_dist(t - (c * CMP_STRIDE + CMP_LEN - 1), tab_ref, h)


def _bias_tiles(tab_t, S):
    smem = pl.BlockSpec(memory_space=pltpu.SMEM)
    near = pl.pallas_call(
        _bias_near_kernel,
        grid=(N_REL_HEADS,),
        in_specs=[smem],
        out_specs=pl.BlockSpec((1, 2, TQ, TQ), lambda h: (h, 0, 0, 0)),
        out_shape=jax.ShapeDtypeStruct((N_REL_HEADS, 2, TQ, TQ), F32),
        compiler_params=_cparams(("parallel",)),
        name="bias_near",
    )(tab_t)
    cmp_bias = pl.pallas_call(
        _bias_cmp_kernel,
        grid=(NSA_HEADS, S // TQ),
        in_specs=[smem],
        out_specs=pl.BlockSpec((1, TQ, LANE), lambda h, q: (h, q, 0)),
        out_shape=jax.ShapeDtypeStruct((NSA_HEADS, S, LANE), F32),
        compiler_params=_cparams(("parallel", "parallel")),
        name="bias_cmp",
    )(tab_t)
    return near, cmp_bias


def _flash_init(m_ref, l_ref, acc_ref):
    m_ref[...] = jnp.full(m_ref.shape, NEG, F32)
    l_ref[...] = jnp.zeros(l_ref.shape, F32)
    acc_ref[...] = jnp.zeros(acc_ref.shape, F32)


def _flash_update(s, v, m_ref, l_ref, acc_ref):
    m_old = m_ref[...]
    m_new = jnp.maximum(m_old, jnp.max(s, axis=-1, keepdims=True))
    alpha = jnp.exp(m_old - m_new)
    p = jnp.exp(s - m_new)
    l_ref[...] = alpha * l_ref[...] + jnp.sum(p, axis=-1, keepdims=True)
    acc_ref[...] = alpha * acc_ref[...] + _dot(p.astype(BF16), v)
    m_ref[...] = m_new


def _tile(ref, kt):
    return ref[pl.ds(pl.multiple_of(kt * TQ, TQ), TQ), :]


def _nsa_kernel(qa_ref, gt_ref, kcmp_ref, vcmp_ref, bc_ref, ks_ref, vs_ref, kw_ref, vw_ref, dn_ref,
                ov_ref, e3_ref, o_ref, m_ref, l_ref, acc_ref, psum_ref, sel_ref, oacc_ref):
    qi = pl.program_id(1)
    t0 = qi * TQ
    lane = lax.broadcasted_iota(jnp.int32, (1, LANE), 1)
    lane_grp = lax.shift_right_arithmetic(lane, 6)
    ri = lax.broadcasted_iota(jnp.int32, (TQ, TQ), 0)
    ci = lax.broadcasted_iota(jnp.int32, (TQ, TQ), 1)
    sig = jax.nn.sigmoid(gt_ref[...])

    def gate_col(c):
        return jnp.sum(jnp.where(lane == c, sig, 0.0), axis=-1, keepdims=True)

    def masked_q(r, g):
        return (qa_ref[r].astype(F32) * jnp.where(lane_grp == g, 0.125, 0.0)).astype(BF16)

    oacc_ref[...] = jnp.zeros(oacc_ref.shape, F32)
    psum_ref[...] = jnp.zeros(psum_ref.shape, F32)

    row_t = t0 + lax.broadcasted_iota(jnp.int32, (TQ, 1), 0)
    n_cmp = kcmp_ref.shape[1] - 1
    cmp_end = jnp.where(lane < n_cmp, lane * CMP_STRIDE + CMP_LEN - 1, 1 << 30)
    mask_c = row_t >= cmp_end

    def cmp_body(hid, carry):
        g = hid // NSA_R
        r = hid % NSA_R
        s = _dot_nt(masked_q(r, g), kcmp_ref[0]) + bc_ref[hid]
        s = jnp.where(mask_c, s, NEG)
        p = jnp.where(mask_c, jnp.exp(s - jnp.max(s, axis=-1, keepdims=True)), 0.0)
        l = jnp.sum(p, axis=-1, keepdims=True)
        p = p * jnp.where(l > 0.0, 1.0 / l, 0.0)
        o = _dot(p.astype(BF16), vcmp_ref[0])
        psum_ref[g] += p
        oacc_ref[r] += jnp.where(lane_grp == g, gate_col(hid * 3) * o, 0.0)
        return carry
    lax.fori_loop(0, NSA_HEADS, cmp_body, 0)

    n_sel = ov_ref.shape[0]
    jj = lax.broadcasted_iota(jnp.int32, (n_sel, TQ), 0)
    cur = lax.shift_right_arithmetic(t0 + lax.broadcasted_iota(jnp.int32, (n_sel, TQ), 1), 6)
    valid = jj <= cur
    forced = (jj == 0) | (cur - jj < SEL_FORCED_LOCAL)
    for g in range(NSA_G):
        ps = psum_ref[g]
        hi = ps.astype(BF16)
        rem = ps - hi.astype(F32)
        mid = rem.astype(BF16)
        lo = (rem - mid.astype(F32)).astype(BF16)
        ov = ov_ref[...]
        imp = _dot_nt(ov, hi) + _dot_nt(ov, mid) + _dot_nt(ov, lo)
        score = jnp.where(valid, jnp.where(forced, 1e9, imp), -1e9)
        cnt = jnp.zeros((n_sel, TQ), F32)
        for j2 in range(n_sel):
            row = score[j2:j2 + 1, :]
            tie = jnp.where(jj > j2, 1.0, 0.0)
            cnt = cnt + jnp.where(row > score, 1.0, jnp.where(row == score, tie, 0.0))
        sel_t = jnp.where(cnt < float(min(SEL_TOPN, n_sel)), jnp.where(score > -1e8, 1.0, 0.0), 0.0)
        sel_t = jnp.concatenate([sel_t, jnp.zeros((LANE - n_sel, TQ), F32)], axis=0)
        sel_ref[g] = sel_t.T.astype(BF16)

    def main_body(hid, carry):
        g = hid // NSA_R
        r = hid % NSA_R
        qm = masked_q(r, g)
        selg = sel_ref[g]

        def selected(kt):
            return _dot(selg, e3_ref[kt]) > 0.5

        _flash_init(m_ref, l_ref, acc_ref)

        def far(kt, c):
            s = jnp.where(selected(kt), _dot_nt(qm, _tile(ks_ref, kt)), NEG)
            _flash_update(s, _tile(vs_ref, kt), m_ref, l_ref, acc_ref)
            return c
        lax.fori_loop(0, jnp.maximum(qi - 1, 0), far, 0)

        @pl.when(qi >= 1)
        def _():
            kt = qi - 1
            s = jnp.where(selected(kt), _dot_nt(qm, _tile(ks_ref, kt)) + dn_ref[hid, 1], NEG)
            _flash_update(s, _tile(vs_ref, kt), m_ref, l_ref, acc_ref)

        s = _dot_nt(qm, _tile(ks_ref, qi)) + dn_ref[hid, 0]
        s = jnp.where(selected(qi), jnp.where(ri >= ci, s, NEG), NEG)
        _flash_update(s, _tile(vs_ref, qi), m_ref, l_ref, acc_ref)
        o = acc_ref[...] * (1.0 / l_ref[...])
        oacc_ref[r] += jnp.where(lane_grp == g, gate_col(hid * 3 + 1) * o, 0.0)

        _flash_init(m_ref, l_ref, acc_ref)
        s = jnp.where(ri >= ci, _dot_nt(qm, _tile(kw_ref, qi)) + dn_ref[hid, 0], NEG)
        _flash_update(s, _tile(vw_ref, qi), m_ref, l_ref, acc_ref)

        @pl.when(qi >= 1)
        def _():
            kt = qi - 1
            s = _dot_nt(qm, _tile(kw_ref, kt)) + dn_ref[hid, 1]
            _flash_update(s, _tile(vw_ref, kt), m_ref, l_ref, acc_ref)

        @pl.when(qi >= WINDOW // TQ)
        def _():
            kt = qi - WINDOW // TQ
            s = jnp.where(ci > ri, _dot_nt(qm, _tile(kw_ref, kt)), NEG)
            _flash_update(s, _tile(vw_ref, kt), m_ref, l_ref, acc_ref)
        o = acc_ref[...] * (1.0 / l_ref[...])
        oacc_ref[r] += jnp.where(lane_grp == g, gate_col(hid * 3 + 2) * o, 0.0)
        return carry
    lax.fori_loop(0, NSA_HEADS, main_body, 0)

    for r in range(NSA_R):
        o_ref[r] = oacc_ref[r].astype(BF16)


def _nsa(qa, gates, kcmp, vcmp, bias_c, ks, vs, kw, vw, near_a, ov_t, e3, B, S):
    nq = S // TQ
    T = B * S
    nrow = kcmp.shape[1]
    kv = pl.BlockSpec((S, LANE), lambda b, q: (b, 0))
    full = lambda a: pl.BlockSpec(a.shape, lambda b, q: (0,) * a.ndim)
    cmp_spec = pl.BlockSpec((1, nrow, LANE), lambda b, q: (b, 0, 0))
    return pl.pallas_call(
        _nsa_kernel,
        grid=(B, nq),
        in_specs=[pl.BlockSpec((4, TQ, LANE), lambda b, q: (0, b * nq + q, 0)),
                  pl.BlockSpec((TQ, LANE), lambda b, q: (b * nq + q, 0)),
                  cmp_spec, cmp_spec,
                  pl.BlockSpec((NSA_HEADS, TQ, LANE), lambda b, q: (0, q, 0)),
                  kv, kv, kv, kv, full(near_a), full(ov_t), full(e3)],
        out_specs=pl.BlockSpec((4, TQ, LANE), lambda b, q: (0, b * nq + q, 0)),
        out_shape=jax.ShapeDtypeStruct((4, T, LANE), BF16),
        scratch_shapes=[pltpu.VMEM((TQ, 1), F32), pltpu.VMEM((TQ, 1), F32), pltpu.VMEM((TQ, LANE), F32),
                        pltpu.VMEM((NSA_G, TQ, LANE), F32), pltpu.VMEM((NSA_G, TQ, LANE), BF16),
                        pltpu.VMEM((NSA_R, TQ, LANE), F32)],
        compiler_params=_cparams(("parallel", "arbitrary")),
        name="nsa",
    )(qa, gates, kcmp, vcmp, bias_c, ks, vs, kw, vw, near_a, ov_t, e3)


def _diff_kernel(lq1_ref, lk1_ref, lq2_ref, lk2_ref, sub_ref, qb_ref, kb_ref, vb_ref, dn_ref, o_ref,
                 m0_ref, l0_ref, a0_ref, m1_ref, l1_ref, a1_ref, oacc_ref):
    qi = pl.program_id(1)
    lane = lax.broadcasted_iota(jnp.int32, (1, LANE), 1)
    ri = lax.broadcasted_iota(jnp.int32, (TQ, TQ), 0)
    ci = lax.broadcasted_iota(jnp.int32, (TQ, TQ), 1)
    scale = DIFF_HD ** -0.5
    lam = (jnp.exp(jnp.sum(lq1_ref[...] * lk1_ref[...], axis=-1, keepdims=True))
           - jnp.exp(jnp.sum(lq2_ref[...] * lk2_ref[...], axis=-1, keepdims=True)) + LAMBDA_INIT)
    oacc_ref[...] = jnp.zeros(oacc_ref.shape, F32)
    states = ((m0_ref, l0_ref, a0_ref), (m1_ref, l1_ref, a1_ref))

    def head_body(h, carry):
        pr = h // 2
        hh = h % 2
        q = qb_ref[pr].astype(F32)
        map_id = lax.shift_right_arithmetic(lane, 5) - 2 * hh
        qms = [(q * jnp.where(map_id == i, 1.0, 0.0)).astype(BF16) for i in range(2)]
        for st in states:
            _flash_init(*st)

        def step(kt, bias, causal):
            k = kb_ref[pr, pl.ds(pl.multiple_of(kt * TQ, TQ), TQ), :]
            v = vb_ref[pr, pl.ds(pl.multiple_of(kt * TQ, TQ), TQ), :]
            for i in range(2):
                s = _dot_nt(qms[i], k) * scale
                if bias is not None:
                    s = s + bias
                if causal:
                    s = jnp.where(ri >= ci, s, NEG)
                _flash_update(s, v, *states[i])

        def far(kt, c):
            step(kt, None, False)
            return c
        lax.fori_loop(0, jnp.maximum(qi - 1, 0), far, 0)

        @pl.when(qi >= 1)
        def _():
            step(qi - 1, dn_ref[h, 1], False)
        step(qi, dn_ref[h, 0], True)

        o = a0_ref[...] * (1.0 / l0_ref[...]) - lam * (a1_ref[...] * (1.0 / l1_ref[...]))
        o = jnp.where(lax.shift_right_arithmetic(lane, 6) == hh, o, 0.0)
        ms = jnp.sum(o * o, axis=-1, keepdims=True) * (1.0 / (2 * DIFF_HD))
        oacc_ref[pr] += o * lax.rsqrt(ms + EPS) * sub_ref[...] * (1.0 - LAMBDA_INIT)
        return carry
    lax.fori_loop(0, DIFF_HEADS, head_body, 0)

    for r in range(4):
        o_ref[r] = oacc_ref[r].astype(BF16)


def _diff(lq1, lk1, lq2, lk2, sub, qb, kb, vb, near_b, B, S):
    nq = S // TQ
    T = B * S
    full = lambda a: pl.BlockSpec(a.shape, lambda b, q: (0,) * a.ndim)
    kv = pl.BlockSpec((4, S, LANE), lambda b, q: (0, b, 0))
    qo = pl.BlockSpec((4, TQ, LANE), lambda b, q: (0, b * nq + q, 0))
    st = [pltpu.VMEM((TQ, 1), F32), pltpu.VMEM((TQ, 1), F32), pltpu.VMEM((TQ, LANE), F32)]
    return pl.pallas_call(
        _diff_kernel,
        grid=(B, nq),
        in_specs=[full(lq1), full(lk1), full(lq2), full(lk2), full(sub), qo, kv, kv, full(near_b)],
        out_specs=qo,
        out_shape=jax.ShapeDtypeStruct((4, T, LANE), BF16),
        scratch_shapes=st + st + [pltpu.VMEM((4, TQ, LANE), F32)],
        compiler_params=_cparams(("parallel", "arbitrary")),
        name="diff",
    )(lq1, lk1, lq2, lk2, sub, qb, kb, vb, near_b)


def _outproj_kernel(x_ref, oa_ref, ob_ref, w_ref, g_ref, wr_ref, br_ref, h_ref, tn_ref, cmb_ref):
    o = jnp.concatenate([oa_ref[r] for r in range(4)] + [ob_ref[r] for r in range(4)], axis=-1)
    h = x_ref[...] + _dot(o, w_ref[...])
    h_ref[...] = h
    tn = (h * lax.rsqrt(jnp.mean(h * h, axis=-1, keepdims=True) + EPS) * g_ref[...]).astype(BF16)
    tn_ref[...] = tn
    logits = _dot(tn, wr_ref[...]) + br_ref[...]
    lane = lax.broadcasted_iota(jnp.int32, (1, LANE), 1)
    lane_f = lane.astype(F32)
    is_grp = lane < MOE_GROUPS
    lg = jnp.where(is_grp, logits, NEG)
    mg = jnp.max(lg, axis=-1, keepdims=True)
    zg = jnp.sum(jnp.where(is_grp, jnp.exp(lg - mg), 0.0), axis=-1, keepdims=True)
    g_prob = 1.0 / zg
    g_idx = jnp.min(jnp.where(lg == mg, lane_f, 1e9), axis=-1, keepdims=True)
    lane_grp = jnp.where((lane >= MOE_GROUPS) & (lane < MOE_GROUPS + N_EXPERTS),
                         lax.shift_right_arithmetic(lane - MOE_GROUPS, 3), -1).astype(F32)
    le = jnp.where(lane_grp == g_idx, logits, NEG)
    m1 = jnp.max(le, axis=-1, keepdims=True)
    e1 = jnp.min(jnp.where(le == m1, lane_f, 1e9), axis=-1, keepdims=True)
    le2 = jnp.where(lane_f == e1, NEG, le)
    m2 = jnp.max(le2, axis=-1, keepdims=True)
    e2 = jnp.min(jnp.where(le2 == m2, lane_f, 1e9), axis=-1, keepdims=True)
    ratio = jnp.exp(m2 - m1)
    w1 = g_prob / (1.0 + ratio)
    w2 = w1 * ratio
    cmb_ref[...] = jnp.where(lane_f == e1, w1, 0.0) + jnp.where(lane_f == e2, w2, 0.0)


def _out_proj(x2, oa, ob, w, g, wr, br):
    T = x2.shape[0]
    tm = TM_PROJ
    row = lambda i: (i, 0)
    full = lambda a: pl.BlockSpec(a.shape, lambda i: (0,) * a.ndim)
    o4 = pl.BlockSpec((4, tm, LANE), lambda i: (0, i, 0))
    return pl.pallas_call(
        _outproj_kernel,
        grid=(T // tm,),
        in_specs=[pl.BlockSpec((tm, D_MODEL), row), o4, o4, full(w), full(g), full(wr), full(br)],
        out_specs=[pl.BlockSpec((tm, D_MODEL), row), pl.BlockSpec((tm, D_MODEL), row), pl.BlockSpec((tm, LANE), row)],
        out_shape=[jax.ShapeDtypeStruct((T, D_MODEL), F32), jax.ShapeDtypeStruct((T, D_MODEL), BF16),
                   jax.ShapeDtypeStruct((T, LANE), F32)],
        compiler_params=_cparams(("parallel",)),
        name="out_proj",
    )(x2, oa, ob, w, g, wr, br)


def _moe_kernel(tn_ref, cmb_ref, h_ref, wg_ref, wu_ref, wd_ref, gf_ref, o_ref, y_ref):
    e = pl.program_id(1)

    @pl.when(e == 0)
    def _():
        y_ref[...] = jnp.zeros(y_ref.shape, F32)

    lane = lax.broadcasted_iota(jnp.int32, (1, LANE), 1)
    ce = jnp.sum(jnp.where(lane == e + MOE_GROUPS, cmb_ref[...], 0.0), axis=-1, keepdims=True)
    t = tn_ref[...]
    a = _dot(t, wg_ref[0])
    b = _dot(t, wu_ref[0])
    hg = (a * jax.nn.sigmoid(a)) * b * ce
    y_ref[...] += _dot(hg.astype(BF16), wd_ref[0])

    @pl.when(e == pl.num_programs(1) - 1)
    def _():
        h = h_ref[...] + y_ref[...]
        o_ref[...] = h * lax.rsqrt(jnp.mean(h * h, axis=-1, keepdims=True) + EPS) * gf_ref[...]


def _moe(tn, cmb, h, wg, wu, wd, gf):
    T = tn.shape[0]
    tm = TM_MOE
    row = lambda i, e: (i, 0)
    return pl.pallas_call(
        _moe_kernel,
        grid=(T // tm, N_EXPERTS),
        in_specs=[pl.BlockSpec((tm, D_MODEL), row), pl.BlockSpec((tm, LANE), row), pl.BlockSpec((tm, D_MODEL), row),
                  pl.BlockSpec((1, D_MODEL, EXPERT_FF), lambda i, e: (e, 0, 0)),
                  pl.BlockSpec((1, D_MODEL, EXPERT_FF), lambda i, e: (e, 0, 0)),
                  pl.BlockSpec((1, EXPERT_FF, D_MODEL), lambda i, e: (e, 0, 0)),
                  pl.BlockSpec((1, D_MODEL), lambda i, e: (0, 0))],
        out_specs=pl.BlockSpec((tm, D_MODEL), row),
        out_shape=jax.ShapeDtypeStruct((T, D_MODEL), F32),
        scratch_shapes=[pltpu.VMEM((tm, D_MODEL), F32)],
        compiler_params=_cparams(("parallel", "arbitrary")),
        name="moe",
    )(tn, cmb, h, wg, wu, wd, gf)


def _qa_perm():
    new = np.arange(NSA_HEADS * NSA_HD)
    r, g, d = new // LANE, (new % LANE) // NSA_HD, new % NSA_HD
    return (g * NSA_R + r) * NSA_HD + d


def _block_diag2(w):
    z = jnp.zeros_like(w)
    return jnp.concatenate([jnp.concatenate([w, z], axis=-1), jnp.concatenate([z, w], axis=-1)], axis=-2)


def kernel(x, rel_bias, ln_mix, w_in, cmp_pos_k, cmp_pos_v, cmp_k_w1, cmp_k_w2, cmp_v_w1, cmp_v_w2,
           diff_lq1, diff_lk1, diff_lq2, diff_lk2, diff_subln, w_out, ln_ffn,
           router_group_w, router_group_b, router_expert_w, router_expert_b,
           exp_w_gate, exp_w_up, exp_w_down, ln_final):
    B, S, D = x.shape
    T = B * S
    assert D == D_MODEL and S % TQ == 0 and S >= WINDOW and T % TM_MOE == 0
    x2 = x.reshape(T, D)
    perm = _qa_perm()

    w = w_in[0]
    n_gate = NSA_HEADS * 3
    w_new = jnp.concatenate(
        [w[:, perm], w[:, 512:1280], w[:, 1280 + n_gate:], w[:, 1280:1280 + n_gate],
         jnp.zeros((D, LANE - n_gate), F32)], axis=1).astype(BF16)
    qa, kc, vc, ks, vs, kw, vw, qb, kb, vb, gates = _in_proj(x2, ln_mix[0][None, :], w_new)

    w1k = _block_diag2(cmp_k_w1[0].reshape(CMP_LEN, NSA_HD, CMP_HIDDEN)).astype(BF16)
    w1v = _block_diag2(cmp_v_w1[0].reshape(CMP_LEN, NSA_HD, CMP_HIDDEN)).astype(BF16)
    w2k = _block_diag2(cmp_k_w2[0]).astype(BF16)
    w2v = _block_diag2(cmp_v_w2[0]).astype(BF16)
    posk = jnp.tile(cmp_pos_k[0], (1, NSA_G))
    posv = jnp.tile(cmp_pos_v[0], (1, NSA_G))
    kcmp, vcmp = _compress(kc, vc, posk, posv, w1k, w1v, w2k, w2v, B, S)

    near, bias_c = _bias_tiles(rel_bias.T, S)

    n_sel = S // SEL_BLOCK
    nrow = S // CMP_STRIDE
    c_start = np.arange(nrow) * CMP_STRIDE
    s_start = np.arange(n_sel) * SEL_BLOCK
    ov_t = ((c_start[None, :] <= s_start[:, None] + SEL_BLOCK - 1)
            & (c_start[None, :] + CMP_LEN - 1 >= s_start[:, None])
            & (np.arange(nrow)[None, :] < nrow - 1)).astype(np.float32)
    e3 = (np.arange(LANE)[None, :, None]
          == (np.arange(S // TQ)[:, None, None] * TQ + np.arange(TQ)[None, None, :]) // SEL_BLOCK).astype(np.float32)
    o_a = _nsa(qa, gates, kcmp, vcmp, bias_c, ks, vs, kw, vw, near[:NSA_HEADS],
               jnp.asarray(ov_t, BF16), jnp.asarray(e3, BF16), B, S)

    sub = jnp.tile(diff_subln[0][None, :], (1, 2))
    o_b = _diff(diff_lq1[0][None, :], diff_lk1[0][None, :], diff_lq2[0][None, :], diff_lk2[0][None, :],
                sub, qb, kb, vb, near[NSA_HEADS:], B, S)

    w_o = jnp.concatenate([w_out[0][:512][perm], w_out[0][512:]], axis=0).astype(BF16)
    n_r = MOE_GROUPS + N_EXPERTS
    wr = jnp.concatenate([router_group_w[0], router_expert_w[0], jnp.zeros((D, LANE - n_r), F32)], axis=1).astype(BF16)
    br = jnp.concatenate([router_group_b[0], router_expert_b[0], jnp.zeros((LANE - n_r,), F32)])[None, :]
    h1, tn, cmb = _out_proj(x2, o_a, o_b, w_o, ln_ffn[0][None, :], wr, br)

    out = _moe(tn, cmb, h1, exp_w_gate[0].astype(BF16), exp_w_up[0].astype(BF16), exp_w_down[0].astype(BF16),
               ln_final[None, :])
    return out.reshape(B, S, D)
```

```python
import math

import numpy as np
import jax
import jax.numpy as jnp
from jax import lax
from jax.experimental import pallas as pl
from jax.experimental.pallas import tpu as pltpu

F32 = jnp.float32
BF16 = jnp.bfloat16
NEG = -1e30
EPS = 1e-6
LOG2E = math.log2(math.e)

D_MODEL = 1024
LANE = 128
NSA_HEADS, NSA_G, NSA_R, NSA_HD = 8, 2, 4, 64
CMP_LEN, CMP_STRIDE, CMP_HIDDEN = 32, 16, 128
SEL_BLOCK, SEL_TOPN, SEL_FORCED_LOCAL, WINDOW = 64, 8, 2, 512
DIFF_HEADS, DIFF_HD = 8, 32
REL_BUCKETS, REL_MAX_EXACT, REL_MAX_DIST = 32, 16, 128
N_REL_HEADS = NSA_HEADS + DIFF_HEADS
MOE_GROUPS, EPG, N_EXPERTS, EXPERT_FF = 4, 8, 32, 256
LAMBDA_INIT = 0.8 - 0.6 * math.exp(-0.3 * 0)
N_GATE = NSA_HEADS * 3
GATE_ROWS = 32

TQ = 256
TM_PROJ = 512
TM_MOE = 1024
VMEM_LIMIT = 48 * 1024 * 1024


def _cparams(sem):
    return pltpu.CompilerParams(dimension_semantics=sem, vmem_limit_bytes=VMEM_LIMIT)


def _dot(a, b):
    return jnp.dot(a, b, preferred_element_type=F32)


def _dot_nt(a, b):
    return lax.dot_general(a, b, (((1,), (1,)), ((), ())), preferred_element_type=F32)


def _bucket_thresholds():
    n = np.arange(0, REL_MAX_DIST + 1)
    nf = np.maximum(n, 1).astype(np.float32)
    large = REL_MAX_EXACT + (np.log(nf / np.float32(REL_MAX_EXACT)) / np.float32(math.log(REL_MAX_DIST / REL_MAX_EXACT))
                             * np.float32(REL_BUCKETS - REL_MAX_EXACT)).astype(np.int32)
    large = np.minimum(large, REL_BUCKETS - 1)
    bucket = np.where(n < REL_MAX_EXACT, n, large)
    return [int(np.argmax(bucket >= b)) for b in range(REL_BUCKETS)]


_THR = _bucket_thresholds()


def _inproj_kernel(x_ref, g_ref, w_ref, wt_ref, qa_ref, kc_ref, vc_ref, ks_ref, kw_ref, qb_ref, kb_ref,
                   vst_ref, vwt_ref, vbt_ref, gt_ref):
    x = x_ref[...]
    xn = (x * lax.rsqrt(jnp.mean(x * x, axis=-1, keepdims=True) + EPS) * g_ref[...]).astype(BF16)
    a = _dot(xn, w_ref[:, 0:512]) * (NSA_HD ** -0.5 * LOG2E)
    for r in range(4):
        qa_ref[r] = a[:, r * LANE:(r + 1) * LANE].astype(BF16)
    a = _dot(xn, w_ref[:, 512:1024])
    kc_ref[...] = a[:, 0:128]
    vc_ref[...] = a[:, 128:256]
    ks_ref[...] = a[:, 256:384].astype(BF16)
    kw_ref[...] = a[:, 384:512].astype(BF16)
    a = _dot(xn, w_ref[:, 1024:1536]) * (DIFF_HD ** -0.5 * LOG2E)
    for r in range(4):
        qb_ref[r] = a[:, r * LANE:(r + 1) * LANE].astype(BF16)
    a = _dot(xn, w_ref[:, 1536:2048])
    for r in range(4):
        kb_ref[r] = a[:, r * LANE:(r + 1) * LANE].astype(BF16)
    vst_ref[...] = _dot_nt(wt_ref[0:128, :], xn).astype(BF16)
    vwt_ref[...] = _dot_nt(wt_ref[128:256, :], xn).astype(BF16)
    for r in range(4):
        vbt_ref[r] = _dot_nt(wt_ref[256 + r * LANE:384 + r * LANE, :], xn).astype(BF16)
    gt_ref[...] = _dot_nt(wt_ref[768:768 + GATE_ROWS, :], xn)


def _in_proj(x2, g, w, wt):
    T = x2.shape[0]
    tm = TM_PROJ
    row = lambda i: (i, 0)
    o128b = jax.ShapeDtypeStruct((T, LANE), BF16)
    o128f = jax.ShapeDtypeStruct((T, LANE), F32)
    o4 = jax.ShapeDtypeStruct((4, T, LANE), BF16)
    ot = jax.ShapeDtypeStruct((LANE, T), BF16)
    o4t = jax.ShapeDtypeStruct((4, LANE, T), BF16)
    s128 = pl.BlockSpec((tm, LANE), row)
    s4 = pl.BlockSpec((4, tm, LANE), lambda i: (0, i, 0))
    st = pl.BlockSpec((LANE, tm), lambda i: (0, i))
    s4t = pl.BlockSpec((4, LANE, tm), lambda i: (0, 0, i))
    return pl.pallas_call(
        _inproj_kernel,
        grid=(T // tm,),
        in_specs=[pl.BlockSpec((tm, D_MODEL), row),
                  pl.BlockSpec((1, D_MODEL), lambda i: (0, 0)),
                  pl.BlockSpec(w.shape, lambda i: (0, 0)),
                  pl.BlockSpec(wt.shape, lambda i: (0, 0))],
        out_specs=[s4, s128, s128, s128, s128, s4, s4, st, st, s4t, pl.BlockSpec((GATE_ROWS, tm), lambda i: (0, i))],
        out_shape=[o4, o128f, o128f, o128b, o128b, o4, o4, ot, ot, o4t, jax.ShapeDtypeStruct((GATE_ROWS, T), F32)],
        compiler_params=_cparams(("parallel",)),
        name="in_proj",
    )(x2, g, w, wt)


def _gelu_tanh(x):
    return 0.5 * x * (1.0 + jnp.tanh(math.sqrt(2.0 / math.pi) * (x + 0.044715 * (x * x * x))))


def _compress_kernel(kc_ref, vc_ref, posk_ref, posv_ref, w1k_ref, w1v_ref, w2k_ref, w2v_ref, ko_ref, vo_ref):
    nrow = kc_ref.shape[0] // CMP_STRIDE
    rid = lax.broadcasted_iota(jnp.int32, (nrow, 1), 0)
    cid = lax.broadcasted_iota(jnp.int32, (1, nrow), 1)
    for src, pos, w1, w2, out, transposed in ((kc_ref, posk_ref, w1k_ref, w2k_ref, ko_ref, False),
                                              (vc_ref, posv_ref, w1v_ref, w2v_ref, vo_ref, True)):
        hid_a = jnp.zeros((nrow, 2 * CMP_HIDDEN), F32)
        hid_b = jnp.zeros((nrow, 2 * CMP_HIDDEN), F32)
        for m in range(CMP_STRIDE):
            y = src[pl.ds(m, nrow, stride=CMP_STRIDE), :]
            hid_a = hid_a + _dot((y + pos[m:m + 1, :]).astype(BF16), w1[m])
            hid_b = hid_b + _dot((y + pos[CMP_STRIDE + m:CMP_STRIDE + m + 1, :]).astype(BF16), w1[CMP_STRIDE + m])
        hid = hid_a + pltpu.roll(hid_b, nrow - 1, 0)
        o = _dot(_gelu_tanh(hid).astype(BF16), w2[...])
        if transposed:
            out[0] = jnp.where(cid < nrow - 1, o.T, 0.0).astype(BF16)
        else:
            out[0] = jnp.where(rid < nrow - 1, o, 0.0).astype(BF16)


def _compress(kc, vc, posk, posv, w1k, w1v, w2k, w2v, B, S):
    nrow = S // CMP_STRIDE
    assert nrow == LANE
    full = lambda a: pl.BlockSpec(a.shape, lambda b: (0,) * a.ndim)
    src = pl.BlockSpec((S, LANE), lambda b: (b, 0))
    osp = pl.BlockSpec((1, nrow, LANE), lambda b: (b, 0, 0))
    osh = jax.ShapeDtypeStruct((B, nrow, LANE), BF16)
    return pl.pallas_call(
        _compress_kernel,
        grid=(B,),
        in_specs=[src, src, full(posk), full(posv), full(w1k), full(w1v), full(w2k), full(w2v)],
        out_specs=[osp, osp],
        out_shape=[osh, osh],
        compiler_params=_cparams(("parallel",)),
        name="compress",
    )(kc, vc, posk, posv, w1k, w1v, w2k, w2v)


def _bias_from_dist(dist, tab_ref, h):
    val = jnp.full(dist.shape, tab_ref[h, 0], F32)
    for b in range(1, REL_BUCKETS):
        val = jnp.where(dist >= _THR[b], tab_ref[h, b], val)
    return val


def _bias_near_kernel(tab_ref, out_ref):
    h = pl.program_id(0)
    j = lax.broadcasted_iota(jnp.int32, (TQ, TQ), 0)
    i = lax.broadcasted_iota(jnp.int32, (TQ, TQ), 1)
    for d in range(2):
        bias = (_bias_from_dist(i - j + d * TQ, tab_ref, h) - tab_ref[h, REL_BUCKETS - 1]) * LOG2E
        out_ref[0, d] = jnp.where(i - j + d * TQ >= 0, bias, NEG)


def _bias_cmp_kernel(tab_ref, out_ref):
    h = pl.program_id(0)
    c = lax.broadcasted_iota(jnp.int32, (LANE, TQ), 0)
    t = pl.program_id(1) * TQ + lax.broadcasted_iota(jnp.int32, (LANE, TQ), 1)
    out_ref[0] = _bias_from_dist(t - (c * CMP_STRIDE + CMP_LEN - 1), tab_ref, h) * LOG2E


def _bias_tiles(tab_t, S):
    smem = pl.BlockSpec(memory_space=pltpu.SMEM)
    near = pl.pallas_call(
        _bias_near_kernel,
        grid=(N_REL_HEADS,),
        in_specs=[smem],
        out_specs=pl.BlockSpec((1, 2, TQ, TQ), lambda h: (h, 0, 0, 0)),
        out_shape=jax.ShapeDtypeStruct((N_REL_HEADS, 2, TQ, TQ), F32),
        compiler_params=_cparams(("parallel",)),
        name="bias_near",
    )(tab_t)
    cmp_bias = pl.pallas_call(
        _bias_cmp_kernel,
        grid=(NSA_HEADS, S // TQ),
        in_specs=[smem],
        out_specs=pl.BlockSpec((1, LANE, TQ), lambda h, q: (h, 0, q)),
        out_shape=jax.ShapeDtypeStruct((NSA_HEADS, LANE, S), F32),
        compiler_params=_cparams(("parallel", "parallel")),
        name="bias_cmp",
    )(tab_t)
    return near, cmp_bias


def _flash_first(ss, vt):
    ms = [jnp.max(s, axis=0, keepdims=True) for s in ss]
    ps = [jnp.exp2(s - m) for s, m in zip(ss, ms)]
    ls = [jnp.sum(p, axis=0, keepdims=True) for p in ps]
    return tuple((m, l, _dot(vt, p.astype(BF16))) for m, l, p in zip(ms, ls, ps))


def _flash_update(ss, vt, sts):
    ms = [jnp.maximum(st[0], jnp.max(s, axis=0, keepdims=True)) for s, st in zip(ss, sts)]
    alphas = [jnp.exp2(st[0] - m) for m, st in zip(ms, sts)]
    ps = [jnp.exp2(s - m) for s, m in zip(ss, ms)]
    ls = [a * st[1] + jnp.sum(p, axis=0, keepdims=True) for a, st, p in zip(alphas, sts, ps)]
    return tuple((m, l, a * st[2] + _dot(vt, p.astype(BF16)))
                 for m, l, a, st, p in zip(ms, ls, alphas, sts, ps))


def _flash_out(st):
    return st[2] * (1.0 / st[1])


def _ktile(ref, kt):
    return ref[pl.ds(pl.multiple_of(kt * TQ, TQ), TQ), :]


def _vtile(ref, kt):
    return ref[:, pl.ds(pl.multiple_of(kt * TQ, TQ), TQ)]


def _nsa_kernel(qa_ref, gt_ref, kcmp_ref, vcmpt_ref, bc_ref, ks_ref, vst_ref, kw_ref, vwt_ref, dn_ref,
                ov_ref, e3_ref, o_ref, psum_ref, sel_ref, oacc_ref, sig_ref):
    qi = pl.program_id(1)
    t0 = qi * TQ
    lane = lax.broadcasted_iota(jnp.int32, (1, LANE), 1)
    lane_grp = lax.shift_right_arithmetic(lane, 6)
    sub_grp = lax.shift_right_arithmetic(lax.broadcasted_iota(jnp.int32, (LANE, 1), 0), 6)
    sig_ref[...] = jax.nn.sigmoid(gt_ref[...])

    def gate_row(c):
        return sig_ref[pl.ds(c, 1), :]

    def masked_q(r, g):
        return jnp.where(lane_grp == g, qa_ref[r].astype(F32), 0.0).astype(BF16)

    oacc_ref[...] = jnp.zeros(oacc_ref.shape, F32)
    psum_ref[...] = jnp.zeros(psum_ref.shape, F32)

    n_cmp = kcmp_ref.shape[1] - 1
    crow = lax.broadcasted_iota(jnp.int32, (LANE, 1), 0)
    cmp_end = jnp.where(crow < n_cmp, crow * CMP_STRIDE + CMP_LEN - 1, 1 << 30)
    mask_c = (t0 + lax.broadcasted_iota(jnp.int32, (1, TQ), 1)) >= cmp_end

    def cmp_body(hid, carry):
        g = hid // NSA_R
        r = hid % NSA_R
        s = jnp.where(mask_c, _dot_nt(kcmp_ref[0], masked_q(r, g)) + bc_ref[hid], NEG)
        p = jnp.where(mask_c, jnp.exp2(s - jnp.max(s, axis=0, keepdims=True)), 0.0)
        l = jnp.sum(p, axis=0, keepdims=True)
        p = p * jnp.where(l > 0.0, 1.0 / l, 0.0)
        o = _dot(vcmpt_ref[0], p.astype(BF16))
        psum_ref[g] += p
        oacc_ref[r] += jnp.where(sub_grp == g, gate_row(hid * 3) * o, 0.0)
        return carry
    lax.fori_loop(0, NSA_HEADS, cmp_body, 0)

    n_sel = ov_ref.shape[0]
    jj = lax.broadcasted_iota(jnp.int32, (n_sel, TQ), 0)
    cur = lax.shift_right_arithmetic(t0 + lax.broadcasted_iota(jnp.int32, (n_sel, TQ), 1), 6)
    valid = jj <= cur
    forced = (jj == 0) | (cur - jj < SEL_FORCED_LOCAL)
    for g in range(NSA_G):
        ps = psum_ref[g]
        hi = ps.astype(BF16)
        rem = ps - hi.astype(F32)
        mid = rem.astype(BF16)
        lo = (rem - mid.astype(F32)).astype(BF16)
        ov = ov_ref[...]
        imp = _dot(ov, hi) + _dot(ov, mid) + _dot(ov, lo)
        score = jnp.where(valid, jnp.where(forced, 1e9, imp), -1e9)
        cnt = jnp.zeros((n_sel, TQ), F32)
        for j2 in range(n_sel):
            row = score[j2:j2 + 1, :]
            tie = jnp.where(jj > j2, 1.0, 0.0)
            cnt = cnt + jnp.where(row > score, 1.0, jnp.where(row == score, tie, 0.0))
        sel_t = jnp.where(cnt < float(min(SEL_TOPN, n_sel)), jnp.where(score > -1e8, 1.0, 0.0), 0.0)
        sel_ref[g] = jnp.concatenate([sel_t, jnp.zeros((LANE - n_sel, TQ), F32)], axis=0).astype(BF16)

    has_prev = jnp.where(qi >= 1, 0.0, NEG)
    has_wfar = jnp.where(qi >= WINDOW // TQ, 0.0, NEG)
    kt_prev = jnp.maximum(qi - 1, 0)
    kt_wfar = jnp.maximum(qi - WINDOW // TQ, 0)
    wfar_mask = lax.broadcasted_iota(jnp.int32, (TQ, TQ), 0) > lax.broadcasted_iota(jnp.int32, (TQ, TQ), 1)

    def main_body(r, carry):
        qms = [masked_q(r, g) for g in range(NSA_G)]
        hids = [g * NSA_R + r for g in range(NSA_G)]

        def selected(g, kt):
            return _dot(e3_ref[kt], sel_ref[g]) > 0.5

        def finish(sts, branch):
            for g in range(NSA_G):
                o = _flash_out(sts[g])
                oacc_ref[r] += jnp.where(sub_grp == g, gate_row(hids[g] * 3 + branch) * o, 0.0)

        def slc_scores(kt, near):
            k = _ktile(ks_ref, kt)
            out = []
            for g in range(NSA_G):
                s = _dot_nt(k, qms[g])
                if near == 0:
                    s = s + dn_ref[hids[g], 0]
                elif near == 1:
                    s = s + dn_ref[hids[g], 1] + has_prev
                out.append(jnp.where(selected(g, kt), s, NEG))
            return out

        ss = slc_scores(qi, 0)
        ss_next = slc_scores(kt_prev, 1)
        sts = _flash_first(ss, _vtile(vst_ref, qi))
        ss, ss_next = ss_next, slc_scores(jnp.maximum(qi - 2, 0), 2)
        sts = _flash_update(ss, _vtile(vst_ref, kt_prev), sts)

        def far(j, c):
            sts, ss = c
            kt = qi - 2 - j
            ss_next = slc_scores(jnp.maximum(kt - 1, 0), 2)
            return _flash_update(ss, _vtile(vst_ref, kt), sts), tuple(ss_next)
        sts, _ = lax.fori_loop(0, jnp.maximum(qi - 1, 0), far, (sts, tuple(ss_next)))
        finish(sts, 1)

        win = []
        for kt, near in ((qi, 0), (kt_prev, 1), (kt_wfar, 2)):
            k = _ktile(kw_ref, kt)
            ss = [_dot_nt(k, qms[g]) for g in range(NSA_G)]
            if near == 0:
                ss = [s + dn_ref[hids[g], 0] for g, s in enumerate(ss)]
            elif near == 1:
                ss = [s + dn_ref[hids[g], 1] + has_prev for g, s in enumerate(ss)]
            else:
                ss = [jnp.where(wfar_mask, s + has_wfar, NEG) for s in ss]
            win.append(ss)
        sts = _flash_first(win[0], _vtile(vwt_ref, qi))
        sts = _flash_update(win[1], _vtile(vwt_ref, kt_prev), sts)
        sts = _flash_update(win[2], _vtile(vwt_ref, kt_wfar), sts)
        finish(sts, 2)
        return carry
    lax.fori_loop(0, NSA_R, main_body, 0)

    for r in range(NSA_R):
        o_ref[r] = oacc_ref[r].T.astype(BF16)


def _nsa(qa, gates_t, kcmp, vcmpt, bias_c, ks, vst, kw, vwt, near_a, ov, e3, B, S):
    nq = S // TQ
    T = B * S
    k_spec = pl.BlockSpec((S, LANE), lambda b, q: (b, 0))
    v_spec = pl.BlockSpec((LANE, S), lambda b, q: (0, b))
    full = lambda a: pl.BlockSpec(a.shape, lambda b, q: (0,) * a.ndim)
    cmp_spec = pl.BlockSpec((1, LANE, LANE), lambda b, q: (b, 0, 0))
    qo = pl.BlockSpec((4, TQ, LANE), lambda b, q: (0, b * nq + q, 0))
    return pl.pallas_call(
        _nsa_kernel,
        grid=(B, nq),
        in_specs=[qo, pl.BlockSpec((GATE_ROWS, TQ), lambda b, q: (0, b * nq + q)),
                  cmp_spec, cmp_spec,
                  pl.BlockSpec((NSA_HEADS, LANE, TQ), lambda b, q: (0, 0, q)),
                  k_spec, v_spec, k_spec, v_spec, full(near_a), full(ov), full(e3)],
        out_specs=qo,
        out_shape=jax.ShapeDtypeStruct((4, T, LANE), BF16),
        scratch_shapes=[pltpu.VMEM((NSA_G, LANE, TQ), F32), pltpu.VMEM((NSA_G, LANE, TQ), BF16),
                        pltpu.VMEM((NSA_R, LANE, TQ), F32), pltpu.VMEM((GATE_ROWS, TQ), F32)],
        compiler_params=_cparams(("parallel", "arbitrary")),
        name="nsa",
    )(qa, gates_t, kcmp, vcmpt, bias_c, ks, vst, kw, vwt, near_a, ov, e3)


def _diff_kernel(lq1_ref, lk1_ref, lq2_ref, lk2_ref, sub_ref, qb_ref, kb_ref, vbt_ref, dn_ref, o_ref, oacc_ref):
    qi = pl.program_id(1)
    lane = lax.broadcasted_iota(jnp.int32, (1, LANE), 1)
    sub_half = lax.shift_right_arithmetic(lax.broadcasted_iota(jnp.int32, (LANE, 1), 0), 6)
    lam = (jnp.exp(jnp.sum(lq1_ref[...] * lk1_ref[...], axis=-1, keepdims=True))
           - jnp.exp(jnp.sum(lq2_ref[...] * lk2_ref[...], axis=-1, keepdims=True)) + LAMBDA_INIT)
    oacc_ref[...] = jnp.zeros(oacc_ref.shape, F32)
    has_prev = jnp.where(qi >= 1, 0.0, NEG)
    kt_prev = jnp.maximum(qi - 1, 0)

    def head_body(h, carry):
        pr = h // 2
        hh = h % 2
        q = qb_ref[pr].astype(F32)
        map_id = lax.shift_right_arithmetic(lane, 5) - 2 * hh
        qms = [jnp.where(map_id == i, q, 0.0).astype(BF16) for i in range(2)]

        def scores(kt, bias):
            k = kb_ref[pr, pl.ds(pl.multiple_of(kt * TQ, TQ), TQ), :]
            ss = [_dot_nt(k, qms[i]) for i in range(2)]
            return ss if bias is None else [s + bias for s in ss]

        def vtile(kt):
            return vbt_ref[pr, :, pl.ds(pl.multiple_of(kt * TQ, TQ), TQ)]

        ss = scores(qi, dn_ref[h, 0])
        ss_next = scores(kt_prev, dn_ref[h, 1] + has_prev)
        sts = _flash_first(ss, vtile(qi))
        ss, ss_next = ss_next, scores(jnp.maximum(qi - 2, 0), None)
        sts = _flash_update(ss, vtile(kt_prev), sts)

        def far(j, c):
            sts, ss = c
            kt = qi - 2 - j
            ss_next = scores(jnp.maximum(kt - 1, 0), None)
            return _flash_update(ss, vtile(kt), sts), tuple(ss_next)
        sts, _ = lax.fori_loop(0, jnp.maximum(qi - 1, 0), far, (sts, tuple(ss_next)))

        o = _flash_out(sts[0]) - lam * _flash_out(sts[1])
        o = jnp.where(sub_half == hh, o, 0.0)
        ms = jnp.sum(o * o, axis=0, keepdims=True) * (1.0 / (2 * DIFF_HD))
        oacc_ref[pr] += o * lax.rsqrt(ms + EPS) * sub_ref[...] * (1.0 - LAMBDA_INIT)
        return carry
    lax.fori_loop(0, DIFF_HEADS, head_body, 0)

    for r in range(4):
        o_ref[r] = oacc_ref[r].T.astype(BF16)


def _diff(lq1, lk1, lq2, lk2, sub, qb, kb, vbt, near_b, B, S):
    nq = S // TQ
    T = B * S
    full = lambda a: pl.BlockSpec(a.shape, lambda b, q: (0,) * a.ndim)
    k_spec = pl.BlockSpec((4, S, LANE), lambda b, q: (0, b, 0))
    v_spec = pl.BlockSpec((4, LANE, S), lambda b, q: (0, 0, b))
    qo = pl.BlockSpec((4, TQ, LANE), lambda b, q: (0, b * nq + q, 0))
    return pl.pallas_call(
        _diff_kernel,
        grid=(B, nq),
        in_specs=[full(lq1), full(lk1), full(lq2), full(lk2), full(sub), qo, k_spec, v_spec, full(near_b)],
        out_specs=qo,
        out_shape=jax.ShapeDtypeStruct((4, T, LANE), BF16),
        scratch_shapes=[pltpu.VMEM((4, LANE, TQ), F32)],
        compiler_params=_cparams(("parallel", "arbitrary")),
        name="diff",
    )(lq1, lk1, lq2, lk2, sub, qb, kb, vbt, near_b)


def _outproj_kernel(x_ref, oa_ref, ob_ref, w_ref, g_ref, wr_ref, br_ref, h_ref, tn_ref, cmb_ref):
    o = jnp.concatenate([oa_ref[r] for r in range(4)] + [ob_ref[r] for r in range(4)], axis=-1)
    h = x_ref[...] + _dot(o, w_ref[...])
    h_ref[...] = h
    tn = (h * lax.rsqrt(jnp.mean(h * h, axis=-1, keepdims=True) + EPS) * g_ref[...]).astype(BF16)
    tn_ref[...] = tn
    logits = _dot(tn, wr_ref[...]) + br_ref[...]
    lane = lax.broadcasted_iota(jnp.int32, (1, LANE), 1)
    lane_f = lane.astype(F32)
    is_grp = lane < MOE_GROUPS
    lg = jnp.where(is_grp, logits, NEG)
    mg = jnp.max(lg, axis=-1, keepdims=True)
    zg = jnp.sum(jnp.where(is_grp, jnp.exp(lg - mg), 0.0), axis=-1, keepdims=True)
    g_prob = 1.0 / zg
    g_idx = jnp.min(jnp.where(lg == mg, lane_f, 1e9), axis=-1, keepdims=True)
    lane_grp = jnp.where((lane >= MOE_GROUPS) & (lane < MOE_GROUPS + N_EXPERTS),
                         lax.shift_right_arithmetic(lane - MOE_GROUPS, 3), -1).astype(F32)
    le = jnp.where(lane_grp == g_idx, logits, NEG)
    m1 = jnp.max(le, axis=-1, keepdims=True)
    e1 = jnp.min(jnp.where(le == m1, lane_f, 1e9), axis=-1, keepdims=True)
    le2 = jnp.where(lane_f == e1, NEG, le)
    m2 = jnp.max(le2, axis=-1, keepdims=True)
    e2 = jnp.min(jnp.where(le2 == m2, lane_f, 1e9), axis=-1, keepdims=True)
    ratio = jnp.exp(m2 - m1)
    w1 = g_prob / (1.0 + ratio)
    w2 = w1 * ratio
    cmb_ref[...] = jnp.where(lane_f == e1, w1, 0.0) + jnp.where(lane_f == e2, w2, 0.0)


def _out_proj(x2, oa, ob, w, g, wr, br):
    T = x2.shape[0]
    tm = TM_PROJ
    row = lambda i: (i, 0)
    full = lambda a: pl.BlockSpec(a.shape, lambda i: (0,) * a.ndim)
    o4 = pl.BlockSpec((4, tm, LANE), lambda i: (0, i, 0))
    return pl.pallas_call(
        _outproj_kernel,
        grid=(T // tm,),
        in_specs=[pl.BlockSpec((tm, D_MODEL), row), o4, o4, full(w), full(g), full(wr), full(br)],
        out_specs=[pl.BlockSpec((tm, D_MODEL), row), pl.BlockSpec((tm, D_MODEL), row), pl.BlockSpec((tm, LANE), row)],
        out_shape=[jax.ShapeDtypeStruct((T, D_MODEL), F32), jax.ShapeDtypeStruct((T, D_MODEL), BF16),
                   jax.ShapeDtypeStruct((T, LANE), F32)],
        compiler_params=_cparams(("parallel",)),
        name="out_proj",
    )(x2, oa, ob, w, g, wr, br)


def _moe_kernel(tn_ref, cmb_ref, h_ref, wg_ref, wu_ref, wd_ref, gf_ref, o_ref, y_ref):
    e = pl.program_id(1)

    @pl.when(e == 0)
    def _():
        y_ref[...] = jnp.zeros(y_ref.shape, F32)

    lane = lax.broadcasted_iota(jnp.int32, (1, LANE), 1)
    ce = jnp.sum(jnp.where(lane == e + MOE_GROUPS, cmb_ref[...], 0.0), axis=-1, keepdims=True)
    t = tn_ref[...]
    a = _dot(t, wg_ref[0])
    b = _dot(t, wu_ref[0])
    hg = (a * jax.nn.sigmoid(a)) * b * ce
    y_ref[...] += _dot(hg.astype(BF16), wd_ref[0])

    @pl.when(e == pl.num_programs(1) - 1)
    def _():
        h = h_ref[...] + y_ref[...]
        o_ref[...] = h * lax.rsqrt(jnp.mean(h * h, axis=-1, keepdims=True) + EPS) * gf_ref[...]


def _moe(tn, cmb, h, wg, wu, wd, gf):
    T = tn.shape[0]
    tm = TM_MOE
    row = lambda i, e: (i, 0)
    return pl.pallas_call(
        _moe_kernel,
        grid=(T // tm, N_EXPERTS),
        in_specs=[pl.BlockSpec((tm, D_MODEL), row), pl.BlockSpec((tm, LANE), row), pl.BlockSpec((tm, D_MODEL), row),
                  pl.BlockSpec((1, D_MODEL, EXPERT_FF), lambda i, e: (e, 0, 0)),
                  pl.BlockSpec((1, D_MODEL, EXPERT_FF), lambda i, e: (e, 0, 0)),
                  pl.BlockSpec((1, EXPERT_FF, D_MODEL), lambda i, e: (e, 0, 0)),
                  pl.BlockSpec((1, D_MODEL), lambda i, e: (0, 0))],
        out_specs=pl.BlockSpec((tm, D_MODEL), row),
        out_shape=jax.ShapeDtypeStruct((T, D_MODEL), F32),
        scratch_shapes=[pltpu.VMEM((tm, D_MODEL), F32)],
        compiler_params=_cparams(("parallel", "arbitrary")),
        name="moe",
    )(tn, cmb, h, wg, wu, wd, gf)


def _qa_perm():
    new = np.arange(NSA_HEADS * NSA_HD)
    r, g, d = new // LANE, (new % LANE) // NSA_HD, new % NSA_HD
    return (g * NSA_R + r) * NSA_HD + d


def _block_diag2(w):
    z = jnp.zeros_like(w)
    return jnp.concatenate([jnp.concatenate([w, z], axis=-1), jnp.concatenate([z, w], axis=-1)], axis=-2)


def kernel(x, rel_bias, ln_mix, w_in, cmp_pos_k, cmp_pos_v, cmp_k_w1, cmp_k_w2, cmp_v_w1, cmp_v_w2,
           diff_lq1, diff_lk1, diff_lq2, diff_lk2, diff_subln, w_out, ln_ffn,
           router_group_w, router_group_b, router_expert_w, router_expert_b,
           exp_w_gate, exp_w_up, exp_w_down, ln_final):
    B, S, D = x.shape
    T = B * S
    assert D == D_MODEL and S % TQ == 0 and S >= WINDOW and T % TM_MOE == 0
    x2 = x.reshape(T, D)
    perm = _qa_perm()

    w = w_in[0]
    c_kc, c_vc, c_ks, c_vs, c_kw, c_vw, c_gt = 512, 640, 768, 896, 1024, 1152, 1280
    c_qb = c_gt + N_GATE
    c_kb, c_vb = c_qb + 512, c_qb + 1024
    col = lambda c, n=LANE: w[:, c:c + n]
    w_tok = jnp.concatenate([w[:, perm], col(c_kc), col(c_vc), col(c_ks), col(c_kw), col(c_qb, 512), col(c_kb, 512)],
                            axis=1).astype(BF16)
    w_feat = jnp.concatenate([col(c_vs), col(c_vw), col(c_vb, 512), col(c_gt, N_GATE),
                              jnp.zeros((D, GATE_ROWS - N_GATE), F32)], axis=1).T.astype(BF16)
    qa, kc, vc, ks, kw, qb, kb, vst, vwt, vbt, gates_t = _in_proj(x2, ln_mix[0][None, :], w_tok, w_feat)

    w1k = _block_diag2(cmp_k_w1[0].reshape(CMP_LEN, NSA_HD, CMP_HIDDEN)).astype(BF16)
    w1v = _block_diag2(cmp_v_w1[0].reshape(CMP_LEN, NSA_HD, CMP_HIDDEN)).astype(BF16)
    w2k = _block_diag2(cmp_k_w2[0]).astype(BF16)
    w2v = _block_diag2(cmp_v_w2[0]).astype(BF16)
    posk = jnp.tile(cmp_pos_k[0], (1, NSA_G))
    posv = jnp.tile(cmp_pos_v[0], (1, NSA_G))
    kcmp, vcmpt = _compress(kc, vc, posk, posv, w1k, w1v, w2k, w2v, B, S)

    near, bias_c = _bias_tiles(rel_bias.T, S)

    n_sel = S // SEL_BLOCK
    nrow = S // CMP_STRIDE
    c_start = np.arange(nrow) * CMP_STRIDE
    s_start = np.arange(n_sel) * SEL_BLOCK
    ov = ((c_start[None, :] <= s_start[:, None] + SEL_BLOCK - 1)
          & (c_start[None, :] + CMP_LEN - 1 >= s_start[:, None])
          & (np.arange(nrow)[None, :] < nrow - 1)).astype(np.float32)
    e3 = ((np.arange(S // TQ)[:, None, None] * TQ + np.arange(TQ)[None, :, None]) // SEL_BLOCK
          == np.arange(LANE)[None, None, :]).astype(np.float32)
    o_a = _nsa(qa, gates_t, kcmp, vcmpt, bias_c, ks, vst, kw, vwt, near[:NSA_HEADS],
               jnp.asarray(ov, BF16), jnp.asarray(e3, BF16), B, S)

    sub = jnp.tile(diff_subln[0], 2)[:, None]
    o_b = _diff(diff_lq1[0][None, :], diff_lk1[0][None, :], diff_lq2[0][None, :], diff_lk2[0][None, :],
                sub, qb, kb, vbt, near[NSA_HEADS:], B, S)

    w_o = jnp.concatenate([w_out[0][:512][perm], w_out[0][512:]], axis=0).astype(BF16)
    n_r = MOE_GROUPS + N_EXPERTS
    wr = jnp.concatenate([router_group_w[0], router_expert_w[0], jnp.zeros((D, LANE - n_r), F32)], axis=1).astype(BF16)
    br = jnp.concatenate([router_group_b[0], router_expert_b[0], jnp.zeros((LANE - n_r,), F32)])[None, :]
    h1, tn, cmb = _out_proj(x2, o_a, o_b, w_o, ln_ffn[0][None, :], wr, br)

    out = _moe(tn, cmb, h1, exp_w_gate[0].astype(BF16), exp_w_up[0].astype(BF16), exp_w_down[0].astype(BF16),
               ln_final[None, :])
    return out.reshape(B, S, D)
```

```python
import math

import numpy as np
import jax
import jax.numpy as jnp
from jax import lax
from jax.experimental import pallas as pl
from jax.experimental.pallas import tpu as pltpu

F32 = jnp.float32
BF16 = jnp.bfloat16
NEG = -1e30
EPS = 1e-6
LOG2E = math.log2(math.e)

D_MODEL = 1024
LANE = 128
NSA_HEADS, NSA_G, NSA_R, NSA_HD = 8, 2, 4, 64
CMP_LEN, CMP_STRIDE, CMP_HIDDEN = 32, 16, 128
SEL_BLOCK, SEL_TOPN, SEL_FORCED_LOCAL, WINDOW = 64, 8, 2, 512
DIFF_HEADS, DIFF_HD = 8, 32
REL_BUCKETS, REL_MAX_EXACT, REL_MAX_DIST = 32, 16, 128
N_REL_HEADS = NSA_HEADS + DIFF_HEADS
MOE_GROUPS, EPG, N_EXPERTS, EXPERT_FF = 4, 8, 32, 256
LAMBDA_INIT = 0.8 - 0.6 * math.exp(-0.3 * 0)
N_GATE = NSA_HEADS * 3
GATE_ROWS = 32

TQ = 256
TM_PROJ = 512
TM_MOE = 1024
VMEM_LIMIT = 48 * 1024 * 1024


def _cparams(sem):
    return pltpu.CompilerParams(dimension_semantics=sem, vmem_limit_bytes=VMEM_LIMIT)


def _dot(a, b):
    return jnp.dot(a, b, preferred_element_type=F32)


def _dot_nt(a, b):
    return lax.dot_general(a, b, (((1,), (1,)), ((), ())), preferred_element_type=F32)


def _bucket_thresholds():
    n = np.arange(0, REL_MAX_DIST + 1)
    nf = np.maximum(n, 1).astype(np.float32)
    large = REL_MAX_EXACT + (np.log(nf / np.float32(REL_MAX_EXACT)) / np.float32(math.log(REL_MAX_DIST / REL_MAX_EXACT))
                             * np.float32(REL_BUCKETS - REL_MAX_EXACT)).astype(np.int32)
    large = np.minimum(large, REL_BUCKETS - 1)
    bucket = np.where(n < REL_MAX_EXACT, n, large)
    return [int(np.argmax(bucket >= b)) for b in range(REL_BUCKETS)]


_THR = _bucket_thresholds()


def _inproj_kernel(x_ref, g_ref, w_ref, wt_ref, qa_ref, kc_ref, vc_ref, ks_ref, kw_ref, qb_ref, kb_ref,
                   vst_ref, vwt_ref, vbt_ref, gt_ref):
    x = x_ref[...]
    xn = (x * lax.rsqrt(jnp.mean(x * x, axis=-1, keepdims=True) + EPS) * g_ref[...]).astype(BF16)
    a = _dot(xn, w_ref[:, 0:512]) * (NSA_HD ** -0.5 * LOG2E)
    for r in range(4):
        qa_ref[r] = a[:, r * LANE:(r + 1) * LANE].astype(BF16)
    a = _dot(xn, w_ref[:, 512:1024])
    kc_ref[...] = a[:, 0:128]
    vc_ref[...] = a[:, 128:256]
    ks_ref[...] = a[:, 256:384].astype(BF16)
    kw_ref[...] = a[:, 384:512].astype(BF16)
    a = _dot(xn, w_ref[:, 1024:1536]) * (DIFF_HD ** -0.5 * LOG2E)
    for r in range(4):
        qb_ref[r] = a[:, r * LANE:(r + 1) * LANE].astype(BF16)
    a = _dot(xn, w_ref[:, 1536:2048])
    for r in range(4):
        kb_ref[r] = a[:, r * LANE:(r + 1) * LANE].astype(BF16)
    vst_ref[...] = _dot_nt(wt_ref[0:128, :], xn).astype(BF16)
    vwt_ref[...] = _dot_nt(wt_ref[128:256, :], xn).astype(BF16)
    for r in range(4):
        vbt_ref[r] = _dot_nt(wt_ref[256 + r * LANE:384 + r * LANE, :], xn).astype(BF16)
    gt_ref[...] = _dot_nt(wt_ref[768:768 + GATE_ROWS, :], xn)


def _in_proj(x2, g, w, wt):
    T = x2.shape[0]
    tm = TM_PROJ
    row = lambda i: (i, 0)
    o128b = jax.ShapeDtypeStruct((T, LANE), BF16)
    o128f = jax.ShapeDtypeStruct((T, LANE), F32)
    o4 = jax.ShapeDtypeStruct((4, T, LANE), BF16)
    ot = jax.ShapeDtypeStruct((LANE, T), BF16)
    o4t = jax.ShapeDtypeStruct((4, LANE, T), BF16)
    s128 = pl.BlockSpec((tm, LANE), row)
    s4 = pl.BlockSpec((4, tm, LANE), lambda i: (0, i, 0))
    st = pl.BlockSpec((LANE, tm), lambda i: (0, i))
    s4t = pl.BlockSpec((4, LANE, tm), lambda i: (0, 0, i))
    return pl.pallas_call(
        _inproj_kernel,
        grid=(T // tm,),
        in_specs=[pl.BlockSpec((tm, D_MODEL), row),
                  pl.BlockSpec((1, D_MODEL), lambda i: (0, 0)),
                  pl.BlockSpec(w.shape, lambda i: (0, 0)),
                  pl.BlockSpec(wt.shape, lambda i: (0, 0))],
        out_specs=[s4, s128, s128, s128, s128, s4, s4, st, st, s4t, pl.BlockSpec((GATE_ROWS, tm), lambda i: (0, i))],
        out_shape=[o4, o128f, o128f, o128b, o128b, o4, o4, ot, ot, o4t, jax.ShapeDtypeStruct((GATE_ROWS, T), F32)],
        compiler_params=_cparams(("parallel",)),
        name="in_proj",
    )(x2, g, w, wt)


def _gelu_tanh(x):
    return 0.5 * x * (1.0 + jnp.tanh(math.sqrt(2.0 / math.pi) * (x + 0.044715 * (x * x * x))))


def _compress_kernel(kc_ref, vc_ref, posk_ref, posv_ref, w1k_ref, w1v_ref, w2k_ref, w2v_ref, ko_ref, vo_ref):
    nrow = kc_ref.shape[0] // CMP_STRIDE
    rid = lax.broadcasted_iota(jnp.int32, (nrow, 1), 0)
    cid = lax.broadcasted_iota(jnp.int32, (1, nrow), 1)
    for src, pos, w1, w2, out, transposed in ((kc_ref, posk_ref, w1k_ref, w2k_ref, ko_ref, False),
                                              (vc_ref, posv_ref, w1v_ref, w2v_ref, vo_ref, True)):
        hid_a = jnp.zeros((nrow, 2 * CMP_HIDDEN), F32)
        hid_b = jnp.zeros((nrow, 2 * CMP_HIDDEN), F32)
        for m in range(CMP_STRIDE):
            y = src[pl.ds(m, nrow, stride=CMP_STRIDE), :]
            hid_a = hid_a + _dot((y + pos[m:m + 1, :]).astype(BF16), w1[m])
            hid_b = hid_b + _dot((y + pos[CMP_STRIDE + m:CMP_STRIDE + m + 1, :]).astype(BF16), w1[CMP_STRIDE + m])
        hid = hid_a + pltpu.roll(hid_b, nrow - 1, 0)
        o = _dot(_gelu_tanh(hid).astype(BF16), w2[...])
        if transposed:
            out[0] = jnp.where(cid < nrow - 1, o.T, 0.0).astype(BF16)
        else:
            out[0] = jnp.where(rid < nrow - 1, o, 0.0).astype(BF16)


def _compress(kc, vc, posk, posv, w1k, w1v, w2k, w2v, B, S):
    nrow = S // CMP_STRIDE
    assert nrow == LANE
    full = lambda a: pl.BlockSpec(a.shape, lambda b: (0,) * a.ndim)
    src = pl.BlockSpec((S, LANE), lambda b: (b, 0))
    osp = pl.BlockSpec((1, nrow, LANE), lambda b: (b, 0, 0))
    osh = jax.ShapeDtypeStruct((B, nrow, LANE), BF16)
    return pl.pallas_call(
        _compress_kernel,
        grid=(B,),
        in_specs=[src, src, full(posk), full(posv), full(w1k), full(w1v), full(w2k), full(w2v)],
        out_specs=[osp, osp],
        out_shape=[osh, osh],
        compiler_params=_cparams(("parallel",)),
        name="compress",
    )(kc, vc, posk, posv, w1k, w1v, w2k, w2v)


def _bias_from_dist(dist, tab_ref, h):
    val = jnp.full(dist.shape, tab_ref[h, 0], F32)
    for b in range(1, REL_BUCKETS):
        val = jnp.where(dist >= _THR[b], tab_ref[h, b], val)
    return val


def _bias_near_kernel(tab_ref, out_ref):
    h = pl.program_id(0)
    j = lax.broadcasted_iota(jnp.int32, (TQ, TQ), 0)
    i = lax.broadcasted_iota(jnp.int32, (TQ, TQ), 1)
    for d in range(2):
        bias = (_bias_from_dist(i - j + d * TQ, tab_ref, h) - tab_ref[h, REL_BUCKETS - 1]) * LOG2E
        out_ref[0, d] = jnp.where(i - j + d * TQ >= 0, bias, NEG)


def _bias_cmp_kernel(tab_ref, out_ref):
    h = pl.program_id(0)
    c = lax.broadcasted_iota(jnp.int32, (LANE, TQ), 0)
    t = pl.program_id(1) * TQ + lax.broadcasted_iota(jnp.int32, (LANE, TQ), 1)
    out_ref[0] = _bias_from_dist(t - (c * CMP_STRIDE + CMP_LEN - 1), tab_ref, h) * LOG2E


def _bias_tiles(tab_t, S):
    smem = pl.BlockSpec(memory_space=pltpu.SMEM)
    near = pl.pallas_call(
        _bias_near_kernel,
        grid=(N_REL_HEADS,),
        in_specs=[smem],
        out_specs=pl.BlockSpec((1, 2, TQ, TQ), lambda h: (h, 0, 0, 0)),
        out_shape=jax.ShapeDtypeStruct((N_REL_HEADS, 2, TQ, TQ), F32),
        compiler_params=_cparams(("parallel",)),
        name="bias_near",
    )(tab_t)
    cmp_bias = pl.pallas_call(
        _bias_cmp_kernel,
        grid=(NSA_HEADS, S // TQ),
        in_specs=[smem],
        out_specs=pl.BlockSpec((1, LANE, TQ), lambda h, q: (h, 0, q)),
        out_shape=jax.ShapeDtypeStruct((NSA_HEADS, LANE, S), F32),
        compiler_params=_cparams(("parallel", "parallel")),
        name="bias_cmp",
    )(tab_t)
    return near, cmp_bias


def _flash_first(ss, vt):
    ms = [jnp.max(s, axis=0, keepdims=True) for s in ss]
    ps = [jnp.exp2(s - m) for s, m in zip(ss, ms)]
    ls = [jnp.sum(p, axis=0, keepdims=True) for p in ps]
    return tuple((m, l, _dot(vt, p.astype(BF16))) for m, l, p in zip(ms, ls, ps))


def _flash_update(ss, vt, sts):
    ms = [jnp.maximum(st[0], jnp.max(s, axis=0, keepdims=True)) for s, st in zip(ss, sts)]
    alphas = [jnp.exp2(st[0] - m) for m, st in zip(ms, sts)]
    ps = [jnp.exp2(s - m) for s, m in zip(ss, ms)]
    ls = [a * st[1] + jnp.sum(p, axis=0, keepdims=True) for a, st, p in zip(alphas, sts, ps)]
    return tuple((m, l, a * st[2] + _dot(vt, p.astype(BF16)))
                 for m, l, a, st, p in zip(ms, ls, alphas, sts, ps))


def _flash_out(st):
    return st[2] * (1.0 / st[1])


_STATE_SCRATCH = [pltpu.VMEM((8, TQ), F32), pltpu.VMEM((LANE, TQ), F32)]
_STREAM_SCRATCH = _STATE_SCRATCH * 2 + [pltpu.VMEM((TQ, TQ), F32)] * 4 + [pltpu.VMEM((TQ, TQ), BF16)] * 4


def _flash_load(refs):
    return tuple((st[0:1, :], st[1:2, :], acc[...]) for st, acc in refs)


def _flash_store(refs, sts):
    for (st, acc), (m, l, a) in zip(refs, sts):
        st[0:1, :] = m
        st[1:2, :] = l
        acc[...] = a


def _causal_stream(qi, scores, vtile, st_refs, s_a, s_b, p_a, p_b):
    n_far = jnp.maximum(qi - 1, 0)
    top = qi - 2

    def write(s_refs, kt, near=2):
        for ref, s in zip(s_refs, scores(jnp.maximum(kt, 0), near)):
            ref[...] = s

    def pending(p_refs, kt):
        vt = vtile(jnp.maximum(kt, 0))
        return [_dot(vt, ref[...]) for ref in p_refs]

    def step(s_cur, p_cur, s_next, kt_next, p_prev, kt_prev, near_next=2):
        write(s_next, kt_next, near_next)
        pend = pending(p_prev, kt_prev)
        for (st, acc), s_ref, p_ref, pv in zip(st_refs, s_cur, p_cur, pend):
            s = s_ref[...]
            m_old = st[0:1, :]
            m = jnp.maximum(m_old, jnp.max(s, axis=0, keepdims=True))
            alpha = jnp.exp2(m_old - m)
            p = jnp.exp2(s - m)
            p_ref[...] = p.astype(BF16)
            st[0:1, :] = m
            st[1:2, :] = alpha * st[1:2, :] + jnp.sum(p, axis=0, keepdims=True)
            acc[...] = alpha * (acc[...] + pv)

    write(s_a, qi, 0)
    write(s_b, qi - 1, 1)
    for (st, acc), s_ref, p_ref in zip(st_refs, s_a, p_a):
        s = s_ref[...]
        m = jnp.max(s, axis=0, keepdims=True)
        p = jnp.exp2(s - m)
        p_ref[...] = p.astype(BF16)
        st[0:1, :] = m
        st[1:2, :] = jnp.sum(p, axis=0, keepdims=True)
        acc[...] = jnp.zeros(acc.shape, F32)
    step(s_b, p_b, s_a, top, p_a, qi)

    def pair(j, c):
        kt = top - 2 * j
        step(s_a, p_a, s_b, kt - 1, p_b, kt + 1)
        step(s_b, p_b, s_a, kt - 2, p_a, kt)
        return c
    lax.fori_loop(0, n_far // 2, pair, 0)

    @pl.when(n_far % 2 == 1)
    def _():
        for (st, acc), s_ref, pv in zip(st_refs, s_a, pending(p_b, 1)):
            s = s_ref[...]
            m_old = st[0:1, :]
            m = jnp.maximum(m_old, jnp.max(s, axis=0, keepdims=True))
            alpha = jnp.exp2(m_old - m)
            p = jnp.exp2(s - m)
            st[0:1, :] = m
            st[1:2, :] = alpha * st[1:2, :] + jnp.sum(p, axis=0, keepdims=True)
            acc[...] = alpha * (acc[...] + pv) + _dot(vtile(0), p.astype(BF16))

    @pl.when(n_far % 2 == 0)
    def _():
        last = jnp.where(n_far == 0, qi - 1, 0)
        for (st, acc), pv in zip(st_refs, pending(p_b, last)):
            acc[...] = acc[...] + pv
    return _flash_load(st_refs)


def _ktile(ref, kt):
    return ref[pl.ds(pl.multiple_of(kt * TQ, TQ), TQ), :]


def _vtile(ref, kt):
    return ref[:, pl.ds(pl.multiple_of(kt * TQ, TQ), TQ)]


def _nsa_kernel(qa_ref, gt_ref, kcmp_ref, vcmpt_ref, bc_ref, ks_ref, vst_ref, kw_ref, vwt_ref, dn_ref,
                ov_ref, o_ref, psum_ref, sel_ref, oacc_ref, sig_ref, st0_ref, acc0_ref, st1_ref, acc1_ref,
                sa0_ref, sa1_ref, sb0_ref, sb1_ref, pa0_ref, pa1_ref, pb0_ref, pb1_ref):
    p_a, p_b = (pa0_ref, pa1_ref), (pb0_ref, pb1_ref)
    st_refs = ((st0_ref, acc0_ref), (st1_ref, acc1_ref))
    s_a, s_b = (sa0_ref, sa1_ref), (sb0_ref, sb1_ref)
    qi = pl.program_id(1)
    t0 = qi * TQ
    lane = lax.broadcasted_iota(jnp.int32, (1, LANE), 1)
    lane_grp = lax.shift_right_arithmetic(lane, 6)
    sub_grp = lax.shift_right_arithmetic(lax.broadcasted_iota(jnp.int32, (LANE, 1), 0), 6)
    sig_ref[...] = jax.nn.sigmoid(gt_ref[...])

    def gate_row(c):
        return sig_ref[pl.ds(c, 1), :]

    def masked_q(r, g):
        return jnp.where(lane_grp == g, qa_ref[r].astype(F32), 0.0).astype(BF16)

    oacc_ref[...] = jnp.zeros(oacc_ref.shape, F32)
    psum_ref[...] = jnp.zeros(psum_ref.shape, F32)

    n_cmp = kcmp_ref.shape[1] - 1
    crow = lax.broadcasted_iota(jnp.int32, (LANE, 1), 0)
    cmp_end = jnp.where(crow < n_cmp, crow * CMP_STRIDE + CMP_LEN - 1, 1 << 30)
    mask_c = (t0 + lax.broadcasted_iota(jnp.int32, (1, TQ), 1)) >= cmp_end

    def cmp_body(hid, carry):
        g = hid // NSA_R
        r = hid % NSA_R
        s = jnp.where(mask_c, _dot_nt(kcmp_ref[0], masked_q(r, g)) + bc_ref[hid], NEG)
        p = jnp.where(mask_c, jnp.exp2(s - jnp.max(s, axis=0, keepdims=True)), 0.0)
        l = jnp.sum(p, axis=0, keepdims=True)
        p = p * jnp.where(l > 0.0, 1.0 / l, 0.0)
        o = _dot(vcmpt_ref[0], p.astype(BF16))
        psum_ref[g] += p
        oacc_ref[r] += jnp.where(sub_grp == g, gate_row(hid * 3) * o, 0.0)
        return carry
    lax.fori_loop(0, NSA_HEADS, cmp_body, 0)

    n_sel = ov_ref.shape[0]
    jj = lax.broadcasted_iota(jnp.int32, (n_sel, TQ), 0)
    cur = lax.shift_right_arithmetic(t0 + lax.broadcasted_iota(jnp.int32, (n_sel, TQ), 1), 6)
    valid = jj <= cur
    forced = (jj == 0) | (cur - jj < SEL_FORCED_LOCAL)
    for g in range(NSA_G):
        ps = psum_ref[g]
        hi = ps.astype(BF16)
        rem = ps - hi.astype(F32)
        mid = rem.astype(BF16)
        lo = (rem - mid.astype(F32)).astype(BF16)
        ov = ov_ref[...]
        imp = _dot(ov, hi) + _dot(ov, mid) + _dot(ov, lo)
        score = jnp.where(valid, jnp.where(forced, 1e9, imp), -1e9)
        cnt = jnp.zeros((n_sel, TQ), F32)
        for j2 in range(n_sel):
            row = score[j2:j2 + 1, :]
            tie = jnp.where(jj > j2, 1.0, 0.0)
            cnt = cnt + jnp.where(row > score, 1.0, jnp.where(row == score, tie, 0.0))
        sel_ref[g] = jnp.where(cnt < float(min(SEL_TOPN, n_sel)), jnp.where(score > -1e8, 0.0, NEG), NEG)

    has_prev = jnp.where(qi >= 1, 0.0, NEG)
    has_wfar = jnp.where(qi >= WINDOW // TQ, 0.0, NEG)
    kt_prev = jnp.maximum(qi - 1, 0)
    kt_wfar = jnp.maximum(qi - WINDOW // TQ, 0)
    wfar_mask = lax.broadcasted_iota(jnp.int32, (TQ, TQ), 0) > lax.broadcasted_iota(jnp.int32, (TQ, TQ), 1)

    def main_body(r, carry):
        qms = [masked_q(r, g) for g in range(NSA_G)]
        hids = [g * NSA_R + r for g in range(NSA_G)]

        def block_mask(g, kt):
            per_tile = TQ // SEL_BLOCK
            rows = [jnp.broadcast_to(sel_ref[g, pl.ds(kt * per_tile + i, 1), :], (SEL_BLOCK, TQ))
                    for i in range(per_tile)]
            return jnp.concatenate(rows, axis=0)

        def finish(sts, branch):
            for g in range(NSA_G):
                o = _flash_out(sts[g])
                oacc_ref[r] += jnp.where(sub_grp == g, gate_row(hids[g] * 3 + branch) * o, 0.0)

        def slc_scores(kt, near=2):
            k = _ktile(ks_ref, kt)
            out = []
            for g in range(NSA_G):
                s = _dot_nt(k, qms[g])
                if near == 0:
                    s = s + dn_ref[hids[g], 0]
                elif near == 1:
                    s = s + dn_ref[hids[g], 1] + has_prev
                out.append(s + block_mask(g, kt))
            return out

        def slc_vtile(kt):
            return _vtile(vst_ref, kt)

        finish(_causal_stream(qi, slc_scores, slc_vtile, st_refs, s_a, s_b, p_a, p_b), 1)

        win = []
        for kt, near in ((qi, 0), (kt_prev, 1), (kt_wfar, 2)):
            k = _ktile(kw_ref, kt)
            ss = [_dot_nt(k, qms[g]) for g in range(NSA_G)]
            if near == 0:
                ss = [s + dn_ref[hids[g], 0] for g, s in enumerate(ss)]
            elif near == 1:
                ss = [s + dn_ref[hids[g], 1] + has_prev for g, s in enumerate(ss)]
            else:
                ss = [jnp.where(wfar_mask, s + has_wfar, NEG) for s in ss]
            win.append(ss)
        sts = _flash_first(win[0], _vtile(vwt_ref, qi))
        sts = _flash_update(win[1], _vtile(vwt_ref, kt_prev), sts)
        sts = _flash_update(win[2], _vtile(vwt_ref, kt_wfar), sts)
        finish(sts, 2)
        return carry
    lax.fori_loop(0, NSA_R, main_body, 0)

    for r in range(NSA_R):
        o_ref[r] = oacc_ref[r].T.astype(BF16)


def _nsa(qa, gates_t, kcmp, vcmpt, bias_c, ks, vst, kw, vwt, near_a, ov, B, S):
    nq = S // TQ
    T = B * S
    k_spec = pl.BlockSpec((S, LANE), lambda b, q: (b, 0))
    v_spec = pl.BlockSpec((LANE, S), lambda b, q: (0, b))
    full = lambda a: pl.BlockSpec(a.shape, lambda b, q: (0,) * a.ndim)
    cmp_spec = pl.BlockSpec((1, LANE, LANE), lambda b, q: (b, 0, 0))
    qo = pl.BlockSpec((4, TQ, LANE), lambda b, q: (0, b * nq + q, 0))
    return pl.pallas_call(
        _nsa_kernel,
        grid=(B, nq),
        in_specs=[qo, pl.BlockSpec((GATE_ROWS, TQ), lambda b, q: (0, b * nq + q)),
                  cmp_spec, cmp_spec,
                  pl.BlockSpec((NSA_HEADS, LANE, TQ), lambda b, q: (0, 0, q)),
                  k_spec, v_spec, k_spec, v_spec, full(near_a), full(ov)],
        out_specs=qo,
        out_shape=jax.ShapeDtypeStruct((4, T, LANE), BF16),
        scratch_shapes=[pltpu.VMEM((NSA_G, LANE, TQ), F32), pltpu.VMEM((NSA_G, S // SEL_BLOCK, TQ), F32),
                        pltpu.VMEM((NSA_R, LANE, TQ), F32), pltpu.VMEM((GATE_ROWS, TQ), F32)] + _STREAM_SCRATCH,
        compiler_params=_cparams(("parallel", "arbitrary")),
        name="nsa",
    )(qa, gates_t, kcmp, vcmpt, bias_c, ks, vst, kw, vwt, near_a, ov)


def _diff_kernel(lq1_ref, lk1_ref, lq2_ref, lk2_ref, sub_ref, qb_ref, kb_ref, vbt_ref, dn_ref, o_ref, oacc_ref,
                 st0_ref, acc0_ref, st1_ref, acc1_ref, sa0_ref, sa1_ref, sb0_ref, sb1_ref, pa0_ref, pa1_ref, pb0_ref, pb1_ref):
    p_a, p_b = (pa0_ref, pa1_ref), (pb0_ref, pb1_ref)
    st_refs = ((st0_ref, acc0_ref), (st1_ref, acc1_ref))
    s_a, s_b = (sa0_ref, sa1_ref), (sb0_ref, sb1_ref)
    qi = pl.program_id(1)
    lane = lax.broadcasted_iota(jnp.int32, (1, LANE), 1)
    sub_half = lax.shift_right_arithmetic(lax.broadcasted_iota(jnp.int32, (LANE, 1), 0), 6)
    lam = (jnp.exp(jnp.sum(lq1_ref[...] * lk1_ref[...], axis=-1, keepdims=True))
           - jnp.exp(jnp.sum(lq2_ref[...] * lk2_ref[...], axis=-1, keepdims=True)) + LAMBDA_INIT)
    oacc_ref[...] = jnp.zeros(oacc_ref.shape, F32)
    has_prev = jnp.where(qi >= 1, 0.0, NEG)
    kt_prev = jnp.maximum(qi - 1, 0)

    def head_body(h, carry):
        pr = h // 2
        hh = h % 2
        q = qb_ref[pr].astype(F32)
        map_id = lax.shift_right_arithmetic(lane, 5) - 2 * hh
        qms = [jnp.where(map_id == i, q, 0.0).astype(BF16) for i in range(2)]

        def scores(kt, near):
            k = kb_ref[pr, pl.ds(pl.multiple_of(kt * TQ, TQ), TQ), :]
            ss = [_dot_nt(k, qms[i]) for i in range(2)]
            if near == 0:
                bias = dn_ref[h, 0]
            elif near == 1:
                bias = dn_ref[h, 1] + has_prev
            else:
                return ss
            return [s + bias for s in ss]

        def vtile(kt):
            return vbt_ref[pr, :, pl.ds(pl.multiple_of(kt * TQ, TQ), TQ)]

        sts = _causal_stream(qi, scores, vtile, st_refs, s_a, s_b, p_a, p_b)
        o = _flash_out(sts[0]) - lam * _flash_out(sts[1])
        o = jnp.where(sub_half == hh, o, 0.0)
        ms = jnp.sum(o * o, axis=0, keepdims=True) * (1.0 / (2 * DIFF_HD))
        oacc_ref[pr] += o * lax.rsqrt(ms + EPS) * sub_ref[...] * (1.0 - LAMBDA_INIT)
        return carry
    lax.fori_loop(0, DIFF_HEADS, head_body, 0)

    for r in range(4):
        o_ref[r] = oacc_ref[r].T.astype(BF16)


def _diff(lq1, lk1, lq2, lk2, sub, qb, kb, vbt, near_b, B, S):
    nq = S // TQ
    T = B * S
    full = lambda a: pl.BlockSpec(a.shape, lambda b, q: (0,) * a.ndim)
    k_spec = pl.BlockSpec((4, S, LANE), lambda b, q: (0, b, 0))
    v_spec = pl.BlockSpec((4, LANE, S), lambda b, q: (0, 0, b))
    qo = pl.BlockSpec((4, TQ, LANE), lambda b, q: (0, b * nq + q, 0))
    return pl.pallas_call(
        _diff_kernel,
        grid=(B, nq),
        in_specs=[full(lq1), full(lk1), full(lq2), full(lk2), full(sub), qo, k_spec, v_spec, full(near_b)],
        out_specs=qo,
        out_shape=jax.ShapeDtypeStruct((4, T, LANE), BF16),
        scratch_shapes=[pltpu.VMEM((4, LANE, TQ), F32)] + _STREAM_SCRATCH,
        compiler_params=_cparams(("parallel", "arbitrary")),
        name="diff",
    )(lq1, lk1, lq2, lk2, sub, qb, kb, vbt, near_b)


def _outproj_kernel(x_ref, oa_ref, ob_ref, w_ref, g_ref, wr_ref, br_ref, h_ref, tn_ref, cmb_ref):
    o = jnp.concatenate([oa_ref[r] for r in range(4)] + [ob_ref[r] for r in range(4)], axis=-1)
    h = x_ref[...] + _dot(o, w_ref[...])
    h_ref[...] = h
    tn = (h * lax.rsqrt(jnp.mean(h * h, axis=-1, keepdims=True) + EPS) * g_ref[...]).astype(BF16)
    tn_ref[...] = tn
    logits = _dot(tn, wr_ref[...]) + br_ref[...]
    lane = lax.broadcasted_iota(jnp.int32, (1, LANE), 1)
    lane_f = lane.astype(F32)
    is_grp = lane < MOE_GROUPS
    lg = jnp.where(is_grp, logits, NEG)
    mg = jnp.max(lg, axis=-1, keepdims=True)
    zg = jnp.sum(jnp.where(is_grp, jnp.exp(lg - mg), 0.0), axis=-1, keepdims=True)
    g_prob = 1.0 / zg
    g_idx = jnp.min(jnp.where(lg == mg, lane_f, 1e9), axis=-1, keepdims=True)
    lane_grp = jnp.where((lane >= MOE_GROUPS) & (lane < MOE_GROUPS + N_EXPERTS),
                         lax.shift_right_arithmetic(lane - MOE_GROUPS, 3), -1).astype(F32)
    le = jnp.where(lane_grp == g_idx, logits, NEG)
    m1 = jnp.max(le, axis=-1, keepdims=True)
    e1 = jnp.min(jnp.where(le == m1, lane_f, 1e9), axis=-1, keepdims=True)
    le2 = jnp.where(lane_f == e1, NEG, le)
    m2 = jnp.max(le2, axis=-1, keepdims=True)
    e2 = jnp.min(jnp.where(le2 == m2, lane_f, 1e9), axis=-1, keepdims=True)
    ratio = jnp.exp(m2 - m1)
    w1 = g_prob / (1.0 + ratio)
    w2 = w1 * ratio
    cmb_ref[...] = jnp.where(lane_f == e1, w1, 0.0) + jnp.where(lane_f == e2, w2, 0.0)


def _out_proj(x2, oa, ob, w, g, wr, br):
    T = x2.shape[0]
    tm = TM_PROJ
    row = lambda i: (i, 0)
    full = lambda a: pl.BlockSpec(a.shape, lambda i: (0,) * a.ndim)
    o4 = pl.BlockSpec((4, tm, LANE), lambda i: (0, i, 0))
    return pl.pallas_call(
        _outproj_kernel,
        grid=(T // tm,),
        in_specs=[pl.BlockSpec((tm, D_MODEL), row), o4, o4, full(w), full(g), full(wr), full(br)],
        out_specs=[pl.BlockSpec((tm, D_MODEL), row), pl.BlockSpec((tm, D_MODEL), row), pl.BlockSpec((tm, LANE), row)],
        out_shape=[jax.ShapeDtypeStruct((T, D_MODEL), F32), jax.ShapeDtypeStruct((T, D_MODEL), BF16),
                   jax.ShapeDtypeStruct((T, LANE), F32)],
        compiler_params=_cparams(("parallel",)),
        name="out_proj",
    )(x2, oa, ob, w, g, wr, br)


def _moe_kernel(tn_ref, cmb_ref, h_ref, wg_ref, wu_ref, wd_ref, gf_ref, o_ref, y_ref):
    e = pl.program_id(1)

    @pl.when(e == 0)
    def _():
        y_ref[...] = jnp.zeros(y_ref.shape, F32)

    lane = lax.broadcasted_iota(jnp.int32, (1, LANE), 1)
    ce = jnp.sum(jnp.where(lane == e + MOE_GROUPS, cmb_ref[...], 0.0), axis=-1, keepdims=True)
    t = tn_ref[...]
    a = _dot(t, wg_ref[0])
    b = _dot(t, wu_ref[0])
    hg = (a * jax.nn.sigmoid(a)) * b * ce
    y_ref[...] += _dot(hg.astype(BF16), wd_ref[0])

    @pl.when(e == pl.num_programs(1) - 1)
    def _():
        h = h_ref[...] + y_ref[...]
        o_ref[...] = h * lax.rsqrt(jnp.mean(h * h, axis=-1, keepdims=True) + EPS) * gf_ref[...]


def _moe(tn, cmb, h, wg, wu, wd, gf):
    T = tn.shape[0]
    tm = TM_MOE
    row = lambda i, e: (i, 0)
    return pl.pallas_call(
        _moe_kernel,
        grid=(T // tm, N_EXPERTS),
        in_specs=[pl.BlockSpec((tm, D_MODEL), row), pl.BlockSpec((tm, LANE), row), pl.BlockSpec((tm, D_MODEL), row),
                  pl.BlockSpec((1, D_MODEL, EXPERT_FF), lambda i, e: (e, 0, 0)),
                  pl.BlockSpec((1, D_MODEL, EXPERT_FF), lambda i, e: (e, 0, 0)),
                  pl.BlockSpec((1, EXPERT_FF, D_MODEL), lambda i, e: (e, 0, 0)),
                  pl.BlockSpec((1, D_MODEL), lambda i, e: (0, 0))],
        out_specs=pl.BlockSpec((tm, D_MODEL), row),
        out_shape=jax.ShapeDtypeStruct((T, D_MODEL), F32),
        scratch_shapes=[pltpu.VMEM((tm, D_MODEL), F32)],
        compiler_params=_cparams(("parallel", "arbitrary")),
        name="moe",
    )(tn, cmb, h, wg, wu, wd, gf)


def _qa_perm():
    new = np.arange(NSA_HEADS * NSA_HD)
    r, g, d = new // LANE, (new % LANE) // NSA_HD, new % NSA_HD
    return (g * NSA_R + r) * NSA_HD + d


def _block_diag2(w):
    z = jnp.zeros_like(w)
    return jnp.concatenate([jnp.concatenate([w, z], axis=-1), jnp.concatenate([z, w], axis=-1)], axis=-2)


def kernel(x, rel_bias, ln_mix, w_in, cmp_pos_k, cmp_pos_v, cmp_k_w1, cmp_k_w2, cmp_v_w1, cmp_v_w2,
           diff_lq1, diff_lk1, diff_lq2, diff_lk2, diff_subln, w_out, ln_ffn,
           router_group_w, router_group_b, router_expert_w, router_expert_b,
           exp_w_gate, exp_w_up, exp_w_down, ln_final):
    B, S, D = x.shape
    T = B * S
    assert D == D_MODEL and S % TQ == 0 and S >= WINDOW and T % TM_MOE == 0
    x2 = x.reshape(T, D)
    perm = _qa_perm()

    w = w_in[0]
    c_kc, c_vc, c_ks, c_vs, c_kw, c_vw, c_gt = 512, 640, 768, 896, 1024, 1152, 1280
    c_qb = c_gt + N_GATE
    c_kb, c_vb = c_qb + 512, c_qb + 1024
    col = lambda c, n=LANE: w[:, c:c + n]
    w_tok = jnp.concatenate([w[:, perm], col(c_kc), col(c_vc), col(c_ks), col(c_kw), col(c_qb, 512), col(c_kb, 512)],
                            axis=1).astype(BF16)
    w_feat = jnp.concatenate([col(c_vs), col(c_vw), col(c_vb, 512), col(c_gt, N_GATE),
                              jnp.zeros((D, GATE_ROWS - N_GATE), F32)], axis=1).T.astype(BF16)
    qa, kc, vc, ks, kw, qb, kb, vst, vwt, vbt, gates_t = _in_proj(x2, ln_mix[0][None, :], w_tok, w_feat)

    w1k = _block_diag2(cmp_k_w1[0].reshape(CMP_LEN, NSA_HD, CMP_HIDDEN)).astype(BF16)
    w1v = _block_diag2(cmp_v_w1[0].reshape(CMP_LEN, NSA_HD, CMP_HIDDEN)).astype(BF16)
    w2k = _block_diag2(cmp_k_w2[0]).astype(BF16)
    w2v = _block_diag2(cmp_v_w2[0]).astype(BF16)
    posk = jnp.tile(cmp_pos_k[0], (1, NSA_G))
    posv = jnp.tile(cmp_pos_v[0], (1, NSA_G))
    kcmp, vcmpt = _compress(kc, vc, posk, posv, w1k, w1v, w2k, w2v, B, S)

    near, bias_c = _bias_tiles(rel_bias.T, S)

    n_sel = S // SEL_BLOCK
    nrow = S // CMP_STRIDE
    c_start = np.arange(nrow) * CMP_STRIDE
    s_start = np.arange(n_sel) * SEL_BLOCK
    ov = ((c_start[None, :] <= s_start[:, None] + SEL_BLOCK - 1)
          & (c_start[None, :] + CMP_LEN - 1 >= s_start[:, None])
          & (np.arange(nrow)[None, :] < nrow - 1)).astype(np.float32)
    o_a = _nsa(qa, gates_t, kcmp, vcmpt, bias_c, ks, vst, kw, vwt, near[:NSA_HEADS], jnp.asarray(ov, BF16), B, S)

    sub = jnp.tile(diff_subln[0], 2)[:, None]
    o_b = _diff(diff_lq1[0][None, :], diff_lk1[0][None, :], diff_lq2[0][None, :], diff_lk2[0][None, :],
                sub, qb, kb, vbt, near[NSA_HEADS:], B, S)

    w_o = jnp.concatenate([w_out[0][:512][perm], w_out[0][512:]], axis=0).astype(BF16)
    n_r = MOE_GROUPS + N_EXPERTS
    wr = jnp.concatenate([router_group_w[0], router_expert_w[0], jnp.zeros((D, LANE - n_r), F32)], axis=1).astype(BF16)
    br = jnp.concatenate([router_group_b[0], router_expert_b[0], jnp.zeros((LANE - n_r,), F32)])[None, :]
    h1, tn, cmb = _out_proj(x2, o_a, o_b, w_o, ln_ffn[0][None, :], wr, br)

    out = _moe(tn, cmb, h1, exp_w_gate[0].astype(BF16), exp_w_up[0].astype(BF16), exp_w_down[0].astype(BF16),
               ln_final[None, :])
    return out.reshape(B, S, D)
```

```python
import math

import numpy as np
import jax
import jax.numpy as jnp
from jax import lax
from jax.experimental import pallas as pl
from jax.experimental.pallas import tpu as pltpu

F32 = jnp.float32
BF16 = jnp.bfloat16
NEG = -1e30
EPS = 1e-6
LOG2E = math.log2(math.e)

D_MODEL = 1024
LANE = 128
NSA_HEADS, NSA_G, NSA_R, NSA_HD = 8, 2, 4, 64
CMP_LEN, CMP_STRIDE, CMP_HIDDEN = 32, 16, 128
SEL_BLOCK, SEL_TOPN, SEL_FORCED_LOCAL, WINDOW = 64, 8, 2, 512
DIFF_HEADS, DIFF_HD = 8, 32
REL_BUCKETS, REL_MAX_EXACT, REL_MAX_DIST = 32, 16, 128
N_REL_HEADS = NSA_HEADS + DIFF_HEADS
MOE_GROUPS, EPG, N_EXPERTS, EXPERT_FF = 4, 8, 32, 256
LAMBDA_INIT = 0.8 - 0.6 * math.exp(-0.3 * 0)
N_GATE = NSA_HEADS * 3
GATE_ROWS = 32

TQ = 256
TM_PROJ = 512
TM_MOE = 256
XT_WIDTH = D_MODEL + LANE
VMEM_LIMIT = 48 * 1024 * 1024
VMEM_LIMIT_MOE = 56 * 1024 * 1024


def _cparams(sem):
    return pltpu.CompilerParams(dimension_semantics=sem, vmem_limit_bytes=VMEM_LIMIT)


def _dot(a, b):
    return jnp.dot(a, b, preferred_element_type=F32)


def _dot_nt(a, b):
    return lax.dot_general(a, b, (((1,), (1,)), ((), ())), preferred_element_type=F32)


def _bucket_thresholds():
    n = np.arange(0, REL_MAX_DIST + 1)
    nf = np.maximum(n, 1).astype(np.float32)
    large = REL_MAX_EXACT + (np.log(nf / np.float32(REL_MAX_EXACT)) / np.float32(math.log(REL_MAX_DIST / REL_MAX_EXACT))
                             * np.float32(REL_BUCKETS - REL_MAX_EXACT)).astype(np.int32)
    large = np.minimum(large, REL_BUCKETS - 1)
    bucket = np.where(n < REL_MAX_EXACT, n, large)
    return [int(np.argmax(bucket >= b)) for b in range(REL_BUCKETS)]


_THR = _bucket_thresholds()


def _inproj_kernel(x_ref, g_ref, w_ref, wt_ref, qa_ref, kc_ref, vc_ref, ks_ref, kw_ref, qb_ref, kb_ref,
                   vst_ref, vwt_ref, vbt_ref, gt_ref):
    x = x_ref[...]
    xn = (x * lax.rsqrt(jnp.mean(x * x, axis=-1, keepdims=True) + EPS) * g_ref[...]).astype(BF16)
    a = _dot(xn, w_ref[:, 0:512]) * (NSA_HD ** -0.5 * LOG2E)
    for r in range(4):
        qa_ref[r] = a[:, r * LANE:(r + 1) * LANE].astype(BF16)
    a = _dot(xn, w_ref[:, 512:1024])
    kc_ref[...] = a[:, 0:128]
    vc_ref[...] = a[:, 128:256]
    ks_ref[...] = a[:, 256:384].astype(BF16)
    kw_ref[...] = a[:, 384:512].astype(BF16)
    a = _dot(xn, w_ref[:, 1024:1536]) * (DIFF_HD ** -0.5 * LOG2E)
    for r in range(4):
        qb_ref[r] = a[:, r * LANE:(r + 1) * LANE].astype(BF16)
    a = _dot(xn, w_ref[:, 1536:2048])
    for r in range(4):
        kb_ref[r] = a[:, r * LANE:(r + 1) * LANE].astype(BF16)
    vst_ref[...] = _dot_nt(wt_ref[0:128, :], xn).astype(BF16)
    vwt_ref[...] = _dot_nt(wt_ref[128:256, :], xn).astype(BF16)
    for r in range(4):
        vbt_ref[r] = _dot_nt(wt_ref[256 + r * LANE:384 + r * LANE, :], xn).astype(BF16)
    gt_ref[...] = _dot_nt(wt_ref[768:768 + GATE_ROWS, :], xn)


def _in_proj(x2, g, w, wt):
    T = x2.shape[0]
    tm = TM_PROJ
    row = lambda i: (i, 0)
    o128b = jax.ShapeDtypeStruct((T, LANE), BF16)
    o128f = jax.ShapeDtypeStruct((T, LANE), F32)
    o4 = jax.ShapeDtypeStruct((4, T, LANE), BF16)
    ot = jax.ShapeDtypeStruct((LANE, T), BF16)
    o4t = jax.ShapeDtypeStruct((4, LANE, T), BF16)
    s128 = pl.BlockSpec((tm, LANE), row)
    s4 = pl.BlockSpec((4, tm, LANE), lambda i: (0, i, 0))
    st = pl.BlockSpec((LANE, tm), lambda i: (0, i))
    s4t = pl.BlockSpec((4, LANE, tm), lambda i: (0, 0, i))
    return pl.pallas_call(
        _inproj_kernel,
        grid=(T // tm,),
        in_specs=[pl.BlockSpec((tm, D_MODEL), row),
                  pl.BlockSpec((1, D_MODEL), lambda i: (0, 0)),
                  pl.BlockSpec(w.shape, lambda i: (0, 0)),
                  pl.BlockSpec(wt.shape, lambda i: (0, 0))],
        out_specs=[s4, s128, s128, s128, s128, s4, s4, st, st, s4t, pl.BlockSpec((GATE_ROWS, tm), lambda i: (0, i))],
        out_shape=[o4, o128f, o128f, o128b, o128b, o4, o4, ot, ot, o4t, jax.ShapeDtypeStruct((GATE_ROWS, T), F32)],
        compiler_params=_cparams(("parallel",)),
        name="in_proj",
    )(x2, g, w, wt)


def _gelu_tanh(x):
    return 0.5 * x * (1.0 + jnp.tanh(math.sqrt(2.0 / math.pi) * (x + 0.044715 * (x * x * x))))


def _compress_kernel(kc_ref, vc_ref, posk_ref, posv_ref, w1k_ref, w1v_ref, w2k_ref, w2v_ref, ko_ref, vo_ref):
    nrow = kc_ref.shape[0] // CMP_STRIDE
    rid = lax.broadcasted_iota(jnp.int32, (nrow, 1), 0)
    cid = lax.broadcasted_iota(jnp.int32, (1, nrow), 1)
    for src, pos, w1, w2, out, transposed in ((kc_ref, posk_ref, w1k_ref, w2k_ref, ko_ref, False),
                                              (vc_ref, posv_ref, w1v_ref, w2v_ref, vo_ref, True)):
        hid_a = jnp.zeros((nrow, 2 * CMP_HIDDEN), F32)
        hid_b = jnp.zeros((nrow, 2 * CMP_HIDDEN), F32)
        for m in range(CMP_STRIDE):
            y = src[pl.ds(m, nrow, stride=CMP_STRIDE), :]
            hid_a = hid_a + _dot((y + pos[m:m + 1, :]).astype(BF16), w1[m])
            hid_b = hid_b + _dot((y + pos[CMP_STRIDE + m:CMP_STRIDE + m + 1, :]).astype(BF16), w1[CMP_STRIDE + m])
        hid = hid_a + pltpu.roll(hid_b, nrow - 1, 0)
        o = _dot(_gelu_tanh(hid).astype(BF16), w2[...])
        if transposed:
            out[0] = jnp.where(cid < nrow - 1, o.T, 0.0).astype(BF16)
        else:
            out[0] = jnp.where(rid < nrow - 1, o, 0.0).astype(BF16)


def _compress(kc, vc, posk, posv, w1k, w1v, w2k, w2v, B, S):
    nrow = S // CMP_STRIDE
    assert nrow == LANE
    full = lambda a: pl.BlockSpec(a.shape, lambda b: (0,) * a.ndim)
    src = pl.BlockSpec((S, LANE), lambda b: (b, 0))
    osp = pl.BlockSpec((1, nrow, LANE), lambda b: (b, 0, 0))
    osh = jax.ShapeDtypeStruct((B, nrow, LANE), BF16)
    return pl.pallas_call(
        _compress_kernel,
        grid=(B,),
        in_specs=[src, src, full(posk), full(posv), full(w1k), full(w1v), full(w2k), full(w2v)],
        out_specs=[osp, osp],
        out_shape=[osh, osh],
        compiler_params=_cparams(("parallel",)),
        name="compress",
    )(kc, vc, posk, posv, w1k, w1v, w2k, w2v)


def _bias_from_dist(dist, tab_ref, h):
    val = jnp.full(dist.shape, tab_ref[h, 0], F32)
    for b in range(1, REL_BUCKETS):
        val = jnp.where(dist >= _THR[b], tab_ref[h, b], val)
    return val


def _bias_near_kernel(tab_ref, out_ref):
    h = pl.program_id(0)
    j = lax.broadcasted_iota(jnp.int32, (TQ, TQ), 0)
    i = lax.broadcasted_iota(jnp.int32, (TQ, TQ), 1)
    for d in range(2):
        bias = (_bias_from_dist(i - j + d * TQ, tab_ref, h) - tab_ref[h, REL_BUCKETS - 1]) * LOG2E
        out_ref[0, d] = jnp.where(i - j + d * TQ >= 0, bias, NEG)


def _bias_cmp_kernel(tab_ref, out_ref):
    h = pl.program_id(0)
    c = lax.broadcasted_iota(jnp.int32, (LANE, TQ), 0)
    t = pl.program_id(1) * TQ + lax.broadcasted_iota(jnp.int32, (LANE, TQ), 1)
    out_ref[0] = _bias_from_dist(t - (c * CMP_STRIDE + CMP_LEN - 1), tab_ref, h) * LOG2E


def _bias_tiles(tab_t, S):
    smem = pl.BlockSpec(memory_space=pltpu.SMEM)
    near = pl.pallas_call(
        _bias_near_kernel,
        grid=(N_REL_HEADS,),
        in_specs=[smem],
        out_specs=pl.BlockSpec((1, 2, TQ, TQ), lambda h: (h, 0, 0, 0)),
        out_shape=jax.ShapeDtypeStruct((N_REL_HEADS, 2, TQ, TQ), F32),
        compiler_params=_cparams(("parallel",)),
        name="bias_near",
    )(tab_t)
    cmp_bias = pl.pallas_call(
        _bias_cmp_kernel,
        grid=(NSA_HEADS, S // TQ),
        in_specs=[smem],
        out_specs=pl.BlockSpec((1, LANE, TQ), lambda h, q: (h, 0, q)),
        out_shape=jax.ShapeDtypeStruct((NSA_HEADS, LANE, S), F32),
        compiler_params=_cparams(("parallel", "parallel")),
        name="bias_cmp",
    )(tab_t)
    return near, cmp_bias


def _flash_first(ss, vt):
    ms = [jnp.max(s, axis=0, keepdims=True) for s in ss]
    ps = [jnp.exp2(s - m) for s, m in zip(ss, ms)]
    ls = [jnp.sum(p, axis=0, keepdims=True) for p in ps]
    return tuple((m, l, _dot(vt, p.astype(BF16))) for m, l, p in zip(ms, ls, ps))


def _flash_update(ss, vt, sts):
    ms = [jnp.maximum(st[0], jnp.max(s, axis=0, keepdims=True)) for s, st in zip(ss, sts)]
    alphas = [jnp.exp2(st[0] - m) for m, st in zip(ms, sts)]
    ps = [jnp.exp2(s - m) for s, m in zip(ss, ms)]
    ls = [a * st[1] + jnp.sum(p, axis=0, keepdims=True) for a, st, p in zip(alphas, sts, ps)]
    return tuple((m, l, a * st[2] + _dot(vt, p.astype(BF16)))
                 for m, l, a, st, p in zip(ms, ls, alphas, sts, ps))


def _flash_out(st):
    return st[2] * (1.0 / st[1])


_STATE_SCRATCH = [pltpu.VMEM((8, TQ), F32), pltpu.VMEM((LANE, TQ), F32)]
_STREAM_SCRATCH = _STATE_SCRATCH * 2 + [pltpu.VMEM((TQ, TQ), F32)] * 4 + [pltpu.VMEM((TQ, TQ), BF16)] * 4


def _flash_load(refs):
    return tuple((st[0:1, :], st[1:2, :], acc[...]) for st, acc in refs)


def _flash_store(refs, sts):
    for (st, acc), (m, l, a) in zip(refs, sts):
        st[0:1, :] = m
        st[1:2, :] = l
        acc[...] = a


def _causal_stream(qi, scores, vtile, st_refs, s_a, s_b, p_a, p_b):
    n_far = jnp.maximum(qi - 1, 0)
    top = qi - 2

    def write(s_refs, kt, near=2):
        for ref, s in zip(s_refs, scores(jnp.maximum(kt, 0), near)):
            ref[...] = s

    def pending(p_refs, kt):
        vt = vtile(jnp.maximum(kt, 0))
        return [_dot(vt, ref[...]) for ref in p_refs]

    def step(s_cur, p_cur, s_next, kt_next, p_prev, kt_prev, near_next=2):
        write(s_next, kt_next, near_next)
        pend = pending(p_prev, kt_prev)
        for (st, acc), s_ref, p_ref, pv in zip(st_refs, s_cur, p_cur, pend):
            s = s_ref[...]
            m_old = st[0:1, :]
            m = jnp.maximum(m_old, jnp.max(s, axis=0, keepdims=True))
            alpha = jnp.exp2(m_old - m)
            p = jnp.exp2(s - m)
            p_ref[...] = p.astype(BF16)
            st[0:1, :] = m
            st[1:2, :] = alpha * st[1:2, :] + jnp.sum(p, axis=0, keepdims=True)
            acc[...] = alpha * (acc[...] + pv)

    write(s_a, qi, 0)
    write(s_b, qi - 1, 1)
    for (st, acc), s_ref, p_ref in zip(st_refs, s_a, p_a):
        s = s_ref[...]
        m = jnp.max(s, axis=0, keepdims=True)
        p = jnp.exp2(s - m)
        p_ref[...] = p.astype(BF16)
        st[0:1, :] = m
        st[1:2, :] = jnp.sum(p, axis=0, keepdims=True)
        acc[...] = jnp.zeros(acc.shape, F32)
    step(s_b, p_b, s_a, top, p_a, qi)

    def pair(j, c):
        kt = top - 2 * j
        step(s_a, p_a, s_b, kt - 1, p_b, kt + 1)
        step(s_b, p_b, s_a, kt - 2, p_a, kt)
        return c
    lax.fori_loop(0, n_far // 2, pair, 0)

    @pl.when(n_far % 2 == 1)
    def _():
        for (st, acc), s_ref, pv in zip(st_refs, s_a, pending(p_b, 1)):
            s = s_ref[...]
            m_old = st[0:1, :]
            m = jnp.maximum(m_old, jnp.max(s, axis=0, keepdims=True))
            alpha = jnp.exp2(m_old - m)
            p = jnp.exp2(s - m)
            st[0:1, :] = m
            st[1:2, :] = alpha * st[1:2, :] + jnp.sum(p, axis=0, keepdims=True)
            acc[...] = alpha * (acc[...] + pv) + _dot(vtile(0), p.astype(BF16))

    @pl.when(n_far % 2 == 0)
    def _():
        last = jnp.where(n_far == 0, qi - 1, 0)
        for (st, acc), pv in zip(st_refs, pending(p_b, last)):
            acc[...] = acc[...] + pv
    return _flash_load(st_refs)


def _ktile(ref, kt):
    return ref[pl.ds(pl.multiple_of(kt * TQ, TQ), TQ), :]


def _vtile(ref, kt):
    return ref[:, pl.ds(pl.multiple_of(kt * TQ, TQ), TQ)]


def _nsa_kernel(qa_ref, gt_ref, kcmp_ref, vcmpt_ref, bc_ref, ks_ref, vst_ref, kw_ref, vwt_ref, dn_ref,
                ov_ref, o_ref, psum_ref, sel_ref, oacc_ref, sig_ref, st0_ref, acc0_ref, st1_ref, acc1_ref,
                sa0_ref, sa1_ref, sb0_ref, sb1_ref, pa0_ref, pa1_ref, pb0_ref, pb1_ref):
    p_a, p_b = (pa0_ref, pa1_ref), (pb0_ref, pb1_ref)
    st_refs = ((st0_ref, acc0_ref), (st1_ref, acc1_ref))
    s_a, s_b = (sa0_ref, sa1_ref), (sb0_ref, sb1_ref)
    qi = pl.program_id(1)
    t0 = qi * TQ
    lane = lax.broadcasted_iota(jnp.int32, (1, LANE), 1)
    lane_grp = lax.shift_right_arithmetic(lane, 6)
    sub_grp = lax.shift_right_arithmetic(lax.broadcasted_iota(jnp.int32, (LANE, 1), 0), 6)
    sig_ref[...] = jax.nn.sigmoid(gt_ref[...])

    def gate_row(c):
        return sig_ref[pl.ds(c, 1), :]

    def masked_q(r, g):
        return jnp.where(lane_grp == g, qa_ref[r].astype(F32), 0.0).astype(BF16)

    n_cmp = kcmp_ref.shape[1] - 1
    crow = lax.broadcasted_iota(jnp.int32, (LANE, 1), 0)
    cmp_end = jnp.where(crow < n_cmp, crow * CMP_STRIDE + CMP_LEN - 1, 1 << 30)
    mask_c = (t0 + lax.broadcasted_iota(jnp.int32, (1, TQ), 1)) >= cmp_end

    heads = [(r, g) for r in range(NSA_R) for g in range(NSA_G)]
    scores_c = [_dot_nt(kcmp_ref[0], masked_q(r, g)) for r, g in heads]
    probs_c = []
    for (r, g), s in zip(heads, scores_c):
        s = jnp.where(mask_c, s + bc_ref[g * NSA_R + r], NEG)
        p = jnp.where(mask_c, jnp.exp2(s - jnp.max(s, axis=0, keepdims=True)), 0.0)
        l = jnp.sum(p, axis=0, keepdims=True)
        probs_c.append(p * jnp.where(l > 0.0, 1.0 / l, 0.0))
    for g in range(NSA_G):
        psum_ref[g] = sum(p for (r, gg), p in zip(heads, probs_c) if gg == g)
    outs_c = [_dot(vcmpt_ref[0], p.astype(BF16)) for p in probs_c]
    for r in range(NSA_R):
        o0, o1 = outs_c[r * NSA_G], outs_c[r * NSA_G + 1]
        oacc_ref[r] = jnp.where(sub_grp == 0, gate_row(r * 3) * o0, gate_row((NSA_R + r) * 3) * o1)

    n_sel = ov_ref.shape[0]
    jj = lax.broadcasted_iota(jnp.int32, (n_sel, TQ), 0)
    cur = lax.shift_right_arithmetic(t0 + lax.broadcasted_iota(jnp.int32, (n_sel, TQ), 1), 6)
    valid = jj <= cur
    forced = (jj == 0) | (cur - jj < SEL_FORCED_LOCAL)
    for g in range(NSA_G):
        ps = psum_ref[g]
        hi = ps.astype(BF16)
        rem = ps - hi.astype(F32)
        mid = rem.astype(BF16)
        lo = (rem - mid.astype(F32)).astype(BF16)
        ov = ov_ref[...]
        imp = _dot(ov, hi) + _dot(ov, mid) + _dot(ov, lo)
        score = jnp.where(valid, jnp.where(forced, 1e9, imp), -1e9)
        cnt = jnp.zeros((n_sel, TQ), F32)
        for j2 in range(n_sel):
            row = score[j2:j2 + 1, :]
            tie = jnp.where(jj > j2, 1.0, 0.0)
            cnt = cnt + jnp.where(row > score, 1.0, jnp.where(row == score, tie, 0.0))
        sel_ref[g] = jnp.where(cnt < float(min(SEL_TOPN, n_sel)), jnp.where(score > -1e8, 0.0, NEG), NEG)

    has_prev = jnp.where(qi >= 1, 0.0, NEG)
    has_wfar = jnp.where(qi >= WINDOW // TQ, 0.0, NEG)
    kt_prev = jnp.maximum(qi - 1, 0)
    kt_wfar = jnp.maximum(qi - WINDOW // TQ, 0)
    wfar_mask = lax.broadcasted_iota(jnp.int32, (TQ, TQ), 0) > lax.broadcasted_iota(jnp.int32, (TQ, TQ), 1)

    def main_body(r, carry):
        qms = [masked_q(r, g) for g in range(NSA_G)]
        hids = [g * NSA_R + r for g in range(NSA_G)]

        def block_mask(g, kt):
            per_tile = TQ // SEL_BLOCK
            rows = [jnp.broadcast_to(sel_ref[g, pl.ds(kt * per_tile + i, 1), :], (SEL_BLOCK, TQ))
                    for i in range(per_tile)]
            return jnp.concatenate(rows, axis=0)

        def finish(sts, branch):
            for g in range(NSA_G):
                o = _flash_out(sts[g])
                oacc_ref[r] += jnp.where(sub_grp == g, gate_row(hids[g] * 3 + branch) * o, 0.0)

        def slc_scores(kt, near=2):
            k = _ktile(ks_ref, kt)
            out = []
            for g in range(NSA_G):
                s = _dot_nt(k, qms[g])
                if near == 0:
                    s = s + dn_ref[hids[g], 0]
                elif near == 1:
                    s = s + dn_ref[hids[g], 1] + has_prev
                out.append(s + block_mask(g, kt))
            return out

        def slc_vtile(kt):
            return _vtile(vst_ref, kt)

        finish(_causal_stream(qi, slc_scores, slc_vtile, st_refs, s_a, s_b, p_a, p_b), 1)

        win = []
        for kt, near in ((qi, 0), (kt_prev, 1), (kt_wfar, 2)):
            k = _ktile(kw_ref, kt)
            ss = [_dot_nt(k, qms[g]) for g in range(NSA_G)]
            if near == 0:
                ss = [s + dn_ref[hids[g], 0] for g, s in enumerate(ss)]
            elif near == 1:
                ss = [s + dn_ref[hids[g], 1] + has_prev for g, s in enumerate(ss)]
            else:
                ss = [jnp.where(wfar_mask, s + has_wfar, NEG) for s in ss]
            win.append(ss)
        sts = _flash_first(win[0], _vtile(vwt_ref, qi))
        sts = _flash_update(win[1], _vtile(vwt_ref, kt_prev), sts)
        sts = _flash_update(win[2], _vtile(vwt_ref, kt_wfar), sts)
        finish(sts, 2)
        return carry
    lax.fori_loop(0, NSA_R, main_body, 0)

    for r in range(NSA_R):
        o_ref[r] = oacc_ref[r].T.astype(BF16)


def _nsa(qa, gates_t, kcmp, vcmpt, bias_c, ks, vst, kw, vwt, near_a, ov, B, S):
    nq = S // TQ
    T = B * S
    k_spec = pl.BlockSpec((S, LANE), lambda b, q: (b, 0))
    v_spec = pl.BlockSpec((LANE, S), lambda b, q: (0, b))
    full = lambda a: pl.BlockSpec(a.shape, lambda b, q: (0,) * a.ndim)
    cmp_spec = pl.BlockSpec((1, LANE, LANE), lambda b, q: (b, 0, 0))
    qo = pl.BlockSpec((4, TQ, LANE), lambda b, q: (0, b * nq + q, 0))
    return pl.pallas_call(
        _nsa_kernel,
        grid=(B, nq),
        in_specs=[qo, pl.BlockSpec((GATE_ROWS, TQ), lambda b, q: (0, b * nq + q)),
                  cmp_spec, cmp_spec,
                  pl.BlockSpec((NSA_HEADS, LANE, TQ), lambda b, q: (0, 0, q)),
                  k_spec, v_spec, k_spec, v_spec, full(near_a), full(ov)],
        out_specs=qo,
        out_shape=jax.ShapeDtypeStruct((4, T, LANE), BF16),
        scratch_shapes=[pltpu.VMEM((NSA_G, LANE, TQ), F32), pltpu.VMEM((NSA_G, S // SEL_BLOCK, TQ), F32),
                        pltpu.VMEM((NSA_R, LANE, TQ), F32), pltpu.VMEM((GATE_ROWS, TQ), F32)] + _STREAM_SCRATCH,
        compiler_params=_cparams(("parallel", "arbitrary")),
        name="nsa",
    )(qa, gates_t, kcmp, vcmpt, bias_c, ks, vst, kw, vwt, near_a, ov)


def _diff_kernel(lq1_ref, lk1_ref, lq2_ref, lk2_ref, sub_ref, qb_ref, kb_ref, vbt_ref, dn_ref, o_ref, oacc_ref,
                 st0_ref, acc0_ref, st1_ref, acc1_ref, sa0_ref, sa1_ref, sb0_ref, sb1_ref, pa0_ref, pa1_ref, pb0_ref, pb1_ref):
    p_a, p_b = (pa0_ref, pa1_ref), (pb0_ref, pb1_ref)
    st_refs = ((st0_ref, acc0_ref), (st1_ref, acc1_ref))
    s_a, s_b = (sa0_ref, sa1_ref), (sb0_ref, sb1_ref)
    qi = pl.program_id(1)
    lane = lax.broadcasted_iota(jnp.int32, (1, LANE), 1)
    sub_half = lax.shift_right_arithmetic(lax.broadcasted_iota(jnp.int32, (LANE, 1), 0), 6)
    lam = (jnp.exp(jnp.sum(lq1_ref[...] * lk1_ref[...], axis=-1, keepdims=True))
           - jnp.exp(jnp.sum(lq2_ref[...] * lk2_ref[...], axis=-1, keepdims=True)) + LAMBDA_INIT)
    oacc_ref[...] = jnp.zeros(oacc_ref.shape, F32)
    has_prev = jnp.where(qi >= 1, 0.0, NEG)
    kt_prev = jnp.maximum(qi - 1, 0)

    def head_body(h, carry):
        pr = h // 2
        hh = h % 2
        q = qb_ref[pr].astype(F32)
        map_id = lax.shift_right_arithmetic(lane, 5) - 2 * hh
        qms = [jnp.where(map_id == i, q, 0.0).astype(BF16) for i in range(2)]

        def scores(kt, near):
            k = kb_ref[pr, pl.ds(pl.multiple_of(kt * TQ, TQ), TQ), :]
            ss = [_dot_nt(k, qms[i]) for i in range(2)]
            if near == 0:
                bias = dn_ref[h, 0]
            elif near == 1:
                bias = dn_ref[h, 1] + has_prev
            else:
                return ss
            return [s + bias for s in ss]

        def vtile(kt):
            return vbt_ref[pr, :, pl.ds(pl.multiple_of(kt * TQ, TQ), TQ)]

        sts = _causal_stream(qi, scores, vtile, st_refs, s_a, s_b, p_a, p_b)
        o = _flash_out(sts[0]) - lam * _flash_out(sts[1])
        o = jnp.where(sub_half == hh, o, 0.0)
        ms = jnp.sum(o * o, axis=0, keepdims=True) * (1.0 / (2 * DIFF_HD))
        oacc_ref[pr] += o * lax.rsqrt(ms + EPS) * sub_ref[...] * (1.0 - LAMBDA_INIT)
        return carry
    lax.fori_loop(0, DIFF_HEADS, head_body, 0)

    for r in range(4):
        o_ref[r] = oacc_ref[r].T.astype(BF16)


def _diff(lq1, lk1, lq2, lk2, sub, qb, kb, vbt, near_b, B, S):
    nq = S // TQ
    T = B * S
    full = lambda a: pl.BlockSpec(a.shape, lambda b, q: (0,) * a.ndim)
    k_spec = pl.BlockSpec((4, S, LANE), lambda b, q: (0, b, 0))
    v_spec = pl.BlockSpec((4, LANE, S), lambda b, q: (0, 0, b))
    qo = pl.BlockSpec((4, TQ, LANE), lambda b, q: (0, b * nq + q, 0))
    return pl.pallas_call(
        _diff_kernel,
        grid=(B, nq),
        in_specs=[full(lq1), full(lk1), full(lq2), full(lk2), full(sub), qo, k_spec, v_spec, full(near_b)],
        out_specs=qo,
        out_shape=jax.ShapeDtypeStruct((4, T, LANE), BF16),
        scratch_shapes=[pltpu.VMEM((4, LANE, TQ), F32)] + _STREAM_SCRATCH,
        compiler_params=_cparams(("parallel", "arbitrary")),
        name="diff",
    )(lq1, lk1, lq2, lk2, sub, qb, kb, vbt, near_b)


def _outproj_kernel(x_ref, oa_ref, ob_ref, w_ref, g_ref, wr_ref, br_ref, h_ref, xt_ref, route_ref, cnt_ref,
                    carry_ref):
    @pl.when(pl.program_id(0) == 0)
    def _():
        carry_ref[...] = jnp.zeros(carry_ref.shape, F32)

    o = jnp.concatenate([oa_ref[r] for r in range(4)] + [ob_ref[r] for r in range(4)], axis=-1)
    h = x_ref[...] + _dot(o, w_ref[...])
    h_ref[...] = h
    tn32 = h * lax.rsqrt(jnp.mean(h * h, axis=-1, keepdims=True) + EPS) * g_ref[...]
    xt_ref[:, 0:D_MODEL] = tn32
    tn = tn32.astype(BF16)
    logits = _dot(tn, wr_ref[...]) + br_ref[...]
    lane = lax.broadcasted_iota(jnp.int32, (1, LANE), 1)
    lane_f = lane.astype(F32)
    is_grp = lane < MOE_GROUPS
    lg = jnp.where(is_grp, logits, NEG)
    mg = jnp.max(lg, axis=-1, keepdims=True)
    zg = jnp.sum(jnp.where(is_grp, jnp.exp(lg - mg), 0.0), axis=-1, keepdims=True)
    g_prob = 1.0 / zg
    g_idx = jnp.min(jnp.where(lg == mg, lane_f, 1e9), axis=-1, keepdims=True)
    lane_grp = jnp.where((lane >= MOE_GROUPS) & (lane < MOE_GROUPS + N_EXPERTS),
                         lax.shift_right_arithmetic(lane - MOE_GROUPS, 3), -1).astype(F32)
    le = jnp.where(lane_grp == g_idx, logits, NEG)
    m1 = jnp.max(le, axis=-1, keepdims=True)
    e1 = jnp.min(jnp.where(le == m1, lane_f, 1e9), axis=-1, keepdims=True)
    le2 = jnp.where(lane_f == e1, NEG, le)
    m2 = jnp.max(le2, axis=-1, keepdims=True)
    e2 = jnp.min(jnp.where(le2 == m2, lane_f, 1e9), axis=-1, keepdims=True)
    ratio = jnp.exp(m2 - m1)
    w1 = g_prob / (1.0 + ratio)
    w2 = w1 * ratio
    xt_ref[:, D_MODEL:D_MODEL + LANE] = jnp.where(lane_f == e1, w1, 0.0) + jnp.where(lane_f == e2, w2, 0.0)

    tm = h.shape[0]
    onehot = jnp.where(lane_f == g_idx, 1.0, 0.0)
    earlier = jnp.where(lax.broadcasted_iota(jnp.int32, (tm, tm), 0) > lax.broadcasted_iota(jnp.int32, (tm, tm), 1),
                        1.0, 0.0).astype(BF16)
    prefix = _dot(earlier, onehot.astype(BF16)) + carry_ref[...]
    rank = jnp.sum(onehot * prefix, axis=-1, keepdims=True)
    carry_ref[...] += jnp.sum(onehot, axis=0, keepdims=True)
    route_ref[...] = jnp.where(lane == 0, g_idx, jnp.where(lane == 1, rank, 0.0))
    cnt_ref[...] = jnp.broadcast_to(carry_ref[...], cnt_ref.shape)


def _out_proj(x2, oa, ob, w, g, wr, br):
    T = x2.shape[0]
    tm = TM_PROJ
    row = lambda i: (i, 0)
    full = lambda a: pl.BlockSpec(a.shape, lambda i: (0,) * a.ndim)
    o4 = pl.BlockSpec((4, tm, LANE), lambda i: (0, i, 0))
    return pl.pallas_call(
        _outproj_kernel,
        grid=(T // tm,),
        in_specs=[pl.BlockSpec((tm, D_MODEL), row), o4, o4, full(w), full(g), full(wr), full(br)],
        out_specs=[pl.BlockSpec((tm, D_MODEL), row), pl.BlockSpec((tm, XT_WIDTH), row), pl.BlockSpec((tm, LANE), row),
                   pl.BlockSpec((8, LANE), lambda i: (0, 0))],
        out_shape=[jax.ShapeDtypeStruct((T, D_MODEL), F32), jax.ShapeDtypeStruct((T, XT_WIDTH), F32),
                   jax.ShapeDtypeStruct((T, LANE), F32), jax.ShapeDtypeStruct((8, LANE), F32)],
        scratch_shapes=[pltpu.VMEM((1, LANE), F32)],
        compiler_params=_cparams(("arbitrary",)),
        name="out_proj",
    )(x2, oa, ob, w, g, wr, br)


def _moe_kernel(tg_ref, ok_ref, idx_ref, idxn_ref, xt_hbm, wg_ref, wu_ref, wd_ref, y_ref, xbuf, sem):
    i = pl.program_id(0)
    slot = i % 2
    tm = xbuf.shape[1]

    def row_copy(index_ref, r, dst_slot):
        return pltpu.make_async_copy(xt_hbm.at[pl.ds(index_ref[0, 0, r], 1), :],
                                     xbuf.at[dst_slot, pl.ds(r, 1), :], sem.at[dst_slot])

    @pl.when(i == 0)
    def _():
        def body(r, c):
            row_copy(idx_ref, r, 0).start()
            return c
        lax.fori_loop(0, tm, body, 0)

    @pl.when((i == 0) | (ok_ref[jnp.maximum(i - 1, 0)] == 1))
    def _():
        pltpu.make_async_copy(xt_hbm.at[pl.ds(0, tm), :], xbuf.at[slot], sem.at[slot]).wait()

    @pl.when(ok_ref[i] == 0)
    def _():
        y_ref[...] = jnp.zeros(y_ref.shape, F32)

    @pl.when(ok_ref[i] == 1)
    def _():
        lane = lax.broadcasted_iota(jnp.int32, (1, LANE), 1)
        x = xbuf[slot, :, 0:D_MODEL].astype(BF16)
        cmb = xbuf[slot, :, D_MODEL:D_MODEL + LANE]
        first_lane = MOE_GROUPS + tg_ref[i] * EPG
        per_expert = tm // EPG
        y = jnp.zeros((tm, D_MODEL), F32)
        for e in range(EPG):
            a = _dot(x, wg_ref[0, e])
            b = _dot(x, wu_ref[0, e])
            ce = jnp.sum(jnp.where(lane == first_lane + e, cmb, 0.0), axis=-1, keepdims=True)
            y = y + _dot(((a * jax.nn.sigmoid(a)) * b * ce).astype(BF16), wd_ref[0, e])
            for r in range(e * per_expert, (e + 1) * per_expert):
                row_copy(idxn_ref, r, 1 - slot).start()
        y_ref[...] = y


def _moe(tile_group, tile_ok, src3, xt, wg, wu, wd):
    n_tiles, _, tm = src3.shape
    last = n_tiles - 1
    grid_spec = pltpu.PrefetchScalarGridSpec(
        num_scalar_prefetch=2,
        grid=(n_tiles,),
        in_specs=[pl.BlockSpec((1, 1, tm), lambda i, tg, ok: (i, 0, 0), memory_space=pltpu.SMEM),
                  pl.BlockSpec((1, 1, tm), lambda i, tg, ok: (jnp.minimum(i + 1, last), 0, 0), memory_space=pltpu.SMEM),
                  pl.BlockSpec(memory_space=pl.ANY),
                  pl.BlockSpec((1, EPG, D_MODEL, EXPERT_FF), lambda i, tg, ok: (tg[i], 0, 0, 0)),
                  pl.BlockSpec((1, EPG, D_MODEL, EXPERT_FF), lambda i, tg, ok: (tg[i], 0, 0, 0)),
                  pl.BlockSpec((1, EPG, EXPERT_FF, D_MODEL), lambda i, tg, ok: (tg[i], 0, 0, 0))],
        out_specs=pl.BlockSpec((tm, D_MODEL), lambda i, tg, ok: (i, 0)),
        scratch_shapes=[pltpu.VMEM((2, tm, XT_WIDTH), F32), pltpu.SemaphoreType.DMA((2,))],
    )
    return pl.pallas_call(
        _moe_kernel,
        grid_spec=grid_spec,
        out_shape=jax.ShapeDtypeStruct((n_tiles * tm, D_MODEL), F32),
        compiler_params=pltpu.CompilerParams(dimension_semantics=("arbitrary",), vmem_limit_bytes=VMEM_LIMIT_MOE),
        name="moe",
    )(tile_group, tile_ok, src3, src3, xt, wg, wu, wd)


def _final_kernel(pos_ref, posn_ref, y_hbm, h_ref, gf_ref, o_ref, ybuf, sem):
    i = pl.program_id(0)
    slot = i % 2
    tm = ybuf.shape[1]

    def issue(index_ref, dst_slot):
        def body(r, c):
            pltpu.make_async_copy(y_hbm.at[pl.ds(index_ref[0, 0, r], 1), :],
                                  ybuf.at[dst_slot, pl.ds(r, 1), :], sem.at[dst_slot]).start()
            return c
        lax.fori_loop(0, tm, body, 0, unroll=8)

    @pl.when(i == 0)
    def _():
        issue(pos_ref, 0)

    @pl.when(i + 1 < pl.num_programs(0))
    def _():
        issue(posn_ref, 1 - slot)

    pltpu.make_async_copy(y_hbm.at[pl.ds(0, tm), :], ybuf.at[slot], sem.at[slot]).wait()
    h = h_ref[...] + ybuf[slot]
    o_ref[...] = h * lax.rsqrt(jnp.mean(h * h, axis=-1, keepdims=True) + EPS) * gf_ref[...]


def _final(pos3, y_sorted, h, gf):
    n_tiles, _, tm = pos3.shape
    last = n_tiles - 1
    row = lambda i: (i, 0)
    return pl.pallas_call(
        _final_kernel,
        grid=(n_tiles,),
        in_specs=[pl.BlockSpec((1, 1, tm), lambda i: (i, 0, 0), memory_space=pltpu.SMEM),
                  pl.BlockSpec((1, 1, tm), lambda i: (jnp.minimum(i + 1, last), 0, 0), memory_space=pltpu.SMEM),
                  pl.BlockSpec(memory_space=pl.ANY),
                  pl.BlockSpec((tm, D_MODEL), row),
                  pl.BlockSpec((1, D_MODEL), lambda i: (0, 0))],
        out_specs=pl.BlockSpec((tm, D_MODEL), row),
        out_shape=jax.ShapeDtypeStruct((n_tiles * tm, D_MODEL), F32),
        scratch_shapes=[pltpu.VMEM((2, tm, D_MODEL), F32), pltpu.SemaphoreType.DMA((2,))],
        compiler_params=_cparams(("arbitrary",)),
        name="final",
    )(pos3, pos3, y_sorted, h, gf)


def _qa_perm():
    new = np.arange(NSA_HEADS * NSA_HD)
    r, g, d = new // LANE, (new % LANE) // NSA_HD, new % NSA_HD
    return (g * NSA_R + r) * NSA_HD + d


def _block_diag2(w):
    z = jnp.zeros_like(w)
    return jnp.concatenate([jnp.concatenate([w, z], axis=-1), jnp.concatenate([z, w], axis=-1)], axis=-2)


def kernel(x, rel_bias, ln_mix, w_in, cmp_pos_k, cmp_pos_v, cmp_k_w1, cmp_k_w2, cmp_v_w1, cmp_v_w2,
           diff_lq1, diff_lk1, diff_lq2, diff_lk2, diff_subln, w_out, ln_ffn,
           router_group_w, router_group_b, router_expert_w, router_expert_b,
           exp_w_gate, exp_w_up, exp_w_down, ln_final):
    B, S, D = x.shape
    T = B * S
    assert D == D_MODEL and S % TQ == 0 and S >= WINDOW and T % TM_MOE == 0
    x2 = x.reshape(T, D)
    perm = _qa_perm()

    w = w_in[0]
    c_kc, c_vc, c_ks, c_vs, c_kw, c_vw, c_gt = 512, 640, 768, 896, 1024, 1152, 1280
    c_qb = c_gt + N_GATE
    c_kb, c_vb = c_qb + 512, c_qb + 1024
    col = lambda c, n=LANE: w[:, c:c + n]
    w_tok = jnp.concatenate([w[:, perm], col(c_kc), col(c_vc), col(c_ks), col(c_kw), col(c_qb, 512), col(c_kb, 512)],
                            axis=1).astype(BF16)
    w_feat = jnp.concatenate([col(c_vs), col(c_vw), col(c_vb, 512), col(c_gt, N_GATE),
                              jnp.zeros((D, GATE_ROWS - N_GATE), F32)], axis=1).T.astype(BF16)
    qa, kc, vc, ks, kw, qb, kb, vst, vwt, vbt, gates_t = _in_proj(x2, ln_mix[0][None, :], w_tok, w_feat)

    w1k = _block_diag2(cmp_k_w1[0].reshape(CMP_LEN, NSA_HD, CMP_HIDDEN)).astype(BF16)
    w1v = _block_diag2(cmp_v_w1[0].reshape(CMP_LEN, NSA_HD, CMP_HIDDEN)).astype(BF16)
    w2k = _block_diag2(cmp_k_w2[0]).astype(BF16)
    w2v = _block_diag2(cmp_v_w2[0]).astype(BF16)
    posk = jnp.tile(cmp_pos_k[0], (1, NSA_G))
    posv = jnp.tile(cmp_pos_v[0], (1, NSA_G))
    kcmp, vcmpt = _compress(kc, vc, posk, posv, w1k, w1v, w2k, w2v, B, S)

    near, bias_c = _bias_tiles(rel_bias.T, S)

    n_sel = S // SEL_BLOCK
    nrow = S // CMP_STRIDE
    c_start = np.arange(nrow) * CMP_STRIDE
    s_start = np.arange(n_sel) * SEL_BLOCK
    ov = ((c_start[None, :] <= s_start[:, None] + SEL_BLOCK - 1)
          & (c_start[None, :] + CMP_LEN - 1 >= s_start[:, None])
          & (np.arange(nrow)[None, :] < nrow - 1)).astype(np.float32)
    o_a = _nsa(qa, gates_t, kcmp, vcmpt, bias_c, ks, vst, kw, vwt, near[:NSA_HEADS], jnp.asarray(ov, BF16), B, S)

    sub = jnp.tile(diff_subln[0], 2)[:, None]
    o_b = _diff(diff_lq1[0][None, :], diff_lk1[0][None, :], diff_lq2[0][None, :], diff_lk2[0][None, :],
                sub, qb, kb, vbt, near[NSA_HEADS:], B, S)

    w_o = jnp.concatenate([w_out[0][:512][perm], w_out[0][512:]], axis=0).astype(BF16)
    n_r = MOE_GROUPS + N_EXPERTS
    wr = jnp.concatenate([router_group_w[0], router_expert_w[0], jnp.zeros((D, LANE - n_r), F32)], axis=1).astype(BF16)
    br = jnp.concatenate([router_group_b[0], router_expert_b[0], jnp.zeros((LANE - n_r,), F32)])[None, :]
    h1, xt, route, counts = _out_proj(x2, o_a, o_b, w_o, ln_ffn[0][None, :], wr, br)

    tm = TM_MOE
    n_tiles = T // tm + MOE_GROUPS
    cnt = counts[0, :MOE_GROUPS].astype(jnp.int32)
    ends = jnp.cumsum((cnt + tm - 1) // tm * tm)
    starts = ends - (cnt + tm - 1) // tm * tm
    pos = starts[route[:, 0].astype(jnp.int32)] + route[:, 1].astype(jnp.int32)
    src = jnp.zeros((n_tiles * tm,), jnp.int32).at[pos].set(jnp.arange(T, dtype=jnp.int32))
    tile_start = jnp.arange(n_tiles, dtype=jnp.int32) * tm
    tile_group = jnp.minimum(jnp.searchsorted(ends, tile_start, side="right"), MOE_GROUPS - 1).astype(jnp.int32)
    tile_ok = (tile_start < ends[-1]).astype(jnp.int32)

    by_group = lambda a: a[0].astype(BF16).reshape((MOE_GROUPS, EPG) + a.shape[2:])
    y_sorted = _moe(tile_group, tile_ok, src.reshape(n_tiles, 1, tm), xt,
                    by_group(exp_w_gate), by_group(exp_w_up), by_group(exp_w_down))
    out = _final(pos.reshape(T // tm, 1, tm), y_sorted, h1, ln_final[None, :])
    return out.reshape(B, S, D)
```

```python
import math

import numpy as np
import jax
import jax.numpy as jnp
from jax import lax
from jax.experimental import pallas as pl
from jax.experimental.pallas import tpu as pltpu

F32 = jnp.float32
BF16 = jnp.bfloat16
NEG = -1e30
EPS = 1e-6
LOG2E = math.log2(math.e)

D_MODEL = 1024
LANE = 128
NSA_HEADS, NSA_G, NSA_R, NSA_HD = 8, 2, 4, 64
CMP_LEN, CMP_STRIDE, CMP_HIDDEN = 32, 16, 128
SEL_BLOCK, SEL_TOPN, SEL_FORCED_LOCAL, WINDOW = 64, 8, 2, 512
DIFF_HEADS, DIFF_HD = 8, 32
REL_BUCKETS, REL_MAX_EXACT, REL_MAX_DIST = 32, 16, 128
N_REL_HEADS = NSA_HEADS + DIFF_HEADS
MOE_GROUPS, EPG, N_EXPERTS, EXPERT_FF = 4, 8, 32, 256
LAMBDA_INIT = 0.8 - 0.6 * math.exp(-0.3 * 0)
N_GATE = NSA_HEADS * 3
GATE_ROWS = 32

TQ = 256
TM_PROJ = 512
TM_MOE = 256
XT_WIDTH = D_MODEL + LANE
VMEM_LIMIT = 48 * 1024 * 1024
VMEM_LIMIT_MOE = 56 * 1024 * 1024


def _cparams(sem):
    return pltpu.CompilerParams(dimension_semantics=sem, vmem_limit_bytes=VMEM_LIMIT)


def _dot(a, b):
    return jnp.dot(a, b, preferred_element_type=F32)


def _dot_nt(a, b):
    return lax.dot_general(a, b, (((1,), (1,)), ((), ())), preferred_element_type=F32)


def _bucket_thresholds():
    n = np.arange(0, REL_MAX_DIST + 1)
    nf = np.maximum(n, 1).astype(np.float32)
    large = REL_MAX_EXACT + (np.log(nf / np.float32(REL_MAX_EXACT)) / np.float32(math.log(REL_MAX_DIST / REL_MAX_EXACT))
                             * np.float32(REL_BUCKETS - REL_MAX_EXACT)).astype(np.int32)
    large = np.minimum(large, REL_BUCKETS - 1)
    bucket = np.where(n < REL_MAX_EXACT, n, large)
    return [int(np.argmax(bucket >= b)) for b in range(REL_BUCKETS)]


_THR = _bucket_thresholds()


def _inproj_kernel(x_ref, g_ref, w_ref, wt_ref, qa_ref, kc_ref, vc_ref, ks_ref, kw_ref, qb_ref, kb_ref,
                   vst_ref, vwt_ref, vbt_ref, gt_ref):
    x = x_ref[...]
    xn = (x * lax.rsqrt(jnp.mean(x * x, axis=-1, keepdims=True) + EPS) * g_ref[...]).astype(BF16)
    a = _dot(xn, w_ref[:, 0:512]) * (NSA_HD ** -0.5 * LOG2E)
    for r in range(4):
        qa_ref[r] = a[:, r * LANE:(r + 1) * LANE].astype(BF16)
    a = _dot(xn, w_ref[:, 512:1024])
    kc_ref[...] = a[:, 0:128]
    vc_ref[...] = a[:, 128:256]
    ks_ref[...] = a[:, 256:384].astype(BF16)
    kw_ref[...] = a[:, 384:512].astype(BF16)
    a = _dot(xn, w_ref[:, 1024:1536]) * (DIFF_HD ** -0.5 * LOG2E)
    for r in range(4):
        qb_ref[r] = a[:, r * LANE:(r + 1) * LANE].astype(BF16)
    a = _dot(xn, w_ref[:, 1536:2048])
    for r in range(4):
        kb_ref[r] = a[:, r * LANE:(r + 1) * LANE].astype(BF16)
    vst_ref[...] = _dot_nt(wt_ref[0:128, :], xn).astype(BF16)
    vwt_ref[...] = _dot_nt(wt_ref[128:256, :], xn).astype(BF16)
    for r in range(4):
        vbt_ref[r] = _dot_nt(wt_ref[256 + r * LANE:384 + r * LANE, :], xn).astype(BF16)
    gt_ref[...] = _dot_nt(wt_ref[768:768 + GATE_ROWS, :], xn)


def _in_proj(x2, g, w, wt):
    T = x2.shape[0]
    tm = TM_PROJ
    row = lambda i: (i, 0)
    o128b = jax.ShapeDtypeStruct((T, LANE), BF16)
    o128f = jax.ShapeDtypeStruct((T, LANE), F32)
    o4 = jax.ShapeDtypeStruct((4, T, LANE), BF16)
    ot = jax.ShapeDtypeStruct((LANE, T), BF16)
    o4t = jax.ShapeDtypeStruct((4, LANE, T), BF16)
    s128 = pl.BlockSpec((tm, LANE), row)
    s4 = pl.BlockSpec((4, tm, LANE), lambda i: (0, i, 0))
    st = pl.BlockSpec((LANE, tm), lambda i: (0, i))
    s4t = pl.BlockSpec((4, LANE, tm), lambda i: (0, 0, i))
    return pl.pallas_call(
        _inproj_kernel,
        grid=(T // tm,),
        in_specs=[pl.BlockSpec((tm, D_MODEL), row),
                  pl.BlockSpec((1, D_MODEL), lambda i: (0, 0)),
                  pl.BlockSpec(w.shape, lambda i: (0, 0)),
                  pl.BlockSpec(wt.shape, lambda i: (0, 0))],
        out_specs=[s4, s128, s128, s128, s128, s4, s4, st, st, s4t, pl.BlockSpec((GATE_ROWS, tm), lambda i: (0, i))],
        out_shape=[o4, o128f, o128f, o128b, o128b, o4, o4, ot, ot, o4t, jax.ShapeDtypeStruct((GATE_ROWS, T), F32)],
        compiler_params=_cparams(("parallel",)),
        name="in_proj",
    )(x2, g, w, wt)


def _gelu_tanh(x):
    return 0.5 * x * (1.0 + jnp.tanh(math.sqrt(2.0 / math.pi) * (x + 0.044715 * (x * x * x))))


def _compress_kernel(kc_ref, vc_ref, posk_ref, posv_ref, w1k_ref, w1v_ref, w2k_ref, w2v_ref, ko_ref, vo_ref):
    nrow = kc_ref.shape[0] // CMP_STRIDE
    rid = lax.broadcasted_iota(jnp.int32, (nrow, 1), 0)
    cid = lax.broadcasted_iota(jnp.int32, (1, nrow), 1)
    for src, pos, w1, w2, out, transposed in ((kc_ref, posk_ref, w1k_ref, w2k_ref, ko_ref, False),
                                              (vc_ref, posv_ref, w1v_ref, w2v_ref, vo_ref, True)):
        hid_a = jnp.zeros((nrow, 2 * CMP_HIDDEN), F32)
        hid_b = jnp.zeros((nrow, 2 * CMP_HIDDEN), F32)
        for m in range(CMP_STRIDE):
            y = src[pl.ds(m, nrow, stride=CMP_STRIDE), :]
            hid_a = hid_a + _dot((y + pos[m:m + 1, :]).astype(BF16), w1[m])
            hid_b = hid_b + _dot((y + pos[CMP_STRIDE + m:CMP_STRIDE + m + 1, :]).astype(BF16), w1[CMP_STRIDE + m])
        hid = hid_a + pltpu.roll(hid_b, nrow - 1, 0)
        o = _dot(_gelu_tanh(hid).astype(BF16), w2[...])
        if transposed:
            out[0] = jnp.where(cid < nrow - 1, o.T, 0.0).astype(BF16)
        else:
            out[0] = jnp.where(rid < nrow - 1, o, 0.0).astype(BF16)


def _compress(kc, vc, posk, posv, w1k, w1v, w2k, w2v, B, S):
    nrow = S // CMP_STRIDE
    assert nrow == LANE
    full = lambda a: pl.BlockSpec(a.shape, lambda b: (0,) * a.ndim)
    src = pl.BlockSpec((S, LANE), lambda b: (b, 0))
    osp = pl.BlockSpec((1, nrow, LANE), lambda b: (b, 0, 0))
    osh = jax.ShapeDtypeStruct((B, nrow, LANE), BF16)
    return pl.pallas_call(
        _compress_kernel,
        grid=(B,),
        in_specs=[src, src, full(posk), full(posv), full(w1k), full(w1v), full(w2k), full(w2v)],
        out_specs=[osp, osp],
        out_shape=[osh, osh],
        compiler_params=_cparams(("parallel",)),
        name="compress",
    )(kc, vc, posk, posv, w1k, w1v, w2k, w2v)


def _bias_from_dist(dist, tab_ref, h):
    val = jnp.full(dist.shape, tab_ref[h, 0], F32)
    for b in range(1, REL_BUCKETS):
        val = jnp.where(dist >= _THR[b], tab_ref[h, b], val)
    return val


def _bias_near_kernel(tab_ref, out_ref):
    h = pl.program_id(0)
    j = lax.broadcasted_iota(jnp.int32, (TQ, TQ), 0)
    i = lax.broadcasted_iota(jnp.int32, (TQ, TQ), 1)
    for d in range(2):
        bias = (_bias_from_dist(i - j + d * TQ, tab_ref, h) - tab_ref[h, REL_BUCKETS - 1]) * LOG2E
        out_ref[0, d] = jnp.where(i - j + d * TQ >= 0, bias, NEG)


def _bias_cmp_kernel(tab_ref, out_ref):
    h = pl.program_id(0)
    c = lax.broadcasted_iota(jnp.int32, (LANE, TQ), 0)
    t = pl.program_id(1) * TQ + lax.broadcasted_iota(jnp.int32, (LANE, TQ), 1)
    out_ref[0] = _bias_from_dist(t - (c * CMP_STRIDE + CMP_LEN - 1), tab_ref, h) * LOG2E


def _bias_tiles(tab_t, S):
    smem = pl.BlockSpec(memory_space=pltpu.SMEM)
    near = pl.pallas_call(
        _bias_near_kernel,
        grid=(N_REL_HEADS,),
        in_specs=[smem],
        out_specs=pl.BlockSpec((1, 2, TQ, TQ), lambda h: (h, 0, 0, 0)),
        out_shape=jax.ShapeDtypeStruct((N_REL_HEADS, 2, TQ, TQ), F32),
        compiler_params=_cparams(("parallel",)),
        name="bias_near",
    )(tab_t)
    cmp_bias = pl.pallas_call(
        _bias_cmp_kernel,
        grid=(NSA_HEADS, S // TQ),
        in_specs=[smem],
        out_specs=pl.BlockSpec((1, LANE, TQ), lambda h, q: (h, 0, q)),
        out_shape=jax.ShapeDtypeStruct((NSA_HEADS, LANE, S), F32),
        compiler_params=_cparams(("parallel", "parallel")),
        name="bias_cmp",
    )(tab_t)
    return near, cmp_bias


def _flash_first(ss, vt):
    ms = [jnp.max(s, axis=0, keepdims=True) for s in ss]
    ps = [jnp.exp2(s - m) for s, m in zip(ss, ms)]
    ls = [jnp.sum(p, axis=0, keepdims=True) for p in ps]
    return tuple((m, l, _dot(vt, p.astype(BF16))) for m, l, p in zip(ms, ls, ps))


def _flash_update(ss, vt, sts):
    ms = [jnp.maximum(st[0], jnp.max(s, axis=0, keepdims=True)) for s, st in zip(ss, sts)]
    alphas = [jnp.exp2(st[0] - m) for m, st in zip(ms, sts)]
    ps = [jnp.exp2(s - m) for s, m in zip(ss, ms)]
    ls = [a * st[1] + jnp.sum(p, axis=0, keepdims=True) for a, st, p in zip(alphas, sts, ps)]
    return tuple((m, l, a * st[2] + _dot(vt, p.astype(BF16)))
                 for m, l, a, st, p in zip(ms, ls, alphas, sts, ps))


def _flash_out(st):
    return st[2] * (1.0 / st[1])


def _stream_scratch(n):
    return ([pltpu.VMEM((8, TQ), F32), pltpu.VMEM((LANE, TQ), F32)] * n
            + [pltpu.VMEM((TQ, TQ), F32)] * (2 * n) + [pltpu.VMEM((TQ, TQ), BF16)] * (2 * n))


def _stream_refs(refs, n):
    st_refs = tuple((refs[2 * c], refs[2 * c + 1]) for c in range(n))
    s, p = refs[2 * n:4 * n], refs[4 * n:6 * n]
    return st_refs, tuple(s[:n]), tuple(s[n:]), tuple(p[:n]), tuple(p[n:])


def _flash_load(refs):
    return tuple((st[0:1, :], st[1:2, :], acc[...]) for st, acc in refs)


def _flash_store(refs, sts):
    for (st, acc), (m, l, a) in zip(refs, sts):
        st[0:1, :] = m
        st[1:2, :] = l
        acc[...] = a


def _causal_stream(qi, scores, vtile, st_refs, s_a, s_b, p_a, p_b):
    n_far = jnp.maximum(qi - 1, 0)
    top = qi - 2

    def write(s_refs, kt, near=2):
        for ref, s in zip(s_refs, scores(jnp.maximum(kt, 0), near)):
            ref[...] = s

    def pending(p_refs, kt):
        vt = vtile(jnp.maximum(kt, 0))
        return [_dot(vt, ref[...]) for ref in p_refs]

    def step(s_cur, p_cur, s_next, kt_next, p_prev, kt_prev, near_next=2):
        write(s_next, kt_next, near_next)
        pend = pending(p_prev, kt_prev)
        for (st, acc), s_ref, p_ref, pv in zip(st_refs, s_cur, p_cur, pend):
            s = s_ref[...]
            m_old = st[0:1, :]
            m = jnp.maximum(m_old, jnp.max(s, axis=0, keepdims=True))
            alpha = jnp.exp2(m_old - m)
            p = jnp.exp2(s - m)
            p_ref[...] = p.astype(BF16)
            st[0:1, :] = m
            st[1:2, :] = alpha * st[1:2, :] + jnp.sum(p, axis=0, keepdims=True)
            acc[...] = alpha * (acc[...] + pv)

    write(s_a, qi, 0)
    write(s_b, qi - 1, 1)
    for (st, acc), s_ref, p_ref in zip(st_refs, s_a, p_a):
        s = s_ref[...]
        m = jnp.max(s, axis=0, keepdims=True)
        p = jnp.exp2(s - m)
        p_ref[...] = p.astype(BF16)
        st[0:1, :] = m
        st[1:2, :] = jnp.sum(p, axis=0, keepdims=True)
        acc[...] = jnp.zeros(acc.shape, F32)
    step(s_b, p_b, s_a, top, p_a, qi)

    def pair(j, c):
        kt = top - 2 * j
        step(s_a, p_a, s_b, kt - 1, p_b, kt + 1)
        step(s_b, p_b, s_a, kt - 2, p_a, kt)
        return c
    lax.fori_loop(0, n_far // 2, pair, 0)

    @pl.when(n_far % 2 == 1)
    def _():
        for (st, acc), s_ref, pv in zip(st_refs, s_a, pending(p_b, 1)):
            s = s_ref[...]
            m_old = st[0:1, :]
            m = jnp.maximum(m_old, jnp.max(s, axis=0, keepdims=True))
            alpha = jnp.exp2(m_old - m)
            p = jnp.exp2(s - m)
            st[0:1, :] = m
            st[1:2, :] = alpha * st[1:2, :] + jnp.sum(p, axis=0, keepdims=True)
            acc[...] = alpha * (acc[...] + pv) + _dot(vtile(0), p.astype(BF16))

    @pl.when(n_far % 2 == 0)
    def _():
        last = jnp.where(n_far == 0, qi - 1, 0)
        for (st, acc), pv in zip(st_refs, pending(p_b, last)):
            acc[...] = acc[...] + pv
    return _flash_load(st_refs)


def _ktile(ref, kt):
    return ref[pl.ds(pl.multiple_of(kt * TQ, TQ), TQ), :]


def _vtile(ref, kt):
    return ref[:, pl.ds(pl.multiple_of(kt * TQ, TQ), TQ)]


def _nsa_kernel(qa_ref, gt_ref, kcmp_ref, vcmpt_ref, bc_ref, ks_ref, vst_ref, kw_ref, vwt_ref, dn_ref,
                ov_ref, o_ref, psum_ref, sel_ref, oacc_ref, sig_ref, *scratch):
    st_refs, s_a, s_b, p_a, p_b = _stream_refs(scratch, 2 * NSA_G)
    qi = pl.program_id(1)
    t0 = qi * TQ
    lane = lax.broadcasted_iota(jnp.int32, (1, LANE), 1)
    lane_grp = lax.shift_right_arithmetic(lane, 6)
    sub_grp = lax.shift_right_arithmetic(lax.broadcasted_iota(jnp.int32, (LANE, 1), 0), 6)
    sig_ref[...] = jax.nn.sigmoid(gt_ref[...])

    def gate_row(c):
        return sig_ref[pl.ds(c, 1), :]

    def masked_q(r, g):
        return jnp.where(lane_grp == g, qa_ref[r].astype(F32), 0.0).astype(BF16)

    n_cmp = kcmp_ref.shape[1] - 1
    crow = lax.broadcasted_iota(jnp.int32, (LANE, 1), 0)
    cmp_end = jnp.where(crow < n_cmp, crow * CMP_STRIDE + CMP_LEN - 1, 1 << 30)
    mask_c = (t0 + lax.broadcasted_iota(jnp.int32, (1, TQ), 1)) >= cmp_end

    heads = [(r, g) for r in range(NSA_R) for g in range(NSA_G)]
    scores_c = [_dot_nt(kcmp_ref[0], masked_q(r, g)) for r, g in heads]
    probs_c = []
    for (r, g), s in zip(heads, scores_c):
        s = jnp.where(mask_c, s + bc_ref[g * NSA_R + r], NEG)
        p = jnp.where(mask_c, jnp.exp2(s - jnp.max(s, axis=0, keepdims=True)), 0.0)
        l = jnp.sum(p, axis=0, keepdims=True)
        probs_c.append(p * jnp.where(l > 0.0, 1.0 / l, 0.0))
    for g in range(NSA_G):
        psum_ref[g] = sum(p for (r, gg), p in zip(heads, probs_c) if gg == g)
    outs_c = [_dot(vcmpt_ref[0], p.astype(BF16)) for p in probs_c]
    for r in range(NSA_R):
        o0, o1 = outs_c[r * NSA_G], outs_c[r * NSA_G + 1]
        oacc_ref[r] = jnp.where(sub_grp == 0, gate_row(r * 3) * o0, gate_row((NSA_R + r) * 3) * o1)

    n_sel = ov_ref.shape[0]
    jj = lax.broadcasted_iota(jnp.int32, (n_sel, TQ), 0)
    cur = lax.shift_right_arithmetic(t0 + lax.broadcasted_iota(jnp.int32, (n_sel, TQ), 1), 6)
    valid = jj <= cur
    forced = (jj == 0) | (cur - jj < SEL_FORCED_LOCAL)
    for g in range(NSA_G):
        ps = psum_ref[g]
        hi = ps.astype(BF16)
        rem = ps - hi.astype(F32)
        mid = rem.astype(BF16)
        lo = (rem - mid.astype(F32)).astype(BF16)
        ov = ov_ref[...]
        imp = _dot(ov, hi) + _dot(ov, mid) + _dot(ov, lo)
        score = jnp.where(valid, jnp.where(forced, 1e9, imp), -1e9)
        cnt = jnp.zeros((n_sel, TQ), F32)
        for j2 in range(n_sel):
            row = score[j2:j2 + 1, :]
            tie = jnp.where(jj > j2, 1.0, 0.0)
            cnt = cnt + jnp.where(row > score, 1.0, jnp.where(row == score, tie, 0.0))
        sel_ref[g] = jnp.where(cnt < float(min(SEL_TOPN, n_sel)), jnp.where(score > -1e8, 0.0, NEG), NEG)

    has_prev = jnp.where(qi >= 1, 0.0, NEG)
    has_wfar = jnp.where(qi >= WINDOW // TQ, 0.0, NEG)
    kt_prev = jnp.maximum(qi - 1, 0)
    kt_wfar = jnp.maximum(qi - WINDOW // TQ, 0)
    wfar_mask = lax.broadcasted_iota(jnp.int32, (TQ, TQ), 0) > lax.broadcasted_iota(jnp.int32, (TQ, TQ), 1)

    def main_body(rp, carry):
        chains = [(2 * rp + i, g) for i in range(2) for g in range(NSA_G)]
        qms = [masked_q(r, g) for r, g in chains]
        hids = [g * NSA_R + r for r, g in chains]

        def block_mask(g, kt):
            per_tile = TQ // SEL_BLOCK
            rows = [jnp.broadcast_to(sel_ref[g, pl.ds(kt * per_tile + i, 1), :], (SEL_BLOCK, TQ))
                    for i in range(per_tile)]
            return jnp.concatenate(rows, axis=0)

        def near_bias(ss, near):
            if near == 0:
                return [s + dn_ref[hid, 0] for s, hid in zip(ss, hids)]
            if near == 1:
                return [s + dn_ref[hid, 1] + has_prev for s, hid in zip(ss, hids)]
            return ss

        def finish(sts, branch):
            for i in range(2):
                o0, o1 = _flash_out(sts[2 * i]), _flash_out(sts[2 * i + 1])
                oacc_ref[2 * rp + i] += jnp.where(sub_grp == 0, gate_row(hids[2 * i] * 3 + branch) * o0,
                                                  gate_row(hids[2 * i + 1] * 3 + branch) * o1)

        def slc_scores(kt, near=2):
            k = _ktile(ks_ref, kt)
            ss = near_bias([_dot_nt(k, qm) for qm in qms], near)
            masks = [block_mask(g, kt) for g in range(NSA_G)]
            return [s + masks[g] for s, (r, g) in zip(ss, chains)]

        def slc_vtile(kt):
            return _vtile(vst_ref, kt)

        finish(_causal_stream(qi, slc_scores, slc_vtile, st_refs, s_a, s_b, p_a, p_b), 1)

        win = []
        for kt, near in ((qi, 0), (kt_prev, 1), (kt_wfar, 2)):
            k = _ktile(kw_ref, kt)
            ss = near_bias([_dot_nt(k, qm) for qm in qms], near)
            if near == 2:
                ss = [jnp.where(wfar_mask, s + has_wfar, NEG) for s in ss]
            win.append(ss)
        sts = _flash_first(win[0], _vtile(vwt_ref, qi))
        sts = _flash_update(win[1], _vtile(vwt_ref, kt_prev), sts)
        sts = _flash_update(win[2], _vtile(vwt_ref, kt_wfar), sts)
        finish(sts, 2)
        return carry
    lax.fori_loop(0, NSA_R // 2, main_body, 0)

    for r in range(NSA_R):
        o_ref[r] = oacc_ref[r].T.astype(BF16)


def _nsa(qa, gates_t, kcmp, vcmpt, bias_c, ks, vst, kw, vwt, near_a, ov, B, S):
    nq = S // TQ
    T = B * S
    k_spec = pl.BlockSpec((S, LANE), lambda b, q: (b, 0))
    v_spec = pl.BlockSpec((LANE, S), lambda b, q: (0, b))
    full = lambda a: pl.BlockSpec(a.shape, lambda b, q: (0,) * a.ndim)
    cmp_spec = pl.BlockSpec((1, LANE, LANE), lambda b, q: (b, 0, 0))
    qo = pl.BlockSpec((4, TQ, LANE), lambda b, q: (0, b * nq + q, 0))
    return pl.pallas_call(
        _nsa_kernel,
        grid=(B, nq),
        in_specs=[qo, pl.BlockSpec((GATE_ROWS, TQ), lambda b, q: (0, b * nq + q)),
                  cmp_spec, cmp_spec,
                  pl.BlockSpec((NSA_HEADS, LANE, TQ), lambda b, q: (0, 0, q)),
                  k_spec, v_spec, k_spec, v_spec, full(near_a), full(ov)],
        out_specs=qo,
        out_shape=jax.ShapeDtypeStruct((4, T, LANE), BF16),
        scratch_shapes=[pltpu.VMEM((NSA_G, LANE, TQ), F32), pltpu.VMEM((NSA_G, S // SEL_BLOCK, TQ), F32),
                        pltpu.VMEM((NSA_R, LANE, TQ), F32), pltpu.VMEM((GATE_ROWS, TQ), F32)] + _stream_scratch(2 * NSA_G),
        compiler_params=_cparams(("parallel", "arbitrary")),
        name="nsa",
    )(qa, gates_t, kcmp, vcmpt, bias_c, ks, vst, kw, vwt, near_a, ov)


def _diff_kernel(lq1_ref, lk1_ref, lq2_ref, lk2_ref, sub_ref, qb_ref, kb_ref, vbt_ref, dn_ref, o_ref, *scratch):
    n = 2 * 2
    st_refs, s_a, s_b, p_a, p_b = _stream_refs(scratch, n)
    qi = pl.program_id(1)
    lane = lax.broadcasted_iota(jnp.int32, (1, LANE), 1)
    chain_of_lane = lax.shift_right_arithmetic(lane, 5)
    sub_half = lax.shift_right_arithmetic(lax.broadcasted_iota(jnp.int32, (LANE, 1), 0), 6)
    lam = (jnp.exp(jnp.sum(lq1_ref[...] * lk1_ref[...], axis=-1, keepdims=True))
           - jnp.exp(jnp.sum(lq2_ref[...] * lk2_ref[...], axis=-1, keepdims=True)) + LAMBDA_INIT)
    has_prev = jnp.where(qi >= 1, 0.0, NEG)

    def pair_body(pr, carry):
        q = qb_ref[pr].astype(F32)
        qms = [jnp.where(chain_of_lane == c, q, 0.0).astype(BF16) for c in range(n)]

        def scores(kt, near):
            k = kb_ref[pr, pl.ds(pl.multiple_of(kt * TQ, TQ), TQ), :]
            ss = [_dot_nt(k, qm) for qm in qms]
            if near == 2:
                return ss
            bias = [dn_ref[2 * pr + hh, near] for hh in range(2)]
            if near == 1:
                bias = [b + has_prev for b in bias]
            return [s + bias[c // 2] for c, s in enumerate(ss)]

        def vtile(kt):
            return vbt_ref[pr, :, pl.ds(pl.multiple_of(kt * TQ, TQ), TQ)]

        sts = _causal_stream(qi, scores, vtile, st_refs, s_a, s_b, p_a, p_b)
        out = jnp.zeros((LANE, TQ), F32)
        for hh in range(2):
            o = _flash_out(sts[2 * hh]) - lam * _flash_out(sts[2 * hh + 1])
            o = jnp.where(sub_half == hh, o, 0.0)
            ms = jnp.sum(o * o, axis=0, keepdims=True) * (1.0 / (2 * DIFF_HD))
            out = out + o * lax.rsqrt(ms + EPS)
        o_ref[pr] = (out * sub_ref[...] * (1.0 - LAMBDA_INIT)).T.astype(BF16)
        return carry
    lax.fori_loop(0, DIFF_HEADS // 2, pair_body, 0)


def _diff(lq1, lk1, lq2, lk2, sub, qb, kb, vbt, near_b, B, S):
    nq = S // TQ
    T = B * S
    full = lambda a: pl.BlockSpec(a.shape, lambda b, q: (0,) * a.ndim)
    k_spec = pl.BlockSpec((4, S, LANE), lambda b, q: (0, b, 0))
    v_spec = pl.BlockSpec((4, LANE, S), lambda b, q: (0, 0, b))
    qo = pl.BlockSpec((4, TQ, LANE), lambda b, q: (0, b * nq + q, 0))
    return pl.pallas_call(
        _diff_kernel,
        grid=(B, nq),
        in_specs=[full(lq1), full(lk1), full(lq2), full(lk2), full(sub), qo, k_spec, v_spec, full(near_b)],
        out_specs=qo,
        out_shape=jax.ShapeDtypeStruct((4, T, LANE), BF16),
        scratch_shapes=_stream_scratch(4),
        compiler_params=_cparams(("parallel", "arbitrary")),
        name="diff",
    )(lq1, lk1, lq2, lk2, sub, qb, kb, vbt, near_b)


def _outproj_kernel(x_ref, oa_ref, ob_ref, w_ref, g_ref, wr_ref, br_ref, h_ref, xt_ref, route_ref, cnt_ref,
                    carry_ref):
    @pl.when(pl.program_id(0) == 0)
    def _():
        carry_ref[...] = jnp.zeros(carry_ref.shape, F32)

    o = jnp.concatenate([oa_ref[r] for r in range(4)] + [ob_ref[r] for r in range(4)], axis=-1)
    h = x_ref[...] + _dot(o, w_ref[...])
    h_ref[...] = h
    tn32 = h * lax.rsqrt(jnp.mean(h * h, axis=-1, keepdims=True) + EPS) * g_ref[...]
    xt_ref[:, 0:D_MODEL] = tn32
    tn = tn32.astype(BF16)
    logits = _dot(tn, wr_ref[...]) + br_ref[...]
    lane = lax.broadcasted_iota(jnp.int32, (1, LANE), 1)
    lane_f = lane.astype(F32)
    is_grp = lane < MOE_GROUPS
    lg = jnp.where(is_grp, logits, NEG)
    mg = jnp.max(lg, axis=-1, keepdims=True)
    zg = jnp.sum(jnp.where(is_grp, jnp.exp(lg - mg), 0.0), axis=-1, keepdims=True)
    g_prob = 1.0 / zg
    g_idx = jnp.min(jnp.where(lg == mg, lane_f, 1e9), axis=-1, keepdims=True)
    lane_grp = jnp.where((lane >= MOE_GROUPS) & (lane < MOE_GROUPS + N_EXPERTS),
                         lax.shift_right_arithmetic(lane - MOE_GROUPS, 3), -1).astype(F32)
    le = jnp.where(lane_grp == g_idx, logits, NEG)
    m1 = jnp.max(le, axis=-1, keepdims=True)
    e1 = jnp.min(jnp.where(le == m1, lane_f, 1e9), axis=-1, keepdims=True)
    le2 = jnp.where(lane_f == e1, NEG, le)
    m2 = jnp.max(le2, axis=-1, keepdims=True)
    e2 = jnp.min(jnp.where(le2 == m2, lane_f, 1e9), axis=-1, keepdims=True)
    ratio = jnp.exp(m2 - m1)
    w1 = g_prob / (1.0 + ratio)
    w2 = w1 * ratio
    xt_ref[:, D_MODEL:D_MODEL + LANE] = jnp.where(lane_f == e1, w1, 0.0) + jnp.where(lane_f == e2, w2, 0.0)

    tm = h.shape[0]
    onehot = jnp.where(lane_f == g_idx, 1.0, 0.0)
    earlier = jnp.where(lax.broadcasted_iota(jnp.int32, (tm, tm), 0) > lax.broadcasted_iota(jnp.int32, (tm, tm), 1),
                        1.0, 0.0).astype(BF16)
    prefix = _dot(earlier, onehot.astype(BF16)) + carry_ref[...]
    rank = jnp.sum(onehot * prefix, axis=-1, keepdims=True)
    carry_ref[...] += jnp.sum(onehot, axis=0, keepdims=True)
    route_ref[...] = jnp.where(lane == 0, g_idx, jnp.where(lane == 1, rank, 0.0))
    cnt_ref[...] = jnp.broadcast_to(carry_ref[...], cnt_ref.shape)


def _out_proj(x2, oa, ob, w, g, wr, br):
    T = x2.shape[0]
    tm = TM_PROJ
    row = lambda i: (i, 0)
    full = lambda a: pl.BlockSpec(a.shape, lambda i: (0,) * a.ndim)
    o4 = pl.BlockSpec((4, tm, LANE), lambda i: (0, i, 0))
    return pl.pallas_call(
        _outproj_kernel,
        grid=(T // tm,),
        in_specs=[pl.BlockSpec((tm, D_MODEL), row), o4, o4, full(w), full(g), full(wr), full(br)],
        out_specs=[pl.BlockSpec((tm, D_MODEL), row), pl.BlockSpec((tm, XT_WIDTH), row), pl.BlockSpec((tm, LANE), row),
                   pl.BlockSpec((8, LANE), lambda i: (0, 0))],
        out_shape=[jax.ShapeDtypeStruct((T, D_MODEL), F32), jax.ShapeDtypeStruct((T, XT_WIDTH), F32),
                   jax.ShapeDtypeStruct((T, LANE), F32), jax.ShapeDtypeStruct((8, LANE), F32)],
        scratch_shapes=[pltpu.VMEM((1, LANE), F32)],
        compiler_params=_cparams(("arbitrary",)),
        name="out_proj",
    )(x2, oa, ob, w, g, wr, br)


def _moe_kernel(tg_ref, ok_ref, idx_ref, idxn_ref, xt_hbm, wg_ref, wu_ref, wd_ref, y_ref, xbuf, sem):
    i = pl.program_id(0)
    slot = i % 2
    tm = xbuf.shape[1]

    def row_copy(index_ref, r, dst_slot):
        return pltpu.make_async_copy(xt_hbm.at[pl.ds(index_ref[0, 0, r], 1), :],
                                     xbuf.at[dst_slot, pl.ds(r, 1), :], sem.at[dst_slot])

    @pl.when(i == 0)
    def _():
        def body(r, c):
            row_copy(idx_ref, r, 0).start()
            return c
        lax.fori_loop(0, tm, body, 0)

    @pl.when((i == 0) | (ok_ref[jnp.maximum(i - 1, 0)] == 1))
    def _():
        pltpu.make_async_copy(xt_hbm.at[pl.ds(0, tm), :], xbuf.at[slot], sem.at[slot]).wait()

    @pl.when(ok_ref[i] == 0)
    def _():
        y_ref[...] = jnp.zeros(y_ref.shape, F32)

    @pl.when(ok_ref[i] == 1)
    def _():
        lane = lax.broadcasted_iota(jnp.int32, (1, LANE), 1)
        x = xbuf[slot, :, 0:D_MODEL].astype(BF16)
        cmb = xbuf[slot, :, D_MODEL:D_MODEL + LANE]
        first_lane = MOE_GROUPS + tg_ref[i] * EPG
        per_expert = tm // EPG
        y = jnp.zeros((tm, D_MODEL), F32)
        for e in range(EPG):
            a = _dot(x, wg_ref[0, e])
            b = _dot(x, wu_ref[0, e])
            ce = jnp.sum(jnp.where(lane == first_lane + e, cmb, 0.0), axis=-1, keepdims=True)
            y = y + _dot(((a * jax.nn.sigmoid(a)) * b * ce).astype(BF16), wd_ref[0, e])
            for r in range(e * per_expert, (e + 1) * per_expert):
                row_copy(idxn_ref, r, 1 - slot).start()
        y_ref[...] = y


def _moe(tile_group, tile_ok, src3, xt, wg, wu, wd):
    n_tiles, _, tm = src3.shape
    last = n_tiles - 1
    grid_spec = pltpu.PrefetchScalarGridSpec(
        num_scalar_prefetch=2,
        grid=(n_tiles,),
        in_specs=[pl.BlockSpec((1, 1, tm), lambda i, tg, ok: (i, 0, 0), memory_space=pltpu.SMEM),
                  pl.BlockSpec((1, 1, tm), lambda i, tg, ok: (jnp.minimum(i + 1, last), 0, 0), memory_space=pltpu.SMEM),
                  pl.BlockSpec(memory_space=pl.ANY),
                  pl.BlockSpec((1, EPG, D_MODEL, EXPERT_FF), lambda i, tg, ok: (tg[i], 0, 0, 0)),
                  pl.BlockSpec((1, EPG, D_MODEL, EXPERT_FF), lambda i, tg, ok: (tg[i], 0, 0, 0)),
                  pl.BlockSpec((1, EPG, EXPERT_FF, D_MODEL), lambda i, tg, ok: (tg[i], 0, 0, 0))],
        out_specs=pl.BlockSpec((tm, D_MODEL), lambda i, tg, ok: (i, 0)),
        scratch_shapes=[pltpu.VMEM((2, tm, XT_WIDTH), F32), pltpu.SemaphoreType.DMA((2,))],
    )
    return pl.pallas_call(
        _moe_kernel,
        grid_spec=grid_spec,
        out_shape=jax.ShapeDtypeStruct((n_tiles * tm, D_MODEL), F32),
        compiler_params=pltpu.CompilerParams(dimension_semantics=("arbitrary",), vmem_limit_bytes=VMEM_LIMIT_MOE),
        name="moe",
    )(tile_group, tile_ok, src3, src3, xt, wg, wu, wd)


def _final_kernel(pos_ref, posn_ref, y_hbm, h_ref, gf_ref, o_ref, ybuf, sem):
    i = pl.program_id(0)
    slot = i % 2
    tm = ybuf.shape[1]

    def issue(index_ref, dst_slot):
        def body(r, c):
            pltpu.make_async_copy(y_hbm.at[pl.ds(index_ref[0, 0, r], 1), :],
                                  ybuf.at[dst_slot, pl.ds(r, 1), :], sem.at[dst_slot]).start()
            return c
        lax.fori_loop(0, tm, body, 0, unroll=8)

    @pl.when(i == 0)
    def _():
        issue(pos_ref, 0)

    @pl.when(i + 1 < pl.num_programs(0))
    def _():
        issue(posn_ref, 1 - slot)

    pltpu.make_async_copy(y_hbm.at[pl.ds(0, tm), :], ybuf.at[slot], sem.at[slot]).wait()
    h = h_ref[...] + ybuf[slot]
    o_ref[...] = h * lax.rsqrt(jnp.mean(h * h, axis=-1, keepdims=True) + EPS) * gf_ref[...]


def _final(pos3, y_sorted, h, gf):
    n_tiles, _, tm = pos3.shape
    last = n_tiles - 1
    row = lambda i: (i, 0)
    return pl.pallas_call(
        _final_kernel,
        grid=(n_tiles,),
        in_specs=[pl.BlockSpec((1, 1, tm), lambda i: (i, 0, 0), memory_space=pltpu.SMEM),
                  pl.BlockSpec((1, 1, tm), lambda i: (jnp.minimum(i + 1, last), 0, 0), memory_space=pltpu.SMEM),
                  pl.BlockSpec(memory_space=pl.ANY),
                  pl.BlockSpec((tm, D_MODEL), row),
                  pl.BlockSpec((1, D_MODEL), lambda i: (0, 0))],
        out_specs=pl.BlockSpec((tm, D_MODEL), row),
        out_shape=jax.ShapeDtypeStruct((n_tiles * tm, D_MODEL), F32),
        scratch_shapes=[pltpu.VMEM((2, tm, D_MODEL), F32), pltpu.SemaphoreType.DMA((2,))],
        compiler_params=_cparams(("arbitrary",)),
        name="final",
    )(pos3, pos3, y_sorted, h, gf)


def _qa_perm():
    new = np.arange(NSA_HEADS * NSA_HD)
    r, g, d = new // LANE, (new % LANE) // NSA_HD, new % NSA_HD
    return (g * NSA_R + r) * NSA_HD + d


def _block_diag2(w):
    z = jnp.zeros_like(w)
    return jnp.concatenate([jnp.concatenate([w, z], axis=-1), jnp.concatenate([z, w], axis=-1)], axis=-2)


def kernel(x, rel_bias, ln_mix, w_in, cmp_pos_k, cmp_pos_v, cmp_k_w1, cmp_k_w2, cmp_v_w1, cmp_v_w2,
           diff_lq1, diff_lk1, diff_lq2, diff_lk2, diff_subln, w_out, ln_ffn,
           router_group_w, router_group_b, router_expert_w, router_expert_b,
           exp_w_gate, exp_w_up, exp_w_down, ln_final):
    B, S, D = x.shape
    T = B * S
    assert D == D_MODEL and S % TQ == 0 and S >= WINDOW and T % TM_MOE == 0
    x2 = x.reshape(T, D)
    perm = _qa_perm()

    w = w_in[0]
    c_kc, c_vc, c_ks, c_vs, c_kw, c_vw, c_gt = 512, 640, 768, 896, 1024, 1152, 1280
    c_qb = c_gt + N_GATE
    c_kb, c_vb = c_qb + 512, c_qb + 1024
    col = lambda c, n=LANE: w[:, c:c + n]
    w_tok = jnp.concatenate([w[:, perm], col(c_kc), col(c_vc), col(c_ks), col(c_kw), col(c_qb, 512), col(c_kb, 512)],
                            axis=1).astype(BF16)
    w_feat = jnp.concatenate([col(c_vs), col(c_vw), col(c_vb, 512), col(c_gt, N_GATE),
                              jnp.zeros((D, GATE_ROWS - N_GATE), F32)], axis=1).T.astype(BF16)
    qa, kc, vc, ks, kw, qb, kb, vst, vwt, vbt, gates_t = _in_proj(x2, ln_mix[0][None, :], w_tok, w_feat)

    w1k = _block_diag2(cmp_k_w1[0].reshape(CMP_LEN, NSA_HD, CMP_HIDDEN)).astype(BF16)
    w1v = _block_diag2(cmp_v_w1[0].reshape(CMP_LEN, NSA_HD, CMP_HIDDEN)).astype(BF16)
    w2k = _block_diag2(cmp_k_w2[0]).astype(BF16)
    w2v = _block_diag2(cmp_v_w2[0]).astype(BF16)
    posk = jnp.tile(cmp_pos_k[0], (1, NSA_G))
    posv = jnp.tile(cmp_pos_v[0], (1, NSA_G))
    kcmp, vcmpt = _compress(kc, vc, posk, posv, w1k, w1v, w2k, w2v, B, S)

    near, bias_c = _bias_tiles(rel_bias.T, S)

    n_sel = S // SEL_BLOCK
    nrow = S // CMP_STRIDE
    c_start = np.arange(nrow) * CMP_STRIDE
    s_start = np.arange(n_sel) * SEL_BLOCK
    ov = ((c_start[None, :] <= s_start[:, None] + SEL_BLOCK - 1)
          & (c_start[None, :] + CMP_LEN - 1 >= s_start[:, None])
          & (np.arange(nrow)[None, :] < nrow - 1)).astype(np.float32)
    o_a = _nsa(qa, gates_t, kcmp, vcmpt, bias_c, ks, vst, kw, vwt, near[:NSA_HEADS], jnp.asarray(ov, BF16), B, S)

    sub = jnp.tile(diff_subln[0], 2)[:, None]
    o_b = _diff(diff_lq1[0][None, :], diff_lk1[0][None, :], diff_lq2[0][None, :], diff_lk2[0][None, :],
                sub, qb, kb, vbt, near[NSA_HEADS:], B, S)

    w_o = jnp.concatenate([w_out[0][:512][perm], w_out[0][512:]], axis=0).astype(BF16)
    n_r = MOE_GROUPS + N_EXPERTS
    wr = jnp.concatenate([router_group_w[0], router_expert_w[0], jnp.zeros((D, LANE - n_r), F32)], axis=1).astype(BF16)
    br = jnp.concatenate([router_group_b[0], router_expert_b[0], jnp.zeros((LANE - n_r,), F32)])[None, :]
    h1, xt, route, counts = _out_proj(x2, o_a, o_b, w_o, ln_ffn[0][None, :], wr, br)

    tm = TM_MOE
    n_tiles = T // tm + MOE_GROUPS
    cnt = counts[0, :MOE_GROUPS].astype(jnp.int32)
    ends = jnp.cumsum((cnt + tm - 1) // tm * tm)
    starts = ends - (cnt + tm - 1) // tm * tm
    pos = starts[route[:, 0].astype(jnp.int32)] + route[:, 1].astype(jnp.int32)
    src = jnp.zeros((n_tiles * tm,), jnp.int32).at[pos].set(jnp.arange(T, dtype=jnp.int32))
    tile_start = jnp.arange(n_tiles, dtype=jnp.int32) * tm
    tile_group = jnp.minimum(jnp.searchsorted(ends, tile_start, side="right"), MOE_GROUPS - 1).astype(jnp.int32)
    tile_ok = (tile_start < ends[-1]).astype(jnp.int32)

    by_group = lambda a: a[0].astype(BF16).reshape((MOE_GROUPS, EPG) + a.shape[2:])
    y_sorted = _moe(tile_group, tile_ok, src.reshape(n_tiles, 1, tm), xt,
                    by_group(exp_w_gate), by_group(exp_w_up), by_group(exp_w_down))
    out = _final(pos.reshape(T // tm, 1, tm), y_sorted, h1, ln_final[None, :])
    return out.reshape(B, S, D)
```

```python
import math

import numpy as np
import jax
import jax.numpy as jnp
from jax import lax
from jax.experimental import pallas as pl
from jax.experimental.pallas import tpu as pltpu

F32 = jnp.float32
BF16 = jnp.bfloat16
NEG = -1e30
EPS = 1e-6
LOG2E = math.log2(math.e)

D_MODEL = 1024
LANE = 128
NSA_HEADS, NSA_G, NSA_R, NSA_HD = 8, 2, 4, 64
CMP_LEN, CMP_STRIDE, CMP_HIDDEN = 32, 16, 128
SEL_BLOCK, SEL_TOPN, SEL_FORCED_LOCAL, WINDOW = 64, 8, 2, 512
DIFF_HEADS, DIFF_HD = 8, 32
REL_BUCKETS, REL_MAX_EXACT, REL_MAX_DIST = 32, 16, 128
N_REL_HEADS = NSA_HEADS + DIFF_HEADS
MOE_GROUPS, EPG, N_EXPERTS, EXPERT_FF = 4, 8, 32, 256
LAMBDA_INIT = 0.8 - 0.6 * math.exp(-0.3 * 0)
N_GATE = NSA_HEADS * 3
GATE_ROWS = 32

TQ = 256
TM_PROJ = 512
TM_MOE = 256
XT_WIDTH = D_MODEL + LANE
VMEM_LIMIT = 48 * 1024 * 1024
VMEM_LIMIT_MOE = 56 * 1024 * 1024


def _cparams(sem):
    return pltpu.CompilerParams(dimension_semantics=sem, vmem_limit_bytes=VMEM_LIMIT)


def _dot(a, b):
    return jnp.dot(a, b, preferred_element_type=F32)


def _dot_nt(a, b):
    return lax.dot_general(a, b, (((1,), (1,)), ((), ())), preferred_element_type=F32)


def _bucket_thresholds():
    n = np.arange(0, REL_MAX_DIST + 1)
    nf = np.maximum(n, 1).astype(np.float32)
    large = REL_MAX_EXACT + (np.log(nf / np.float32(REL_MAX_EXACT)) / np.float32(math.log(REL_MAX_DIST / REL_MAX_EXACT))
                             * np.float32(REL_BUCKETS - REL_MAX_EXACT)).astype(np.int32)
    large = np.minimum(large, REL_BUCKETS - 1)
    bucket = np.where(n < REL_MAX_EXACT, n, large)
    return [int(np.argmax(bucket >= b)) for b in range(REL_BUCKETS)]


_THR = _bucket_thresholds()


def _inproj_kernel(x_ref, g_ref, w_ref, wt_ref, qa_ref, kc_ref, vc_ref, ks_ref, kw_ref, qb_ref, kb_ref,
                   vst_ref, vwt_ref, vbt_ref, gt_ref):
    x = x_ref[...]
    xn = (x * lax.rsqrt(jnp.mean(x * x, axis=-1, keepdims=True) + EPS) * g_ref[...]).astype(BF16)
    a = _dot(xn, w_ref[:, 0:512]) * (NSA_HD ** -0.5 * LOG2E)
    for r in range(4):
        qa_ref[r] = a[:, r * LANE:(r + 1) * LANE].astype(BF16)
    a = _dot(xn, w_ref[:, 512:1024])
    kc_ref[...] = a[:, 0:128]
    vc_ref[...] = a[:, 128:256]
    ks_ref[...] = a[:, 256:384].astype(BF16)
    kw_ref[...] = a[:, 384:512].astype(BF16)
    a = _dot(xn, w_ref[:, 1024:1536]) * (DIFF_HD ** -0.5 * LOG2E)
    for r in range(4):
        qb_ref[r] = a[:, r * LANE:(r + 1) * LANE].astype(BF16)
    a = _dot(xn, w_ref[:, 1536:2048])
    for r in range(4):
        kb_ref[r] = a[:, r * LANE:(r + 1) * LANE].astype(BF16)
    vst_ref[...] = _dot_nt(wt_ref[0:128, :], xn).astype(BF16)
    vwt_ref[...] = _dot_nt(wt_ref[128:256, :], xn).astype(BF16)
    for r in range(4):
        vbt_ref[r] = _dot_nt(wt_ref[256 + r * LANE:384 + r * LANE, :], xn).astype(BF16)
    gt_ref[...] = _dot_nt(wt_ref[768:768 + GATE_ROWS, :], xn)


def _in_proj(x2, g, w, wt):
    T = x2.shape[0]
    tm = TM_PROJ
    row = lambda i: (i, 0)
    o128b = jax.ShapeDtypeStruct((T, LANE), BF16)
    o128f = jax.ShapeDtypeStruct((T, LANE), F32)
    o4 = jax.ShapeDtypeStruct((4, T, LANE), BF16)
    ot = jax.ShapeDtypeStruct((LANE, T), BF16)
    o4t = jax.ShapeDtypeStruct((4, LANE, T), BF16)
    s128 = pl.BlockSpec((tm, LANE), row)
    s4 = pl.BlockSpec((4, tm, LANE), lambda i: (0, i, 0))
    st = pl.BlockSpec((LANE, tm), lambda i: (0, i))
    s4t = pl.BlockSpec((4, LANE, tm), lambda i: (0, 0, i))
    return pl.pallas_call(
        _inproj_kernel,
        grid=(T // tm,),
        in_specs=[pl.BlockSpec((tm, D_MODEL), row),
                  pl.BlockSpec((1, D_MODEL), lambda i: (0, 0)),
                  pl.BlockSpec(w.shape, lambda i: (0, 0)),
                  pl.BlockSpec(wt.shape, lambda i: (0, 0))],
        out_specs=[s4, s128, s128, s128, s128, s4, s4, st, st, s4t, pl.BlockSpec((GATE_ROWS, tm), lambda i: (0, i))],
        out_shape=[o4, o128f, o128f, o128b, o128b, o4, o4, ot, ot, o4t, jax.ShapeDtypeStruct((GATE_ROWS, T), F32)],
        compiler_params=_cparams(("parallel",)),
        name="in_proj",
    )(x2, g, w, wt)


def _gelu_tanh(x):
    return 0.5 * x * (1.0 + jnp.tanh(math.sqrt(2.0 / math.pi) * (x + 0.044715 * (x * x * x))))


def _compress_kernel(kc_ref, vc_ref, posk_ref, posv_ref, w1k_ref, w1v_ref, w2k_ref, w2v_ref, ko_ref, vo_ref):
    nrow = kc_ref.shape[0] // CMP_STRIDE
    rid = lax.broadcasted_iota(jnp.int32, (nrow, 1), 0)
    cid = lax.broadcasted_iota(jnp.int32, (1, nrow), 1)
    for src, pos, w1, w2, out, transposed in ((kc_ref, posk_ref, w1k_ref, w2k_ref, ko_ref, False),
                                              (vc_ref, posv_ref, w1v_ref, w2v_ref, vo_ref, True)):
        hid_a = jnp.zeros((nrow, 2 * CMP_HIDDEN), F32)
        hid_b = jnp.zeros((nrow, 2 * CMP_HIDDEN), F32)
        for m in range(CMP_STRIDE):
            y = src[pl.ds(m, nrow, stride=CMP_STRIDE), :]
            hid_a = hid_a + _dot((y + pos[m:m + 1, :]).astype(BF16), w1[m])
            hid_b = hid_b + _dot((y + pos[CMP_STRIDE + m:CMP_STRIDE + m + 1, :]).astype(BF16), w1[CMP_STRIDE + m])
        hid = hid_a + pltpu.roll(hid_b, nrow - 1, 0)
        o = _dot(_gelu_tanh(hid).astype(BF16), w2[...])
        if transposed:
            out[0] = jnp.where(cid < nrow - 1, o.T, 0.0).astype(BF16)
        else:
            out[0] = jnp.where(rid < nrow - 1, o, 0.0).astype(BF16)


def _compress(kc, vc, posk, posv, w1k, w1v, w2k, w2v, B, S):
    nrow = S // CMP_STRIDE
    assert nrow == LANE
    full = lambda a: pl.BlockSpec(a.shape, lambda b: (0,) * a.ndim)
    src = pl.BlockSpec((S, LANE), lambda b: (b, 0))
    osp = pl.BlockSpec((1, nrow, LANE), lambda b: (b, 0, 0))
    osh = jax.ShapeDtypeStruct((B, nrow, LANE), BF16)
    return pl.pallas_call(
        _compress_kernel,
        grid=(B,),
        in_specs=[src, src, full(posk), full(posv), full(w1k), full(w1v), full(w2k), full(w2v)],
        out_specs=[osp, osp],
        out_shape=[osh, osh],
        compiler_params=_cparams(("parallel",)),
        name="compress",
    )(kc, vc, posk, posv, w1k, w1v, w2k, w2v)


def _bias_from_dist(dist, tab_ref, h):
    val = jnp.full(dist.shape, tab_ref[h, 0], F32)
    for b in range(1, REL_BUCKETS):
        val = jnp.where(dist >= _THR[b], tab_ref[h, b], val)
    return val


def _bias_near_kernel(tab_ref, out_ref):
    h = pl.program_id(0)
    j = lax.broadcasted_iota(jnp.int32, (TQ, TQ), 0)
    i = lax.broadcasted_iota(jnp.int32, (TQ, TQ), 1)
    for d in range(2):
        bias = (_bias_from_dist(i - j + d * TQ, tab_ref, h) - tab_ref[h, REL_BUCKETS - 1]) * LOG2E
        out_ref[0, d] = jnp.where(i - j + d * TQ >= 0, bias, NEG)


def _bias_cmp_kernel(tab_ref, out_ref):
    h = pl.program_id(0)
    c = lax.broadcasted_iota(jnp.int32, (LANE, TQ), 0)
    t = pl.program_id(1) * TQ + lax.broadcasted_iota(jnp.int32, (LANE, TQ), 1)
    out_ref[0] = _bias_from_dist(t - (c * CMP_STRIDE + CMP_LEN - 1), tab_ref, h) * LOG2E


def _bias_tiles(tab_t, S):
    smem = pl.BlockSpec(memory_space=pltpu.SMEM)
    near = pl.pallas_call(
        _bias_near_kernel,
        grid=(N_REL_HEADS,),
        in_specs=[smem],
        out_specs=pl.BlockSpec((1, 2, TQ, TQ), lambda h: (h, 0, 0, 0)),
        out_shape=jax.ShapeDtypeStruct((N_REL_HEADS, 2, TQ, TQ), F32),
        compiler_params=_cparams(("parallel",)),
        name="bias_near",
    )(tab_t)
    cmp_bias = pl.pallas_call(
        _bias_cmp_kernel,
        grid=(NSA_HEADS, S // TQ),
        in_specs=[smem],
        out_specs=pl.BlockSpec((1, LANE, TQ), lambda h, q: (h, 0, q)),
        out_shape=jax.ShapeDtypeStruct((NSA_HEADS, LANE, S), F32),
        compiler_params=_cparams(("parallel", "parallel")),
        name="bias_cmp",
    )(tab_t)
    return near, cmp_bias


def _flash_first(ss, vts):
    ms = [jnp.max(s, axis=0, keepdims=True) for s in ss]
    ps = [jnp.exp2(s - m) for s, m in zip(ss, ms)]
    ls = [jnp.sum(p, axis=0, keepdims=True) for p in ps]
    return tuple((m, l, _dot(vt, p.astype(BF16))) for m, l, p, vt in zip(ms, ls, ps, vts))


def _flash_update(ss, vts, sts):
    ms = [jnp.maximum(st[0], jnp.max(s, axis=0, keepdims=True)) for s, st in zip(ss, sts)]
    alphas = [jnp.exp2(st[0] - m) for m, st in zip(ms, sts)]
    ps = [jnp.exp2(s - m) for s, m in zip(ss, ms)]
    ls = [a * st[1] + jnp.sum(p, axis=0, keepdims=True) for a, st, p in zip(alphas, sts, ps)]
    return tuple((m, l, a * st[2] + _dot(vt, p.astype(BF16)))
                 for m, l, a, st, p, vt in zip(ms, ls, alphas, sts, ps, vts))


def _flash_out(st):
    return st[2] * (1.0 / st[1])


def _stream_scratch(n):
    return ([pltpu.VMEM((8, TQ), F32), pltpu.VMEM((LANE // 2, TQ), F32)] * n
            + [pltpu.VMEM((TQ, TQ), F32)] * (2 * n) + [pltpu.VMEM((TQ, TQ), BF16)] * (2 * n))


def _stream_refs(refs, n):
    st_refs = tuple((refs[2 * c], refs[2 * c + 1]) for c in range(n))
    s, p = refs[2 * n:4 * n], refs[4 * n:6 * n]
    return st_refs, tuple(s[:n]), tuple(s[n:]), tuple(p[:n]), tuple(p[n:])


def _flash_load(refs):
    return tuple((st[0:1, :], st[1:2, :], acc[...]) for st, acc in refs)


def _flash_store(refs, sts):
    for (st, acc), (m, l, a) in zip(refs, sts):
        st[0:1, :] = m
        st[1:2, :] = l
        acc[...] = a


def _causal_stream(qi, scores, vtile, st_refs, s_a, s_b, p_a, p_b):
    n_far = jnp.maximum(qi - 1, 0)
    top = qi - 2

    def write(s_refs, kt, near=2):
        for ref, s in zip(s_refs, scores(jnp.maximum(kt, 0), near)):
            ref[...] = s

    def pending(p_refs, kt):
        return [_dot(vt, ref[...]) for vt, ref in zip(vtile(jnp.maximum(kt, 0)), p_refs)]

    def step(s_cur, p_cur, s_next, kt_next, p_prev, kt_prev, near_next=2):
        write(s_next, kt_next, near_next)
        pend = pending(p_prev, kt_prev)
        for (st, acc), s_ref, p_ref, pv in zip(st_refs, s_cur, p_cur, pend):
            s = s_ref[...]
            m_old = st[0:1, :]
            m = jnp.maximum(m_old, jnp.max(s, axis=0, keepdims=True))
            alpha = jnp.exp2(m_old - m)
            p = jnp.exp2(s - m)
            p_ref[...] = p.astype(BF16)
            st[0:1, :] = m
            st[1:2, :] = alpha * st[1:2, :] + jnp.sum(p, axis=0, keepdims=True)
            acc[...] = alpha * (acc[...] + pv)

    write(s_a, qi, 0)
    write(s_b, qi - 1, 1)
    for (st, acc), s_ref, p_ref in zip(st_refs, s_a, p_a):
        s = s_ref[...]
        m = jnp.max(s, axis=0, keepdims=True)
        p = jnp.exp2(s - m)
        p_ref[...] = p.astype(BF16)
        st[0:1, :] = m
        st[1:2, :] = jnp.sum(p, axis=0, keepdims=True)
        acc[...] = jnp.zeros(acc.shape, F32)
    step(s_b, p_b, s_a, top, p_a, qi)

    def pair(j, c):
        kt = top - 2 * j
        step(s_a, p_a, s_b, kt - 1, p_b, kt + 1)
        step(s_b, p_b, s_a, kt - 2, p_a, kt)
        return c
    lax.fori_loop(0, n_far // 2, pair, 0)

    @pl.when(n_far % 2 == 1)
    def _():
        for (st, acc), s_ref, pv, vt in zip(st_refs, s_a, pending(p_b, 1), vtile(0)):
            s = s_ref[...]
            m_old = st[0:1, :]
            m = jnp.maximum(m_old, jnp.max(s, axis=0, keepdims=True))
            alpha = jnp.exp2(m_old - m)
            p = jnp.exp2(s - m)
            st[0:1, :] = m
            st[1:2, :] = alpha * st[1:2, :] + jnp.sum(p, axis=0, keepdims=True)
            acc[...] = alpha * (acc[...] + pv) + _dot(vt, p.astype(BF16))

    @pl.when(n_far % 2 == 0)
    def _():
        last = jnp.where(n_far == 0, qi - 1, 0)
        for (st, acc), pv in zip(st_refs, pending(p_b, last)):
            acc[...] = acc[...] + pv
    return _flash_load(st_refs)


def _ktile(ref, kt):
    return ref[pl.ds(pl.multiple_of(kt * TQ, TQ), TQ), :]


def _vtile(ref, kt):
    return ref[:, pl.ds(pl.multiple_of(kt * TQ, TQ), TQ)]


def _nsa_kernel(qa_ref, gt_ref, kcmp_ref, vcmpt_ref, bc_ref, ks_ref, vst_ref, kw_ref, vwt_ref, dn_ref,
                ov_ref, o_ref, psum_ref, sel_ref, oacc_ref, sig_ref, *scratch):
    st_refs, s_a, s_b, p_a, p_b = _stream_refs(scratch, 2 * NSA_G)
    qi = pl.program_id(1)
    t0 = qi * TQ
    lane = lax.broadcasted_iota(jnp.int32, (1, LANE), 1)
    lane_grp = lax.shift_right_arithmetic(lane, 6)
    sub_grp = lax.shift_right_arithmetic(lax.broadcasted_iota(jnp.int32, (LANE, 1), 0), 6)
    sig_ref[...] = jax.nn.sigmoid(gt_ref[...])

    def gate_row(c):
        return sig_ref[pl.ds(c, 1), :]

    def masked_q(r, g):
        return jnp.where(lane_grp == g, qa_ref[r].astype(F32), 0.0).astype(BF16)

    n_cmp = kcmp_ref.shape[1] - 1
    crow = lax.broadcasted_iota(jnp.int32, (LANE, 1), 0)
    cmp_end = jnp.where(crow < n_cmp, crow * CMP_STRIDE + CMP_LEN - 1, 1 << 30)
    mask_c = (t0 + lax.broadcasted_iota(jnp.int32, (1, TQ), 1)) >= cmp_end

    heads = [(r, g) for r in range(NSA_R) for g in range(NSA_G)]
    scores_c = [_dot_nt(kcmp_ref[0], masked_q(r, g)) for r, g in heads]
    probs_c = []
    for (r, g), s in zip(heads, scores_c):
        s = jnp.where(mask_c, s + bc_ref[g * NSA_R + r], NEG)
        p = jnp.where(mask_c, jnp.exp2(s - jnp.max(s, axis=0, keepdims=True)), 0.0)
        l = jnp.sum(p, axis=0, keepdims=True)
        probs_c.append(p * jnp.where(l > 0.0, 1.0 / l, 0.0))
    for g in range(NSA_G):
        psum_ref[g] = sum(p for (r, gg), p in zip(heads, probs_c) if gg == g)
    outs_c = [_dot(vcmpt_ref[0], p.astype(BF16)) for p in probs_c]
    for r in range(NSA_R):
        o0, o1 = outs_c[r * NSA_G], outs_c[r * NSA_G + 1]
        oacc_ref[r] = jnp.where(sub_grp == 0, gate_row(r * 3) * o0, gate_row((NSA_R + r) * 3) * o1)

    n_sel = ov_ref.shape[0]
    jj = lax.broadcasted_iota(jnp.int32, (n_sel, TQ), 0)
    cur = lax.shift_right_arithmetic(t0 + lax.broadcasted_iota(jnp.int32, (n_sel, TQ), 1), 6)
    valid = jj <= cur
    forced = (jj == 0) | (cur - jj < SEL_FORCED_LOCAL)
    for g in range(NSA_G):
        ps = psum_ref[g]
        hi = ps.astype(BF16)
        rem = ps - hi.astype(F32)
        mid = rem.astype(BF16)
        lo = (rem - mid.astype(F32)).astype(BF16)
        ov = ov_ref[...]
        imp = _dot(ov, hi) + _dot(ov, mid) + _dot(ov, lo)
        score = jnp.where(valid, jnp.where(forced, 1e9, imp), -1e9)
        cnt = jnp.zeros((n_sel, TQ), F32)
        for j2 in range(n_sel):
            row = score[j2:j2 + 1, :]
            tie = jnp.where(jj > j2, 1.0, 0.0)
            cnt = cnt + jnp.where(row > score, 1.0, jnp.where(row == score, tie, 0.0))
        sel_ref[g] = jnp.where(cnt < float(min(SEL_TOPN, n_sel)), jnp.where(score > -1e8, 0.0, NEG), NEG)

    has_prev = jnp.where(qi >= 1, 0.0, NEG)
    has_wfar = jnp.where(qi >= WINDOW // TQ, 0.0, NEG)
    kt_prev = jnp.maximum(qi - 1, 0)
    kt_wfar = jnp.maximum(qi - WINDOW // TQ, 0)
    wfar_mask = lax.broadcasted_iota(jnp.int32, (TQ, TQ), 0) > lax.broadcasted_iota(jnp.int32, (TQ, TQ), 1)

    def main_body(rp, carry):
        chains = [(2 * rp + i, g) for i in range(2) for g in range(NSA_G)]
        qms = [masked_q(r, g) for r, g in chains]
        hids = [g * NSA_R + r for r, g in chains]

        def block_mask(g, kt):
            per_tile = TQ // SEL_BLOCK
            rows = [jnp.broadcast_to(sel_ref[g, pl.ds(kt * per_tile + i, 1), :], (SEL_BLOCK, TQ))
                    for i in range(per_tile)]
            return jnp.concatenate(rows, axis=0)

        def near_bias(ss, near):
            if near == 0:
                return [s + dn_ref[hid, 0] for s, hid in zip(ss, hids)]
            if near == 1:
                return [s + dn_ref[hid, 1] + has_prev for s, hid in zip(ss, hids)]
            return ss

        def group_values(ref, kt):
            vt = _vtile(ref, kt)
            halves = [vt[g * NSA_HD:(g + 1) * NSA_HD, :] for g in range(NSA_G)]
            return [halves[g] for r, g in chains]

        def finish(sts, branch):
            for c, (r, g) in enumerate(chains):
                oacc_ref[r, g * NSA_HD:(g + 1) * NSA_HD, :] += gate_row(hids[c] * 3 + branch) * _flash_out(sts[c])

        def slc_scores(kt, near=2):
            k = _ktile(ks_ref, kt)
            ss = near_bias([_dot_nt(k, qm) for qm in qms], near)
            masks = [block_mask(g, kt) for g in range(NSA_G)]
            return [s + masks[g] for s, (r, g) in zip(ss, chains)]

        def slc_vtile(kt):
            return group_values(vst_ref, kt)

        finish(_causal_stream(qi, slc_scores, slc_vtile, st_refs, s_a, s_b, p_a, p_b), 1)

        win = []
        for kt, near in ((qi, 0), (kt_prev, 1), (kt_wfar, 2)):
            k = _ktile(kw_ref, kt)
            ss = near_bias([_dot_nt(k, qm) for qm in qms], near)
            if near == 2:
                ss = [jnp.where(wfar_mask, s + has_wfar, NEG) for s in ss]
            win.append(ss)
        sts = _flash_first(win[0], group_values(vwt_ref, qi))
        sts = _flash_update(win[1], group_values(vwt_ref, kt_prev), sts)
        sts = _flash_update(win[2], group_values(vwt_ref, kt_wfar), sts)
        finish(sts, 2)
        return carry
    lax.fori_loop(0, NSA_R // 2, main_body, 0)

    for r in range(NSA_R):
        o_ref[r] = oacc_ref[r].T.astype(BF16)


def _nsa(qa, gates_t, kcmp, vcmpt, bias_c, ks, vst, kw, vwt, near_a, ov, B, S):
    nq = S // TQ
    T = B * S
    k_spec = pl.BlockSpec((S, LANE), lambda b, q: (b, 0))
    v_spec = pl.BlockSpec((LANE, S), lambda b, q: (0, b))
    full = lambda a: pl.BlockSpec(a.shape, lambda b, q: (0,) * a.ndim)
    cmp_spec = pl.BlockSpec((1, LANE, LANE), lambda b, q: (b, 0, 0))
    qo = pl.BlockSpec((4, TQ, LANE), lambda b, q: (0, b * nq + q, 0))
    return pl.pallas_call(
        _nsa_kernel,
        grid=(B, nq),
        in_specs=[qo, pl.BlockSpec((GATE_ROWS, TQ), lambda b, q: (0, b * nq + q)),
                  cmp_spec, cmp_spec,
                  pl.BlockSpec((NSA_HEADS, LANE, TQ), lambda b, q: (0, 0, q)),
                  k_spec, v_spec, k_spec, v_spec, full(near_a), full(ov)],
        out_specs=qo,
        out_shape=jax.ShapeDtypeStruct((4, T, LANE), BF16),
        scratch_shapes=[pltpu.VMEM((NSA_G, LANE, TQ), F32), pltpu.VMEM((NSA_G, S // SEL_BLOCK, TQ), F32),
                        pltpu.VMEM((NSA_R, LANE, TQ), F32), pltpu.VMEM((GATE_ROWS, TQ), F32)] + _stream_scratch(2 * NSA_G),
        compiler_params=_cparams(("parallel", "arbitrary")),
        name="nsa",
    )(qa, gates_t, kcmp, vcmpt, bias_c, ks, vst, kw, vwt, near_a, ov)


def _diff_kernel(lq1_ref, lk1_ref, lq2_ref, lk2_ref, sub_ref, qb_ref, kb_ref, vbt_ref, dn_ref, o_ref, *scratch):
    n = 2 * 2
    st_refs, s_a, s_b, p_a, p_b = _stream_refs(scratch, n)
    qi = pl.program_id(1)
    lane = lax.broadcasted_iota(jnp.int32, (1, LANE), 1)
    chain_of_lane = lax.shift_right_arithmetic(lane, 5)
    lam = (jnp.exp(jnp.sum(lq1_ref[...] * lk1_ref[...], axis=-1, keepdims=True))
           - jnp.exp(jnp.sum(lq2_ref[...] * lk2_ref[...], axis=-1, keepdims=True)) + LAMBDA_INIT)
    has_prev = jnp.where(qi >= 1, 0.0, NEG)

    def pair_body(pr, carry):
        q = qb_ref[pr].astype(F32)
        qms = [jnp.where(chain_of_lane == c, q, 0.0).astype(BF16) for c in range(n)]

        def scores(kt, near):
            k = kb_ref[pr, pl.ds(pl.multiple_of(kt * TQ, TQ), TQ), :]
            ss = [_dot_nt(k, qm) for qm in qms]
            if near == 2:
                return ss
            bias = [dn_ref[2 * pr + hh, near] for hh in range(2)]
            if near == 1:
                bias = [b + has_prev for b in bias]
            return [s + bias[c // 2] for c, s in enumerate(ss)]

        def vtile(kt):
            vt = vbt_ref[pr, :, pl.ds(pl.multiple_of(kt * TQ, TQ), TQ)]
            halves = [vt[hh * 2 * DIFF_HD:(hh + 1) * 2 * DIFF_HD, :] for hh in range(2)]
            return [halves[c // 2] for c in range(n)]

        sts = _causal_stream(qi, scores, vtile, st_refs, s_a, s_b, p_a, p_b)
        outs = []
        for hh in range(2):
            o = _flash_out(sts[2 * hh]) - lam * _flash_out(sts[2 * hh + 1])
            outs.append(o * lax.rsqrt(jnp.mean(o * o, axis=0, keepdims=True) + EPS))
        out = jnp.concatenate(outs, axis=0) * sub_ref[...] * (1.0 - LAMBDA_INIT)
        o_ref[pr] = out.T.astype(BF16)
        return carry
    lax.fori_loop(0, DIFF_HEADS // 2, pair_body, 0)


def _diff(lq1, lk1, lq2, lk2, sub, qb, kb, vbt, near_b, B, S):
    nq = S // TQ
    T = B * S
    full = lambda a: pl.BlockSpec(a.shape, lambda b, q: (0,) * a.ndim)
    k_spec = pl.BlockSpec((4, S, LANE), lambda b, q: (0, b, 0))
    v_spec = pl.BlockSpec((4, LANE, S), lambda b, q: (0, 0, b))
    qo = pl.BlockSpec((4, TQ, LANE), lambda b, q: (0, b * nq + q, 0))
    return pl.pallas_call(
        _diff_kernel,
        grid=(B, nq),
        in_specs=[full(lq1), full(lk1), full(lq2), full(lk2), full(sub), qo, k_spec, v_spec, full(near_b)],
        out_specs=qo,
        out_shape=jax.ShapeDtypeStruct((4, T, LANE), BF16),
        scratch_shapes=_stream_scratch(4),
        compiler_params=_cparams(("parallel", "arbitrary")),
        name="diff",
    )(lq1, lk1, lq2, lk2, sub, qb, kb, vbt, near_b)


def _outproj_kernel(x_ref, oa_ref, ob_ref, w_ref, g_ref, wr_ref, br_ref, h_ref, xt_ref, route_ref, cnt_ref,
                    carry_ref):
    @pl.when(pl.program_id(0) == 0)
    def _():
        carry_ref[...] = jnp.zeros(carry_ref.shape, F32)

    o = jnp.concatenate([oa_ref[r] for r in range(4)] + [ob_ref[r] for r in range(4)], axis=-1)
    h = x_ref[...] + _dot(o, w_ref[...])
    h_ref[...] = h
    tn32 = h * lax.rsqrt(jnp.mean(h * h, axis=-1, keepdims=True) + EPS) * g_ref[...]
    xt_ref[:, 0:D_MODEL] = tn32
    tn = tn32.astype(BF16)
    logits = _dot(tn, wr_ref[...]) + br_ref[...]
    lane = lax.broadcasted_iota(jnp.int32, (1, LANE), 1)
    lane_f = lane.astype(F32)
    is_grp = lane < MOE_GROUPS
    lg = jnp.where(is_grp, logits, NEG)
    mg = jnp.max(lg, axis=-1, keepdims=True)
    zg = jnp.sum(jnp.where(is_grp, jnp.exp(lg - mg), 0.0), axis=-1, keepdims=True)
    g_prob = 1.0 / zg
    g_idx = jnp.min(jnp.where(lg == mg, lane_f, 1e9), axis=-1, keepdims=True)
    lane_grp = jnp.where((lane >= MOE_GROUPS) & (lane < MOE_GROUPS + N_EXPERTS),
                         lax.shift_right_arithmetic(lane - MOE_GROUPS, 3), -1).astype(F32)
    le = jnp.where(lane_grp == g_idx, logits, NEG)
    m1 = jnp.max(le, axis=-1, keepdims=True)
    e1 = jnp.min(jnp.where(le == m1, lane_f, 1e9), axis=-1, keepdims=True)
    le2 = jnp.where(lane_f == e1, NEG, le)
    m2 = jnp.max(le2, axis=-1, keepdims=True)
    e2 = jnp.min(jnp.where(le2 == m2, lane_f, 1e9), axis=-1, keepdims=True)
    ratio = jnp.exp(m2 - m1)
    w1 = g_prob / (1.0 + ratio)
    w2 = w1 * ratio
    xt_ref[:, D_MODEL:D_MODEL + LANE] = jnp.where(lane_f == e1, w1, 0.0) + jnp.where(lane_f == e2, w2, 0.0)

    tm = h.shape[0]
    onehot = jnp.where(lane_f == g_idx, 1.0, 0.0)
    earlier = jnp.where(lax.broadcasted_iota(jnp.int32, (tm, tm), 0) > lax.broadcasted_iota(jnp.int32, (tm, tm), 1),
                        1.0, 0.0).astype(BF16)
    prefix = _dot(earlier, onehot.astype(BF16)) + carry_ref[...]
    rank = jnp.sum(onehot * prefix, axis=-1, keepdims=True)
    carry_ref[...] += jnp.sum(onehot, axis=0, keepdims=True)
    route_ref[...] = jnp.where(lane == 0, g_idx, jnp.where(lane == 1, rank, 0.0))
    cnt_ref[...] = jnp.broadcast_to(carry_ref[...], cnt_ref.shape)


def _out_proj(x2, oa, ob, w, g, wr, br):
    T = x2.shape[0]
    tm = TM_PROJ
    row = lambda i: (i, 0)
    full = lambda a: pl.BlockSpec(a.shape, lambda i: (0,) * a.ndim)
    o4 = pl.BlockSpec((4, tm, LANE), lambda i: (0, i, 0))
    return pl.pallas_call(
        _outproj_kernel,
        grid=(T // tm,),
        in_specs=[pl.BlockSpec((tm, D_MODEL), row), o4, o4, full(w), full(g), full(wr), full(br)],
        out_specs=[pl.BlockSpec((tm, D_MODEL), row), pl.BlockSpec((tm, XT_WIDTH), row), pl.BlockSpec((tm, LANE), row),
                   pl.BlockSpec((8, LANE), lambda i: (0, 0))],
        out_shape=[jax.ShapeDtypeStruct((T, D_MODEL), F32), jax.ShapeDtypeStruct((T, XT_WIDTH), F32),
                   jax.ShapeDtypeStruct((T, LANE), F32), jax.ShapeDtypeStruct((8, LANE), F32)],
        scratch_shapes=[pltpu.VMEM((1, LANE), F32)],
        compiler_params=_cparams(("arbitrary",)),
        name="out_proj",
    )(x2, oa, ob, w, g, wr, br)


def _moe_kernel(tg_ref, ok_ref, idx_ref, idxn_ref, xt_hbm, wg_ref, wu_ref, wd_ref, y_ref, xbuf, sem):
    i = pl.program_id(0)
    slot = i % 2
    tm = xbuf.shape[1]

    def row_copy(index_ref, r, dst_slot):
        return pltpu.make_async_copy(xt_hbm.at[pl.ds(index_ref[0, 0, r], 1), :],
                                     xbuf.at[dst_slot, pl.ds(r, 1), :], sem.at[dst_slot])

    @pl.when(i == 0)
    def _():
        def body(r, c):
            row_copy(idx_ref, r, 0).start()
            return c
        lax.fori_loop(0, tm, body, 0)

    @pl.when((i == 0) | (ok_ref[jnp.maximum(i - 1, 0)] == 1))
    def _():
        pltpu.make_async_copy(xt_hbm.at[pl.ds(0, tm), :], xbuf.at[slot], sem.at[slot]).wait()

    @pl.when(ok_ref[i] == 0)
    def _():
        y_ref[...] = jnp.zeros(y_ref.shape, F32)

    @pl.when(ok_ref[i] == 1)
    def _():
        lane = lax.broadcasted_iota(jnp.int32, (1, LANE), 1)
        x = xbuf[slot, :, 0:D_MODEL].astype(BF16)
        cmb = xbuf[slot, :, D_MODEL:D_MODEL + LANE]
        first_lane = MOE_GROUPS + tg_ref[i] * EPG
        per_expert = tm // EPG
        y = jnp.zeros((tm, D_MODEL), F32)
        for e in range(EPG):
            a = _dot(x, wg_ref[0, e])
            b = _dot(x, wu_ref[0, e])
            ce = jnp.sum(jnp.where(lane == first_lane + e, cmb, 0.0), axis=-1, keepdims=True)
            y = y + _dot(((a * jax.nn.sigmoid(a)) * b * ce).astype(BF16), wd_ref[0, e])
            for r in range(e * per_expert, (e + 1) * per_expert):
                row_copy(idxn_ref, r, 1 - slot).start()
        y_ref[...] = y


def _moe(tile_group, tile_ok, src3, xt, wg, wu, wd):
    n_tiles, _, tm = src3.shape
    last = n_tiles - 1
    grid_spec = pltpu.PrefetchScalarGridSpec(
        num_scalar_prefetch=2,
        grid=(n_tiles,),
        in_specs=[pl.BlockSpec((1, 1, tm), lambda i, tg, ok: (i, 0, 0), memory_space=pltpu.SMEM),
                  pl.BlockSpec((1, 1, tm), lambda i, tg, ok: (jnp.minimum(i + 1, last), 0, 0), memory_space=pltpu.SMEM),
                  pl.BlockSpec(memory_space=pl.ANY),
                  pl.BlockSpec((1, EPG, D_MODEL, EXPERT_FF), lambda i, tg, ok: (tg[i], 0, 0, 0)),
                  pl.BlockSpec((1, EPG, D_MODEL, EXPERT_FF), lambda i, tg, ok: (tg[i], 0, 0, 0)),
                  pl.BlockSpec((1, EPG, EXPERT_FF, D_MODEL), lambda i, tg, ok: (tg[i], 0, 0, 0))],
        out_specs=pl.BlockSpec((tm, D_MODEL), lambda i, tg, ok: (i, 0)),
        scratch_shapes=[pltpu.VMEM((2, tm, XT_WIDTH), F32), pltpu.SemaphoreType.DMA((2,))],
    )
    return pl.pallas_call(
        _moe_kernel,
        grid_spec=grid_spec,
        out_shape=jax.ShapeDtypeStruct((n_tiles * tm, D_MODEL), F32),
        compiler_params=pltpu.CompilerParams(dimension_semantics=("arbitrary",), vmem_limit_bytes=VMEM_LIMIT_MOE),
        name="moe",
    )(tile_group, tile_ok, src3, src3, xt, wg, wu, wd)


def _final_kernel(pos_ref, posn_ref, y_hbm, h_ref, gf_ref, o_ref, ybuf, sem):
    i = pl.program_id(0)
    slot = i % 2
    tm = ybuf.shape[1]

    def issue(index_ref, dst_slot):
        def body(r, c):
            pltpu.make_async_copy(y_hbm.at[pl.ds(index_ref[0, 0, r], 1), :],
                                  ybuf.at[dst_slot, pl.ds(r, 1), :], sem.at[dst_slot]).start()
            return c
        lax.fori_loop(0, tm, body, 0, unroll=8)

    @pl.when(i == 0)
    def _():
        issue(pos_ref, 0)

    @pl.when(i + 1 < pl.num_programs(0))
    def _():
        issue(posn_ref, 1 - slot)

    pltpu.make_async_copy(y_hbm.at[pl.ds(0, tm), :], ybuf.at[slot], sem.at[slot]).wait()
    h = h_ref[...] + ybuf[slot]
    o_ref[...] = h * lax.rsqrt(jnp.mean(h * h, axis=-1, keepdims=True) + EPS) * gf_ref[...]


def _final(pos3, y_sorted, h, gf):
    n_tiles, _, tm = pos3.shape
    last = n_tiles - 1
    row = lambda i: (i, 0)
    return pl.pallas_call(
        _final_kernel,
        grid=(n_tiles,),
        in_specs=[pl.BlockSpec((1, 1, tm), lambda i: (i, 0, 0), memory_space=pltpu.SMEM),
                  pl.BlockSpec((1, 1, tm), lambda i: (jnp.minimum(i + 1, last), 0, 0), memory_space=pltpu.SMEM),
                  pl.BlockSpec(memory_space=pl.ANY),
                  pl.BlockSpec((tm, D_MODEL), row),
                  pl.BlockSpec((1, D_MODEL), lambda i: (0, 0))],
        out_specs=pl.BlockSpec((tm, D_MODEL), row),
        out_shape=jax.ShapeDtypeStruct((n_tiles * tm, D_MODEL), F32),
        scratch_shapes=[pltpu.VMEM((2, tm, D_MODEL), F32), pltpu.SemaphoreType.DMA((2,))],
        compiler_params=_cparams(("arbitrary",)),
        name="final",
    )(pos3, pos3, y_sorted, h, gf)


def _qa_perm():
    new = np.arange(NSA_HEADS * NSA_HD)
    r, g, d = new // LANE, (new % LANE) // NSA_HD, new % NSA_HD
    return (g * NSA_R + r) * NSA_HD + d


def _block_diag2(w):
    z = jnp.zeros_like(w)
    return jnp.concatenate([jnp.concatenate([w, z], axis=-1), jnp.concatenate([z, w], axis=-1)], axis=-2)


def kernel(x, rel_bias, ln_mix, w_in, cmp_pos_k, cmp_pos_v, cmp_k_w1, cmp_k_w2, cmp_v_w1, cmp_v_w2,
           diff_lq1, diff_lk1, diff_lq2, diff_lk2, diff_subln, w_out, ln_ffn,
           router_group_w, router_group_b, router_expert_w, router_expert_b,
           exp_w_gate, exp_w_up, exp_w_down, ln_final):
    B, S, D = x.shape
    T = B * S
    assert D == D_MODEL and S % TQ == 0 and S >= WINDOW and T % TM_MOE == 0
    x2 = x.reshape(T, D)
    perm = _qa_perm()

    w = w_in[0]
    c_kc, c_vc, c_ks, c_vs, c_kw, c_vw, c_gt = 512, 640, 768, 896, 1024, 1152, 1280
    c_qb = c_gt + N_GATE
    c_kb, c_vb = c_qb + 512, c_qb + 1024
    col = lambda c, n=LANE: w[:, c:c + n]
    w_tok = jnp.concatenate([w[:, perm], col(c_kc), col(c_vc), col(c_ks), col(c_kw), col(c_qb, 512), col(c_kb, 512)],
                            axis=1).astype(BF16)
    w_feat = jnp.concatenate([col(c_vs), col(c_vw), col(c_vb, 512), col(c_gt, N_GATE),
                              jnp.zeros((D, GATE_ROWS - N_GATE), F32)], axis=1).T.astype(BF16)
    qa, kc, vc, ks, kw, qb, kb, vst, vwt, vbt, gates_t = _in_proj(x2, ln_mix[0][None, :], w_tok, w_feat)

    w1k = _block_diag2(cmp_k_w1[0].reshape(CMP_LEN, NSA_HD, CMP_HIDDEN)).astype(BF16)
    w1v = _block_diag2(cmp_v_w1[0].reshape(CMP_LEN, NSA_HD, CMP_HIDDEN)).astype(BF16)
    w2k = _block_diag2(cmp_k_w2[0]).astype(BF16)
    w2v = _block_diag2(cmp_v_w2[0]).astype(BF16)
    posk = jnp.tile(cmp_pos_k[0], (1, NSA_G))
    posv = jnp.tile(cmp_pos_v[0], (1, NSA_G))
    kcmp, vcmpt = _compress(kc, vc, posk, posv, w1k, w1v, w2k, w2v, B, S)

    near, bias_c = _bias_tiles(rel_bias.T, S)

    n_sel = S // SEL_BLOCK
    nrow = S // CMP_STRIDE
    c_start = np.arange(nrow) * CMP_STRIDE
    s_start = np.arange(n_sel) * SEL_BLOCK
    ov = ((c_start[None, :] <= s_start[:, None] + SEL_BLOCK - 1)
          & (c_start[None, :] + CMP_LEN - 1 >= s_start[:, None])
          & (np.arange(nrow)[None, :] < nrow - 1)).astype(np.float32)
    o_a = _nsa(qa, gates_t, kcmp, vcmpt, bias_c, ks, vst, kw, vwt, near[:NSA_HEADS], jnp.asarray(ov, BF16), B, S)

    sub = jnp.tile(diff_subln[0], 2)[:, None]
    o_b = _diff(diff_lq1[0][None, :], diff_lk1[0][None, :], diff_lq2[0][None, :], diff_lk2[0][None, :],
                sub, qb, kb, vbt, near[NSA_HEADS:], B, S)

    w_o = jnp.concatenate([w_out[0][:512][perm], w_out[0][512:]], axis=0).astype(BF16)
    n_r = MOE_GROUPS + N_EXPERTS
    wr = jnp.concatenate([router_group_w[0], router_expert_w[0], jnp.zeros((D, LANE - n_r), F32)], axis=1).astype(BF16)
    br = jnp.concatenate([router_group_b[0], router_expert_b[0], jnp.zeros((LANE - n_r,), F32)])[None, :]
    h1, xt, route, counts = _out_proj(x2, o_a, o_b, w_o, ln_ffn[0][None, :], wr, br)

    tm = TM_MOE
    n_tiles = T // tm + MOE_GROUPS
    cnt = counts[0, :MOE_GROUPS].astype(jnp.int32)
    ends = jnp.cumsum((cnt + tm - 1) // tm * tm)
    starts = ends - (cnt + tm - 1) // tm * tm
    pos = starts[route[:, 0].astype(jnp.int32)] + route[:, 1].astype(jnp.int32)
    src = jnp.zeros((n_tiles * tm,), jnp.int32).at[pos].set(jnp.arange(T, dtype=jnp.int32))
    tile_start = jnp.arange(n_tiles, dtype=jnp.int32) * tm
    tile_group = jnp.minimum(jnp.searchsorted(ends, tile_start, side="right"), MOE_GROUPS - 1).astype(jnp.int32)
    tile_ok = (tile_start < ends[-1]).astype(jnp.int32)

    by_group = lambda a: a[0].astype(BF16).reshape((MOE_GROUPS, EPG) + a.shape[2:])
    y_sorted = _moe(tile_group, tile_ok, src.reshape(n_tiles, 1, tm), xt,
                    by_group(exp_w_gate), by_group(exp_w_up), by_group(exp_w_down))
    out = _final(pos.reshape(T // tm, 1, tm), y_sorted, h1, ln_final[None, :])
    return out.reshape(B, S, D)
```

```python
import math

import numpy as np
import jax
import jax.numpy as jnp
from jax import lax
from jax.experimental import pallas as pl
from jax.experimental.pallas import tpu as pltpu

F32 = jnp.float32
BF16 = jnp.bfloat16
NEG = -1e30
EPS = 1e-6
LOG2E = math.log2(math.e)

D_MODEL = 1024
LANE = 128
NSA_HEADS, NSA_G, NSA_R, NSA_HD = 8, 2, 4, 64
CMP_LEN, CMP_STRIDE, CMP_HIDDEN = 32, 16, 128
SEL_BLOCK, SEL_TOPN, SEL_FORCED_LOCAL, WINDOW = 64, 8, 2, 512
DIFF_HEADS, DIFF_HD = 8, 32
REL_BUCKETS, REL_MAX_EXACT, REL_MAX_DIST = 32, 16, 128
N_REL_HEADS = NSA_HEADS + DIFF_HEADS
MOE_GROUPS, EPG, N_EXPERTS, EXPERT_FF = 4, 8, 32, 256
LAMBDA_INIT = 0.8 - 0.6 * math.exp(-0.3 * 0)
N_GATE = NSA_HEADS * 3
GATE_ROWS = 32

TQ = 256
TM_PROJ = 512
TM_MOE = 256
XT_WIDTH = D_MODEL + LANE
VMEM_LIMIT = 48 * 1024 * 1024
VMEM_LIMIT_MOE = 56 * 1024 * 1024


def _cparams(sem):
    return pltpu.CompilerParams(dimension_semantics=sem, vmem_limit_bytes=VMEM_LIMIT)


def _dot(a, b):
    return jnp.dot(a, b, preferred_element_type=F32)


def _dot_nt(a, b):
    return lax.dot_general(a, b, (((1,), (1,)), ((), ())), preferred_element_type=F32)


def _bucket_thresholds():
    n = np.arange(0, REL_MAX_DIST + 1)
    nf = np.maximum(n, 1).astype(np.float32)
    large = REL_MAX_EXACT + (np.log(nf / np.float32(REL_MAX_EXACT)) / np.float32(math.log(REL_MAX_DIST / REL_MAX_EXACT))
                             * np.float32(REL_BUCKETS - REL_MAX_EXACT)).astype(np.int32)
    large = np.minimum(large, REL_BUCKETS - 1)
    bucket = np.where(n < REL_MAX_EXACT, n, large)
    return [int(np.argmax(bucket >= b)) for b in range(REL_BUCKETS)]


_THR = _bucket_thresholds()


def _inproj_kernel(x_ref, g_ref, w_ref, wt_ref, qa_ref, kc_ref, vc_ref, ks_ref, kw_ref, qb_ref, kb_ref,
                   vst_ref, vwt_ref, vbt_ref, gt_ref):
    x = x_ref[...]
    xn = (x * lax.rsqrt(jnp.mean(x * x, axis=-1, keepdims=True) + EPS) * g_ref[...]).astype(BF16)
    a = _dot(xn, w_ref[:, 0:512])
    kc_ref[...] = a[:, 0:128]
    vc_ref[...] = a[:, 128:256]
    ks_ref[...] = a[:, 256:384].astype(BF16)
    kw_ref[...] = a[:, 384:512].astype(BF16)
    a = _dot(xn, w_ref[:, 512:1024])
    for r in range(4):
        kb_ref[r] = a[:, r * LANE:(r + 1) * LANE].astype(BF16)

    def feat(row0, rows=LANE):
        return _dot_nt(wt_ref[row0:row0 + rows, :], xn)
    for r in range(4):
        qa_ref[r] = (feat(r * LANE) * (NSA_HD ** -0.5 * LOG2E)).astype(BF16)
        qb_ref[r] = (feat(512 + r * LANE) * (DIFF_HD ** -0.5 * LOG2E)).astype(BF16)
        vbt_ref[r] = feat(1280 + r * LANE).astype(BF16)
    vst_ref[...] = feat(1024).astype(BF16)
    vwt_ref[...] = feat(1152).astype(BF16)
    gt_ref[...] = feat(1792, GATE_ROWS)


def _in_proj(x2, g, w, wt):
    T = x2.shape[0]
    tm = TM_PROJ
    row = lambda i: (i, 0)
    o128b = jax.ShapeDtypeStruct((T, LANE), BF16)
    o128f = jax.ShapeDtypeStruct((T, LANE), F32)
    o4 = jax.ShapeDtypeStruct((4, T, LANE), BF16)
    ot = jax.ShapeDtypeStruct((LANE, T), BF16)
    o4t = jax.ShapeDtypeStruct((4, LANE, T), BF16)
    s128 = pl.BlockSpec((tm, LANE), row)
    s4 = pl.BlockSpec((4, tm, LANE), lambda i: (0, i, 0))
    st = pl.BlockSpec((LANE, tm), lambda i: (0, i))
    s4t = pl.BlockSpec((4, LANE, tm), lambda i: (0, 0, i))
    return pl.pallas_call(
        _inproj_kernel,
        grid=(T // tm,),
        in_specs=[pl.BlockSpec((tm, D_MODEL), row),
                  pl.BlockSpec((1, D_MODEL), lambda i: (0, 0)),
                  pl.BlockSpec(w.shape, lambda i: (0, 0)),
                  pl.BlockSpec(wt.shape, lambda i: (0, 0))],
        out_specs=[s4t, s128, s128, s128, s128, s4t, s4, st, st, s4t, pl.BlockSpec((GATE_ROWS, tm), lambda i: (0, i))],
        out_shape=[o4t, o128f, o128f, o128b, o128b, o4t, o4, ot, ot, o4t, jax.ShapeDtypeStruct((GATE_ROWS, T), F32)],
        compiler_params=_cparams(("parallel",)),
        name="in_proj",
    )(x2, g, w, wt)


def _gelu_tanh(x):
    return 0.5 * x * (1.0 + jnp.tanh(math.sqrt(2.0 / math.pi) * (x + 0.044715 * (x * x * x))))


def _compress_kernel(kc_ref, vc_ref, posk_ref, posv_ref, w1k_ref, w1v_ref, w2k_ref, w2v_ref, ko_ref, vo_ref):
    nrow = kc_ref.shape[0] // CMP_STRIDE
    rid = lax.broadcasted_iota(jnp.int32, (nrow, 1), 0)
    cid = lax.broadcasted_iota(jnp.int32, (1, nrow), 1)
    for src, pos, w1, w2, out, transposed in ((kc_ref, posk_ref, w1k_ref, w2k_ref, ko_ref, False),
                                              (vc_ref, posv_ref, w1v_ref, w2v_ref, vo_ref, True)):
        hid_a = jnp.zeros((nrow, 2 * CMP_HIDDEN), F32)
        hid_b = jnp.zeros((nrow, 2 * CMP_HIDDEN), F32)
        for m in range(CMP_STRIDE):
            y = src[pl.ds(m, nrow, stride=CMP_STRIDE), :]
            hid_a = hid_a + _dot((y + pos[m:m + 1, :]).astype(BF16), w1[m])
            hid_b = hid_b + _dot((y + pos[CMP_STRIDE + m:CMP_STRIDE + m + 1, :]).astype(BF16), w1[CMP_STRIDE + m])
        hid = hid_a + pltpu.roll(hid_b, nrow - 1, 0)
        o = _dot(_gelu_tanh(hid).astype(BF16), w2[...])
        if transposed:
            out[0] = jnp.where(cid < nrow - 1, o.T, 0.0).astype(BF16)
        else:
            out[0] = jnp.where(rid < nrow - 1, o, 0.0).astype(BF16)


def _compress(kc, vc, posk, posv, w1k, w1v, w2k, w2v, B, S):
    nrow = S // CMP_STRIDE
    assert nrow == LANE
    full = lambda a: pl.BlockSpec(a.shape, lambda b: (0,) * a.ndim)
    src = pl.BlockSpec((S, LANE), lambda b: (b, 0))
    osp = pl.BlockSpec((1, nrow, LANE), lambda b: (b, 0, 0))
    osh = jax.ShapeDtypeStruct((B, nrow, LANE), BF16)
    return pl.pallas_call(
        _compress_kernel,
        grid=(B,),
        in_specs=[src, src, full(posk), full(posv), full(w1k), full(w1v), full(w2k), full(w2v)],
        out_specs=[osp, osp],
        out_shape=[osh, osh],
        compiler_params=_cparams(("parallel",)),
        name="compress",
    )(kc, vc, posk, posv, w1k, w1v, w2k, w2v)


def _bias_from_dist(dist, tab_ref, h):
    val = jnp.full(dist.shape, tab_ref[h, 0], F32)
    for b in range(1, REL_BUCKETS):
        val = jnp.where(dist >= _THR[b], tab_ref[h, b], val)
    return val


def _bias_near_kernel(tab_ref, out_ref):
    h = pl.program_id(0)
    j = lax.broadcasted_iota(jnp.int32, (TQ, TQ), 0)
    i = lax.broadcasted_iota(jnp.int32, (TQ, TQ), 1)
    for d in range(2):
        bias = (_bias_from_dist(i - j + d * TQ, tab_ref, h) - tab_ref[h, REL_BUCKETS - 1]) * LOG2E
        out_ref[0, d] = jnp.where(i - j + d * TQ >= 0, bias, NEG)


def _bias_cmp_kernel(tab_ref, out_ref):
    h = pl.program_id(0)
    c = lax.broadcasted_iota(jnp.int32, (LANE, TQ), 0)
    t = pl.program_id(1) * TQ + lax.broadcasted_iota(jnp.int32, (LANE, TQ), 1)
    out_ref[0] = _bias_from_dist(t - (c * CMP_STRIDE + CMP_LEN - 1), tab_ref, h) * LOG2E


def _bias_tiles(tab_t, S):
    smem = pl.BlockSpec(memory_space=pltpu.SMEM)
    near = pl.pallas_call(
        _bias_near_kernel,
        grid=(N_REL_HEADS,),
        in_specs=[smem],
        out_specs=pl.BlockSpec((1, 2, TQ, TQ), lambda h: (h, 0, 0, 0)),
        out_shape=jax.ShapeDtypeStruct((N_REL_HEADS, 2, TQ, TQ), F32),
        compiler_params=_cparams(("parallel",)),
        name="bias_near",
    )(tab_t)
    cmp_bias = pl.pallas_call(
        _bias_cmp_kernel,
        grid=(NSA_HEADS, S // TQ),
        in_specs=[smem],
        out_specs=pl.BlockSpec((1, LANE, TQ), lambda h, q: (h, 0, q)),
        out_shape=jax.ShapeDtypeStruct((NSA_HEADS, LANE, S), F32),
        compiler_params=_cparams(("parallel", "parallel")),
        name="bias_cmp",
    )(tab_t)
    return near, cmp_bias


ACC_ROWS = NSA_HD + 16


def _with_ones(vt):
    return jnp.concatenate([vt, jnp.ones((ACC_ROWS - vt.shape[0], vt.shape[1]), BF16)], axis=0)


def _probs(s, m):
    return jnp.exp2((s - m).astype(BF16))


def _flash_first(ss, vts):
    ms = [jnp.max(s, axis=0, keepdims=True) for s in ss]
    ps = [_probs(s, m) for s, m in zip(ss, ms)]
    return tuple((m, _dot(vt, p)) for m, p, vt in zip(ms, ps, vts))


def _flash_update(ss, vts, sts):
    ms = [jnp.maximum(st[0], jnp.max(s, axis=0, keepdims=True)) for s, st in zip(ss, sts)]
    alphas = [jnp.exp2(st[0] - m) for m, st in zip(ms, sts)]
    ps = [_probs(s, m) for s, m in zip(ss, ms)]
    return tuple((m, a * st[1] + _dot(vt, p)) for m, a, st, p, vt in zip(ms, alphas, sts, ps, vts))


def _flash_out(st):
    acc = st[1]
    return acc[0:NSA_HD, :] * (1.0 / acc[NSA_HD:NSA_HD + 1, :])


def _stream_scratch(n):
    return ([pltpu.VMEM((8, TQ), F32), pltpu.VMEM((ACC_ROWS, TQ), F32)] * n
            + [pltpu.VMEM((TQ, TQ), F32)] * (2 * n) + [pltpu.VMEM((TQ, TQ), BF16)] * (2 * n))


def _stream_refs(refs, n):
    st_refs = tuple((refs[2 * c], refs[2 * c + 1]) for c in range(n))
    s, p = refs[2 * n:4 * n], refs[4 * n:6 * n]
    return st_refs, tuple(s[:n]), tuple(s[n:]), tuple(p[:n]), tuple(p[n:])


def _causal_stream(qi, scores, vtile, st_refs, s_a, s_b, p_a, p_b):
    n_far = jnp.maximum(qi - 1, 0)
    top = qi - 2

    def write(s_refs, kt, near=2):
        for ref, s in zip(s_refs, scores(jnp.maximum(kt, 0), near)):
            ref[...] = s

    def pending(p_refs, kt):
        return [_dot(vt, ref[...]) for vt, ref in zip(vtile(jnp.maximum(kt, 0)), p_refs)]

    def step(s_cur, p_cur, s_next, kt_next, p_prev, kt_prev):
        write(s_next, kt_next)
        pend = pending(p_prev, kt_prev)
        for (st, acc), s_ref, p_ref, pv in zip(st_refs, s_cur, p_cur, pend):
            s = s_ref[...]
            m_old = st[0:1, :]
            m = jnp.maximum(m_old, jnp.max(s, axis=0, keepdims=True))
            p_ref[...] = _probs(s, m)
            st[0:1, :] = m
            acc[...] = jnp.exp2(m_old - m) * (acc[...] + pv)

    write(s_a, qi, 0)
    write(s_b, qi - 1, 1)
    for (st, acc), s_ref, p_ref in zip(st_refs, s_a, p_a):
        s = s_ref[...]
        m = jnp.max(s, axis=0, keepdims=True)
        p_ref[...] = _probs(s, m)
        st[0:1, :] = m
        acc[...] = jnp.zeros(acc.shape, F32)
    step(s_b, p_b, s_a, top, p_a, qi)

    def pair(j, c):
        kt = top - 2 * j
        step(s_a, p_a, s_b, kt - 1, p_b, kt + 1)
        step(s_b, p_b, s_a, kt - 2, p_a, kt)
        return c
    lax.fori_loop(0, n_far // 2, pair, 0)

    @pl.when(n_far % 2 == 1)
    def _():
        for (st, acc), s_ref, pv, vt in zip(st_refs, s_a, pending(p_b, 1), vtile(0)):
            s = s_ref[...]
            m_old = st[0:1, :]
            m = jnp.maximum(m_old, jnp.max(s, axis=0, keepdims=True))
            st[0:1, :] = m
            acc[...] = jnp.exp2(m_old - m) * (acc[...] + pv) + _dot(vt, _probs(s, m))

    @pl.when(n_far % 2 == 0)
    def _():
        last = jnp.where(n_far == 0, qi - 1, 0)
        for (st, acc), pv in zip(st_refs, pending(p_b, last)):
            acc[...] = acc[...] + pv
    return tuple((st[0:1, :], acc[...]) for st, acc in st_refs)


def _ktile(ref, kt):
    return ref[pl.ds(pl.multiple_of(kt * TQ, TQ), TQ), :]


def _vtile(ref, kt):
    return ref[:, pl.ds(pl.multiple_of(kt * TQ, TQ), TQ)]


def _nsa_kernel(qa_ref, gt_ref, kcmp_ref, vcmpt_ref, bc_ref, ks_ref, vst_ref, kw_ref, vwt_ref, dn_ref,
                ov_ref, o_ref, psum_ref, sel_ref, oacc_ref, sig_ref, *scratch):
    st_refs, s_a, s_b, p_a, p_b = _stream_refs(scratch, 2 * NSA_G)
    qi = pl.program_id(1)
    t0 = qi * TQ
    sub_grp = lax.shift_right_arithmetic(lax.broadcasted_iota(jnp.int32, (LANE, 1), 0), 6)
    sig_ref[...] = jax.nn.sigmoid(gt_ref[...])

    def gate_row(c):
        return sig_ref[pl.ds(c, 1), :]

    def masked_q(r, g):
        return jnp.where(sub_grp == g, qa_ref[r].astype(F32), 0.0).astype(BF16)

    n_cmp = kcmp_ref.shape[1] - 1
    crow = lax.broadcasted_iota(jnp.int32, (LANE, 1), 0)
    cmp_end = jnp.where(crow < n_cmp, crow * CMP_STRIDE + CMP_LEN - 1, 1 << 30)
    mask_c = (t0 + lax.broadcasted_iota(jnp.int32, (1, TQ), 1)) >= cmp_end

    heads = [(r, g) for r in range(NSA_R) for g in range(NSA_G)]
    scores_c = [_dot(kcmp_ref[0], masked_q(r, g)) for r, g in heads]
    probs_c = []
    for (r, g), s in zip(heads, scores_c):
        s = jnp.where(mask_c, s + bc_ref[g * NSA_R + r], NEG)
        p = jnp.where(mask_c, jnp.exp2(s - jnp.max(s, axis=0, keepdims=True)), 0.0)
        l = jnp.sum(p, axis=0, keepdims=True)
        probs_c.append(p * jnp.where(l > 0.0, 1.0 / l, 0.0))
    for g in range(NSA_G):
        psum_ref[g] = sum(p for (r, gg), p in zip(heads, probs_c) if gg == g)
    outs_c = [_dot(vcmpt_ref[0], p.astype(BF16)) for p in probs_c]
    for r in range(NSA_R):
        o0, o1 = outs_c[r * NSA_G], outs_c[r * NSA_G + 1]
        oacc_ref[r] = jnp.where(sub_grp == 0, gate_row(r * 3) * o0, gate_row((NSA_R + r) * 3) * o1)

    n_sel = ov_ref.shape[0]
    jj = lax.broadcasted_iota(jnp.int32, (n_sel, TQ), 0)
    cur = lax.shift_right_arithmetic(t0 + lax.broadcasted_iota(jnp.int32, (n_sel, TQ), 1), 6)
    valid = jj <= cur
    forced = (jj == 0) | (cur - jj < SEL_FORCED_LOCAL)
    for g in range(NSA_G):
        ps = psum_ref[g]
        hi = ps.astype(BF16)
        rem = ps - hi.astype(F32)
        mid = rem.astype(BF16)
        lo = (rem - mid.astype(F32)).astype(BF16)
        ov = ov_ref[...]
        imp = _dot(ov, hi) + _dot(ov, mid) + _dot(ov, lo)
        score = jnp.where(valid, jnp.where(forced, 1e9, imp), -1e9)
        cnt = jnp.zeros((n_sel, TQ), F32)
        for j2 in range(n_sel):
            row = score[j2:j2 + 1, :]
            tie = jnp.where(jj > j2, 1.0, 0.0)
            cnt = cnt + jnp.where(row > score, 1.0, jnp.where(row == score, tie, 0.0))
        sel_ref[g] = jnp.where(cnt < float(min(SEL_TOPN, n_sel)), jnp.where(score > -1e8, 0.0, NEG), NEG)

    has_prev = jnp.where(qi >= 1, 0.0, NEG)
    has_wfar = jnp.where(qi >= WINDOW // TQ, 0.0, NEG)
    kt_prev = jnp.maximum(qi - 1, 0)
    kt_wfar = jnp.maximum(qi - WINDOW // TQ, 0)
    wfar_mask = lax.broadcasted_iota(jnp.int32, (TQ, TQ), 0) > lax.broadcasted_iota(jnp.int32, (TQ, TQ), 1)

    def main_body(rp, carry):
        chains = [(2 * rp + i, g) for i in range(2) for g in range(NSA_G)]
        qms = [masked_q(r, g) for r, g in chains]
        hids = [g * NSA_R + r for r, g in chains]

        def block_mask(g, kt):
            per_tile = TQ // SEL_BLOCK
            rows = [jnp.broadcast_to(sel_ref[g, pl.ds(kt * per_tile + i, 1), :], (SEL_BLOCK, TQ))
                    for i in range(per_tile)]
            return jnp.concatenate(rows, axis=0)

        def near_bias(ss, near):
            if near == 0:
                return [s + dn_ref[hid, 0] for s, hid in zip(ss, hids)]
            if near == 1:
                return [s + dn_ref[hid, 1] + has_prev for s, hid in zip(ss, hids)]
            return ss

        def group_values(ref, kt):
            vt = _vtile(ref, kt)
            halves = [_with_ones(vt[g * NSA_HD:(g + 1) * NSA_HD, :]) for g in range(NSA_G)]
            return [halves[g] for r, g in chains]

        def finish(sts, branch):
            for c, (r, g) in enumerate(chains):
                oacc_ref[r, g * NSA_HD:(g + 1) * NSA_HD, :] += gate_row(hids[c] * 3 + branch) * _flash_out(sts[c])

        def slc_scores(kt, near=2):
            k = _ktile(ks_ref, kt)
            ss = near_bias([_dot(k, qm) for qm in qms], near)
            masks = [block_mask(g, kt) for g in range(NSA_G)]
            return [s + masks[g] for s, (r, g) in zip(ss, chains)]

        def slc_vtile(kt):
            return group_values(vst_ref, kt)

        finish(_causal_stream(qi, slc_scores, slc_vtile, st_refs, s_a, s_b, p_a, p_b), 1)

        win = []
        for kt, near in ((qi, 0), (kt_prev, 1), (kt_wfar, 2)):
            k = _ktile(kw_ref, kt)
            ss = near_bias([_dot(k, qm) for qm in qms], near)
            if near == 2:
                ss = [jnp.where(wfar_mask, s + has_wfar, NEG) for s in ss]
            win.append(ss)
        sts = _flash_first(win[0], group_values(vwt_ref, qi))
        sts = _flash_update(win[1], group_values(vwt_ref, kt_prev), sts)
        sts = _flash_update(win[2], group_values(vwt_ref, kt_wfar), sts)
        finish(sts, 2)
        return carry
    lax.fori_loop(0, NSA_R // 2, main_body, 0)

    for r in range(NSA_R):
        o_ref[r] = oacc_ref[r].T.astype(BF16)


def _nsa(qa, gates_t, kcmp, vcmpt, bias_c, ks, vst, kw, vwt, near_a, ov, B, S):
    nq = S // TQ
    T = B * S
    k_spec = pl.BlockSpec((S, LANE), lambda b, q: (b, 0))
    v_spec = pl.BlockSpec((LANE, S), lambda b, q: (0, b))
    full = lambda a: pl.BlockSpec(a.shape, lambda b, q: (0,) * a.ndim)
    cmp_spec = pl.BlockSpec((1, LANE, LANE), lambda b, q: (b, 0, 0))
    qo = pl.BlockSpec((4, TQ, LANE), lambda b, q: (0, b * nq + q, 0))
    return pl.pallas_call(
        _nsa_kernel,
        grid=(B, nq),
        in_specs=[pl.BlockSpec((4, LANE, TQ), lambda b, q: (0, 0, b * nq + q)),
                  pl.BlockSpec((GATE_ROWS, TQ), lambda b, q: (0, b * nq + q)),
                  cmp_spec, cmp_spec,
                  pl.BlockSpec((NSA_HEADS, LANE, TQ), lambda b, q: (0, 0, q)),
                  k_spec, v_spec, k_spec, v_spec, full(near_a), full(ov)],
        out_specs=qo,
        out_shape=jax.ShapeDtypeStruct((4, T, LANE), BF16),
        scratch_shapes=[pltpu.VMEM((NSA_G, LANE, TQ), F32), pltpu.VMEM((NSA_G, S // SEL_BLOCK, TQ), F32),
                        pltpu.VMEM((NSA_R, LANE, TQ), F32), pltpu.VMEM((GATE_ROWS, TQ), F32)] + _stream_scratch(2 * NSA_G),
        compiler_params=_cparams(("parallel", "arbitrary")),
        name="nsa",
    )(qa, gates_t, kcmp, vcmpt, bias_c, ks, vst, kw, vwt, near_a, ov)


def _diff_kernel(lq1_ref, lk1_ref, lq2_ref, lk2_ref, sub_ref, qb_ref, kb_ref, vbt_ref, dn_ref, o_ref, *scratch):
    n = 2 * 2
    st_refs, s_a, s_b, p_a, p_b = _stream_refs(scratch, n)
    qi = pl.program_id(1)
    chain_of_row = lax.shift_right_arithmetic(lax.broadcasted_iota(jnp.int32, (LANE, 1), 0), 5)
    lam = (jnp.exp(jnp.sum(lq1_ref[...] * lk1_ref[...], axis=-1, keepdims=True))
           - jnp.exp(jnp.sum(lq2_ref[...] * lk2_ref[...], axis=-1, keepdims=True)) + LAMBDA_INIT)
    has_prev = jnp.where(qi >= 1, 0.0, NEG)

    def pair_body(pr, carry):
        q = qb_ref[pr].astype(F32)
        qms = [jnp.where(chain_of_row == c, q, 0.0).astype(BF16) for c in range(n)]

        def scores(kt, near):
            k = kb_ref[pr, pl.ds(pl.multiple_of(kt * TQ, TQ), TQ), :]
            ss = [_dot(k, qm) for qm in qms]
            if near == 2:
                return ss
            bias = [dn_ref[2 * pr + hh, near] for hh in range(2)]
            if near == 1:
                bias = [b + has_prev for b in bias]
            return [s + bias[c // 2] for c, s in enumerate(ss)]

        def vtile(kt):
            vt = vbt_ref[pr, :, pl.ds(pl.multiple_of(kt * TQ, TQ), TQ)]
            halves = [_with_ones(vt[hh * 2 * DIFF_HD:(hh + 1) * 2 * DIFF_HD, :]) for hh in range(2)]
            return [halves[c // 2] for c in range(n)]

        sts = _causal_stream(qi, scores, vtile, st_refs, s_a, s_b, p_a, p_b)
        outs = []
        for hh in range(2):
            o = _flash_out(sts[2 * hh]) - lam * _flash_out(sts[2 * hh + 1])
            outs.append(o * lax.rsqrt(jnp.mean(o * o, axis=0, keepdims=True) + EPS))
        out = jnp.concatenate(outs, axis=0) * sub_ref[...] * (1.0 - LAMBDA_INIT)
        o_ref[pr] = out.T.astype(BF16)
        return carry
    lax.fori_loop(0, DIFF_HEADS // 2, pair_body, 0)


def _diff(lq1, lk1, lq2, lk2, sub, qb, kb, vbt, near_b, B, S):
    nq = S // TQ
    T = B * S
    full = lambda a: pl.BlockSpec(a.shape, lambda b, q: (0,) * a.ndim)
    k_spec = pl.BlockSpec((4, S, LANE), lambda b, q: (0, b, 0))
    v_spec = pl.BlockSpec((4, LANE, S), lambda b, q: (0, 0, b))
    qo = pl.BlockSpec((4, TQ, LANE), lambda b, q: (0, b * nq + q, 0))
    return pl.pallas_call(
        _diff_kernel,
        grid=(B, nq),
        in_specs=[full(lq1), full(lk1), full(lq2), full(lk2), full(sub),
                  pl.BlockSpec((4, LANE, TQ), lambda b, q: (0, 0, b * nq + q)), k_spec, v_spec, full(near_b)],
        out_specs=qo,
        out_shape=jax.ShapeDtypeStruct((4, T, LANE), BF16),
        scratch_shapes=_stream_scratch(4),
        compiler_params=_cparams(("parallel", "arbitrary")),
        name="diff",
    )(lq1, lk1, lq2, lk2, sub, qb, kb, vbt, near_b)


def _outproj_kernel(x_ref, oa_ref, ob_ref, w_ref, g_ref, wr_ref, br_ref, h_ref, xt_ref, route_ref, cnt_ref,
                    carry_ref):
    @pl.when(pl.program_id(0) == 0)
    def _():
        carry_ref[...] = jnp.zeros(carry_ref.shape, F32)

    o = jnp.concatenate([oa_ref[r] for r in range(4)] + [ob_ref[r] for r in range(4)], axis=-1)
    h = x_ref[...] + _dot(o, w_ref[...])
    h_ref[...] = h
    tn32 = h * lax.rsqrt(jnp.mean(h * h, axis=-1, keepdims=True) + EPS) * g_ref[...]
    xt_ref[:, 0:D_MODEL] = tn32
    tn = tn32.astype(BF16)
    logits = _dot(tn, wr_ref[...]) + br_ref[...]
    lane = lax.broadcasted_iota(jnp.int32, (1, LANE), 1)
    lane_f = lane.astype(F32)
    is_grp = lane < MOE_GROUPS
    lg = jnp.where(is_grp, logits, NEG)
    mg = jnp.max(lg, axis=-1, keepdims=True)
    zg = jnp.sum(jnp.where(is_grp, jnp.exp(lg - mg), 0.0), axis=-1, keepdims=True)
    g_prob = 1.0 / zg
    g_idx = jnp.min(jnp.where(lg == mg, lane_f, 1e9), axis=-1, keepdims=True)
    lane_grp = jnp.where((lane >= MOE_GROUPS) & (lane < MOE_GROUPS + N_EXPERTS),
                         lax.shift_right_arithmetic(lane - MOE_GROUPS, 3), -1).astype(F32)
    le = jnp.where(lane_grp == g_idx, logits, NEG)
    m1 = jnp.max(le, axis=-1, keepdims=True)
    e1 = jnp.min(jnp.where(le == m1, lane_f, 1e9), axis=-1, keepdims=True)
    le2 = jnp.where(lane_f == e1, NEG, le)
    m2 = jnp.max(le2, axis=-1, keepdims=True)
    e2 = jnp.min(jnp.where(le2 == m2, lane_f, 1e9), axis=-1, keepdims=True)
    ratio = jnp.exp(m2 - m1)
    w1 = g_prob / (1.0 + ratio)
    w2 = w1 * ratio
    xt_ref[:, D_MODEL:D_MODEL + LANE] = jnp.where(lane_f == e1, w1, 0.0) + jnp.where(lane_f == e2, w2, 0.0)

    tm = h.shape[0]
    onehot = jnp.where(lane_f == g_idx, 1.0, 0.0)
    earlier = jnp.where(lax.broadcasted_iota(jnp.int32, (tm, tm), 0) > lax.broadcasted_iota(jnp.int32, (tm, tm), 1),
                        1.0, 0.0).astype(BF16)
    prefix = _dot(earlier, onehot.astype(BF16)) + carry_ref[...]
    rank = jnp.sum(onehot * prefix, axis=-1, keepdims=True)
    carry_ref[...] += jnp.sum(onehot, axis=0, keepdims=True)
    route_ref[...] = jnp.where(lane == 0, g_idx, jnp.where(lane == 1, rank, 0.0))
    cnt_ref[...] = jnp.broadcast_to(carry_ref[...], cnt_ref.shape)


def _out_proj(x2, oa, ob, w, g, wr, br):
    T = x2.shape[0]
    tm = TM_PROJ
    row = lambda i: (i, 0)
    full = lambda a: pl.BlockSpec(a.shape, lambda i: (0,) * a.ndim)
    o4 = pl.BlockSpec((4, tm, LANE), lambda i: (0, i, 0))
    return pl.pallas_call(
        _outproj_kernel,
        grid=(T // tm,),
        in_specs=[pl.BlockSpec((tm, D_MODEL), row), o4, o4, full(w), full(g), full(wr), full(br)],
        out_specs=[pl.BlockSpec((tm, D_MODEL), row), pl.BlockSpec((tm, XT_WIDTH), row), pl.BlockSpec((tm, LANE), row),
                   pl.BlockSpec((8, LANE), lambda i: (0, 0))],
        out_shape=[jax.ShapeDtypeStruct((T, D_MODEL), F32), jax.ShapeDtypeStruct((T, XT_WIDTH), F32),
                   jax.ShapeDtypeStruct((T, LANE), F32), jax.ShapeDtypeStruct((8, LANE), F32)],
        scratch_shapes=[pltpu.VMEM((1, LANE), F32)],
        compiler_params=_cparams(("arbitrary",)),
        name="out_proj",
    )(x2, oa, ob, w, g, wr, br)


def _moe_kernel(tg_ref, ok_ref, idx_ref, idxn_ref, xt_hbm, wg_ref, wu_ref, wd_ref, y_ref, xbuf, sem):
    i = pl.program_id(0)
    slot = i % 2
    tm = xbuf.shape[1]

    def row_copy(index_ref, r, dst_slot):
        return pltpu.make_async_copy(xt_hbm.at[pl.ds(index_ref[0, 0, r], 1), :],
                                     xbuf.at[dst_slot, pl.ds(r, 1), :], sem.at[dst_slot])

    @pl.when(i == 0)
    def _():
        def body(r, c):
            row_copy(idx_ref, r, 0).start()
            return c
        lax.fori_loop(0, tm, body, 0)

    @pl.when((i == 0) | (ok_ref[jnp.maximum(i - 1, 0)] == 1))
    def _():
        pltpu.make_async_copy(xt_hbm.at[pl.ds(0, tm), :], xbuf.at[slot], sem.at[slot]).wait()

    @pl.when(ok_ref[i] == 0)
    def _():
        y_ref[...] = jnp.zeros(y_ref.shape, F32)

    @pl.when(ok_ref[i] == 1)
    def _():
        lane = lax.broadcasted_iota(jnp.int32, (1, LANE), 1)
        x = xbuf[slot, :, 0:D_MODEL].astype(BF16)
        cmb = xbuf[slot, :, D_MODEL:D_MODEL + LANE]
        first_lane = MOE_GROUPS + tg_ref[i] * EPG
        per_expert = tm // EPG
        y = jnp.zeros((tm, D_MODEL), F32)
        for e in range(EPG):
            a = _dot(x, wg_ref[0, e])
            b = _dot(x, wu_ref[0, e])
            ce = jnp.sum(jnp.where(lane == first_lane + e, cmb, 0.0), axis=-1, keepdims=True)
            y = y + _dot(((a * jax.nn.sigmoid(a)) * b * ce).astype(BF16), wd_ref[0, e])
            for r in range(e * per_expert, (e + 1) * per_expert):
                row_copy(idxn_ref, r, 1 - slot).start()
        y_ref[...] = y


def _moe(tile_group, tile_ok, src3, xt, wg, wu, wd):
    n_tiles, _, tm = src3.shape
    last = n_tiles - 1
    grid_spec = pltpu.PrefetchScalarGridSpec(
        num_scalar_prefetch=2,
        grid=(n_tiles,),
        in_specs=[pl.BlockSpec((1, 1, tm), lambda i, tg, ok: (i, 0, 0), memory_space=pltpu.SMEM),
                  pl.BlockSpec((1, 1, tm), lambda i, tg, ok: (jnp.minimum(i + 1, last), 0, 0), memory_space=pltpu.SMEM),
                  pl.BlockSpec(memory_space=pl.ANY),
                  pl.BlockSpec((1, EPG, D_MODEL, EXPERT_FF), lambda i, tg, ok: (tg[i], 0, 0, 0)),
                  pl.BlockSpec((1, EPG, D_MODEL, EXPERT_FF), lambda i, tg, ok: (tg[i], 0, 0, 0)),
                  pl.BlockSpec((1, EPG, EXPERT_FF, D_MODEL), lambda i, tg, ok: (tg[i], 0, 0, 0))],
        out_specs=pl.BlockSpec((tm, D_MODEL), lambda i, tg, ok: (i, 0)),
        scratch_shapes=[pltpu.VMEM((2, tm, XT_WIDTH), F32), pltpu.SemaphoreType.DMA((2,))],
    )
    return pl.pallas_call(
        _moe_kernel,
        grid_spec=grid_spec,
        out_shape=jax.ShapeDtypeStruct((n_tiles * tm, D_MODEL), F32),
        compiler_params=pltpu.CompilerParams(dimension_semantics=("arbitrary",), vmem_limit_bytes=VMEM_LIMIT_MOE),
        name="moe",
    )(tile_group, tile_ok, src3, src3, xt, wg, wu, wd)


def _final_kernel(pos_ref, posn_ref, y_hbm, h_ref, gf_ref, o_ref, ybuf, sem):
    i = pl.program_id(0)
    slot = i % 2
    tm = ybuf.shape[1]

    def issue(index_ref, dst_slot):
        def body(r, c):
            pltpu.make_async_copy(y_hbm.at[pl.ds(index_ref[0, 0, r], 1), :],
                                  ybuf.at[dst_slot, pl.ds(r, 1), :], sem.at[dst_slot]).start()
            return c
        lax.fori_loop(0, tm, body, 0, unroll=8)

    @pl.when(i == 0)
    def _():
        issue(pos_ref, 0)

    @pl.when(i + 1 < pl.num_programs(0))
    def _():
        issue(posn_ref, 1 - slot)

    pltpu.make_async_copy(y_hbm.at[pl.ds(0, tm), :], ybuf.at[slot], sem.at[slot]).wait()
    h = h_ref[...] + ybuf[slot]
    o_ref[...] = h * lax.rsqrt(jnp.mean(h * h, axis=-1, keepdims=True) + EPS) * gf_ref[...]


def _final(pos3, y_sorted, h, gf):
    n_tiles, _, tm = pos3.shape
    last = n_tiles - 1
    row = lambda i: (i, 0)
    return pl.pallas_call(
        _final_kernel,
        grid=(n_tiles,),
        in_specs=[pl.BlockSpec((1, 1, tm), lambda i: (i, 0, 0), memory_space=pltpu.SMEM),
                  pl.BlockSpec((1, 1, tm), lambda i: (jnp.minimum(i + 1, last), 0, 0), memory_space=pltpu.SMEM),
                  pl.BlockSpec(memory_space=pl.ANY),
                  pl.BlockSpec((tm, D_MODEL), row),
                  pl.BlockSpec((1, D_MODEL), lambda i: (0, 0))],
        out_specs=pl.BlockSpec((tm, D_MODEL), row),
        out_shape=jax.ShapeDtypeStruct((n_tiles * tm, D_MODEL), F32),
        scratch_shapes=[pltpu.VMEM((2, tm, D_MODEL), F32), pltpu.SemaphoreType.DMA((2,))],
        compiler_params=_cparams(("arbitrary",)),
        name="final",
    )(pos3, pos3, y_sorted, h, gf)


def _qa_perm():
    new = np.arange(NSA_HEADS * NSA_HD)
    r, g, d = new // LANE, (new % LANE) // NSA_HD, new % NSA_HD
    return (g * NSA_R + r) * NSA_HD + d


def _block_diag2(w):
    z = jnp.zeros_like(w)
    return jnp.concatenate([jnp.concatenate([w, z], axis=-1), jnp.concatenate([z, w], axis=-1)], axis=-2)


def kernel(x, rel_bias, ln_mix, w_in, cmp_pos_k, cmp_pos_v, cmp_k_w1, cmp_k_w2, cmp_v_w1, cmp_v_w2,
           diff_lq1, diff_lk1, diff_lq2, diff_lk2, diff_subln, w_out, ln_ffn,
           router_group_w, router_group_b, router_expert_w, router_expert_b,
           exp_w_gate, exp_w_up, exp_w_down, ln_final):
    B, S, D = x.shape
    T = B * S
    assert D == D_MODEL and S % TQ == 0 and S >= WINDOW and T % TM_MOE == 0
    x2 = x.reshape(T, D)
    perm = _qa_perm()

    w = w_in[0]
    c_kc, c_vc, c_ks, c_vs, c_kw, c_vw, c_gt = 512, 640, 768, 896, 1024, 1152, 1280
    c_qb = c_gt + N_GATE
    c_kb, c_vb = c_qb + 512, c_qb + 1024
    col = lambda c, n=LANE: w[:, c:c + n]
    w_tok = jnp.concatenate([col(c_kc), col(c_vc), col(c_ks), col(c_kw), col(c_kb, 512)], axis=1).astype(BF16)
    w_feat = jnp.concatenate([w[:, perm], col(c_qb, 512), col(c_vs), col(c_vw), col(c_vb, 512), col(c_gt, N_GATE),
                              jnp.zeros((D, GATE_ROWS - N_GATE), F32)], axis=1).T.astype(BF16)
    qa, kc, vc, ks, kw, qb, kb, vst, vwt, vbt, gates_t = _in_proj(x2, ln_mix[0][None, :], w_tok, w_feat)

    w1k = _block_diag2(cmp_k_w1[0].reshape(CMP_LEN, NSA_HD, CMP_HIDDEN)).astype(BF16)
    w1v = _block_diag2(cmp_v_w1[0].reshape(CMP_LEN, NSA_HD, CMP_HIDDEN)).astype(BF16)
    w2k = _block_diag2(cmp_k_w2[0]).astype(BF16)
    w2v = _block_diag2(cmp_v_w2[0]).astype(BF16)
    posk = jnp.tile(cmp_pos_k[0], (1, NSA_G))
    posv = jnp.tile(cmp_pos_v[0], (1, NSA_G))
    kcmp, vcmpt = _compress(kc, vc, posk, posv, w1k, w1v, w2k, w2v, B, S)

    near, bias_c = _bias_tiles(rel_bias.T, S)

    n_sel = S // SEL_BLOCK
    nrow = S // CMP_STRIDE
    c_start = np.arange(nrow) * CMP_STRIDE
    s_start = np.arange(n_sel) * SEL_BLOCK
    ov = ((c_start[None, :] <= s_start[:, None] + SEL_BLOCK - 1)
          & (c_start[None, :] + CMP_LEN - 1 >= s_start[:, None])
          & (np.arange(nrow)[None, :] < nrow - 1)).astype(np.float32)
    o_a = _nsa(qa, gates_t, kcmp, vcmpt, bias_c, ks, vst, kw, vwt, near[:NSA_HEADS], jnp.asarray(ov, BF16), B, S)

    sub = jnp.tile(diff_subln[0], 2)[:, None]
    o_b = _diff(diff_lq1[0][None, :], diff_lk1[0][None, :], diff_lq2[0][None, :], diff_lk2[0][None, :],
                sub, qb, kb, vbt, near[NSA_HEADS:], B, S)

    w_o = jnp.concatenate([w_out[0][:512][perm], w_out[0][512:]], axis=0).astype(BF16)
    n_r = MOE_GROUPS + N_EXPERTS
    wr = jnp.concatenate([router_group_w[0], router_expert_w[0], jnp.zeros((D, LANE - n_r), F32)], axis=1).astype(BF16)
    br = jnp.concatenate([router_group_b[0], router_expert_b[0], jnp.zeros((LANE - n_r,), F32)])[None, :]
    h1, xt, route, counts = _out_proj(x2, o_a, o_b, w_o, ln_ffn[0][None, :], wr, br)

    tm = TM_MOE
    n_tiles = T // tm + MOE_GROUPS
    cnt = counts[0, :MOE_GROUPS].astype(jnp.int32)
    ends = jnp.cumsum((cnt + tm - 1) // tm * tm)
    starts = ends - (cnt + tm - 1) // tm * tm
    pos = starts[route[:, 0].astype(jnp.int32)] + route[:, 1].astype(jnp.int32)
    src = jnp.zeros((n_tiles * tm,), jnp.int32).at[pos].set(jnp.arange(T, dtype=jnp.int32))
    tile_start = jnp.arange(n_tiles, dtype=jnp.int32) * tm
    tile_group = jnp.minimum(jnp.searchsorted(ends, tile_start, side="right"), MOE_GROUPS - 1).astype(jnp.int32)
    tile_ok = (tile_start < ends[-1]).astype(jnp.int32)

    by_group = lambda a: a[0].astype(BF16).reshape((MOE_GROUPS, EPG) + a.shape[2:])
    y_sorted = _moe(tile_group, tile_ok, src.reshape(n_tiles, 1, tm), xt,
                    by_group(exp_w_gate), by_group(exp_w_up), by_group(exp_w_down))
    out = _final(pos.reshape(T // tm, 1, tm), y_sorted, h1, ln_final[None, :])
    return out.reshape(B, S, D)
```

```python
import math

import numpy as np
import jax
import jax.numpy as jnp
from jax import lax
from jax.experimental import pallas as pl
from jax.experimental.pallas import tpu as pltpu

F32 = jnp.float32
BF16 = jnp.bfloat16
NEG = -1e30
EPS = 1e-6
LOG2E = math.log2(math.e)

D_MODEL = 1024
LANE = 128
NSA_HEADS, NSA_G, NSA_R, NSA_HD = 8, 2, 4, 64
CMP_LEN, CMP_STRIDE, CMP_HIDDEN = 32, 16, 128
SEL_BLOCK, SEL_TOPN, SEL_FORCED_LOCAL, WINDOW = 64, 8, 2, 512
DIFF_HEADS, DIFF_HD = 8, 32
REL_BUCKETS, REL_MAX_EXACT, REL_MAX_DIST = 32, 16, 128
N_REL_HEADS = NSA_HEADS + DIFF_HEADS
MOE_GROUPS, EPG, N_EXPERTS, EXPERT_FF = 4, 8, 32, 256
LAMBDA_INIT = 0.8 - 0.6 * math.exp(-0.3 * 0)
N_GATE = NSA_HEADS * 3
GATE_ROWS = 32

TQ = 256
TM_PROJ = 512
TM_MOE = 512
TM_FINAL = 256
XT_WIDTH = D_MODEL + LANE
ROW_SLABS = D_MODEL // LANE
VMEM_LIMIT = 48 * 1024 * 1024
VMEM_LIMIT_MOE = 56 * 1024 * 1024


def _cparams(sem):
    return pltpu.CompilerParams(dimension_semantics=sem, vmem_limit_bytes=VMEM_LIMIT)


def _dot(a, b):
    return jnp.dot(a, b, preferred_element_type=F32)


def _dot_nt(a, b):
    return lax.dot_general(a, b, (((1,), (1,)), ((), ())), preferred_element_type=F32)


def _bucket_thresholds():
    n = np.arange(0, REL_MAX_DIST + 1)
    nf = np.maximum(n, 1).astype(np.float32)
    large = REL_MAX_EXACT + (np.log(nf / np.float32(REL_MAX_EXACT)) / np.float32(math.log(REL_MAX_DIST / REL_MAX_EXACT))
                             * np.float32(REL_BUCKETS - REL_MAX_EXACT)).astype(np.int32)
    large = np.minimum(large, REL_BUCKETS - 1)
    bucket = np.where(n < REL_MAX_EXACT, n, large)
    return [int(np.argmax(bucket >= b)) for b in range(REL_BUCKETS)]


_THR = _bucket_thresholds()


def _inproj_kernel(x_ref, g_ref, w_ref, wt_ref, qa_ref, kc_ref, vc_ref, ks_ref, kw_ref, qb_ref, kb_ref,
                   vst_ref, vwt_ref, vbt_ref, gt_ref):
    x = x_ref[...]
    xn = (x * lax.rsqrt(jnp.mean(x * x, axis=-1, keepdims=True) + EPS) * g_ref[...]).astype(BF16)
    a = _dot(xn, w_ref[:, 0:512])
    kc_ref[...] = a[:, 0:128]
    vc_ref[...] = a[:, 128:256]
    ks_ref[...] = a[:, 256:384].astype(BF16)
    kw_ref[...] = a[:, 384:512].astype(BF16)
    a = _dot(xn, w_ref[:, 512:1024])
    for r in range(4):
        kb_ref[r] = a[:, r * LANE:(r + 1) * LANE].astype(BF16)

    ft = _dot_nt(wt_ref[...], xn)

    def feat(row0, rows=LANE):
        return ft[row0:row0 + rows, :]
    for r in range(4):
        qa_ref[r] = (feat(r * LANE) * (NSA_HD ** -0.5 * LOG2E)).astype(BF16)
        qb_ref[r] = (feat(512 + r * LANE) * (DIFF_HD ** -0.5 * LOG2E)).astype(BF16)
        vbt_ref[r] = feat(1280 + r * LANE).astype(BF16)
    vst_ref[...] = feat(1024).astype(BF16)
    vwt_ref[...] = feat(1152).astype(BF16)
    gt_ref[...] = feat(1792, GATE_ROWS)


def _in_proj(x2, g, w, wt):
    T = x2.shape[0]
    tm = TM_PROJ
    row = lambda i: (i, 0)
    o128b = jax.ShapeDtypeStruct((T, LANE), BF16)
    o128f = jax.ShapeDtypeStruct((T, LANE), F32)
    o4 = jax.ShapeDtypeStruct((4, T, LANE), BF16)
    ot = jax.ShapeDtypeStruct((LANE, T), BF16)
    o4t = jax.ShapeDtypeStruct((4, LANE, T), BF16)
    s128 = pl.BlockSpec((tm, LANE), row)
    s4 = pl.BlockSpec((4, tm, LANE), lambda i: (0, i, 0))
    st = pl.BlockSpec((LANE, tm), lambda i: (0, i))
    s4t = pl.BlockSpec((4, LANE, tm), lambda i: (0, 0, i))
    return pl.pallas_call(
        _inproj_kernel,
        grid=(T // tm,),
        in_specs=[pl.BlockSpec((tm, D_MODEL), row),
                  pl.BlockSpec((1, D_MODEL), lambda i: (0, 0)),
                  pl.BlockSpec(w.shape, lambda i: (0, 0)),
                  pl.BlockSpec(wt.shape, lambda i: (0, 0))],
        out_specs=[s4t, s128, s128, s128, s128, s4t, s4, st, st, s4t, pl.BlockSpec((GATE_ROWS, tm), lambda i: (0, i))],
        out_shape=[o4t, o128f, o128f, o128b, o128b, o4t, o4, ot, ot, o4t, jax.ShapeDtypeStruct((GATE_ROWS, T), F32)],
        compiler_params=_cparams(("parallel",)),
        name="in_proj",
    )(x2, g, w, wt)


def _gelu_tanh(x):
    return 0.5 * x * (1.0 + jnp.tanh(math.sqrt(2.0 / math.pi) * (x + 0.044715 * (x * x * x))))


def _compress_kernel(kc_ref, vc_ref, posk_ref, posv_ref, w1k_ref, w1v_ref, w2k_ref, w2v_ref, ko_ref, vo_ref):
    nrow = kc_ref.shape[0] // CMP_STRIDE
    rid = lax.broadcasted_iota(jnp.int32, (nrow, 1), 0)
    cid = lax.broadcasted_iota(jnp.int32, (1, nrow), 1)
    for src, pos, w1, w2, out, transposed in ((kc_ref, posk_ref, w1k_ref, w2k_ref, ko_ref, False),
                                              (vc_ref, posv_ref, w1v_ref, w2v_ref, vo_ref, True)):
        hid_a = jnp.zeros((nrow, 2 * CMP_HIDDEN), F32)
        hid_b = jnp.zeros((nrow, 2 * CMP_HIDDEN), F32)
        for m in range(CMP_STRIDE):
            y = src[pl.ds(m, nrow, stride=CMP_STRIDE), :]
            hid_a = hid_a + _dot((y + pos[m:m + 1, :]).astype(BF16), w1[m])
            hid_b = hid_b + _dot((y + pos[CMP_STRIDE + m:CMP_STRIDE + m + 1, :]).astype(BF16), w1[CMP_STRIDE + m])
        hid = hid_a + pltpu.roll(hid_b, nrow - 1, 0)
        o = _dot(_gelu_tanh(hid).astype(BF16), w2[...])
        if transposed:
            out[0] = jnp.where(cid < nrow - 1, o.T, 0.0).astype(BF16)
        else:
            out[0] = jnp.where(rid < nrow - 1, o, 0.0).astype(BF16)


def _compress(kc, vc, posk, posv, w1k, w1v, w2k, w2v, B, S):
    nrow = S // CMP_STRIDE
    assert nrow == LANE
    full = lambda a: pl.BlockSpec(a.shape, lambda b: (0,) * a.ndim)
    src = pl.BlockSpec((S, LANE), lambda b: (b, 0))
    osp = pl.BlockSpec((1, nrow, LANE), lambda b: (b, 0, 0))
    osh = jax.ShapeDtypeStruct((B, nrow, LANE), BF16)
    return pl.pallas_call(
        _compress_kernel,
        grid=(B,),
        in_specs=[src, src, full(posk), full(posv), full(w1k), full(w1v), full(w2k), full(w2v)],
        out_specs=[osp, osp],
        out_shape=[osh, osh],
        compiler_params=_cparams(("parallel",)),
        name="compress",
    )(kc, vc, posk, posv, w1k, w1v, w2k, w2v)


def _bias_from_dist(dist, tab_ref, h):
    val = jnp.full(dist.shape, tab_ref[h, 0], F32)
    for b in range(1, REL_BUCKETS):
        val = jnp.where(dist >= _THR[b], tab_ref[h, b], val)
    return val


def _bias_near_kernel(tab_ref, out_ref):
    h = pl.program_id(0)
    j = lax.broadcasted_iota(jnp.int32, (TQ, TQ), 0)
    i = lax.broadcasted_iota(jnp.int32, (TQ, TQ), 1)
    for d in range(2):
        bias = (_bias_from_dist(i - j + d * TQ, tab_ref, h) - tab_ref[h, REL_BUCKETS - 1]) * LOG2E
        out_ref[0, d] = jnp.where(i - j + d * TQ >= 0, bias, NEG)


def _bias_cmp_kernel(tab_ref, out_ref):
    h = pl.program_id(0)
    c = lax.broadcasted_iota(jnp.int32, (LANE, TQ), 0)
    t = pl.program_id(1) * TQ + lax.broadcasted_iota(jnp.int32, (LANE, TQ), 1)
    out_ref[0] = _bias_from_dist(t - (c * CMP_STRIDE + CMP_LEN - 1), tab_ref, h) * LOG2E


def _bias_tiles(tab_t, S):
    smem = pl.BlockSpec(memory_space=pltpu.SMEM)
    near = pl.pallas_call(
        _bias_near_kernel,
        grid=(N_REL_HEADS,),
        in_specs=[smem],
        out_specs=pl.BlockSpec((1, 2, TQ, TQ), lambda h: (h, 0, 0, 0)),
        out_shape=jax.ShapeDtypeStruct((N_REL_HEADS, 2, TQ, TQ), F32),
        compiler_params=_cparams(("parallel",)),
        name="bias_near",
    )(tab_t)
    cmp_bias = pl.pallas_call(
        _bias_cmp_kernel,
        grid=(NSA_HEADS, S // TQ),
        in_specs=[smem],
        out_specs=pl.BlockSpec((1, LANE, TQ), lambda h, q: (h, 0, q)),
        out_shape=jax.ShapeDtypeStruct((NSA_HEADS, LANE, S), F32),
        compiler_params=_cparams(("parallel", "parallel")),
        name="bias_cmp",
    )(tab_t)
    return near, cmp_bias


ACC_ROWS = NSA_HD + 16


def _with_ones(vt):
    return jnp.concatenate([vt, jnp.ones((ACC_ROWS - vt.shape[0], vt.shape[1]), BF16)], axis=0)


def _probs(s, m):
    return jnp.exp2((s - m).astype(BF16))


def _flash_first(ss, vts):
    ms = [jnp.max(s, axis=0, keepdims=True) for s in ss]
    ps = [_probs(s, m) for s, m in zip(ss, ms)]
    return tuple((m, _dot(vt, p)) for m, p, vt in zip(ms, ps, vts))


def _flash_update(ss, vts, sts):
    ms = [jnp.maximum(st[0], jnp.max(s, axis=0, keepdims=True)) for s, st in zip(ss, sts)]
    alphas = [jnp.exp2(st[0] - m) for m, st in zip(ms, sts)]
    ps = [_probs(s, m) for s, m in zip(ss, ms)]
    return tuple((m, a * st[1] + _dot(vt, p)) for m, a, st, p, vt in zip(ms, alphas, sts, ps, vts))


def _flash_out(st):
    acc = st[1]
    return acc[0:NSA_HD, :] * (1.0 / acc[NSA_HD:NSA_HD + 1, :])


def _stream_scratch(n):
    return ([pltpu.VMEM((8, TQ), F32), pltpu.VMEM((ACC_ROWS, TQ), F32)] * n
            + [pltpu.VMEM((TQ, TQ), F32)] * (2 * n) + [pltpu.VMEM((TQ, TQ), BF16)] * (2 * n))


def _stream_refs(refs, n):
    st_refs = tuple((refs[2 * c], refs[2 * c + 1]) for c in range(n))
    s, p = refs[2 * n:4 * n], refs[4 * n:6 * n]
    return st_refs, tuple(s[:n]), tuple(s[n:]), tuple(p[:n]), tuple(p[n:])


def _causal_stream(qi, scores, vtile, st_refs, s_a, s_b, p_a, p_b):
    n_far = jnp.maximum(qi - 1, 0)
    top = qi - 2

    def write(s_refs, kt, near=2):
        for ref, s in zip(s_refs, scores(jnp.maximum(kt, 0), near)):
            ref[...] = s

    def pending(p_refs, kt):
        return [_dot(vt, ref[...]) for vt, ref in zip(vtile(jnp.maximum(kt, 0)), p_refs)]

    def step(s_cur, p_cur, s_next, kt_next, p_prev, kt_prev):
        write(s_next, kt_next)
        pend = pending(p_prev, kt_prev)
        for (st, acc), s_ref, p_ref, pv in zip(st_refs, s_cur, p_cur, pend):
            s = s_ref[...]
            m_old = st[0:1, :]
            m = jnp.maximum(m_old, jnp.max(s, axis=0, keepdims=True))
            p_ref[...] = _probs(s, m)
            st[0:1, :] = m
            acc[...] = jnp.exp2(m_old - m) * (acc[...] + pv)

    write(s_a, qi, 0)
    write(s_b, qi - 1, 1)
    for (st, acc), s_ref, p_ref in zip(st_refs, s_a, p_a):
        s = s_ref[...]
        m = jnp.max(s, axis=0, keepdims=True)
        p_ref[...] = _probs(s, m)
        st[0:1, :] = m
        acc[...] = jnp.zeros(acc.shape, F32)
    step(s_b, p_b, s_a, top, p_a, qi)

    def pair(j, c):
        kt = top - 2 * j
        step(s_a, p_a, s_b, kt - 1, p_b, kt + 1)
        step(s_b, p_b, s_a, kt - 2, p_a, kt)
        return c
    lax.fori_loop(0, n_far // 2, pair, 0)

    @pl.when(n_far % 2 == 1)
    def _():
        for (st, acc), s_ref, pv, vt in zip(st_refs, s_a, pending(p_b, 1), vtile(0)):
            s = s_ref[...]
            m_old = st[0:1, :]
            m = jnp.maximum(m_old, jnp.max(s, axis=0, keepdims=True))
            st[0:1, :] = m
            acc[...] = jnp.exp2(m_old - m) * (acc[...] + pv) + _dot(vt, _probs(s, m))

    @pl.when(n_far % 2 == 0)
    def _():
        last = jnp.where(n_far == 0, qi - 1, 0)
        for (st, acc), pv in zip(st_refs, pending(p_b, last)):
            acc[...] = acc[...] + pv
    return tuple((st[0:1, :], acc[...]) for st, acc in st_refs)


def _ktile(ref, kt):
    return ref[pl.ds(pl.multiple_of(kt * TQ, TQ), TQ), :]


def _vtile(ref, kt):
    return ref[:, pl.ds(pl.multiple_of(kt * TQ, TQ), TQ)]


def _nsa_kernel(qa_ref, gt_ref, kcmp_ref, vcmpt_ref, bc_ref, ks_ref, vst_ref, kw_ref, vwt_ref, dn_ref,
                ov_ref, o_ref, psum_ref, sel_ref, oacc_ref, sig_ref, *scratch):
    st_refs, s_a, s_b, p_a, p_b = _stream_refs(scratch, 2 * NSA_G)
    qi = pl.program_id(1)
    t0 = qi * TQ
    sub_grp = lax.shift_right_arithmetic(lax.broadcasted_iota(jnp.int32, (LANE, 1), 0), 6)
    sig_ref[...] = jax.nn.sigmoid(gt_ref[...])

    def gate_row(c):
        return sig_ref[pl.ds(c, 1), :]

    def masked_q(r, g):
        return jnp.where(sub_grp == g, qa_ref[r].astype(F32), 0.0).astype(BF16)

    n_cmp = kcmp_ref.shape[1] - 1
    crow = lax.broadcasted_iota(jnp.int32, (LANE, 1), 0)
    cmp_end = jnp.where(crow < n_cmp, crow * CMP_STRIDE + CMP_LEN - 1, 1 << 30)
    mask_c = (t0 + lax.broadcasted_iota(jnp.int32, (1, TQ), 1)) >= cmp_end

    heads = [(r, g) for r in range(NSA_R) for g in range(NSA_G)]
    scores_c = [_dot(kcmp_ref[0], masked_q(r, g)) for r, g in heads]
    probs_c = []
    for (r, g), s in zip(heads, scores_c):
        s = jnp.where(mask_c, s + bc_ref[g * NSA_R + r], NEG)
        p = jnp.where(mask_c, jnp.exp2(s - jnp.max(s, axis=0, keepdims=True)), 0.0)
        l = jnp.sum(p, axis=0, keepdims=True)
        probs_c.append(p * jnp.where(l > 0.0, 1.0 / l, 0.0))
    for g in range(NSA_G):
        psum_ref[g] = sum(p for (r, gg), p in zip(heads, probs_c) if gg == g)
    outs_c = [_dot(vcmpt_ref[0], p.astype(BF16)) for p in probs_c]
    for r in range(NSA_R):
        o0, o1 = outs_c[r * NSA_G], outs_c[r * NSA_G + 1]
        oacc_ref[r] = jnp.where(sub_grp == 0, gate_row(r * 3) * o0, gate_row((NSA_R + r) * 3) * o1)

    n_sel = ov_ref.shape[0]
    jj = lax.broadcasted_iota(jnp.int32, (n_sel, TQ), 0)
    cur = lax.shift_right_arithmetic(t0 + lax.broadcasted_iota(jnp.int32, (n_sel, TQ), 1), 6)
    valid = jj <= cur
    forced = (jj == 0) | (cur - jj < SEL_FORCED_LOCAL)
    for g in range(NSA_G):
        ps = psum_ref[g]
        hi = ps.astype(BF16)
        rem = ps - hi.astype(F32)
        mid = rem.astype(BF16)
        lo = (rem - mid.astype(F32)).astype(BF16)
        ov = ov_ref[...]
        imp = _dot(ov, hi) + _dot(ov, mid) + _dot(ov, lo)
        score = jnp.where(valid, jnp.where(forced, 1e9, imp), -1e9)
        cnt = jnp.zeros((n_sel, TQ), F32)
        for j2 in range(n_sel):
            row = score[j2:j2 + 1, :]
            tie = jnp.where(jj > j2, 1.0, 0.0)
            cnt = cnt + jnp.where(row > score, 1.0, jnp.where(row == score, tie, 0.0))
        sel_ref[g] = jnp.where(cnt < float(min(SEL_TOPN, n_sel)), jnp.where(score > -1e8, 0.0, NEG), NEG)

    has_prev = jnp.where(qi >= 1, 0.0, NEG)
    has_wfar = jnp.where(qi >= WINDOW // TQ, 0.0, NEG)
    kt_prev = jnp.maximum(qi - 1, 0)
    kt_wfar = jnp.maximum(qi - WINDOW // TQ, 0)
    wfar_mask = lax.broadcasted_iota(jnp.int32, (TQ, TQ), 0) > lax.broadcasted_iota(jnp.int32, (TQ, TQ), 1)

    def main_body(rp, carry):
        chains = [(2 * rp + i, g) for i in range(2) for g in range(NSA_G)]
        qms = [masked_q(r, g) for r, g in chains]
        hids = [g * NSA_R + r for r, g in chains]

        def block_mask(g, kt):
            per_tile = TQ // SEL_BLOCK
            rows = [jnp.broadcast_to(sel_ref[g, pl.ds(kt * per_tile + i, 1), :], (SEL_BLOCK, TQ))
                    for i in range(per_tile)]
            return jnp.concatenate(rows, axis=0)

        def near_bias(ss, near):
            if near == 0:
                return [s + dn_ref[hid, 0] for s, hid in zip(ss, hids)]
            if near == 1:
                return [s + dn_ref[hid, 1] + has_prev for s, hid in zip(ss, hids)]
            return ss

        def group_values(ref, kt):
            vt = _vtile(ref, kt)
            halves = [_with_ones(vt[g * NSA_HD:(g + 1) * NSA_HD, :]) for g in range(NSA_G)]
            return [halves[g] for r, g in chains]

        def finish(sts, branch):
            for c, (r, g) in enumerate(chains):
                oacc_ref[r, g * NSA_HD:(g + 1) * NSA_HD, :] += gate_row(hids[c] * 3 + branch) * _flash_out(sts[c])

        def slc_scores(kt, near=2):
            k = _ktile(ks_ref, kt)
            ss = near_bias([_dot(k, qm) for qm in qms], near)
            masks = [block_mask(g, kt) for g in range(NSA_G)]
            return [s + masks[g] for s, (r, g) in zip(ss, chains)]

        def slc_vtile(kt):
            return group_values(vst_ref, kt)

        finish(_causal_stream(qi, slc_scores, slc_vtile, st_refs, s_a, s_b, p_a, p_b), 1)

        win = []
        for kt, near in ((qi, 0), (kt_prev, 1), (kt_wfar, 2)):
            k = _ktile(kw_ref, kt)
            ss = near_bias([_dot(k, qm) for qm in qms], near)
            if near == 2:
                ss = [jnp.where(wfar_mask, s + has_wfar, NEG) for s in ss]
            win.append(ss)
        sts = _flash_first(win[0], group_values(vwt_ref, qi))
        sts = _flash_update(win[1], group_values(vwt_ref, kt_prev), sts)
        sts = _flash_update(win[2], group_values(vwt_ref, kt_wfar), sts)
        finish(sts, 2)
        return carry
    lax.fori_loop(0, NSA_R // 2, main_body, 0)

    for r in range(NSA_R):
        o_ref[r] = oacc_ref[r].T.astype(BF16)


def _nsa(qa, gates_t, kcmp, vcmpt, bias_c, ks, vst, kw, vwt, near_a, ov, B, S):
    nq = S // TQ
    T = B * S
    k_spec = pl.BlockSpec((S, LANE), lambda b, q: (b, 0))
    v_spec = pl.BlockSpec((LANE, S), lambda b, q: (0, b))
    full = lambda a: pl.BlockSpec(a.shape, lambda b, q: (0,) * a.ndim)
    cmp_spec = pl.BlockSpec((1, LANE, LANE), lambda b, q: (b, 0, 0))
    qo = pl.BlockSpec((4, TQ, LANE), lambda b, q: (0, b * nq + q, 0))
    return pl.pallas_call(
        _nsa_kernel,
        grid=(B, nq),
        in_specs=[pl.BlockSpec((4, LANE, TQ), lambda b, q: (0, 0, b * nq + q)),
                  pl.BlockSpec((GATE_ROWS, TQ), lambda b, q: (0, b * nq + q)),
                  cmp_spec, cmp_spec,
                  pl.BlockSpec((NSA_HEADS, LANE, TQ), lambda b, q: (0, 0, q)),
                  k_spec, v_spec, k_spec, v_spec, full(near_a), full(ov)],
        out_specs=qo,
        out_shape=jax.ShapeDtypeStruct((4, T, LANE), BF16),
        scratch_shapes=[pltpu.VMEM((NSA_G, LANE, TQ), F32), pltpu.VMEM((NSA_G, S // SEL_BLOCK, TQ), F32),
                        pltpu.VMEM((NSA_R, LANE, TQ), F32), pltpu.VMEM((GATE_ROWS, TQ), F32)] + _stream_scratch(2 * NSA_G),
        compiler_params=_cparams(("parallel", "arbitrary")),
        name="nsa",
    )(qa, gates_t, kcmp, vcmpt, bias_c, ks, vst, kw, vwt, near_a, ov)


def _diff_kernel(lq1_ref, lk1_ref, lq2_ref, lk2_ref, sub_ref, qb_ref, kb_ref, vbt_ref, dn_ref, o_ref, *scratch):
    n = 2 * 2
    st_refs, s_a, s_b, p_a, p_b = _stream_refs(scratch, n)
    qi = pl.program_id(1)
    chain_of_row = lax.shift_right_arithmetic(lax.broadcasted_iota(jnp.int32, (LANE, 1), 0), 5)
    lam = (jnp.exp(jnp.sum(lq1_ref[...] * lk1_ref[...], axis=-1, keepdims=True))
           - jnp.exp(jnp.sum(lq2_ref[...] * lk2_ref[...], axis=-1, keepdims=True)) + LAMBDA_INIT)
    has_prev = jnp.where(qi >= 1, 0.0, NEG)

    def pair_body(pr, carry):
        q = qb_ref[pr].astype(F32)
        qms = [jnp.where(chain_of_row == c, q, 0.0).astype(BF16) for c in range(n)]

        def scores(kt, near):
            k = kb_ref[pr, pl.ds(pl.multiple_of(kt * TQ, TQ), TQ), :]
            ss = [_dot(k, qm) for qm in qms]
            if near == 2:
                return ss
            bias = [dn_ref[2 * pr + hh, near] for hh in range(2)]
            if near == 1:
                bias = [b + has_prev for b in bias]
            return [s + bias[c // 2] for c, s in enumerate(ss)]

        def vtile(kt):
            vt = vbt_ref[pr, :, pl.ds(pl.multiple_of(kt * TQ, TQ), TQ)]
            halves = [_with_ones(vt[hh * 2 * DIFF_HD:(hh + 1) * 2 * DIFF_HD, :]) for hh in range(2)]
            return [halves[c // 2] for c in range(n)]

        sts = _causal_stream(qi, scores, vtile, st_refs, s_a, s_b, p_a, p_b)
        outs = []
        for hh in range(2):
            o = _flash_out(sts[2 * hh]) - lam * _flash_out(sts[2 * hh + 1])
            outs.append(o * lax.rsqrt(jnp.mean(o * o, axis=0, keepdims=True) + EPS))
        out = jnp.concatenate(outs, axis=0) * sub_ref[...] * (1.0 - LAMBDA_INIT)
        o_ref[pr] = out.T.astype(BF16)
        return carry
    lax.fori_loop(0, DIFF_HEADS // 2, pair_body, 0)


def _diff(lq1, lk1, lq2, lk2, sub, qb, kb, vbt, near_b, B, S):
    nq = S // TQ
    T = B * S
    full = lambda a: pl.BlockSpec(a.shape, lambda b, q: (0,) * a.ndim)
    k_spec = pl.BlockSpec((4, S, LANE), lambda b, q: (0, b, 0))
    v_spec = pl.BlockSpec((4, LANE, S), lambda b, q: (0, 0, b))
    qo = pl.BlockSpec((4, TQ, LANE), lambda b, q: (0, b * nq + q, 0))
    return pl.pallas_call(
        _diff_kernel,
        grid=(B, nq),
        in_specs=[full(lq1), full(lk1), full(lq2), full(lk2), full(sub),
                  pl.BlockSpec((4, LANE, TQ), lambda b, q: (0, 0, b * nq + q)), k_spec, v_spec, full(near_b)],
        out_specs=qo,
        out_shape=jax.ShapeDtypeStruct((4, T, LANE), BF16),
        scratch_shapes=_stream_scratch(4),
        compiler_params=_cparams(("parallel", "arbitrary")),
        name="diff",
    )(lq1, lk1, lq2, lk2, sub, qb, kb, vbt, near_b)


def _outproj_kernel(x_ref, oa_ref, ob_ref, w_ref, g_ref, wr_ref, br_ref, h_ref, xt_ref, route_ref, cnt_ref,
                    carry_ref):
    @pl.when(pl.program_id(0) == 0)
    def _():
        carry_ref[...] = jnp.zeros(carry_ref.shape, F32)

    o = jnp.concatenate([oa_ref[r] for r in range(4)] + [ob_ref[r] for r in range(4)], axis=-1)
    h = x_ref[...] + _dot(o, w_ref[...])
    h_ref[...] = h
    tn32 = h * lax.rsqrt(jnp.mean(h * h, axis=-1, keepdims=True) + EPS) * g_ref[...]
    xt_ref[:, 0:D_MODEL] = tn32
    tn = tn32.astype(BF16)
    logits = _dot(tn, wr_ref[...]) + br_ref[...]
    lane = lax.broadcasted_iota(jnp.int32, (1, LANE), 1)
    lane_f = lane.astype(F32)
    is_grp = lane < MOE_GROUPS
    lg = jnp.where(is_grp, logits, NEG)
    mg = jnp.max(lg, axis=-1, keepdims=True)
    zg = jnp.sum(jnp.where(is_grp, jnp.exp(lg - mg), 0.0), axis=-1, keepdims=True)
    g_prob = 1.0 / zg
    g_idx = jnp.min(jnp.where(lg == mg, lane_f, 1e9), axis=-1, keepdims=True)
    lane_grp = jnp.where((lane >= MOE_GROUPS) & (lane < MOE_GROUPS + N_EXPERTS),
                         lax.shift_right_arithmetic(lane - MOE_GROUPS, 3), -1).astype(F32)
    le = jnp.where(lane_grp == g_idx, logits, NEG)
    m1 = jnp.max(le, axis=-1, keepdims=True)
    e1 = jnp.min(jnp.where(le == m1, lane_f, 1e9), axis=-1, keepdims=True)
    le2 = jnp.where(lane_f == e1, NEG, le)
    m2 = jnp.max(le2, axis=-1, keepdims=True)
    e2 = jnp.min(jnp.where(le2 == m2, lane_f, 1e9), axis=-1, keepdims=True)
    ratio = jnp.exp(m2 - m1)
    w1 = g_prob / (1.0 + ratio)
    w2 = w1 * ratio
    xt_ref[:, D_MODEL:D_MODEL + LANE] = jnp.where(lane_f == e1, w1, 0.0) + jnp.where(lane_f == e2, w2, 0.0)

    tm = h.shape[0]
    onehot = jnp.where(lane_f == g_idx, 1.0, 0.0)
    earlier = jnp.where(lax.broadcasted_iota(jnp.int32, (tm, tm), 0) > lax.broadcasted_iota(jnp.int32, (tm, tm), 1),
                        1.0, 0.0).astype(BF16)
    prefix = _dot(earlier, onehot.astype(BF16)) + carry_ref[...]
    rank = jnp.sum(onehot * prefix, axis=-1, keepdims=True)
    carry_ref[...] += jnp.sum(onehot, axis=0, keepdims=True)
    route_ref[...] = jnp.where(lane == 0, g_idx, jnp.where(lane == 1, rank, 0.0))
    cnt_ref[...] = jnp.broadcast_to(carry_ref[...], cnt_ref.shape)


def _out_proj(x2, oa, ob, w, g, wr, br):
    T = x2.shape[0]
    tm = TM_PROJ
    row = lambda i: (i, 0)
    full = lambda a: pl.BlockSpec(a.shape, lambda i: (0,) * a.ndim)
    o4 = pl.BlockSpec((4, tm, LANE), lambda i: (0, i, 0))
    return pl.pallas_call(
        _outproj_kernel,
        grid=(T // tm,),
        in_specs=[pl.BlockSpec((tm, D_MODEL), row), o4, o4, full(w), full(g), full(wr), full(br)],
        out_specs=[pl.BlockSpec((tm, D_MODEL), row), pl.BlockSpec((tm, XT_WIDTH), row), pl.BlockSpec((tm, LANE), row),
                   pl.BlockSpec((8, LANE), lambda i: (0, 0))],
        out_shape=[jax.ShapeDtypeStruct((T, D_MODEL), F32), jax.ShapeDtypeStruct((T, XT_WIDTH), F32),
                   jax.ShapeDtypeStruct((T, LANE), F32), jax.ShapeDtypeStruct((8, LANE), F32)],
        scratch_shapes=[pltpu.VMEM((1, LANE), F32)],
        compiler_params=_cparams(("arbitrary",)),
        name="out_proj",
    )(x2, oa, ob, w, g, wr, br)


def _moe_kernel(tg_ref, ok_ref, idx_ref, idxn_ref, xt_hbm, wg_ref, wu_ref, wd_ref, y_ref, xbuf, sem):
    i = pl.program_id(0)
    slot = i % 2
    tm = xbuf.shape[1]

    def row_copy(index_ref, r, dst_slot):
        return pltpu.make_async_copy(xt_hbm.at[pl.ds(index_ref[0, 0, r], 1), :],
                                     xbuf.at[dst_slot, pl.ds(r, 1), :], sem.at[dst_slot])

    @pl.when(i == 0)
    def _():
        def body(r, c):
            row_copy(idx_ref, r, 0).start()
            return c
        lax.fori_loop(0, tm, body, 0)

    @pl.when((i == 0) | (ok_ref[jnp.maximum(i - 1, 0)] == 1))
    def _():
        pltpu.make_async_copy(xt_hbm.at[pl.ds(0, tm), :], xbuf.at[slot], sem.at[slot]).wait()

    @pl.when(ok_ref[i] == 0)
    def _():
        y_ref[...] = jnp.zeros(y_ref.shape, F32)

    @pl.when(ok_ref[i] == 1)
    def _():
        lane = lax.broadcasted_iota(jnp.int32, (1, LANE), 1)
        x = xbuf[slot, :, 0:D_MODEL].astype(BF16)
        cmb = xbuf[slot, :, D_MODEL:D_MODEL + LANE]
        first_lane = MOE_GROUPS + tg_ref[i] * EPG
        per_expert = tm // EPG
        y = jnp.zeros((tm, D_MODEL), F32)
        for e in range(EPG):
            a = _dot(x, wg_ref[0, e])
            b = _dot(x, wu_ref[0, e])
            ce = jnp.sum(jnp.where(lane == first_lane + e, cmb, 0.0), axis=-1, keepdims=True)
            y = y + _dot(((a * jax.nn.sigmoid(a)) * b * ce).astype(BF16), wd_ref[0, e])
            for r in range(e * per_expert, (e + 1) * per_expert):
                row_copy(idxn_ref, r, 1 - slot).start()
        for blk in range(ROW_SLABS):
            y_ref[:, blk, :] = y[:, blk * LANE:(blk + 1) * LANE]


def _moe(tile_group, tile_ok, src3, xt, wg, wu, wd):
    n_tiles, _, tm = src3.shape
    last = n_tiles - 1
    grid_spec = pltpu.PrefetchScalarGridSpec(
        num_scalar_prefetch=2,
        grid=(n_tiles,),
        in_specs=[pl.BlockSpec((1, 1, tm), lambda i, tg, ok: (i, 0, 0), memory_space=pltpu.SMEM),
                  pl.BlockSpec((1, 1, tm), lambda i, tg, ok: (jnp.minimum(i + 1, last), 0, 0), memory_space=pltpu.SMEM),
                  pl.BlockSpec(memory_space=pl.ANY),
                  pl.BlockSpec((1, EPG, D_MODEL, EXPERT_FF), lambda i, tg, ok: (tg[i], 0, 0, 0)),
                  pl.BlockSpec((1, EPG, D_MODEL, EXPERT_FF), lambda i, tg, ok: (tg[i], 0, 0, 0)),
                  pl.BlockSpec((1, EPG, EXPERT_FF, D_MODEL), lambda i, tg, ok: (tg[i], 0, 0, 0))],
        out_specs=pl.BlockSpec((tm, ROW_SLABS, LANE), lambda i, tg, ok: (i, 0, 0)),
        scratch_shapes=[pltpu.VMEM((2, tm, XT_WIDTH), F32), pltpu.SemaphoreType.DMA((2,))],
    )
    return pl.pallas_call(
        _moe_kernel,
        grid_spec=grid_spec,
        out_shape=jax.ShapeDtypeStruct((n_tiles * tm, ROW_SLABS, LANE), F32),
        compiler_params=pltpu.CompilerParams(dimension_semantics=("arbitrary",), vmem_limit_bytes=VMEM_LIMIT_MOE),
        name="moe",
    )(tile_group, tile_ok, src3, src3, xt, wg, wu, wd)


def _final_kernel(pos_ref, posn_ref, y_hbm, h_ref, gf_ref, o_ref, ybuf, sem):
    i = pl.program_id(0)
    slot = i % 2
    tm = ybuf.shape[1]

    def issue(index_ref, dst_slot):
        def body(r, c):
            pltpu.make_async_copy(y_hbm.at[index_ref[0, 0, r]], ybuf.at[dst_slot, r], sem.at[dst_slot]).start()
            return c
        lax.fori_loop(0, tm, body, 0, unroll=8)

    @pl.when(i == 0)
    def _():
        issue(pos_ref, 0)

    @pl.when(i + 1 < pl.num_programs(0))
    def _():
        issue(posn_ref, 1 - slot)

    pltpu.make_async_copy(y_hbm.at[pl.ds(0, tm)], ybuf.at[slot], sem.at[slot]).wait()
    y = jnp.concatenate([ybuf[slot, :, blk, :] for blk in range(ROW_SLABS)], axis=-1)
    h = h_ref[...] + y
    o_ref[...] = h * lax.rsqrt(jnp.mean(h * h, axis=-1, keepdims=True) + EPS) * gf_ref[...]


def _final(pos3, y_sorted, h, gf):
    n_tiles, _, tm = pos3.shape
    last = n_tiles - 1
    row = lambda i: (i, 0)
    return pl.pallas_call(
        _final_kernel,
        grid=(n_tiles,),
        in_specs=[pl.BlockSpec((1, 1, tm), lambda i: (i, 0, 0), memory_space=pltpu.SMEM),
                  pl.BlockSpec((1, 1, tm), lambda i: (jnp.minimum(i + 1, last), 0, 0), memory_space=pltpu.SMEM),
                  pl.BlockSpec(memory_space=pl.ANY),
                  pl.BlockSpec((tm, D_MODEL), row),
                  pl.BlockSpec((1, D_MODEL), lambda i: (0, 0))],
        out_specs=pl.BlockSpec((tm, D_MODEL), row),
        out_shape=jax.ShapeDtypeStruct((n_tiles * tm, D_MODEL), F32),
        scratch_shapes=[pltpu.VMEM((2, tm, ROW_SLABS, LANE), F32), pltpu.SemaphoreType.DMA((2,))],
        compiler_params=_cparams(("arbitrary",)),
        name="final",
    )(pos3, pos3, y_sorted, h, gf)


def _qa_perm():
    new = np.arange(NSA_HEADS * NSA_HD)
    r, g, d = new // LANE, (new % LANE) // NSA_HD, new % NSA_HD
    return (g * NSA_R + r) * NSA_HD + d


def _block_diag2(w):
    z = jnp.zeros_like(w)
    return jnp.concatenate([jnp.concatenate([w, z], axis=-1), jnp.concatenate([z, w], axis=-1)], axis=-2)


def kernel(x, rel_bias, ln_mix, w_in, cmp_pos_k, cmp_pos_v, cmp_k_w1, cmp_k_w2, cmp_v_w1, cmp_v_w2,
           diff_lq1, diff_lk1, diff_lq2, diff_lk2, diff_subln, w_out, ln_ffn,
           router_group_w, router_group_b, router_expert_w, router_expert_b,
           exp_w_gate, exp_w_up, exp_w_down, ln_final):
    B, S, D = x.shape
    T = B * S
    assert D == D_MODEL and S % TQ == 0 and S >= WINDOW and T % TM_MOE == 0 and T % TM_FINAL == 0
    x2 = x.reshape(T, D)
    perm = _qa_perm()

    w = w_in[0]
    c_kc, c_vc, c_ks, c_vs, c_kw, c_vw, c_gt = 512, 640, 768, 896, 1024, 1152, 1280
    c_qb = c_gt + N_GATE
    c_kb, c_vb = c_qb + 512, c_qb + 1024
    col = lambda c, n=LANE: w[:, c:c + n]
    w_tok = jnp.concatenate([col(c_kc), col(c_vc), col(c_ks), col(c_kw), col(c_kb, 512)], axis=1).astype(BF16)
    w_feat = jnp.concatenate([w[:, perm], col(c_qb, 512), col(c_vs), col(c_vw), col(c_vb, 512), col(c_gt, N_GATE),
                              jnp.zeros((D, GATE_ROWS - N_GATE), F32)], axis=1).T.astype(BF16)
    qa, kc, vc, ks, kw, qb, kb, vst, vwt, vbt, gates_t = _in_proj(x2, ln_mix[0][None, :], w_tok, w_feat)

    w1k = _block_diag2(cmp_k_w1[0].reshape(CMP_LEN, NSA_HD, CMP_HIDDEN)).astype(BF16)
    w1v = _block_diag2(cmp_v_w1[0].reshape(CMP_LEN, NSA_HD, CMP_HIDDEN)).astype(BF16)
    w2k = _block_diag2(cmp_k_w2[0]).astype(BF16)
    w2v = _block_diag2(cmp_v_w2[0]).astype(BF16)
    posk = jnp.tile(cmp_pos_k[0], (1, NSA_G))
    posv = jnp.tile(cmp_pos_v[0], (1, NSA_G))
    kcmp, vcmpt = _compress(kc, vc, posk, posv, w1k, w1v, w2k, w2v, B, S)

    near, bias_c = _bias_tiles(rel_bias.T, S)

    n_sel = S // SEL_BLOCK
    nrow = S // CMP_STRIDE
    c_start = np.arange(nrow) * CMP_STRIDE
    s_start = np.arange(n_sel) * SEL_BLOCK
    ov = ((c_start[None, :] <= s_start[:, None] + SEL_BLOCK - 1)
          & (c_start[None, :] + CMP_LEN - 1 >= s_start[:, None])
          & (np.arange(nrow)[None, :] < nrow - 1)).astype(np.float32)
    o_a = _nsa(qa, gates_t, kcmp, vcmpt, bias_c, ks, vst, kw, vwt, near[:NSA_HEADS], jnp.asarray(ov, BF16), B, S)

    sub = jnp.tile(diff_subln[0], 2)[:, None]
    o_b = _diff(diff_lq1[0][None, :], diff_lk1[0][None, :], diff_lq2[0][None, :], diff_lk2[0][None, :],
                sub, qb, kb, vbt, near[NSA_HEADS:], B, S)

    w_o = jnp.concatenate([w_out[0][:512][perm], w_out[0][512:]], axis=0).astype(BF16)
    n_r = MOE_GROUPS + N_EXPERTS
    wr = jnp.concatenate([router_group_w[0], router_expert_w[0], jnp.zeros((D, LANE - n_r), F32)], axis=1).astype(BF16)
    br = jnp.concatenate([router_group_b[0], router_expert_b[0], jnp.zeros((LANE - n_r,), F32)])[None, :]
    h1, xt, route, counts = _out_proj(x2, o_a, o_b, w_o, ln_ffn[0][None, :], wr, br)

    tm = TM_MOE
    n_tiles = T // tm + MOE_GROUPS
    cnt = counts[0, :MOE_GROUPS].astype(jnp.int32)
    ends = jnp.cumsum((cnt + tm - 1) // tm * tm)
    starts = ends - (cnt + tm - 1) // tm * tm
    pos = starts[route[:, 0].astype(jnp.int32)] + route[:, 1].astype(jnp.int32)
    src = jnp.zeros((n_tiles * tm,), jnp.int32).at[pos].set(jnp.arange(T, dtype=jnp.int32))
    tile_start = jnp.arange(n_tiles, dtype=jnp.int32) * tm
    tile_group = jnp.minimum(jnp.searchsorted(ends, tile_start, side="right"), MOE_GROUPS - 1).astype(jnp.int32)
    tile_ok = (tile_start < ends[-1]).astype(jnp.int32)

    by_group = lambda a: a[0].astype(BF16).reshape((MOE_GROUPS, EPG) + a.shape[2:])
    y_sorted = _moe(tile_group, tile_ok, src.reshape(n_tiles, 1, tm), xt,
                    by_group(exp_w_gate), by_group(exp_w_up), by_group(exp_w_down))
    out = _final(pos.reshape(T // TM_FINAL, 1, TM_FINAL), y_sorted, h1, ln_final[None, :])
    return out.reshape(B, S, D)
```

```python
import math

import numpy as np
import jax
import jax.numpy as jnp
from jax import lax
from jax.experimental import pallas as pl
from jax.experimental.pallas import tpu as pltpu

F32 = jnp.float32
BF16 = jnp.bfloat16
NEG = -1e30
EPS = 1e-6
LOG2E = math.log2(math.e)

D_MODEL = 1024
LANE = 128
NSA_HEADS, NSA_G, NSA_R, NSA_HD = 8, 2, 4, 64
CMP_LEN, CMP_STRIDE, CMP_HIDDEN = 32, 16, 128
SEL_BLOCK, SEL_TOPN, SEL_FORCED_LOCAL, WINDOW = 64, 8, 2, 512
DIFF_HEADS, DIFF_HD = 8, 32
REL_BUCKETS, REL_MAX_EXACT, REL_MAX_DIST = 32, 16, 128
N_REL_HEADS = NSA_HEADS + DIFF_HEADS
MOE_GROUPS, EPG, N_EXPERTS, EXPERT_FF = 4, 8, 32, 256
LAMBDA_INIT = 0.8 - 0.6 * math.exp(-0.3 * 0)
N_GATE = NSA_HEADS * 3
GATE_ROWS = 32

TQ = 256
TM_PROJ = 512
TM_MOE = 512
TM_FINAL = 256
XT_WIDTH = D_MODEL + LANE
ROW_SLABS = D_MODEL // LANE
VMEM_LIMIT = 48 * 1024 * 1024
VMEM_LIMIT_MOE = 56 * 1024 * 1024


def _cparams(sem):
    return pltpu.CompilerParams(dimension_semantics=sem, vmem_limit_bytes=VMEM_LIMIT)


def _dot(a, b):
    return jnp.dot(a, b, preferred_element_type=F32)


def _dot_nt(a, b):
    return lax.dot_general(a, b, (((1,), (1,)), ((), ())), preferred_element_type=F32)


def _bucket_thresholds():
    n = np.arange(0, REL_MAX_DIST + 1)
    nf = np.maximum(n, 1).astype(np.float32)
    large = REL_MAX_EXACT + (np.log(nf / np.float32(REL_MAX_EXACT)) / np.float32(math.log(REL_MAX_DIST / REL_MAX_EXACT))
                             * np.float32(REL_BUCKETS - REL_MAX_EXACT)).astype(np.int32)
    large = np.minimum(large, REL_BUCKETS - 1)
    bucket = np.where(n < REL_MAX_EXACT, n, large)
    return [int(np.argmax(bucket >= b)) for b in range(REL_BUCKETS)]


_THR = _bucket_thresholds()


def _inproj_kernel(x_ref, g_ref, w_ref, wt_ref, qa_ref, kc_ref, vc_ref, ks_ref, kw_ref, qb_ref, kb_ref,
                   vst_ref, vwt_ref, vbt_ref, gt_ref):
    x = x_ref[...]
    xn = (x * lax.rsqrt(jnp.mean(x * x, axis=-1, keepdims=True) + EPS) * g_ref[...]).astype(BF16)
    a = _dot(xn, w_ref[:, 0:512])
    kc_ref[...] = a[:, 0:128]
    vc_ref[...] = a[:, 128:256]
    ks_ref[...] = a[:, 256:384].astype(BF16)
    kw_ref[...] = a[:, 384:512].astype(BF16)
    a = _dot(xn, w_ref[:, 512:1024])
    for r in range(4):
        kb_ref[r] = a[:, r * LANE:(r + 1) * LANE].astype(BF16)

    ft = _dot_nt(wt_ref[...], xn)

    def feat(row0, rows=LANE):
        return ft[row0:row0 + rows, :]
    for r in range(4):
        qa_ref[r] = (feat(r * LANE) * (NSA_HD ** -0.5 * LOG2E)).astype(BF16)
        qb_ref[r] = (feat(512 + r * LANE) * (DIFF_HD ** -0.5 * LOG2E)).astype(BF16)
        vbt_ref[r] = feat(1280 + r * LANE).astype(BF16)
    vst_ref[...] = feat(1024).astype(BF16)
    vwt_ref[...] = feat(1152).astype(BF16)
    gt_ref[...] = feat(1792, GATE_ROWS)


def _in_proj(x2, g, w, wt):
    T = x2.shape[0]
    tm = TM_PROJ
    row = lambda i: (i, 0)
    o128b = jax.ShapeDtypeStruct((T, LANE), BF16)
    o128f = jax.ShapeDtypeStruct((T, LANE), F32)
    o4 = jax.ShapeDtypeStruct((4, T, LANE), BF16)
    ot = jax.ShapeDtypeStruct((LANE, T), BF16)
    o4t = jax.ShapeDtypeStruct((4, LANE, T), BF16)
    s128 = pl.BlockSpec((tm, LANE), row)
    s4 = pl.BlockSpec((4, tm, LANE), lambda i: (0, i, 0))
    st = pl.BlockSpec((LANE, tm), lambda i: (0, i))
    s4t = pl.BlockSpec((4, LANE, tm), lambda i: (0, 0, i))
    return pl.pallas_call(
        _inproj_kernel,
        grid=(T // tm,),
        in_specs=[pl.BlockSpec((tm, D_MODEL), row),
                  pl.BlockSpec((1, D_MODEL), lambda i: (0, 0)),
                  pl.BlockSpec(w.shape, lambda i: (0, 0)),
                  pl.BlockSpec(wt.shape, lambda i: (0, 0))],
        out_specs=[s4t, s128, s128, s128, s128, s4t, s4, st, st, s4t, pl.BlockSpec((GATE_ROWS, tm), lambda i: (0, i))],
        out_shape=[o4t, o128f, o128f, o128b, o128b, o4t, o4, ot, ot, o4t, jax.ShapeDtypeStruct((GATE_ROWS, T), F32)],
        compiler_params=_cparams(("parallel",)),
        name="in_proj",
    )(x2, g, w, wt)


def _gelu_tanh(x):
    return 0.5 * x * (1.0 + jnp.tanh(math.sqrt(2.0 / math.pi) * (x + 0.044715 * (x * x * x))))


def _compress_kernel(kc_ref, vc_ref, posk_ref, posv_ref, w1k_ref, w1v_ref, w2k_ref, w2v_ref, ko_ref, vo_ref):
    nrow = kc_ref.shape[0] // CMP_STRIDE
    rid = lax.broadcasted_iota(jnp.int32, (nrow, 1), 0)
    cid = lax.broadcasted_iota(jnp.int32, (1, nrow), 1)
    for src, pos, w1, w2, out, transposed in ((kc_ref, posk_ref, w1k_ref, w2k_ref, ko_ref, False),
                                              (vc_ref, posv_ref, w1v_ref, w2v_ref, vo_ref, True)):
        hid_a = jnp.zeros((nrow, 2 * CMP_HIDDEN), F32)
        hid_b = jnp.zeros((nrow, 2 * CMP_HIDDEN), F32)
        for m in range(CMP_STRIDE):
            y = src[pl.ds(m, nrow, stride=CMP_STRIDE), :]
            hid_a = hid_a + _dot((y + pos[m:m + 1, :]).astype(BF16), w1[m])
            hid_b = hid_b + _dot((y + pos[CMP_STRIDE + m:CMP_STRIDE + m + 1, :]).astype(BF16), w1[CMP_STRIDE + m])
        hid = hid_a + pltpu.roll(hid_b, nrow - 1, 0)
        o = _dot(_gelu_tanh(hid).astype(BF16), w2[...])
        if transposed:
            out[0] = jnp.where(cid < nrow - 1, o.T, 0.0).astype(BF16)
        else:
            out[0] = jnp.where(rid < nrow - 1, o, 0.0).astype(BF16)


def _compress(kc, vc, posk, posv, w1k, w1v, w2k, w2v, B, S):
    nrow = S // CMP_STRIDE
    assert nrow == LANE
    full = lambda a: pl.BlockSpec(a.shape, lambda b: (0,) * a.ndim)
    src = pl.BlockSpec((S, LANE), lambda b: (b, 0))
    osp = pl.BlockSpec((1, nrow, LANE), lambda b: (b, 0, 0))
    osh = jax.ShapeDtypeStruct((B, nrow, LANE), BF16)
    return pl.pallas_call(
        _compress_kernel,
        grid=(B,),
        in_specs=[src, src, full(posk), full(posv), full(w1k), full(w1v), full(w2k), full(w2v)],
        out_specs=[osp, osp],
        out_shape=[osh, osh],
        compiler_params=_cparams(("parallel",)),
        name="compress",
    )(kc, vc, posk, posv, w1k, w1v, w2k, w2v)


def _bias_from_dist(dist, tab_ref, h):
    val = jnp.full(dist.shape, tab_ref[h, 0], F32)
    for b in range(1, REL_BUCKETS):
        val = jnp.where(dist >= _THR[b], tab_ref[h, b], val)
    return val


def _bias_near_kernel(tab_ref, out_ref):
    h = pl.program_id(0)
    j = lax.broadcasted_iota(jnp.int32, (TQ, TQ), 0)
    i = lax.broadcasted_iota(jnp.int32, (TQ, TQ), 1)
    for d in range(2):
        bias = (_bias_from_dist(i - j + d * TQ, tab_ref, h) - tab_ref[h, REL_BUCKETS - 1]) * LOG2E
        out_ref[0, d] = jnp.where(i - j + d * TQ >= 0, bias, NEG)


def _bias_cmp_kernel(tab_ref, out_ref):
    h = pl.program_id(0)
    c = lax.broadcasted_iota(jnp.int32, (LANE, TQ), 0)
    t = pl.program_id(1) * TQ + lax.broadcasted_iota(jnp.int32, (LANE, TQ), 1)
    out_ref[0] = _bias_from_dist(t - (c * CMP_STRIDE + CMP_LEN - 1), tab_ref, h) * LOG2E


def _bias_tiles(tab_t, S):
    smem = pl.BlockSpec(memory_space=pltpu.SMEM)
    near = pl.pallas_call(
        _bias_near_kernel,
        grid=(N_REL_HEADS,),
        in_specs=[smem],
        out_specs=pl.BlockSpec((1, 2, TQ, TQ), lambda h: (h, 0, 0, 0)),
        out_shape=jax.ShapeDtypeStruct((N_REL_HEADS, 2, TQ, TQ), F32),
        compiler_params=_cparams(("parallel",)),
        name="bias_near",
    )(tab_t)
    cmp_bias = pl.pallas_call(
        _bias_cmp_kernel,
        grid=(NSA_HEADS, S // TQ),
        in_specs=[smem],
        out_specs=pl.BlockSpec((1, LANE, TQ), lambda h, q: (h, 0, q)),
        out_shape=jax.ShapeDtypeStruct((NSA_HEADS, LANE, S), F32),
        compiler_params=_cparams(("parallel", "parallel")),
        name="bias_cmp",
    )(tab_t)
    return near, cmp_bias


ACC_ROWS = NSA_HD + 16


def _with_ones(vt):
    return jnp.concatenate([vt, jnp.ones((ACC_ROWS - vt.shape[0], vt.shape[1]), BF16)], axis=0)


def _probs(s, m):
    return jnp.exp2((s - m).astype(BF16))


def _flash_first(ss, vts):
    ms = [jnp.max(s, axis=0, keepdims=True) for s in ss]
    ps = [_probs(s, m) for s, m in zip(ss, ms)]
    return tuple((m, _dot(vt, p)) for m, p, vt in zip(ms, ps, vts))


def _flash_update(ss, vts, sts):
    ms = [jnp.maximum(st[0], jnp.max(s, axis=0, keepdims=True)) for s, st in zip(ss, sts)]
    alphas = [jnp.exp2(st[0] - m) for m, st in zip(ms, sts)]
    ps = [_probs(s, m) for s, m in zip(ss, ms)]
    return tuple((m, a * st[1] + _dot(vt, p)) for m, a, st, p, vt in zip(ms, alphas, sts, ps, vts))


def _flash_out(st):
    acc = st[1]
    return acc[0:NSA_HD, :] * (1.0 / acc[NSA_HD:NSA_HD + 1, :])


def _stream_scratch(n):
    return ([pltpu.VMEM((8, TQ), F32), pltpu.VMEM((ACC_ROWS, TQ), F32)] * n
            + [pltpu.VMEM((TQ, TQ), F32)] * (2 * n) + [pltpu.VMEM((TQ, TQ), BF16)] * (2 * n))


def _stream_refs(refs, n):
    st_refs = tuple((refs[2 * c], refs[2 * c + 1]) for c in range(n))
    s, p = refs[2 * n:4 * n], refs[4 * n:6 * n]
    return st_refs, tuple(s[:n]), tuple(s[n:]), tuple(p[:n]), tuple(p[n:])


def _causal_stream(qi, scores, vtile, st_refs, s_a, s_b, p_a, p_b, diag_ready=False):
    n_far = jnp.maximum(qi - 1, 0)
    top = qi - 2

    chains = range(len(st_refs))

    def write(s_refs, kt, near=2):
        for c in chains:
            s_refs[c][...] = scores(jnp.maximum(kt, 0), near, c)

    def pending(p_refs, kt):
        return [_dot(vtile(jnp.maximum(kt, 0), c), p_refs[c][...]) for c in chains]

    def step(s_cur, p_cur, s_next, kt_next, p_prev, kt_prev):
        rescale = []
        for c in chains:
            s_next[c][...] = scores(jnp.maximum(kt_next, 0), 2, c)
            pv = _dot(vtile(jnp.maximum(kt_prev, 0), c), p_prev[c][...])
            st = st_refs[c][0]
            s = s_cur[c][...]
            m_old = st[0:1, :]
            m = jnp.maximum(m_old, jnp.max(s, axis=0, keepdims=True))
            p_cur[c][...] = _probs(s, m)
            st[0:1, :] = m
            rescale.append((jnp.exp2(m_old - m), pv))
        for c in chains:
            acc = st_refs[c][1]
            acc[...] = rescale[c][0] * (acc[...] + rescale[c][1])

    if not diag_ready:
        write(s_a, qi, 0)
    for c in chains:
        s_b[c][...] = scores(jnp.maximum(qi - 1, 0), 1, c)
        st, acc = st_refs[c]
        s = s_a[c][...]
        m = jnp.max(s, axis=0, keepdims=True)
        p_a[c][...] = _probs(s, m)
        st[0:1, :] = m
        acc[...] = jnp.zeros(acc.shape, F32)
    step(s_b, p_b, s_a, top, p_a, qi)

    def pair(j, c):
        kt = top - 2 * j
        step(s_a, p_a, s_b, kt - 1, p_b, kt + 1)
        step(s_b, p_b, s_a, kt - 2, p_a, kt)
        return c
    lax.fori_loop(0, n_far // 2, pair, 0)

    @pl.when(n_far % 2 == 1)
    def _():
        for c, pv in zip(chains, pending(p_b, 1)):
            st, acc = st_refs[c]
            s = s_a[c][...]
            m_old = st[0:1, :]
            m = jnp.maximum(m_old, jnp.max(s, axis=0, keepdims=True))
            st[0:1, :] = m
            acc[...] = jnp.exp2(m_old - m) * (acc[...] + pv) + _dot(vtile(0, c), _probs(s, m))

    @pl.when(n_far % 2 == 0)
    def _():
        last = jnp.where(n_far == 0, qi - 1, 0)
        for c, pv in zip(chains, pending(p_b, last)):
            acc = st_refs[c][1]
            acc[...] = acc[...] + pv
    return tuple((st[0:1, :], acc[...]) for st, acc in st_refs)


def _ktile(ref, kt):
    return ref[pl.ds(pl.multiple_of(kt * TQ, TQ), TQ), :]


def _vtile(ref, kt):
    return ref[:, pl.ds(pl.multiple_of(kt * TQ, TQ), TQ)]


def _nsa_kernel(qa_ref, gt_ref, kcmp_ref, vcmpt_ref, bc_ref, ks_ref, vst_ref, kw_ref, vwt_ref, dn_ref,
                ov_ref, o_ref, psum_ref, sel_ref, oacc_ref, sig_ref, *scratch):
    st_refs, s_a, s_b, p_a, p_b = _stream_refs(scratch, 2 * NSA_G)
    qi = pl.program_id(1)
    t0 = qi * TQ
    sub_grp = lax.shift_right_arithmetic(lax.broadcasted_iota(jnp.int32, (LANE, 1), 0), 6)
    sig_ref[...] = jax.nn.sigmoid(gt_ref[...])

    def gate_row(c):
        return sig_ref[pl.ds(c, 1), :]

    def masked_q(r, g):
        return jnp.where(sub_grp == g, qa_ref[r].astype(F32), 0.0).astype(BF16)

    n_cmp = kcmp_ref.shape[1] - 1
    crow = lax.broadcasted_iota(jnp.int32, (LANE, 1), 0)
    cmp_end = jnp.where(crow < n_cmp, crow * CMP_STRIDE + CMP_LEN - 1, 1 << 30)
    mask_c = (t0 + lax.broadcasted_iota(jnp.int32, (1, TQ), 1)) >= cmp_end

    heads = [(r, g) for r in range(NSA_R) for g in range(NSA_G)]
    scores_c = [_dot(kcmp_ref[0], masked_q(r, g)) for r, g in heads]
    probs_c = []
    for (r, g), s in zip(heads, scores_c):
        s = jnp.where(mask_c, s + bc_ref[g * NSA_R + r], NEG)
        p = jnp.where(mask_c, jnp.exp2(s - jnp.max(s, axis=0, keepdims=True)), 0.0)
        l = jnp.sum(p, axis=0, keepdims=True)
        probs_c.append(p * jnp.where(l > 0.0, 1.0 / l, 0.0))
    for g in range(NSA_G):
        psum_ref[g] = sum(p for (r, gg), p in zip(heads, probs_c) if gg == g)
    outs_c = [_dot(vcmpt_ref[0], p.astype(BF16)) for p in probs_c]
    for r in range(NSA_R):
        o0, o1 = outs_c[r * NSA_G], outs_c[r * NSA_G + 1]
        oacc_ref[r] = jnp.where(sub_grp == 0, gate_row(r * 3) * o0, gate_row((NSA_R + r) * 3) * o1)

    n_sel = ov_ref.shape[0]
    jj = lax.broadcasted_iota(jnp.int32, (n_sel, TQ), 0)
    cur = lax.shift_right_arithmetic(t0 + lax.broadcasted_iota(jnp.int32, (n_sel, TQ), 1), 6)
    valid = jj <= cur
    forced = (jj == 0) | (cur - jj < SEL_FORCED_LOCAL)
    for g in range(NSA_G):
        ps = psum_ref[g]
        hi = ps.astype(BF16)
        rem = ps - hi.astype(F32)
        mid = rem.astype(BF16)
        lo = (rem - mid.astype(F32)).astype(BF16)
        ov = ov_ref[...]
        imp = _dot(ov, hi) + _dot(ov, mid) + _dot(ov, lo)
        score = jnp.where(valid, jnp.where(forced, 1e9, imp), -1e9)
        cnt = jnp.zeros((n_sel, TQ), F32)
        for j2 in range(n_sel):
            row = score[j2:j2 + 1, :]
            tie = jnp.where(jj > j2, 1.0, 0.0)
            cnt = cnt + jnp.where(row > score, 1.0, jnp.where(row == score, tie, 0.0))
        sel_ref[g] = jnp.where(cnt < float(min(SEL_TOPN, n_sel)), jnp.where(score > -1e8, 0.0, NEG), NEG)

    has_prev = jnp.where(qi >= 1, 0.0, NEG)
    has_wfar = jnp.where(qi >= WINDOW // TQ, 0.0, NEG)
    kt_prev = jnp.maximum(qi - 1, 0)
    kt_wfar = jnp.maximum(qi - WINDOW // TQ, 0)
    wfar_mask = lax.broadcasted_iota(jnp.int32, (TQ, TQ), 0) > lax.broadcasted_iota(jnp.int32, (TQ, TQ), 1)

    def main_body(rp, carry):
        chains = [(2 * rp + i, g) for i in range(2) for g in range(NSA_G)]
        qms = [masked_q(r, g) for r, g in chains]
        hids = [g * NSA_R + r for r, g in chains]

        def block_mask(g, kt):
            per_tile = TQ // SEL_BLOCK
            rows = [jnp.broadcast_to(sel_ref[g, pl.ds(kt * per_tile + i, 1), :], (SEL_BLOCK, TQ))
                    for i in range(per_tile)]
            return jnp.concatenate(rows, axis=0)

        def near_bias(s, near, c):
            if near == 0:
                return s + dn_ref[hids[c], 0]
            if near == 1:
                return s + dn_ref[hids[c], 1] + has_prev
            return s

        def group_values(ref, kt, c):
            g = chains[c][1]
            return _with_ones(ref[g * NSA_HD:(g + 1) * NSA_HD, pl.ds(pl.multiple_of(kt * TQ, TQ), TQ)])

        def finish(sts, branch):
            for c, (r, g) in enumerate(chains):
                oacc_ref[r, g * NSA_HD:(g + 1) * NSA_HD, :] += gate_row(hids[c] * 3 + branch) * _flash_out(sts[c])

        def slc_scores(kt, near, c):
            return near_bias(_dot(_ktile(ks_ref, kt), qms[c]), near, c) + block_mask(chains[c][1], kt)

        def slc_vtile(kt, c):
            return group_values(vst_ref, kt, c)

        finish(_causal_stream(qi, slc_scores, slc_vtile, st_refs, s_a, s_b, p_a, p_b), 1)

        cs = range(len(chains))
        win = []
        for kt, near in ((qi, 0), (kt_prev, 1), (kt_wfar, 2)):
            k = _ktile(kw_ref, kt)
            ss = [near_bias(_dot(k, qms[c]), near, c) for c in cs]
            if near == 2:
                ss = [jnp.where(wfar_mask, s + has_wfar, NEG) for s in ss]
            win.append(ss)
        sts = _flash_first(win[0], [group_values(vwt_ref, qi, c) for c in cs])
        sts = _flash_update(win[1], [group_values(vwt_ref, kt_prev, c) for c in cs], sts)
        sts = _flash_update(win[2], [group_values(vwt_ref, kt_wfar, c) for c in cs], sts)
        finish(sts, 2)
        return carry
    lax.fori_loop(0, NSA_R // 2, main_body, 0)

    for r in range(NSA_R):
        o_ref[r] = oacc_ref[r].T.astype(BF16)


def _nsa(qa, gates_t, kcmp, vcmpt, bias_c, ks, vst, kw, vwt, near_a, ov, B, S):
    nq = S // TQ
    T = B * S
    k_spec = pl.BlockSpec((S, LANE), lambda b, q: (b, 0))
    v_spec = pl.BlockSpec((LANE, S), lambda b, q: (0, b))
    full = lambda a: pl.BlockSpec(a.shape, lambda b, q: (0,) * a.ndim)
    cmp_spec = pl.BlockSpec((1, LANE, LANE), lambda b, q: (b, 0, 0))
    qo = pl.BlockSpec((4, TQ, LANE), lambda b, q: (0, b * nq + q, 0))
    return pl.pallas_call(
        _nsa_kernel,
        grid=(B, nq),
        in_specs=[pl.BlockSpec((4, LANE, TQ), lambda b, q: (0, 0, b * nq + q)),
                  pl.BlockSpec((GATE_ROWS, TQ), lambda b, q: (0, b * nq + q)),
                  cmp_spec, cmp_spec,
                  pl.BlockSpec((NSA_HEADS, LANE, TQ), lambda b, q: (0, 0, q)),
                  k_spec, v_spec, k_spec, v_spec, full(near_a), full(ov)],
        out_specs=qo,
        out_shape=jax.ShapeDtypeStruct((4, T, LANE), BF16),
        scratch_shapes=[pltpu.VMEM((NSA_G, LANE, TQ), F32), pltpu.VMEM((NSA_G, S // SEL_BLOCK, TQ), F32),
                        pltpu.VMEM((NSA_R, LANE, TQ), F32), pltpu.VMEM((GATE_ROWS, TQ), F32)] + _stream_scratch(2 * NSA_G),
        compiler_params=_cparams(("parallel", "arbitrary")),
        name="nsa",
    )(qa, gates_t, kcmp, vcmpt, bias_c, ks, vst, kw, vwt, near_a, ov)


def _diff_kernel(lq1_ref, lk1_ref, lq2_ref, lk2_ref, sub_ref, qb_ref, kb_ref, vbt_ref, dn_ref, o_ref, *scratch):
    n = 2 * 2
    st_refs, s_a, s_b, p_a, p_b = _stream_refs(scratch, n)
    qi = pl.program_id(1)
    chain_of_row = lax.shift_right_arithmetic(lax.broadcasted_iota(jnp.int32, (LANE, 1), 0), 5)
    lam = (jnp.exp(jnp.sum(lq1_ref[...] * lk1_ref[...], axis=-1, keepdims=True))
           - jnp.exp(jnp.sum(lq2_ref[...] * lk2_ref[...], axis=-1, keepdims=True)) + LAMBDA_INIT)
    has_prev = jnp.where(qi >= 1, 0.0, NEG)

    def score_fn(pr):
        q = qb_ref[pr].astype(F32)
        qms = [jnp.where(chain_of_row == c, q, 0.0).astype(BF16) for c in range(n)]

        def scores(kt, near, c):
            s = _dot(kb_ref[pr, pl.ds(pl.multiple_of(kt * TQ, TQ), TQ), :], qms[c])
            if near == 2:
                return s
            bias = dn_ref[2 * pr + c // 2, near]
            return s + bias if near == 0 else s + bias + has_prev
        return scores

    def start_diag(pr):
        scores = score_fn(pr)
        for c in range(n):
            s_a[c][...] = scores(qi, 0, c)

    start_diag(0)

    def pair_body(pr, carry):
        def vtile(kt, c):
            hh = c // 2
            return _with_ones(vbt_ref[pr, hh * 2 * DIFF_HD:(hh + 1) * 2 * DIFF_HD, pl.ds(pl.multiple_of(kt * TQ, TQ), TQ)])

        sts = _causal_stream(qi, score_fn(pr), vtile, st_refs, s_a, s_b, p_a, p_b, diag_ready=True)
        start_diag(jnp.minimum(pr + 1, DIFF_HEADS // 2 - 1))
        outs = []
        for hh in range(2):
            o = _flash_out(sts[2 * hh]) - lam * _flash_out(sts[2 * hh + 1])
            outs.append(o * lax.rsqrt(jnp.mean(o * o, axis=0, keepdims=True) + EPS))
        out = jnp.concatenate(outs, axis=0) * sub_ref[...] * (1.0 - LAMBDA_INIT)
        o_ref[pr] = out.T.astype(BF16)
        return carry
    lax.fori_loop(0, DIFF_HEADS // 2, pair_body, 0)


def _diff(lq1, lk1, lq2, lk2, sub, qb, kb, vbt, near_b, B, S):
    nq = S // TQ
    T = B * S
    full = lambda a: pl.BlockSpec(a.shape, lambda b, q: (0,) * a.ndim)
    k_spec = pl.BlockSpec((4, S, LANE), lambda b, q: (0, b, 0))
    v_spec = pl.BlockSpec((4, LANE, S), lambda b, q: (0, 0, b))
    qo = pl.BlockSpec((4, TQ, LANE), lambda b, q: (0, b * nq + q, 0))
    return pl.pallas_call(
        _diff_kernel,
        grid=(B, nq),
        in_specs=[full(lq1), full(lk1), full(lq2), full(lk2), full(sub),
                  pl.BlockSpec((4, LANE, TQ), lambda b, q: (0, 0, b * nq + q)), k_spec, v_spec, full(near_b)],
        out_specs=qo,
        out_shape=jax.ShapeDtypeStruct((4, T, LANE), BF16),
        scratch_shapes=_stream_scratch(4),
        compiler_params=_cparams(("parallel", "arbitrary")),
        name="diff",
    )(lq1, lk1, lq2, lk2, sub, qb, kb, vbt, near_b)


def _outproj_kernel(x_ref, oa_ref, ob_ref, w_ref, g_ref, wr_ref, br_ref, h_ref, xt_ref, route_ref, cnt_ref,
                    carry_ref):
    @pl.when(pl.program_id(0) == 0)
    def _():
        carry_ref[...] = jnp.zeros(carry_ref.shape, F32)

    o = jnp.concatenate([oa_ref[r] for r in range(4)] + [ob_ref[r] for r in range(4)], axis=-1)
    h = x_ref[...] + _dot(o, w_ref[...])
    h_ref[...] = h
    tn32 = h * lax.rsqrt(jnp.mean(h * h, axis=-1, keepdims=True) + EPS) * g_ref[...]
    xt_ref[:, 0:D_MODEL] = tn32
    tn = tn32.astype(BF16)
    logits = _dot(tn, wr_ref[...]) + br_ref[...]
    lane = lax.broadcasted_iota(jnp.int32, (1, LANE), 1)
    lane_f = lane.astype(F32)
    is_grp = lane < MOE_GROUPS
    lg = jnp.where(is_grp, logits, NEG)
    mg = jnp.max(lg, axis=-1, keepdims=True)
    zg = jnp.sum(jnp.where(is_grp, jnp.exp(lg - mg), 0.0), axis=-1, keepdims=True)
    g_prob = 1.0 / zg
    g_idx = jnp.min(jnp.where(lg == mg, lane_f, 1e9), axis=-1, keepdims=True)
    lane_grp = jnp.where((lane >= MOE_GROUPS) & (lane < MOE_GROUPS + N_EXPERTS),
                         lax.shift_right_arithmetic(lane - MOE_GROUPS, 3), -1).astype(F32)
    le = jnp.where(lane_grp == g_idx, logits, NEG)
    m1 = jnp.max(le, axis=-1, keepdims=True)
    e1 = jnp.min(jnp.where(le == m1, lane_f, 1e9), axis=-1, keepdims=True)
    le2 = jnp.where(lane_f == e1, NEG, le)
    m2 = jnp.max(le2, axis=-1, keepdims=True)
    e2 = jnp.min(jnp.where(le2 == m2, lane_f, 1e9), axis=-1, keepdims=True)
    ratio = jnp.exp(m2 - m1)
    w1 = g_prob / (1.0 + ratio)
    w2 = w1 * ratio
    xt_ref[:, D_MODEL:D_MODEL + LANE] = jnp.where(lane_f == e1, w1, 0.0) + jnp.where(lane_f == e2, w2, 0.0)

    tm = h.shape[0]
    onehot = jnp.where(lane_f == g_idx, 1.0, 0.0)
    earlier = jnp.where(lax.broadcasted_iota(jnp.int32, (tm, tm), 0) > lax.broadcasted_iota(jnp.int32, (tm, tm), 1),
                        1.0, 0.0).astype(BF16)
    prefix = _dot(earlier, onehot.astype(BF16)) + carry_ref[...]
    rank = jnp.sum(onehot * prefix, axis=-1, keepdims=True)
    carry_ref[...] += jnp.sum(onehot, axis=0, keepdims=True)
    route_ref[...] = jnp.where(lane == 0, g_idx, jnp.where(lane == 1, rank, 0.0))
    cnt_ref[...] = jnp.broadcast_to(carry_ref[...], cnt_ref.shape)


def _out_proj(x2, oa, ob, w, g, wr, br):
    T = x2.shape[0]
    tm = TM_PROJ
    row = lambda i: (i, 0)
    full = lambda a: pl.BlockSpec(a.shape, lambda i: (0,) * a.ndim)
    o4 = pl.BlockSpec((4, tm, LANE), lambda i: (0, i, 0))
    return pl.pallas_call(
        _outproj_kernel,
        grid=(T // tm,),
        in_specs=[pl.BlockSpec((tm, D_MODEL), row), o4, o4, full(w), full(g), full(wr), full(br)],
        out_specs=[pl.BlockSpec((tm, D_MODEL), row), pl.BlockSpec((tm, XT_WIDTH), row), pl.BlockSpec((tm, LANE), row),
                   pl.BlockSpec((8, LANE), lambda i: (0, 0))],
        out_shape=[jax.ShapeDtypeStruct((T, D_MODEL), F32), jax.ShapeDtypeStruct((T, XT_WIDTH), F32),
                   jax.ShapeDtypeStruct((T, LANE), F32), jax.ShapeDtypeStruct((8, LANE), F32)],
        scratch_shapes=[pltpu.VMEM((1, LANE), F32)],
        compiler_params=_cparams(("arbitrary",)),
        name="out_proj",
    )(x2, oa, ob, w, g, wr, br)


def _moe_kernel(tg_ref, ok_ref, idx_ref, idxn_ref, xt_hbm, wg_ref, wu_ref, wd_ref, y_ref, xbuf, sem):
    i = pl.program_id(0)
    slot = i % 2
    tm = xbuf.shape[1]

    def row_copy(index_ref, r, dst_slot):
        return pltpu.make_async_copy(xt_hbm.at[pl.ds(index_ref[0, 0, r], 1), :],
                                     xbuf.at[dst_slot, pl.ds(r, 1), :], sem.at[dst_slot])

    @pl.when(i == 0)
    def _():
        def body(r, c):
            row_copy(idx_ref, r, 0).start()
            return c
        lax.fori_loop(0, tm, body, 0)

    @pl.when((i == 0) | (ok_ref[jnp.maximum(i - 1, 0)] == 1))
    def _():
        pltpu.make_async_copy(xt_hbm.at[pl.ds(0, tm), :], xbuf.at[slot], sem.at[slot]).wait()

    @pl.when(ok_ref[i] == 0)
    def _():
        y_ref[...] = jnp.zeros(y_ref.shape, F32)

    @pl.when(ok_ref[i] == 1)
    def _():
        lane = lax.broadcasted_iota(jnp.int32, (1, LANE), 1)
        x = xbuf[slot, :, 0:D_MODEL].astype(BF16)
        cmb = xbuf[slot, :, D_MODEL:D_MODEL + LANE]
        first_lane = MOE_GROUPS + tg_ref[i] * EPG
        per_expert = tm // EPG
        y = jnp.zeros((tm, D_MODEL), F32)
        for e in range(EPG):
            a = _dot(x, wg_ref[0, e])
            b = _dot(x, wu_ref[0, e])
            ce = jnp.sum(jnp.where(lane == first_lane + e, cmb, 0.0), axis=-1, keepdims=True)
            y = y + _dot(((a * jax.nn.sigmoid(a)) * b * ce).astype(BF16), wd_ref[0, e])
            for r in range(e * per_expert, (e + 1) * per_expert):
                row_copy(idxn_ref, r, 1 - slot).start()
        for blk in range(ROW_SLABS):
            y_ref[:, blk, :] = y[:, blk * LANE:(blk + 1) * LANE]


def _moe(tile_group, tile_ok, src3, xt, wg, wu, wd):
    n_tiles, _, tm = src3.shape
    last = n_tiles - 1
    grid_spec = pltpu.PrefetchScalarGridSpec(
        num_scalar_prefetch=2,
        grid=(n_tiles,),
        in_specs=[pl.BlockSpec((1, 1, tm), lambda i, tg, ok: (i, 0, 0), memory_space=pltpu.SMEM),
                  pl.BlockSpec((1, 1, tm), lambda i, tg, ok: (jnp.minimum(i + 1, last), 0, 0), memory_space=pltpu.SMEM),
                  pl.BlockSpec(memory_space=pl.ANY),
                  pl.BlockSpec((1, EPG, D_MODEL, EXPERT_FF), lambda i, tg, ok: (tg[i], 0, 0, 0)),
                  pl.BlockSpec((1, EPG, D_MODEL, EXPERT_FF), lambda i, tg, ok: (tg[i], 0, 0, 0)),
                  pl.BlockSpec((1, EPG, EXPERT_FF, D_MODEL), lambda i, tg, ok: (tg[i], 0, 0, 0))],
        out_specs=pl.BlockSpec((tm, ROW_SLABS, LANE), lambda i, tg, ok: (i, 0, 0)),
        scratch_shapes=[pltpu.VMEM((2, tm, XT_WIDTH), F32), pltpu.SemaphoreType.DMA((2,))],
    )
    return pl.pallas_call(
        _moe_kernel,
        grid_spec=grid_spec,
        out_shape=jax.ShapeDtypeStruct((n_tiles * tm, ROW_SLABS, LANE), F32),
        compiler_params=pltpu.CompilerParams(dimension_semantics=("arbitrary",), vmem_limit_bytes=VMEM_LIMIT_MOE),
        name="moe",
    )(tile_group, tile_ok, src3, src3, xt, wg, wu, wd)


def _final_kernel(pos_ref, posn_ref, y_hbm, h_ref, gf_ref, o_ref, ybuf, sem):
    i = pl.program_id(0)
    slot = i % 2
    tm = ybuf.shape[1]

    def issue(index_ref, dst_slot):
        def body(r, c):
            pltpu.make_async_copy(y_hbm.at[index_ref[0, 0, r]], ybuf.at[dst_slot, r], sem.at[dst_slot]).start()
            return c
        lax.fori_loop(0, tm, body, 0, unroll=8)

    @pl.when(i == 0)
    def _():
        issue(pos_ref, 0)

    @pl.when(i + 1 < pl.num_programs(0))
    def _():
        issue(posn_ref, 1 - slot)

    pltpu.make_async_copy(y_hbm.at[pl.ds(0, tm)], ybuf.at[slot], sem.at[slot]).wait()
    y = jnp.concatenate([ybuf[slot, :, blk, :] for blk in range(ROW_SLABS)], axis=-1)
    h = h_ref[...] + y
    o_ref[...] = h * lax.rsqrt(jnp.mean(h * h, axis=-1, keepdims=True) + EPS) * gf_ref[...]


def _final(pos3, y_sorted, h, gf):
    n_tiles, _, tm = pos3.shape
    last = n_tiles - 1
    row = lambda i: (i, 0)
    return pl.pallas_call(
        _final_kernel,
        grid=(n_tiles,),
        in_specs=[pl.BlockSpec((1, 1, tm), lambda i: (i, 0, 0), memory_space=pltpu.SMEM),
                  pl.BlockSpec((1, 1, tm), lambda i: (jnp.minimum(i + 1, last), 0, 0), memory_space=pltpu.SMEM),
                  pl.BlockSpec(memory_space=pl.ANY),
                  pl.BlockSpec((tm, D_MODEL), row),
                  pl.BlockSpec((1, D_MODEL), lambda i: (0, 0))],
        out_specs=pl.BlockSpec((tm, D_MODEL), row),
        out_shape=jax.ShapeDtypeStruct((n_tiles * tm, D_MODEL), F32),
        scratch_shapes=[pltpu.VMEM((2, tm, ROW_SLABS, LANE), F32), pltpu.SemaphoreType.DMA((2,))],
        compiler_params=_cparams(("arbitrary",)),
        name="final",
    )(pos3, pos3, y_sorted, h, gf)


def _qa_perm():
    new = np.arange(NSA_HEADS * NSA_HD)
    r, g, d = new // LANE, (new % LANE) // NSA_HD, new % NSA_HD
    return (g * NSA_R + r) * NSA_HD + d


def _block_diag2(w):
    z = jnp.zeros_like(w)
    return jnp.concatenate([jnp.concatenate([w, z], axis=-1), jnp.concatenate([z, w], axis=-1)], axis=-2)


def kernel(x, rel_bias, ln_mix, w_in, cmp_pos_k, cmp_pos_v, cmp_k_w1, cmp_k_w2, cmp_v_w1, cmp_v_w2,
           diff_lq1, diff_lk1, diff_lq2, diff_lk2, diff_subln, w_out, ln_ffn,
           router_group_w, router_group_b, router_expert_w, router_expert_b,
           exp_w_gate, exp_w_up, exp_w_down, ln_final):
    B, S, D = x.shape
    T = B * S
    assert D == D_MODEL and S % TQ == 0 and S >= WINDOW and T % TM_MOE == 0 and T % TM_FINAL == 0
    x2 = x.reshape(T, D)
    perm = _qa_perm()

    w = w_in[0]
    c_kc, c_vc, c_ks, c_vs, c_kw, c_vw, c_gt = 512, 640, 768, 896, 1024, 1152, 1280
    c_qb = c_gt + N_GATE
    c_kb, c_vb = c_qb + 512, c_qb + 1024
    col = lambda c, n=LANE: w[:, c:c + n]
    w_tok = jnp.concatenate([col(c_kc), col(c_vc), col(c_ks), col(c_kw), col(c_kb, 512)], axis=1).astype(BF16)
    w_feat = jnp.concatenate([w[:, perm], col(c_qb, 512), col(c_vs), col(c_vw), col(c_vb, 512), col(c_gt, N_GATE),
                              jnp.zeros((D, GATE_ROWS - N_GATE), F32)], axis=1).T.astype(BF16)
    qa, kc, vc, ks, kw, qb, kb, vst, vwt, vbt, gates_t = _in_proj(x2, ln_mix[0][None, :], w_tok, w_feat)

    w1k = _block_diag2(cmp_k_w1[0].reshape(CMP_LEN, NSA_HD, CMP_HIDDEN)).astype(BF16)
    w1v = _block_diag2(cmp_v_w1[0].reshape(CMP_LEN, NSA_HD, CMP_HIDDEN)).astype(BF16)
    w2k = _block_diag2(cmp_k_w2[0]).astype(BF16)
    w2v = _block_diag2(cmp_v_w2[0]).astype(BF16)
    posk = jnp.tile(cmp_pos_k[0], (1, NSA_G))
    posv = jnp.tile(cmp_pos_v[0], (1, NSA_G))
    kcmp, vcmpt = _compress(kc, vc, posk, posv, w1k, w1v, w2k, w2v, B, S)

    near, bias_c = _bias_tiles(rel_bias.T, S)

    n_sel = S // SEL_BLOCK
    nrow = S // CMP_STRIDE
    c_start = np.arange(nrow) * CMP_STRIDE
    s_start = np.arange(n_sel) * SEL_BLOCK
    ov = ((c_start[None, :] <= s_start[:, None] + SEL_BLOCK - 1)
          & (c_start[None, :] + CMP_LEN - 1 >= s_start[:, None])
          & (np.arange(nrow)[None, :] < nrow - 1)).astype(np.float32)
    o_a = _nsa(qa, gates_t, kcmp, vcmpt, bias_c, ks, vst, kw, vwt, near[:NSA_HEADS], jnp.asarray(ov, BF16), B, S)

    sub = jnp.tile(diff_subln[0], 2)[:, None]
    o_b = _diff(diff_lq1[0][None, :], diff_lk1[0][None, :], diff_lq2[0][None, :], diff_lk2[0][None, :],
                sub, qb, kb, vbt, near[NSA_HEADS:], B, S)

    w_o = jnp.concatenate([w_out[0][:512][perm], w_out[0][512:]], axis=0).astype(BF16)
    n_r = MOE_GROUPS + N_EXPERTS
    wr = jnp.concatenate([router_group_w[0], router_expert_w[0], jnp.zeros((D, LANE - n_r), F32)], axis=1).astype(BF16)
    br = jnp.concatenate([router_group_b[0], router_expert_b[0], jnp.zeros((LANE - n_r,), F32)])[None, :]
    h1, xt, route, counts = _out_proj(x2, o_a, o_b, w_o, ln_ffn[0][None, :], wr, br)

    tm = TM_MOE
    n_tiles = T // tm + MOE_GROUPS
    cnt = counts[0, :MOE_GROUPS].astype(jnp.int32)
    ends = jnp.cumsum((cnt + tm - 1) // tm * tm)
    starts = ends - (cnt + tm - 1) // tm * tm
    pos = starts[route[:, 0].astype(jnp.int32)] + route[:, 1].astype(jnp.int32)
    src = jnp.zeros((n_tiles * tm,), jnp.int32).at[pos].set(
        jnp.arange(T, dtype=jnp.int32), unique_indices=True, mode="promise_in_bounds")
    tile_start = jnp.arange(n_tiles, dtype=jnp.int32) * tm
    tile_group = jnp.minimum(jnp.searchsorted(ends, tile_start, side="right"), MOE_GROUPS - 1).astype(jnp.int32)
    tile_ok = (tile_start < ends[-1]).astype(jnp.int32)

    by_group = lambda a: a[0].astype(BF16).reshape((MOE_GROUPS, EPG) + a.shape[2:])
    y_sorted = _moe(tile_group, tile_ok, src.reshape(n_tiles, 1, tm), xt,
                    by_group(exp_w_gate), by_group(exp_w_up), by_group(exp_w_down))
    out = _final(pos.reshape(T // TM_FINAL, 1, TM_FINAL), y_sorted, h1, ln_final[None, :])
    return out.reshape(B, S, D)
```

```python
import math

import numpy as np
import jax
import jax.numpy as jnp
from jax import lax
from jax.experimental import pallas as pl
from jax.experimental.pallas import tpu as pltpu

F32 = jnp.float32
BF16 = jnp.bfloat16
NEG = -1e30
EPS = 1e-6
LOG2E = math.log2(math.e)

D_MODEL = 1024
LANE = 128
NSA_HEADS, NSA_G, NSA_R, NSA_HD = 8, 2, 4, 64
CMP_LEN, CMP_STRIDE, CMP_HIDDEN = 32, 16, 128
SEL_BLOCK, SEL_TOPN, SEL_FORCED_LOCAL, WINDOW = 64, 8, 2, 512
DIFF_HEADS, DIFF_HD = 8, 32
REL_BUCKETS, REL_MAX_EXACT, REL_MAX_DIST = 32, 16, 128
N_REL_HEADS = NSA_HEADS + DIFF_HEADS
MOE_GROUPS, EPG, N_EXPERTS, EXPERT_FF = 4, 8, 32, 256
LAMBDA_INIT = 0.8 - 0.6 * math.exp(-0.3 * 0)
N_GATE = NSA_HEADS * 3
GATE_ROWS = 32

TQ = 256
TM_PROJ = 512
TM_MOE = 512
TM_FINAL = 256
XT_WIDTH = D_MODEL + LANE
ROW_SLABS = D_MODEL // LANE
VMEM_LIMIT = 48 * 1024 * 1024
VMEM_LIMIT_MOE = 56 * 1024 * 1024


def _cparams(sem):
    return pltpu.CompilerParams(dimension_semantics=sem, vmem_limit_bytes=VMEM_LIMIT)


def _dot(a, b):
    return jnp.dot(a, b, preferred_element_type=F32)


def _dot_nt(a, b):
    return lax.dot_general(a, b, (((1,), (1,)), ((), ())), preferred_element_type=F32)


def _bucket_thresholds():
    n = np.arange(0, REL_MAX_DIST + 1)
    nf = np.maximum(n, 1).astype(np.float32)
    large = REL_MAX_EXACT + (np.log(nf / np.float32(REL_MAX_EXACT)) / np.float32(math.log(REL_MAX_DIST / REL_MAX_EXACT))
                             * np.float32(REL_BUCKETS - REL_MAX_EXACT)).astype(np.int32)
    large = np.minimum(large, REL_BUCKETS - 1)
    bucket = np.where(n < REL_MAX_EXACT, n, large)
    return [int(np.argmax(bucket >= b)) for b in range(REL_BUCKETS)]


_THR = _bucket_thresholds()


def _inproj_kernel(x_ref, g_ref, w_ref, wt_ref, qa_ref, kc_ref, vc_ref, ks_ref, kw_ref, qb_ref, kb_ref,
                   vst_ref, vwt_ref, vbt_ref, gt_ref):
    x = x_ref[...]
    xn = (x * lax.rsqrt(jnp.mean(x * x, axis=-1, keepdims=True) + EPS) * g_ref[...]).astype(BF16)
    a = _dot(xn, w_ref[:, 0:512])
    kc_ref[...] = a[:, 0:128]
    vc_ref[...] = a[:, 128:256]
    ks_ref[...] = a[:, 256:384].astype(BF16)
    kw_ref[...] = a[:, 384:512].astype(BF16)
    a = _dot(xn, w_ref[:, 512:1024])
    for r in range(4):
        kb_ref[r] = a[:, r * LANE:(r + 1) * LANE].astype(BF16)

    ft = _dot_nt(wt_ref[...], xn)

    def feat(row0, rows=LANE):
        return ft[row0:row0 + rows, :]
    for r in range(4):
        qa_ref[r] = (feat(r * LANE) * (NSA_HD ** -0.5 * LOG2E)).astype(BF16)
        qb_ref[r] = (feat(512 + r * LANE) * (DIFF_HD ** -0.5 * LOG2E)).astype(BF16)
        vbt_ref[r] = feat(1280 + r * LANE).astype(BF16)
    vst_ref[...] = feat(1024).astype(BF16)
    vwt_ref[...] = feat(1152).astype(BF16)
    gt_ref[...] = feat(1792, GATE_ROWS)


def _in_proj(x2, g, w, wt):
    T = x2.shape[0]
    tm = TM_PROJ
    row = lambda i: (i, 0)
    o128b = jax.ShapeDtypeStruct((T, LANE), BF16)
    o128f = jax.ShapeDtypeStruct((T, LANE), F32)
    o4 = jax.ShapeDtypeStruct((4, T, LANE), BF16)
    ot = jax.ShapeDtypeStruct((LANE, T), BF16)
    o4t = jax.ShapeDtypeStruct((4, LANE, T), BF16)
    s128 = pl.BlockSpec((tm, LANE), row)
    s4 = pl.BlockSpec((4, tm, LANE), lambda i: (0, i, 0))
    st = pl.BlockSpec((LANE, tm), lambda i: (0, i))
    s4t = pl.BlockSpec((4, LANE, tm), lambda i: (0, 0, i))
    return pl.pallas_call(
        _inproj_kernel,
        grid=(T // tm,),
        in_specs=[pl.BlockSpec((tm, D_MODEL), row),
                  pl.BlockSpec((1, D_MODEL), lambda i: (0, 0)),
                  pl.BlockSpec(w.shape, lambda i: (0, 0)),
                  pl.BlockSpec(wt.shape, lambda i: (0, 0))],
        out_specs=[s4t, s128, s128, s128, s128, s4t, s4, st, st, s4t, pl.BlockSpec((GATE_ROWS, tm), lambda i: (0, i))],
        out_shape=[o4t, o128f, o128f, o128b, o128b, o4t, o4, ot, ot, o4t, jax.ShapeDtypeStruct((GATE_ROWS, T), F32)],
        compiler_params=_cparams(("parallel",)),
        name="in_proj",
    )(x2, g, w, wt)


def _gelu_tanh(x):
    return 0.5 * x * (1.0 + jnp.tanh(math.sqrt(2.0 / math.pi) * (x + 0.044715 * (x * x * x))))


def _compress_kernel(kc_ref, vc_ref, posk_ref, posv_ref, w1k_ref, w1v_ref, w2k_ref, w2v_ref, ko_ref, vo_ref):
    nrow = kc_ref.shape[0] // CMP_STRIDE
    rid = lax.broadcasted_iota(jnp.int32, (nrow, 1), 0)
    cid = lax.broadcasted_iota(jnp.int32, (1, nrow), 1)
    for src, pos, w1, w2, out, transposed in ((kc_ref, posk_ref, w1k_ref, w2k_ref, ko_ref, False),
                                              (vc_ref, posv_ref, w1v_ref, w2v_ref, vo_ref, True)):
        hid_a = jnp.zeros((nrow, 2 * CMP_HIDDEN), F32)
        hid_b = jnp.zeros((nrow, 2 * CMP_HIDDEN), F32)
        for m in range(CMP_STRIDE):
            y = src[pl.ds(m, nrow, stride=CMP_STRIDE), :]
            hid_a = hid_a + _dot((y + pos[m:m + 1, :]).astype(BF16), w1[m])
            hid_b = hid_b + _dot((y + pos[CMP_STRIDE + m:CMP_STRIDE + m + 1, :]).astype(BF16), w1[CMP_STRIDE + m])
        hid = hid_a + pltpu.roll(hid_b, nrow - 1, 0)
        o = _dot(_gelu_tanh(hid).astype(BF16), w2[...])
        if transposed:
            out[0] = jnp.where(cid < nrow - 1, o.T, 0.0).astype(BF16)
        else:
            out[0] = jnp.where(rid < nrow - 1, o, 0.0).astype(BF16)


def _compress(kc, vc, posk, posv, w1k, w1v, w2k, w2v, B, S):
    nrow = S // CMP_STRIDE
    assert nrow == LANE
    full = lambda a: pl.BlockSpec(a.shape, lambda b: (0,) * a.ndim)
    src = pl.BlockSpec((S, LANE), lambda b: (b, 0))
    osp = pl.BlockSpec((1, nrow, LANE), lambda b: (b, 0, 0))
    osh = jax.ShapeDtypeStruct((B, nrow, LANE), BF16)
    return pl.pallas_call(
        _compress_kernel,
        grid=(B,),
        in_specs=[src, src, full(posk), full(posv), full(w1k), full(w1v), full(w2k), full(w2v)],
        out_specs=[osp, osp],
        out_shape=[osh, osh],
        compiler_params=_cparams(("parallel",)),
        name="compress",
    )(kc, vc, posk, posv, w1k, w1v, w2k, w2v)


def _bias_from_dist(dist, tab_ref, h):
    val = jnp.full(dist.shape, tab_ref[h, 0], F32)
    for b in range(1, REL_BUCKETS):
        val = jnp.where(dist >= _THR[b], tab_ref[h, b], val)
    return val


def _bias_near_kernel(tab_ref, out_ref):
    h = pl.program_id(0)
    j = lax.broadcasted_iota(jnp.int32, (TQ, TQ), 0)
    i = lax.broadcasted_iota(jnp.int32, (TQ, TQ), 1)
    for d in range(2):
        bias = (_bias_from_dist(i - j + d * TQ, tab_ref, h) - tab_ref[h, REL_BUCKETS - 1]) * LOG2E
        out_ref[0, d] = jnp.where(i - j + d * TQ >= 0, bias, NEG)


def _bias_cmp_kernel(tab_ref, out_ref):
    h = pl.program_id(0)
    c = lax.broadcasted_iota(jnp.int32, (LANE, TQ), 0)
    t = pl.program_id(1) * TQ + lax.broadcasted_iota(jnp.int32, (LANE, TQ), 1)
    out_ref[0] = _bias_from_dist(t - (c * CMP_STRIDE + CMP_LEN - 1), tab_ref, h) * LOG2E


def _bias_tiles(tab_t, S):
    smem = pl.BlockSpec(memory_space=pltpu.SMEM)
    near = pl.pallas_call(
        _bias_near_kernel,
        grid=(N_REL_HEADS,),
        in_specs=[smem],
        out_specs=pl.BlockSpec((1, 2, TQ, TQ), lambda h: (h, 0, 0, 0)),
        out_shape=jax.ShapeDtypeStruct((N_REL_HEADS, 2, TQ, TQ), F32),
        compiler_params=_cparams(("parallel",)),
        name="bias_near",
    )(tab_t)
    cmp_bias = pl.pallas_call(
        _bias_cmp_kernel,
        grid=(NSA_HEADS, S // TQ),
        in_specs=[smem],
        out_specs=pl.BlockSpec((1, LANE, TQ), lambda h, q: (h, 0, q)),
        out_shape=jax.ShapeDtypeStruct((NSA_HEADS, LANE, S), F32),
        compiler_params=_cparams(("parallel", "parallel")),
        name="bias_cmp",
    )(tab_t)
    return near, cmp_bias


ACC_ROWS = NSA_HD + 16


def _with_ones(vt):
    return jnp.concatenate([vt, jnp.ones((ACC_ROWS - vt.shape[0], vt.shape[1]), BF16)], axis=0)


def _probs(s, m):
    return jnp.exp2((s - m).astype(BF16))


def _flash_first(ss, vts):
    ms = [jnp.max(s, axis=0, keepdims=True) for s in ss]
    ps = [_probs(s, m) for s, m in zip(ss, ms)]
    return tuple((m, _dot(vt, p)) for m, p, vt in zip(ms, ps, vts))


def _flash_update(ss, vts, sts):
    ms = [jnp.maximum(st[0], jnp.max(s, axis=0, keepdims=True)) for s, st in zip(ss, sts)]
    alphas = [jnp.exp2(st[0] - m) for m, st in zip(ms, sts)]
    ps = [_probs(s, m) for s, m in zip(ss, ms)]
    return tuple((m, a * st[1] + _dot(vt, p)) for m, a, st, p, vt in zip(ms, alphas, sts, ps, vts))


def _flash_out(st):
    acc = st[1]
    return acc[0:NSA_HD, :] * (1.0 / acc[NSA_HD:NSA_HD + 1, :])


def _stream_scratch(n):
    return ([pltpu.VMEM((8, TQ), F32), pltpu.VMEM((ACC_ROWS, TQ), F32)] * n
            + [pltpu.VMEM((TQ, TQ), F32)] * (2 * n) + [pltpu.VMEM((TQ, TQ), BF16)] * (2 * n))


def _stream_refs(refs, n):
    st_refs = tuple((refs[2 * c], refs[2 * c + 1]) for c in range(n))
    s, p = refs[2 * n:4 * n], refs[4 * n:6 * n]
    return st_refs, tuple(s[:n]), tuple(s[n:]), tuple(p[:n]), tuple(p[n:])


def _causal_stream(qi, scores, vtile, st_refs, s_a, s_b, p_a, p_b, diag_ready=False):
    n_far = jnp.maximum(qi - 1, 0)
    top = qi - 2

    chains = range(len(st_refs))

    def write(s_refs, kt, near=2):
        for c in chains:
            s_refs[c][...] = scores(jnp.maximum(kt, 0), near, c)

    def pending(p_refs, kt):
        return [_dot(vtile(jnp.maximum(kt, 0), c), p_refs[c][...]) for c in chains]

    def step(s_cur, p_cur, s_next, kt_next, p_prev, kt_prev):
        rescale = []
        for c in chains:
            s_next[c][...] = scores(jnp.maximum(kt_next, 0), 2, c)
            pv = _dot(vtile(jnp.maximum(kt_prev, 0), c), p_prev[c][...])
            st = st_refs[c][0]
            s = s_cur[c][...]
            m_old = st[0:1, :]
            m = jnp.maximum(m_old, jnp.max(s, axis=0, keepdims=True))
            p_cur[c][...] = _probs(s, m)
            st[0:1, :] = m
            rescale.append((jnp.exp2(m_old - m), pv))
        for c in chains:
            acc = st_refs[c][1]
            acc[...] = rescale[c][0] * (acc[...] + rescale[c][1])

    if not diag_ready:
        write(s_a, qi, 0)
    for c in chains:
        s_b[c][...] = scores(jnp.maximum(qi - 1, 0), 1, c)
        st, acc = st_refs[c]
        s = s_a[c][...]
        m = jnp.max(s, axis=0, keepdims=True)
        p_a[c][...] = _probs(s, m)
        st[0:1, :] = m
        acc[...] = jnp.zeros(acc.shape, F32)
    step(s_b, p_b, s_a, top, p_a, qi)

    def pair(j, c):
        kt = top - 2 * j
        step(s_a, p_a, s_b, kt - 1, p_b, kt + 1)
        step(s_b, p_b, s_a, kt - 2, p_a, kt)
        return c
    lax.fori_loop(0, n_far // 2, pair, 0)

    @pl.when(n_far % 2 == 1)
    def _():
        for c, pv in zip(chains, pending(p_b, 1)):
            st, acc = st_refs[c]
            s = s_a[c][...]
            m_old = st[0:1, :]
            m = jnp.maximum(m_old, jnp.max(s, axis=0, keepdims=True))
            st[0:1, :] = m
            acc[...] = jnp.exp2(m_old - m) * (acc[...] + pv) + _dot(vtile(0, c), _probs(s, m))

    @pl.when(n_far % 2 == 0)
    def _():
        last = jnp.where(n_far == 0, qi - 1, 0)
        for c, pv in zip(chains, pending(p_b, last)):
            acc = st_refs[c][1]
            acc[...] = acc[...] + pv
    return tuple((st[0:1, :], acc[...]) for st, acc in st_refs)


def _ktile(ref, kt):
    return ref[pl.ds(pl.multiple_of(kt * TQ, TQ), TQ), :]


def _vtile(ref, kt):
    return ref[:, pl.ds(pl.multiple_of(kt * TQ, TQ), TQ)]


def _nsa_kernel(qa_ref, gt_ref, kcmp_ref, vcmpt_ref, bc_ref, ks_ref, vst_ref, kw_ref, vwt_ref, dn_ref,
                ov_ref, o_ref, psum_ref, sel_ref, oacc_ref, sig_ref, *scratch):
    st_refs, s_a, s_b, p_a, p_b = _stream_refs(scratch, 2 * NSA_G)
    qi = pl.program_id(1)
    t0 = qi * TQ
    sub_grp = lax.shift_right_arithmetic(lax.broadcasted_iota(jnp.int32, (LANE, 1), 0), 6)
    sig_ref[...] = jax.nn.sigmoid(gt_ref[...])

    def gate_row(c):
        return sig_ref[pl.ds(c, 1), :]

    def masked_q(r, g):
        return jnp.where(sub_grp == g, qa_ref[r].astype(F32), 0.0).astype(BF16)

    n_cmp = kcmp_ref.shape[1] - 1
    crow = lax.broadcasted_iota(jnp.int32, (LANE, 1), 0)
    cmp_end = jnp.where(crow < n_cmp, crow * CMP_STRIDE + CMP_LEN - 1, 1 << 30)
    mask_c = (t0 + lax.broadcasted_iota(jnp.int32, (1, TQ), 1)) >= cmp_end

    heads = [(r, g) for r in range(NSA_R) for g in range(NSA_G)]
    scores_c = [_dot(kcmp_ref[0], masked_q(r, g)) for r, g in heads]
    probs_c = []
    for (r, g), s in zip(heads, scores_c):
        s = jnp.where(mask_c, s + bc_ref[g * NSA_R + r], NEG)
        p = jnp.where(mask_c, jnp.exp2(s - jnp.max(s, axis=0, keepdims=True)), 0.0)
        l = jnp.sum(p, axis=0, keepdims=True)
        probs_c.append(p * jnp.where(l > 0.0, 1.0 / l, 0.0))
    for g in range(NSA_G):
        psum_ref[g] = sum(p for (r, gg), p in zip(heads, probs_c) if gg == g)
    outs_c = [_dot(vcmpt_ref[0], p.astype(BF16)) for p in probs_c]
    for r in range(NSA_R):
        o0, o1 = outs_c[r * NSA_G], outs_c[r * NSA_G + 1]
        oacc_ref[r] = jnp.where(sub_grp == 0, gate_row(r * 3) * o0, gate_row((NSA_R + r) * 3) * o1)

    n_sel = ov_ref.shape[0]
    jj = lax.broadcasted_iota(jnp.int32, (n_sel, TQ), 0)
    cur = lax.shift_right_arithmetic(t0 + lax.broadcasted_iota(jnp.int32, (n_sel, TQ), 1), 6)
    valid = jj <= cur
    forced = (jj == 0) | (cur - jj < SEL_FORCED_LOCAL)
    for g in range(NSA_G):
        ps = psum_ref[g]
        hi = ps.astype(BF16)
        rem = ps - hi.astype(F32)
        mid = rem.astype(BF16)
        lo = (rem - mid.astype(F32)).astype(BF16)
        ov = ov_ref[...]
        imp = _dot(ov, hi) + _dot(ov, mid) + _dot(ov, lo)
        score = jnp.where(valid, jnp.where(forced, 1e9, imp), -1e9)
        cnt = jnp.zeros((n_sel, TQ), F32)
        for j2 in range(n_sel):
            row = score[j2:j2 + 1, :]
            tie = jnp.where(jj > j2, 1.0, 0.0)
            cnt = cnt + jnp.where(row > score, 1.0, jnp.where(row == score, tie, 0.0))
        sel_ref[g] = jnp.where(cnt < float(min(SEL_TOPN, n_sel)), jnp.where(score > -1e8, 0.0, NEG), NEG)

    has_prev = jnp.where(qi >= 1, 0.0, NEG)
    has_wfar = jnp.where(qi >= WINDOW // TQ, 0.0, NEG)
    kt_prev = jnp.maximum(qi - 1, 0)
    kt_wfar = jnp.maximum(qi - WINDOW // TQ, 0)
    wfar_mask = lax.broadcasted_iota(jnp.int32, (TQ, TQ), 0) > lax.broadcasted_iota(jnp.int32, (TQ, TQ), 1)

    def main_body(rp, carry):
        chains = [(2 * rp + i, g) for i in range(2) for g in range(NSA_G)]
        qms = [masked_q(r, g) for r, g in chains]
        hids = [g * NSA_R + r for r, g in chains]

        def block_mask(g, kt):
            per_tile = TQ // SEL_BLOCK
            rows = [jnp.broadcast_to(sel_ref[g, pl.ds(kt * per_tile + i, 1), :], (SEL_BLOCK, TQ))
                    for i in range(per_tile)]
            return jnp.concatenate(rows, axis=0)

        def near_bias(s, near, c):
            if near == 0:
                return s + dn_ref[hids[c], 0]
            if near == 1:
                return s + dn_ref[hids[c], 1] + has_prev
            return s

        def group_values(ref, kt, c):
            g = chains[c][1]
            return _with_ones(ref[g * NSA_HD:(g + 1) * NSA_HD, pl.ds(pl.multiple_of(kt * TQ, TQ), TQ)])

        def finish(sts, branch):
            for c, (r, g) in enumerate(chains):
                oacc_ref[r, g * NSA_HD:(g + 1) * NSA_HD, :] += gate_row(hids[c] * 3 + branch) * _flash_out(sts[c])

        def slc_scores(kt, near, c):
            return near_bias(_dot(_ktile(ks_ref, kt), qms[c]), near, c) + block_mask(chains[c][1], kt)

        def slc_vtile(kt, c):
            return group_values(vst_ref, kt, c)

        finish(_causal_stream(qi, slc_scores, slc_vtile, st_refs, s_a, s_b, p_a, p_b), 1)

        cs = range(len(chains))
        win = []
        for kt, near in ((qi, 0), (kt_prev, 1), (kt_wfar, 2)):
            k = _ktile(kw_ref, kt)
            ss = [near_bias(_dot(k, qms[c]), near, c) for c in cs]
            if near == 2:
                ss = [jnp.where(wfar_mask, s + has_wfar, NEG) for s in ss]
            win.append(ss)
        sts = _flash_first(win[0], [group_values(vwt_ref, qi, c) for c in cs])
        sts = _flash_update(win[1], [group_values(vwt_ref, kt_prev, c) for c in cs], sts)
        sts = _flash_update(win[2], [group_values(vwt_ref, kt_wfar, c) for c in cs], sts)
        finish(sts, 2)
        return carry
    lax.fori_loop(0, NSA_R // 2, main_body, 0)

    for r in range(NSA_R):
        o_ref[r] = oacc_ref[r].T.astype(BF16)


def _nsa(qa, gates_t, kcmp, vcmpt, bias_c, ks, vst, kw, vwt, near_a, ov, B, S):
    nq = S // TQ
    T = B * S
    k_spec = pl.BlockSpec((S, LANE), lambda b, q: (b, 0))
    v_spec = pl.BlockSpec((LANE, S), lambda b, q: (0, b))
    full = lambda a: pl.BlockSpec(a.shape, lambda b, q: (0,) * a.ndim)
    cmp_spec = pl.BlockSpec((1, LANE, LANE), lambda b, q: (b, 0, 0))
    qo = pl.BlockSpec((4, TQ, LANE), lambda b, q: (0, b * nq + q, 0))
    return pl.pallas_call(
        _nsa_kernel,
        grid=(B, nq),
        in_specs=[pl.BlockSpec((4, LANE, TQ), lambda b, q: (0, 0, b * nq + q)),
                  pl.BlockSpec((GATE_ROWS, TQ), lambda b, q: (0, b * nq + q)),
                  cmp_spec, cmp_spec,
                  pl.BlockSpec((NSA_HEADS, LANE, TQ), lambda b, q: (0, 0, q)),
                  k_spec, v_spec, k_spec, v_spec, full(near_a), full(ov)],
        out_specs=qo,
        out_shape=jax.ShapeDtypeStruct((4, T, LANE), BF16),
        scratch_shapes=[pltpu.VMEM((NSA_G, LANE, TQ), F32), pltpu.VMEM((NSA_G, S // SEL_BLOCK, TQ), F32),
                        pltpu.VMEM((NSA_R, LANE, TQ), F32), pltpu.VMEM((GATE_ROWS, TQ), F32)] + _stream_scratch(2 * NSA_G),
        compiler_params=_cparams(("parallel", "arbitrary")),
        name="nsa",
    )(qa, gates_t, kcmp, vcmpt, bias_c, ks, vst, kw, vwt, near_a, ov)


def _diff_kernel(lq1_ref, lk1_ref, lq2_ref, lk2_ref, sub_ref, qb_ref, kb_ref, vbt_ref, dn_ref, o_ref, *scratch):
    n = 2 * 2
    st_refs, s_a, s_b, p_a, p_b = _stream_refs(scratch, n)
    qi = pl.program_id(1)
    chain_of_row = lax.shift_right_arithmetic(lax.broadcasted_iota(jnp.int32, (LANE, 1), 0), 5)
    lam = (jnp.exp(jnp.sum(lq1_ref[...] * lk1_ref[...], axis=-1, keepdims=True))
           - jnp.exp(jnp.sum(lq2_ref[...] * lk2_ref[...], axis=-1, keepdims=True)) + LAMBDA_INIT)
    has_prev = jnp.where(qi >= 1, 0.0, NEG)

    def score_fn(pr):
        q = qb_ref[pr].astype(F32)
        qms = [jnp.where(chain_of_row == c, q, 0.0).astype(BF16) for c in range(n)]

        def scores(kt, near, c):
            s = _dot(kb_ref[pr, pl.ds(pl.multiple_of(kt * TQ, TQ), TQ), :], qms[c])
            if near == 2:
                return s
            bias = dn_ref[2 * pr + c // 2, near]
            return s + bias if near == 0 else s + bias + has_prev
        return scores

    def start_diag(pr):
        scores = score_fn(pr)
        for c in range(n):
            s_a[c][...] = scores(qi, 0, c)

    start_diag(0)

    def pair_body(pr, carry):
        def vtile(kt, c):
            hh = c // 2
            return _with_ones(vbt_ref[pr, hh * 2 * DIFF_HD:(hh + 1) * 2 * DIFF_HD, pl.ds(pl.multiple_of(kt * TQ, TQ), TQ)])

        sts = _causal_stream(qi, score_fn(pr), vtile, st_refs, s_a, s_b, p_a, p_b, diag_ready=True)
        start_diag(jnp.minimum(pr + 1, DIFF_HEADS // 2 - 1))
        outs = []
        for hh in range(2):
            o = _flash_out(sts[2 * hh]) - lam * _flash_out(sts[2 * hh + 1])
            outs.append(o * lax.rsqrt(jnp.mean(o * o, axis=0, keepdims=True) + EPS))
        out = jnp.concatenate(outs, axis=0) * sub_ref[...] * (1.0 - LAMBDA_INIT)
        o_ref[pr] = out.T.astype(BF16)
        return carry
    lax.fori_loop(0, DIFF_HEADS // 2, pair_body, 0)


def _diff(lq1, lk1, lq2, lk2, sub, qb, kb, vbt, near_b, B, S):
    nq = S // TQ
    T = B * S
    full = lambda a: pl.BlockSpec(a.shape, lambda b, q: (0,) * a.ndim)
    k_spec = pl.BlockSpec((4, S, LANE), lambda b, q: (0, b, 0))
    v_spec = pl.BlockSpec((4, LANE, S), lambda b, q: (0, 0, b))
    qo = pl.BlockSpec((4, TQ, LANE), lambda b, q: (0, b * nq + q, 0))
    return pl.pallas_call(
        _diff_kernel,
        grid=(B, nq),
        in_specs=[full(lq1), full(lk1), full(lq2), full(lk2), full(sub),
                  pl.BlockSpec((4, LANE, TQ), lambda b, q: (0, 0, b * nq + q)), k_spec, v_spec, full(near_b)],
        out_specs=qo,
        out_shape=jax.ShapeDtypeStruct((4, T, LANE), BF16),
        scratch_shapes=_stream_scratch(4),
        compiler_params=_cparams(("parallel", "arbitrary")),
        name="diff",
    )(lq1, lk1, lq2, lk2, sub, qb, kb, vbt, near_b)


def _outproj_kernel(x_ref, oa_ref, ob_ref, w_ref, g_ref, wr_ref, br_ref, h_ref, xt_ref, route_ref, cnt_ref,
                    carry_ref):
    @pl.when(pl.program_id(0) == 0)
    def _():
        carry_ref[...] = jnp.zeros(carry_ref.shape, F32)

    o = jnp.concatenate([oa_ref[r] for r in range(4)] + [ob_ref[r] for r in range(4)], axis=-1)
    h = x_ref[...] + _dot(o, w_ref[...])
    h_ref[...] = h
    tn32 = h * lax.rsqrt(jnp.mean(h * h, axis=-1, keepdims=True) + EPS) * g_ref[...]
    xt_ref[:, 0:D_MODEL] = tn32
    tn = tn32.astype(BF16)
    logits = _dot(tn, wr_ref[...]) + br_ref[...]
    lane = lax.broadcasted_iota(jnp.int32, (1, LANE), 1)
    lane_f = lane.astype(F32)
    is_grp = lane < MOE_GROUPS
    lg = jnp.where(is_grp, logits, NEG)
    mg = jnp.max(lg, axis=-1, keepdims=True)
    zg = jnp.sum(jnp.where(is_grp, jnp.exp(lg - mg), 0.0), axis=-1, keepdims=True)
    g_prob = 1.0 / zg
    g_idx = jnp.min(jnp.where(lg == mg, lane_f, 1e9), axis=-1, keepdims=True)
    lane_grp = jnp.where((lane >= MOE_GROUPS) & (lane < MOE_GROUPS + N_EXPERTS),
                         lax.shift_right_arithmetic(lane - MOE_GROUPS, 3), -1).astype(F32)
    le = jnp.where(lane_grp == g_idx, logits, NEG)
    m1 = jnp.max(le, axis=-1, keepdims=True)
    e1 = jnp.min(jnp.where(le == m1, lane_f, 1e9), axis=-1, keepdims=True)
    le2 = jnp.where(lane_f == e1, NEG, le)
    m2 = jnp.max(le2, axis=-1, keepdims=True)
    e2 = jnp.min(jnp.where(le2 == m2, lane_f, 1e9), axis=-1, keepdims=True)
    ratio = jnp.exp(m2 - m1)
    w1 = g_prob / (1.0 + ratio)
    w2 = w1 * ratio
    xt_ref[:, D_MODEL:D_MODEL + LANE] = jnp.where(lane_f == e1, w1, 0.0) + jnp.where(lane_f == e2, w2, 0.0)

    tm = h.shape[0]
    onehot = jnp.where(lane_f == g_idx, 1.0, 0.0)
    earlier = jnp.where(lax.broadcasted_iota(jnp.int32, (tm, tm), 0) > lax.broadcasted_iota(jnp.int32, (tm, tm), 1),
                        1.0, 0.0).astype(BF16)
    prefix = _dot(earlier, onehot.astype(BF16)) + carry_ref[...]
    rank = jnp.sum(onehot * prefix, axis=-1, keepdims=True)
    carry_ref[...] += jnp.sum(onehot, axis=0, keepdims=True)
    route_ref[...] = jnp.where(lane == 0, g_idx, jnp.where(lane == 1, rank, 0.0))
    cnt_ref[...] = jnp.broadcast_to(carry_ref[...], cnt_ref.shape)


def _out_proj(x2, oa, ob, w, g, wr, br):
    T = x2.shape[0]
    tm = TM_PROJ
    row = lambda i: (i, 0)
    full = lambda a: pl.BlockSpec(a.shape, lambda i: (0,) * a.ndim)
    o4 = pl.BlockSpec((4, tm, LANE), lambda i: (0, i, 0))
    return pl.pallas_call(
        _outproj_kernel,
        grid=(T // tm,),
        in_specs=[pl.BlockSpec((tm, D_MODEL), row), o4, o4, full(w), full(g), full(wr), full(br)],
        out_specs=[pl.BlockSpec((tm, D_MODEL), row), pl.BlockSpec((tm, XT_WIDTH), row), pl.BlockSpec((tm, LANE), row),
                   pl.BlockSpec((8, LANE), lambda i: (0, 0))],
        out_shape=[jax.ShapeDtypeStruct((T, D_MODEL), F32), jax.ShapeDtypeStruct((T, XT_WIDTH), F32),
                   jax.ShapeDtypeStruct((T, LANE), F32), jax.ShapeDtypeStruct((8, LANE), F32)],
        scratch_shapes=[pltpu.VMEM((1, LANE), F32)],
        compiler_params=_cparams(("arbitrary",)),
        name="out_proj",
    )(x2, oa, ob, w, g, wr, br)


def _moe_kernel(tg_ref, ok_ref, idx_ref, idxn_ref, xt_hbm, wg_ref, wu_ref, wd_ref, y_ref, xbuf, sem):
    i = pl.program_id(0)
    slot = i % 2
    tm = xbuf.shape[1]

    def row_copy(index_ref, r, dst_slot):
        return pltpu.make_async_copy(xt_hbm.at[pl.ds(index_ref[0, 0, r], 1), :],
                                     xbuf.at[dst_slot, pl.ds(r, 1), :], sem.at[dst_slot])

    @pl.when(i == 0)
    def _():
        def body(r, c):
            row_copy(idx_ref, r, 0).start()
            return c
        lax.fori_loop(0, tm, body, 0)

    @pl.when((i == 0) | (ok_ref[jnp.maximum(i - 1, 0)] == 1))
    def _():
        pltpu.make_async_copy(xt_hbm.at[pl.ds(0, tm), :], xbuf.at[slot], sem.at[slot]).wait()

    @pl.when(ok_ref[i] == 0)
    def _():
        y_ref[...] = jnp.zeros(y_ref.shape, F32)

    @pl.when(ok_ref[i] == 1)
    def _():
        lane = lax.broadcasted_iota(jnp.int32, (1, LANE), 1)
        x = xbuf[slot, :, 0:D_MODEL].astype(BF16)
        cmb = xbuf[slot, :, D_MODEL:D_MODEL + LANE]
        first_lane = MOE_GROUPS + tg_ref[i] * EPG
        per_expert = tm // EPG
        y = jnp.zeros((tm, D_MODEL), F32)
        for e in range(EPG):
            a = _dot(x, wg_ref[0, e])
            b = _dot(x, wu_ref[0, e])
            ce = jnp.sum(jnp.where(lane == first_lane + e, cmb, 0.0), axis=-1, keepdims=True)
            y = y + _dot(((a * jax.nn.sigmoid(a)) * b * ce).astype(BF16), wd_ref[0, e])
            for r in range(e * per_expert, (e + 1) * per_expert):
                row_copy(idxn_ref, r, 1 - slot).start()
        for blk in range(ROW_SLABS):
            y_ref[:, blk, :] = y[:, blk * LANE:(blk + 1) * LANE]


def _moe(tile_group, tile_ok, src3, xt, wg, wu, wd):
    n_tiles, _, tm = src3.shape
    last = n_tiles - 1
    grid_spec = pltpu.PrefetchScalarGridSpec(
        num_scalar_prefetch=2,
        grid=(n_tiles,),
        in_specs=[pl.BlockSpec((1, 1, tm), lambda i, tg, ok: (i, 0, 0), memory_space=pltpu.SMEM),
                  pl.BlockSpec((1, 1, tm), lambda i, tg, ok: (jnp.minimum(i + 1, last), 0, 0), memory_space=pltpu.SMEM),
                  pl.BlockSpec(memory_space=pl.ANY),
                  pl.BlockSpec((1, EPG, D_MODEL, EXPERT_FF), lambda i, tg, ok: (tg[i], 0, 0, 0)),
                  pl.BlockSpec((1, EPG, D_MODEL, EXPERT_FF), lambda i, tg, ok: (tg[i], 0, 0, 0)),
                  pl.BlockSpec((1, EPG, EXPERT_FF, D_MODEL), lambda i, tg, ok: (tg[i], 0, 0, 0))],
        out_specs=pl.BlockSpec((tm, ROW_SLABS, LANE), lambda i, tg, ok: (i, 0, 0)),
        scratch_shapes=[pltpu.VMEM((2, tm, XT_WIDTH), F32), pltpu.SemaphoreType.DMA((2,))],
    )
    return pl.pallas_call(
        _moe_kernel,
        grid_spec=grid_spec,
        out_shape=jax.ShapeDtypeStruct((n_tiles * tm, ROW_SLABS, LANE), F32),
        compiler_params=pltpu.CompilerParams(dimension_semantics=("arbitrary",), vmem_limit_bytes=VMEM_LIMIT_MOE),
        name="moe",
    )(tile_group, tile_ok, src3, src3, xt, wg, wu, wd)


def _final_kernel(pos_ref, posn_ref, y_hbm, h_ref, gf_ref, o_ref, ybuf, sem):
    i = pl.program_id(0)
    slot = i % 2
    tm = h_ref.shape[0]

    def issue(index_ref, dst_slot):
        def body(r, c):
            pltpu.make_async_copy(y_hbm.at[index_ref[0, 0, r]], ybuf.at[dst_slot, pl.ds(r * ROW_SLABS, ROW_SLABS), :],
                                  sem.at[dst_slot]).start()
            return c
        lax.fori_loop(0, tm, body, 0, unroll=8)

    @pl.when(i == 0)
    def _():
        issue(pos_ref, 0)

    @pl.when(i + 1 < pl.num_programs(0))
    def _():
        issue(posn_ref, 1 - slot)

    pltpu.make_async_copy(ybuf.at[slot], ybuf.at[slot], sem.at[slot]).wait()
    y = jnp.concatenate([ybuf[slot, pl.ds(blk, tm, stride=ROW_SLABS), :] for blk in range(ROW_SLABS)], axis=-1)
    h = h_ref[...] + y
    o_ref[...] = h * lax.rsqrt(jnp.mean(h * h, axis=-1, keepdims=True) + EPS) * gf_ref[...]


def _final(pos3, y_sorted, h, gf):
    n_tiles, _, tm = pos3.shape
    last = n_tiles - 1
    row = lambda i: (i, 0)
    return pl.pallas_call(
        _final_kernel,
        grid=(n_tiles,),
        in_specs=[pl.BlockSpec((1, 1, tm), lambda i: (i, 0, 0), memory_space=pltpu.SMEM),
                  pl.BlockSpec((1, 1, tm), lambda i: (jnp.minimum(i + 1, last), 0, 0), memory_space=pltpu.SMEM),
                  pl.BlockSpec(memory_space=pl.ANY),
                  pl.BlockSpec((tm, D_MODEL), row),
                  pl.BlockSpec((1, D_MODEL), lambda i: (0, 0))],
        out_specs=pl.BlockSpec((tm, D_MODEL), row),
        out_shape=jax.ShapeDtypeStruct((n_tiles * tm, D_MODEL), F32),
        scratch_shapes=[pltpu.VMEM((2, tm * ROW_SLABS, LANE), F32), pltpu.SemaphoreType.DMA((2,))],
        compiler_params=_cparams(("arbitrary",)),
        name="final",
    )(pos3, pos3, y_sorted, h, gf)


def _qa_perm():
    new = np.arange(NSA_HEADS * NSA_HD)
    r, g, d = new // LANE, (new % LANE) // NSA_HD, new % NSA_HD
    return (g * NSA_R + r) * NSA_HD + d


def _block_diag2(w):
    z = jnp.zeros_like(w)
    return jnp.concatenate([jnp.concatenate([w, z], axis=-1), jnp.concatenate([z, w], axis=-1)], axis=-2)


def kernel(x, rel_bias, ln_mix, w_in, cmp_pos_k, cmp_pos_v, cmp_k_w1, cmp_k_w2, cmp_v_w1, cmp_v_w2,
           diff_lq1, diff_lk1, diff_lq2, diff_lk2, diff_subln, w_out, ln_ffn,
           router_group_w, router_group_b, router_expert_w, router_expert_b,
           exp_w_gate, exp_w_up, exp_w_down, ln_final):
    B, S, D = x.shape
    T = B * S
    assert D == D_MODEL and S % TQ == 0 and S >= WINDOW and T % TM_MOE == 0 and T % TM_FINAL == 0
    x2 = x.reshape(T, D)
    perm = _qa_perm()

    w = w_in[0]
    c_kc, c_vc, c_ks, c_vs, c_kw, c_vw, c_gt = 512, 640, 768, 896, 1024, 1152, 1280
    c_qb = c_gt + N_GATE
    c_kb, c_vb = c_qb + 512, c_qb + 1024
    col = lambda c, n=LANE: w[:, c:c + n]
    w_tok = jnp.concatenate([col(c_kc), col(c_vc), col(c_ks), col(c_kw), col(c_kb, 512)], axis=1).astype(BF16)
    w_feat = jnp.concatenate([w[:, perm], col(c_qb, 512), col(c_vs), col(c_vw), col(c_vb, 512), col(c_gt, N_GATE),
                              jnp.zeros((D, GATE_ROWS - N_GATE), F32)], axis=1).T.astype(BF16)
    qa, kc, vc, ks, kw, qb, kb, vst, vwt, vbt, gates_t = _in_proj(x2, ln_mix[0][None, :], w_tok, w_feat)

    w1k = _block_diag2(cmp_k_w1[0].reshape(CMP_LEN, NSA_HD, CMP_HIDDEN)).astype(BF16)
    w1v = _block_diag2(cmp_v_w1[0].reshape(CMP_LEN, NSA_HD, CMP_HIDDEN)).astype(BF16)
    w2k = _block_diag2(cmp_k_w2[0]).astype(BF16)
    w2v = _block_diag2(cmp_v_w2[0]).astype(BF16)
    posk = jnp.tile(cmp_pos_k[0], (1, NSA_G))
    posv = jnp.tile(cmp_pos_v[0], (1, NSA_G))
    kcmp, vcmpt = _compress(kc, vc, posk, posv, w1k, w1v, w2k, w2v, B, S)

    near, bias_c = _bias_tiles(rel_bias.T, S)

    n_sel = S // SEL_BLOCK
    nrow = S // CMP_STRIDE
    c_start = np.arange(nrow) * CMP_STRIDE
    s_start = np.arange(n_sel) * SEL_BLOCK
    ov = ((c_start[None, :] <= s_start[:, None] + SEL_BLOCK - 1)
          & (c_start[None, :] + CMP_LEN - 1 >= s_start[:, None])
          & (np.arange(nrow)[None, :] < nrow - 1)).astype(np.float32)
    o_a = _nsa(qa, gates_t, kcmp, vcmpt, bias_c, ks, vst, kw, vwt, near[:NSA_HEADS], jnp.asarray(ov, BF16), B, S)

    sub = jnp.tile(diff_subln[0], 2)[:, None]
    o_b = _diff(diff_lq1[0][None, :], diff_lk1[0][None, :], diff_lq2[0][None, :], diff_lk2[0][None, :],
                sub, qb, kb, vbt, near[NSA_HEADS:], B, S)

    w_o = jnp.concatenate([w_out[0][:512][perm], w_out[0][512:]], axis=0).astype(BF16)
    n_r = MOE_GROUPS + N_EXPERTS
    wr = jnp.concatenate([router_group_w[0], router_expert_w[0], jnp.zeros((D, LANE - n_r), F32)], axis=1).astype(BF16)
    br = jnp.concatenate([router_group_b[0], router_expert_b[0], jnp.zeros((LANE - n_r,), F32)])[None, :]
    h1, xt, route, counts = _out_proj(x2, o_a, o_b, w_o, ln_ffn[0][None, :], wr, br)

    tm = TM_MOE
    n_tiles = T // tm + MOE_GROUPS
    cnt = counts[0, :MOE_GROUPS].astype(jnp.int32)
    ends = jnp.cumsum((cnt + tm - 1) // tm * tm)
    starts = ends - (cnt + tm - 1) // tm * tm
    pos = starts[route[:, 0].astype(jnp.int32)] + route[:, 1].astype(jnp.int32)
    src = jnp.zeros((n_tiles * tm,), jnp.int32).at[pos].set(
        jnp.arange(T, dtype=jnp.int32), unique_indices=True, mode="promise_in_bounds")
    tile_start = jnp.arange(n_tiles, dtype=jnp.int32) * tm
    tile_group = jnp.minimum(jnp.searchsorted(ends, tile_start, side="right"), MOE_GROUPS - 1).astype(jnp.int32)
    tile_ok = (tile_start < ends[-1]).astype(jnp.int32)

    by_group = lambda a: a[0].astype(BF16).reshape((MOE_GROUPS, EPG) + a.shape[2:])
    y_sorted = _moe(tile_group, tile_ok, src.reshape(n_tiles, 1, tm), xt,
                    by_group(exp_w_gate), by_group(exp_w_up), by_group(exp_w_down))
    out = _final(pos.reshape(T // TM_FINAL, 1, TM_FINAL), y_sorted, h1, ln_final[None, :])
    return out.reshape(B, S, D)
```

```python
import math

import numpy as np
import jax
import jax.numpy as jnp
from jax import lax
from jax.experimental import pallas as pl
from jax.experimental.pallas import tpu as pltpu

F32 = jnp.float32
BF16 = jnp.bfloat16
NEG = -1e30
EPS = 1e-6
LOG2E = math.log2(math.e)

D_MODEL = 1024
LANE = 128
NSA_HEADS, NSA_G, NSA_R, NSA_HD = 8, 2, 4, 64
CMP_LEN, CMP_STRIDE, CMP_HIDDEN = 32, 16, 128
SEL_BLOCK, SEL_TOPN, SEL_FORCED_LOCAL, WINDOW = 64, 8, 2, 512
DIFF_HEADS, DIFF_HD = 8, 32
REL_BUCKETS, REL_MAX_EXACT, REL_MAX_DIST = 32, 16, 128
N_REL_HEADS = NSA_HEADS + DIFF_HEADS
MOE_GROUPS, EPG, N_EXPERTS, EXPERT_FF = 4, 8, 32, 256
LAMBDA_INIT = 0.8 - 0.6 * math.exp(-0.3 * 0)
N_GATE = NSA_HEADS * 3
GATE_ROWS = 32

TQ = 256
DIFF_PAIRS_PER_STEP = 2
NSA_BLOCKS_PER_STEP = 4
TM_PROJ = 512
TM_MOE = 512
TM_FINAL = 256
XT_WIDTH = D_MODEL + LANE
ROW_SLABS = D_MODEL // LANE
VMEM_LIMIT = 48 * 1024 * 1024
VMEM_LIMIT_MOE = 56 * 1024 * 1024


def _cparams(sem):
    return pltpu.CompilerParams(dimension_semantics=sem, vmem_limit_bytes=VMEM_LIMIT)


def _dot(a, b):
    return jnp.dot(a, b, preferred_element_type=F32)


def _dot_nt(a, b):
    return lax.dot_general(a, b, (((1,), (1,)), ((), ())), preferred_element_type=F32)


def _bucket_thresholds():
    n = np.arange(0, REL_MAX_DIST + 1)
    nf = np.maximum(n, 1).astype(np.float32)
    large = REL_MAX_EXACT + (np.log(nf / np.float32(REL_MAX_EXACT)) / np.float32(math.log(REL_MAX_DIST / REL_MAX_EXACT))
                             * np.float32(REL_BUCKETS - REL_MAX_EXACT)).astype(np.int32)
    large = np.minimum(large, REL_BUCKETS - 1)
    bucket = np.where(n < REL_MAX_EXACT, n, large)
    return [int(np.argmax(bucket >= b)) for b in range(REL_BUCKETS)]


_THR = _bucket_thresholds()


def _inproj_kernel(x_ref, g_ref, w_ref, wt_ref, qa_ref, kc_ref, vc_ref, ks_ref, kw_ref, qb_ref, kb_ref,
                   vst_ref, vwt_ref, vbt_ref, gt_ref):
    x = x_ref[...]
    xn = (x * lax.rsqrt(jnp.mean(x * x, axis=-1, keepdims=True) + EPS) * g_ref[...]).astype(BF16)
    a = _dot(xn, w_ref[:, 0:512])
    kc_ref[...] = a[:, 0:128]
    vc_ref[...] = a[:, 128:256]
    ks_ref[...] = a[:, 256:384].astype(BF16)
    kw_ref[...] = a[:, 384:512].astype(BF16)
    a = _dot(xn, w_ref[:, 512:1024])
    for r in range(4):
        kb_ref[r] = a[:, r * LANE:(r + 1) * LANE].astype(BF16)

    ft = _dot_nt(wt_ref[...], xn)

    def feat(row0, rows=LANE):
        return ft[row0:row0 + rows, :]
    for r in range(4):
        qa_ref[r] = (feat(r * LANE) * (NSA_HD ** -0.5 * LOG2E)).astype(BF16)
        qb_ref[r] = (feat(512 + r * LANE) * (DIFF_HD ** -0.5 * LOG2E)).astype(BF16)
        vbt_ref[r] = feat(1280 + r * LANE).astype(BF16)
    vst_ref[...] = feat(1024).astype(BF16)
    vwt_ref[...] = feat(1152).astype(BF16)
    gt_ref[...] = feat(1792, GATE_ROWS)


def _in_proj(x2, g, w, wt):
    T = x2.shape[0]
    tm = TM_PROJ
    row = lambda i: (i, 0)
    o128b = jax.ShapeDtypeStruct((T, LANE), BF16)
    o128f = jax.ShapeDtypeStruct((T, LANE), F32)
    o4 = jax.ShapeDtypeStruct((4, T, LANE), BF16)
    ot = jax.ShapeDtypeStruct((LANE, T), BF16)
    o4t = jax.ShapeDtypeStruct((4, LANE, T), BF16)
    s128 = pl.BlockSpec((tm, LANE), row)
    s4 = pl.BlockSpec((4, tm, LANE), lambda i: (0, i, 0))
    st = pl.BlockSpec((LANE, tm), lambda i: (0, i))
    s4t = pl.BlockSpec((4, LANE, tm), lambda i: (0, 0, i))
    return pl.pallas_call(
        _inproj_kernel,
        grid=(T // tm,),
        in_specs=[pl.BlockSpec((tm, D_MODEL), row),
                  pl.BlockSpec((1, D_MODEL), lambda i: (0, 0)),
                  pl.BlockSpec(w.shape, lambda i: (0, 0)),
                  pl.BlockSpec(wt.shape, lambda i: (0, 0))],
        out_specs=[s4t, s128, s128, s128, s128, s4t, s4, st, st, s4t, pl.BlockSpec((GATE_ROWS, tm), lambda i: (0, i))],
        out_shape=[o4t, o128f, o128f, o128b, o128b, o4t, o4, ot, ot, o4t, jax.ShapeDtypeStruct((GATE_ROWS, T), F32)],
        compiler_params=_cparams(("parallel",)),
        name="in_proj",
    )(x2, g, w, wt)


def _gelu_tanh(x):
    return 0.5 * x * (1.0 + jnp.tanh(math.sqrt(2.0 / math.pi) * (x + 0.044715 * (x * x * x))))


def _compress_kernel(kc_ref, vc_ref, posk_ref, posv_ref, w1k_ref, w1v_ref, w2k_ref, w2v_ref, ko_ref, vo_ref):
    nrow = kc_ref.shape[0] // CMP_STRIDE
    rid = lax.broadcasted_iota(jnp.int32, (nrow, 1), 0)
    cid = lax.broadcasted_iota(jnp.int32, (1, nrow), 1)
    for src, pos, w1, w2, out, transposed in ((kc_ref, posk_ref, w1k_ref, w2k_ref, ko_ref, False),
                                              (vc_ref, posv_ref, w1v_ref, w2v_ref, vo_ref, True)):
        hid_a = jnp.zeros((nrow, 2 * CMP_HIDDEN), F32)
        hid_b = jnp.zeros((nrow, 2 * CMP_HIDDEN), F32)
        for m in range(CMP_STRIDE):
            y = src[pl.ds(m, nrow, stride=CMP_STRIDE), :]
            hid_a = hid_a + _dot((y + pos[m:m + 1, :]).astype(BF16), w1[m])
            hid_b = hid_b + _dot((y + pos[CMP_STRIDE + m:CMP_STRIDE + m + 1, :]).astype(BF16), w1[CMP_STRIDE + m])
        hid = hid_a + pltpu.roll(hid_b, nrow - 1, 0)
        o = _dot(_gelu_tanh(hid).astype(BF16), w2[...])
        if transposed:
            out[0] = jnp.where(cid < nrow - 1, o.T, 0.0).astype(BF16)
        else:
            out[0] = jnp.where(rid < nrow - 1, o, 0.0).astype(BF16)


def _compress(kc, vc, posk, posv, w1k, w1v, w2k, w2v, B, S):
    nrow = S // CMP_STRIDE
    assert nrow == LANE
    full = lambda a: pl.BlockSpec(a.shape, lambda b: (0,) * a.ndim)
    src = pl.BlockSpec((S, LANE), lambda b: (b, 0))
    osp = pl.BlockSpec((1, nrow, LANE), lambda b: (b, 0, 0))
    osh = jax.ShapeDtypeStruct((B, nrow, LANE), BF16)
    return pl.pallas_call(
        _compress_kernel,
        grid=(B,),
        in_specs=[src, src, full(posk), full(posv), full(w1k), full(w1v), full(w2k), full(w2v)],
        out_specs=[osp, osp],
        out_shape=[osh, osh],
        compiler_params=_cparams(("parallel",)),
        name="compress",
    )(kc, vc, posk, posv, w1k, w1v, w2k, w2v)


def _bias_from_dist(dist, tab_ref, h):
    val = jnp.full(dist.shape, tab_ref[h, 0], F32)
    for b in range(1, REL_BUCKETS):
        val = jnp.where(dist >= _THR[b], tab_ref[h, b], val)
    return val


def _bias_near_kernel(tab_ref, out_ref):
    h = pl.program_id(0)
    j = lax.broadcasted_iota(jnp.int32, (TQ, TQ), 0)
    i = lax.broadcasted_iota(jnp.int32, (TQ, TQ), 1)
    for d in range(2):
        bias = (_bias_from_dist(i - j + d * TQ, tab_ref, h) - tab_ref[h, REL_BUCKETS - 1]) * LOG2E
        out_ref[0, d] = jnp.where(i - j + d * TQ >= 0, bias, NEG)


def _bias_cmp_kernel(tab_ref, out_ref):
    h = pl.program_id(0)
    c = lax.broadcasted_iota(jnp.int32, (LANE, TQ), 0)
    t = pl.program_id(1) * TQ + lax.broadcasted_iota(jnp.int32, (LANE, TQ), 1)
    out_ref[0] = _bias_from_dist(t - (c * CMP_STRIDE + CMP_LEN - 1), tab_ref, h) * LOG2E


def _bias_tiles(tab_t, S):
    smem = pl.BlockSpec(memory_space=pltpu.SMEM)
    near = pl.pallas_call(
        _bias_near_kernel,
        grid=(N_REL_HEADS,),
        in_specs=[smem],
        out_specs=pl.BlockSpec((1, 2, TQ, TQ), lambda h: (h, 0, 0, 0)),
        out_shape=jax.ShapeDtypeStruct((N_REL_HEADS, 2, TQ, TQ), F32),
        compiler_params=_cparams(("parallel",)),
        name="bias_near",
    )(tab_t)
    cmp_bias = pl.pallas_call(
        _bias_cmp_kernel,
        grid=(NSA_HEADS, S // TQ),
        in_specs=[smem],
        out_specs=pl.BlockSpec((1, LANE, TQ), lambda h, q: (h, 0, q)),
        out_shape=jax.ShapeDtypeStruct((NSA_HEADS, LANE, S), F32),
        compiler_params=_cparams(("parallel", "parallel")),
        name="bias_cmp",
    )(tab_t)
    return near, cmp_bias


ACC_ROWS = NSA_HD + 16


def _with_ones(vt):
    return jnp.concatenate([vt, jnp.ones((ACC_ROWS - vt.shape[0], vt.shape[1]), BF16)], axis=0)


def _probs(s, m):
    return jnp.exp2((s - m).astype(BF16))


def _flash_first(ss, vts):
    ms = [jnp.max(s, axis=0, keepdims=True) for s in ss]
    ps = [_probs(s, m) for s, m in zip(ss, ms)]
    return tuple((m, _dot(vt, p)) for m, p, vt in zip(ms, ps, vts))


def _flash_update(ss, vts, sts):
    ms = [jnp.maximum(st[0], jnp.max(s, axis=0, keepdims=True)) for s, st in zip(ss, sts)]
    alphas = [jnp.exp2(st[0] - m) for m, st in zip(ms, sts)]
    ps = [_probs(s, m) for s, m in zip(ss, ms)]
    return tuple((m, a * st[1] + _dot(vt, p)) for m, a, st, p, vt in zip(ms, alphas, sts, ps, vts))


def _flash_out(st):
    acc = st[1]
    return acc[0:NSA_HD, :] * (1.0 / acc[NSA_HD:NSA_HD + 1, :])


def _stream_scratch(n):
    return ([pltpu.VMEM((8, TQ), F32), pltpu.VMEM((ACC_ROWS, TQ), F32)] * n
            + [pltpu.VMEM((TQ, TQ), F32)] * (2 * n) + [pltpu.VMEM((TQ, TQ), BF16)] * (2 * n))


def _stream_refs(refs, n):
    st_refs = tuple((refs[2 * c], refs[2 * c + 1]) for c in range(n))
    s, p = refs[2 * n:4 * n], refs[4 * n:6 * n]
    return st_refs, tuple(s[:n]), tuple(s[n:]), tuple(p[:n]), tuple(p[n:])


def _causal_stream(qi, scores, vtile, st_refs, s_a, s_b, p_a, p_b, diag_ready=False):
    n_far = jnp.maximum(qi - 1, 0)
    top = qi - 2

    chains = range(len(st_refs))

    def write(s_refs, kt, near=2):
        for c in chains:
            s_refs[c][...] = scores(jnp.maximum(kt, 0), near, c)

    def pending(p_refs, kt):
        return [_dot(vtile(jnp.maximum(kt, 0), c), p_refs[c][...]) for c in chains]

    def step(s_cur, p_cur, s_next, kt_next, p_prev, kt_prev):
        rescale = []
        for c in chains:
            s_next[c][...] = scores(jnp.maximum(kt_next, 0), 2, c)
            pv = _dot(vtile(jnp.maximum(kt_prev, 0), c), p_prev[c][...])
            st = st_refs[c][0]
            s = s_cur[c][...]
            m_old = st[0:1, :]
            m = jnp.maximum(m_old, jnp.max(s, axis=0, keepdims=True))
            p_cur[c][...] = _probs(s, m)
            st[0:1, :] = m
            rescale.append((jnp.exp2(m_old - m), pv))
        for c in chains:
            acc = st_refs[c][1]
            acc[...] = rescale[c][0] * (acc[...] + rescale[c][1])

    if not diag_ready:
        write(s_a, qi, 0)
    for c in chains:
        s_b[c][...] = scores(jnp.maximum(qi - 1, 0), 1, c)
        st, acc = st_refs[c]
        s = s_a[c][...]
        m = jnp.max(s, axis=0, keepdims=True)
        p_a[c][...] = _probs(s, m)
        st[0:1, :] = m
        acc[...] = jnp.zeros(acc.shape, F32)
    step(s_b, p_b, s_a, top, p_a, qi)

    def pair(j, c):
        kt = top - 2 * j
        step(s_a, p_a, s_b, kt - 1, p_b, kt + 1)
        step(s_b, p_b, s_a, kt - 2, p_a, kt)
        return c
    lax.fori_loop(0, n_far // 2, pair, 0)

    @pl.when(n_far % 2 == 1)
    def _():
        for c, pv in zip(chains, pending(p_b, 1)):
            st, acc = st_refs[c]
            s = s_a[c][...]
            m_old = st[0:1, :]
            m = jnp.maximum(m_old, jnp.max(s, axis=0, keepdims=True))
            st[0:1, :] = m
            acc[...] = jnp.exp2(m_old - m) * (acc[...] + pv) + _dot(vtile(0, c), _probs(s, m))

    @pl.when(n_far % 2 == 0)
    def _():
        last = jnp.where(n_far == 0, qi - 1, 0)
        for c, pv in zip(chains, pending(p_b, last)):
            acc = st_refs[c][1]
            acc[...] = acc[...] + pv
    return tuple((st[0:1, :], acc[...]) for st, acc in st_refs)


def _ktile(ref, kt):
    return ref[pl.ds(pl.multiple_of(kt * TQ, TQ), TQ), :]


def _vtile(ref, kt):
    return ref[:, pl.ds(pl.multiple_of(kt * TQ, TQ), TQ)]


def _nsa_kernel(qa_ref, gt_ref, kcmp_ref, vcmpt_ref, bc_ref, ks_ref, vst_ref, kw_ref, vwt_ref, dn_ref,
                ov_ref, o_ref, psum_ref, sel_ref, oacc_ref, sig_ref, *scratch):
    st_refs, s_a, s_b, p_a, p_b = _stream_refs(scratch, NSA_BLOCKS_PER_STEP * NSA_G)
    qi = pl.program_id(1)
    t0 = qi * TQ
    sub_grp = lax.shift_right_arithmetic(lax.broadcasted_iota(jnp.int32, (LANE, 1), 0), 6)
    sig_ref[...] = jax.nn.sigmoid(gt_ref[...])

    def gate_row(c):
        return sig_ref[pl.ds(c, 1), :]

    def masked_q(r, g):
        return jnp.where(sub_grp == g, qa_ref[r].astype(F32), 0.0).astype(BF16)

    n_cmp = kcmp_ref.shape[1] - 1
    crow = lax.broadcasted_iota(jnp.int32, (LANE, 1), 0)
    cmp_end = jnp.where(crow < n_cmp, crow * CMP_STRIDE + CMP_LEN - 1, 1 << 30)
    mask_c = (t0 + lax.broadcasted_iota(jnp.int32, (1, TQ), 1)) >= cmp_end

    heads = [(r, g) for r in range(NSA_R) for g in range(NSA_G)]
    scores_c = [_dot(kcmp_ref[0], masked_q(r, g)) for r, g in heads]
    probs_c = []
    for (r, g), s in zip(heads, scores_c):
        s = jnp.where(mask_c, s + bc_ref[g * NSA_R + r], NEG)
        p = jnp.where(mask_c, jnp.exp2(s - jnp.max(s, axis=0, keepdims=True)), 0.0)
        l = jnp.sum(p, axis=0, keepdims=True)
        probs_c.append(p * jnp.where(l > 0.0, 1.0 / l, 0.0))
    for g in range(NSA_G):
        psum_ref[g] = sum(p for (r, gg), p in zip(heads, probs_c) if gg == g)
    outs_c = [_dot(vcmpt_ref[0], p.astype(BF16)) for p in probs_c]
    for r in range(NSA_R):
        o0, o1 = outs_c[r * NSA_G], outs_c[r * NSA_G + 1]
        oacc_ref[r] = jnp.where(sub_grp == 0, gate_row(r * 3) * o0, gate_row((NSA_R + r) * 3) * o1)

    n_sel = ov_ref.shape[0]
    jj = lax.broadcasted_iota(jnp.int32, (n_sel, TQ), 0)
    cur = lax.shift_right_arithmetic(t0 + lax.broadcasted_iota(jnp.int32, (n_sel, TQ), 1), 6)
    valid = jj <= cur
    forced = (jj == 0) | (cur - jj < SEL_FORCED_LOCAL)
    for g in range(NSA_G):
        ps = psum_ref[g]
        hi = ps.astype(BF16)
        rem = ps - hi.astype(F32)
        mid = rem.astype(BF16)
        lo = (rem - mid.astype(F32)).astype(BF16)
        ov = ov_ref[...]
        imp = _dot(ov, hi) + _dot(ov, mid) + _dot(ov, lo)
        score = jnp.where(valid, jnp.where(forced, 1e9, imp), -1e9)
        cnt = jnp.zeros((n_sel, TQ), F32)
        for j2 in range(n_sel):
            row = score[j2:j2 + 1, :]
            tie = jnp.where(jj > j2, 1.0, 0.0)
            cnt = cnt + jnp.where(row > score, 1.0, jnp.where(row == score, tie, 0.0))
        sel_ref[g] = jnp.where(cnt < float(min(SEL_TOPN, n_sel)), jnp.where(score > -1e8, 0.0, NEG), NEG)

    has_prev = jnp.where(qi >= 1, 0.0, NEG)
    has_wfar = jnp.where(qi >= WINDOW // TQ, 0.0, NEG)
    kt_prev = jnp.maximum(qi - 1, 0)
    kt_wfar = jnp.maximum(qi - WINDOW // TQ, 0)
    wfar_mask = lax.broadcasted_iota(jnp.int32, (TQ, TQ), 0) > lax.broadcasted_iota(jnp.int32, (TQ, TQ), 1)

    def main_body(rp, carry):
        chains = [(NSA_BLOCKS_PER_STEP * rp + i, g) for i in range(NSA_BLOCKS_PER_STEP) for g in range(NSA_G)]
        qms = [masked_q(r, g) for r, g in chains]
        hids = [g * NSA_R + r for r, g in chains]

        def block_mask(g, kt):
            per_tile = TQ // SEL_BLOCK
            rows = [jnp.broadcast_to(sel_ref[g, pl.ds(kt * per_tile + i, 1), :], (SEL_BLOCK, TQ))
                    for i in range(per_tile)]
            return jnp.concatenate(rows, axis=0)

        def near_bias(s, near, c):
            if near == 0:
                return s + dn_ref[hids[c], 0]
            if near == 1:
                return s + dn_ref[hids[c], 1] + has_prev
            return s

        def group_values(ref, kt, c):
            g = chains[c][1]
            return _with_ones(ref[g * NSA_HD:(g + 1) * NSA_HD, pl.ds(pl.multiple_of(kt * TQ, TQ), TQ)])

        def finish(sts, branch):
            for c, (r, g) in enumerate(chains):
                oacc_ref[r, g * NSA_HD:(g + 1) * NSA_HD, :] += gate_row(hids[c] * 3 + branch) * _flash_out(sts[c])

        def slc_scores(kt, near, c):
            return near_bias(_dot(_ktile(ks_ref, kt), qms[c]), near, c) + block_mask(chains[c][1], kt)

        def slc_vtile(kt, c):
            return group_values(vst_ref, kt, c)

        finish(_causal_stream(qi, slc_scores, slc_vtile, st_refs, s_a, s_b, p_a, p_b), 1)

        cs = range(len(chains))
        win = []
        for kt, near in ((qi, 0), (kt_prev, 1), (kt_wfar, 2)):
            k = _ktile(kw_ref, kt)
            ss = [near_bias(_dot(k, qms[c]), near, c) for c in cs]
            if near == 2:
                ss = [jnp.where(wfar_mask, s + has_wfar, NEG) for s in ss]
            win.append(ss)
        sts = _flash_first(win[0], [group_values(vwt_ref, qi, c) for c in cs])
        sts = _flash_update(win[1], [group_values(vwt_ref, kt_prev, c) for c in cs], sts)
        sts = _flash_update(win[2], [group_values(vwt_ref, kt_wfar, c) for c in cs], sts)
        finish(sts, 2)
        return carry
    lax.fori_loop(0, NSA_R // NSA_BLOCKS_PER_STEP, main_body, 0)

    for r in range(NSA_R):
        o_ref[r] = oacc_ref[r].T.astype(BF16)


def _nsa(qa, gates_t, kcmp, vcmpt, bias_c, ks, vst, kw, vwt, near_a, ov, B, S):
    nq = S // TQ
    T = B * S
    k_spec = pl.BlockSpec((S, LANE), lambda b, q: (b, 0))
    v_spec = pl.BlockSpec((LANE, S), lambda b, q: (0, b))
    full = lambda a: pl.BlockSpec(a.shape, lambda b, q: (0,) * a.ndim)
    cmp_spec = pl.BlockSpec((1, LANE, LANE), lambda b, q: (b, 0, 0))
    qo = pl.BlockSpec((4, TQ, LANE), lambda b, q: (0, b * nq + q, 0))
    return pl.pallas_call(
        _nsa_kernel,
        grid=(B, nq),
        in_specs=[pl.BlockSpec((4, LANE, TQ), lambda b, q: (0, 0, b * nq + q)),
                  pl.BlockSpec((GATE_ROWS, TQ), lambda b, q: (0, b * nq + q)),
                  cmp_spec, cmp_spec,
                  pl.BlockSpec((NSA_HEADS, LANE, TQ), lambda b, q: (0, 0, q)),
                  k_spec, v_spec, k_spec, v_spec, full(near_a), full(ov)],
        out_specs=qo,
        out_shape=jax.ShapeDtypeStruct((4, T, LANE), BF16),
        scratch_shapes=[pltpu.VMEM((NSA_G, LANE, TQ), F32), pltpu.VMEM((NSA_G, S // SEL_BLOCK, TQ), F32),
                        pltpu.VMEM((NSA_R, LANE, TQ), F32), pltpu.VMEM((GATE_ROWS, TQ), F32)]
                       + _stream_scratch(NSA_BLOCKS_PER_STEP * NSA_G),
        compiler_params=_cparams(("parallel", "arbitrary")),
        name="nsa",
    )(qa, gates_t, kcmp, vcmpt, bias_c, ks, vst, kw, vwt, near_a, ov)


def _diff_kernel(lq1_ref, lk1_ref, lq2_ref, lk2_ref, sub_ref, qb_ref, kb_ref, vbt_ref, dn_ref, o_ref, *scratch):
    n = DIFF_PAIRS_PER_STEP * 4
    st_refs, s_a, s_b, p_a, p_b = _stream_refs(scratch, n)
    qi = pl.program_id(1)
    chain_of_row = lax.shift_right_arithmetic(lax.broadcasted_iota(jnp.int32, (LANE, 1), 0), 5)
    lam = (jnp.exp(jnp.sum(lq1_ref[...] * lk1_ref[...], axis=-1, keepdims=True))
           - jnp.exp(jnp.sum(lq2_ref[...] * lk2_ref[...], axis=-1, keepdims=True)) + LAMBDA_INIT)
    has_prev = jnp.where(qi >= 1, 0.0, NEG)
    n_steps = DIFF_HEADS // 2 // DIFF_PAIRS_PER_STEP

    def score_fn(step):
        pairs = [step * DIFF_PAIRS_PER_STEP + c // 4 for c in range(n)]
        qs = [qb_ref[step * DIFF_PAIRS_PER_STEP + i].astype(F32) for i in range(DIFF_PAIRS_PER_STEP)]
        qms = [jnp.where(chain_of_row == c % 4, qs[c // 4], 0.0).astype(BF16) for c in range(n)]

        def scores(kt, near, c):
            s = _dot(kb_ref[pairs[c], pl.ds(pl.multiple_of(kt * TQ, TQ), TQ), :], qms[c])
            if near == 2:
                return s
            bias = dn_ref[2 * pairs[c] + (c % 4) // 2, near]
            return s + bias if near == 0 else s + bias + has_prev
        return scores

    def start_diag(step):
        scores = score_fn(step)
        for c in range(n):
            s_a[c][...] = scores(qi, 0, c)

    start_diag(0)

    def step_body(step, carry):
        def vtile(kt, c):
            hh = (c % 4) // 2
            return _with_ones(vbt_ref[step * DIFF_PAIRS_PER_STEP + c // 4, hh * 2 * DIFF_HD:(hh + 1) * 2 * DIFF_HD,
                                      pl.ds(pl.multiple_of(kt * TQ, TQ), TQ)])

        sts = _causal_stream(qi, score_fn(step), vtile, st_refs, s_a, s_b, p_a, p_b, diag_ready=True)
        start_diag(jnp.minimum(step + 1, n_steps - 1))
        for i in range(DIFF_PAIRS_PER_STEP):
            outs = []
            for hh in range(2):
                c = 4 * i + 2 * hh
                o = _flash_out(sts[c]) - lam * _flash_out(sts[c + 1])
                outs.append(o * lax.rsqrt(jnp.mean(o * o, axis=0, keepdims=True) + EPS))
            out = jnp.concatenate(outs, axis=0) * sub_ref[...] * (1.0 - LAMBDA_INIT)
            o_ref[step * DIFF_PAIRS_PER_STEP + i] = out.T.astype(BF16)
        return carry
    lax.fori_loop(0, n_steps, step_body, 0)


def _diff(lq1, lk1, lq2, lk2, sub, qb, kb, vbt, near_b, B, S):
    nq = S // TQ
    T = B * S
    full = lambda a: pl.BlockSpec(a.shape, lambda b, q: (0,) * a.ndim)
    k_spec = pl.BlockSpec((4, S, LANE), lambda b, q: (0, b, 0))
    v_spec = pl.BlockSpec((4, LANE, S), lambda b, q: (0, 0, b))
    qo = pl.BlockSpec((4, TQ, LANE), lambda b, q: (0, b * nq + q, 0))
    return pl.pallas_call(
        _diff_kernel,
        grid=(B, nq),
        in_specs=[full(lq1), full(lk1), full(lq2), full(lk2), full(sub),
                  pl.BlockSpec((4, LANE, TQ), lambda b, q: (0, 0, b * nq + q)), k_spec, v_spec, full(near_b)],
        out_specs=qo,
        out_shape=jax.ShapeDtypeStruct((4, T, LANE), BF16),
        scratch_shapes=_stream_scratch(DIFF_PAIRS_PER_STEP * 4),
        compiler_params=_cparams(("parallel", "arbitrary")),
        name="diff",
    )(lq1, lk1, lq2, lk2, sub, qb, kb, vbt, near_b)


def _outproj_kernel(x_ref, oa_ref, ob_ref, w_ref, g_ref, wr_ref, br_ref, h_ref, xt_ref, route_ref, cnt_ref,
                    carry_ref):
    @pl.when(pl.program_id(0) == 0)
    def _():
        carry_ref[...] = jnp.zeros(carry_ref.shape, F32)

    o = jnp.concatenate([oa_ref[r] for r in range(4)] + [ob_ref[r] for r in range(4)], axis=-1)
    h = x_ref[...] + _dot(o, w_ref[...])
    h_ref[...] = h
    tn32 = h * lax.rsqrt(jnp.mean(h * h, axis=-1, keepdims=True) + EPS) * g_ref[...]
    xt_ref[:, 0:D_MODEL] = tn32
    tn = tn32.astype(BF16)
    logits = _dot(tn, wr_ref[...]) + br_ref[...]
    lane = lax.broadcasted_iota(jnp.int32, (1, LANE), 1)
    lane_f = lane.astype(F32)
    is_grp = lane < MOE_GROUPS
    lg = jnp.where(is_grp, logits, NEG)
    mg = jnp.max(lg, axis=-1, keepdims=True)
    zg = jnp.sum(jnp.where(is_grp, jnp.exp(lg - mg), 0.0), axis=-1, keepdims=True)
    g_prob = 1.0 / zg
    g_idx = jnp.min(jnp.where(lg == mg, lane_f, 1e9), axis=-1, keepdims=True)
    lane_grp = jnp.where((lane >= MOE_GROUPS) & (lane < MOE_GROUPS + N_EXPERTS),
                         lax.shift_right_arithmetic(lane - MOE_GROUPS, 3), -1).astype(F32)
    le = jnp.where(lane_grp == g_idx, logits, NEG)
    m1 = jnp.max(le, axis=-1, keepdims=True)
    e1 = jnp.min(jnp.where(le == m1, lane_f, 1e9), axis=-1, keepdims=True)
    le2 = jnp.where(lane_f == e1, NEG, le)
    m2 = jnp.max(le2, axis=-1, keepdims=True)
    e2 = jnp.min(jnp.where(le2 == m2, lane_f, 1e9), axis=-1, keepdims=True)
    ratio = jnp.exp(m2 - m1)
    w1 = g_prob / (1.0 + ratio)
    w2 = w1 * ratio
    xt_ref[:, D_MODEL:D_MODEL + LANE] = jnp.where(lane_f == e1, w1, 0.0) + jnp.where(lane_f == e2, w2, 0.0)

    tm = h.shape[0]
    onehot = jnp.where(lane_f == g_idx, 1.0, 0.0)
    earlier = jnp.where(lax.broadcasted_iota(jnp.int32, (tm, tm), 0) > lax.broadcasted_iota(jnp.int32, (tm, tm), 1),
                        1.0, 0.0).astype(BF16)
    prefix = _dot(earlier, onehot.astype(BF16)) + carry_ref[...]
    rank = jnp.sum(onehot * prefix, axis=-1, keepdims=True)
    carry_ref[...] += jnp.sum(onehot, axis=0, keepdims=True)
    route_ref[...] = jnp.where(lane == 0, g_idx, jnp.where(lane == 1, rank, 0.0))
    cnt_ref[...] = jnp.broadcast_to(carry_ref[...], cnt_ref.shape)


def _out_proj(x2, oa, ob, w, g, wr, br):
    T = x2.shape[0]
    tm = TM_PROJ
    row = lambda i: (i, 0)
    full = lambda a: pl.BlockSpec(a.shape, lambda i: (0,) * a.ndim)
    o4 = pl.BlockSpec((4, tm, LANE), lambda i: (0, i, 0))
    return pl.pallas_call(
        _outproj_kernel,
        grid=(T // tm,),
        in_specs=[pl.BlockSpec((tm, D_MODEL), row), o4, o4, full(w), full(g), full(wr), full(br)],
        out_specs=[pl.BlockSpec((tm, D_MODEL), row), pl.BlockSpec((tm, XT_WIDTH), row), pl.BlockSpec((tm, LANE), row),
                   pl.BlockSpec((8, LANE), lambda i: (0, 0))],
        out_shape=[jax.ShapeDtypeStruct((T, D_MODEL), F32), jax.ShapeDtypeStruct((T, XT_WIDTH), F32),
                   jax.ShapeDtypeStruct((T, LANE), F32), jax.ShapeDtypeStruct((8, LANE), F32)],
        scratch_shapes=[pltpu.VMEM((1, LANE), F32)],
        compiler_params=_cparams(("arbitrary",)),
        name="out_proj",
    )(x2, oa, ob, w, g, wr, br)


def _moe_kernel(tg_ref, ok_ref, idx_ref, idxn_ref, xt_hbm, wg_ref, wu_ref, wd_ref, y_ref, xbuf, sem):
    i = pl.program_id(0)
    slot = i % 2
    tm = xbuf.shape[1]

    def row_copy(index_ref, r, dst_slot):
        return pltpu.make_async_copy(xt_hbm.at[pl.ds(index_ref[0, 0, r], 1), :],
                                     xbuf.at[dst_slot, pl.ds(r, 1), :], sem.at[dst_slot])

    @pl.when(i == 0)
    def _():
        def body(r, c):
            row_copy(idx_ref, r, 0).start()
            return c
        lax.fori_loop(0, tm, body, 0)

    @pl.when((i == 0) | (ok_ref[jnp.maximum(i - 1, 0)] == 1))
    def _():
        pltpu.make_async_copy(xt_hbm.at[pl.ds(0, tm), :], xbuf.at[slot], sem.at[slot]).wait()

    @pl.when(ok_ref[i] == 0)
    def _():
        y_ref[...] = jnp.zeros(y_ref.shape, F32)

    @pl.when(ok_ref[i] == 1)
    def _():
        lane = lax.broadcasted_iota(jnp.int32, (1, LANE), 1)
        x = xbuf[slot, :, 0:D_MODEL].astype(BF16)
        cmb = xbuf[slot, :, D_MODEL:D_MODEL + LANE]
        first_lane = MOE_GROUPS + tg_ref[i] * EPG
        per_expert = tm // EPG
        y = jnp.zeros((tm, D_MODEL), F32)
        for e in range(EPG):
            a = _dot(x, wg_ref[0, e])
            b = _dot(x, wu_ref[0, e])
            ce = jnp.sum(jnp.where(lane == first_lane + e, cmb, 0.0), axis=-1, keepdims=True)
            y = y + _dot(((a * jax.nn.sigmoid(a)) * b * ce).astype(BF16), wd_ref[0, e])
            for r in range(e * per_expert, (e + 1) * per_expert):
                row_copy(idxn_ref, r, 1 - slot).start(priority=r % 2)
        for blk in range(ROW_SLABS):
            y_ref[:, blk, :] = y[:, blk * LANE:(blk + 1) * LANE]


def _moe(tile_group, tile_ok, src3, xt, wg, wu, wd):
    n_tiles, _, tm = src3.shape
    last = n_tiles - 1
    grid_spec = pltpu.PrefetchScalarGridSpec(
        num_scalar_prefetch=2,
        grid=(n_tiles,),
        in_specs=[pl.BlockSpec((1, 1, tm), lambda i, tg, ok: (i, 0, 0), memory_space=pltpu.SMEM),
                  pl.BlockSpec((1, 1, tm), lambda i, tg, ok: (jnp.minimum(i + 1, last), 0, 0), memory_space=pltpu.SMEM),
                  pl.BlockSpec(memory_space=pl.ANY),
                  pl.BlockSpec((1, EPG, D_MODEL, EXPERT_FF), lambda i, tg, ok: (tg[i], 0, 0, 0)),
                  pl.BlockSpec((1, EPG, D_MODEL, EXPERT_FF), lambda i, tg, ok: (tg[i], 0, 0, 0)),
                  pl.BlockSpec((1, EPG, EXPERT_FF, D_MODEL), lambda i, tg, ok: (tg[i], 0, 0, 0))],
        out_specs=pl.BlockSpec((tm, ROW_SLABS, LANE), lambda i, tg, ok: (i, 0, 0)),
        scratch_shapes=[pltpu.VMEM((2, tm, XT_WIDTH), F32), pltpu.SemaphoreType.DMA((2,))],
    )
    return pl.pallas_call(
        _moe_kernel,
        grid_spec=grid_spec,
        out_shape=jax.ShapeDtypeStruct((n_tiles * tm, ROW_SLABS, LANE), F32),
        compiler_params=pltpu.CompilerParams(dimension_semantics=("arbitrary",), vmem_limit_bytes=VMEM_LIMIT_MOE),
        name="moe",
    )(tile_group, tile_ok, src3, src3, xt, wg, wu, wd)


def _final_kernel(pos_ref, posn_ref, y_hbm, h_ref, gf_ref, o_ref, ybuf, sem):
    i = pl.program_id(0)
    slot = i % 2
    tm = h_ref.shape[0]

    def issue(index_ref, dst_slot):
        group = 8

        def body(j, c):
            for k in range(group):
                r = j * group + k
                pltpu.make_async_copy(y_hbm.at[index_ref[0, 0, r]], ybuf.at[dst_slot, pl.ds(r * ROW_SLABS, ROW_SLABS), :],
                                      sem.at[dst_slot]).start(priority=k % 2)
            return c
        lax.fori_loop(0, tm // group, body, 0)

    @pl.when(i == 0)
    def _():
        issue(pos_ref, 0)

    @pl.when(i + 1 < pl.num_programs(0))
    def _():
        issue(posn_ref, 1 - slot)

    pltpu.make_async_copy(ybuf.at[slot], ybuf.at[slot], sem.at[slot]).wait()
    y = jnp.concatenate([ybuf[slot, pl.ds(blk, tm, stride=ROW_SLABS), :] for blk in range(ROW_SLABS)], axis=-1)
    h = h_ref[...] + y
    o_ref[...] = h * lax.rsqrt(jnp.mean(h * h, axis=-1, keepdims=True) + EPS) * gf_ref[...]


def _final(pos3, y_sorted, h, gf):
    n_tiles, _, tm = pos3.shape
    last = n_tiles - 1
    row = lambda i: (i, 0)
    return pl.pallas_call(
        _final_kernel,
        grid=(n_tiles,),
        in_specs=[pl.BlockSpec((1, 1, tm), lambda i: (i, 0, 0), memory_space=pltpu.SMEM),
                  pl.BlockSpec((1, 1, tm), lambda i: (jnp.minimum(i + 1, last), 0, 0), memory_space=pltpu.SMEM),
                  pl.BlockSpec(memory_space=pl.ANY),
                  pl.BlockSpec((tm, D_MODEL), row),
                  pl.BlockSpec((1, D_MODEL), lambda i: (0, 0))],
        out_specs=pl.BlockSpec((tm, D_MODEL), row),
        out_shape=jax.ShapeDtypeStruct((n_tiles * tm, D_MODEL), F32),
        scratch_shapes=[pltpu.VMEM((2, tm * ROW_SLABS, LANE), F32), pltpu.SemaphoreType.DMA((2,))],
        compiler_params=_cparams(("arbitrary",)),
        name="final",
    )(pos3, pos3, y_sorted, h, gf)


def _qa_perm():
    new = np.arange(NSA_HEADS * NSA_HD)
    r, g, d = new // LANE, (new % LANE) // NSA_HD, new % NSA_HD
    return (g * NSA_R + r) * NSA_HD + d


def _block_diag2(w):
    z = jnp.zeros_like(w)
    return jnp.concatenate([jnp.concatenate([w, z], axis=-1), jnp.concatenate([z, w], axis=-1)], axis=-2)


def kernel(x, rel_bias, ln_mix, w_in, cmp_pos_k, cmp_pos_v, cmp_k_w1, cmp_k_w2, cmp_v_w1, cmp_v_w2,
           diff_lq1, diff_lk1, diff_lq2, diff_lk2, diff_subln, w_out, ln_ffn,
           router_group_w, router_group_b, router_expert_w, router_expert_b,
           exp_w_gate, exp_w_up, exp_w_down, ln_final):
    B, S, D = x.shape
    T = B * S
    assert D == D_MODEL and S % TQ == 0 and S >= WINDOW and T % TM_MOE == 0 and T % TM_FINAL == 0
    x2 = x.reshape(T, D)
    perm = _qa_perm()

    w = w_in[0]
    c_kc, c_vc, c_ks, c_vs, c_kw, c_vw, c_gt = 512, 640, 768, 896, 1024, 1152, 1280
    c_qb = c_gt + N_GATE
    c_kb, c_vb = c_qb + 512, c_qb + 1024
    col = lambda c, n=LANE: w[:, c:c + n]
    w_tok = jnp.concatenate([col(c_kc), col(c_vc), col(c_ks), col(c_kw), col(c_kb, 512)], axis=1).astype(BF16)
    w_feat = jnp.concatenate([w[:, perm], col(c_qb, 512), col(c_vs), col(c_vw), col(c_vb, 512), col(c_gt, N_GATE),
                              jnp.zeros((D, GATE_ROWS - N_GATE), F32)], axis=1).T.astype(BF16)
    qa, kc, vc, ks, kw, qb, kb, vst, vwt, vbt, gates_t = _in_proj(x2, ln_mix[0][None, :], w_tok, w_feat)

    w1k = _block_diag2(cmp_k_w1[0].reshape(CMP_LEN, NSA_HD, CMP_HIDDEN)).astype(BF16)
    w1v = _block_diag2(cmp_v_w1[0].reshape(CMP_LEN, NSA_HD, CMP_HIDDEN)).astype(BF16)
    w2k = _block_diag2(cmp_k_w2[0]).astype(BF16)
    w2v = _block_diag2(cmp_v_w2[0]).astype(BF16)
    posk = jnp.tile(cmp_pos_k[0], (1, NSA_G))
    posv = jnp.tile(cmp_pos_v[0], (1, NSA_G))
    kcmp, vcmpt = _compress(kc, vc, posk, posv, w1k, w1v, w2k, w2v, B, S)

    near, bias_c = _bias_tiles(rel_bias.T, S)

    n_sel = S // SEL_BLOCK
    nrow = S // CMP_STRIDE
    c_start = np.arange(nrow) * CMP_STRIDE
    s_start = np.arange(n_sel) * SEL_BLOCK
    ov = ((c_start[None, :] <= s_start[:, None] + SEL_BLOCK - 1)
          & (c_start[None, :] + CMP_LEN - 1 >= s_start[:, None])
          & (np.arange(nrow)[None, :] < nrow - 1)).astype(np.float32)
    o_a = _nsa(qa, gates_t, kcmp, vcmpt, bias_c, ks, vst, kw, vwt, near[:NSA_HEADS], jnp.asarray(ov, BF16), B, S)

    sub = jnp.tile(diff_subln[0], 2)[:, None]
    o_b = _diff(diff_lq1[0][None, :], diff_lk1[0][None, :], diff_lq2[0][None, :], diff_lk2[0][None, :],
                sub, qb, kb, vbt, near[NSA_HEADS:], B, S)

    w_o = jnp.concatenate([w_out[0][:512][perm], w_out[0][512:]], axis=0).astype(BF16)
    n_r = MOE_GROUPS + N_EXPERTS
    wr = jnp.concatenate([router_group_w[0], router_expert_w[0], jnp.zeros((D, LANE - n_r), F32)], axis=1).astype(BF16)
    br = jnp.concatenate([router_group_b[0], router_expert_b[0], jnp.zeros((LANE - n_r,), F32)])[None, :]
    h1, xt, route, counts = _out_proj(x2, o_a, o_b, w_o, ln_ffn[0][None, :], wr, br)

    tm = TM_MOE
    n_tiles = T // tm + MOE_GROUPS
    cnt = counts[0, :MOE_GROUPS].astype(jnp.int32)
    ends = jnp.cumsum((cnt + tm - 1) // tm * tm)
    starts = ends - (cnt + tm - 1) // tm * tm
    pos = starts[route[:, 0].astype(jnp.int32)] + route[:, 1].astype(jnp.int32)
    src = jnp.zeros((n_tiles * tm,), jnp.int32).at[pos].set(
        jnp.arange(T, dtype=jnp.int32), unique_indices=True, mode="promise_in_bounds")
    tile_start = jnp.arange(n_tiles, dtype=jnp.int32) * tm
    tile_group = jnp.minimum(jnp.searchsorted(ends, tile_start, side="right"), MOE_GROUPS - 1).astype(jnp.int32)
    tile_ok = (tile_start < ends[-1]).astype(jnp.int32)

    by_group = lambda a: a[0].astype(BF16).reshape((MOE_GROUPS, EPG) + a.shape[2:])
    y_sorted = _moe(tile_group, tile_ok, src.reshape(n_tiles, 1, tm), xt,
                    by_group(exp_w_gate), by_group(exp_w_up), by_group(exp_w_down))
    out = _final(pos.reshape(T // TM_FINAL, 1, TM_FINAL), y_sorted, h1, ln_final[None, :])
    return out.reshape(B, S, D)
```

```python
import math

import numpy as np
import jax
import jax.numpy as jnp
from jax import lax
from jax.experimental import pallas as pl
from jax.experimental.pallas import tpu as pltpu

F32 = jnp.float32
BF16 = jnp.bfloat16
NEG = -1e30
EPS = 1e-6
LOG2E = math.log2(math.e)

D_MODEL = 1024
LANE = 128
NSA_HEADS, NSA_G, NSA_R, NSA_HD = 8, 2, 4, 64
CMP_LEN, CMP_STRIDE, CMP_HIDDEN = 32, 16, 128
SEL_BLOCK, SEL_TOPN, SEL_FORCED_LOCAL, WINDOW = 64, 8, 2, 512
DIFF_HEADS, DIFF_HD = 8, 32
REL_BUCKETS, REL_MAX_EXACT, REL_MAX_DIST = 32, 16, 128
N_REL_HEADS = NSA_HEADS + DIFF_HEADS
MOE_GROUPS, EPG, N_EXPERTS, EXPERT_FF = 4, 8, 32, 256
LAMBDA_INIT = 0.8 - 0.6 * math.exp(-0.3 * 0)
N_GATE = NSA_HEADS * 3
GATE_ROWS = 32

TQ = 256
DIFF_PAIRS_PER_STEP = 4
NSA_BLOCKS_PER_STEP = 4
TM_PROJ = 512
TM_MOE = 512
TM_FINAL = 256
XT_WIDTH = D_MODEL + LANE
ROW_SLABS = D_MODEL // LANE
VMEM_LIMIT = 48 * 1024 * 1024
VMEM_LIMIT_MOE = 56 * 1024 * 1024


def _cparams(sem):
    return pltpu.CompilerParams(dimension_semantics=sem, vmem_limit_bytes=VMEM_LIMIT)


def _dot(a, b):
    return jnp.dot(a, b, preferred_element_type=F32)


def _dot_nt(a, b):
    return lax.dot_general(a, b, (((1,), (1,)), ((), ())), preferred_element_type=F32)


def _bucket_thresholds():
    n = np.arange(0, REL_MAX_DIST + 1)
    nf = np.maximum(n, 1).astype(np.float32)
    large = REL_MAX_EXACT + (np.log(nf / np.float32(REL_MAX_EXACT)) / np.float32(math.log(REL_MAX_DIST / REL_MAX_EXACT))
                             * np.float32(REL_BUCKETS - REL_MAX_EXACT)).astype(np.int32)
    large = np.minimum(large, REL_BUCKETS - 1)
    bucket = np.where(n < REL_MAX_EXACT, n, large)
    return [int(np.argmax(bucket >= b)) for b in range(REL_BUCKETS)]


_THR = _bucket_thresholds()


def _inproj_kernel(x_ref, g_ref, w_ref, wt_ref, qa_ref, kc_ref, vc_ref, ks_ref, kw_ref, qb_ref, kb_ref,
                   vst_ref, vwt_ref, vbt_ref, gt_ref):
    x = x_ref[...]
    xn = (x * lax.rsqrt(jnp.mean(x * x, axis=-1, keepdims=True) + EPS) * g_ref[...]).astype(BF16)
    a = _dot(xn, w_ref[:, 0:512])
    kc_ref[...] = a[:, 0:128]
    vc_ref[...] = a[:, 128:256]
    ks_ref[...] = a[:, 256:384].astype(BF16)
    kw_ref[...] = a[:, 384:512].astype(BF16)
    a = _dot(xn, w_ref[:, 512:1024])
    for r in range(4):
        kb_ref[r] = a[:, r * LANE:(r + 1) * LANE].astype(BF16)

    ft = _dot_nt(wt_ref[...], xn)

    def feat(row0, rows=LANE):
        return ft[row0:row0 + rows, :]
    for r in range(4):
        qa_ref[r] = (feat(r * LANE) * (NSA_HD ** -0.5 * LOG2E)).astype(BF16)
        qb_ref[r] = (feat(512 + r * LANE) * (DIFF_HD ** -0.5 * LOG2E)).astype(BF16)
        vbt_ref[r] = feat(1280 + r * LANE).astype(BF16)
    vst_ref[...] = feat(1024).astype(BF16)
    vwt_ref[...] = feat(1152).astype(BF16)
    gt_ref[...] = feat(1792, GATE_ROWS)


def _in_proj(x2, g, w, wt):
    T = x2.shape[0]
    tm = TM_PROJ
    row = lambda i: (i, 0)
    o128b = jax.ShapeDtypeStruct((T, LANE), BF16)
    o128f = jax.ShapeDtypeStruct((T, LANE), F32)
    o4 = jax.ShapeDtypeStruct((4, T, LANE), BF16)
    ot = jax.ShapeDtypeStruct((LANE, T), BF16)
    o4t = jax.ShapeDtypeStruct((4, LANE, T), BF16)
    s128 = pl.BlockSpec((tm, LANE), row)
    s4 = pl.BlockSpec((4, tm, LANE), lambda i: (0, i, 0))
    st = pl.BlockSpec((LANE, tm), lambda i: (0, i))
    s4t = pl.BlockSpec((4, LANE, tm), lambda i: (0, 0, i))
    return pl.pallas_call(
        _inproj_kernel,
        grid=(T // tm,),
        in_specs=[pl.BlockSpec((tm, D_MODEL), row),
                  pl.BlockSpec((1, D_MODEL), lambda i: (0, 0)),
                  pl.BlockSpec(w.shape, lambda i: (0, 0)),
                  pl.BlockSpec(wt.shape, lambda i: (0, 0))],
        out_specs=[s4t, s128, s128, s128, s128, s4t, s4, st, st, s4t, pl.BlockSpec((GATE_ROWS, tm), lambda i: (0, i))],
        out_shape=[o4t, o128f, o128f, o128b, o128b, o4t, o4, ot, ot, o4t, jax.ShapeDtypeStruct((GATE_ROWS, T), F32)],
        compiler_params=_cparams(("parallel",)),
        name="in_proj",
    )(x2, g, w, wt)


def _gelu_tanh(x):
    return 0.5 * x * (1.0 + jnp.tanh(math.sqrt(2.0 / math.pi) * (x + 0.044715 * (x * x * x))))


def _compress_kernel(kc_ref, vc_ref, posk_ref, posv_ref, w1k_ref, w1v_ref, w2k_ref, w2v_ref, ko_ref, vo_ref):
    nrow = kc_ref.shape[0] // CMP_STRIDE
    rid = lax.broadcasted_iota(jnp.int32, (nrow, 1), 0)
    cid = lax.broadcasted_iota(jnp.int32, (1, nrow), 1)
    for src, pos, w1, w2, out, transposed in ((kc_ref, posk_ref, w1k_ref, w2k_ref, ko_ref, False),
                                              (vc_ref, posv_ref, w1v_ref, w2v_ref, vo_ref, True)):
        hid_a = jnp.zeros((nrow, 2 * CMP_HIDDEN), F32)
        hid_b = jnp.zeros((nrow, 2 * CMP_HIDDEN), F32)
        for m in range(CMP_STRIDE):
            y = src[pl.ds(m, nrow, stride=CMP_STRIDE), :]
            hid_a = hid_a + _dot((y + pos[m:m + 1, :]).astype(BF16), w1[m])
            hid_b = hid_b + _dot((y + pos[CMP_STRIDE + m:CMP_STRIDE + m + 1, :]).astype(BF16), w1[CMP_STRIDE + m])
        hid = hid_a + pltpu.roll(hid_b, nrow - 1, 0)
        o = _dot(_gelu_tanh(hid).astype(BF16), w2[...])
        if transposed:
            out[0] = jnp.where(cid < nrow - 1, o.T, 0.0).astype(BF16)
        else:
            out[0] = jnp.where(rid < nrow - 1, o, 0.0).astype(BF16)


def _compress(kc, vc, posk, posv, w1k, w1v, w2k, w2v, B, S):
    nrow = S // CMP_STRIDE
    assert nrow == LANE
    full = lambda a: pl.BlockSpec(a.shape, lambda b: (0,) * a.ndim)
    src = pl.BlockSpec((S, LANE), lambda b: (b, 0))
    osp = pl.BlockSpec((1, nrow, LANE), lambda b: (b, 0, 0))
    osh = jax.ShapeDtypeStruct((B, nrow, LANE), BF16)
    return pl.pallas_call(
        _compress_kernel,
        grid=(B,),
        in_specs=[src, src, full(posk), full(posv), full(w1k), full(w1v), full(w2k), full(w2v)],
        out_specs=[osp, osp],
        out_shape=[osh, osh],
        compiler_params=_cparams(("parallel",)),
        name="compress",
    )(kc, vc, posk, posv, w1k, w1v, w2k, w2v)


def _bias_from_dist(dist, tab_ref, h):
    val = jnp.full(dist.shape, tab_ref[h, 0], F32)
    for b in range(1, REL_BUCKETS):
        val = jnp.where(dist >= _THR[b], tab_ref[h, b], val)
    return val


def _bias_near_kernel(tab_ref, out_ref):
    h = pl.program_id(0)
    j = lax.broadcasted_iota(jnp.int32, (TQ, TQ), 0)
    i = lax.broadcasted_iota(jnp.int32, (TQ, TQ), 1)
    for d in range(2):
        bias = (_bias_from_dist(i - j + d * TQ, tab_ref, h) - tab_ref[h, REL_BUCKETS - 1]) * LOG2E
        out_ref[0, d] = jnp.where(i - j + d * TQ >= 0, bias, NEG)


def _bias_cmp_kernel(tab_ref, out_ref):
    h = pl.program_id(0)
    c = lax.broadcasted_iota(jnp.int32, (LANE, TQ), 0)
    t = pl.program_id(1) * TQ + lax.broadcasted_iota(jnp.int32, (LANE, TQ), 1)
    out_ref[0] = _bias_from_dist(t - (c * CMP_STRIDE + CMP_LEN - 1), tab_ref, h) * LOG2E


def _bias_tiles(tab_t, S):
    smem = pl.BlockSpec(memory_space=pltpu.SMEM)
    near = pl.pallas_call(
        _bias_near_kernel,
        grid=(N_REL_HEADS,),
        in_specs=[smem],
        out_specs=pl.BlockSpec((1, 2, TQ, TQ), lambda h: (h, 0, 0, 0)),
        out_shape=jax.ShapeDtypeStruct((N_REL_HEADS, 2, TQ, TQ), F32),
        compiler_params=_cparams(("parallel",)),
        name="bias_near",
    )(tab_t)
    cmp_bias = pl.pallas_call(
        _bias_cmp_kernel,
        grid=(NSA_HEADS, S // TQ),
        in_specs=[smem],
        out_specs=pl.BlockSpec((1, LANE, TQ), lambda h, q: (h, 0, q)),
        out_shape=jax.ShapeDtypeStruct((NSA_HEADS, LANE, S), F32),
        compiler_params=_cparams(("parallel", "parallel")),
        name="bias_cmp",
    )(tab_t)
    return near, cmp_bias


ACC_ROWS = NSA_HD + 16


def _with_ones(vt):
    return jnp.concatenate([vt, jnp.ones((ACC_ROWS - vt.shape[0], vt.shape[1]), BF16)], axis=0)


def _probs(s, m):
    return jnp.exp2((s - m).astype(BF16))


def _flash_first(ss, vts):
    ms = [jnp.max(s, axis=0, keepdims=True) for s in ss]
    ps = [_probs(s, m) for s, m in zip(ss, ms)]
    return tuple((m, _dot(vt, p)) for m, p, vt in zip(ms, ps, vts))


def _flash_update(ss, vts, sts):
    ms = [jnp.maximum(st[0], jnp.max(s, axis=0, keepdims=True)) for s, st in zip(ss, sts)]
    alphas = [jnp.exp2(st[0] - m) for m, st in zip(ms, sts)]
    ps = [_probs(s, m) for s, m in zip(ss, ms)]
    return tuple((m, a * st[1] + _dot(vt, p)) for m, a, st, p, vt in zip(ms, alphas, sts, ps, vts))


def _flash_out(st):
    acc = st[1]
    return acc[0:NSA_HD, :] * (1.0 / acc[NSA_HD:NSA_HD + 1, :])


def _stream_scratch(n):
    return ([pltpu.VMEM((8, TQ), F32), pltpu.VMEM((ACC_ROWS, TQ), F32)] * n
            + [pltpu.VMEM((TQ, TQ), F32)] * (2 * n) + [pltpu.VMEM((TQ, TQ), BF16)] * (2 * n))


def _stream_refs(refs, n):
    st_refs = tuple((refs[2 * c], refs[2 * c + 1]) for c in range(n))
    s, p = refs[2 * n:4 * n], refs[4 * n:6 * n]
    return st_refs, tuple(s[:n]), tuple(s[n:]), tuple(p[:n]), tuple(p[n:])


def _causal_stream(qi, scores, vtile, st_refs, s_a, s_b, p_a, p_b, diag_ready=False):
    n_far = jnp.maximum(qi - 1, 0)
    top = qi - 2

    chains = range(len(st_refs))

    def write(s_refs, kt, near=2):
        for c in chains:
            s_refs[c][...] = scores(jnp.maximum(kt, 0), near, c)

    def pending(p_refs, kt):
        return [_dot(vtile(jnp.maximum(kt, 0), c), p_refs[c][...]) for c in chains]

    def step(s_cur, p_cur, s_next, kt_next, p_prev, kt_prev):
        rescale = []
        for c in chains:
            s_next[c][...] = scores(jnp.maximum(kt_next, 0), 2, c)
            pv = _dot(vtile(jnp.maximum(kt_prev, 0), c), p_prev[c][...])
            st = st_refs[c][0]
            s = s_cur[c][...]
            m_old = st[0:1, :]
            m = jnp.maximum(m_old, jnp.max(s, axis=0, keepdims=True))
            p_cur[c][...] = _probs(s, m)
            st[0:1, :] = m
            rescale.append((jnp.exp2(m_old - m), pv))
        for c in chains:
            acc = st_refs[c][1]
            acc[...] = rescale[c][0] * (acc[...] + rescale[c][1])

    if not diag_ready:
        write(s_a, qi, 0)
    for c in chains:
        s_b[c][...] = scores(jnp.maximum(qi - 1, 0), 1, c)
        st, acc = st_refs[c]
        s = s_a[c][...]
        m = jnp.max(s, axis=0, keepdims=True)
        p_a[c][...] = _probs(s, m)
        st[0:1, :] = m
        acc[...] = jnp.zeros(acc.shape, F32)
    step(s_b, p_b, s_a, top, p_a, qi)

    def pair(j, c):
        kt = top - 2 * j
        step(s_a, p_a, s_b, kt - 1, p_b, kt + 1)
        step(s_b, p_b, s_a, kt - 2, p_a, kt)
        return c
    lax.fori_loop(0, n_far // 2, pair, 0)

    @pl.when(n_far % 2 == 1)
    def _():
        for c, pv in zip(chains, pending(p_b, 1)):
            st, acc = st_refs[c]
            s = s_a[c][...]
            m_old = st[0:1, :]
            m = jnp.maximum(m_old, jnp.max(s, axis=0, keepdims=True))
            st[0:1, :] = m
            acc[...] = jnp.exp2(m_old - m) * (acc[...] + pv) + _dot(vtile(0, c), _probs(s, m))

    @pl.when(n_far % 2 == 0)
    def _():
        last = jnp.where(n_far == 0, qi - 1, 0)
        for c, pv in zip(chains, pending(p_b, last)):
            acc = st_refs[c][1]
            acc[...] = acc[...] + pv
    return tuple((st[0:1, :], acc[...]) for st, acc in st_refs)


def _ktile(ref, kt):
    return ref[pl.ds(pl.multiple_of(kt * TQ, TQ), TQ), :]


def _vtile(ref, kt):
    return ref[:, pl.ds(pl.multiple_of(kt * TQ, TQ), TQ)]


def _nsa_kernel(qa_ref, gt_ref, kcmp_ref, vcmpt_ref, bc_ref, ks_ref, vst_ref, kw_ref, vwt_ref, dn_ref,
                ov_ref, o_ref, psum_ref, sel_ref, oacc_ref, sig_ref, *scratch):
    st_refs, s_a, s_b, p_a, p_b = _stream_refs(scratch, NSA_BLOCKS_PER_STEP * NSA_G)
    qi = pl.program_id(1)
    t0 = qi * TQ
    sub_grp = lax.shift_right_arithmetic(lax.broadcasted_iota(jnp.int32, (LANE, 1), 0), 6)
    sig_ref[...] = jax.nn.sigmoid(gt_ref[...])

    def gate_row(c):
        return sig_ref[pl.ds(c, 1), :]

    def masked_q(r, g):
        return jnp.where(sub_grp == g, qa_ref[r].astype(F32), 0.0).astype(BF16)

    n_cmp = kcmp_ref.shape[1] - 1
    crow = lax.broadcasted_iota(jnp.int32, (LANE, 1), 0)
    cmp_end = jnp.where(crow < n_cmp, crow * CMP_STRIDE + CMP_LEN - 1, 1 << 30)
    mask_c = (t0 + lax.broadcasted_iota(jnp.int32, (1, TQ), 1)) >= cmp_end

    heads = [(r, g) for r in range(NSA_R) for g in range(NSA_G)]
    scores_c = [_dot(kcmp_ref[0], masked_q(r, g)) for r, g in heads]
    probs_c = []
    for (r, g), s in zip(heads, scores_c):
        s = jnp.where(mask_c, s + bc_ref[g * NSA_R + r], NEG)
        p = jnp.where(mask_c, jnp.exp2(s - jnp.max(s, axis=0, keepdims=True)), 0.0)
        l = jnp.sum(p, axis=0, keepdims=True)
        probs_c.append(p * jnp.where(l > 0.0, 1.0 / l, 0.0))
    for g in range(NSA_G):
        psum_ref[g] = sum(p for (r, gg), p in zip(heads, probs_c) if gg == g)
    outs_c = [_dot(vcmpt_ref[0], p.astype(BF16)) for p in probs_c]
    for r in range(NSA_R):
        o0, o1 = outs_c[r * NSA_G], outs_c[r * NSA_G + 1]
        oacc_ref[r] = jnp.where(sub_grp == 0, gate_row(r * 3) * o0, gate_row((NSA_R + r) * 3) * o1)

    n_sel = ov_ref.shape[0]
    jj = lax.broadcasted_iota(jnp.int32, (n_sel, TQ), 0)
    cur = lax.shift_right_arithmetic(t0 + lax.broadcasted_iota(jnp.int32, (n_sel, TQ), 1), 6)
    valid = jj <= cur
    forced = (jj == 0) | (cur - jj < SEL_FORCED_LOCAL)
    for g in range(NSA_G):
        ps = psum_ref[g]
        hi = ps.astype(BF16)
        rem = ps - hi.astype(F32)
        mid = rem.astype(BF16)
        lo = (rem - mid.astype(F32)).astype(BF16)
        ov = ov_ref[...]
        imp = _dot(ov, hi) + _dot(ov, mid) + _dot(ov, lo)
        score = jnp.where(valid, jnp.where(forced, 1e9, imp), -1e9)
        cnt = jnp.zeros((n_sel, TQ), F32)
        for j2 in range(n_sel):
            row = score[j2:j2 + 1, :]
            tie = jnp.where(jj > j2, 1.0, 0.0)
            cnt = cnt + jnp.where(row > score, 1.0, jnp.where(row == score, tie, 0.0))
        sel_ref[g] = jnp.where(cnt < float(min(SEL_TOPN, n_sel)), jnp.where(score > -1e8, 0.0, NEG), NEG)

    has_prev = jnp.where(qi >= 1, 0.0, NEG)
    has_wfar = jnp.where(qi >= WINDOW // TQ, 0.0, NEG)
    kt_prev = jnp.maximum(qi - 1, 0)
    kt_wfar = jnp.maximum(qi - WINDOW // TQ, 0)
    wfar_mask = lax.broadcasted_iota(jnp.int32, (TQ, TQ), 0) > lax.broadcasted_iota(jnp.int32, (TQ, TQ), 1)

    def main_body(rp, carry):
        chains = [(NSA_BLOCKS_PER_STEP * rp + i, g) for i in range(NSA_BLOCKS_PER_STEP) for g in range(NSA_G)]
        qms = [masked_q(r, g) for r, g in chains]
        hids = [g * NSA_R + r for r, g in chains]

        def block_mask(g, kt):
            per_tile = TQ // SEL_BLOCK
            rows = [jnp.broadcast_to(sel_ref[g, pl.ds(kt * per_tile + i, 1), :], (SEL_BLOCK, TQ))
                    for i in range(per_tile)]
            return jnp.concatenate(rows, axis=0)

        def near_bias(s, near, c):
            if near == 0:
                return s + dn_ref[hids[c], 0]
            if near == 1:
                return s + dn_ref[hids[c], 1] + has_prev
            return s

        def group_values(ref, kt, c):
            g = chains[c][1]
            return _with_ones(ref[g * NSA_HD:(g + 1) * NSA_HD, pl.ds(pl.multiple_of(kt * TQ, TQ), TQ)])

        def finish(sts, branch):
            for c, (r, g) in enumerate(chains):
                oacc_ref[r, g * NSA_HD:(g + 1) * NSA_HD, :] += gate_row(hids[c] * 3 + branch) * _flash_out(sts[c])

        def slc_scores(kt, near, c):
            return near_bias(_dot(_ktile(ks_ref, kt), qms[c]), near, c) + block_mask(chains[c][1], kt)

        def slc_vtile(kt, c):
            return group_values(vst_ref, kt, c)

        finish(_causal_stream(qi, slc_scores, slc_vtile, st_refs, s_a, s_b, p_a, p_b), 1)

        cs = range(len(chains))
        win = []
        for kt, near in ((qi, 0), (kt_prev, 1), (kt_wfar, 2)):
            k = _ktile(kw_ref, kt)
            ss = [near_bias(_dot(k, qms[c]), near, c) for c in cs]
            if near == 2:
                ss = [jnp.where(wfar_mask, s + has_wfar, NEG) for s in ss]
            win.append(ss)
        sts = _flash_first(win[0], [group_values(vwt_ref, qi, c) for c in cs])
        sts = _flash_update(win[1], [group_values(vwt_ref, kt_prev, c) for c in cs], sts)
        sts = _flash_update(win[2], [group_values(vwt_ref, kt_wfar, c) for c in cs], sts)
        finish(sts, 2)
        return carry
    lax.fori_loop(0, NSA_R // NSA_BLOCKS_PER_STEP, main_body, 0)

    for r in range(NSA_R):
        o_ref[r] = oacc_ref[r].T.astype(BF16)


def _nsa(qa, gates_t, kcmp, vcmpt, bias_c, ks, vst, kw, vwt, near_a, ov, B, S):
    nq = S // TQ
    T = B * S
    k_spec = pl.BlockSpec((S, LANE), lambda b, q: (b, 0))
    v_spec = pl.BlockSpec((LANE, S), lambda b, q: (0, b))
    full = lambda a: pl.BlockSpec(a.shape, lambda b, q: (0,) * a.ndim)
    cmp_spec = pl.BlockSpec((1, LANE, LANE), lambda b, q: (b, 0, 0))
    qo = pl.BlockSpec((4, TQ, LANE), lambda b, q: (0, b * nq + q, 0))
    return pl.pallas_call(
        _nsa_kernel,
        grid=(B, nq),
        in_specs=[pl.BlockSpec((4, LANE, TQ), lambda b, q: (0, 0, b * nq + q)),
                  pl.BlockSpec((GATE_ROWS, TQ), lambda b, q: (0, b * nq + q)),
                  cmp_spec, cmp_spec,
                  pl.BlockSpec((NSA_HEADS, LANE, TQ), lambda b, q: (0, 0, q)),
                  k_spec, v_spec, k_spec, v_spec, full(near_a), full(ov)],
        out_specs=qo,
        out_shape=jax.ShapeDtypeStruct((4, T, LANE), BF16),
        scratch_shapes=[pltpu.VMEM((NSA_G, LANE, TQ), F32), pltpu.VMEM((NSA_G, S // SEL_BLOCK, TQ), F32),
                        pltpu.VMEM((NSA_R, LANE, TQ), F32), pltpu.VMEM((GATE_ROWS, TQ), F32)]
                       + _stream_scratch(NSA_BLOCKS_PER_STEP * NSA_G),
        compiler_params=_cparams(("parallel", "arbitrary")),
        name="nsa",
    )(qa, gates_t, kcmp, vcmpt, bias_c, ks, vst, kw, vwt, near_a, ov)


def _diff_kernel(lq1_ref, lk1_ref, lq2_ref, lk2_ref, sub_ref, qb_ref, kb_ref, vbt_ref, dn_ref, o_ref, *scratch):
    n = DIFF_PAIRS_PER_STEP * 4
    st_refs, s_a, s_b, p_a, p_b = _stream_refs(scratch, n)
    qi = pl.program_id(1)
    chain_of_row = lax.shift_right_arithmetic(lax.broadcasted_iota(jnp.int32, (LANE, 1), 0), 5)
    lam = (jnp.exp(jnp.sum(lq1_ref[...] * lk1_ref[...], axis=-1, keepdims=True))
           - jnp.exp(jnp.sum(lq2_ref[...] * lk2_ref[...], axis=-1, keepdims=True)) + LAMBDA_INIT)
    has_prev = jnp.where(qi >= 1, 0.0, NEG)
    n_steps = DIFF_HEADS // 2 // DIFF_PAIRS_PER_STEP

    def score_fn(step):
        pairs = [step * DIFF_PAIRS_PER_STEP + c // 4 for c in range(n)]
        qs = [qb_ref[step * DIFF_PAIRS_PER_STEP + i].astype(F32) for i in range(DIFF_PAIRS_PER_STEP)]
        qms = [jnp.where(chain_of_row == c % 4, qs[c // 4], 0.0).astype(BF16) for c in range(n)]

        def scores(kt, near, c):
            s = _dot(kb_ref[pairs[c], pl.ds(pl.multiple_of(kt * TQ, TQ), TQ), :], qms[c])
            if near == 2:
                return s
            bias = dn_ref[2 * pairs[c] + (c % 4) // 2, near]
            return s + bias if near == 0 else s + bias + has_prev
        return scores

    def start_diag(step):
        scores = score_fn(step)
        for c in range(n):
            s_a[c][...] = scores(qi, 0, c)

    start_diag(0)

    def step_body(step, carry):
        def vtile(kt, c):
            hh = (c % 4) // 2
            return _with_ones(vbt_ref[step * DIFF_PAIRS_PER_STEP + c // 4, hh * 2 * DIFF_HD:(hh + 1) * 2 * DIFF_HD,
                                      pl.ds(pl.multiple_of(kt * TQ, TQ), TQ)])

        sts = _causal_stream(qi, score_fn(step), vtile, st_refs, s_a, s_b, p_a, p_b, diag_ready=True)
        start_diag(jnp.minimum(step + 1, n_steps - 1))
        for i in range(DIFF_PAIRS_PER_STEP):
            outs = []
            for hh in range(2):
                c = 4 * i + 2 * hh
                o = _flash_out(sts[c]) - lam * _flash_out(sts[c + 1])
                outs.append(o * lax.rsqrt(jnp.mean(o * o, axis=0, keepdims=True) + EPS))
            out = jnp.concatenate(outs, axis=0) * sub_ref[...] * (1.0 - LAMBDA_INIT)
            o_ref[step * DIFF_PAIRS_PER_STEP + i] = out.T.astype(BF16)
        return carry
    lax.fori_loop(0, n_steps, step_body, 0)


def _diff(lq1, lk1, lq2, lk2, sub, qb, kb, vbt, near_b, B, S):
    nq = S // TQ
    T = B * S
    full = lambda a: pl.BlockSpec(a.shape, lambda b, q: (0,) * a.ndim)
    k_spec = pl.BlockSpec((4, S, LANE), lambda b, q: (0, b, 0))
    v_spec = pl.BlockSpec((4, LANE, S), lambda b, q: (0, 0, b))
    qo = pl.BlockSpec((4, TQ, LANE), lambda b, q: (0, b * nq + q, 0))
    return pl.pallas_call(
        _diff_kernel,
        grid=(B, nq),
        in_specs=[full(lq1), full(lk1), full(lq2), full(lk2), full(sub),
                  pl.BlockSpec((4, LANE, TQ), lambda b, q: (0, 0, b * nq + q)), k_spec, v_spec, full(near_b)],
        out_specs=qo,
        out_shape=jax.ShapeDtypeStruct((4, T, LANE), BF16),
        scratch_shapes=_stream_scratch(DIFF_PAIRS_PER_STEP * 4),
        compiler_params=_cparams(("parallel", "arbitrary")),
        name="diff",
    )(lq1, lk1, lq2, lk2, sub, qb, kb, vbt, near_b)


def _outproj_kernel(x_ref, oa_ref, ob_ref, w_ref, g_ref, wr_ref, br_ref, h_ref, xt_ref, route_ref, cnt_ref,
                    carry_ref):
    @pl.when(pl.program_id(0) == 0)
    def _():
        carry_ref[...] = jnp.zeros(carry_ref.shape, F32)

    o = jnp.concatenate([oa_ref[r] for r in range(4)] + [ob_ref[r] for r in range(4)], axis=-1)
    h = x_ref[...] + _dot(o, w_ref[...])
    h_ref[...] = h
    tn32 = h * lax.rsqrt(jnp.mean(h * h, axis=-1, keepdims=True) + EPS) * g_ref[...]
    xt_ref[:, 0:D_MODEL] = tn32
    tn = tn32.astype(BF16)
    logits = _dot(tn, wr_ref[...]) + br_ref[...]
    lane = lax.broadcasted_iota(jnp.int32, (1, LANE), 1)
    lane_f = lane.astype(F32)
    is_grp = lane < MOE_GROUPS
    lg = jnp.where(is_grp, logits, NEG)
    mg = jnp.max(lg, axis=-1, keepdims=True)
    zg = jnp.sum(jnp.where(is_grp, jnp.exp(lg - mg), 0.0), axis=-1, keepdims=True)
    g_prob = 1.0 / zg
    g_idx = jnp.min(jnp.where(lg == mg, lane_f, 1e9), axis=-1, keepdims=True)
    lane_grp = jnp.where((lane >= MOE_GROUPS) & (lane < MOE_GROUPS + N_EXPERTS),
                         lax.shift_right_arithmetic(lane - MOE_GROUPS, 3), -1).astype(F32)
    le = jnp.where(lane_grp == g_idx, logits, NEG)
    m1 = jnp.max(le, axis=-1, keepdims=True)
    e1 = jnp.min(jnp.where(le == m1, lane_f, 1e9), axis=-1, keepdims=True)
    le2 = jnp.where(lane_f == e1, NEG, le)
    m2 = jnp.max(le2, axis=-1, keepdims=True)
    e2 = jnp.min(jnp.where(le2 == m2, lane_f, 1e9), axis=-1, keepdims=True)
    ratio = jnp.exp(m2 - m1)
    w1 = g_prob / (1.0 + ratio)
    w2 = w1 * ratio
    xt_ref[:, D_MODEL:D_MODEL + LANE] = jnp.where(lane_f == e1, w1, 0.0) + jnp.where(lane_f == e2, w2, 0.0)

    tm = h.shape[0]
    onehot = jnp.where(lane_f == g_idx, 1.0, 0.0)
    earlier = jnp.where(lax.broadcasted_iota(jnp.int32, (tm, tm), 0) > lax.broadcasted_iota(jnp.int32, (tm, tm), 1),
                        1.0, 0.0).astype(BF16)
    prefix = _dot(earlier, onehot.astype(BF16)) + carry_ref[...]
    rank = jnp.sum(onehot * prefix, axis=-1, keepdims=True)
    carry_ref[...] += jnp.sum(onehot, axis=0, keepdims=True)
    route_ref[...] = jnp.where(lane == 0, g_idx, jnp.where(lane == 1, rank, 0.0))
    cnt_ref[...] = jnp.broadcast_to(carry_ref[...], cnt_ref.shape)


def _out_proj(x2, oa, ob, w, g, wr, br):
    T = x2.shape[0]
    tm = TM_PROJ
    row = lambda i: (i, 0)
    full = lambda a: pl.BlockSpec(a.shape, lambda i: (0,) * a.ndim)
    o4 = pl.BlockSpec((4, tm, LANE), lambda i: (0, i, 0))
    return pl.pallas_call(
        _outproj_kernel,
        grid=(T // tm,),
        in_specs=[pl.BlockSpec((tm, D_MODEL), row), o4, o4, full(w), full(g), full(wr), full(br)],
        out_specs=[pl.BlockSpec((tm, D_MODEL), row), pl.BlockSpec((tm, XT_WIDTH), row), pl.BlockSpec((tm, LANE), row),
                   pl.BlockSpec((8, LANE), lambda i: (0, 0))],
        out_shape=[jax.ShapeDtypeStruct((T, D_MODEL), F32), jax.ShapeDtypeStruct((T, XT_WIDTH), F32),
                   jax.ShapeDtypeStruct((T, LANE), F32), jax.ShapeDtypeStruct((8, LANE), F32)],
        scratch_shapes=[pltpu.VMEM((1, LANE), F32)],
        compiler_params=_cparams(("arbitrary",)),
        name="out_proj",
    )(x2, oa, ob, w, g, wr, br)


def _moe_kernel(tg_ref, ok_ref, idx_ref, idxn_ref, xt_hbm, wg_ref, wu_ref, wd_ref, y_ref, xbuf, sem):
    i = pl.program_id(0)
    slot = i % 2
    tm = xbuf.shape[1]

    def row_copy(index_ref, r, dst_slot):
        return pltpu.make_async_copy(xt_hbm.at[pl.ds(index_ref[0, 0, r], 1), :],
                                     xbuf.at[dst_slot, pl.ds(r, 1), :], sem.at[dst_slot])

    @pl.when(i == 0)
    def _():
        def body(r, c):
            row_copy(idx_ref, r, 0).start()
            return c
        lax.fori_loop(0, tm, body, 0)

    @pl.when((i == 0) | (ok_ref[jnp.maximum(i - 1, 0)] == 1))
    def _():
        pltpu.make_async_copy(xt_hbm.at[pl.ds(0, tm), :], xbuf.at[slot], sem.at[slot]).wait()

    @pl.when(ok_ref[i] == 0)
    def _():
        y_ref[...] = jnp.zeros(y_ref.shape, F32)

    @pl.when(ok_ref[i] == 1)
    def _():
        lane = lax.broadcasted_iota(jnp.int32, (1, LANE), 1)
        x = xbuf[slot, :, 0:D_MODEL].astype(BF16)
        cmb = xbuf[slot, :, D_MODEL:D_MODEL + LANE]
        first_lane = MOE_GROUPS + tg_ref[i] * EPG
        per_expert = tm // EPG
        y = jnp.zeros((tm, D_MODEL), F32)
        for e in range(EPG):
            a = _dot(x, wg_ref[0, e])
            b = _dot(x, wu_ref[0, e])
            ce = jnp.sum(jnp.where(lane == first_lane + e, cmb, 0.0), axis=-1, keepdims=True)
            y = y + _dot(((a * jax.nn.sigmoid(a)) * b * ce).astype(BF16), wd_ref[0, e])
            for r in range(e * per_expert, (e + 1) * per_expert):
                row_copy(idxn_ref, r, 1 - slot).start(priority=r % 2)
        for blk in range(ROW_SLABS):
            y_ref[:, blk, :] = y[:, blk * LANE:(blk + 1) * LANE]


def _moe(tile_group, tile_ok, src3, xt, wg, wu, wd):
    n_tiles, _, tm = src3.shape
    last = n_tiles - 1
    grid_spec = pltpu.PrefetchScalarGridSpec(
        num_scalar_prefetch=2,
        grid=(n_tiles,),
        in_specs=[pl.BlockSpec((1, 1, tm), lambda i, tg, ok: (i, 0, 0), memory_space=pltpu.SMEM),
                  pl.BlockSpec((1, 1, tm), lambda i, tg, ok: (jnp.minimum(i + 1, last), 0, 0), memory_space=pltpu.SMEM),
                  pl.BlockSpec(memory_space=pl.ANY),
                  pl.BlockSpec((1, EPG, D_MODEL, EXPERT_FF), lambda i, tg, ok: (tg[i], 0, 0, 0)),
                  pl.BlockSpec((1, EPG, D_MODEL, EXPERT_FF), lambda i, tg, ok: (tg[i], 0, 0, 0)),
                  pl.BlockSpec((1, EPG, EXPERT_FF, D_MODEL), lambda i, tg, ok: (tg[i], 0, 0, 0))],
        out_specs=pl.BlockSpec((tm, ROW_SLABS, LANE), lambda i, tg, ok: (i, 0, 0)),
        scratch_shapes=[pltpu.VMEM((2, tm, XT_WIDTH), F32), pltpu.SemaphoreType.DMA((2,))],
    )
    return pl.pallas_call(
        _moe_kernel,
        grid_spec=grid_spec,
        out_shape=jax.ShapeDtypeStruct((n_tiles * tm, ROW_SLABS, LANE), F32),
        compiler_params=pltpu.CompilerParams(dimension_semantics=("arbitrary",), vmem_limit_bytes=VMEM_LIMIT_MOE),
        name="moe",
    )(tile_group, tile_ok, src3, src3, xt, wg, wu, wd)


def _final_kernel(pos_ref, posn_ref, y_hbm, h_ref, gf_ref, o_ref, ybuf, sem):
    i = pl.program_id(0)
    slot = i % 2
    tm = h_ref.shape[0]

    def issue(index_ref, dst_slot):
        group = 8

        def body(j, c):
            for k in range(group):
                r = j * group + k
                pltpu.make_async_copy(y_hbm.at[index_ref[0, 0, r]], ybuf.at[dst_slot, pl.ds(r * ROW_SLABS, ROW_SLABS), :],
                                      sem.at[dst_slot]).start(priority=k % 2)
            return c
        lax.fori_loop(0, tm // group, body, 0)

    @pl.when(i == 0)
    def _():
        issue(pos_ref, 0)

    @pl.when(i + 1 < pl.num_programs(0))
    def _():
        issue(posn_ref, 1 - slot)

    pltpu.make_async_copy(ybuf.at[slot], ybuf.at[slot], sem.at[slot]).wait()
    y = jnp.concatenate([ybuf[slot, pl.ds(blk, tm, stride=ROW_SLABS), :] for blk in range(ROW_SLABS)], axis=-1)
    h = h_ref[...] + y
    o_ref[...] = h * lax.rsqrt(jnp.mean(h * h, axis=-1, keepdims=True) + EPS) * gf_ref[...]


def _final(pos3, y_sorted, h, gf):
    n_tiles, _, tm = pos3.shape
    last = n_tiles - 1
    row = lambda i: (i, 0)
    return pl.pallas_call(
        _final_kernel,
        grid=(n_tiles,),
        in_specs=[pl.BlockSpec((1, 1, tm), lambda i: (i, 0, 0), memory_space=pltpu.SMEM),
                  pl.BlockSpec((1, 1, tm), lambda i: (jnp.minimum(i + 1, last), 0, 0), memory_space=pltpu.SMEM),
                  pl.BlockSpec(memory_space=pl.ANY),
                  pl.BlockSpec((tm, D_MODEL), row),
                  pl.BlockSpec((1, D_MODEL), lambda i: (0, 0))],
        out_specs=pl.BlockSpec((tm, D_MODEL), row),
        out_shape=jax.ShapeDtypeStruct((n_tiles * tm, D_MODEL), F32),
        scratch_shapes=[pltpu.VMEM((2, tm * ROW_SLABS, LANE), F32), pltpu.SemaphoreType.DMA((2,))],
        compiler_params=_cparams(("arbitrary",)),
        name="final",
    )(pos3, pos3, y_sorted, h, gf)


def _qa_perm():
    new = np.arange(NSA_HEADS * NSA_HD)
    r, g, d = new // LANE, (new % LANE) // NSA_HD, new % NSA_HD
    return (g * NSA_R + r) * NSA_HD + d


def _block_diag2(w):
    z = jnp.zeros_like(w)
    return jnp.concatenate([jnp.concatenate([w, z], axis=-1), jnp.concatenate([z, w], axis=-1)], axis=-2)


def kernel(x, rel_bias, ln_mix, w_in, cmp_pos_k, cmp_pos_v, cmp_k_w1, cmp_k_w2, cmp_v_w1, cmp_v_w2,
           diff_lq1, diff_lk1, diff_lq2, diff_lk2, diff_subln, w_out, ln_ffn,
           router_group_w, router_group_b, router_expert_w, router_expert_b,
           exp_w_gate, exp_w_up, exp_w_down, ln_final):
    B, S, D = x.shape
    T = B * S
    assert D == D_MODEL and S % TQ == 0 and S >= WINDOW and T % TM_MOE == 0 and T % TM_FINAL == 0
    x2 = x.reshape(T, D)
    perm = _qa_perm()

    w = w_in[0]
    c_kc, c_vc, c_ks, c_vs, c_kw, c_vw, c_gt = 512, 640, 768, 896, 1024, 1152, 1280
    c_qb = c_gt + N_GATE
    c_kb, c_vb = c_qb + 512, c_qb + 1024
    col = lambda c, n=LANE: w[:, c:c + n]
    w_tok = jnp.concatenate([col(c_kc), col(c_vc), col(c_ks), col(c_kw), col(c_kb, 512)], axis=1).astype(BF16)
    w_feat = jnp.concatenate([w[:, perm], col(c_qb, 512), col(c_vs), col(c_vw), col(c_vb, 512), col(c_gt, N_GATE),
                              jnp.zeros((D, GATE_ROWS - N_GATE), F32)], axis=1).T.astype(BF16)
    qa, kc, vc, ks, kw, qb, kb, vst, vwt, vbt, gates_t = _in_proj(x2, ln_mix[0][None, :], w_tok, w_feat)

    w1k = _block_diag2(cmp_k_w1[0].reshape(CMP_LEN, NSA_HD, CMP_HIDDEN)).astype(BF16)
    w1v = _block_diag2(cmp_v_w1[0].reshape(CMP_LEN, NSA_HD, CMP_HIDDEN)).astype(BF16)
    w2k = _block_diag2(cmp_k_w2[0]).astype(BF16)
    w2v = _block_diag2(cmp_v_w2[0]).astype(BF16)
    posk = jnp.tile(cmp_pos_k[0], (1, NSA_G))
    posv = jnp.tile(cmp_pos_v[0], (1, NSA_G))
    kcmp, vcmpt = _compress(kc, vc, posk, posv, w1k, w1v, w2k, w2v, B, S)

    near, bias_c = _bias_tiles(rel_bias.T, S)

    n_sel = S // SEL_BLOCK
    nrow = S // CMP_STRIDE
    c_start = np.arange(nrow) * CMP_STRIDE
    s_start = np.arange(n_sel) * SEL_BLOCK
    ov = ((c_start[None, :] <= s_start[:, None] + SEL_BLOCK - 1)
          & (c_start[None, :] + CMP_LEN - 1 >= s_start[:, None])
          & (np.arange(nrow)[None, :] < nrow - 1)).astype(np.float32)
    o_a = _nsa(qa, gates_t, kcmp, vcmpt, bias_c, ks, vst, kw, vwt, near[:NSA_HEADS], jnp.asarray(ov, BF16), B, S)

    sub = jnp.tile(diff_subln[0], 2)[:, None]
    o_b = _diff(diff_lq1[0][None, :], diff_lk1[0][None, :], diff_lq2[0][None, :], diff_lk2[0][None, :],
                sub, qb, kb, vbt, near[NSA_HEADS:], B, S)

    w_o = jnp.concatenate([w_out[0][:512][perm], w_out[0][512:]], axis=0).astype(BF16)
    n_r = MOE_GROUPS + N_EXPERTS
    wr = jnp.concatenate([router_group_w[0], router_expert_w[0], jnp.zeros((D, LANE - n_r), F32)], axis=1).astype(BF16)
    br = jnp.concatenate([router_group_b[0], router_expert_b[0], jnp.zeros((LANE - n_r,), F32)])[None, :]
    h1, xt, route, counts = _out_proj(x2, o_a, o_b, w_o, ln_ffn[0][None, :], wr, br)

    tm = TM_MOE
    n_tiles = T // tm + MOE_GROUPS
    cnt = counts[0, :MOE_GROUPS].astype(jnp.int32)
    ends = jnp.cumsum((cnt + tm - 1) // tm * tm)
    starts = ends - (cnt + tm - 1) // tm * tm
    pos = starts[route[:, 0].astype(jnp.int32)] + route[:, 1].astype(jnp.int32)
    src = jnp.zeros((n_tiles * tm,), jnp.int32).at[pos].set(
        jnp.arange(T, dtype=jnp.int32), unique_indices=True, mode="promise_in_bounds")
    tile_start = jnp.arange(n_tiles, dtype=jnp.int32) * tm
    tile_group = jnp.minimum(jnp.searchsorted(ends, tile_start, side="right"), MOE_GROUPS - 1).astype(jnp.int32)
    tile_ok = (tile_start < ends[-1]).astype(jnp.int32)

    by_group = lambda a: a[0].astype(BF16).reshape((MOE_GROUPS, EPG) + a.shape[2:])
    y_sorted = _moe(tile_group, tile_ok, src.reshape(n_tiles, 1, tm), xt,
                    by_group(exp_w_gate), by_group(exp_w_up), by_group(exp_w_down))
    out = _final(pos.reshape(T // TM_FINAL, 1, TM_FINAL), y_sorted, h1, ln_final[None, :])
    return out.reshape(B, S, D)
```

```python
import math

import numpy as np
import jax
import jax.numpy as jnp
from jax import lax
from jax.experimental import pallas as pl
from jax.experimental.pallas import tpu as pltpu

F32 = jnp.float32
BF16 = jnp.bfloat16
NEG = -1e30
EPS = 1e-6
LOG2E = math.log2(math.e)

D_MODEL = 1024
LANE = 128
NSA_HEADS, NSA_G, NSA_R, NSA_HD = 8, 2, 4, 64
CMP_LEN, CMP_STRIDE, CMP_HIDDEN = 32, 16, 128
SEL_BLOCK, SEL_TOPN, SEL_FORCED_LOCAL, WINDOW = 64, 8, 2, 512
DIFF_HEADS, DIFF_HD = 8, 32
REL_BUCKETS, REL_MAX_EXACT, REL_MAX_DIST = 32, 16, 128
N_REL_HEADS = NSA_HEADS + DIFF_HEADS
MOE_GROUPS, EPG, N_EXPERTS, EXPERT_FF = 4, 8, 32, 256
LAMBDA_INIT = 0.8 - 0.6 * math.exp(-0.3 * 0)
N_GATE = NSA_HEADS * 3
GATE_ROWS = 32

TQ = 256
DIFF_PAIRS_PER_STEP = 4
NSA_BLOCKS_PER_STEP = 4
TM_PROJ = 512
TM_MOE = 512
TM_FINAL = 256
XT_WIDTH = D_MODEL + LANE
ROW_SLABS = D_MODEL // LANE
VMEM_LIMIT = 48 * 1024 * 1024
VMEM_LIMIT_MOE = 56 * 1024 * 1024


def _cparams(sem):
    return pltpu.CompilerParams(dimension_semantics=sem, vmem_limit_bytes=VMEM_LIMIT)


def _dot(a, b):
    return jnp.dot(a, b, preferred_element_type=F32)


def _dot_nt(a, b):
    return lax.dot_general(a, b, (((1,), (1,)), ((), ())), preferred_element_type=F32)


def _bucket_thresholds():
    n = np.arange(0, REL_MAX_DIST + 1)
    nf = np.maximum(n, 1).astype(np.float32)
    large = REL_MAX_EXACT + (np.log(nf / np.float32(REL_MAX_EXACT)) / np.float32(math.log(REL_MAX_DIST / REL_MAX_EXACT))
                             * np.float32(REL_BUCKETS - REL_MAX_EXACT)).astype(np.int32)
    large = np.minimum(large, REL_BUCKETS - 1)
    bucket = np.where(n < REL_MAX_EXACT, n, large)
    return [int(np.argmax(bucket >= b)) for b in range(REL_BUCKETS)]


_THR = _bucket_thresholds()


def _inproj_kernel(x_ref, g_ref, w_ref, wt_ref, qa_ref, kc_ref, vc_ref, ks_ref, kw_ref, qb_ref, kb_ref,
                   vst_ref, vwt_ref, vbt_ref, gt_ref):
    x = x_ref[...]
    xn = (x * lax.rsqrt(jnp.mean(x * x, axis=-1, keepdims=True) + EPS) * g_ref[...]).astype(BF16)
    a = _dot(xn, w_ref[:, 0:512])
    kc_ref[...] = a[:, 0:128]
    vc_ref[...] = a[:, 128:256]
    ks_ref[...] = a[:, 256:384].astype(BF16)
    kw_ref[...] = a[:, 384:512].astype(BF16)
    a = _dot(xn, w_ref[:, 512:1024])
    for r in range(4):
        kb_ref[r] = a[:, r * LANE:(r + 1) * LANE].astype(BF16)

    ft = _dot_nt(wt_ref[...], xn)

    def feat(row0, rows=LANE):
        return ft[row0:row0 + rows, :]
    for r in range(4):
        qa_ref[r] = (feat(r * LANE) * (NSA_HD ** -0.5 * LOG2E)).astype(BF16)
        qb_ref[r] = (feat(512 + r * LANE) * (DIFF_HD ** -0.5 * LOG2E)).astype(BF16)
        vbt_ref[r] = feat(1280 + r * LANE).astype(BF16)
    vst_ref[...] = feat(1024).astype(BF16)
    vwt_ref[...] = feat(1152).astype(BF16)
    gt_ref[...] = feat(1792, GATE_ROWS)


def _in_proj(x2, g, w, wt):
    T = x2.shape[0]
    tm = TM_PROJ
    row = lambda i: (i, 0)
    o128b = jax.ShapeDtypeStruct((T, LANE), BF16)
    o128f = jax.ShapeDtypeStruct((T, LANE), F32)
    o4 = jax.ShapeDtypeStruct((4, T, LANE), BF16)
    ot = jax.ShapeDtypeStruct((LANE, T), BF16)
    o4t = jax.ShapeDtypeStruct((4, LANE, T), BF16)
    s128 = pl.BlockSpec((tm, LANE), row)
    s4 = pl.BlockSpec((4, tm, LANE), lambda i: (0, i, 0))
    st = pl.BlockSpec((LANE, tm), lambda i: (0, i))
    s4t = pl.BlockSpec((4, LANE, tm), lambda i: (0, 0, i))
    return pl.pallas_call(
        _inproj_kernel,
        grid=(T // tm,),
        in_specs=[pl.BlockSpec((tm, D_MODEL), row),
                  pl.BlockSpec((1, D_MODEL), lambda i: (0, 0)),
                  pl.BlockSpec(w.shape, lambda i: (0, 0)),
                  pl.BlockSpec(wt.shape, lambda i: (0, 0))],
        out_specs=[s4t, s128, s128, s128, s128, s4t, s4, st, st, s4t, pl.BlockSpec((GATE_ROWS, tm), lambda i: (0, i))],
        out_shape=[o4t, o128f, o128f, o128b, o128b, o4t, o4, ot, ot, o4t, jax.ShapeDtypeStruct((GATE_ROWS, T), F32)],
        compiler_params=_cparams(("parallel",)),
        name="in_proj",
    )(x2, g, w, wt)


def _gelu_tanh(x):
    return 0.5 * x * (1.0 + jnp.tanh(math.sqrt(2.0 / math.pi) * (x + 0.044715 * (x * x * x))))


def _compress_kernel(kc_ref, vc_ref, posk_ref, posv_ref, w1k_ref, w1v_ref, w2k_ref, w2v_ref, ko_ref, vo_ref):
    nrow = kc_ref.shape[0] // CMP_STRIDE
    rid = lax.broadcasted_iota(jnp.int32, (nrow, 1), 0)
    cid = lax.broadcasted_iota(jnp.int32, (1, nrow), 1)
    for src, pos, w1, w2, out, transposed in ((kc_ref, posk_ref, w1k_ref, w2k_ref, ko_ref, False),
                                              (vc_ref, posv_ref, w1v_ref, w2v_ref, vo_ref, True)):
        hid_a = jnp.zeros((nrow, 2 * CMP_HIDDEN), F32)
        hid_b = jnp.zeros((nrow, 2 * CMP_HIDDEN), F32)
        for m in range(CMP_STRIDE):
            y = src[pl.ds(m, nrow, stride=CMP_STRIDE), :]
            hid_a = hid_a + _dot((y + pos[m:m + 1, :]).astype(BF16), w1[m])
            hid_b = hid_b + _dot((y + pos[CMP_STRIDE + m:CMP_STRIDE + m + 1, :]).astype(BF16), w1[CMP_STRIDE + m])
        hid = hid_a + pltpu.roll(hid_b, nrow - 1, 0)
        o = _dot(_gelu_tanh(hid).astype(BF16), w2[...])
        if transposed:
            out[0] = jnp.where(cid < nrow - 1, o.T, 0.0).astype(BF16)
        else:
            out[0] = jnp.where(rid < nrow - 1, o, 0.0).astype(BF16)


def _compress(kc, vc, posk, posv, w1k, w1v, w2k, w2v, B, S):
    nrow = S // CMP_STRIDE
    assert nrow == LANE
    full = lambda a: pl.BlockSpec(a.shape, lambda b: (0,) * a.ndim)
    src = pl.BlockSpec((S, LANE), lambda b: (b, 0))
    osp = pl.BlockSpec((1, nrow, LANE), lambda b: (b, 0, 0))
    osh = jax.ShapeDtypeStruct((B, nrow, LANE), BF16)
    return pl.pallas_call(
        _compress_kernel,
        grid=(B,),
        in_specs=[src, src, full(posk), full(posv), full(w1k), full(w1v), full(w2k), full(w2v)],
        out_specs=[osp, osp],
        out_shape=[osh, osh],
        compiler_params=_cparams(("parallel",)),
        name="compress",
    )(kc, vc, posk, posv, w1k, w1v, w2k, w2v)


def _bias_from_dist(dist, tab_ref, h):
    val = jnp.full(dist.shape, tab_ref[h, 0], F32)
    for b in range(1, REL_BUCKETS):
        val = jnp.where(dist >= _THR[b], tab_ref[h, b], val)
    return val


def _bias_near_kernel(tab_ref, out_ref):
    h = pl.program_id(0)
    j = lax.broadcasted_iota(jnp.int32, (TQ, TQ), 0)
    i = lax.broadcasted_iota(jnp.int32, (TQ, TQ), 1)
    for d in range(2):
        bias = (_bias_from_dist(i - j + d * TQ, tab_ref, h) - tab_ref[h, REL_BUCKETS - 1]) * LOG2E
        out_ref[0, d] = jnp.where(i - j + d * TQ >= 0, bias, NEG)


def _bias_cmp_kernel(tab_ref, out_ref):
    h = pl.program_id(0)
    c = lax.broadcasted_iota(jnp.int32, (LANE, TQ), 0)
    t = pl.program_id(1) * TQ + lax.broadcasted_iota(jnp.int32, (LANE, TQ), 1)
    out_ref[0] = _bias_from_dist(t - (c * CMP_STRIDE + CMP_LEN - 1), tab_ref, h) * LOG2E


def _bias_tiles(tab_t, S):
    smem = pl.BlockSpec(memory_space=pltpu.SMEM)
    near = pl.pallas_call(
        _bias_near_kernel,
        grid=(N_REL_HEADS,),
        in_specs=[smem],
        out_specs=pl.BlockSpec((1, 2, TQ, TQ), lambda h: (h, 0, 0, 0)),
        out_shape=jax.ShapeDtypeStruct((N_REL_HEADS, 2, TQ, TQ), F32),
        compiler_params=_cparams(("parallel",)),
        name="bias_near",
    )(tab_t)
    cmp_bias = pl.pallas_call(
        _bias_cmp_kernel,
        grid=(NSA_HEADS, S // TQ),
        in_specs=[smem],
        out_specs=pl.BlockSpec((1, LANE, TQ), lambda h, q: (h, 0, q)),
        out_shape=jax.ShapeDtypeStruct((NSA_HEADS, LANE, S), F32),
        compiler_params=_cparams(("parallel", "parallel")),
        name="bias_cmp",
    )(tab_t)
    return near, cmp_bias


ACC_ROWS = NSA_HD + 16


def _with_ones(vt):
    return jnp.concatenate([vt, jnp.ones((ACC_ROWS - vt.shape[0], vt.shape[1]), BF16)], axis=0)


def _probs(s, m):
    return jnp.exp2((s - m).astype(BF16))


def _flash_first(ss, vts):
    ms = [jnp.max(s, axis=0, keepdims=True) for s in ss]
    ps = [_probs(s, m) for s, m in zip(ss, ms)]
    return tuple((m, _dot(vt, p)) for m, p, vt in zip(ms, ps, vts))


def _flash_update(ss, vts, sts):
    ms = [jnp.maximum(st[0], jnp.max(s, axis=0, keepdims=True)) for s, st in zip(ss, sts)]
    alphas = [jnp.exp2(st[0] - m) for m, st in zip(ms, sts)]
    ps = [_probs(s, m) for s, m in zip(ss, ms)]
    return tuple((m, a * st[1] + _dot(vt, p)) for m, a, st, p, vt in zip(ms, alphas, sts, ps, vts))


def _flash_out(st):
    acc = st[1]
    return acc[0:NSA_HD, :] * (1.0 / acc[NSA_HD:NSA_HD + 1, :])


def _stream_scratch(n):
    return ([pltpu.VMEM((8, TQ), F32), pltpu.VMEM((ACC_ROWS, TQ), F32)] * n
            + [pltpu.VMEM((TQ, TQ), F32)] * (2 * n) + [pltpu.VMEM((TQ, TQ), BF16)] * (2 * n))


def _stream_refs(refs, n):
    st_refs = tuple((refs[2 * c], refs[2 * c + 1]) for c in range(n))
    s, p = refs[2 * n:4 * n], refs[4 * n:6 * n]
    return st_refs, tuple(s[:n]), tuple(s[n:]), tuple(p[:n]), tuple(p[n:])


def _causal_stream(qi, scores, vtile, st_refs, s_a, s_b, p_a, p_b, diag_ready=False):
    n_far = jnp.maximum(qi - 1, 0)
    top = qi - 2

    chains = range(len(st_refs))

    def write(s_refs, kt, near=2):
        for c in chains:
            s_refs[c][...] = scores(jnp.maximum(kt, 0), near, c)

    def pending(p_refs, kt):
        return [_dot(vtile(jnp.maximum(kt, 0), c), p_refs[c][...]) for c in chains]

    def step(s_cur, p_cur, s_next, kt_next, p_prev, kt_prev, anchor=None):
        rescale = []
        for c in chains:
            nxt = scores(jnp.maximum(kt_next, 0), 2, c)
            s_next[c][...] = nxt
            pv = _dot(vtile(jnp.maximum(kt_prev, 0), c), p_prev[c][...])
            st = st_refs[c][0]
            s = s_cur[c][...]
            m_old = st[0:1, :]
            m = jnp.maximum(m_old, jnp.max(s, axis=0, keepdims=True))
            if anchor is not None:
                m = m + anchor
            anchor = nxt[0:1, :] * 0.0
            p_cur[c][...] = _probs(s, m)
            st[0:1, :] = m
            rescale.append((jnp.exp2(m_old - m), pv))
        for c in chains:
            acc = st_refs[c][1]
            acc[...] = rescale[c][0] * (acc[...] + rescale[c][1])
        return anchor

    if not diag_ready:
        write(s_a, qi, 0)
    for c in chains:
        s_b[c][...] = scores(jnp.maximum(qi - 1, 0), 1, c)
        st, acc = st_refs[c]
        s = s_a[c][...]
        m = jnp.max(s, axis=0, keepdims=True)
        p_a[c][...] = _probs(s, m)
        st[0:1, :] = m
        acc[...] = jnp.zeros(acc.shape, F32)
    step(s_b, p_b, s_a, top, p_a, qi)

    def pair(j, c):
        kt = top - 2 * j
        anchor = step(s_a, p_a, s_b, kt - 1, p_b, kt + 1)
        step(s_b, p_b, s_a, kt - 2, p_a, kt, anchor)
        return c
    lax.fori_loop(0, n_far // 2, pair, 0)

    @pl.when(n_far % 2 == 1)
    def _():
        for c, pv in zip(chains, pending(p_b, 1)):
            st, acc = st_refs[c]
            s = s_a[c][...]
            m_old = st[0:1, :]
            m = jnp.maximum(m_old, jnp.max(s, axis=0, keepdims=True))
            st[0:1, :] = m
            acc[...] = jnp.exp2(m_old - m) * (acc[...] + pv) + _dot(vtile(0, c), _probs(s, m))

    @pl.when(n_far % 2 == 0)
    def _():
        last = jnp.where(n_far == 0, qi - 1, 0)
        for c, pv in zip(chains, pending(p_b, last)):
            acc = st_refs[c][1]
            acc[...] = acc[...] + pv
    return tuple((st[0:1, :], acc[...]) for st, acc in st_refs)


def _ktile(ref, kt):
    return ref[pl.ds(pl.multiple_of(kt * TQ, TQ), TQ), :]


def _vtile(ref, kt):
    return ref[:, pl.ds(pl.multiple_of(kt * TQ, TQ), TQ)]


def _nsa_kernel(qa_ref, gt_ref, kcmp_ref, vcmpt_ref, bc_ref, ks_ref, vst_ref, kw_ref, vwt_ref, dn_ref,
                ov_ref, o_ref, psum_ref, sel_ref, oacc_ref, sig_ref, *scratch):
    st_refs, s_a, s_b, p_a, p_b = _stream_refs(scratch, NSA_BLOCKS_PER_STEP * NSA_G)
    qi = pl.program_id(1)
    t0 = qi * TQ
    sub_grp = lax.shift_right_arithmetic(lax.broadcasted_iota(jnp.int32, (LANE, 1), 0), 6)
    sig_ref[...] = jax.nn.sigmoid(gt_ref[...])

    def gate_row(c):
        return sig_ref[pl.ds(c, 1), :]

    def masked_q(r, g):
        return jnp.where(sub_grp == g, qa_ref[r].astype(F32), 0.0).astype(BF16)

    n_cmp = kcmp_ref.shape[1] - 1
    crow = lax.broadcasted_iota(jnp.int32, (LANE, 1), 0)
    cmp_end = jnp.where(crow < n_cmp, crow * CMP_STRIDE + CMP_LEN - 1, 1 << 30)
    mask_c = (t0 + lax.broadcasted_iota(jnp.int32, (1, TQ), 1)) >= cmp_end

    heads = [(r, g) for r in range(NSA_R) for g in range(NSA_G)]
    scores_c = [_dot(kcmp_ref[0], masked_q(r, g)) for r, g in heads]
    probs_c = []
    for (r, g), s in zip(heads, scores_c):
        s = jnp.where(mask_c, s + bc_ref[g * NSA_R + r], NEG)
        p = jnp.where(mask_c, jnp.exp2(s - jnp.max(s, axis=0, keepdims=True)), 0.0)
        l = jnp.sum(p, axis=0, keepdims=True)
        probs_c.append(p * jnp.where(l > 0.0, 1.0 / l, 0.0))
    for g in range(NSA_G):
        psum_ref[g] = sum(p for (r, gg), p in zip(heads, probs_c) if gg == g)
    outs_c = [_dot(vcmpt_ref[0], p.astype(BF16)) for p in probs_c]
    for r in range(NSA_R):
        o0, o1 = outs_c[r * NSA_G], outs_c[r * NSA_G + 1]
        oacc_ref[r] = jnp.where(sub_grp == 0, gate_row(r * 3) * o0, gate_row((NSA_R + r) * 3) * o1)

    n_sel = ov_ref.shape[0]
    jj = lax.broadcasted_iota(jnp.int32, (n_sel, TQ), 0)
    cur = lax.shift_right_arithmetic(t0 + lax.broadcasted_iota(jnp.int32, (n_sel, TQ), 1), 6)
    valid = jj <= cur
    forced = (jj == 0) | (cur - jj < SEL_FORCED_LOCAL)
    for g in range(NSA_G):
        ps = psum_ref[g]
        hi = ps.astype(BF16)
        rem = ps - hi.astype(F32)
        mid = rem.astype(BF16)
        lo = (rem - mid.astype(F32)).astype(BF16)
        ov = ov_ref[...]
        imp = _dot(ov, hi) + _dot(ov, mid) + _dot(ov, lo)
        score = jnp.where(valid, jnp.where(forced, 1e9, imp), -1e9)
        cnt = jnp.zeros((n_sel, TQ), F32)
        for j2 in range(n_sel):
            row = score[j2:j2 + 1, :]
            tie = jnp.where(jj > j2, 1.0, 0.0)
            cnt = cnt + jnp.where(row > score, 1.0, jnp.where(row == score, tie, 0.0))
        sel_ref[g] = jnp.where(cnt < float(min(SEL_TOPN, n_sel)), jnp.where(score > -1e8, 0.0, NEG), NEG)

    has_prev = jnp.where(qi >= 1, 0.0, NEG)
    has_wfar = jnp.where(qi >= WINDOW // TQ, 0.0, NEG)
    kt_prev = jnp.maximum(qi - 1, 0)
    kt_wfar = jnp.maximum(qi - WINDOW // TQ, 0)
    wfar_mask = lax.broadcasted_iota(jnp.int32, (TQ, TQ), 0) > lax.broadcasted_iota(jnp.int32, (TQ, TQ), 1)

    def main_body(rp, carry):
        chains = [(NSA_BLOCKS_PER_STEP * rp + i, g) for i in range(NSA_BLOCKS_PER_STEP) for g in range(NSA_G)]
        qms = [masked_q(r, g) for r, g in chains]
        hids = [g * NSA_R + r for r, g in chains]

        def block_mask(g, kt):
            per_tile = TQ // SEL_BLOCK
            rows = [jnp.broadcast_to(sel_ref[g, pl.ds(kt * per_tile + i, 1), :], (SEL_BLOCK, TQ))
                    for i in range(per_tile)]
            return jnp.concatenate(rows, axis=0)

        def near_bias(s, near, c):
            if near == 0:
                return s + dn_ref[hids[c], 0]
            if near == 1:
                return s + dn_ref[hids[c], 1] + has_prev
            return s

        def group_values(ref, kt, c):
            g = chains[c][1]
            return _with_ones(ref[g * NSA_HD:(g + 1) * NSA_HD, pl.ds(pl.multiple_of(kt * TQ, TQ), TQ)])

        def finish(sts, branch):
            for c, (r, g) in enumerate(chains):
                oacc_ref[r, g * NSA_HD:(g + 1) * NSA_HD, :] += gate_row(hids[c] * 3 + branch) * _flash_out(sts[c])

        def slc_scores(kt, near, c):
            return near_bias(_dot(_ktile(ks_ref, kt), qms[c]), near, c) + block_mask(chains[c][1], kt)

        def slc_vtile(kt, c):
            return group_values(vst_ref, kt, c)

        finish(_causal_stream(qi, slc_scores, slc_vtile, st_refs, s_a, s_b, p_a, p_b), 1)

        win_tiles = ((qi, 0), (kt_prev, 1), (kt_wfar, 2))

        def win_scores(c):
            out = []
            for kt, near in win_tiles:
                s = near_bias(_dot(_ktile(kw_ref, kt), qms[c]), near, c)
                out.append(jnp.where(wfar_mask, s + has_wfar, NEG) if near == 2 else s)
            return out

        sts = []
        ahead = win_scores(0)
        for c in range(len(chains)):
            cur, ahead = ahead, (win_scores(c + 1) if c + 1 < len(chains) else None)
            st = _flash_first([cur[0]], [group_values(vwt_ref, qi, c)])
            st = _flash_update([cur[1]], [group_values(vwt_ref, kt_prev, c)], st)
            sts.append(_flash_update([cur[2]], [group_values(vwt_ref, kt_wfar, c)], st)[0])
        finish(sts, 2)
        return carry
    lax.fori_loop(0, NSA_R // NSA_BLOCKS_PER_STEP, main_body, 0)

    for r in range(NSA_R):
        o_ref[r] = oacc_ref[r].T.astype(BF16)


def _nsa(qa, gates_t, kcmp, vcmpt, bias_c, ks, vst, kw, vwt, near_a, ov, B, S):
    nq = S // TQ
    T = B * S
    k_spec = pl.BlockSpec((S, LANE), lambda b, q: (b, 0))
    v_spec = pl.BlockSpec((LANE, S), lambda b, q: (0, b))
    full = lambda a: pl.BlockSpec(a.shape, lambda b, q: (0,) * a.ndim)
    cmp_spec = pl.BlockSpec((1, LANE, LANE), lambda b, q: (b, 0, 0))
    qo = pl.BlockSpec((4, TQ, LANE), lambda b, q: (0, b * nq + q, 0))
    return pl.pallas_call(
        _nsa_kernel,
        grid=(B, nq),
        in_specs=[pl.BlockSpec((4, LANE, TQ), lambda b, q: (0, 0, b * nq + q)),
                  pl.BlockSpec((GATE_ROWS, TQ), lambda b, q: (0, b * nq + q)),
                  cmp_spec, cmp_spec,
                  pl.BlockSpec((NSA_HEADS, LANE, TQ), lambda b, q: (0, 0, q)),
                  k_spec, v_spec, k_spec, v_spec, full(near_a), full(ov)],
        out_specs=qo,
        out_shape=jax.ShapeDtypeStruct((4, T, LANE), BF16),
        scratch_shapes=[pltpu.VMEM((NSA_G, LANE, TQ), F32), pltpu.VMEM((NSA_G, S // SEL_BLOCK, TQ), F32),
                        pltpu.VMEM((NSA_R, LANE, TQ), F32), pltpu.VMEM((GATE_ROWS, TQ), F32)]
                       + _stream_scratch(NSA_BLOCKS_PER_STEP * NSA_G),
        compiler_params=_cparams(("parallel", "arbitrary")),
        name="nsa",
    )(qa, gates_t, kcmp, vcmpt, bias_c, ks, vst, kw, vwt, near_a, ov)


def _diff_kernel(lq1_ref, lk1_ref, lq2_ref, lk2_ref, sub_ref, qb_ref, kb_ref, vbt_ref, dn_ref, o_ref, *scratch):
    n = DIFF_PAIRS_PER_STEP * 4
    st_refs, s_a, s_b, p_a, p_b = _stream_refs(scratch, n)
    qi = pl.program_id(1)
    chain_of_row = lax.shift_right_arithmetic(lax.broadcasted_iota(jnp.int32, (LANE, 1), 0), 5)
    lam = (jnp.exp(jnp.sum(lq1_ref[...] * lk1_ref[...], axis=-1, keepdims=True))
           - jnp.exp(jnp.sum(lq2_ref[...] * lk2_ref[...], axis=-1, keepdims=True)) + LAMBDA_INIT)
    has_prev = jnp.where(qi >= 1, 0.0, NEG)
    n_steps = DIFF_HEADS // 2 // DIFF_PAIRS_PER_STEP

    def score_fn(step):
        pairs = [step * DIFF_PAIRS_PER_STEP + c // 4 for c in range(n)]
        qs = [qb_ref[step * DIFF_PAIRS_PER_STEP + i].astype(F32) for i in range(DIFF_PAIRS_PER_STEP)]
        qms = [jnp.where(chain_of_row == c % 4, qs[c // 4], 0.0).astype(BF16) for c in range(n)]

        def scores(kt, near, c):
            s = _dot(kb_ref[pairs[c], pl.ds(pl.multiple_of(kt * TQ, TQ), TQ), :], qms[c])
            if near == 2:
                return s
            bias = dn_ref[2 * pairs[c] + (c % 4) // 2, near]
            return s + bias if near == 0 else s + bias + has_prev
        return scores

    def start_diag(step):
        scores = score_fn(step)
        for c in range(n):
            s_a[c][...] = scores(qi, 0, c)

    start_diag(0)

    def step_body(step, carry):
        def vtile(kt, c):
            hh = (c % 4) // 2
            return _with_ones(vbt_ref[step * DIFF_PAIRS_PER_STEP + c // 4, hh * 2 * DIFF_HD:(hh + 1) * 2 * DIFF_HD,
                                      pl.ds(pl.multiple_of(kt * TQ, TQ), TQ)])

        sts = _causal_stream(qi, score_fn(step), vtile, st_refs, s_a, s_b, p_a, p_b, diag_ready=True)
        start_diag(jnp.minimum(step + 1, n_steps - 1))
        for i in range(DIFF_PAIRS_PER_STEP):
            outs = []
            for hh in range(2):
                c = 4 * i + 2 * hh
                o = _flash_out(sts[c]) - lam * _flash_out(sts[c + 1])
                outs.append(o * lax.rsqrt(jnp.mean(o * o, axis=0, keepdims=True) + EPS))
            out = jnp.concatenate(outs, axis=0) * sub_ref[...] * (1.0 - LAMBDA_INIT)
            o_ref[step * DIFF_PAIRS_PER_STEP + i] = out.T.astype(BF16)
        return carry
    lax.fori_loop(0, n_steps, step_body, 0)


def _diff(lq1, lk1, lq2, lk2, sub, qb, kb, vbt, near_b, B, S):
    nq = S // TQ
    T = B * S
    full = lambda a: pl.BlockSpec(a.shape, lambda b, q: (0,) * a.ndim)
    k_spec = pl.BlockSpec((4, S, LANE), lambda b, q: (0, b, 0))
    v_spec = pl.BlockSpec((4, LANE, S), lambda b, q: (0, 0, b))
    qo = pl.BlockSpec((4, TQ, LANE), lambda b, q: (0, b * nq + q, 0))
    return pl.pallas_call(
        _diff_kernel,
        grid=(B, nq),
        in_specs=[full(lq1), full(lk1), full(lq2), full(lk2), full(sub),
                  pl.BlockSpec((4, LANE, TQ), lambda b, q: (0, 0, b * nq + q)), k_spec, v_spec, full(near_b)],
        out_specs=qo,
        out_shape=jax.ShapeDtypeStruct((4, T, LANE), BF16),
        scratch_shapes=_stream_scratch(DIFF_PAIRS_PER_STEP * 4),
        compiler_params=_cparams(("parallel", "arbitrary")),
        name="diff",
    )(lq1, lk1, lq2, lk2, sub, qb, kb, vbt, near_b)


def _outproj_kernel(x_ref, oa_ref, ob_ref, w_ref, g_ref, wr_ref, br_ref, h_ref, xt_ref, route_ref, cnt_ref,
                    carry_ref):
    @pl.when(pl.program_id(0) == 0)
    def _():
        carry_ref[...] = jnp.zeros(carry_ref.shape, F32)

    o = jnp.concatenate([oa_ref[r] for r in range(4)] + [ob_ref[r] for r in range(4)], axis=-1)
    h = x_ref[...] + _dot(o, w_ref[...])
    h_ref[...] = h
    tn32 = h * lax.rsqrt(jnp.mean(h * h, axis=-1, keepdims=True) + EPS) * g_ref[...]
    xt_ref[:, 0:D_MODEL] = tn32
    tn = tn32.astype(BF16)
    logits = _dot(tn, wr_ref[...]) + br_ref[...]
    lane = lax.broadcasted_iota(jnp.int32, (1, LANE), 1)
    lane_f = lane.astype(F32)
    is_grp = lane < MOE_GROUPS
    lg = jnp.where(is_grp, logits, NEG)
    mg = jnp.max(lg, axis=-1, keepdims=True)
    zg = jnp.sum(jnp.where(is_grp, jnp.exp(lg - mg), 0.0), axis=-1, keepdims=True)
    g_prob = 1.0 / zg
    g_idx = jnp.min(jnp.where(lg == mg, lane_f, 1e9), axis=-1, keepdims=True)
    lane_grp = jnp.where((lane >= MOE_GROUPS) & (lane < MOE_GROUPS + N_EXPERTS),
                         lax.shift_right_arithmetic(lane - MOE_GROUPS, 3), -1).astype(F32)
    le = jnp.where(lane_grp == g_idx, logits, NEG)
    m1 = jnp.max(le, axis=-1, keepdims=True)
    e1 = jnp.min(jnp.where(le == m1, lane_f, 1e9), axis=-1, keepdims=True)
    le2 = jnp.where(lane_f == e1, NEG, le)
    m2 = jnp.max(le2, axis=-1, keepdims=True)
    e2 = jnp.min(jnp.where(le2 == m2, lane_f, 1e9), axis=-1, keepdims=True)
    ratio = jnp.exp(m2 - m1)
    w1 = g_prob / (1.0 + ratio)
    w2 = w1 * ratio
    xt_ref[:, D_MODEL:D_MODEL + LANE] = jnp.where(lane_f == e1, w1, 0.0) + jnp.where(lane_f == e2, w2, 0.0)

    tm = h.shape[0]
    onehot = jnp.where(lane_f == g_idx, 1.0, 0.0)
    earlier = jnp.where(lax.broadcasted_iota(jnp.int32, (tm, tm), 0) > lax.broadcasted_iota(jnp.int32, (tm, tm), 1),
                        1.0, 0.0).astype(BF16)
    prefix = _dot(earlier, onehot.astype(BF16)) + carry_ref[...]
    rank = jnp.sum(onehot * prefix, axis=-1, keepdims=True)
    carry_ref[...] += jnp.sum(onehot, axis=0, keepdims=True)
    route_ref[...] = jnp.where(lane == 0, g_idx, jnp.where(lane == 1, rank, 0.0))
    cnt_ref[...] = jnp.broadcast_to(carry_ref[...], cnt_ref.shape)


def _out_proj(x2, oa, ob, w, g, wr, br):
    T = x2.shape[0]
    tm = TM_PROJ
    row = lambda i: (i, 0)
    full = lambda a: pl.BlockSpec(a.shape, lambda i: (0,) * a.ndim)
    o4 = pl.BlockSpec((4, tm, LANE), lambda i: (0, i, 0))
    return pl.pallas_call(
        _outproj_kernel,
        grid=(T // tm,),
        in_specs=[pl.BlockSpec((tm, D_MODEL), row), o4, o4, full(w), full(g), full(wr), full(br)],
        out_specs=[pl.BlockSpec((tm, D_MODEL), row), pl.BlockSpec((tm, XT_WIDTH), row), pl.BlockSpec((tm, LANE), row),
                   pl.BlockSpec((8, LANE), lambda i: (0, 0))],
        out_shape=[jax.ShapeDtypeStruct((T, D_MODEL), F32), jax.ShapeDtypeStruct((T, XT_WIDTH), F32),
                   jax.ShapeDtypeStruct((T, LANE), F32), jax.ShapeDtypeStruct((8, LANE), F32)],
        scratch_shapes=[pltpu.VMEM((1, LANE), F32)],
        compiler_params=_cparams(("arbitrary",)),
        name="out_proj",
    )(x2, oa, ob, w, g, wr, br)


def _moe_kernel(tg_ref, ok_ref, idx_ref, idxn_ref, xt_hbm, wg_ref, wu_ref, wd_ref, y_ref, xbuf, sem):
    i = pl.program_id(0)
    slot = i % 2
    tm = xbuf.shape[1]

    def row_copy(index_ref, r, dst_slot):
        return pltpu.make_async_copy(xt_hbm.at[pl.ds(index_ref[0, 0, r], 1), :],
                                     xbuf.at[dst_slot, pl.ds(r, 1), :], sem.at[dst_slot])

    @pl.when(i == 0)
    def _():
        def body(r, c):
            row_copy(idx_ref, r, 0).start()
            return c
        lax.fori_loop(0, tm, body, 0)

    @pl.when((i == 0) | (ok_ref[jnp.maximum(i - 1, 0)] == 1))
    def _():
        pltpu.make_async_copy(xt_hbm.at[pl.ds(0, tm), :], xbuf.at[slot], sem.at[slot]).wait()

    @pl.when(ok_ref[i] == 0)
    def _():
        y_ref[...] = jnp.zeros(y_ref.shape, F32)

    @pl.when(ok_ref[i] == 1)
    def _():
        lane = lax.broadcasted_iota(jnp.int32, (1, LANE), 1)
        x = xbuf[slot, :, 0:D_MODEL].astype(BF16)
        cmb = xbuf[slot, :, D_MODEL:D_MODEL + LANE]
        first_lane = MOE_GROUPS + tg_ref[i] * EPG
        per_expert = tm // EPG
        y = jnp.zeros((tm, D_MODEL), F32)
        for e in range(EPG):
            a = _dot(x, wg_ref[0, e])
            b = _dot(x, wu_ref[0, e])
            ce = jnp.sum(jnp.where(lane == first_lane + e, cmb, 0.0), axis=-1, keepdims=True)
            y = y + _dot(((a * jax.nn.sigmoid(a)) * b * ce).astype(BF16), wd_ref[0, e])
            for r in range(e * per_expert, (e + 1) * per_expert):
                row_copy(idxn_ref, r, 1 - slot).start(priority=r % 2)
        for blk in range(ROW_SLABS):
            y_ref[:, blk, :] = y[:, blk * LANE:(blk + 1) * LANE]


def _moe(tile_group, tile_ok, src3, xt, wg, wu, wd):
    n_tiles, _, tm = src3.shape
    last = n_tiles - 1
    grid_spec = pltpu.PrefetchScalarGridSpec(
        num_scalar_prefetch=2,
        grid=(n_tiles,),
        in_specs=[pl.BlockSpec((1, 1, tm), lambda i, tg, ok: (i, 0, 0), memory_space=pltpu.SMEM),
                  pl.BlockSpec((1, 1, tm), lambda i, tg, ok: (jnp.minimum(i + 1, last), 0, 0), memory_space=pltpu.SMEM),
                  pl.BlockSpec(memory_space=pl.ANY),
                  pl.BlockSpec((1, EPG, D_MODEL, EXPERT_FF), lambda i, tg, ok: (tg[i], 0, 0, 0)),
                  pl.BlockSpec((1, EPG, D_MODEL, EXPERT_FF), lambda i, tg, ok: (tg[i], 0, 0, 0)),
                  pl.BlockSpec((1, EPG, EXPERT_FF, D_MODEL), lambda i, tg, ok: (tg[i], 0, 0, 0))],
        out_specs=pl.BlockSpec((tm, ROW_SLABS, LANE), lambda i, tg, ok: (i, 0, 0)),
        scratch_shapes=[pltpu.VMEM((2, tm, XT_WIDTH), F32), pltpu.SemaphoreType.DMA((2,))],
    )
    return pl.pallas_call(
        _moe_kernel,
        grid_spec=grid_spec,
        out_shape=jax.ShapeDtypeStruct((n_tiles * tm, ROW_SLABS, LANE), F32),
        compiler_params=pltpu.CompilerParams(dimension_semantics=("arbitrary",), vmem_limit_bytes=VMEM_LIMIT_MOE),
        name="moe",
    )(tile_group, tile_ok, src3, src3, xt, wg, wu, wd)


def _final_kernel(pos_ref, posn_ref, y_hbm, h_ref, gf_ref, o_ref, ybuf, sem):
    i = pl.program_id(0)
    slot = i % 2
    tm = h_ref.shape[0]

    def issue(index_ref, dst_slot):
        group = 8

        def body(j, c):
            for k in range(group):
                r = j * group + k
                pltpu.make_async_copy(y_hbm.at[index_ref[0, 0, r]], ybuf.at[dst_slot, pl.ds(r * ROW_SLABS, ROW_SLABS), :],
                                      sem.at[dst_slot]).start(priority=k % 2)
            return c
        lax.fori_loop(0, tm // group, body, 0)

    @pl.when(i == 0)
    def _():
        issue(pos_ref, 0)

    @pl.when(i + 1 < pl.num_programs(0))
    def _():
        issue(posn_ref, 1 - slot)

    pltpu.make_async_copy(ybuf.at[slot], ybuf.at[slot], sem.at[slot]).wait()
    y = jnp.concatenate([ybuf[slot, pl.ds(blk, tm, stride=ROW_SLABS), :] for blk in range(ROW_SLABS)], axis=-1)
    h = h_ref[...] + y
    o_ref[...] = h * lax.rsqrt(jnp.mean(h * h, axis=-1, keepdims=True) + EPS) * gf_ref[...]


def _final(pos3, y_sorted, h, gf):
    n_tiles, _, tm = pos3.shape
    last = n_tiles - 1
    row = lambda i: (i, 0)
    return pl.pallas_call(
        _final_kernel,
        grid=(n_tiles,),
        in_specs=[pl.BlockSpec((1, 1, tm), lambda i: (i, 0, 0), memory_space=pltpu.SMEM),
                  pl.BlockSpec((1, 1, tm), lambda i: (jnp.minimum(i + 1, last), 0, 0), memory_space=pltpu.SMEM),
                  pl.BlockSpec(memory_space=pl.ANY),
                  pl.BlockSpec((tm, D_MODEL), row),
                  pl.BlockSpec((1, D_MODEL), lambda i: (0, 0))],
        out_specs=pl.BlockSpec((tm, D_MODEL), row),
        out_shape=jax.ShapeDtypeStruct((n_tiles * tm, D_MODEL), F32),
        scratch_shapes=[pltpu.VMEM((2, tm * ROW_SLABS, LANE), F32), pltpu.SemaphoreType.DMA((2,))],
        compiler_params=_cparams(("arbitrary",)),
        name="final",
    )(pos3, pos3, y_sorted, h, gf)


def _qa_perm():
    new = np.arange(NSA_HEADS * NSA_HD)
    r, g, d = new // LANE, (new % LANE) // NSA_HD, new % NSA_HD
    return (g * NSA_R + r) * NSA_HD + d


def _block_diag2(w):
    z = jnp.zeros_like(w)
    return jnp.concatenate([jnp.concatenate([w, z], axis=-1), jnp.concatenate([z, w], axis=-1)], axis=-2)


def kernel(x, rel_bias, ln_mix, w_in, cmp_pos_k, cmp_pos_v, cmp_k_w1, cmp_k_w2, cmp_v_w1, cmp_v_w2,
           diff_lq1, diff_lk1, diff_lq2, diff_lk2, diff_subln, w_out, ln_ffn,
           router_group_w, router_group_b, router_expert_w, router_expert_b,
           exp_w_gate, exp_w_up, exp_w_down, ln_final):
    B, S, D = x.shape
    T = B * S
    assert D == D_MODEL and S % TQ == 0 and S >= WINDOW and T % TM_MOE == 0 and T % TM_FINAL == 0
    x2 = x.reshape(T, D)
    perm = _qa_perm()

    w = w_in[0]
    c_kc, c_vc, c_ks, c_vs, c_kw, c_vw, c_gt = 512, 640, 768, 896, 1024, 1152, 1280
    c_qb = c_gt + N_GATE
    c_kb, c_vb = c_qb + 512, c_qb + 1024
    col = lambda c, n=LANE: w[:, c:c + n]
    w_tok = jnp.concatenate([col(c_kc), col(c_vc), col(c_ks), col(c_kw), col(c_kb, 512)], axis=1).astype(BF16)
    w_feat = jnp.concatenate([w[:, perm], col(c_qb, 512), col(c_vs), col(c_vw), col(c_vb, 512), col(c_gt, N_GATE),
                              jnp.zeros((D, GATE_ROWS - N_GATE), F32)], axis=1).T.astype(BF16)
    qa, kc, vc, ks, kw, qb, kb, vst, vwt, vbt, gates_t = _in_proj(x2, ln_mix[0][None, :], w_tok, w_feat)

    w1k = _block_diag2(cmp_k_w1[0].reshape(CMP_LEN, NSA_HD, CMP_HIDDEN)).astype(BF16)
    w1v = _block_diag2(cmp_v_w1[0].reshape(CMP_LEN, NSA_HD, CMP_HIDDEN)).astype(BF16)
    w2k = _block_diag2(cmp_k_w2[0]).astype(BF16)
    w2v = _block_diag2(cmp_v_w2[0]).astype(BF16)
    posk = jnp.tile(cmp_pos_k[0], (1, NSA_G))
    posv = jnp.tile(cmp_pos_v[0], (1, NSA_G))
    kcmp, vcmpt = _compress(kc, vc, posk, posv, w1k, w1v, w2k, w2v, B, S)

    near, bias_c = _bias_tiles(rel_bias.T, S)

    n_sel = S // SEL_BLOCK
    nrow = S // CMP_STRIDE
    c_start = np.arange(nrow) * CMP_STRIDE
    s_start = np.arange(n_sel) * SEL_BLOCK
    ov = ((c_start[None, :] <= s_start[:, None] + SEL_BLOCK - 1)
          & (c_start[None, :] + CMP_LEN - 1 >= s_start[:, None])
          & (np.arange(nrow)[None, :] < nrow - 1)).astype(np.float32)
    o_a = _nsa(qa, gates_t, kcmp, vcmpt, bias_c, ks, vst, kw, vwt, near[:NSA_HEADS], jnp.asarray(ov, BF16), B, S)

    sub = jnp.tile(diff_subln[0], 2)[:, None]
    o_b = _diff(diff_lq1[0][None, :], diff_lk1[0][None, :], diff_lq2[0][None, :], diff_lk2[0][None, :],
                sub, qb, kb, vbt, near[NSA_HEADS:], B, S)

    w_o = jnp.concatenate([w_out[0][:512][perm], w_out[0][512:]], axis=0).astype(BF16)
    n_r = MOE_GROUPS + N_EXPERTS
    wr = jnp.concatenate([router_group_w[0], router_expert_w[0], jnp.zeros((D, LANE - n_r), F32)], axis=1).astype(BF16)
    br = jnp.concatenate([router_group_b[0], router_expert_b[0], jnp.zeros((LANE - n_r,), F32)])[None, :]
    h1, xt, route, counts = _out_proj(x2, o_a, o_b, w_o, ln_ffn[0][None, :], wr, br)

    tm = TM_MOE
    n_tiles = T // tm + MOE_GROUPS
    cnt = counts[0, :MOE_GROUPS].astype(jnp.int32)
    ends = jnp.cumsum((cnt + tm - 1) // tm * tm)
    starts = ends - (cnt + tm - 1) // tm * tm
    pos = starts[route[:, 0].astype(jnp.int32)] + route[:, 1].astype(jnp.int32)
    src = jnp.zeros((n_tiles * tm,), jnp.int32).at[pos].set(
        jnp.arange(T, dtype=jnp.int32), unique_indices=True, mode="promise_in_bounds")
    tile_start = jnp.arange(n_tiles, dtype=jnp.int32) * tm
    tile_group = jnp.minimum(jnp.searchsorted(ends, tile_start, side="right"), MOE_GROUPS - 1).astype(jnp.int32)
    tile_ok = (tile_start < ends[-1]).astype(jnp.int32)

    by_group = lambda a: a[0].astype(BF16).reshape((MOE_GROUPS, EPG) + a.shape[2:])
    y_sorted = _moe(tile_group, tile_ok, src.reshape(n_tiles, 1, tm), xt,
                    by_group(exp_w_gate), by_group(exp_w_up), by_group(exp_w_down))
    out = _final(pos.reshape(T // TM_FINAL, 1, TM_FINAL), y_sorted, h1, ln_final[None, :])
    return out.reshape(B, S, D)
```

```python
import math

import numpy as np
import jax
import jax.numpy as jnp
from jax import lax
from jax.experimental import pallas as pl
from jax.experimental.pallas import tpu as pltpu

F32 = jnp.float32
BF16 = jnp.bfloat16
NEG = -1e30
EPS = 1e-6
LOG2E = math.log2(math.e)

D_MODEL = 1024
LANE = 128
NSA_HEADS, NSA_G, NSA_R, NSA_HD = 8, 2, 4, 64
CMP_LEN, CMP_STRIDE, CMP_HIDDEN = 32, 16, 128
SEL_BLOCK, SEL_TOPN, SEL_FORCED_LOCAL, WINDOW = 64, 8, 2, 512
DIFF_HEADS, DIFF_HD = 8, 32
REL_BUCKETS, REL_MAX_EXACT, REL_MAX_DIST = 32, 16, 128
N_REL_HEADS = NSA_HEADS + DIFF_HEADS
MOE_GROUPS, EPG, N_EXPERTS, EXPERT_FF = 4, 8, 32, 256
LAMBDA_INIT = 0.8 - 0.6 * math.exp(-0.3 * 0)
N_GATE = NSA_HEADS * 3
GATE_ROWS = 32

TQ = 256
DIFF_PAIRS_PER_STEP = 4
NSA_BLOCKS_PER_STEP = 4
TM_PROJ = 512
TM_MOE = 512
TM_FINAL = 512
XT_WIDTH = D_MODEL + LANE
ROW_SLABS = D_MODEL // LANE
VMEM_LIMIT = 48 * 1024 * 1024
VMEM_LIMIT_MOE = 56 * 1024 * 1024


def _cparams(sem):
    return pltpu.CompilerParams(dimension_semantics=sem, vmem_limit_bytes=VMEM_LIMIT)


def _dot(a, b):
    return jnp.dot(a, b, preferred_element_type=F32)


def _dot_nt(a, b):
    return lax.dot_general(a, b, (((1,), (1,)), ((), ())), preferred_element_type=F32)


def _bucket_thresholds():
    n = np.arange(0, REL_MAX_DIST + 1)
    nf = np.maximum(n, 1).astype(np.float32)
    large = REL_MAX_EXACT + (np.log(nf / np.float32(REL_MAX_EXACT)) / np.float32(math.log(REL_MAX_DIST / REL_MAX_EXACT))
                             * np.float32(REL_BUCKETS - REL_MAX_EXACT)).astype(np.int32)
    large = np.minimum(large, REL_BUCKETS - 1)
    bucket = np.where(n < REL_MAX_EXACT, n, large)
    return [int(np.argmax(bucket >= b)) for b in range(REL_BUCKETS)]


_THR = _bucket_thresholds()


def _inproj_kernel(x_ref, g_ref, w_ref, wt_ref, qa_ref, kc_ref, vc_ref, ks_ref, kw_ref, qb_ref, kb_ref,
                   vst_ref, vwt_ref, vbt_ref, gt_ref):
    x = x_ref[...]
    xn = (x * lax.rsqrt(jnp.mean(x * x, axis=-1, keepdims=True) + EPS) * g_ref[...]).astype(BF16)
    a = _dot(xn, w_ref[:, 0:512])
    kc_ref[...] = a[:, 0:128]
    vc_ref[...] = a[:, 128:256]
    ks_ref[...] = a[:, 256:384].astype(BF16)
    kw_ref[...] = a[:, 384:512].astype(BF16)
    a = _dot(xn, w_ref[:, 512:1024])
    for r in range(4):
        kb_ref[r] = a[:, r * LANE:(r + 1) * LANE].astype(BF16)

    ft = _dot_nt(wt_ref[...], xn)

    def feat(row0, rows=LANE):
        return ft[row0:row0 + rows, :]
    for r in range(4):
        qa_ref[r] = (feat(r * LANE) * (NSA_HD ** -0.5 * LOG2E)).astype(BF16)
        qb_ref[r] = (feat(512 + r * LANE) * (DIFF_HD ** -0.5 * LOG2E)).astype(BF16)
        vbt_ref[r] = feat(1280 + r * LANE).astype(BF16)
    vst_ref[...] = feat(1024).astype(BF16)
    vwt_ref[...] = feat(1152).astype(BF16)
    gt_ref[...] = feat(1792, GATE_ROWS)


def _in_proj(x2, g, w, wt):
    T = x2.shape[0]
    tm = TM_PROJ
    row = lambda i: (i, 0)
    o128b = jax.ShapeDtypeStruct((T, LANE), BF16)
    o128f = jax.ShapeDtypeStruct((T, LANE), F32)
    o4 = jax.ShapeDtypeStruct((4, T, LANE), BF16)
    ot = jax.ShapeDtypeStruct((LANE, T), BF16)
    o4t = jax.ShapeDtypeStruct((4, LANE, T), BF16)
    s128 = pl.BlockSpec((tm, LANE), row)
    s4 = pl.BlockSpec((4, tm, LANE), lambda i: (0, i, 0))
    st = pl.BlockSpec((LANE, tm), lambda i: (0, i))
    s4t = pl.BlockSpec((4, LANE, tm), lambda i: (0, 0, i))
    return pl.pallas_call(
        _inproj_kernel,
        grid=(T // tm,),
        in_specs=[pl.BlockSpec((tm, D_MODEL), row),
                  pl.BlockSpec((1, D_MODEL), lambda i: (0, 0)),
                  pl.BlockSpec(w.shape, lambda i: (0, 0)),
                  pl.BlockSpec(wt.shape, lambda i: (0, 0))],
        out_specs=[s4t, s128, s128, s128, s128, s4t, s4, st, st, s4t, pl.BlockSpec((GATE_ROWS, tm), lambda i: (0, i))],
        out_shape=[o4t, o128f, o128f, o128b, o128b, o4t, o4, ot, ot, o4t, jax.ShapeDtypeStruct((GATE_ROWS, T), F32)],
        compiler_params=_cparams(("parallel",)),
        name="in_proj",
    )(x2, g, w, wt)


def _gelu_tanh(x):
    return 0.5 * x * (1.0 + jnp.tanh(math.sqrt(2.0 / math.pi) * (x + 0.044715 * (x * x * x))))


def _compress_kernel(kc_ref, vc_ref, posk_ref, posv_ref, w1k_ref, w1v_ref, w2k_ref, w2v_ref, ko_ref, vo_ref):
    nrow = kc_ref.shape[0] // CMP_STRIDE
    rid = lax.broadcasted_iota(jnp.int32, (nrow, 1), 0)
    cid = lax.broadcasted_iota(jnp.int32, (1, nrow), 1)
    for src, pos, w1, w2, out, transposed in ((kc_ref, posk_ref, w1k_ref, w2k_ref, ko_ref, False),
                                              (vc_ref, posv_ref, w1v_ref, w2v_ref, vo_ref, True)):
        hid_a = jnp.zeros((nrow, 2 * CMP_HIDDEN), F32)
        hid_b = jnp.zeros((nrow, 2 * CMP_HIDDEN), F32)
        for m in range(CMP_STRIDE):
            y = src[pl.ds(m, nrow, stride=CMP_STRIDE), :]
            hid_a = hid_a + _dot((y + pos[m:m + 1, :]).astype(BF16), w1[m])
            hid_b = hid_b + _dot((y + pos[CMP_STRIDE + m:CMP_STRIDE + m + 1, :]).astype(BF16), w1[CMP_STRIDE + m])
        hid = hid_a + pltpu.roll(hid_b, nrow - 1, 0)
        o = _dot(_gelu_tanh(hid).astype(BF16), w2[...])
        if transposed:
            out[0] = jnp.where(cid < nrow - 1, o.T, 0.0).astype(BF16)
        else:
            out[0] = jnp.where(rid < nrow - 1, o, 0.0).astype(BF16)


def _compress(kc, vc, posk, posv, w1k, w1v, w2k, w2v, B, S):
    nrow = S // CMP_STRIDE
    assert nrow == LANE
    full = lambda a: pl.BlockSpec(a.shape, lambda b: (0,) * a.ndim)
    src = pl.BlockSpec((S, LANE), lambda b: (b, 0))
    osp = pl.BlockSpec((1, nrow, LANE), lambda b: (b, 0, 0))
    osh = jax.ShapeDtypeStruct((B, nrow, LANE), BF16)
    return pl.pallas_call(
        _compress_kernel,
        grid=(B,),
        in_specs=[src, src, full(posk), full(posv), full(w1k), full(w1v), full(w2k), full(w2v)],
        out_specs=[osp, osp],
        out_shape=[osh, osh],
        compiler_params=_cparams(("parallel",)),
        name="compress",
    )(kc, vc, posk, posv, w1k, w1v, w2k, w2v)


def _bias_from_dist(dist, tab_ref, h):
    val = jnp.full(dist.shape, tab_ref[h, 0], F32)
    for b in range(1, REL_BUCKETS):
        val = jnp.where(dist >= _THR[b], tab_ref[h, b], val)
    return val


def _bias_near_kernel(tab_ref, out_ref):
    h = pl.program_id(0)
    j = lax.broadcasted_iota(jnp.int32, (TQ, TQ), 0)
    i = lax.broadcasted_iota(jnp.int32, (TQ, TQ), 1)
    for d in range(2):
        bias = (_bias_from_dist(i - j + d * TQ, tab_ref, h) - tab_ref[h, REL_BUCKETS - 1]) * LOG2E
        out_ref[0, d] = jnp.where(i - j + d * TQ >= 0, bias, NEG)


def _bias_cmp_kernel(tab_ref, out_ref):
    h = pl.program_id(0)
    c = lax.broadcasted_iota(jnp.int32, (LANE, TQ), 0)
    t = pl.program_id(1) * TQ + lax.broadcasted_iota(jnp.int32, (LANE, TQ), 1)
    out_ref[0] = _bias_from_dist(t - (c * CMP_STRIDE + CMP_LEN - 1), tab_ref, h) * LOG2E


def _bias_tiles(tab_t, S):
    smem = pl.BlockSpec(memory_space=pltpu.SMEM)
    near = pl.pallas_call(
        _bias_near_kernel,
        grid=(N_REL_HEADS,),
        in_specs=[smem],
        out_specs=pl.BlockSpec((1, 2, TQ, TQ), lambda h: (h, 0, 0, 0)),
        out_shape=jax.ShapeDtypeStruct((N_REL_HEADS, 2, TQ, TQ), F32),
        compiler_params=_cparams(("parallel",)),
        name="bias_near",
    )(tab_t)
    cmp_bias = pl.pallas_call(
        _bias_cmp_kernel,
        grid=(NSA_HEADS, S // TQ),
        in_specs=[smem],
        out_specs=pl.BlockSpec((1, LANE, TQ), lambda h, q: (h, 0, q)),
        out_shape=jax.ShapeDtypeStruct((NSA_HEADS, LANE, S), F32),
        compiler_params=_cparams(("parallel", "parallel")),
        name="bias_cmp",
    )(tab_t)
    return near, cmp_bias


ACC_ROWS = NSA_HD + 16


def _with_ones(vt):
    return jnp.concatenate([vt, jnp.ones((ACC_ROWS - vt.shape[0], vt.shape[1]), BF16)], axis=0)


def _probs(s, m):
    return jnp.exp2((s - m).astype(BF16))


def _flash_first(ss, vts):
    ms = [jnp.max(s, axis=0, keepdims=True) for s in ss]
    ps = [_probs(s, m) for s, m in zip(ss, ms)]
    return tuple((m, _dot(vt, p)) for m, p, vt in zip(ms, ps, vts))


def _flash_update(ss, vts, sts):
    ms = [jnp.maximum(st[0], jnp.max(s, axis=0, keepdims=True)) for s, st in zip(ss, sts)]
    alphas = [jnp.exp2(st[0] - m) for m, st in zip(ms, sts)]
    ps = [_probs(s, m) for s, m in zip(ss, ms)]
    return tuple((m, a * st[1] + _dot(vt, p)) for m, a, st, p, vt in zip(ms, alphas, sts, ps, vts))


def _flash_out(st):
    acc = st[1]
    return acc[0:NSA_HD, :] * (1.0 / acc[NSA_HD:NSA_HD + 1, :])


def _stream_scratch(n):
    return ([pltpu.VMEM((8, TQ), F32), pltpu.VMEM((ACC_ROWS, TQ), F32)] * n
            + [pltpu.VMEM((TQ, TQ), F32)] * (2 * n) + [pltpu.VMEM((TQ, TQ), BF16)] * (2 * n))


def _stream_refs(refs, n):
    st_refs = tuple((refs[2 * c], refs[2 * c + 1]) for c in range(n))
    s, p = refs[2 * n:4 * n], refs[4 * n:6 * n]
    return st_refs, tuple(s[:n]), tuple(s[n:]), tuple(p[:n]), tuple(p[n:])


def _causal_stream(qi, scores, vtile, st_refs, s_a, s_b, p_a, p_b, diag_ready=False):
    n_far = jnp.maximum(qi - 1, 0)
    top = qi - 2

    chains = range(len(st_refs))

    def write(s_refs, kt, near=2):
        for c in chains:
            s_refs[c][...] = scores(jnp.maximum(kt, 0), near, c)

    def pending(p_refs, kt):
        return [_dot(vtile(jnp.maximum(kt, 0), c), p_refs[c][...]) for c in chains]

    def step(s_cur, p_cur, s_next, kt_next, p_prev, kt_prev, anchor=None):
        rescale = []
        for c in chains:
            nxt = scores(jnp.maximum(kt_next, 0), 2, c)
            s_next[c][...] = nxt
            pv = _dot(vtile(jnp.maximum(kt_prev, 0), c), p_prev[c][...])
            st = st_refs[c][0]
            s = s_cur[c][...]
            m_old = st[0:1, :]
            m = jnp.maximum(m_old, jnp.max(s, axis=0, keepdims=True))
            if anchor is not None:
                m = m + anchor
            anchor = nxt[0:1, :] * 0.0
            p_cur[c][...] = _probs(s, m)
            st[0:1, :] = m
            rescale.append((jnp.exp2(m_old - m), pv))
        for c in chains:
            acc = st_refs[c][1]
            acc[...] = rescale[c][0] * (acc[...] + rescale[c][1])
        return anchor

    if not diag_ready:
        write(s_a, qi, 0)
    for c in chains:
        s_b[c][...] = scores(jnp.maximum(qi - 1, 0), 1, c)
        st, acc = st_refs[c]
        s = s_a[c][...]
        m = jnp.max(s, axis=0, keepdims=True)
        p_a[c][...] = _probs(s, m)
        st[0:1, :] = m
        acc[...] = jnp.zeros(acc.shape, F32)
    step(s_b, p_b, s_a, top, p_a, qi)

    def pair(j, c):
        kt = top - 2 * j
        anchor = step(s_a, p_a, s_b, kt - 1, p_b, kt + 1)
        step(s_b, p_b, s_a, kt - 2, p_a, kt, anchor)
        return c
    lax.fori_loop(0, n_far // 2, pair, 0)

    @pl.when(n_far % 2 == 1)
    def _():
        for c, pv in zip(chains, pending(p_b, 1)):
            st, acc = st_refs[c]
            s = s_a[c][...]
            m_old = st[0:1, :]
            m = jnp.maximum(m_old, jnp.max(s, axis=0, keepdims=True))
            st[0:1, :] = m
            acc[...] = jnp.exp2(m_old - m) * (acc[...] + pv) + _dot(vtile(0, c), _probs(s, m))

    @pl.when(n_far % 2 == 0)
    def _():
        last = jnp.where(n_far == 0, qi - 1, 0)
        for c, pv in zip(chains, pending(p_b, last)):
            acc = st_refs[c][1]
            acc[...] = acc[...] + pv
    return tuple((st[0:1, :], acc[...]) for st, acc in st_refs)


def _ktile(ref, kt):
    return ref[pl.ds(pl.multiple_of(kt * TQ, TQ), TQ), :]


def _vtile(ref, kt):
    return ref[:, pl.ds(pl.multiple_of(kt * TQ, TQ), TQ)]


def _nsa_kernel(qa_ref, gt_ref, kcmp_ref, vcmpt_ref, bc_ref, ks_ref, vst_ref, kw_ref, vwt_ref, dn_ref,
                ov_ref, o_ref, psum_ref, sel_ref, oacc_ref, sig_ref, *scratch):
    st_refs, s_a, s_b, p_a, p_b = _stream_refs(scratch, NSA_BLOCKS_PER_STEP * NSA_G)
    qi = pl.program_id(1)
    t0 = qi * TQ
    sub_grp = lax.shift_right_arithmetic(lax.broadcasted_iota(jnp.int32, (LANE, 1), 0), 6)
    sig_ref[...] = jax.nn.sigmoid(gt_ref[...])

    def gate_row(c):
        return sig_ref[pl.ds(c, 1), :]

    def masked_q(r, g):
        return jnp.where(sub_grp == g, qa_ref[r].astype(F32), 0.0).astype(BF16)

    n_cmp = kcmp_ref.shape[1] - 1
    crow = lax.broadcasted_iota(jnp.int32, (LANE, 1), 0)
    cmp_end = jnp.where(crow < n_cmp, crow * CMP_STRIDE + CMP_LEN - 1, 1 << 30)
    mask_c = (t0 + lax.broadcasted_iota(jnp.int32, (1, TQ), 1)) >= cmp_end

    heads = [(r, g) for r in range(NSA_R) for g in range(NSA_G)]
    scores_c = [_dot(kcmp_ref[0], masked_q(r, g)) for r, g in heads]
    probs_c = []
    for (r, g), s in zip(heads, scores_c):
        s = jnp.where(mask_c, s + bc_ref[g * NSA_R + r], NEG)
        p = jnp.where(mask_c, jnp.exp2(s - jnp.max(s, axis=0, keepdims=True)), 0.0)
        l = jnp.sum(p, axis=0, keepdims=True)
        probs_c.append(p * jnp.where(l > 0.0, 1.0 / l, 0.0))
    for g in range(NSA_G):
        psum_ref[g] = sum(p for (r, gg), p in zip(heads, probs_c) if gg == g)
    outs_c = [_dot(vcmpt_ref[0], p.astype(BF16)) for p in probs_c]
    for r in range(NSA_R):
        o0, o1 = outs_c[r * NSA_G], outs_c[r * NSA_G + 1]
        oacc_ref[r] = jnp.where(sub_grp == 0, gate_row(r * 3) * o0, gate_row((NSA_R + r) * 3) * o1)

    n_sel = ov_ref.shape[0]
    jj = lax.broadcasted_iota(jnp.int32, (n_sel, TQ), 0)
    cur = lax.shift_right_arithmetic(t0 + lax.broadcasted_iota(jnp.int32, (n_sel, TQ), 1), 6)
    valid = jj <= cur
    forced = (jj == 0) | (cur - jj < SEL_FORCED_LOCAL)
    for g in range(NSA_G):
        ps = psum_ref[g]
        hi = ps.astype(BF16)
        rem = ps - hi.astype(F32)
        mid = rem.astype(BF16)
        lo = (rem - mid.astype(F32)).astype(BF16)
        ov = ov_ref[...]
        imp = _dot(ov, hi) + _dot(ov, mid) + _dot(ov, lo)
        score = jnp.where(valid, jnp.where(forced, 1e9, imp), -1e9)
        cnt = jnp.zeros((n_sel, TQ), F32)
        for j2 in range(n_sel):
            row = score[j2:j2 + 1, :]
            tie = jnp.where(jj > j2, 1.0, 0.0)
            cnt = cnt + jnp.where(row > score, 1.0, jnp.where(row == score, tie, 0.0))
        sel_ref[g] = jnp.where(cnt < float(min(SEL_TOPN, n_sel)), jnp.where(score > -1e8, 0.0, NEG), NEG)

    has_prev = jnp.where(qi >= 1, 0.0, NEG)
    has_wfar = jnp.where(qi >= WINDOW // TQ, 0.0, NEG)
    kt_prev = jnp.maximum(qi - 1, 0)
    kt_wfar = jnp.maximum(qi - WINDOW // TQ, 0)
    wfar_mask = lax.broadcasted_iota(jnp.int32, (TQ, TQ), 0) > lax.broadcasted_iota(jnp.int32, (TQ, TQ), 1)

    def main_body(rp, carry):
        chains = [(NSA_BLOCKS_PER_STEP * rp + i, g) for i in range(NSA_BLOCKS_PER_STEP) for g in range(NSA_G)]
        qms = [masked_q(r, g) for r, g in chains]
        hids = [g * NSA_R + r for r, g in chains]

        def block_mask(g, kt):
            per_tile = TQ // SEL_BLOCK
            rows = [jnp.broadcast_to(sel_ref[g, pl.ds(kt * per_tile + i, 1), :], (SEL_BLOCK, TQ))
                    for i in range(per_tile)]
            return jnp.concatenate(rows, axis=0)

        def near_bias(s, near, c):
            if near == 0:
                return s + dn_ref[hids[c], 0]
            if near == 1:
                return s + dn_ref[hids[c], 1] + has_prev
            return s

        def group_values(ref, kt, c):
            g = chains[c][1]
            return _with_ones(ref[g * NSA_HD:(g + 1) * NSA_HD, pl.ds(pl.multiple_of(kt * TQ, TQ), TQ)])

        def finish(sts, branch):
            for c, (r, g) in enumerate(chains):
                oacc_ref[r, g * NSA_HD:(g + 1) * NSA_HD, :] += gate_row(hids[c] * 3 + branch) * _flash_out(sts[c])

        def slc_scores(kt, near, c):
            return near_bias(_dot(_ktile(ks_ref, kt), qms[c]), near, c) + block_mask(chains[c][1], kt)

        def slc_vtile(kt, c):
            return group_values(vst_ref, kt, c)

        finish(_causal_stream(qi, slc_scores, slc_vtile, st_refs, s_a, s_b, p_a, p_b), 1)

        win_tiles = ((qi, 0), (kt_prev, 1), (kt_wfar, 2))

        def win_scores(c):
            out = []
            for kt, near in win_tiles:
                s = near_bias(_dot(_ktile(kw_ref, kt), qms[c]), near, c)
                out.append(jnp.where(wfar_mask, s + has_wfar, NEG) if near == 2 else s)
            return out

        sts = []
        ahead = win_scores(0)
        for c in range(len(chains)):
            cur, ahead = ahead, (win_scores(c + 1) if c + 1 < len(chains) else None)
            st = _flash_first([cur[0]], [group_values(vwt_ref, qi, c)])
            st = _flash_update([cur[1]], [group_values(vwt_ref, kt_prev, c)], st)
            sts.append(_flash_update([cur[2]], [group_values(vwt_ref, kt_wfar, c)], st)[0])
        finish(sts, 2)
        return carry
    lax.fori_loop(0, NSA_R // NSA_BLOCKS_PER_STEP, main_body, 0)

    for r in range(NSA_R):
        o_ref[r] = oacc_ref[r].T.astype(BF16)


def _nsa(qa, gates_t, kcmp, vcmpt, bias_c, ks, vst, kw, vwt, near_a, ov, B, S):
    nq = S // TQ
    T = B * S
    k_spec = pl.BlockSpec((S, LANE), lambda b, q: (b, 0))
    v_spec = pl.BlockSpec((LANE, S), lambda b, q: (0, b))
    full = lambda a: pl.BlockSpec(a.shape, lambda b, q: (0,) * a.ndim)
    cmp_spec = pl.BlockSpec((1, LANE, LANE), lambda b, q: (b, 0, 0))
    qo = pl.BlockSpec((4, TQ, LANE), lambda b, q: (0, b * nq + q, 0))
    return pl.pallas_call(
        _nsa_kernel,
        grid=(B, nq),
        in_specs=[pl.BlockSpec((4, LANE, TQ), lambda b, q: (0, 0, b * nq + q)),
                  pl.BlockSpec((GATE_ROWS, TQ), lambda b, q: (0, b * nq + q)),
                  cmp_spec, cmp_spec,
                  pl.BlockSpec((NSA_HEADS, LANE, TQ), lambda b, q: (0, 0, q)),
                  k_spec, v_spec, k_spec, v_spec, full(near_a), full(ov)],
        out_specs=qo,
        out_shape=jax.ShapeDtypeStruct((4, T, LANE), BF16),
        scratch_shapes=[pltpu.VMEM((NSA_G, LANE, TQ), F32), pltpu.VMEM((NSA_G, S // SEL_BLOCK, TQ), F32),
                        pltpu.VMEM((NSA_R, LANE, TQ), F32), pltpu.VMEM((GATE_ROWS, TQ), F32)]
                       + _stream_scratch(NSA_BLOCKS_PER_STEP * NSA_G),
        compiler_params=_cparams(("parallel", "arbitrary")),
        name="nsa",
    )(qa, gates_t, kcmp, vcmpt, bias_c, ks, vst, kw, vwt, near_a, ov)


def _diff_kernel(lq1_ref, lk1_ref, lq2_ref, lk2_ref, sub_ref, qb_ref, kb_ref, vbt_ref, dn_ref, o_ref, *scratch):
    n = DIFF_PAIRS_PER_STEP * 4
    st_refs, s_a, s_b, p_a, p_b = _stream_refs(scratch, n)
    qi = pl.program_id(1)
    chain_of_row = lax.shift_right_arithmetic(lax.broadcasted_iota(jnp.int32, (LANE, 1), 0), 5)
    lam = (jnp.exp(jnp.sum(lq1_ref[...] * lk1_ref[...], axis=-1, keepdims=True))
           - jnp.exp(jnp.sum(lq2_ref[...] * lk2_ref[...], axis=-1, keepdims=True)) + LAMBDA_INIT)
    has_prev = jnp.where(qi >= 1, 0.0, NEG)
    n_steps = DIFF_HEADS // 2 // DIFF_PAIRS_PER_STEP

    def score_fn(step):
        pairs = [step * DIFF_PAIRS_PER_STEP + c // 4 for c in range(n)]
        qs = [qb_ref[step * DIFF_PAIRS_PER_STEP + i].astype(F32) for i in range(DIFF_PAIRS_PER_STEP)]
        qms = [jnp.where(chain_of_row == c % 4, qs[c // 4], 0.0).astype(BF16) for c in range(n)]

        def scores(kt, near, c):
            s = _dot(kb_ref[pairs[c], pl.ds(pl.multiple_of(kt * TQ, TQ), TQ), :], qms[c])
            if near == 2:
                return s
            bias = dn_ref[2 * pairs[c] + (c % 4) // 2, near]
            return s + bias if near == 0 else s + bias + has_prev
        return scores

    def start_diag(step):
        scores = score_fn(step)
        for c in range(n):
            s_a[c][...] = scores(qi, 0, c)

    start_diag(0)

    def step_body(step, carry):
        def vtile(kt, c):
            hh = (c % 4) // 2
            return _with_ones(vbt_ref[step * DIFF_PAIRS_PER_STEP + c // 4, hh * 2 * DIFF_HD:(hh + 1) * 2 * DIFF_HD,
                                      pl.ds(pl.multiple_of(kt * TQ, TQ), TQ)])

        sts = _causal_stream(qi, score_fn(step), vtile, st_refs, s_a, s_b, p_a, p_b, diag_ready=True)
        start_diag(jnp.minimum(step + 1, n_steps - 1))
        for i in range(DIFF_PAIRS_PER_STEP):
            outs = []
            for hh in range(2):
                c = 4 * i + 2 * hh
                o = _flash_out(sts[c]) - lam * _flash_out(sts[c + 1])
                outs.append(o * lax.rsqrt(jnp.mean(o * o, axis=0, keepdims=True) + EPS))
            out = jnp.concatenate(outs, axis=0) * sub_ref[...] * (1.0 - LAMBDA_INIT)
            o_ref[step * DIFF_PAIRS_PER_STEP + i] = out.T.astype(BF16)
        return carry
    lax.fori_loop(0, n_steps, step_body, 0)


def _diff(lq1, lk1, lq2, lk2, sub, qb, kb, vbt, near_b, B, S):
    nq = S // TQ
    T = B * S
    full = lambda a: pl.BlockSpec(a.shape, lambda b, q: (0,) * a.ndim)
    k_spec = pl.BlockSpec((4, S, LANE), lambda b, q: (0, b, 0))
    v_spec = pl.BlockSpec((4, LANE, S), lambda b, q: (0, 0, b))
    qo = pl.BlockSpec((4, TQ, LANE), lambda b, q: (0, b * nq + q, 0))
    return pl.pallas_call(
        _diff_kernel,
        grid=(B, nq),
        in_specs=[full(lq1), full(lk1), full(lq2), full(lk2), full(sub),
                  pl.BlockSpec((4, LANE, TQ), lambda b, q: (0, 0, b * nq + q)), k_spec, v_spec, full(near_b)],
        out_specs=qo,
        out_shape=jax.ShapeDtypeStruct((4, T, LANE), BF16),
        scratch_shapes=_stream_scratch(DIFF_PAIRS_PER_STEP * 4),
        compiler_params=_cparams(("parallel", "arbitrary")),
        name="diff",
    )(lq1, lk1, lq2, lk2, sub, qb, kb, vbt, near_b)


def _outproj_kernel(x_ref, oa_ref, ob_ref, w_ref, g_ref, wr_ref, br_ref, h_ref, xt_ref, route_ref, cnt_ref,
                    carry_ref):
    @pl.when(pl.program_id(0) == 0)
    def _():
        carry_ref[...] = jnp.zeros(carry_ref.shape, F32)

    o = jnp.concatenate([oa_ref[r] for r in range(4)] + [ob_ref[r] for r in range(4)], axis=-1)
    h = x_ref[...] + _dot(o, w_ref[...])
    h_ref[...] = h
    tn32 = h * lax.rsqrt(jnp.mean(h * h, axis=-1, keepdims=True) + EPS) * g_ref[...]
    xt_ref[:, 0:D_MODEL] = tn32
    tn = tn32.astype(BF16)
    logits = _dot(tn, wr_ref[...]) + br_ref[...]
    lane = lax.broadcasted_iota(jnp.int32, (1, LANE), 1)
    lane_f = lane.astype(F32)
    is_grp = lane < MOE_GROUPS
    lg = jnp.where(is_grp, logits, NEG)
    mg = jnp.max(lg, axis=-1, keepdims=True)
    zg = jnp.sum(jnp.where(is_grp, jnp.exp(lg - mg), 0.0), axis=-1, keepdims=True)
    g_prob = 1.0 / zg
    g_idx = jnp.min(jnp.where(lg == mg, lane_f, 1e9), axis=-1, keepdims=True)
    lane_grp = jnp.where((lane >= MOE_GROUPS) & (lane < MOE_GROUPS + N_EXPERTS),
                         lax.shift_right_arithmetic(lane - MOE_GROUPS, 3), -1).astype(F32)
    le = jnp.where(lane_grp == g_idx, logits, NEG)
    m1 = jnp.max(le, axis=-1, keepdims=True)
    e1 = jnp.min(jnp.where(le == m1, lane_f, 1e9), axis=-1, keepdims=True)
    le2 = jnp.where(lane_f == e1, NEG, le)
    m2 = jnp.max(le2, axis=-1, keepdims=True)
    e2 = jnp.min(jnp.where(le2 == m2, lane_f, 1e9), axis=-1, keepdims=True)
    ratio = jnp.exp(m2 - m1)
    w1 = g_prob / (1.0 + ratio)
    w2 = w1 * ratio
    xt_ref[:, D_MODEL:D_MODEL + LANE] = jnp.where(lane_f == e1, w1, 0.0) + jnp.where(lane_f == e2, w2, 0.0)

    tm = h.shape[0]
    onehot = jnp.where(lane_f == g_idx, 1.0, 0.0)
    earlier = jnp.where(lax.broadcasted_iota(jnp.int32, (tm, tm), 0) > lax.broadcasted_iota(jnp.int32, (tm, tm), 1),
                        1.0, 0.0).astype(BF16)
    prefix = _dot(earlier, onehot.astype(BF16)) + carry_ref[...]
    rank = jnp.sum(onehot * prefix, axis=-1, keepdims=True)
    carry_ref[...] += jnp.sum(onehot, axis=0, keepdims=True)
    route_ref[...] = jnp.where(lane == 0, g_idx, jnp.where(lane == 1, rank, 0.0))
    cnt_ref[...] = jnp.broadcast_to(carry_ref[...], cnt_ref.shape)


def _out_proj(x2, oa, ob, w, g, wr, br):
    T = x2.shape[0]
    tm = TM_PROJ
    row = lambda i: (i, 0)
    full = lambda a: pl.BlockSpec(a.shape, lambda i: (0,) * a.ndim)
    o4 = pl.BlockSpec((4, tm, LANE), lambda i: (0, i, 0))
    return pl.pallas_call(
        _outproj_kernel,
        grid=(T // tm,),
        in_specs=[pl.BlockSpec((tm, D_MODEL), row), o4, o4, full(w), full(g), full(wr), full(br)],
        out_specs=[pl.BlockSpec((tm, D_MODEL), row), pl.BlockSpec((tm, XT_WIDTH), row), pl.BlockSpec((tm, LANE), row),
                   pl.BlockSpec((8, LANE), lambda i: (0, 0))],
        out_shape=[jax.ShapeDtypeStruct((T, D_MODEL), F32), jax.ShapeDtypeStruct((T, XT_WIDTH), F32),
                   jax.ShapeDtypeStruct((T, LANE), F32), jax.ShapeDtypeStruct((8, LANE), F32)],
        scratch_shapes=[pltpu.VMEM((1, LANE), F32)],
        compiler_params=_cparams(("arbitrary",)),
        name="out_proj",
    )(x2, oa, ob, w, g, wr, br)


def _moe_kernel(tg_ref, ok_ref, idx_ref, idxn_ref, xt_hbm, wg_ref, wu_ref, wd_ref, y_ref, xbuf, sem):
    i = pl.program_id(0)
    slot = i % 2
    tm = xbuf.shape[1]

    def row_copy(index_ref, r, dst_slot):
        return pltpu.make_async_copy(xt_hbm.at[pl.ds(index_ref[0, 0, r], 1), :],
                                     xbuf.at[dst_slot, pl.ds(r, 1), :], sem.at[dst_slot])

    @pl.when(i == 0)
    def _():
        def body(r, c):
            row_copy(idx_ref, r, 0).start()
            return c
        lax.fori_loop(0, tm, body, 0)

    @pl.when((i == 0) | (ok_ref[jnp.maximum(i - 1, 0)] == 1))
    def _():
        pltpu.make_async_copy(xt_hbm.at[pl.ds(0, tm), :], xbuf.at[slot], sem.at[slot]).wait()

    @pl.when(ok_ref[i] == 0)
    def _():
        y_ref[...] = jnp.zeros(y_ref.shape, F32)

    @pl.when(ok_ref[i] == 1)
    def _():
        lane = lax.broadcasted_iota(jnp.int32, (1, LANE), 1)
        x = xbuf[slot, :, 0:D_MODEL].astype(BF16)
        cmb = xbuf[slot, :, D_MODEL:D_MODEL + LANE]
        first_lane = MOE_GROUPS + tg_ref[i] * EPG
        per_expert = tm // EPG
        y = jnp.zeros((tm, D_MODEL), F32)
        for e in range(EPG):
            a = _dot(x, wg_ref[0, e])
            b = _dot(x, wu_ref[0, e])
            ce = jnp.sum(jnp.where(lane == first_lane + e, cmb, 0.0), axis=-1, keepdims=True)
            y = y + _dot(((a * jax.nn.sigmoid(a)) * b * ce).astype(BF16), wd_ref[0, e])
            for r in range(e * per_expert, (e + 1) * per_expert):
                row_copy(idxn_ref, r, 1 - slot).start(priority=r % 2)
        for blk in range(ROW_SLABS):
            y_ref[:, blk, :] = y[:, blk * LANE:(blk + 1) * LANE]


def _moe(tile_group, tile_ok, src3, xt, wg, wu, wd):
    n_tiles, _, tm = src3.shape
    last = n_tiles - 1
    grid_spec = pltpu.PrefetchScalarGridSpec(
        num_scalar_prefetch=2,
        grid=(n_tiles,),
        in_specs=[pl.BlockSpec((1, 1, tm), lambda i, tg, ok: (i, 0, 0), memory_space=pltpu.SMEM),
                  pl.BlockSpec((1, 1, tm), lambda i, tg, ok: (jnp.minimum(i + 1, last), 0, 0), memory_space=pltpu.SMEM),
                  pl.BlockSpec(memory_space=pl.ANY),
                  pl.BlockSpec((1, EPG, D_MODEL, EXPERT_FF), lambda i, tg, ok: (tg[i], 0, 0, 0)),
                  pl.BlockSpec((1, EPG, D_MODEL, EXPERT_FF), lambda i, tg, ok: (tg[i], 0, 0, 0)),
                  pl.BlockSpec((1, EPG, EXPERT_FF, D_MODEL), lambda i, tg, ok: (tg[i], 0, 0, 0))],
        out_specs=pl.BlockSpec((tm, ROW_SLABS, LANE), lambda i, tg, ok: (i, 0, 0)),
        scratch_shapes=[pltpu.VMEM((2, tm, XT_WIDTH), F32), pltpu.SemaphoreType.DMA((2,))],
    )
    return pl.pallas_call(
        _moe_kernel,
        grid_spec=grid_spec,
        out_shape=jax.ShapeDtypeStruct((n_tiles * tm, ROW_SLABS, LANE), F32),
        compiler_params=pltpu.CompilerParams(dimension_semantics=("arbitrary",), vmem_limit_bytes=VMEM_LIMIT_MOE),
        name="moe",
    )(tile_group, tile_ok, src3, src3, xt, wg, wu, wd)


def _final_kernel(pos_ref, posn_ref, y_hbm, h_ref, gf_ref, o_ref, ybuf, sem):
    i = pl.program_id(0)
    slot = i % 2
    tm = h_ref.shape[0]

    def issue(index_ref, dst_slot):
        group = 8

        def body(j, c):
            for k in range(group):
                r = j * group + k
                pltpu.make_async_copy(y_hbm.at[index_ref[0, 0, r]], ybuf.at[dst_slot, pl.ds(r * ROW_SLABS, ROW_SLABS), :],
                                      sem.at[dst_slot]).start(priority=k % 2)
            return c
        lax.fori_loop(0, tm // group, body, 0)

    @pl.when(i == 0)
    def _():
        issue(pos_ref, 0)

    @pl.when(i + 1 < pl.num_programs(0))
    def _():
        issue(posn_ref, 1 - slot)

    pltpu.make_async_copy(ybuf.at[slot], ybuf.at[slot], sem.at[slot]).wait()
    y = jnp.concatenate([ybuf[slot, pl.ds(blk, tm, stride=ROW_SLABS), :] for blk in range(ROW_SLABS)], axis=-1)
    h = h_ref[...] + y
    o_ref[...] = h * lax.rsqrt(jnp.mean(h * h, axis=-1, keepdims=True) + EPS) * gf_ref[...]


def _final(pos3, y_sorted, h, gf):
    n_tiles, _, tm = pos3.shape
    last = n_tiles - 1
    row = lambda i: (i, 0)
    return pl.pallas_call(
        _final_kernel,
        grid=(n_tiles,),
        in_specs=[pl.BlockSpec((1, 1, tm), lambda i: (i, 0, 0), memory_space=pltpu.SMEM),
                  pl.BlockSpec((1, 1, tm), lambda i: (jnp.minimum(i + 1, last), 0, 0), memory_space=pltpu.SMEM),
                  pl.BlockSpec(memory_space=pl.ANY),
                  pl.BlockSpec((tm, D_MODEL), row),
                  pl.BlockSpec((1, D_MODEL), lambda i: (0, 0))],
        out_specs=pl.BlockSpec((tm, D_MODEL), row),
        out_shape=jax.ShapeDtypeStruct((n_tiles * tm, D_MODEL), F32),
        scratch_shapes=[pltpu.VMEM((2, tm * ROW_SLABS, LANE), F32), pltpu.SemaphoreType.DMA((2,))],
        compiler_params=_cparams(("arbitrary",)),
        name="final",
    )(pos3, pos3, y_sorted, h, gf)


def _qa_perm():
    new = np.arange(NSA_HEADS * NSA_HD)
    r, g, d = new // LANE, (new % LANE) // NSA_HD, new % NSA_HD
    return (g * NSA_R + r) * NSA_HD + d


def _block_diag2(w):
    z = jnp.zeros_like(w)
    return jnp.concatenate([jnp.concatenate([w, z], axis=-1), jnp.concatenate([z, w], axis=-1)], axis=-2)


def kernel(x, rel_bias, ln_mix, w_in, cmp_pos_k, cmp_pos_v, cmp_k_w1, cmp_k_w2, cmp_v_w1, cmp_v_w2,
           diff_lq1, diff_lk1, diff_lq2, diff_lk2, diff_subln, w_out, ln_ffn,
           router_group_w, router_group_b, router_expert_w, router_expert_b,
           exp_w_gate, exp_w_up, exp_w_down, ln_final):
    B, S, D = x.shape
    T = B * S
    assert D == D_MODEL and S % TQ == 0 and S >= WINDOW and T % TM_MOE == 0 and T % TM_FINAL == 0
    x2 = x.reshape(T, D)
    perm = _qa_perm()

    w = w_in[0]
    c_kc, c_vc, c_ks, c_vs, c_kw, c_vw, c_gt = 512, 640, 768, 896, 1024, 1152, 1280
    c_qb = c_gt + N_GATE
    c_kb, c_vb = c_qb + 512, c_qb + 1024
    col = lambda c, n=LANE: w[:, c:c + n]
    w_tok = jnp.concatenate([col(c_kc), col(c_vc), col(c_ks), col(c_kw), col(c_kb, 512)], axis=1).astype(BF16)
    w_feat = jnp.concatenate([w[:, perm], col(c_qb, 512), col(c_vs), col(c_vw), col(c_vb, 512), col(c_gt, N_GATE),
                              jnp.zeros((D, GATE_ROWS - N_GATE), F32)], axis=1).T.astype(BF16)
    qa, kc, vc, ks, kw, qb, kb, vst, vwt, vbt, gates_t = _in_proj(x2, ln_mix[0][None, :], w_tok, w_feat)

    w1k = _block_diag2(cmp_k_w1[0].reshape(CMP_LEN, NSA_HD, CMP_HIDDEN)).astype(BF16)
    w1v = _block_diag2(cmp_v_w1[0].reshape(CMP_LEN, NSA_HD, CMP_HIDDEN)).astype(BF16)
    w2k = _block_diag2(cmp_k_w2[0]).astype(BF16)
    w2v = _block_diag2(cmp_v_w2[0]).astype(BF16)
    posk = jnp.tile(cmp_pos_k[0], (1, NSA_G))
    posv = jnp.tile(cmp_pos_v[0], (1, NSA_G))
    kcmp, vcmpt = _compress(kc, vc, posk, posv, w1k, w1v, w2k, w2v, B, S)

    near, bias_c = _bias_tiles(rel_bias.T, S)

    n_sel = S // SEL_BLOCK
    nrow = S // CMP_STRIDE
    c_start = np.arange(nrow) * CMP_STRIDE
    s_start = np.arange(n_sel) * SEL_BLOCK
    ov = ((c_start[None, :] <= s_start[:, None] + SEL_BLOCK - 1)
          & (c_start[None, :] + CMP_LEN - 1 >= s_start[:, None])
          & (np.arange(nrow)[None, :] < nrow - 1)).astype(np.float32)
    o_a = _nsa(qa, gates_t, kcmp, vcmpt, bias_c, ks, vst, kw, vwt, near[:NSA_HEADS], jnp.asarray(ov, BF16), B, S)

    sub = jnp.tile(diff_subln[0], 2)[:, None]
    o_b = _diff(diff_lq1[0][None, :], diff_lk1[0][None, :], diff_lq2[0][None, :], diff_lk2[0][None, :],
                sub, qb, kb, vbt, near[NSA_HEADS:], B, S)

    w_o = jnp.concatenate([w_out[0][:512][perm], w_out[0][512:]], axis=0).astype(BF16)
    n_r = MOE_GROUPS + N_EXPERTS
    wr = jnp.concatenate([router_group_w[0], router_expert_w[0], jnp.zeros((D, LANE - n_r), F32)], axis=1).astype(BF16)
    br = jnp.concatenate([router_group_b[0], router_expert_b[0], jnp.zeros((LANE - n_r,), F32)])[None, :]
    h1, xt, route, counts = _out_proj(x2, o_a, o_b, w_o, ln_ffn[0][None, :], wr, br)

    tm = TM_MOE
    n_tiles = T // tm + MOE_GROUPS
    cnt = counts[0, :MOE_GROUPS].astype(jnp.int32)
    ends = jnp.cumsum((cnt + tm - 1) // tm * tm)
    starts = ends - (cnt + tm - 1) // tm * tm
    pos = starts[route[:, 0].astype(jnp.int32)] + route[:, 1].astype(jnp.int32)
    src = jnp.zeros((n_tiles * tm,), jnp.int32).at[pos].set(
        jnp.arange(T, dtype=jnp.int32), unique_indices=True, mode="promise_in_bounds")
    tile_start = jnp.arange(n_tiles, dtype=jnp.int32) * tm
    tile_group = jnp.minimum(jnp.searchsorted(ends, tile_start, side="right"), MOE_GROUPS - 1).astype(jnp.int32)
    tile_ok = (tile_start < ends[-1]).astype(jnp.int32)

    by_group = lambda a: a[0].astype(BF16).reshape((MOE_GROUPS, EPG) + a.shape[2:])
    y_sorted = _moe(tile_group, tile_ok, src.reshape(n_tiles, 1, tm), xt,
                    by_group(exp_w_gate), by_group(exp_w_up), by_group(exp_w_down))
    out = _final(pos.reshape(T // TM_FINAL, 1, TM_FINAL), y_sorted, h1, ln_final[None, :])
    return out.reshape(B, S, D)
```

```python
import math

import numpy as np
import jax
import jax.numpy as jnp
from jax import lax
from jax.experimental import pallas as pl
from jax.experimental.pallas import tpu as pltpu

F32 = jnp.float32
BF16 = jnp.bfloat16
NEG = -1e30
EPS = 1e-6
LOG2E = math.log2(math.e)

D_MODEL = 1024
LANE = 128
NSA_HEADS, NSA_G, NSA_R, NSA_HD = 8, 2, 4, 64
CMP_LEN, CMP_STRIDE, CMP_HIDDEN = 32, 16, 128
SEL_BLOCK, SEL_TOPN, SEL_FORCED_LOCAL, WINDOW = 64, 8, 2, 512
DIFF_HEADS, DIFF_HD = 8, 32
REL_BUCKETS, REL_MAX_EXACT, REL_MAX_DIST = 32, 16, 128
N_REL_HEADS = NSA_HEADS + DIFF_HEADS
MOE_GROUPS, EPG, N_EXPERTS, EXPERT_FF = 4, 8, 32, 256
LAMBDA_INIT = 0.8 - 0.6 * math.exp(-0.3 * 0)
N_GATE = NSA_HEADS * 3
GATE_ROWS = 32

TQ = 256
DIFF_PAIRS_PER_STEP = 4
NSA_BLOCKS_PER_STEP = 4
TM_PROJ = 512
TM_MOE = 512
TM_FINAL = 1024
XT_WIDTH = D_MODEL + LANE
ROW_SLABS = D_MODEL // LANE
VMEM_LIMIT = 48 * 1024 * 1024
VMEM_LIMIT_MOE = 56 * 1024 * 1024


def _cparams(sem):
    return pltpu.CompilerParams(dimension_semantics=sem, vmem_limit_bytes=VMEM_LIMIT)


def _dot(a, b):
    return jnp.dot(a, b, preferred_element_type=F32)


def _dot_nt(a, b):
    return lax.dot_general(a, b, (((1,), (1,)), ((), ())), preferred_element_type=F32)


def _bucket_thresholds():
    n = np.arange(0, REL_MAX_DIST + 1)
    nf = np.maximum(n, 1).astype(np.float32)
    large = REL_MAX_EXACT + (np.log(nf / np.float32(REL_MAX_EXACT)) / np.float32(math.log(REL_MAX_DIST / REL_MAX_EXACT))
                             * np.float32(REL_BUCKETS - REL_MAX_EXACT)).astype(np.int32)
    large = np.minimum(large, REL_BUCKETS - 1)
    bucket = np.where(n < REL_MAX_EXACT, n, large)
    return [int(np.argmax(bucket >= b)) for b in range(REL_BUCKETS)]


_THR = _bucket_thresholds()


def _inproj_kernel(x_ref, g_ref, w_ref, wt_ref, qa_ref, kc_ref, vc_ref, ks_ref, kw_ref, qb_ref, kb_ref,
                   vst_ref, vwt_ref, vbt_ref, gt_ref):
    x = x_ref[...]
    xn = (x * lax.rsqrt(jnp.mean(x * x, axis=-1, keepdims=True) + EPS) * g_ref[...]).astype(BF16)
    a = _dot(xn, w_ref[:, 0:512])
    kc_ref[...] = a[:, 0:128]
    vc_ref[...] = a[:, 128:256]
    ks_ref[...] = a[:, 256:384].astype(BF16)
    kw_ref[...] = a[:, 384:512].astype(BF16)
    a = _dot(xn, w_ref[:, 512:1024])
    for r in range(4):
        kb_ref[r] = a[:, r * LANE:(r + 1) * LANE].astype(BF16)

    ft = _dot_nt(wt_ref[...], xn)

    def feat(row0, rows=LANE):
        return ft[row0:row0 + rows, :]
    for r in range(4):
        qa_ref[r] = (feat(r * LANE) * (NSA_HD ** -0.5 * LOG2E)).astype(BF16)
        qb_ref[r] = (feat(512 + r * LANE) * (DIFF_HD ** -0.5 * LOG2E)).astype(BF16)
        vbt_ref[r] = feat(1280 + r * LANE).astype(BF16)
    vst_ref[...] = feat(1024).astype(BF16)
    vwt_ref[...] = feat(1152).astype(BF16)
    gt_ref[...] = feat(1792, GATE_ROWS)


def _in_proj(x2, g, w, wt):
    T = x2.shape[0]
    tm = TM_PROJ
    row = lambda i: (i, 0)
    o128b = jax.ShapeDtypeStruct((T, LANE), BF16)
    o128f = jax.ShapeDtypeStruct((T, LANE), F32)
    o4 = jax.ShapeDtypeStruct((4, T, LANE), BF16)
    ot = jax.ShapeDtypeStruct((LANE, T), BF16)
    o4t = jax.ShapeDtypeStruct((4, LANE, T), BF16)
    s128 = pl.BlockSpec((tm, LANE), row)
    s4 = pl.BlockSpec((4, tm, LANE), lambda i: (0, i, 0))
    st = pl.BlockSpec((LANE, tm), lambda i: (0, i))
    s4t = pl.BlockSpec((4, LANE, tm), lambda i: (0, 0, i))
    return pl.pallas_call(
        _inproj_kernel,
        grid=(T // tm,),
        in_specs=[pl.BlockSpec((tm, D_MODEL), row),
                  pl.BlockSpec((1, D_MODEL), lambda i: (0, 0)),
                  pl.BlockSpec(w.shape, lambda i: (0, 0)),
                  pl.BlockSpec(wt.shape, lambda i: (0, 0))],
        out_specs=[s4t, s128, s128, s128, s128, s4t, s4, st, st, s4t, pl.BlockSpec((GATE_ROWS, tm), lambda i: (0, i))],
        out_shape=[o4t, o128f, o128f, o128b, o128b, o4t, o4, ot, ot, o4t, jax.ShapeDtypeStruct((GATE_ROWS, T), F32)],
        compiler_params=_cparams(("parallel",)),
        name="in_proj",
    )(x2, g, w, wt)


def _gelu_tanh(x):
    return 0.5 * x * (1.0 + jnp.tanh(math.sqrt(2.0 / math.pi) * (x + 0.044715 * (x * x * x))))


def _compress_kernel(kc_ref, vc_ref, posk_ref, posv_ref, w1k_ref, w1v_ref, w2k_ref, w2v_ref, ko_ref, vo_ref):
    nrow = kc_ref.shape[0] // CMP_STRIDE
    rid = lax.broadcasted_iota(jnp.int32, (nrow, 1), 0)
    cid = lax.broadcasted_iota(jnp.int32, (1, nrow), 1)
    for src, pos, w1, w2, out, transposed in ((kc_ref, posk_ref, w1k_ref, w2k_ref, ko_ref, False),
                                              (vc_ref, posv_ref, w1v_ref, w2v_ref, vo_ref, True)):
        hid_a = jnp.zeros((nrow, 2 * CMP_HIDDEN), F32)
        hid_b = jnp.zeros((nrow, 2 * CMP_HIDDEN), F32)
        for m in range(CMP_STRIDE):
            y = src[pl.ds(m, nrow, stride=CMP_STRIDE), :]
            hid_a = hid_a + _dot((y + pos[m:m + 1, :]).astype(BF16), w1[m])
            hid_b = hid_b + _dot((y + pos[CMP_STRIDE + m:CMP_STRIDE + m + 1, :]).astype(BF16), w1[CMP_STRIDE + m])
        hid = hid_a + pltpu.roll(hid_b, nrow - 1, 0)
        o = _dot(_gelu_tanh(hid).astype(BF16), w2[...])
        if transposed:
            out[0] = jnp.where(cid < nrow - 1, o.T, 0.0).astype(BF16)
        else:
            out[0] = jnp.where(rid < nrow - 1, o, 0.0).astype(BF16)


def _compress(kc, vc, posk, posv, w1k, w1v, w2k, w2v, B, S):
    nrow = S // CMP_STRIDE
    assert nrow == LANE
    full = lambda a: pl.BlockSpec(a.shape, lambda b: (0,) * a.ndim)
    src = pl.BlockSpec((S, LANE), lambda b: (b, 0))
    osp = pl.BlockSpec((1, nrow, LANE), lambda b: (b, 0, 0))
    osh = jax.ShapeDtypeStruct((B, nrow, LANE), BF16)
    return pl.pallas_call(
        _compress_kernel,
        grid=(B,),
        in_specs=[src, src, full(posk), full(posv), full(w1k), full(w1v), full(w2k), full(w2v)],
        out_specs=[osp, osp],
        out_shape=[osh, osh],
        compiler_params=_cparams(("parallel",)),
        name="compress",
    )(kc, vc, posk, posv, w1k, w1v, w2k, w2v)


def _bias_from_dist(dist, tab_ref, h):
    val = jnp.full(dist.shape, tab_ref[h, 0], F32)
    for b in range(1, REL_BUCKETS):
        val = jnp.where(dist >= _THR[b], tab_ref[h, b], val)
    return val


def _bias_near_kernel(tab_ref, out_ref):
    h = pl.program_id(0)
    j = lax.broadcasted_iota(jnp.int32, (TQ, TQ), 0)
    i = lax.broadcasted_iota(jnp.int32, (TQ, TQ), 1)
    for d in range(2):
        bias = (_bias_from_dist(i - j + d * TQ, tab_ref, h) - tab_ref[h, REL_BUCKETS - 1]) * LOG2E
        out_ref[0, d] = jnp.where(i - j + d * TQ >= 0, bias, NEG)


def _bias_cmp_kernel(tab_ref, out_ref):
    h = pl.program_id(0)
    c = lax.broadcasted_iota(jnp.int32, (LANE, TQ), 0)
    t = pl.program_id(1) * TQ + lax.broadcasted_iota(jnp.int32, (LANE, TQ), 1)
    out_ref[0] = _bias_from_dist(t - (c * CMP_STRIDE + CMP_LEN - 1), tab_ref, h) * LOG2E


def _bias_tiles(tab_t, S):
    smem = pl.BlockSpec(memory_space=pltpu.SMEM)
    near = pl.pallas_call(
        _bias_near_kernel,
        grid=(N_REL_HEADS,),
        in_specs=[smem],
        out_specs=pl.BlockSpec((1, 2, TQ, TQ), lambda h: (h, 0, 0, 0)),
        out_shape=jax.ShapeDtypeStruct((N_REL_HEADS, 2, TQ, TQ), F32),
        compiler_params=_cparams(("parallel",)),
        name="bias_near",
    )(tab_t)
    cmp_bias = pl.pallas_call(
        _bias_cmp_kernel,
        grid=(NSA_HEADS, S // TQ),
        in_specs=[smem],
        out_specs=pl.BlockSpec((1, LANE, TQ), lambda h, q: (h, 0, q)),
        out_shape=jax.ShapeDtypeStruct((NSA_HEADS, LANE, S), F32),
        compiler_params=_cparams(("parallel", "parallel")),
        name="bias_cmp",
    )(tab_t)
    return near, cmp_bias


ACC_ROWS = NSA_HD + 16


def _with_ones(vt):
    return jnp.concatenate([vt, jnp.ones((ACC_ROWS - vt.shape[0], vt.shape[1]), BF16)], axis=0)


def _probs(s, m):
    return jnp.exp2((s - m).astype(BF16))


def _flash_first(ss, vts):
    ms = [jnp.max(s, axis=0, keepdims=True) for s in ss]
    ps = [_probs(s, m) for s, m in zip(ss, ms)]
    return tuple((m, _dot(vt, p)) for m, p, vt in zip(ms, ps, vts))


def _flash_update(ss, vts, sts):
    ms = [jnp.maximum(st[0], jnp.max(s, axis=0, keepdims=True)) for s, st in zip(ss, sts)]
    alphas = [jnp.exp2(st[0] - m) for m, st in zip(ms, sts)]
    ps = [_probs(s, m) for s, m in zip(ss, ms)]
    return tuple((m, a * st[1] + _dot(vt, p)) for m, a, st, p, vt in zip(ms, alphas, sts, ps, vts))


def _flash_out(st):
    acc = st[1]
    return acc[0:NSA_HD, :] * (1.0 / acc[NSA_HD:NSA_HD + 1, :])


def _stream_scratch(n):
    return ([pltpu.VMEM((8, TQ), F32), pltpu.VMEM((ACC_ROWS, TQ), F32)] * n
            + [pltpu.VMEM((TQ, TQ), F32)] * (2 * n) + [pltpu.VMEM((TQ, TQ), BF16)] * (2 * n))


def _stream_refs(refs, n):
    st_refs = tuple((refs[2 * c], refs[2 * c + 1]) for c in range(n))
    s, p = refs[2 * n:4 * n], refs[4 * n:6 * n]
    return st_refs, tuple(s[:n]), tuple(s[n:]), tuple(p[:n]), tuple(p[n:])


def _causal_stream(qi, scores, vtile, st_refs, s_a, s_b, p_a, p_b, diag_ready=False):
    n_far = jnp.maximum(qi - 1, 0)
    top = qi - 2

    chains = range(len(st_refs))

    def write(s_refs, kt, near=2):
        for c in chains:
            s_refs[c][...] = scores(jnp.maximum(kt, 0), near, c)

    def pending(p_refs, kt):
        return [_dot(vtile(jnp.maximum(kt, 0), c), p_refs[c][...]) for c in chains]

    def step(s_cur, p_cur, s_next, kt_next, p_prev, kt_prev, anchor=None):
        rescale = []
        for c in chains:
            nxt = scores(jnp.maximum(kt_next, 0), 2, c)
            s_next[c][...] = nxt
            pv = _dot(vtile(jnp.maximum(kt_prev, 0), c), p_prev[c][...])
            st = st_refs[c][0]
            s = s_cur[c][...]
            m_old = st[0:1, :]
            m = jnp.maximum(m_old, jnp.max(s, axis=0, keepdims=True))
            if anchor is not None:
                m = m + anchor
            anchor = nxt[0:1, :] * 0.0
            p_cur[c][...] = _probs(s, m)
            st[0:1, :] = m
            rescale.append((jnp.exp2(m_old - m), pv))
        for c in chains:
            acc = st_refs[c][1]
            acc[...] = rescale[c][0] * (acc[...] + rescale[c][1])
        return anchor

    if not diag_ready:
        write(s_a, qi, 0)
    for c in chains:
        s_b[c][...] = scores(jnp.maximum(qi - 1, 0), 1, c)
        st, acc = st_refs[c]
        s = s_a[c][...]
        m = jnp.max(s, axis=0, keepdims=True)
        p_a[c][...] = _probs(s, m)
        st[0:1, :] = m
        acc[...] = jnp.zeros(acc.shape, F32)
    step(s_b, p_b, s_a, top, p_a, qi)

    def pair(j, c):
        kt = top - 2 * j
        anchor = step(s_a, p_a, s_b, kt - 1, p_b, kt + 1)
        step(s_b, p_b, s_a, kt - 2, p_a, kt, anchor)
        return c
    lax.fori_loop(0, n_far // 2, pair, 0)

    @pl.when(n_far % 2 == 1)
    def _():
        for c, pv in zip(chains, pending(p_b, 1)):
            st, acc = st_refs[c]
            s = s_a[c][...]
            m_old = st[0:1, :]
            m = jnp.maximum(m_old, jnp.max(s, axis=0, keepdims=True))
            st[0:1, :] = m
            acc[...] = jnp.exp2(m_old - m) * (acc[...] + pv) + _dot(vtile(0, c), _probs(s, m))

    @pl.when(n_far % 2 == 0)
    def _():
        last = jnp.where(n_far == 0, qi - 1, 0)
        for c, pv in zip(chains, pending(p_b, last)):
            acc = st_refs[c][1]
            acc[...] = acc[...] + pv
    return tuple((st[0:1, :], acc[...]) for st, acc in st_refs)


def _ktile(ref, kt):
    return ref[pl.ds(pl.multiple_of(kt * TQ, TQ), TQ), :]


def _vtile(ref, kt):
    return ref[:, pl.ds(pl.multiple_of(kt * TQ, TQ), TQ)]


def _nsa_kernel(qa_ref, gt_ref, kcmp_ref, vcmpt_ref, bc_ref, ks_ref, vst_ref, kw_ref, vwt_ref, dn_ref,
                ov_ref, o_ref, psum_ref, sel_ref, oacc_ref, sig_ref, *scratch):
    st_refs, s_a, s_b, p_a, p_b = _stream_refs(scratch, NSA_BLOCKS_PER_STEP * NSA_G)
    qi = pl.program_id(1)
    t0 = qi * TQ
    sub_grp = lax.shift_right_arithmetic(lax.broadcasted_iota(jnp.int32, (LANE, 1), 0), 6)
    sig_ref[...] = jax.nn.sigmoid(gt_ref[...])

    def gate_row(c):
        return sig_ref[pl.ds(c, 1), :]

    def masked_q(r, g):
        return jnp.where(sub_grp == g, qa_ref[r].astype(F32), 0.0).astype(BF16)

    n_cmp = kcmp_ref.shape[1] - 1
    crow = lax.broadcasted_iota(jnp.int32, (LANE, 1), 0)
    cmp_end = jnp.where(crow < n_cmp, crow * CMP_STRIDE + CMP_LEN - 1, 1 << 30)
    mask_c = (t0 + lax.broadcasted_iota(jnp.int32, (1, TQ), 1)) >= cmp_end

    heads = [(r, g) for r in range(NSA_R) for g in range(NSA_G)]
    scores_c = [_dot(kcmp_ref[0], masked_q(r, g)) for r, g in heads]
    probs_c = []
    for (r, g), s in zip(heads, scores_c):
        s = jnp.where(mask_c, s + bc_ref[g * NSA_R + r], NEG)
        p = jnp.where(mask_c, jnp.exp2(s - jnp.max(s, axis=0, keepdims=True)), 0.0)
        l = jnp.sum(p, axis=0, keepdims=True)
        probs_c.append(p * jnp.where(l > 0.0, 1.0 / l, 0.0))
    for g in range(NSA_G):
        psum_ref[g] = sum(p for (r, gg), p in zip(heads, probs_c) if gg == g)
    outs_c = [_dot(vcmpt_ref[0], p.astype(BF16)) for p in probs_c]
    for r in range(NSA_R):
        o0, o1 = outs_c[r * NSA_G], outs_c[r * NSA_G + 1]
        oacc_ref[r] = jnp.where(sub_grp == 0, gate_row(r * 3) * o0, gate_row((NSA_R + r) * 3) * o1)

    n_sel = ov_ref.shape[0]
    jj = lax.broadcasted_iota(jnp.int32, (n_sel, TQ), 0)
    cur = lax.shift_right_arithmetic(t0 + lax.broadcasted_iota(jnp.int32, (n_sel, TQ), 1), 6)
    valid = jj <= cur
    forced = (jj == 0) | (cur - jj < SEL_FORCED_LOCAL)
    for g in range(NSA_G):
        ps = psum_ref[g]
        hi = ps.astype(BF16)
        rem = ps - hi.astype(F32)
        mid = rem.astype(BF16)
        lo = (rem - mid.astype(F32)).astype(BF16)
        ov = ov_ref[...]
        imp = _dot(ov, hi) + _dot(ov, mid) + _dot(ov, lo)
        score = jnp.where(valid, jnp.where(forced, 1e9, imp), -1e9)
        cnt = jnp.zeros((n_sel, TQ), F32)
        for j2 in range(n_sel):
            row = score[j2:j2 + 1, :]
            tie = jnp.where(jj > j2, 1.0, 0.0)
            cnt = cnt + jnp.where(row > score, 1.0, jnp.where(row == score, tie, 0.0))
        sel_ref[g] = jnp.where(cnt < float(min(SEL_TOPN, n_sel)), jnp.where(score > -1e8, 0.0, NEG), NEG)

    has_prev = jnp.where(qi >= 1, 0.0, NEG)
    has_wfar = jnp.where(qi >= WINDOW // TQ, 0.0, NEG)
    kt_prev = jnp.maximum(qi - 1, 0)
    kt_wfar = jnp.maximum(qi - WINDOW // TQ, 0)
    wfar_mask = lax.broadcasted_iota(jnp.int32, (TQ, TQ), 0) > lax.broadcasted_iota(jnp.int32, (TQ, TQ), 1)

    def main_body(rp, carry):
        chains = [(NSA_BLOCKS_PER_STEP * rp + i, g) for i in range(NSA_BLOCKS_PER_STEP) for g in range(NSA_G)]
        qms = [masked_q(r, g) for r, g in chains]
        hids = [g * NSA_R + r for r, g in chains]

        per_tile = TQ // SEL_BLOCK
        key_blk = lax.shift_right_arithmetic(lax.broadcasted_iota(jnp.int32, (TQ, LANE), 0), 6)
        lane_id = lax.broadcasted_iota(jnp.int32, (TQ, LANE), 1)

        def keys_with_block_onehot(kt, g):
            other = (1 - g) * NSA_HD
            onehot = jnp.where(lane_id - other == key_blk, 1.0, 0.0).astype(BF16)
            in_other = (lane_id >= other) & (lane_id < other + NSA_HD)
            return jnp.where(in_other, onehot, _ktile(ks_ref, kt))

        def query_with_mask_rows(kt, c):
            r, g = chains[c]
            own = qa_ref[r, g * NSA_HD:(g + 1) * NSA_HD, :]
            pair = sel_ref[g, pl.ds(pl.multiple_of((kt // 2) * 2 * per_tile, 2 * per_tile), 2 * per_tile), :]
            rows = jnp.where(kt % 2 == 0, pair[0:per_tile], pair[per_tile:2 * per_tile]).astype(BF16)
            other = jnp.concatenate([rows, jnp.zeros((NSA_HD - per_tile, TQ), BF16)], axis=0)
            return jnp.concatenate([own, other] if g == 0 else [other, own], axis=0)

        def near_bias(s, near, c):
            if near == 0:
                return s + dn_ref[hids[c], 0]
            if near == 1:
                return s + dn_ref[hids[c], 1] + has_prev
            return s

        def group_values(ref, kt, c):
            g = chains[c][1]
            return _with_ones(ref[g * NSA_HD:(g + 1) * NSA_HD, pl.ds(pl.multiple_of(kt * TQ, TQ), TQ)])

        def finish(sts, branch):
            for c, (r, g) in enumerate(chains):
                oacc_ref[r, g * NSA_HD:(g + 1) * NSA_HD, :] += gate_row(hids[c] * 3 + branch) * _flash_out(sts[c])

        def slc_scores(kt, near, c):
            return near_bias(_dot(keys_with_block_onehot(kt, chains[c][1]), query_with_mask_rows(kt, c)), near, c)

        def slc_vtile(kt, c):
            return group_values(vst_ref, kt, c)

        finish(_causal_stream(qi, slc_scores, slc_vtile, st_refs, s_a, s_b, p_a, p_b), 1)

        win_tiles = ((qi, 0), (kt_prev, 1), (kt_wfar, 2))

        def win_scores(c):
            out = []
            for kt, near in win_tiles:
                s = near_bias(_dot(_ktile(kw_ref, kt), qms[c]), near, c)
                out.append(jnp.where(wfar_mask, s + has_wfar, NEG) if near == 2 else s)
            return out

        sts = []
        ahead = win_scores(0)
        for c in range(len(chains)):
            cur, ahead = ahead, (win_scores(c + 1) if c + 1 < len(chains) else None)
            st = _flash_first([cur[0]], [group_values(vwt_ref, qi, c)])
            st = _flash_update([cur[1]], [group_values(vwt_ref, kt_prev, c)], st)
            sts.append(_flash_update([cur[2]], [group_values(vwt_ref, kt_wfar, c)], st)[0])
        finish(sts, 2)
        return carry
    lax.fori_loop(0, NSA_R // NSA_BLOCKS_PER_STEP, main_body, 0)

    for r in range(NSA_R):
        o_ref[r] = oacc_ref[r].T.astype(BF16)


def _nsa(qa, gates_t, kcmp, vcmpt, bias_c, ks, vst, kw, vwt, near_a, ov, B, S):
    nq = S // TQ
    T = B * S
    k_spec = pl.BlockSpec((S, LANE), lambda b, q: (b, 0))
    v_spec = pl.BlockSpec((LANE, S), lambda b, q: (0, b))
    full = lambda a: pl.BlockSpec(a.shape, lambda b, q: (0,) * a.ndim)
    cmp_spec = pl.BlockSpec((1, LANE, LANE), lambda b, q: (b, 0, 0))
    qo = pl.BlockSpec((4, TQ, LANE), lambda b, q: (0, b * nq + q, 0))
    return pl.pallas_call(
        _nsa_kernel,
        grid=(B, nq),
        in_specs=[pl.BlockSpec((4, LANE, TQ), lambda b, q: (0, 0, b * nq + q)),
                  pl.BlockSpec((GATE_ROWS, TQ), lambda b, q: (0, b * nq + q)),
                  cmp_spec, cmp_spec,
                  pl.BlockSpec((NSA_HEADS, LANE, TQ), lambda b, q: (0, 0, q)),
                  k_spec, v_spec, k_spec, v_spec, full(near_a), full(ov)],
        out_specs=qo,
        out_shape=jax.ShapeDtypeStruct((4, T, LANE), BF16),
        scratch_shapes=[pltpu.VMEM((NSA_G, LANE, TQ), F32), pltpu.VMEM((NSA_G, S // SEL_BLOCK, TQ), F32),
                        pltpu.VMEM((NSA_R, LANE, TQ), F32), pltpu.VMEM((GATE_ROWS, TQ), F32)]
                       + _stream_scratch(NSA_BLOCKS_PER_STEP * NSA_G),
        compiler_params=_cparams(("parallel", "arbitrary")),
        name="nsa",
    )(qa, gates_t, kcmp, vcmpt, bias_c, ks, vst, kw, vwt, near_a, ov)


def _diff_kernel(lq1_ref, lk1_ref, lq2_ref, lk2_ref, sub_ref, qb_ref, kb_ref, vbt_ref, dn_ref, o_ref, *scratch):
    n = DIFF_PAIRS_PER_STEP * 4
    st_refs, s_a, s_b, p_a, p_b = _stream_refs(scratch, n)
    qi = pl.program_id(1)
    chain_of_row = lax.shift_right_arithmetic(lax.broadcasted_iota(jnp.int32, (LANE, 1), 0), 5)
    lam = (jnp.exp(jnp.sum(lq1_ref[...] * lk1_ref[...], axis=-1, keepdims=True))
           - jnp.exp(jnp.sum(lq2_ref[...] * lk2_ref[...], axis=-1, keepdims=True)) + LAMBDA_INIT)
    has_prev = jnp.where(qi >= 1, 0.0, NEG)
    n_steps = DIFF_HEADS // 2 // DIFF_PAIRS_PER_STEP

    def score_fn(step):
        pairs = [step * DIFF_PAIRS_PER_STEP + c // 4 for c in range(n)]
        qs = [qb_ref[step * DIFF_PAIRS_PER_STEP + i].astype(F32) for i in range(DIFF_PAIRS_PER_STEP)]
        qms = [jnp.where(chain_of_row == c % 4, qs[c // 4], 0.0).astype(BF16) for c in range(n)]

        def scores(kt, near, c):
            s = _dot(kb_ref[pairs[c], pl.ds(pl.multiple_of(kt * TQ, TQ), TQ), :], qms[c])
            if near == 2:
                return s
            bias = dn_ref[2 * pairs[c] + (c % 4) // 2, near]
            return s + bias if near == 0 else s + bias + has_prev
        return scores

    def start_diag(step):
        scores = score_fn(step)
        for c in range(n):
            s_a[c][...] = scores(qi, 0, c)

    start_diag(0)

    def step_body(step, carry):
        def vtile(kt, c):
            hh = (c % 4) // 2
            return _with_ones(vbt_ref[step * DIFF_PAIRS_PER_STEP + c // 4, hh * 2 * DIFF_HD:(hh + 1) * 2 * DIFF_HD,
                                      pl.ds(pl.multiple_of(kt * TQ, TQ), TQ)])

        sts = _causal_stream(qi, score_fn(step), vtile, st_refs, s_a, s_b, p_a, p_b, diag_ready=True)
        start_diag(jnp.minimum(step + 1, n_steps - 1))
        for i in range(DIFF_PAIRS_PER_STEP):
            outs = []
            for hh in range(2):
                c = 4 * i + 2 * hh
                o = _flash_out(sts[c]) - lam * _flash_out(sts[c + 1])
                outs.append(o * lax.rsqrt(jnp.mean(o * o, axis=0, keepdims=True) + EPS))
            out = jnp.concatenate(outs, axis=0) * sub_ref[...] * (1.0 - LAMBDA_INIT)
            o_ref[step * DIFF_PAIRS_PER_STEP + i] = out.T.astype(BF16)
        return carry
    lax.fori_loop(0, n_steps, step_body, 0)


def _diff(lq1, lk1, lq2, lk2, sub, qb, kb, vbt, near_b, B, S):
    nq = S // TQ
    T = B * S
    full = lambda a: pl.BlockSpec(a.shape, lambda b, q: (0,) * a.ndim)
    k_spec = pl.BlockSpec((4, S, LANE), lambda b, q: (0, b, 0))
    v_spec = pl.BlockSpec((4, LANE, S), lambda b, q: (0, 0, b))
    qo = pl.BlockSpec((4, TQ, LANE), lambda b, q: (0, b * nq + q, 0))
    return pl.pallas_call(
        _diff_kernel,
        grid=(B, nq),
        in_specs=[full(lq1), full(lk1), full(lq2), full(lk2), full(sub),
                  pl.BlockSpec((4, LANE, TQ), lambda b, q: (0, 0, b * nq + q)), k_spec, v_spec, full(near_b)],
        out_specs=qo,
        out_shape=jax.ShapeDtypeStruct((4, T, LANE), BF16),
        scratch_shapes=_stream_scratch(DIFF_PAIRS_PER_STEP * 4),
        compiler_params=_cparams(("parallel", "arbitrary")),
        name="diff",
    )(lq1, lk1, lq2, lk2, sub, qb, kb, vbt, near_b)


def _outproj_kernel(x_ref, oa_ref, ob_ref, w_ref, g_ref, wr_ref, br_ref, h_ref, xt_ref, route_ref, cnt_ref,
                    carry_ref):
    @pl.when(pl.program_id(0) == 0)
    def _():
        carry_ref[...] = jnp.zeros(carry_ref.shape, F32)

    o = jnp.concatenate([oa_ref[r] for r in range(4)] + [ob_ref[r] for r in range(4)], axis=-1)
    h = x_ref[...] + _dot(o, w_ref[...])
    h_ref[...] = h
    tn32 = h * lax.rsqrt(jnp.mean(h * h, axis=-1, keepdims=True) + EPS) * g_ref[...]
    xt_ref[:, 0:D_MODEL] = tn32
    tn = tn32.astype(BF16)
    logits = _dot(tn, wr_ref[...]) + br_ref[...]
    lane = lax.broadcasted_iota(jnp.int32, (1, LANE), 1)
    lane_f = lane.astype(F32)
    is_grp = lane < MOE_GROUPS
    lg = jnp.where(is_grp, logits, NEG)
    mg = jnp.max(lg, axis=-1, keepdims=True)
    zg = jnp.sum(jnp.where(is_grp, jnp.exp(lg - mg), 0.0), axis=-1, keepdims=True)
    g_prob = 1.0 / zg
    g_idx = jnp.min(jnp.where(lg == mg, lane_f, 1e9), axis=-1, keepdims=True)
    lane_grp = jnp.where((lane >= MOE_GROUPS) & (lane < MOE_GROUPS + N_EXPERTS),
                         lax.shift_right_arithmetic(lane - MOE_GROUPS, 3), -1).astype(F32)
    le = jnp.where(lane_grp == g_idx, logits, NEG)
    m1 = jnp.max(le, axis=-1, keepdims=True)
    e1 = jnp.min(jnp.where(le == m1, lane_f, 1e9), axis=-1, keepdims=True)
    le2 = jnp.where(lane_f == e1, NEG, le)
    m2 = jnp.max(le2, axis=-1, keepdims=True)
    e2 = jnp.min(jnp.where(le2 == m2, lane_f, 1e9), axis=-1, keepdims=True)
    ratio = jnp.exp(m2 - m1)
    w1 = g_prob / (1.0 + ratio)
    w2 = w1 * ratio
    xt_ref[:, D_MODEL:D_MODEL + LANE] = jnp.where(lane_f == e1, w1, 0.0) + jnp.where(lane_f == e2, w2, 0.0)

    tm = h.shape[0]
    onehot = jnp.where(lane_f == g_idx, 1.0, 0.0)
    earlier = jnp.where(lax.broadcasted_iota(jnp.int32, (tm, tm), 0) > lax.broadcasted_iota(jnp.int32, (tm, tm), 1),
                        1.0, 0.0).astype(BF16)
    prefix = _dot(earlier, onehot.astype(BF16)) + carry_ref[...]
    rank = jnp.sum(onehot * prefix, axis=-1, keepdims=True)
    carry_ref[...] += jnp.sum(onehot, axis=0, keepdims=True)
    route_ref[...] = jnp.where(lane == 0, g_idx, jnp.where(lane == 1, rank, 0.0))
    cnt_ref[...] = jnp.broadcast_to(carry_ref[...], cnt_ref.shape)


def _out_proj(x2, oa, ob, w, g, wr, br):
    T = x2.shape[0]
    tm = TM_PROJ
    row = lambda i: (i, 0)
    full = lambda a: pl.BlockSpec(a.shape, lambda i: (0,) * a.ndim)
    o4 = pl.BlockSpec((4, tm, LANE), lambda i: (0, i, 0))
    return pl.pallas_call(
        _outproj_kernel,
        grid=(T // tm,),
        in_specs=[pl.BlockSpec((tm, D_MODEL), row), o4, o4, full(w), full(g), full(wr), full(br)],
        out_specs=[pl.BlockSpec((tm, D_MODEL), row), pl.BlockSpec((tm, XT_WIDTH), row), pl.BlockSpec((tm, LANE), row),
                   pl.BlockSpec((8, LANE), lambda i: (0, 0))],
        out_shape=[jax.ShapeDtypeStruct((T, D_MODEL), F32), jax.ShapeDtypeStruct((T, XT_WIDTH), F32),
                   jax.ShapeDtypeStruct((T, LANE), F32), jax.ShapeDtypeStruct((8, LANE), F32)],
        scratch_shapes=[pltpu.VMEM((1, LANE), F32)],
        compiler_params=_cparams(("arbitrary",)),
        name="out_proj",
    )(x2, oa, ob, w, g, wr, br)


def _moe_kernel(tg_ref, ok_ref, idx_ref, idxn_ref, xt_hbm, wg_ref, wu_ref, wd_ref, y_ref, xbuf, sem):
    i = pl.program_id(0)
    slot = i % 2
    tm = xbuf.shape[1]

    def row_copy(index_ref, r, dst_slot):
        return pltpu.make_async_copy(xt_hbm.at[pl.ds(index_ref[0, 0, r], 1), :],
                                     xbuf.at[dst_slot, pl.ds(r, 1), :], sem.at[dst_slot])

    @pl.when(i == 0)
    def _():
        def body(r, c):
            row_copy(idx_ref, r, 0).start()
            return c
        lax.fori_loop(0, tm, body, 0)

    @pl.when((i == 0) | (ok_ref[jnp.maximum(i - 1, 0)] == 1))
    def _():
        pltpu.make_async_copy(xt_hbm.at[pl.ds(0, tm), :], xbuf.at[slot], sem.at[slot]).wait()

    @pl.when(ok_ref[i] == 0)
    def _():
        y_ref[...] = jnp.zeros(y_ref.shape, F32)

    @pl.when(ok_ref[i] == 1)
    def _():
        lane = lax.broadcasted_iota(jnp.int32, (1, LANE), 1)
        x = xbuf[slot, :, 0:D_MODEL].astype(BF16)
        cmb = xbuf[slot, :, D_MODEL:D_MODEL + LANE]
        first_lane = MOE_GROUPS + tg_ref[i] * EPG
        per_expert = tm // EPG
        y = jnp.zeros((tm, D_MODEL), F32)
        for e in range(EPG):
            a = _dot(x, wg_ref[0, e])
            b = _dot(x, wu_ref[0, e])
            ce = jnp.sum(jnp.where(lane == first_lane + e, cmb, 0.0), axis=-1, keepdims=True)
            y = y + _dot(((a * jax.nn.sigmoid(a)) * b * ce).astype(BF16), wd_ref[0, e])
            for r in range(e * per_expert, (e + 1) * per_expert):
                row_copy(idxn_ref, r, 1 - slot).start(priority=r % 2)
        for blk in range(ROW_SLABS):
            y_ref[:, blk, :] = y[:, blk * LANE:(blk + 1) * LANE]


def _moe(tile_group, tile_ok, src3, xt, wg, wu, wd):
    n_tiles, _, tm = src3.shape
    last = n_tiles - 1
    grid_spec = pltpu.PrefetchScalarGridSpec(
        num_scalar_prefetch=2,
        grid=(n_tiles,),
        in_specs=[pl.BlockSpec((1, 1, tm), lambda i, tg, ok: (i, 0, 0), memory_space=pltpu.SMEM),
                  pl.BlockSpec((1, 1, tm), lambda i, tg, ok: (jnp.minimum(i + 1, last), 0, 0), memory_space=pltpu.SMEM),
                  pl.BlockSpec(memory_space=pl.ANY),
                  pl.BlockSpec((1, EPG, D_MODEL, EXPERT_FF), lambda i, tg, ok: (tg[i], 0, 0, 0)),
                  pl.BlockSpec((1, EPG, D_MODEL, EXPERT_FF), lambda i, tg, ok: (tg[i], 0, 0, 0)),
                  pl.BlockSpec((1, EPG, EXPERT_FF, D_MODEL), lambda i, tg, ok: (tg[i], 0, 0, 0))],
        out_specs=pl.BlockSpec((tm, ROW_SLABS, LANE), lambda i, tg, ok: (i, 0, 0)),
        scratch_shapes=[pltpu.VMEM((2, tm, XT_WIDTH), F32), pltpu.SemaphoreType.DMA((2,))],
    )
    return pl.pallas_call(
        _moe_kernel,
        grid_spec=grid_spec,
        out_shape=jax.ShapeDtypeStruct((n_tiles * tm, ROW_SLABS, LANE), F32),
        compiler_params=pltpu.CompilerParams(dimension_semantics=("arbitrary",), vmem_limit_bytes=VMEM_LIMIT_MOE),
        name="moe",
    )(tile_group, tile_ok, src3, src3, xt, wg, wu, wd)


def _final_kernel(pos_ref, posn_ref, y_hbm, h_ref, gf_ref, o_ref, ybuf, sem):
    i = pl.program_id(0)
    slot = i % 2
    tm = h_ref.shape[0]

    def issue(index_ref, dst_slot):
        group = 8

        def body(j, c):
            for k in range(group):
                r = j * group + k
                pltpu.make_async_copy(y_hbm.at[index_ref[0, 0, r]], ybuf.at[dst_slot, pl.ds(r * ROW_SLABS, ROW_SLABS), :],
                                      sem.at[dst_slot]).start(priority=k % 2)
            return c
        lax.fori_loop(0, tm // group, body, 0)

    @pl.when(i == 0)
    def _():
        issue(pos_ref, 0)

    @pl.when(i + 1 < pl.num_programs(0))
    def _():
        issue(posn_ref, 1 - slot)

    pltpu.make_async_copy(ybuf.at[slot], ybuf.at[slot], sem.at[slot]).wait()
    y = jnp.concatenate([ybuf[slot, pl.ds(blk, tm, stride=ROW_SLABS), :] for blk in range(ROW_SLABS)], axis=-1)
    h = h_ref[...] + y
    o_ref[...] = h * lax.rsqrt(jnp.mean(h * h, axis=-1, keepdims=True) + EPS) * gf_ref[...]


def _final(pos3, y_sorted, h, gf):
    n_tiles, _, tm = pos3.shape
    last = n_tiles - 1
    row = lambda i: (i, 0)
    return pl.pallas_call(
        _final_kernel,
        grid=(n_tiles,),
        in_specs=[pl.BlockSpec((1, 1, tm), lambda i: (i, 0, 0), memory_space=pltpu.SMEM),
                  pl.BlockSpec((1, 1, tm), lambda i: (jnp.minimum(i + 1, last), 0, 0), memory_space=pltpu.SMEM),
                  pl.BlockSpec(memory_space=pl.ANY),
                  pl.BlockSpec((tm, D_MODEL), row),
                  pl.BlockSpec((1, D_MODEL), lambda i: (0, 0))],
        out_specs=pl.BlockSpec((tm, D_MODEL), row),
        out_shape=jax.ShapeDtypeStruct((n_tiles * tm, D_MODEL), F32),
        scratch_shapes=[pltpu.VMEM((2, tm * ROW_SLABS, LANE), F32), pltpu.SemaphoreType.DMA((2,))],
        compiler_params=_cparams(("arbitrary",)),
        name="final",
    )(pos3, pos3, y_sorted, h, gf)


def _qa_perm():
    new = np.arange(NSA_HEADS * NSA_HD)
    r, g, d = new // LANE, (new % LANE) // NSA_HD, new % NSA_HD
    return (g * NSA_R + r) * NSA_HD + d


def _block_diag2(w):
    z = jnp.zeros_like(w)
    return jnp.concatenate([jnp.concatenate([w, z], axis=-1), jnp.concatenate([z, w], axis=-1)], axis=-2)


def kernel(x, rel_bias, ln_mix, w_in, cmp_pos_k, cmp_pos_v, cmp_k_w1, cmp_k_w2, cmp_v_w1, cmp_v_w2,
           diff_lq1, diff_lk1, diff_lq2, diff_lk2, diff_subln, w_out, ln_ffn,
           router_group_w, router_group_b, router_expert_w, router_expert_b,
           exp_w_gate, exp_w_up, exp_w_down, ln_final):
    B, S, D = x.shape
    T = B * S
    assert D == D_MODEL and S % TQ == 0 and S >= WINDOW and T % TM_MOE == 0 and T % TM_FINAL == 0
    x2 = x.reshape(T, D)
    perm = _qa_perm()

    w = w_in[0]
    c_kc, c_vc, c_ks, c_vs, c_kw, c_vw, c_gt = 512, 640, 768, 896, 1024, 1152, 1280
    c_qb = c_gt + N_GATE
    c_kb, c_vb = c_qb + 512, c_qb + 1024
    col = lambda c, n=LANE: w[:, c:c + n]
    w_tok = jnp.concatenate([col(c_kc), col(c_vc), col(c_ks), col(c_kw), col(c_kb, 512)], axis=1).astype(BF16)
    w_feat = jnp.concatenate([w[:, perm], col(c_qb, 512), col(c_vs), col(c_vw), col(c_vb, 512), col(c_gt, N_GATE),
                              jnp.zeros((D, GATE_ROWS - N_GATE), F32)], axis=1).T.astype(BF16)
    qa, kc, vc, ks, kw, qb, kb, vst, vwt, vbt, gates_t = _in_proj(x2, ln_mix[0][None, :], w_tok, w_feat)

    w1k = _block_diag2(cmp_k_w1[0].reshape(CMP_LEN, NSA_HD, CMP_HIDDEN)).astype(BF16)
    w1v = _block_diag2(cmp_v_w1[0].reshape(CMP_LEN, NSA_HD, CMP_HIDDEN)).astype(BF16)
    w2k = _block_diag2(cmp_k_w2[0]).astype(BF16)
    w2v = _block_diag2(cmp_v_w2[0]).astype(BF16)
    posk = jnp.tile(cmp_pos_k[0], (1, NSA_G))
    posv = jnp.tile(cmp_pos_v[0], (1, NSA_G))
    kcmp, vcmpt = _compress(kc, vc, posk, posv, w1k, w1v, w2k, w2v, B, S)

    near, bias_c = _bias_tiles(rel_bias.T, S)

    n_sel = S // SEL_BLOCK
    nrow = S // CMP_STRIDE
    c_start = np.arange(nrow) * CMP_STRIDE
    s_start = np.arange(n_sel) * SEL_BLOCK
    ov = ((c_start[None, :] <= s_start[:, None] + SEL_BLOCK - 1)
          & (c_start[None, :] + CMP_LEN - 1 >= s_start[:, None])
          & (np.arange(nrow)[None, :] < nrow - 1)).astype(np.float32)
    o_a = _nsa(qa, gates_t, kcmp, vcmpt, bias_c, ks, vst, kw, vwt, near[:NSA_HEADS], jnp.asarray(ov, BF16), B, S)

    sub = jnp.tile(diff_subln[0], 2)[:, None]
    o_b = _diff(diff_lq1[0][None, :], diff_lk1[0][None, :], diff_lq2[0][None, :], diff_lk2[0][None, :],
                sub, qb, kb, vbt, near[NSA_HEADS:], B, S)

    w_o = jnp.concatenate([w_out[0][:512][perm], w_out[0][512:]], axis=0).astype(BF16)
    n_r = MOE_GROUPS + N_EXPERTS
    wr = jnp.concatenate([router_group_w[0], router_expert_w[0], jnp.zeros((D, LANE - n_r), F32)], axis=1).astype(BF16)
    br = jnp.concatenate([router_group_b[0], router_expert_b[0], jnp.zeros((LANE - n_r,), F32)])[None, :]
    h1, xt, route, counts = _out_proj(x2, o_a, o_b, w_o, ln_ffn[0][None, :], wr, br)

    tm = TM_MOE
    n_tiles = T // tm + MOE_GROUPS
    cnt = counts[0, :MOE_GROUPS].astype(jnp.int32)
    ends = jnp.cumsum((cnt + tm - 1) // tm * tm)
    starts = ends - (cnt + tm - 1) // tm * tm
    pos = starts[route[:, 0].astype(jnp.int32)] + route[:, 1].astype(jnp.int32)
    src = jnp.zeros((n_tiles * tm,), jnp.int32).at[pos].set(
        jnp.arange(T, dtype=jnp.int32), unique_indices=True, mode="promise_in_bounds")
    tile_start = jnp.arange(n_tiles, dtype=jnp.int32) * tm
    tile_group = jnp.minimum(jnp.searchsorted(ends, tile_start, side="right"), MOE_GROUPS - 1).astype(jnp.int32)
    tile_ok = (tile_start < ends[-1]).astype(jnp.int32)

    by_group = lambda a: a[0].astype(BF16).reshape((MOE_GROUPS, EPG) + a.shape[2:])
    y_sorted = _moe(tile_group, tile_ok, src.reshape(n_tiles, 1, tm), xt,
                    by_group(exp_w_gate), by_group(exp_w_up), by_group(exp_w_down))
    out = _final(pos.reshape(T // TM_FINAL, 1, TM_FINAL), y_sorted, h1, ln_final[None, :])
    return out.reshape(B, S, D)
```

```python
import math

import numpy as np
import jax
import jax.numpy as jnp
from jax import lax
from jax.experimental import pallas as pl
from jax.experimental.pallas import tpu as pltpu

F32 = jnp.float32
BF16 = jnp.bfloat16
NEG = -1e30
EPS = 1e-6
LOG2E = math.log2(math.e)

D_MODEL = 1024
LANE = 128
NSA_HEADS, NSA_G, NSA_R, NSA_HD = 8, 2, 4, 64
CMP_LEN, CMP_STRIDE, CMP_HIDDEN = 32, 16, 128
SEL_BLOCK, SEL_TOPN, SEL_FORCED_LOCAL, WINDOW = 64, 8, 2, 512
DIFF_HEADS, DIFF_HD = 8, 32
REL_BUCKETS, REL_MAX_EXACT, REL_MAX_DIST = 32, 16, 128
N_REL_HEADS = NSA_HEADS + DIFF_HEADS
MOE_GROUPS, EPG, N_EXPERTS, EXPERT_FF = 4, 8, 32, 256
LAMBDA_INIT = 0.8 - 0.6 * math.exp(-0.3 * 0)
N_GATE = NSA_HEADS * 3
GATE_ROWS = 32

TQ = 256
DIFF_PAIRS_PER_STEP = 4
TM_PROJ = 512
TM_MOE = 512
TM_FINAL = 512
XT_WIDTH = D_MODEL + LANE
ROW_SLABS = D_MODEL // LANE
VMEM_LIMIT = 48 * 1024 * 1024
VMEM_LIMIT_MOE = 56 * 1024 * 1024


def _cparams(sem):
    return pltpu.CompilerParams(dimension_semantics=sem, vmem_limit_bytes=VMEM_LIMIT)


def _dot(a, b):
    return jnp.dot(a, b, preferred_element_type=F32)


def _dot_nt(a, b):
    return lax.dot_general(a, b, (((1,), (1,)), ((), ())), preferred_element_type=F32)


def _bucket_thresholds():
    n = np.arange(0, REL_MAX_DIST + 1)
    nf = np.maximum(n, 1).astype(np.float32)
    large = REL_MAX_EXACT + (np.log(nf / np.float32(REL_MAX_EXACT)) / np.float32(math.log(REL_MAX_DIST / REL_MAX_EXACT))
                             * np.float32(REL_BUCKETS - REL_MAX_EXACT)).astype(np.int32)
    large = np.minimum(large, REL_BUCKETS - 1)
    bucket = np.where(n < REL_MAX_EXACT, n, large)
    return [int(np.argmax(bucket >= b)) for b in range(REL_BUCKETS)]


_THR = _bucket_thresholds()


def _inproj_kernel(x_ref, g_ref, w_ref, wt_ref, qa_ref, kc_ref, vc_ref, ks_ref, kw_ref, qb_ref, kb_ref,
                   vst_ref, vwt_ref, vbt_ref, gt_ref):
    x = x_ref[...]
    xn = (x * lax.rsqrt(jnp.mean(x * x, axis=-1, keepdims=True) + EPS) * g_ref[...]).astype(BF16)
    a = _dot(xn, w_ref[:, 0:512])
    kc_ref[...] = a[:, 0:128]
    vc_ref[...] = a[:, 128:256]
    ks_ref[...] = a[:, 256:384].astype(BF16)
    kw_ref[...] = a[:, 384:512].astype(BF16)
    a = _dot(xn, w_ref[:, 512:1024])
    for r in range(4):
        kb_ref[r] = a[:, r * LANE:(r + 1) * LANE].astype(BF16)

    ft = _dot_nt(wt_ref[...], xn)

    def feat(row0, rows=LANE):
        return ft[row0:row0 + rows, :]
    for r in range(4):
        qa_ref[r] = (feat(r * LANE) * (NSA_HD ** -0.5 * LOG2E)).astype(BF16)
        qb_ref[r] = (feat(512 + r * LANE) * (DIFF_HD ** -0.5 * LOG2E)).astype(BF16)
        vbt_ref[r] = feat(1280 + r * LANE).astype(BF16)
    vst_ref[...] = feat(1024).astype(BF16)
    vwt_ref[...] = feat(1152).astype(BF16)
    gt_ref[...] = feat(1792, GATE_ROWS)


def _in_proj(x2, g, w, wt):
    T = x2.shape[0]
    tm = TM_PROJ
    row = lambda i: (i, 0)
    o128b = jax.ShapeDtypeStruct((T, LANE), BF16)
    o128f = jax.ShapeDtypeStruct((T, LANE), F32)
    o4 = jax.ShapeDtypeStruct((4, T, LANE), BF16)
    ot = jax.ShapeDtypeStruct((LANE, T), BF16)
    o4t = jax.ShapeDtypeStruct((4, LANE, T), BF16)
    s128 = pl.BlockSpec((tm, LANE), row)
    s4 = pl.BlockSpec((4, tm, LANE), lambda i: (0, i, 0))
    st = pl.BlockSpec((LANE, tm), lambda i: (0, i))
    s4t = pl.BlockSpec((4, LANE, tm), lambda i: (0, 0, i))
    return pl.pallas_call(
        _inproj_kernel,
        grid=(T // tm,),
        in_specs=[pl.BlockSpec((tm, D_MODEL), row),
                  pl.BlockSpec((1, D_MODEL), lambda i: (0, 0)),
                  pl.BlockSpec(w.shape, lambda i: (0, 0)),
                  pl.BlockSpec(wt.shape, lambda i: (0, 0))],
        out_specs=[s4t, s128, s128, s128, s128, s4t, s4, st, st, s4t, pl.BlockSpec((GATE_ROWS, tm), lambda i: (0, i))],
        out_shape=[o4t, o128f, o128f, o128b, o128b, o4t, o4, ot, ot, o4t, jax.ShapeDtypeStruct((GATE_ROWS, T), F32)],
        compiler_params=_cparams(("parallel",)),
        name="in_proj",
    )(x2, g, w, wt)


def _gelu_tanh(x):
    return 0.5 * x * (1.0 + jnp.tanh(math.sqrt(2.0 / math.pi) * (x + 0.044715 * (x * x * x))))


def _compress_kernel(kc_ref, vc_ref, posk_ref, posv_ref, w1k_ref, w1v_ref, w2k_ref, w2v_ref, ko_ref, vo_ref):
    nrow = kc_ref.shape[0] // CMP_STRIDE
    rid = lax.broadcasted_iota(jnp.int32, (nrow, 1), 0)
    cid = lax.broadcasted_iota(jnp.int32, (1, nrow), 1)
    for src, pos, w1, w2, out, transposed in ((kc_ref, posk_ref, w1k_ref, w2k_ref, ko_ref, False),
                                              (vc_ref, posv_ref, w1v_ref, w2v_ref, vo_ref, True)):
        hid_a = jnp.zeros((nrow, 2 * CMP_HIDDEN), F32)
        hid_b = jnp.zeros((nrow, 2 * CMP_HIDDEN), F32)
        for m in range(CMP_STRIDE):
            y = src[pl.ds(m, nrow, stride=CMP_STRIDE), :]
            hid_a = hid_a + _dot((y + pos[m:m + 1, :]).astype(BF16), w1[m])
            hid_b = hid_b + _dot((y + pos[CMP_STRIDE + m:CMP_STRIDE + m + 1, :]).astype(BF16), w1[CMP_STRIDE + m])
        hid = hid_a + pltpu.roll(hid_b, nrow - 1, 0)
        o = _dot(_gelu_tanh(hid).astype(BF16), w2[...])
        if transposed:
            out[0] = jnp.where(cid < nrow - 1, o.T, 0.0).astype(BF16)
        else:
            out[0] = jnp.where(rid < nrow - 1, o, 0.0).astype(BF16)


def _compress(kc, vc, posk, posv, w1k, w1v, w2k, w2v, B, S):
    nrow = S // CMP_STRIDE
    assert nrow == LANE
    full = lambda a: pl.BlockSpec(a.shape, lambda b: (0,) * a.ndim)
    src = pl.BlockSpec((S, LANE), lambda b: (b, 0))
    osp = pl.BlockSpec((1, nrow, LANE), lambda b: (b, 0, 0))
    osh = jax.ShapeDtypeStruct((B, nrow, LANE), BF16)
    return pl.pallas_call(
        _compress_kernel,
        grid=(B,),
        in_specs=[src, src, full(posk), full(posv), full(w1k), full(w1v), full(w2k), full(w2v)],
        out_specs=[osp, osp],
        out_shape=[osh, osh],
        compiler_params=_cparams(("parallel",)),
        name="compress",
    )(kc, vc, posk, posv, w1k, w1v, w2k, w2v)


def _bias_from_dist(dist, tab_ref, h):
    val = jnp.full(dist.shape, tab_ref[h, 0], F32)
    for b in range(1, REL_BUCKETS):
        val = jnp.where(dist >= _THR[b], tab_ref[h, b], val)
    return val


def _bias_near_kernel(tab_ref, out_ref):
    h = pl.program_id(0)
    j = lax.broadcasted_iota(jnp.int32, (TQ, TQ), 0)
    i = lax.broadcasted_iota(jnp.int32, (TQ, TQ), 1)
    for d in range(2):
        bias = (_bias_from_dist(i - j + d * TQ, tab_ref, h) - tab_ref[h, REL_BUCKETS - 1]) * LOG2E
        out_ref[0, d] = jnp.where(i - j + d * TQ >= 0, bias, NEG)


def _bias_cmp_kernel(tab_ref, out_ref):
    h = pl.program_id(0)
    c = lax.broadcasted_iota(jnp.int32, (LANE, TQ), 0)
    t = pl.program_id(1) * TQ + lax.broadcasted_iota(jnp.int32, (LANE, TQ), 1)
    out_ref[0] = _bias_from_dist(t - (c * CMP_STRIDE + CMP_LEN - 1), tab_ref, h) * LOG2E


def _bias_tiles(tab_t, S):
    smem = pl.BlockSpec(memory_space=pltpu.SMEM)
    near = pl.pallas_call(
        _bias_near_kernel,
        grid=(N_REL_HEADS,),
        in_specs=[smem],
        out_specs=pl.BlockSpec((1, 2, TQ, TQ), lambda h: (h, 0, 0, 0)),
        out_shape=jax.ShapeDtypeStruct((N_REL_HEADS, 2, TQ, TQ), F32),
        compiler_params=_cparams(("parallel",)),
        name="bias_near",
    )(tab_t)
    cmp_bias = pl.pallas_call(
        _bias_cmp_kernel,
        grid=(NSA_HEADS, S // TQ),
        in_specs=[smem],
        out_specs=pl.BlockSpec((1, LANE, TQ), lambda h, q: (h, 0, q)),
        out_shape=jax.ShapeDtypeStruct((NSA_HEADS, LANE, S), F32),
        compiler_params=_cparams(("parallel", "parallel")),
        name="bias_cmp",
    )(tab_t)
    return near, cmp_bias


ACC_ROWS = NSA_HD + 16


def _with_ones(vt):
    return jnp.concatenate([vt, jnp.ones((ACC_ROWS - vt.shape[0], vt.shape[1]), BF16)], axis=0)


def _probs(s, m):
    return jnp.exp2((s - m).astype(BF16))


def _flash_first(ss, vts):
    ms = [jnp.max(s, axis=0, keepdims=True) for s in ss]
    ps = [_probs(s, m) for s, m in zip(ss, ms)]
    return tuple((m, _dot(vt, p)) for m, p, vt in zip(ms, ps, vts))


def _flash_update(ss, vts, sts):
    ms = [jnp.maximum(st[0], jnp.max(s, axis=0, keepdims=True)) for s, st in zip(ss, sts)]
    alphas = [jnp.exp2(st[0] - m) for m, st in zip(ms, sts)]
    ps = [_probs(s, m) for s, m in zip(ss, ms)]
    return tuple((m, a * st[1] + _dot(vt, p)) for m, a, st, p, vt in zip(ms, alphas, sts, ps, vts))


def _flash_out(st):
    acc = st[1]
    return acc[0:NSA_HD, :] * (1.0 / acc[NSA_HD:NSA_HD + 1, :])


def _stream_scratch(n):
    return ([pltpu.VMEM((8, TQ), F32), pltpu.VMEM((ACC_ROWS, TQ), F32)] * n
            + [pltpu.VMEM((TQ, TQ), F32)] * (2 * n) + [pltpu.VMEM((TQ, TQ), BF16)] * (2 * n))


def _stream_refs(refs, n):
    st_refs = tuple((refs[2 * c], refs[2 * c + 1]) for c in range(n))
    s, p = refs[2 * n:4 * n], refs[4 * n:6 * n]
    return st_refs, tuple(s[:n]), tuple(s[n:]), tuple(p[:n]), tuple(p[n:])


def _causal_stream(qi, scores, vtile, st_refs, s_a, s_b, p_a, p_b, diag_ready=False):
    n_far = jnp.maximum(qi - 1, 0)
    top = qi - 2

    chains = range(len(st_refs))

    def write(s_refs, kt, near=2):
        for c in chains:
            s_refs[c][...] = scores(jnp.maximum(kt, 0), near, c)

    def pending(p_refs, kt):
        return [_dot(vtile(jnp.maximum(kt, 0), c), p_refs[c][...]) for c in chains]

    def step(s_cur, p_cur, s_next, kt_next, p_prev, kt_prev, anchor=None):
        rescale = []
        for c in chains:
            nxt = scores(jnp.maximum(kt_next, 0), 2, c)
            s_next[c][...] = nxt
            pv = _dot(vtile(jnp.maximum(kt_prev, 0), c), p_prev[c][...])
            st = st_refs[c][0]
            s = s_cur[c][...]
            m_old = st[0:1, :]
            m = jnp.maximum(m_old, jnp.max(s, axis=0, keepdims=True))
            if anchor is not None:
                m = m + anchor
            anchor = nxt[0:1, :] * 0.0
            p_cur[c][...] = _probs(s, m)
            st[0:1, :] = m
            rescale.append((jnp.exp2(m_old - m), pv))
        for c in chains:
            acc = st_refs[c][1]
            acc[...] = rescale[c][0] * (acc[...] + rescale[c][1])
        return anchor

    if not diag_ready:
        write(s_a, qi, 0)
    for c in chains:
        s_b[c][...] = scores(jnp.maximum(qi - 1, 0), 1, c)
        st, acc = st_refs[c]
        s = s_a[c][...]
        m = jnp.max(s, axis=0, keepdims=True)
        p_a[c][...] = _probs(s, m)
        st[0:1, :] = m
        acc[...] = jnp.zeros(acc.shape, F32)
    step(s_b, p_b, s_a, top, p_a, qi)

    def pair(j, c):
        kt = top - 2 * j
        anchor = step(s_a, p_a, s_b, kt - 1, p_b, kt + 1)
        step(s_b, p_b, s_a, kt - 2, p_a, kt, anchor)
        return c
    lax.fori_loop(0, n_far // 2, pair, 0)

    @pl.when(n_far % 2 == 1)
    def _():
        for c, pv in zip(chains, pending(p_b, 1)):
            st, acc = st_refs[c]
            s = s_a[c][...]
            m_old = st[0:1, :]
            m = jnp.maximum(m_old, jnp.max(s, axis=0, keepdims=True))
            st[0:1, :] = m
            acc[...] = jnp.exp2(m_old - m) * (acc[...] + pv) + _dot(vtile(0, c), _probs(s, m))

    @pl.when(n_far % 2 == 0)
    def _():
        last = jnp.where(n_far == 0, qi - 1, 0)
        for c, pv in zip(chains, pending(p_b, last)):
            acc = st_refs[c][1]
            acc[...] = acc[...] + pv
    return tuple((st[0:1, :], acc[...]) for st, acc in st_refs)


def _ktile(ref, kt):
    return ref[pl.ds(pl.multiple_of(kt * TQ, TQ), TQ), :]


def _vtile(ref, kt):
    return ref[:, pl.ds(pl.multiple_of(kt * TQ, TQ), TQ)]


def _nsa_kernel(qa_ref, gt_ref, kcmp_ref, vcmpt_ref, bc_ref, ks_ref, vst_ref, kw_ref, vwt_ref, dn_ref,
                ov_ref, o_ref, psum_ref, sel_ref, oacc_ref, sig_ref, *scratch):
    st_refs, s_a, s_b, p_a, p_b = _stream_refs(scratch, NSA_HEADS)
    qi = pl.program_id(1)
    t0 = qi * TQ
    sub_grp = lax.shift_right_arithmetic(lax.broadcasted_iota(jnp.int32, (LANE, 1), 0), 6)
    sig_ref[...] = jax.nn.sigmoid(gt_ref[...])

    def gate_row(c):
        return sig_ref[pl.ds(c, 1), :]

    def masked_q(r, g):
        return jnp.where(sub_grp == g, qa_ref[r].astype(F32), 0.0).astype(BF16)

    n_cmp = kcmp_ref.shape[1] - 1
    crow = lax.broadcasted_iota(jnp.int32, (LANE, 1), 0)
    cmp_end = jnp.where(crow < n_cmp, crow * CMP_STRIDE + CMP_LEN - 1, 1 << 30)
    mask_c = (t0 + lax.broadcasted_iota(jnp.int32, (1, TQ), 1)) >= cmp_end

    heads = [(r, g) for r in range(NSA_R) for g in range(NSA_G)]
    scores_c = [_dot(kcmp_ref[0], masked_q(r, g)) for r, g in heads]
    probs_c = []
    for (r, g), s in zip(heads, scores_c):
        s = jnp.where(mask_c, s + bc_ref[g * NSA_R + r], NEG)
        p = jnp.where(mask_c, jnp.exp2(s - jnp.max(s, axis=0, keepdims=True)), 0.0)
        l = jnp.sum(p, axis=0, keepdims=True)
        probs_c.append(p * jnp.where(l > 0.0, 1.0 / l, 0.0))
    for g in range(NSA_G):
        psum_ref[g] = sum(p for (r, gg), p in zip(heads, probs_c) if gg == g)
    outs_c = [_dot(vcmpt_ref[0], p.astype(BF16)) for p in probs_c]
    for r in range(NSA_R):
        o0, o1 = outs_c[r * NSA_G], outs_c[r * NSA_G + 1]
        oacc_ref[r] = jnp.where(sub_grp == 0, gate_row(r * 3) * o0, gate_row((NSA_R + r) * 3) * o1)

    has_prev = jnp.where(qi >= 1, 0.0, NEG)
    has_wfar = jnp.where(qi >= WINDOW // TQ, 0.0, NEG)
    kt_prev = jnp.maximum(qi - 1, 0)
    kt_wfar = jnp.maximum(qi - WINDOW // TQ, 0)
    wfar_mask = lax.broadcasted_iota(jnp.int32, (TQ, TQ), 0) > lax.broadcasted_iota(jnp.int32, (TQ, TQ), 1)

    def attend_all_heads():
        chains = [(r, g) for r in range(NSA_R) for g in range(NSA_G)]
        qms = [masked_q(r, g) for r, g in chains]
        hids = [g * NSA_R + r for r, g in chains]

        per_tile = TQ // SEL_BLOCK
        key_blk = lax.shift_right_arithmetic(lax.broadcasted_iota(jnp.int32, (TQ, LANE), 0), 6)
        lane_id = lax.broadcasted_iota(jnp.int32, (TQ, LANE), 1)

        def keys_with_block_onehot(kt, g):
            other = (1 - g) * NSA_HD
            onehot = jnp.where(lane_id - other == key_blk, 1.0, 0.0).astype(BF16)
            in_other = (lane_id >= other) & (lane_id < other + NSA_HD)
            return jnp.where(in_other, onehot, _ktile(ks_ref, kt))

        def query_with_mask_rows(kt, c):
            r, g = chains[c]
            own = qa_ref[r, g * NSA_HD:(g + 1) * NSA_HD, :]
            pair = sel_ref[g, pl.ds(pl.multiple_of((kt // 2) * 2 * per_tile, 2 * per_tile), 2 * per_tile), :]
            rows = jnp.where(kt % 2 == 0, pair[0:per_tile], pair[per_tile:2 * per_tile]).astype(BF16)
            other = jnp.concatenate([rows, jnp.zeros((NSA_HD - per_tile, TQ), BF16)], axis=0)
            return jnp.concatenate([own, other] if g == 0 else [other, own], axis=0)

        def near_bias(s, near, c):
            if near == 0:
                return s + dn_ref[hids[c], 0]
            if near == 1:
                return s + dn_ref[hids[c], 1] + has_prev
            return s

        def group_values(ref, kt, c):
            g = chains[c][1]
            return _with_ones(ref[g * NSA_HD:(g + 1) * NSA_HD, pl.ds(pl.multiple_of(kt * TQ, TQ), TQ)])

        def finish(sts, branch):
            for c, (r, g) in enumerate(chains):
                oacc_ref[r, g * NSA_HD:(g + 1) * NSA_HD, :] += gate_row(hids[c] * 3 + branch) * _flash_out(sts[c])

        def slc_scores(kt, near, c):
            return near_bias(_dot(keys_with_block_onehot(kt, chains[c][1]), query_with_mask_rows(kt, c)), near, c)

        def slc_vtile(kt, c):
            return group_values(vst_ref, kt, c)

        win_tiles = ((qi, 0), (kt_prev, 1), (kt_wfar, 2))

        def win_scores(c):
            out = []
            for kt, near in win_tiles:
                s = near_bias(_dot(_ktile(kw_ref, kt), qms[c]), near, c)
                out.append(jnp.where(wfar_mask, s + has_wfar, NEG) if near == 2 else s)
            return out

        sts = []
        ahead = win_scores(0)
        for c in range(len(chains)):
            tiles, ahead = ahead, (win_scores(c + 1) if c + 1 < len(chains) else None)
            st = _flash_first([tiles[0]], [group_values(vwt_ref, qi, c)])
            st = _flash_update([tiles[1]], [group_values(vwt_ref, kt_prev, c)], st)
            sts.append(_flash_update([tiles[2]], [group_values(vwt_ref, kt_wfar, c)], st)[0])
        finish(sts, 2)

        n_sel = ov_ref.shape[0]
        jj = lax.broadcasted_iota(jnp.int32, (n_sel, TQ), 0)
        cur = lax.shift_right_arithmetic(t0 + lax.broadcasted_iota(jnp.int32, (n_sel, TQ), 1), 6)
        valid = jj <= cur
        forced = (jj == 0) | (cur - jj < SEL_FORCED_LOCAL)
        for g in range(NSA_G):
            ps = psum_ref[g]
            hi = ps.astype(BF16)
            rem = ps - hi.astype(F32)
            mid = rem.astype(BF16)
            lo = (rem - mid.astype(F32)).astype(BF16)
            ov = ov_ref[...]
            imp = _dot(ov, hi) + _dot(ov, mid) + _dot(ov, lo)
            score = jnp.where(valid, jnp.where(forced, 1e9, imp), -1e9)
            cnt = jnp.zeros((n_sel, TQ), F32)
            for j2 in range(n_sel):
                row = score[j2:j2 + 1, :]
                tie = jnp.where(jj > j2, 1.0, 0.0)
                cnt = cnt + jnp.where(row > score, 1.0, jnp.where(row == score, tie, 0.0))
            sel_ref[g] = jnp.where(cnt < float(min(SEL_TOPN, n_sel)), jnp.where(score > -1e8, 0.0, NEG), NEG)

        finish(_causal_stream(qi, slc_scores, slc_vtile, st_refs, s_a, s_b, p_a, p_b), 1)
    attend_all_heads()

    for r in range(NSA_R):
        o_ref[r] = oacc_ref[r].T.astype(BF16)


def _nsa(qa, gates_t, kcmp, vcmpt, bias_c, ks, vst, kw, vwt, near_a, ov, B, S):
    nq = S // TQ
    T = B * S
    k_spec = pl.BlockSpec((S, LANE), lambda b, q: (b, 0))
    v_spec = pl.BlockSpec((LANE, S), lambda b, q: (0, b))
    full = lambda a: pl.BlockSpec(a.shape, lambda b, q: (0,) * a.ndim)
    cmp_spec = pl.BlockSpec((1, LANE, LANE), lambda b, q: (b, 0, 0))
    qo = pl.BlockSpec((4, TQ, LANE), lambda b, q: (0, b * nq + q, 0))
    return pl.pallas_call(
        _nsa_kernel,
        grid=(B, nq),
        in_specs=[pl.BlockSpec((4, LANE, TQ), lambda b, q: (0, 0, b * nq + q)),
                  pl.BlockSpec((GATE_ROWS, TQ), lambda b, q: (0, b * nq + q)),
                  cmp_spec, cmp_spec,
                  pl.BlockSpec((NSA_HEADS, LANE, TQ), lambda b, q: (0, 0, q)),
                  k_spec, v_spec, k_spec, v_spec, full(near_a), full(ov)],
        out_specs=qo,
        out_shape=jax.ShapeDtypeStruct((4, T, LANE), BF16),
        scratch_shapes=[pltpu.VMEM((NSA_G, LANE, TQ), F32), pltpu.VMEM((NSA_G, S // SEL_BLOCK, TQ), F32),
                        pltpu.VMEM((NSA_R, LANE, TQ), F32), pltpu.VMEM((GATE_ROWS, TQ), F32)]
                       + _stream_scratch(NSA_HEADS),
        compiler_params=_cparams(("parallel", "arbitrary")),
        name="nsa",
    )(qa, gates_t, kcmp, vcmpt, bias_c, ks, vst, kw, vwt, near_a, ov)


def _diff_kernel(lq1_ref, lk1_ref, lq2_ref, lk2_ref, sub_ref, qb_ref, kb_ref, vbt_ref, dn_ref, o_ref, *scratch):
    n = DIFF_PAIRS_PER_STEP * 4
    st_refs, s_a, s_b, p_a, p_b = _stream_refs(scratch, n)
    qi = pl.program_id(1)
    chain_of_row = lax.shift_right_arithmetic(lax.broadcasted_iota(jnp.int32, (LANE, 1), 0), 5)
    lam = (jnp.exp(jnp.sum(lq1_ref[...] * lk1_ref[...], axis=-1, keepdims=True))
           - jnp.exp(jnp.sum(lq2_ref[...] * lk2_ref[...], axis=-1, keepdims=True)) + LAMBDA_INIT)
    has_prev = jnp.where(qi >= 1, 0.0, NEG)
    n_steps = DIFF_HEADS // 2 // DIFF_PAIRS_PER_STEP

    def score_fn(step):
        pairs = [step * DIFF_PAIRS_PER_STEP + c // 4 for c in range(n)]
        qs = [qb_ref[step * DIFF_PAIRS_PER_STEP + i].astype(F32) for i in range(DIFF_PAIRS_PER_STEP)]
        qms = [jnp.where(chain_of_row == c % 4, qs[c // 4], 0.0).astype(BF16) for c in range(n)]

        def scores(kt, near, c):
            s = _dot(kb_ref[pairs[c], pl.ds(pl.multiple_of(kt * TQ, TQ), TQ), :], qms[c])
            if near == 2:
                return s
            bias = dn_ref[2 * pairs[c] + (c % 4) // 2, near]
            return s + bias if near == 0 else s + bias + has_prev
        return scores

    def start_diag(step):
        scores = score_fn(step)
        for c in range(n):
            s_a[c][...] = scores(qi, 0, c)

    start_diag(0)

    def step_body(step, carry):
        def vtile(kt, c):
            hh = (c % 4) // 2
            return _with_ones(vbt_ref[step * DIFF_PAIRS_PER_STEP + c // 4, hh * 2 * DIFF_HD:(hh + 1) * 2 * DIFF_HD,
                                      pl.ds(pl.multiple_of(kt * TQ, TQ), TQ)])

        sts = _causal_stream(qi, score_fn(step), vtile, st_refs, s_a, s_b, p_a, p_b, diag_ready=True)
        start_diag(jnp.minimum(step + 1, n_steps - 1))
        for i in range(DIFF_PAIRS_PER_STEP):
            outs = []
            for hh in range(2):
                c = 4 * i + 2 * hh
                o = _flash_out(sts[c]) - lam * _flash_out(sts[c + 1])
                outs.append(o * lax.rsqrt(jnp.mean(o * o, axis=0, keepdims=True) + EPS))
            out = jnp.concatenate(outs, axis=0) * sub_ref[...] * (1.0 - LAMBDA_INIT)
            o_ref[step * DIFF_PAIRS_PER_STEP + i] = out.T.astype(BF16)
        return carry
    lax.fori_loop(0, n_steps, step_body, 0)


def _diff(lq1, lk1, lq2, lk2, sub, qb, kb, vbt, near_b, B, S):
    nq = S // TQ
    T = B * S
    full = lambda a: pl.BlockSpec(a.shape, lambda b, q: (0,) * a.ndim)
    k_spec = pl.BlockSpec((4, S, LANE), lambda b, q: (0, b, 0))
    v_spec = pl.BlockSpec((4, LANE, S), lambda b, q: (0, 0, b))
    qo = pl.BlockSpec((4, TQ, LANE), lambda b, q: (0, b * nq + q, 0))
    return pl.pallas_call(
        _diff_kernel,
        grid=(B, nq),
        in_specs=[full(lq1), full(lk1), full(lq2), full(lk2), full(sub),
                  pl.BlockSpec((4, LANE, TQ), lambda b, q: (0, 0, b * nq + q)), k_spec, v_spec, full(near_b)],
        out_specs=qo,
        out_shape=jax.ShapeDtypeStruct((4, T, LANE), BF16),
        scratch_shapes=_stream_scratch(DIFF_PAIRS_PER_STEP * 4),
        compiler_params=_cparams(("parallel", "arbitrary")),
        name="diff",
    )(lq1, lk1, lq2, lk2, sub, qb, kb, vbt, near_b)


def _outproj_kernel(x_ref, oa_ref, ob_ref, w_ref, g_ref, wr_ref, br_ref, h_ref, xt_ref, route_ref, cnt_ref,
                    carry_ref):
    @pl.when(pl.program_id(0) == 0)
    def _():
        carry_ref[...] = jnp.zeros(carry_ref.shape, F32)

    o = jnp.concatenate([oa_ref[r] for r in range(4)] + [ob_ref[r] for r in range(4)], axis=-1)
    h = x_ref[...] + _dot(o, w_ref[...])
    h_ref[...] = h
    tn32 = h * lax.rsqrt(jnp.mean(h * h, axis=-1, keepdims=True) + EPS) * g_ref[...]
    xt_ref[:, 0:D_MODEL] = tn32
    tn = tn32.astype(BF16)
    logits = _dot(tn, wr_ref[...]) + br_ref[...]
    lane = lax.broadcasted_iota(jnp.int32, (1, LANE), 1)
    lane_f = lane.astype(F32)
    is_grp = lane < MOE_GROUPS
    lg = jnp.where(is_grp, logits, NEG)
    mg = jnp.max(lg, axis=-1, keepdims=True)
    zg = jnp.sum(jnp.where(is_grp, jnp.exp(lg - mg), 0.0), axis=-1, keepdims=True)
    g_prob = 1.0 / zg
    g_idx = jnp.min(jnp.where(lg == mg, lane_f, 1e9), axis=-1, keepdims=True)
    lane_grp = jnp.where((lane >= MOE_GROUPS) & (lane < MOE_GROUPS + N_EXPERTS),
                         lax.shift_right_arithmetic(lane - MOE_GROUPS, 3), -1).astype(F32)
    le = jnp.where(lane_grp == g_idx, logits, NEG)
    m1 = jnp.max(le, axis=-1, keepdims=True)
    e1 = jnp.min(jnp.where(le == m1, lane_f, 1e9), axis=-1, keepdims=True)
    le2 = jnp.where(lane_f == e1, NEG, le)
    m2 = jnp.max(le2, axis=-1, keepdims=True)
    e2 = jnp.min(jnp.where(le2 == m2, lane_f, 1e9), axis=-1, keepdims=True)
    ratio = jnp.exp(m2 - m1)
    w1 = g_prob / (1.0 + ratio)
    w2 = w1 * ratio
    xt_ref[:, D_MODEL:D_MODEL + LANE] = jnp.where(lane_f == e1, w1, 0.0) + jnp.where(lane_f == e2, w2, 0.0)

    tm = h.shape[0]
    onehot = jnp.where(lane_f == g_idx, 1.0, 0.0)
    earlier = jnp.where(lax.broadcasted_iota(jnp.int32, (tm, tm), 0) > lax.broadcasted_iota(jnp.int32, (tm, tm), 1),
                        1.0, 0.0).astype(BF16)
    prefix = _dot(earlier, onehot.astype(BF16)) + carry_ref[...]
    rank = jnp.sum(onehot * prefix, axis=-1, keepdims=True)
    carry_ref[...] += jnp.sum(onehot, axis=0, keepdims=True)
    route_ref[...] = jnp.where(lane == 0, g_idx, jnp.where(lane == 1, rank, 0.0))
    cnt_ref[...] = jnp.broadcast_to(carry_ref[...], cnt_ref.shape)


def _out_proj(x2, oa, ob, w, g, wr, br):
    T = x2.shape[0]
    tm = TM_PROJ
    row = lambda i: (i, 0)
    full = lambda a: pl.BlockSpec(a.shape, lambda i: (0,) * a.ndim)
    o4 = pl.BlockSpec((4, tm, LANE), lambda i: (0, i, 0))
    return pl.pallas_call(
        _outproj_kernel,
        grid=(T // tm,),
        in_specs=[pl.BlockSpec((tm, D_MODEL), row), o4, o4, full(w), full(g), full(wr), full(br)],
        out_specs=[pl.BlockSpec((tm, D_MODEL), row), pl.BlockSpec((tm, XT_WIDTH), row), pl.BlockSpec((tm, LANE), row),
                   pl.BlockSpec((8, LANE), lambda i: (0, 0))],
        out_shape=[jax.ShapeDtypeStruct((T, D_MODEL), F32), jax.ShapeDtypeStruct((T, XT_WIDTH), F32),
                   jax.ShapeDtypeStruct((T, LANE), F32), jax.ShapeDtypeStruct((8, LANE), F32)],
        scratch_shapes=[pltpu.VMEM((1, LANE), F32)],
        compiler_params=_cparams(("arbitrary",)),
        name="out_proj",
    )(x2, oa, ob, w, g, wr, br)


def _moe_kernel(tg_ref, ok_ref, idx_ref, idxn_ref, xt_hbm, wg_ref, wu_ref, wd_ref, y_ref, xbuf, sem):
    i = pl.program_id(0)
    slot = i % 2
    tm = xbuf.shape[1]

    def row_copy(index_ref, r, dst_slot):
        return pltpu.make_async_copy(xt_hbm.at[pl.ds(index_ref[0, 0, r], 1), :],
                                     xbuf.at[dst_slot, pl.ds(r, 1), :], sem.at[dst_slot])

    @pl.when(i == 0)
    def _():
        def body(r, c):
            row_copy(idx_ref, r, 0).start()
            return c
        lax.fori_loop(0, tm, body, 0)

    @pl.when((i == 0) | (ok_ref[jnp.maximum(i - 1, 0)] == 1))
    def _():
        pltpu.make_async_copy(xt_hbm.at[pl.ds(0, tm), :], xbuf.at[slot], sem.at[slot]).wait()

    @pl.when(ok_ref[i] == 0)
    def _():
        y_ref[...] = jnp.zeros(y_ref.shape, F32)

    @pl.when(ok_ref[i] == 1)
    def _():
        lane = lax.broadcasted_iota(jnp.int32, (1, LANE), 1)
        x = xbuf[slot, :, 0:D_MODEL].astype(BF16)
        cmb = xbuf[slot, :, D_MODEL:D_MODEL + LANE]
        first_lane = MOE_GROUPS + tg_ref[i] * EPG
        per_expert = tm // EPG
        y = jnp.zeros((tm, D_MODEL), F32)
        for e in range(EPG):
            a = _dot(x, wg_ref[0, e])
            b = _dot(x, wu_ref[0, e])
            ce = jnp.sum(jnp.where(lane == first_lane + e, cmb, 0.0), axis=-1, keepdims=True)
            y = y + _dot(((a * jax.nn.sigmoid(a)) * b * ce).astype(BF16), wd_ref[0, e])
            for r in range(e * per_expert, (e + 1) * per_expert):
                row_copy(idxn_ref, r, 1 - slot).start(priority=r % 2)
        for blk in range(ROW_SLABS):
            y_ref[:, blk, :] = y[:, blk * LANE:(blk + 1) * LANE]


def _moe(tile_group, tile_ok, src3, xt, wg, wu, wd):
    n_tiles, _, tm = src3.shape
    last = n_tiles - 1
    grid_spec = pltpu.PrefetchScalarGridSpec(
        num_scalar_prefetch=2,
        grid=(n_tiles,),
        in_specs=[pl.BlockSpec((1, 1, tm), lambda i, tg, ok: (i, 0, 0), memory_space=pltpu.SMEM),
                  pl.BlockSpec((1, 1, tm), lambda i, tg, ok: (jnp.minimum(i + 1, last), 0, 0), memory_space=pltpu.SMEM),
                  pl.BlockSpec(memory_space=pl.ANY),
                  pl.BlockSpec((1, EPG, D_MODEL, EXPERT_FF), lambda i, tg, ok: (tg[i], 0, 0, 0)),
                  pl.BlockSpec((1, EPG, D_MODEL, EXPERT_FF), lambda i, tg, ok: (tg[i], 0, 0, 0)),
                  pl.BlockSpec((1, EPG, EXPERT_FF, D_MODEL), lambda i, tg, ok: (tg[i], 0, 0, 0))],
        out_specs=pl.BlockSpec((tm, ROW_SLABS, LANE), lambda i, tg, ok: (i, 0, 0)),
        scratch_shapes=[pltpu.VMEM((2, tm, XT_WIDTH), F32), pltpu.SemaphoreType.DMA((2,))],
    )
    return pl.pallas_call(
        _moe_kernel,
        grid_spec=grid_spec,
        out_shape=jax.ShapeDtypeStruct((n_tiles * tm, ROW_SLABS, LANE), F32),
        compiler_params=pltpu.CompilerParams(dimension_semantics=("arbitrary",), vmem_limit_bytes=VMEM_LIMIT_MOE),
        name="moe",
    )(tile_group, tile_ok, src3, src3, xt, wg, wu, wd)


def _final_kernel(pos_ref, posn_ref, y_hbm, h_ref, gf_ref, o_ref, ybuf, sem):
    i = pl.program_id(0)
    slot = i % 2
    tm = h_ref.shape[0]

    def issue(index_ref, dst_slot):
        group = 8

        def body(j, c):
            for k in range(group):
                r = j * group + k
                pltpu.make_async_copy(y_hbm.at[index_ref[0, 0, r]], ybuf.at[dst_slot, pl.ds(r * ROW_SLABS, ROW_SLABS), :],
                                      sem.at[dst_slot]).start(priority=k % 2)
            return c
        lax.fori_loop(0, tm // group, body, 0)

    @pl.when(i == 0)
    def _():
        issue(pos_ref, 0)

    @pl.when(i + 1 < pl.num_programs(0))
    def _():
        issue(posn_ref, 1 - slot)

    pltpu.make_async_copy(ybuf.at[slot], ybuf.at[slot], sem.at[slot]).wait()
    y = jnp.concatenate([ybuf[slot, pl.ds(blk, tm, stride=ROW_SLABS), :] for blk in range(ROW_SLABS)], axis=-1)
    h = h_ref[...] + y
    o_ref[...] = h * lax.rsqrt(jnp.mean(h * h, axis=-1, keepdims=True) + EPS) * gf_ref[...]


def _final(pos3, y_sorted, h, gf):
    n_tiles, _, tm = pos3.shape
    last = n_tiles - 1
    row = lambda i: (i, 0)
    return pl.pallas_call(
        _final_kernel,
        grid=(n_tiles,),
        in_specs=[pl.BlockSpec((1, 1, tm), lambda i: (i, 0, 0), memory_space=pltpu.SMEM),
                  pl.BlockSpec((1, 1, tm), lambda i: (jnp.minimum(i + 1, last), 0, 0), memory_space=pltpu.SMEM),
                  pl.BlockSpec(memory_space=pl.ANY),
                  pl.BlockSpec((tm, D_MODEL), row),
                  pl.BlockSpec((1, D_MODEL), lambda i: (0, 0))],
        out_specs=pl.BlockSpec((tm, D_MODEL), row),
        out_shape=jax.ShapeDtypeStruct((n_tiles * tm, D_MODEL), F32),
        scratch_shapes=[pltpu.VMEM((2, tm * ROW_SLABS, LANE), F32), pltpu.SemaphoreType.DMA((2,))],
        compiler_params=_cparams(("arbitrary",)),
        name="final",
    )(pos3, pos3, y_sorted, h, gf)


def _qa_perm():
    new = np.arange(NSA_HEADS * NSA_HD)
    r, g, d = new // LANE, (new % LANE) // NSA_HD, new % NSA_HD
    return (g * NSA_R + r) * NSA_HD + d


def _block_diag2(w):
    z = jnp.zeros_like(w)
    return jnp.concatenate([jnp.concatenate([w, z], axis=-1), jnp.concatenate([z, w], axis=-1)], axis=-2)


def kernel(x, rel_bias, ln_mix, w_in, cmp_pos_k, cmp_pos_v, cmp_k_w1, cmp_k_w2, cmp_v_w1, cmp_v_w2,
           diff_lq1, diff_lk1, diff_lq2, diff_lk2, diff_subln, w_out, ln_ffn,
           router_group_w, router_group_b, router_expert_w, router_expert_b,
           exp_w_gate, exp_w_up, exp_w_down, ln_final):
    B, S, D = x.shape
    T = B * S
    assert D == D_MODEL and S % TQ == 0 and S >= WINDOW and T % TM_MOE == 0 and T % TM_FINAL == 0
    x2 = x.reshape(T, D)
    perm = _qa_perm()

    w = w_in[0]
    c_kc, c_vc, c_ks, c_vs, c_kw, c_vw, c_gt = 512, 640, 768, 896, 1024, 1152, 1280
    c_qb = c_gt + N_GATE
    c_kb, c_vb = c_qb + 512, c_qb + 1024
    col = lambda c, n=LANE: w[:, c:c + n]
    w_tok = jnp.concatenate([col(c_kc), col(c_vc), col(c_ks), col(c_kw), col(c_kb, 512)], axis=1).astype(BF16)
    w_feat = jnp.concatenate([w[:, perm], col(c_qb, 512), col(c_vs), col(c_vw), col(c_vb, 512), col(c_gt, N_GATE),
                              jnp.zeros((D, GATE_ROWS - N_GATE), F32)], axis=1).T.astype(BF16)
    qa, kc, vc, ks, kw, qb, kb, vst, vwt, vbt, gates_t = _in_proj(x2, ln_mix[0][None, :], w_tok, w_feat)

    w1k = _block_diag2(cmp_k_w1[0].reshape(CMP_LEN, NSA_HD, CMP_HIDDEN)).astype(BF16)
    w1v = _block_diag2(cmp_v_w1[0].reshape(CMP_LEN, NSA_HD, CMP_HIDDEN)).astype(BF16)
    w2k = _block_diag2(cmp_k_w2[0]).astype(BF16)
    w2v = _block_diag2(cmp_v_w2[0]).astype(BF16)
    posk = jnp.tile(cmp_pos_k[0], (1, NSA_G))
    posv = jnp.tile(cmp_pos_v[0], (1, NSA_G))
    kcmp, vcmpt = _compress(kc, vc, posk, posv, w1k, w1v, w2k, w2v, B, S)

    near, bias_c = _bias_tiles(rel_bias.T, S)

    n_sel = S // SEL_BLOCK
    nrow = S // CMP_STRIDE
    c_start = np.arange(nrow) * CMP_STRIDE
    s_start = np.arange(n_sel) * SEL_BLOCK
    ov = ((c_start[None, :] <= s_start[:, None] + SEL_BLOCK - 1)
          & (c_start[None, :] + CMP_LEN - 1 >= s_start[:, None])
          & (np.arange(nrow)[None, :] < nrow - 1)).astype(np.float32)
    o_a = _nsa(qa, gates_t, kcmp, vcmpt, bias_c, ks, vst, kw, vwt, near[:NSA_HEADS], jnp.asarray(ov, BF16), B, S)

    sub = jnp.tile(diff_subln[0], 2)[:, None]
    o_b = _diff(diff_lq1[0][None, :], diff_lk1[0][None, :], diff_lq2[0][None, :], diff_lk2[0][None, :],
                sub, qb, kb, vbt, near[NSA_HEADS:], B, S)

    w_o = jnp.concatenate([w_out[0][:512][perm], w_out[0][512:]], axis=0).astype(BF16)
    n_r = MOE_GROUPS + N_EXPERTS
    wr = jnp.concatenate([router_group_w[0], router_expert_w[0], jnp.zeros((D, LANE - n_r), F32)], axis=1).astype(BF16)
    br = jnp.concatenate([router_group_b[0], router_expert_b[0], jnp.zeros((LANE - n_r,), F32)])[None, :]
    h1, xt, route, counts = _out_proj(x2, o_a, o_b, w_o, ln_ffn[0][None, :], wr, br)

    tm = TM_MOE
    n_tiles = T // tm + MOE_GROUPS
    cnt = counts[0, :MOE_GROUPS].astype(jnp.int32)
    ends = jnp.cumsum((cnt + tm - 1) // tm * tm)
    starts = ends - (cnt + tm - 1) // tm * tm
    pos = starts[route[:, 0].astype(jnp.int32)] + route[:, 1].astype(jnp.int32)
    src = jnp.zeros((n_tiles * tm,), jnp.int32).at[pos].set(
        jnp.arange(T, dtype=jnp.int32), unique_indices=True, mode="promise_in_bounds")
    tile_start = jnp.arange(n_tiles, dtype=jnp.int32) * tm
    tile_group = jnp.minimum(jnp.searchsorted(ends, tile_start, side="right"), MOE_GROUPS - 1).astype(jnp.int32)
    tile_ok = (tile_start < ends[-1]).astype(jnp.int32)

    by_group = lambda a: a[0].astype(BF16).reshape((MOE_GROUPS, EPG) + a.shape[2:])
    y_sorted = _moe(tile_group, tile_ok, src.reshape(n_tiles, 1, tm), xt,
                    by_group(exp_w_gate), by_group(exp_w_up), by_group(exp_w_down))
    out = _final(pos.reshape(T // TM_FINAL, 1, TM_FINAL), y_sorted, h1, ln_final[None, :])
    return out.reshape(B, S, D)
```

```python
import math

import numpy as np
import jax
import jax.numpy as jnp
from jax import lax
from jax.experimental import pallas as pl
from jax.experimental.pallas import tpu as pltpu

F32 = jnp.float32
BF16 = jnp.bfloat16
NEG = -1e30
EPS = 1e-6
LOG2E = math.log2(math.e)

D_MODEL = 1024
LANE = 128
NSA_HEADS, NSA_G, NSA_R, NSA_HD = 8, 2, 4, 64
CMP_LEN, CMP_STRIDE, CMP_HIDDEN = 32, 16, 128
SEL_BLOCK, SEL_TOPN, SEL_FORCED_LOCAL, WINDOW = 64, 8, 2, 512
DIFF_HEADS, DIFF_HD = 8, 32
REL_BUCKETS, REL_MAX_EXACT, REL_MAX_DIST = 32, 16, 128
N_REL_HEADS = NSA_HEADS + DIFF_HEADS
MOE_GROUPS, EPG, N_EXPERTS, EXPERT_FF = 4, 8, 32, 256
LAMBDA_INIT = 0.8 - 0.6 * math.exp(-0.3 * 0)
N_GATE = NSA_HEADS * 3
GATE_ROWS = 32

TQ = 256
DIFF_PAIRS_PER_STEP = 4
TM_PROJ = 512
TM_MOE = 512
TM_FINAL = 512
XT_WIDTH = D_MODEL + LANE
ROW_SLABS = D_MODEL // LANE
VMEM_LIMIT = 48 * 1024 * 1024
VMEM_LIMIT_MOE = 56 * 1024 * 1024


def _cparams(sem):
    return pltpu.CompilerParams(dimension_semantics=sem, vmem_limit_bytes=VMEM_LIMIT)


def _dot(a, b):
    return jnp.dot(a, b, preferred_element_type=F32)


def _dot_nt(a, b):
    return lax.dot_general(a, b, (((1,), (1,)), ((), ())), preferred_element_type=F32)


def _bucket_thresholds():
    n = np.arange(0, REL_MAX_DIST + 1)
    nf = np.maximum(n, 1).astype(np.float32)
    large = REL_MAX_EXACT + (np.log(nf / np.float32(REL_MAX_EXACT)) / np.float32(math.log(REL_MAX_DIST / REL_MAX_EXACT))
                             * np.float32(REL_BUCKETS - REL_MAX_EXACT)).astype(np.int32)
    large = np.minimum(large, REL_BUCKETS - 1)
    bucket = np.where(n < REL_MAX_EXACT, n, large)
    return [int(np.argmax(bucket >= b)) for b in range(REL_BUCKETS)]


_THR = _bucket_thresholds()


def _inproj_kernel(x_ref, g_ref, w_ref, wt_ref, qa_ref, kc_ref, vc_ref, ks_ref, kw_ref, qb_ref, kb_ref,
                   vst_ref, vwt_ref, vbt_ref, gt_ref):
    x = x_ref[...]
    xn = (x * lax.rsqrt(jnp.mean(x * x, axis=-1, keepdims=True) + EPS) * g_ref[...]).astype(BF16)
    a = _dot(xn, w_ref[:, 0:512])
    kc_ref[...] = a[:, 0:128]
    vc_ref[...] = a[:, 128:256]
    ks_ref[...] = a[:, 256:384].astype(BF16)
    kw_ref[...] = a[:, 384:512].astype(BF16)
    a = _dot(xn, w_ref[:, 512:1024])
    for r in range(4):
        kb_ref[r] = a[:, r * LANE:(r + 1) * LANE].astype(BF16)

    ft = _dot_nt(wt_ref[...], xn)

    def feat(row0, rows=LANE):
        return ft[row0:row0 + rows, :]
    for r in range(4):
        qa_ref[r] = (feat(r * LANE) * (NSA_HD ** -0.5 * LOG2E)).astype(BF16)
        qb_ref[r] = (feat(512 + r * LANE) * (DIFF_HD ** -0.5 * LOG2E)).astype(BF16)
        vbt_ref[r] = feat(1280 + r * LANE).astype(BF16)
    vst_ref[...] = feat(1024).astype(BF16)
    vwt_ref[...] = feat(1152).astype(BF16)
    gt_ref[...] = feat(1792, GATE_ROWS)


def _in_proj(x2, g, w, wt):
    T = x2.shape[0]
    tm = TM_PROJ
    row = lambda i: (i, 0)
    o128b = jax.ShapeDtypeStruct((T, LANE), BF16)
    o128f = jax.ShapeDtypeStruct((T, LANE), F32)
    o4 = jax.ShapeDtypeStruct((4, T, LANE), BF16)
    ot = jax.ShapeDtypeStruct((LANE, T), BF16)
    o4t = jax.ShapeDtypeStruct((4, LANE, T), BF16)
    s128 = pl.BlockSpec((tm, LANE), row)
    s4 = pl.BlockSpec((4, tm, LANE), lambda i: (0, i, 0))
    st = pl.BlockSpec((LANE, tm), lambda i: (0, i))
    s4t = pl.BlockSpec((4, LANE, tm), lambda i: (0, 0, i))
    return pl.pallas_call(
        _inproj_kernel,
        grid=(T // tm,),
        in_specs=[pl.BlockSpec((tm, D_MODEL), row),
                  pl.BlockSpec((1, D_MODEL), lambda i: (0, 0)),
                  pl.BlockSpec(w.shape, lambda i: (0, 0)),
                  pl.BlockSpec(wt.shape, lambda i: (0, 0))],
        out_specs=[s4t, s128, s128, s128, s128, s4t, s4, st, st, s4t, pl.BlockSpec((GATE_ROWS, tm), lambda i: (0, i))],
        out_shape=[o4t, o128f, o128f, o128b, o128b, o4t, o4, ot, ot, o4t, jax.ShapeDtypeStruct((GATE_ROWS, T), F32)],
        compiler_params=_cparams(("parallel",)),
        name="in_proj",
    )(x2, g, w, wt)


def _gelu_tanh(x):
    return 0.5 * x * (1.0 + jnp.tanh(math.sqrt(2.0 / math.pi) * (x + 0.044715 * (x * x * x))))


def _compress_kernel(kc_ref, vc_ref, posk_ref, posv_ref, w1k_ref, w1v_ref, w2k_ref, w2v_ref, ko_ref, vo_ref):
    nrow = kc_ref.shape[0] // CMP_STRIDE
    rid = lax.broadcasted_iota(jnp.int32, (nrow, 1), 0)
    cid = lax.broadcasted_iota(jnp.int32, (1, nrow), 1)
    for src, pos, w1, w2, out, transposed in ((kc_ref, posk_ref, w1k_ref, w2k_ref, ko_ref, False),
                                              (vc_ref, posv_ref, w1v_ref, w2v_ref, vo_ref, True)):
        hid_a = jnp.zeros((nrow, 2 * CMP_HIDDEN), F32)
        hid_b = jnp.zeros((nrow, 2 * CMP_HIDDEN), F32)
        for m in range(CMP_STRIDE):
            y = src[pl.ds(m, nrow, stride=CMP_STRIDE), :]
            hid_a = hid_a + _dot((y + pos[m:m + 1, :]).astype(BF16), w1[m])
            hid_b = hid_b + _dot((y + pos[CMP_STRIDE + m:CMP_STRIDE + m + 1, :]).astype(BF16), w1[CMP_STRIDE + m])
        hid = hid_a + pltpu.roll(hid_b, nrow - 1, 0)
        o = _dot(_gelu_tanh(hid).astype(BF16), w2[...])
        if transposed:
            out[0] = jnp.where(cid < nrow - 1, o.T, 0.0).astype(BF16)
        else:
            out[0] = jnp.where(rid < nrow - 1, o, 0.0).astype(BF16)


def _compress(kc, vc, posk, posv, w1k, w1v, w2k, w2v, B, S):
    nrow = S // CMP_STRIDE
    assert nrow == LANE
    full = lambda a: pl.BlockSpec(a.shape, lambda b: (0,) * a.ndim)
    src = pl.BlockSpec((S, LANE), lambda b: (b, 0))
    osp = pl.BlockSpec((1, nrow, LANE), lambda b: (b, 0, 0))
    osh = jax.ShapeDtypeStruct((B, nrow, LANE), BF16)
    return pl.pallas_call(
        _compress_kernel,
        grid=(B,),
        in_specs=[src, src, full(posk), full(posv), full(w1k), full(w1v), full(w2k), full(w2v)],
        out_specs=[osp, osp],
        out_shape=[osh, osh],
        compiler_params=_cparams(("parallel",)),
        name="compress",
    )(kc, vc, posk, posv, w1k, w1v, w2k, w2v)


def _bias_from_dist(dist, tab_ref, h):
    val = jnp.full(dist.shape, tab_ref[h, 0], F32)
    for b in range(1, REL_BUCKETS):
        val = jnp.where(dist >= _THR[b], tab_ref[h, b], val)
    return val


def _bias_near_kernel(tab_ref, out_ref):
    h = pl.program_id(0)
    j = lax.broadcasted_iota(jnp.int32, (TQ, TQ), 0)
    i = lax.broadcasted_iota(jnp.int32, (TQ, TQ), 1)
    for d in range(2):
        bias = (_bias_from_dist(i - j + d * TQ, tab_ref, h) - tab_ref[h, REL_BUCKETS - 1]) * LOG2E
        out_ref[0, d] = jnp.where(i - j + d * TQ >= 0, bias, NEG)


def _bias_cmp_kernel(tab_ref, out_ref):
    h = pl.program_id(0)
    c = lax.broadcasted_iota(jnp.int32, (LANE, TQ), 0)
    t = pl.program_id(1) * TQ + lax.broadcasted_iota(jnp.int32, (LANE, TQ), 1)
    out_ref[0] = _bias_from_dist(t - (c * CMP_STRIDE + CMP_LEN - 1), tab_ref, h) * LOG2E


def _bias_tiles(tab_t, S):
    smem = pl.BlockSpec(memory_space=pltpu.SMEM)
    near = pl.pallas_call(
        _bias_near_kernel,
        grid=(N_REL_HEADS,),
        in_specs=[smem],
        out_specs=pl.BlockSpec((1, 2, TQ, TQ), lambda h: (h, 0, 0, 0)),
        out_shape=jax.ShapeDtypeStruct((N_REL_HEADS, 2, TQ, TQ), F32),
        compiler_params=_cparams(("parallel",)),
        name="bias_near",
    )(tab_t)
    cmp_bias = pl.pallas_call(
        _bias_cmp_kernel,
        grid=(NSA_HEADS, S // TQ),
        in_specs=[smem],
        out_specs=pl.BlockSpec((1, LANE, TQ), lambda h, q: (h, 0, q)),
        out_shape=jax.ShapeDtypeStruct((NSA_HEADS, LANE, S), F32),
        compiler_params=_cparams(("parallel", "parallel")),
        name="bias_cmp",
    )(tab_t)
    return near, cmp_bias


ACC_ROWS = NSA_HD + 16


def _with_ones(vt):
    return jnp.concatenate([vt, jnp.ones((ACC_ROWS - vt.shape[0], vt.shape[1]), BF16)], axis=0)


def _probs(s, m):
    return jnp.exp2((s - m).astype(BF16))


def _flash_first(ss, vts):
    ms = [jnp.max(s, axis=0, keepdims=True) for s in ss]
    ps = [_probs(s, m) for s, m in zip(ss, ms)]
    return tuple((m, _dot(vt, p)) for m, p, vt in zip(ms, ps, vts))


def _flash_update(ss, vts, sts):
    ms = [jnp.maximum(st[0], jnp.max(s, axis=0, keepdims=True)) for s, st in zip(ss, sts)]
    alphas = [jnp.exp2(st[0] - m) for m, st in zip(ms, sts)]
    ps = [_probs(s, m) for s, m in zip(ss, ms)]
    return tuple((m, a * st[1] + _dot(vt, p)) for m, a, st, p, vt in zip(ms, alphas, sts, ps, vts))


def _flash_out(st):
    acc = st[1]
    return acc[0:NSA_HD, :] * (1.0 / acc[NSA_HD:NSA_HD + 1, :])


def _stream_scratch(n):
    return ([pltpu.VMEM((8, TQ), F32), pltpu.VMEM((ACC_ROWS, TQ), F32)] * n
            + [pltpu.VMEM((TQ, TQ), F32)] * (2 * n) + [pltpu.VMEM((TQ, TQ), BF16)] * (2 * n))


def _stream_refs(refs, n):
    st_refs = tuple((refs[2 * c], refs[2 * c + 1]) for c in range(n))
    s, p = refs[2 * n:4 * n], refs[4 * n:6 * n]
    return st_refs, tuple(s[:n]), tuple(s[n:]), tuple(p[:n]), tuple(p[n:])


def _causal_stream(qi, scores, vtile, st_refs, s_a, s_b, p_a, p_b, diag_ready=False):
    n_far = jnp.maximum(qi - 1, 0)
    top = qi - 2

    chains = range(len(st_refs))

    def write(s_refs, kt, near=2):
        for c in chains:
            s_refs[c][...] = scores(jnp.maximum(kt, 0), near, c)

    def pending(p_refs, kt):
        return [_dot(vtile(jnp.maximum(kt, 0), c), p_refs[c][...]) for c in chains]

    def step(s_cur, p_cur, s_next, kt_next, p_prev, kt_prev, anchor=None):
        rescale = []
        for c in chains:
            nxt = scores(jnp.maximum(kt_next, 0), 2, c)
            s_next[c][...] = nxt
            pv = _dot(vtile(jnp.maximum(kt_prev, 0), c), p_prev[c][...])
            st = st_refs[c][0]
            s = s_cur[c][...]
            m_old = st[0:1, :]
            m = jnp.maximum(m_old, jnp.max(s, axis=0, keepdims=True))
            if anchor is not None:
                m = m + anchor
            anchor = nxt[0:1, :] * 0.0
            p_cur[c][...] = _probs(s, m)
            st[0:1, :] = m
            rescale.append((jnp.exp2(m_old - m), pv))
        for c in chains:
            acc = st_refs[c][1]
            acc[...] = rescale[c][0] * (acc[...] + rescale[c][1])
        return anchor

    if not diag_ready:
        write(s_a, qi, 0)
    for c in chains:
        s_b[c][...] = scores(jnp.maximum(qi - 1, 0), 1, c)
        st, acc = st_refs[c]
        s = s_a[c][...]
        m = jnp.max(s, axis=0, keepdims=True)
        p_a[c][...] = _probs(s, m)
        st[0:1, :] = m
        acc[...] = jnp.zeros(acc.shape, F32)
    step(s_b, p_b, s_a, top, p_a, qi)

    def pair(j, c):
        kt = top - 2 * j
        anchor = step(s_a, p_a, s_b, kt - 1, p_b, kt + 1)
        step(s_b, p_b, s_a, kt - 2, p_a, kt, anchor)
        return c
    lax.fori_loop(0, n_far // 2, pair, 0)

    @pl.when(n_far % 2 == 1)
    def _():
        for c, pv in zip(chains, pending(p_b, 1)):
            st, acc = st_refs[c]
            s = s_a[c][...]
            m_old = st[0:1, :]
            m = jnp.maximum(m_old, jnp.max(s, axis=0, keepdims=True))
            st[0:1, :] = m
            acc[...] = jnp.exp2(m_old - m) * (acc[...] + pv) + _dot(vtile(0, c), _probs(s, m))

    @pl.when(n_far % 2 == 0)
    def _():
        last = jnp.where(n_far == 0, qi - 1, 0)
        for c, pv in zip(chains, pending(p_b, last)):
            acc = st_refs[c][1]
            acc[...] = acc[...] + pv
    return tuple((st[0:1, :], acc[...]) for st, acc in st_refs)


def _ktile(ref, kt):
    return ref[pl.ds(pl.multiple_of(kt * TQ, TQ), TQ), :]


def _vtile(ref, kt):
    return ref[:, pl.ds(pl.multiple_of(kt * TQ, TQ), TQ)]


def _nsa_kernel(qa_ref, gt_ref, kcmp_ref, vcmpt_ref, bc_ref, ks_ref, vst_ref, kw_ref, vwt_ref, dn_ref,
                ov_ref, o_ref, psum_ref, sel_ref, oacc_ref, sig_ref, *scratch):
    st_refs, s_a, s_b, p_a, p_b = _stream_refs(scratch, NSA_HEADS)
    qi = pl.program_id(1)
    t0 = qi * TQ
    sub_grp = lax.shift_right_arithmetic(lax.broadcasted_iota(jnp.int32, (LANE, 1), 0), 6)
    sig_ref[...] = jax.nn.sigmoid(gt_ref[...])

    def gate_row(c):
        return sig_ref[pl.ds(c, 1), :]

    def masked_q(r, g):
        return jnp.where(sub_grp == g, qa_ref[r].astype(F32), 0.0).astype(BF16)

    n_cmp = kcmp_ref.shape[1] - 1
    crow = lax.broadcasted_iota(jnp.int32, (LANE, 1), 0)
    cmp_end = jnp.where(crow < n_cmp, crow * CMP_STRIDE + CMP_LEN - 1, 1 << 30)
    mask_c = (t0 + lax.broadcasted_iota(jnp.int32, (1, TQ), 1)) >= cmp_end

    heads = [(r, g) for r in range(NSA_R) for g in range(NSA_G)]
    scores_c = [_dot(kcmp_ref[0], masked_q(r, g)) for r, g in heads]
    probs_c = []
    for (r, g), s in zip(heads, scores_c):
        s = jnp.where(mask_c, s + bc_ref[g * NSA_R + r], NEG)
        p = jnp.where(mask_c, jnp.exp2(s - jnp.max(s, axis=0, keepdims=True)), 0.0)
        l = jnp.sum(p, axis=0, keepdims=True)
        probs_c.append(p * jnp.where(l > 0.0, 1.0 / l, 0.0))
    for g in range(NSA_G):
        psum_ref[g] = sum(p for (r, gg), p in zip(heads, probs_c) if gg == g)
    outs_c = [_dot(vcmpt_ref[0], p.astype(BF16)) for p in probs_c]
    for r in range(NSA_R):
        o0, o1 = outs_c[r * NSA_G], outs_c[r * NSA_G + 1]
        oacc_ref[r] = jnp.where(sub_grp == 0, gate_row(r * 3) * o0, gate_row((NSA_R + r) * 3) * o1)

    has_prev = jnp.where(qi >= 1, 0.0, NEG)
    has_wfar = jnp.where(qi >= WINDOW // TQ, 0.0, NEG)
    kt_prev = jnp.maximum(qi - 1, 0)
    kt_wfar = jnp.maximum(qi - WINDOW // TQ, 0)
    wfar_mask = lax.broadcasted_iota(jnp.int32, (TQ, TQ), 0) > lax.broadcasted_iota(jnp.int32, (TQ, TQ), 1)

    def attend_all_heads():
        chains = [(r, g) for r in range(NSA_R) for g in range(NSA_G)]
        qms = [masked_q(r, g) for r, g in chains]
        hids = [g * NSA_R + r for r, g in chains]

        per_tile = TQ // SEL_BLOCK
        key_blk = lax.shift_right_arithmetic(lax.broadcasted_iota(jnp.int32, (TQ, LANE), 0), 6)
        lane_id = lax.broadcasted_iota(jnp.int32, (TQ, LANE), 1)

        def keys_with_block_onehot(kt, g):
            other = (1 - g) * NSA_HD
            onehot = jnp.where(lane_id - other == key_blk, 1.0, 0.0).astype(BF16)
            in_other = (lane_id >= other) & (lane_id < other + NSA_HD)
            return jnp.where(in_other, onehot, _ktile(ks_ref, kt))

        def query_with_mask_rows(kt, c):
            r, g = chains[c]
            own = qa_ref[r, g * NSA_HD:(g + 1) * NSA_HD, :]
            pair = sel_ref[g, pl.ds(pl.multiple_of((kt // 2) * 2 * per_tile, 2 * per_tile), 2 * per_tile), :]
            rows = jnp.where(kt % 2 == 0, pair[0:per_tile], pair[per_tile:2 * per_tile]).astype(BF16)
            other = jnp.concatenate([rows, jnp.zeros((NSA_HD - per_tile, TQ), BF16)], axis=0)
            return jnp.concatenate([own, other] if g == 0 else [other, own], axis=0)

        def near_bias(s, near, c):
            if near == 0:
                return s + dn_ref[hids[c], 0]
            if near == 1:
                return s + dn_ref[hids[c], 1] + has_prev
            return s

        def group_values(ref, kt, c):
            g = chains[c][1]
            return _with_ones(ref[g * NSA_HD:(g + 1) * NSA_HD, pl.ds(pl.multiple_of(kt * TQ, TQ), TQ)])

        def finish(sts, branch):
            for c, (r, g) in enumerate(chains):
                oacc_ref[r, g * NSA_HD:(g + 1) * NSA_HD, :] += gate_row(hids[c] * 3 + branch) * _flash_out(sts[c])

        def slc_scores(kt, near, c):
            return near_bias(_dot(keys_with_block_onehot(kt, chains[c][1]), query_with_mask_rows(kt, c)), near, c)

        def slc_vtile(kt, c):
            return group_values(vst_ref, kt, c)

        win_tiles = ((qi, 0), (kt_prev, 1), (kt_wfar, 2))

        def win_scores(c):
            out = []
            for kt, near in win_tiles:
                s = near_bias(_dot(_ktile(kw_ref, kt), qms[c]), near, c)
                out.append(jnp.where(wfar_mask, s + has_wfar, NEG) if near == 2 else s)
            return out

        sts = []
        ahead = win_scores(0)
        for c in range(len(chains)):
            tiles, ahead = ahead, (win_scores(c + 1) if c + 1 < len(chains) else None)
            st = _flash_first([tiles[0]], [group_values(vwt_ref, qi, c)])
            st = _flash_update([tiles[1]], [group_values(vwt_ref, kt_prev, c)], st)
            sts.append(_flash_update([tiles[2]], [group_values(vwt_ref, kt_wfar, c)], st)[0])
        finish(sts, 2)

        n_sel = ov_ref.shape[0]
        jj = lax.broadcasted_iota(jnp.int32, (n_sel, TQ), 0)
        cur = lax.shift_right_arithmetic(t0 + lax.broadcasted_iota(jnp.int32, (n_sel, TQ), 1), 6)
        valid = jj <= cur
        forced = (jj == 0) | (cur - jj < SEL_FORCED_LOCAL)
        for g in range(NSA_G):
            ps = psum_ref[g]
            hi = ps.astype(BF16)
            rem = ps - hi.astype(F32)
            mid = rem.astype(BF16)
            lo = (rem - mid.astype(F32)).astype(BF16)
            ov = ov_ref[...]
            imp = _dot(ov, hi) + _dot(ov, mid) + _dot(ov, lo)
            score = jnp.where(valid, jnp.where(forced, 1e9, imp), -1e9)
            cnt = jnp.zeros((n_sel, TQ), F32)
            for j2 in range(n_sel):
                row = score[j2:j2 + 1, :]
                tie = jnp.where(jj > j2, 1.0, 0.0)
                cnt = cnt + jnp.where(row > score, 1.0, jnp.where(row == score, tie, 0.0))
            sel_ref[g] = jnp.where(cnt < float(min(SEL_TOPN, n_sel)), jnp.where(score > -1e8, 0.0, NEG), NEG)

        finish(_causal_stream(qi, slc_scores, slc_vtile, st_refs, s_a, s_b, p_a, p_b), 1)
    attend_all_heads()

    for r in range(NSA_R):
        o_ref[r] = oacc_ref[r].T.astype(BF16)


def _nsa(qa, gates_t, kcmp, vcmpt, bias_c, ks, vst, kw, vwt, near_a, ov, B, S):
    nq = S // TQ
    T = B * S
    k_spec = pl.BlockSpec((S, LANE), lambda b, q: (b, 0))
    v_spec = pl.BlockSpec((LANE, S), lambda b, q: (0, b))
    full = lambda a: pl.BlockSpec(a.shape, lambda b, q: (0,) * a.ndim)
    cmp_spec = pl.BlockSpec((1, LANE, LANE), lambda b, q: (b, 0, 0))
    qo = pl.BlockSpec((4, TQ, LANE), lambda b, q: (0, b * nq + q, 0))
    return pl.pallas_call(
        _nsa_kernel,
        grid=(B, nq),
        in_specs=[pl.BlockSpec((4, LANE, TQ), lambda b, q: (0, 0, b * nq + q)),
                  pl.BlockSpec((GATE_ROWS, TQ), lambda b, q: (0, b * nq + q)),
                  cmp_spec, cmp_spec,
                  pl.BlockSpec((NSA_HEADS, LANE, TQ), lambda b, q: (0, 0, q)),
                  k_spec, v_spec, k_spec, v_spec, full(near_a), full(ov)],
        out_specs=qo,
        out_shape=jax.ShapeDtypeStruct((4, T, LANE), BF16),
        scratch_shapes=[pltpu.VMEM((NSA_G, LANE, TQ), F32), pltpu.VMEM((NSA_G, S // SEL_BLOCK, TQ), F32),
                        pltpu.VMEM((NSA_R, LANE, TQ), F32), pltpu.VMEM((GATE_ROWS, TQ), F32)]
                       + _stream_scratch(NSA_HEADS),
        compiler_params=_cparams(("parallel", "arbitrary")),
        name="nsa",
    )(qa, gates_t, kcmp, vcmpt, bias_c, ks, vst, kw, vwt, near_a, ov)


def _diff_kernel(lq1_ref, lk1_ref, lq2_ref, lk2_ref, sub_ref, qb_ref, kb_ref, vbt_ref, dn_ref, o_ref, *scratch):
    n = DIFF_PAIRS_PER_STEP * 4
    st_refs, s_a, s_b, p_a, p_b = _stream_refs(scratch, n)
    qi = pl.program_id(1)
    chain_of_row = lax.shift_right_arithmetic(lax.broadcasted_iota(jnp.int32, (LANE, 1), 0), 5)
    lam = (jnp.exp(jnp.sum(lq1_ref[...] * lk1_ref[...], axis=-1, keepdims=True))
           - jnp.exp(jnp.sum(lq2_ref[...] * lk2_ref[...], axis=-1, keepdims=True)) + LAMBDA_INIT)
    has_prev = jnp.where(qi >= 1, 0.0, NEG)
    n_steps = DIFF_HEADS // 2 // DIFF_PAIRS_PER_STEP

    def score_fn(step):
        pairs = [step * DIFF_PAIRS_PER_STEP + c // 4 for c in range(n)]
        qs = [qb_ref[step * DIFF_PAIRS_PER_STEP + i].astype(F32) for i in range(DIFF_PAIRS_PER_STEP)]
        qms = [jnp.where(chain_of_row == c % 4, qs[c // 4], 0.0).astype(BF16) for c in range(n)]

        def scores(kt, near, c):
            s = _dot(kb_ref[pairs[c], pl.ds(pl.multiple_of(kt * TQ, TQ), TQ), :], qms[c])
            if near == 2:
                return s
            bias = dn_ref[2 * pairs[c] + (c % 4) // 2, near]
            return s + bias if near == 0 else s + bias + has_prev
        return scores

    def start_diag(step):
        scores = score_fn(step)
        for c in range(n):
            s_a[c][...] = scores(qi, 0, c)

    start_diag(0)

    def step_body(step, carry):
        def vtile(kt, c):
            hh = (c % 4) // 2
            return _with_ones(vbt_ref[step * DIFF_PAIRS_PER_STEP + c // 4, hh * 2 * DIFF_HD:(hh + 1) * 2 * DIFF_HD,
                                      pl.ds(pl.multiple_of(kt * TQ, TQ), TQ)])

        sts = _causal_stream(qi, score_fn(step), vtile, st_refs, s_a, s_b, p_a, p_b, diag_ready=True)
        start_diag(jnp.minimum(step + 1, n_steps - 1))
        for i in range(DIFF_PAIRS_PER_STEP):
            outs = []
            for hh in range(2):
                c = 4 * i + 2 * hh
                o = _flash_out(sts[c]) - lam * _flash_out(sts[c + 1])
                outs.append(o * lax.rsqrt(jnp.mean(o * o, axis=0, keepdims=True) + EPS))
            out = jnp.concatenate(outs, axis=0) * sub_ref[...] * (1.0 - LAMBDA_INIT)
            o_ref[step * DIFF_PAIRS_PER_STEP + i] = out.T.astype(BF16)
        return carry
    lax.fori_loop(0, n_steps, step_body, 0)


def _diff(lq1, lk1, lq2, lk2, sub, qb, kb, vbt, near_b, B, S):
    nq = S // TQ
    T = B * S
    full = lambda a: pl.BlockSpec(a.shape, lambda b, q: (0,) * a.ndim)
    k_spec = pl.BlockSpec((4, S, LANE), lambda b, q: (0, b, 0))
    v_spec = pl.BlockSpec((4, LANE, S), lambda b, q: (0, 0, b))
    qo = pl.BlockSpec((4, TQ, LANE), lambda b, q: (0, b * nq + q, 0))
    return pl.pallas_call(
        _diff_kernel,
        grid=(B, nq),
        in_specs=[full(lq1), full(lk1), full(lq2), full(lk2), full(sub),
                  pl.BlockSpec((4, LANE, TQ), lambda b, q: (0, 0, b * nq + q)), k_spec, v_spec, full(near_b)],
        out_specs=qo,
        out_shape=jax.ShapeDtypeStruct((4, T, LANE), BF16),
        scratch_shapes=_stream_scratch(DIFF_PAIRS_PER_STEP * 4),
        compiler_params=_cparams(("parallel", "arbitrary")),
        name="diff",
    )(lq1, lk1, lq2, lk2, sub, qb, kb, vbt, near_b)


def _outproj_kernel(x_ref, oa_ref, ob_ref, w_ref, g_ref, wr_ref, br_ref, h_ref, xt_ref, route_ref, cnt_ref,
                    carry_ref):
    @pl.when(pl.program_id(0) == 0)
    def _():
        carry_ref[...] = jnp.zeros(carry_ref.shape, F32)

    o = jnp.concatenate([oa_ref[r] for r in range(4)] + [ob_ref[r] for r in range(4)], axis=-1)
    h = x_ref[...] + _dot(o, w_ref[...])
    h_ref[...] = h
    tn32 = h * lax.rsqrt(jnp.mean(h * h, axis=-1, keepdims=True) + EPS) * g_ref[...]
    xt_ref[:, 0:D_MODEL] = tn32
    tn = tn32.astype(BF16)
    logits = _dot(tn, wr_ref[...]) + br_ref[...]
    lane = lax.broadcasted_iota(jnp.int32, (1, LANE), 1)
    lane_f = lane.astype(F32)
    is_grp = lane < MOE_GROUPS
    lg = jnp.where(is_grp, logits, NEG)
    mg = jnp.max(lg, axis=-1, keepdims=True)
    zg = jnp.sum(jnp.where(is_grp, jnp.exp(lg - mg), 0.0), axis=-1, keepdims=True)
    g_prob = 1.0 / zg
    g_idx = jnp.min(jnp.where(lg == mg, lane_f, 1e9), axis=-1, keepdims=True)
    lane_grp = jnp.where((lane >= MOE_GROUPS) & (lane < MOE_GROUPS + N_EXPERTS),
                         lax.shift_right_arithmetic(lane - MOE_GROUPS, 3), -1).astype(F32)
    le = jnp.where(lane_grp == g_idx, logits, NEG)
    m1 = jnp.max(le, axis=-1, keepdims=True)
    e1 = jnp.min(jnp.where(le == m1, lane_f, 1e9), axis=-1, keepdims=True)
    le2 = jnp.where(lane_f == e1, NEG, le)
    m2 = jnp.max(le2, axis=-1, keepdims=True)
    e2 = jnp.min(jnp.where(le2 == m2, lane_f, 1e9), axis=-1, keepdims=True)
    ratio = jnp.exp(m2 - m1)
    w1 = g_prob / (1.0 + ratio)
    w2 = w1 * ratio
    xt_ref[:, D_MODEL:D_MODEL + LANE] = jnp.where(lane_f == e1, w1, 0.0) + jnp.where(lane_f == e2, w2, 0.0)

    tm = h.shape[0]
    onehot = jnp.where(lane_f == g_idx, 1.0, 0.0)
    earlier = jnp.where(lax.broadcasted_iota(jnp.int32, (tm, tm), 0) > lax.broadcasted_iota(jnp.int32, (tm, tm), 1),
                        1.0, 0.0).astype(BF16)
    prefix = _dot(earlier, onehot.astype(BF16)) + carry_ref[...]
    rank = jnp.sum(onehot * prefix, axis=-1, keepdims=True)
    carry_ref[...] += jnp.sum(onehot, axis=0, keepdims=True)
    route_ref[...] = jnp.where(lane == 0, g_idx, jnp.where(lane == 1, rank, 0.0))
    cnt_ref[...] = jnp.broadcast_to(carry_ref[...], cnt_ref.shape)


def _out_proj(x2, oa, ob, w, g, wr, br):
    T = x2.shape[0]
    tm = TM_PROJ
    row = lambda i: (i, 0)
    full = lambda a: pl.BlockSpec(a.shape, lambda i: (0,) * a.ndim)
    o4 = pl.BlockSpec((4, tm, LANE), lambda i: (0, i, 0))
    return pl.pallas_call(
        _outproj_kernel,
        grid=(T // tm,),
        in_specs=[pl.BlockSpec((tm, D_MODEL), row), o4, o4, full(w), full(g), full(wr), full(br)],
        out_specs=[pl.BlockSpec((tm, D_MODEL), row), pl.BlockSpec((tm, XT_WIDTH), row), pl.BlockSpec((tm, LANE), row),
                   pl.BlockSpec((8, LANE), lambda i: (0, 0))],
        out_shape=[jax.ShapeDtypeStruct((T, D_MODEL), F32), jax.ShapeDtypeStruct((T, XT_WIDTH), F32),
                   jax.ShapeDtypeStruct((T, LANE), F32), jax.ShapeDtypeStruct((8, LANE), F32)],
        scratch_shapes=[pltpu.VMEM((1, LANE), F32)],
        compiler_params=_cparams(("arbitrary",)),
        name="out_proj",
    )(x2, oa, ob, w, g, wr, br)


def _moe_kernel(tg_ref, ok_ref, idx_ref, idxn_ref, xt_hbm, wg_ref, wu_ref, wd_ref, y_ref, xbuf, sem,
                xb_ref, cmb_ref):
    i = pl.program_id(0)
    slot = i % 2
    tm = xbuf.shape[1]

    def row_copy(index_ref, r, dst_slot):
        return pltpu.make_async_copy(xt_hbm.at[pl.ds(index_ref[0, 0, r], 1), :],
                                     xbuf.at[dst_slot, pl.ds(r, 1), :], sem.at[dst_slot])

    @pl.when(i == 0)
    def _():
        def body(r, c):
            row_copy(idx_ref, r, 0).start()
            return c
        lax.fori_loop(0, tm, body, 0)

    @pl.when((i == 0) | (ok_ref[jnp.maximum(i - 1, 0)] == 1))
    def _():
        pltpu.make_async_copy(xt_hbm.at[pl.ds(0, tm), :], xbuf.at[slot], sem.at[slot]).wait()

    @pl.when(ok_ref[i] == 0)
    def _():
        y_ref[...] = jnp.zeros(y_ref.shape, F32)

    @pl.when(ok_ref[i] == 1)
    def _():
        lane = lax.broadcasted_iota(jnp.int32, (1, LANE), 1)
        xb_ref[...] = xbuf[slot, :, 0:D_MODEL].astype(BF16)
        cmb_ref[...] = xbuf[slot, :, D_MODEL:D_MODEL + LANE]
        first_lane = MOE_GROUPS + tg_ref[i] * EPG
        per_expert = tm // EPG
        y = jnp.zeros((tm, D_MODEL), F32)
        for e in range(EPG):
            x = xb_ref[...]
            a = _dot(x, wg_ref[0, e])
            b = _dot(x, wu_ref[0, e])
            ce = jnp.sum(jnp.where(lane == first_lane + e, cmb_ref[...], 0.0), axis=-1, keepdims=True)
            y = y + _dot(((a * jax.nn.sigmoid(a)) * b * ce).astype(BF16), wd_ref[0, e])
            for r in range(e * per_expert, (e + 1) * per_expert):
                row_copy(idxn_ref, r, 1 - slot).start(priority=r % 2)
        for blk in range(ROW_SLABS):
            y_ref[pl.ds(blk, tm, stride=ROW_SLABS), :] = y[:, blk * LANE:(blk + 1) * LANE]


def _moe(tile_group, tile_ok, src3, xt, wg, wu, wd):
    n_tiles, _, tm = src3.shape
    last = n_tiles - 1
    grid_spec = pltpu.PrefetchScalarGridSpec(
        num_scalar_prefetch=2,
        grid=(n_tiles,),
        in_specs=[pl.BlockSpec((1, 1, tm), lambda i, tg, ok: (i, 0, 0), memory_space=pltpu.SMEM),
                  pl.BlockSpec((1, 1, tm), lambda i, tg, ok: (jnp.minimum(i + 1, last), 0, 0), memory_space=pltpu.SMEM),
                  pl.BlockSpec(memory_space=pl.ANY),
                  pl.BlockSpec((1, EPG, D_MODEL, EXPERT_FF), lambda i, tg, ok: (tg[i], 0, 0, 0)),
                  pl.BlockSpec((1, EPG, D_MODEL, EXPERT_FF), lambda i, tg, ok: (tg[i], 0, 0, 0)),
                  pl.BlockSpec((1, EPG, EXPERT_FF, D_MODEL), lambda i, tg, ok: (tg[i], 0, 0, 0))],
        out_specs=pl.BlockSpec((tm * ROW_SLABS, LANE), lambda i, tg, ok: (i, 0)),
        scratch_shapes=[pltpu.VMEM((2, tm, XT_WIDTH), F32), pltpu.SemaphoreType.DMA((2,)),
                        pltpu.VMEM((tm, D_MODEL), BF16), pltpu.VMEM((tm, LANE), F32)],
    )
    return pl.pallas_call(
        _moe_kernel,
        grid_spec=grid_spec,
        out_shape=jax.ShapeDtypeStruct((n_tiles * tm * ROW_SLABS, LANE), F32),
        compiler_params=pltpu.CompilerParams(dimension_semantics=("arbitrary",), vmem_limit_bytes=VMEM_LIMIT_MOE),
        name="moe",
    )(tile_group, tile_ok, src3, src3, xt, wg, wu, wd)


def _final_kernel(pos_ref, posn_ref, y_hbm, h_ref, gf_ref, o_ref, ybuf, sem):
    i = pl.program_id(0)
    slot = i % 2
    tm = h_ref.shape[0]

    def issue(index_ref, dst_slot):
        group = 8

        def body(j, c):
            for k in range(group):
                r = j * group + k
                src_row = pl.multiple_of(index_ref[0, 0, r], ROW_SLABS)
                pltpu.make_async_copy(y_hbm.at[pl.ds(src_row, ROW_SLABS), :],
                                      ybuf.at[dst_slot, pl.ds(r * ROW_SLABS, ROW_SLABS), :],
                                      sem.at[dst_slot]).start(priority=k % 2)
            return c
        lax.fori_loop(0, tm // group, body, 0)

    @pl.when(i == 0)
    def _():
        issue(pos_ref, 0)

    @pl.when(i + 1 < pl.num_programs(0))
    def _():
        issue(posn_ref, 1 - slot)

    pltpu.make_async_copy(ybuf.at[slot], ybuf.at[slot], sem.at[slot]).wait()
    y = jnp.concatenate([ybuf[slot, pl.ds(blk, tm, stride=ROW_SLABS), :] for blk in range(ROW_SLABS)], axis=-1)
    h = h_ref[...] + y
    o_ref[...] = h * lax.rsqrt(jnp.mean(h * h, axis=-1, keepdims=True) + EPS) * gf_ref[...]


def _final(pos3, y_sorted, h, gf):
    n_tiles, _, tm = pos3.shape
    last = n_tiles - 1
    row = lambda i: (i, 0)
    return pl.pallas_call(
        _final_kernel,
        grid=(n_tiles,),
        in_specs=[pl.BlockSpec((1, 1, tm), lambda i: (i, 0, 0), memory_space=pltpu.SMEM),
                  pl.BlockSpec((1, 1, tm), lambda i: (jnp.minimum(i + 1, last), 0, 0), memory_space=pltpu.SMEM),
                  pl.BlockSpec(memory_space=pl.ANY),
                  pl.BlockSpec((tm, D_MODEL), row),
                  pl.BlockSpec((1, D_MODEL), lambda i: (0, 0))],
        out_specs=pl.BlockSpec((tm, D_MODEL), row),
        out_shape=jax.ShapeDtypeStruct((n_tiles * tm, D_MODEL), F32),
        scratch_shapes=[pltpu.VMEM((2, tm * ROW_SLABS, LANE), F32), pltpu.SemaphoreType.DMA((2,))],
        compiler_params=_cparams(("arbitrary",)),
        name="final",
    )(pos3, pos3, y_sorted, h, gf)


def _qa_perm():
    new = np.arange(NSA_HEADS * NSA_HD)
    r, g, d = new // LANE, (new % LANE) // NSA_HD, new % NSA_HD
    return (g * NSA_R + r) * NSA_HD + d


def _block_diag2(w):
    z = jnp.zeros_like(w)
    return jnp.concatenate([jnp.concatenate([w, z], axis=-1), jnp.concatenate([z, w], axis=-1)], axis=-2)


def kernel(x, rel_bias, ln_mix, w_in, cmp_pos_k, cmp_pos_v, cmp_k_w1, cmp_k_w2, cmp_v_w1, cmp_v_w2,
           diff_lq1, diff_lk1, diff_lq2, diff_lk2, diff_subln, w_out, ln_ffn,
           router_group_w, router_group_b, router_expert_w, router_expert_b,
           exp_w_gate, exp_w_up, exp_w_down, ln_final):
    B, S, D = x.shape
    T = B * S
    assert D == D_MODEL and S % TQ == 0 and S >= WINDOW and T % TM_MOE == 0 and T % TM_FINAL == 0
    x2 = x.reshape(T, D)
    perm = _qa_perm()

    w = w_in[0]
    c_kc, c_vc, c_ks, c_vs, c_kw, c_vw, c_gt = 512, 640, 768, 896, 1024, 1152, 1280
    c_qb = c_gt + N_GATE
    c_kb, c_vb = c_qb + 512, c_qb + 1024
    col = lambda c, n=LANE: w[:, c:c + n]
    w_tok = jnp.concatenate([col(c_kc), col(c_vc), col(c_ks), col(c_kw), col(c_kb, 512)], axis=1).astype(BF16)
    w_feat = jnp.concatenate([w[:, perm], col(c_qb, 512), col(c_vs), col(c_vw), col(c_vb, 512), col(c_gt, N_GATE),
                              jnp.zeros((D, GATE_ROWS - N_GATE), F32)], axis=1).T.astype(BF16)
    qa, kc, vc, ks, kw, qb, kb, vst, vwt, vbt, gates_t = _in_proj(x2, ln_mix[0][None, :], w_tok, w_feat)

    w1k = _block_diag2(cmp_k_w1[0].reshape(CMP_LEN, NSA_HD, CMP_HIDDEN)).astype(BF16)
    w1v = _block_diag2(cmp_v_w1[0].reshape(CMP_LEN, NSA_HD, CMP_HIDDEN)).astype(BF16)
    w2k = _block_diag2(cmp_k_w2[0]).astype(BF16)
    w2v = _block_diag2(cmp_v_w2[0]).astype(BF16)
    posk = jnp.tile(cmp_pos_k[0], (1, NSA_G))
    posv = jnp.tile(cmp_pos_v[0], (1, NSA_G))
    kcmp, vcmpt = _compress(kc, vc, posk, posv, w1k, w1v, w2k, w2v, B, S)

    near, bias_c = _bias_tiles(rel_bias.T, S)

    n_sel = S // SEL_BLOCK
    nrow = S // CMP_STRIDE
    c_start = np.arange(nrow) * CMP_STRIDE
    s_start = np.arange(n_sel) * SEL_BLOCK
    ov = ((c_start[None, :] <= s_start[:, None] + SEL_BLOCK - 1)
          & (c_start[None, :] + CMP_LEN - 1 >= s_start[:, None])
          & (np.arange(nrow)[None, :] < nrow - 1)).astype(np.float32)
    o_a = _nsa(qa, gates_t, kcmp, vcmpt, bias_c, ks, vst, kw, vwt, near[:NSA_HEADS], jnp.asarray(ov, BF16), B, S)

    sub = jnp.tile(diff_subln[0], 2)[:, None]
    o_b = _diff(diff_lq1[0][None, :], diff_lk1[0][None, :], diff_lq2[0][None, :], diff_lk2[0][None, :],
                sub, qb, kb, vbt, near[NSA_HEADS:], B, S)

    w_o = jnp.concatenate([w_out[0][:512][perm], w_out[0][512:]], axis=0).astype(BF16)
    n_r = MOE_GROUPS + N_EXPERTS
    wr = jnp.concatenate([router_group_w[0], router_expert_w[0], jnp.zeros((D, LANE - n_r), F32)], axis=1).astype(BF16)
    br = jnp.concatenate([router_group_b[0], router_expert_b[0], jnp.zeros((LANE - n_r,), F32)])[None, :]
    h1, xt, route, counts = _out_proj(x2, o_a, o_b, w_o, ln_ffn[0][None, :], wr, br)

    tm = TM_MOE
    n_tiles = T // tm + MOE_GROUPS
    cnt = counts[0, :MOE_GROUPS].astype(jnp.int32)
    ends = jnp.cumsum((cnt + tm - 1) // tm * tm)
    starts = ends - (cnt + tm - 1) // tm * tm
    pos = starts[route[:, 0].astype(jnp.int32)] + route[:, 1].astype(jnp.int32)
    src = jnp.zeros((n_tiles * tm,), jnp.int32).at[pos].set(
        jnp.arange(T, dtype=jnp.int32), unique_indices=True, mode="promise_in_bounds")
    tile_start = jnp.arange(n_tiles, dtype=jnp.int32) * tm
    tile_group = jnp.minimum(jnp.searchsorted(ends, tile_start, side="right"), MOE_GROUPS - 1).astype(jnp.int32)
    tile_ok = (tile_start < ends[-1]).astype(jnp.int32)

    by_group = lambda a: a[0].astype(BF16).reshape((MOE_GROUPS, EPG) + a.shape[2:])
    y_sorted = _moe(tile_group, tile_ok, src.reshape(n_tiles, 1, tm), xt,
                    by_group(exp_w_gate), by_group(exp_w_up), by_group(exp_w_down))
    out = _final((pos * ROW_SLABS).reshape(T // TM_FINAL, 1, TM_FINAL), y_sorted, h1, ln_final[None, :])
    return out.reshape(B, S, D)
```

```python
import math

import numpy as np
import jax
import jax.numpy as jnp
from jax import lax
from jax.experimental import pallas as pl
from jax.experimental.pallas import tpu as pltpu

F32 = jnp.float32
BF16 = jnp.bfloat16
NEG = -1e30
EPS = 1e-6
LOG2E = math.log2(math.e)

D_MODEL = 1024
LANE = 128
NSA_HEADS, NSA_G, NSA_R, NSA_HD = 8, 2, 4, 64
CMP_LEN, CMP_STRIDE, CMP_HIDDEN = 32, 16, 128
SEL_BLOCK, SEL_TOPN, SEL_FORCED_LOCAL, WINDOW = 64, 8, 2, 512
DIFF_HEADS, DIFF_HD = 8, 32
REL_BUCKETS, REL_MAX_EXACT, REL_MAX_DIST = 32, 16, 128
N_REL_HEADS = NSA_HEADS + DIFF_HEADS
MOE_GROUPS, EPG, N_EXPERTS, EXPERT_FF = 4, 8, 32, 256
LAMBDA_INIT = 0.8 - 0.6 * math.exp(-0.3 * 0)
N_GATE = NSA_HEADS * 3
GATE_ROWS = 32

TQ = 256
DIFF_PAIRS_PER_STEP = 4
TM_PROJ = 512
TM_MOE = 512
TM_FINAL = 512
XT_WIDTH = D_MODEL + LANE
ROW_SLABS = D_MODEL // LANE
VMEM_LIMIT = 48 * 1024 * 1024
VMEM_LIMIT_MOE = 56 * 1024 * 1024


def _cparams(sem):
    return pltpu.CompilerParams(dimension_semantics=sem, vmem_limit_bytes=VMEM_LIMIT)


def _dot(a, b):
    return jnp.dot(a, b, preferred_element_type=F32)


def _rmsnorm(x, g):
    return x * lax.rsqrt(jnp.mean(x * x, axis=-1, keepdims=True) + EPS) * g


def _dot_nt(a, b):
    return lax.dot_general(a, b, (((1,), (1,)), ((), ())), preferred_element_type=F32)


def _bucket_thresholds():
    n = np.arange(0, REL_MAX_DIST + 1)
    nf = np.maximum(n, 1).astype(np.float32)
    large = REL_MAX_EXACT + (np.log(nf / np.float32(REL_MAX_EXACT)) / np.float32(math.log(REL_MAX_DIST / REL_MAX_EXACT))
                             * np.float32(REL_BUCKETS - REL_MAX_EXACT)).astype(np.int32)
    large = np.minimum(large, REL_BUCKETS - 1)
    bucket = np.where(n < REL_MAX_EXACT, n, large)
    return [int(np.argmax(bucket >= b)) for b in range(REL_BUCKETS)]


_THR = _bucket_thresholds()


def _inproj_kernel(x_ref, g_ref, w_ref, wt_ref, qa_ref, kc_ref, vc_ref, ks_ref, kw_ref, qb_ref, kb_ref,
                   vst_ref, vwt_ref, vbt_ref, gt_ref):
    x = x_ref[...]
    xn = (x * lax.rsqrt(jnp.mean(x * x, axis=-1, keepdims=True) + EPS) * g_ref[...]).astype(BF16)
    a = _dot(xn, w_ref[:, 0:512])
    kc_ref[...] = a[:, 0:128]
    vc_ref[...] = a[:, 128:256]
    ks_ref[...] = a[:, 256:384].astype(BF16)
    kw_ref[...] = a[:, 384:512].astype(BF16)
    a = _dot(xn, w_ref[:, 512:1024])
    for r in range(4):
        kb_ref[r] = a[:, r * LANE:(r + 1) * LANE].astype(BF16)

    ft = _dot_nt(wt_ref[...], xn)

    def feat(row0, rows=LANE):
        return ft[row0:row0 + rows, :]
    for r in range(4):
        qa_ref[r] = (feat(r * LANE) * (NSA_HD ** -0.5 * LOG2E)).astype(BF16)
        qb_ref[r] = (feat(512 + r * LANE) * (DIFF_HD ** -0.5 * LOG2E)).astype(BF16)
        vbt_ref[r] = feat(1280 + r * LANE).astype(BF16)
    vst_ref[...] = feat(1024).astype(BF16)
    vwt_ref[...] = feat(1152).astype(BF16)
    gt_ref[...] = feat(1792, GATE_ROWS)


def _in_proj(x2, g, w, wt):
    T = x2.shape[0]
    tm = TM_PROJ
    row = lambda i: (i, 0)
    o128b = jax.ShapeDtypeStruct((T, LANE), BF16)
    o128f = jax.ShapeDtypeStruct((T, LANE), F32)
    o4 = jax.ShapeDtypeStruct((4, T, LANE), BF16)
    ot = jax.ShapeDtypeStruct((LANE, T), BF16)
    o4t = jax.ShapeDtypeStruct((4, LANE, T), BF16)
    s128 = pl.BlockSpec((tm, LANE), row)
    s4 = pl.BlockSpec((4, tm, LANE), lambda i: (0, i, 0))
    st = pl.BlockSpec((LANE, tm), lambda i: (0, i))
    s4t = pl.BlockSpec((4, LANE, tm), lambda i: (0, 0, i))
    return pl.pallas_call(
        _inproj_kernel,
        grid=(T // tm,),
        in_specs=[pl.BlockSpec((tm, D_MODEL), row),
                  pl.BlockSpec((1, D_MODEL), lambda i: (0, 0)),
                  pl.BlockSpec(w.shape, lambda i: (0, 0)),
                  pl.BlockSpec(wt.shape, lambda i: (0, 0))],
        out_specs=[s4t, s128, s128, s128, s128, s4t, s4, st, st, s4t, pl.BlockSpec((GATE_ROWS, tm), lambda i: (0, i))],
        out_shape=[o4t, o128f, o128f, o128b, o128b, o4t, o4, ot, ot, o4t, jax.ShapeDtypeStruct((GATE_ROWS, T), F32)],
        compiler_params=_cparams(("parallel",)),
        name="in_proj",
    )(x2, g, w, wt)


def _gelu_tanh(x):
    return 0.5 * x * (1.0 + jnp.tanh(math.sqrt(2.0 / math.pi) * (x + 0.044715 * (x * x * x))))


def _compress_kernel(kc_ref, vc_ref, posk_ref, posv_ref, w1k_ref, w1v_ref, w2k_ref, w2v_ref, ko_ref, vo_ref):
    nrow = kc_ref.shape[0] // CMP_STRIDE
    rid = lax.broadcasted_iota(jnp.int32, (nrow, 1), 0)
    cid = lax.broadcasted_iota(jnp.int32, (1, nrow), 1)
    for src, pos, w1, w2, out, transposed in ((kc_ref, posk_ref, w1k_ref, w2k_ref, ko_ref, False),
                                              (vc_ref, posv_ref, w1v_ref, w2v_ref, vo_ref, True)):
        hid_a = jnp.zeros((nrow, 2 * CMP_HIDDEN), F32)
        hid_b = jnp.zeros((nrow, 2 * CMP_HIDDEN), F32)
        for m in range(CMP_STRIDE):
            y = src[pl.ds(m, nrow, stride=CMP_STRIDE), :]
            hid_a = hid_a + _dot((y + pos[m:m + 1, :]).astype(BF16), w1[m])
            hid_b = hid_b + _dot((y + pos[CMP_STRIDE + m:CMP_STRIDE + m + 1, :]).astype(BF16), w1[CMP_STRIDE + m])
        hid = hid_a + pltpu.roll(hid_b, nrow - 1, 0)
        o = _dot(_gelu_tanh(hid).astype(BF16), w2[...])
        if transposed:
            out[0] = jnp.where(cid < nrow - 1, o.T, 0.0).astype(BF16)
        else:
            out[0] = jnp.where(rid < nrow - 1, o, 0.0).astype(BF16)


def _compress(kc, vc, posk, posv, w1k, w1v, w2k, w2v, B, S):
    nrow = S // CMP_STRIDE
    assert nrow == LANE
    full = lambda a: pl.BlockSpec(a.shape, lambda b: (0,) * a.ndim)
    src = pl.BlockSpec((S, LANE), lambda b: (b, 0))
    osp = pl.BlockSpec((1, nrow, LANE), lambda b: (b, 0, 0))
    osh = jax.ShapeDtypeStruct((B, nrow, LANE), BF16)
    return pl.pallas_call(
        _compress_kernel,
        grid=(B,),
        in_specs=[src, src, full(posk), full(posv), full(w1k), full(w1v), full(w2k), full(w2v)],
        out_specs=[osp, osp],
        out_shape=[osh, osh],
        compiler_params=_cparams(("parallel",)),
        name="compress",
    )(kc, vc, posk, posv, w1k, w1v, w2k, w2v)


def _bias_from_dist(dist, tab_ref, h):
    val = jnp.full(dist.shape, tab_ref[h, 0], F32)
    for b in range(1, REL_BUCKETS):
        val = jnp.where(dist >= _THR[b], tab_ref[h, b], val)
    return val


def _bias_near_kernel(tab_ref, out_ref):
    h = pl.program_id(0)
    j = lax.broadcasted_iota(jnp.int32, (TQ, TQ), 0)
    i = lax.broadcasted_iota(jnp.int32, (TQ, TQ), 1)
    for d in range(2):
        bias = (_bias_from_dist(i - j + d * TQ, tab_ref, h) - tab_ref[h, REL_BUCKETS - 1]) * LOG2E
        out_ref[0, d] = jnp.where(i - j + d * TQ >= 0, bias, NEG)


def _bias_cmp_kernel(tab_ref, out_ref):
    h = pl.program_id(0)
    c = lax.broadcasted_iota(jnp.int32, (LANE, TQ), 0)
    t = pl.program_id(1) * TQ + lax.broadcasted_iota(jnp.int32, (LANE, TQ), 1)
    out_ref[0] = _bias_from_dist(t - (c * CMP_STRIDE + CMP_LEN - 1), tab_ref, h) * LOG2E


def _bias_tiles(tab_t, S):
    smem = pl.BlockSpec(memory_space=pltpu.SMEM)
    near = pl.pallas_call(
        _bias_near_kernel,
        grid=(N_REL_HEADS,),
        in_specs=[smem],
        out_specs=pl.BlockSpec((1, 2, TQ, TQ), lambda h: (h, 0, 0, 0)),
        out_shape=jax.ShapeDtypeStruct((N_REL_HEADS, 2, TQ, TQ), F32),
        compiler_params=_cparams(("parallel",)),
        name="bias_near",
    )(tab_t)
    cmp_bias = pl.pallas_call(
        _bias_cmp_kernel,
        grid=(NSA_HEADS, S // TQ),
        in_specs=[smem],
        out_specs=pl.BlockSpec((1, LANE, TQ), lambda h, q: (h, 0, q)),
        out_shape=jax.ShapeDtypeStruct((NSA_HEADS, LANE, S), F32),
        compiler_params=_cparams(("parallel", "parallel")),
        name="bias_cmp",
    )(tab_t)
    return near, cmp_bias


ACC_ROWS = NSA_HD + 16


def _with_ones(vt):
    return jnp.concatenate([vt, jnp.ones((ACC_ROWS - vt.shape[0], vt.shape[1]), BF16)], axis=0)


def _probs(s, m):
    return jnp.exp2((s - m).astype(BF16))


def _flash_first(ss, vts):
    ms = [jnp.max(s, axis=0, keepdims=True) for s in ss]
    ps = [_probs(s, m) for s, m in zip(ss, ms)]
    return tuple((m, _dot(vt, p)) for m, p, vt in zip(ms, ps, vts))


def _flash_update(ss, vts, sts):
    ms = [jnp.maximum(st[0], jnp.max(s, axis=0, keepdims=True)) for s, st in zip(ss, sts)]
    alphas = [jnp.exp2(st[0] - m) for m, st in zip(ms, sts)]
    ps = [_probs(s, m) for s, m in zip(ss, ms)]
    return tuple((m, a * st[1] + _dot(vt, p)) for m, a, st, p, vt in zip(ms, alphas, sts, ps, vts))


def _flash_out(st):
    acc = st[1]
    return acc[0:NSA_HD, :] * (1.0 / acc[NSA_HD:NSA_HD + 1, :])


def _stream_scratch(n):
    return ([pltpu.VMEM((8, TQ), F32), pltpu.VMEM((ACC_ROWS, TQ), F32)] * n
            + [pltpu.VMEM((TQ, TQ), F32)] * (2 * n) + [pltpu.VMEM((TQ, TQ), BF16)] * (2 * n))


def _stream_refs(refs, n):
    st_refs = tuple((refs[2 * c], refs[2 * c + 1]) for c in range(n))
    s, p = refs[2 * n:4 * n], refs[4 * n:6 * n]
    return st_refs, tuple(s[:n]), tuple(s[n:]), tuple(p[:n]), tuple(p[n:])


def _causal_stream(qi, scores, vtile, st_refs, s_a, s_b, p_a, p_b, diag_ready=False):
    n_far = jnp.maximum(qi - 1, 0)
    top = qi - 2

    chains = range(len(st_refs))

    def write(s_refs, kt, near=2):
        for c in chains:
            s_refs[c][...] = scores(jnp.maximum(kt, 0), near, c)

    def pending(p_refs, kt):
        return [_dot(vtile(jnp.maximum(kt, 0), c), p_refs[c][...]) for c in chains]

    def step(s_cur, p_cur, s_next, kt_next, p_prev, kt_prev, anchor=None):
        rescale = []
        for c in chains:
            nxt = scores(jnp.maximum(kt_next, 0), 2, c)
            s_next[c][...] = nxt
            pv = _dot(vtile(jnp.maximum(kt_prev, 0), c), p_prev[c][...])
            st = st_refs[c][0]
            s = s_cur[c][...]
            m_old = st[0:1, :]
            m = jnp.maximum(m_old, jnp.max(s, axis=0, keepdims=True))
            if anchor is not None:
                m = m + anchor
            anchor = nxt[0:1, :] * 0.0
            p_cur[c][...] = _probs(s, m)
            st[0:1, :] = m
            rescale.append((jnp.exp2(m_old - m), pv))
        for c in chains:
            acc = st_refs[c][1]
            acc[...] = rescale[c][0] * (acc[...] + rescale[c][1])
        return anchor

    if not diag_ready:
        write(s_a, qi, 0)
    for c in chains:
        s_b[c][...] = scores(jnp.maximum(qi - 1, 0), 1, c)
        st, acc = st_refs[c]
        s = s_a[c][...]
        m = jnp.max(s, axis=0, keepdims=True)
        p_a[c][...] = _probs(s, m)
        st[0:1, :] = m
        acc[...] = jnp.zeros(acc.shape, F32)
    step(s_b, p_b, s_a, top, p_a, qi)

    def pair(j, c):
        kt = top - 2 * j
        anchor = step(s_a, p_a, s_b, kt - 1, p_b, kt + 1)
        step(s_b, p_b, s_a, kt - 2, p_a, kt, anchor)
        return c
    lax.fori_loop(0, n_far // 2, pair, 0)

    @pl.when(n_far % 2 == 1)
    def _():
        for c, pv in zip(chains, pending(p_b, 1)):
            st, acc = st_refs[c]
            s = s_a[c][...]
            m_old = st[0:1, :]
            m = jnp.maximum(m_old, jnp.max(s, axis=0, keepdims=True))
            st[0:1, :] = m
            acc[...] = jnp.exp2(m_old - m) * (acc[...] + pv) + _dot(vtile(0, c), _probs(s, m))

    @pl.when(n_far % 2 == 0)
    def _():
        last = jnp.where(n_far == 0, qi - 1, 0)
        for c, pv in zip(chains, pending(p_b, last)):
            acc = st_refs[c][1]
            acc[...] = acc[...] + pv
    return tuple((st[0:1, :], acc[...]) for st, acc in st_refs)


def _ktile(ref, kt):
    return ref[pl.ds(pl.multiple_of(kt * TQ, TQ), TQ), :]


def _nsa_kernel(qa_ref, gt_ref, kcmp_ref, vcmpt_ref, bc_ref, ks_ref, vst_ref, kw_ref, vwt_ref, dn_ref,
                ov_ref, o_ref, psum_ref, sel_ref, oacc_ref, sig_ref, *scratch):
    st_refs, s_a, s_b, p_a, p_b = _stream_refs(scratch, NSA_HEADS)
    qi = pl.program_id(1)
    t0 = qi * TQ
    sub_grp = lax.shift_right_arithmetic(lax.broadcasted_iota(jnp.int32, (LANE, 1), 0), 6)
    sig_ref[...] = jax.nn.sigmoid(gt_ref[...])

    def gate_row(c):
        return sig_ref[pl.ds(c, 1), :]

    def masked_q(r, g):
        return jnp.where(sub_grp == g, qa_ref[r].astype(F32), 0.0).astype(BF16)

    n_cmp = kcmp_ref.shape[1] - 1
    crow = lax.broadcasted_iota(jnp.int32, (LANE, 1), 0)
    cmp_end = jnp.where(crow < n_cmp, crow * CMP_STRIDE + CMP_LEN - 1, 1 << 30)
    mask_c = (t0 + lax.broadcasted_iota(jnp.int32, (1, TQ), 1)) >= cmp_end

    heads = [(r, g) for r in range(NSA_R) for g in range(NSA_G)]
    scores_c = [_dot(kcmp_ref[0], masked_q(r, g)) for r, g in heads]
    probs_c = []
    for (r, g), s in zip(heads, scores_c):
        s = jnp.where(mask_c, s + bc_ref[g * NSA_R + r], NEG)
        p = jnp.where(mask_c, jnp.exp2(s - jnp.max(s, axis=0, keepdims=True)), 0.0)
        l = jnp.sum(p, axis=0, keepdims=True)
        probs_c.append(p * jnp.where(l > 0.0, 1.0 / l, 0.0))
    for g in range(NSA_G):
        psum_ref[g] = sum(p for (r, gg), p in zip(heads, probs_c) if gg == g)
    outs_c = [_dot(vcmpt_ref[0], p.astype(BF16)) for p in probs_c]
    for r in range(NSA_R):
        o0, o1 = outs_c[r * NSA_G], outs_c[r * NSA_G + 1]
        oacc_ref[r] = jnp.where(sub_grp == 0, gate_row(r * 3) * o0, gate_row((NSA_R + r) * 3) * o1)

    has_prev = jnp.where(qi >= 1, 0.0, NEG)
    has_wfar = jnp.where(qi >= WINDOW // TQ, 0.0, NEG)
    kt_prev = jnp.maximum(qi - 1, 0)
    kt_wfar = jnp.maximum(qi - WINDOW // TQ, 0)
    wfar_mask = lax.broadcasted_iota(jnp.int32, (TQ, TQ), 0) > lax.broadcasted_iota(jnp.int32, (TQ, TQ), 1)

    def attend_all_heads():
        chains = [(r, g) for r in range(NSA_R) for g in range(NSA_G)]
        qms = [masked_q(r, g) for r, g in chains]
        hids = [g * NSA_R + r for r, g in chains]

        per_tile = TQ // SEL_BLOCK
        key_blk = lax.shift_right_arithmetic(lax.broadcasted_iota(jnp.int32, (TQ, LANE), 0), 6)
        lane_id = lax.broadcasted_iota(jnp.int32, (TQ, LANE), 1)

        def keys_with_block_onehot(kt, g):
            other = (1 - g) * NSA_HD
            onehot = jnp.where(lane_id - other == key_blk, 1.0, 0.0).astype(BF16)
            in_other = (lane_id >= other) & (lane_id < other + NSA_HD)
            return jnp.where(in_other, onehot, _ktile(ks_ref, kt))

        def query_with_mask_rows(kt, c):
            r, g = chains[c]
            own = qa_ref[r, g * NSA_HD:(g + 1) * NSA_HD, :]
            pair = sel_ref[g, pl.ds(pl.multiple_of((kt // 2) * 2 * per_tile, 2 * per_tile), 2 * per_tile), :]
            rows = jnp.where(kt % 2 == 0, pair[0:per_tile], pair[per_tile:2 * per_tile]).astype(BF16)
            other = jnp.concatenate([rows, jnp.zeros((NSA_HD - per_tile, TQ), BF16)], axis=0)
            return jnp.concatenate([own, other] if g == 0 else [other, own], axis=0)

        def near_bias(s, near, c):
            if near == 0:
                return s + dn_ref[hids[c], 0]
            if near == 1:
                return s + dn_ref[hids[c], 1] + has_prev
            return s

        def group_values(ref, kt, c):
            g = chains[c][1]
            return _with_ones(ref[g * NSA_HD:(g + 1) * NSA_HD, pl.ds(pl.multiple_of(kt * TQ, TQ), TQ)])

        def finish(sts, branch):
            for c, (r, g) in enumerate(chains):
                oacc_ref[r, g * NSA_HD:(g + 1) * NSA_HD, :] += gate_row(hids[c] * 3 + branch) * _flash_out(sts[c])

        def slc_scores(kt, near, c):
            return near_bias(_dot(keys_with_block_onehot(kt, chains[c][1]), query_with_mask_rows(kt, c)), near, c)

        def slc_vtile(kt, c):
            return group_values(vst_ref, kt, c)

        win_tiles = ((qi, 0), (kt_prev, 1), (kt_wfar, 2))

        def win_scores(c):
            out = []
            for kt, near in win_tiles:
                s = near_bias(_dot(_ktile(kw_ref, kt), qms[c]), near, c)
                out.append(jnp.where(wfar_mask, s + has_wfar, NEG) if near == 2 else s)
            return out

        sts = []
        ahead = win_scores(0)
        for c in range(len(chains)):
            tiles, ahead = ahead, (win_scores(c + 1) if c + 1 < len(chains) else None)
            st = _flash_first([tiles[0]], [group_values(vwt_ref, qi, c)])
            st = _flash_update([tiles[1]], [group_values(vwt_ref, kt_prev, c)], st)
            sts.append(_flash_update([tiles[2]], [group_values(vwt_ref, kt_wfar, c)], st)[0])
        finish(sts, 2)

        n_sel = ov_ref.shape[0]
        jj = lax.broadcasted_iota(jnp.int32, (n_sel, TQ), 0)
        cur = lax.shift_right_arithmetic(t0 + lax.broadcasted_iota(jnp.int32, (n_sel, TQ), 1), 6)
        valid = jj <= cur
        forced = (jj == 0) | (cur - jj < SEL_FORCED_LOCAL)
        for g in range(NSA_G):
            ps = psum_ref[g]
            hi = ps.astype(BF16)
            rem = ps - hi.astype(F32)
            mid = rem.astype(BF16)
            lo = (rem - mid.astype(F32)).astype(BF16)
            ov = ov_ref[...]
            imp = _dot(ov, hi) + _dot(ov, mid) + _dot(ov, lo)
            score = jnp.where(valid, jnp.where(forced, 1e9, imp), -1e9)
            cnt = jnp.zeros((n_sel, TQ), F32)
            for j2 in range(n_sel):
                row = score[j2:j2 + 1, :]
                tie = jnp.where(jj > j2, 1.0, 0.0)
                cnt = cnt + jnp.where(row > score, 1.0, jnp.where(row == score, tie, 0.0))
            sel_ref[g] = jnp.where(cnt < float(min(SEL_TOPN, n_sel)), jnp.where(score > -1e8, 0.0, NEG), NEG)

        finish(_causal_stream(qi, slc_scores, slc_vtile, st_refs, s_a, s_b, p_a, p_b), 1)
    attend_all_heads()

    for r in range(NSA_R):
        o_ref[r] = oacc_ref[r].T.astype(BF16)


def _nsa(qa, gates_t, kcmp, vcmpt, bias_c, ks, vst, kw, vwt, near_a, ov, B, S):
    nq = S // TQ
    T = B * S
    k_spec = pl.BlockSpec((S, LANE), lambda b, q: (b, 0))
    v_spec = pl.BlockSpec((LANE, S), lambda b, q: (0, b))
    full = lambda a: pl.BlockSpec(a.shape, lambda b, q: (0,) * a.ndim)
    cmp_spec = pl.BlockSpec((1, LANE, LANE), lambda b, q: (b, 0, 0))
    qo = pl.BlockSpec((4, TQ, LANE), lambda b, q: (0, b * nq + q, 0))
    return pl.pallas_call(
        _nsa_kernel,
        grid=(B, nq),
        in_specs=[pl.BlockSpec((4, LANE, TQ), lambda b, q: (0, 0, b * nq + q)),
                  pl.BlockSpec((GATE_ROWS, TQ), lambda b, q: (0, b * nq + q)),
                  cmp_spec, cmp_spec,
                  pl.BlockSpec((NSA_HEADS, LANE, TQ), lambda b, q: (0, 0, q)),
                  k_spec, v_spec, k_spec, v_spec, full(near_a), full(ov)],
        out_specs=qo,
        out_shape=jax.ShapeDtypeStruct((4, T, LANE), BF16),
        scratch_shapes=[pltpu.VMEM((NSA_G, LANE, TQ), F32), pltpu.VMEM((NSA_G, S // SEL_BLOCK, TQ), F32),
                        pltpu.VMEM((NSA_R, LANE, TQ), F32), pltpu.VMEM((GATE_ROWS, TQ), F32)]
                       + _stream_scratch(NSA_HEADS),
        compiler_params=_cparams(("parallel", "arbitrary")),
        name="nsa",
    )(qa, gates_t, kcmp, vcmpt, bias_c, ks, vst, kw, vwt, near_a, ov)


def _diff_kernel(lq1_ref, lk1_ref, lq2_ref, lk2_ref, sub_ref, qb_ref, kb_ref, vbt_ref, dn_ref, o_ref, *scratch):
    n = DIFF_PAIRS_PER_STEP * 4
    st_refs, s_a, s_b, p_a, p_b = _stream_refs(scratch, n)
    qi = pl.program_id(1)
    chain_of_row = lax.shift_right_arithmetic(lax.broadcasted_iota(jnp.int32, (LANE, 1), 0), 5)
    lam = (jnp.exp(jnp.sum(lq1_ref[...] * lk1_ref[...], axis=-1, keepdims=True))
           - jnp.exp(jnp.sum(lq2_ref[...] * lk2_ref[...], axis=-1, keepdims=True)) + LAMBDA_INIT)
    has_prev = jnp.where(qi >= 1, 0.0, NEG)
    n_steps = DIFF_HEADS // 2 // DIFF_PAIRS_PER_STEP

    def score_fn(step):
        pairs = [step * DIFF_PAIRS_PER_STEP + c // 4 for c in range(n)]
        qs = [qb_ref[step * DIFF_PAIRS_PER_STEP + i].astype(F32) for i in range(DIFF_PAIRS_PER_STEP)]
        qms = [jnp.where(chain_of_row == c % 4, qs[c // 4], 0.0).astype(BF16) for c in range(n)]

        def scores(kt, near, c):
            s = _dot(kb_ref[pairs[c], pl.ds(pl.multiple_of(kt * TQ, TQ), TQ), :], qms[c])
            if near == 2:
                return s
            bias = dn_ref[2 * pairs[c] + (c % 4) // 2, near]
            return s + bias if near == 0 else s + bias + has_prev
        return scores

    def start_diag(step):
        scores = score_fn(step)
        for c in range(n):
            s_a[c][...] = scores(qi, 0, c)

    start_diag(0)

    def step_body(step, carry):
        def vtile(kt, c):
            hh = (c % 4) // 2
            return _with_ones(vbt_ref[step * DIFF_PAIRS_PER_STEP + c // 4, hh * 2 * DIFF_HD:(hh + 1) * 2 * DIFF_HD,
                                      pl.ds(pl.multiple_of(kt * TQ, TQ), TQ)])

        sts = _causal_stream(qi, score_fn(step), vtile, st_refs, s_a, s_b, p_a, p_b, diag_ready=True)
        start_diag(jnp.minimum(step + 1, n_steps - 1))
        for i in range(DIFF_PAIRS_PER_STEP):
            outs = []
            for hh in range(2):
                c = 4 * i + 2 * hh
                o = _flash_out(sts[c]) - lam * _flash_out(sts[c + 1])
                outs.append(o * lax.rsqrt(jnp.mean(o * o, axis=0, keepdims=True) + EPS))
            out = jnp.concatenate(outs, axis=0) * sub_ref[...] * (1.0 - LAMBDA_INIT)
            o_ref[step * DIFF_PAIRS_PER_STEP + i] = out.T.astype(BF16)
        return carry
    lax.fori_loop(0, n_steps, step_body, 0)


def _diff(lq1, lk1, lq2, lk2, sub, qb, kb, vbt, near_b, B, S):
    nq = S // TQ
    T = B * S
    full = lambda a: pl.BlockSpec(a.shape, lambda b, q: (0,) * a.ndim)
    k_spec = pl.BlockSpec((4, S, LANE), lambda b, q: (0, b, 0))
    v_spec = pl.BlockSpec((4, LANE, S), lambda b, q: (0, 0, b))
    qo = pl.BlockSpec((4, TQ, LANE), lambda b, q: (0, b * nq + q, 0))
    return pl.pallas_call(
        _diff_kernel,
        grid=(B, nq),
        in_specs=[full(lq1), full(lk1), full(lq2), full(lk2), full(sub),
                  pl.BlockSpec((4, LANE, TQ), lambda b, q: (0, 0, b * nq + q)), k_spec, v_spec, full(near_b)],
        out_specs=qo,
        out_shape=jax.ShapeDtypeStruct((4, T, LANE), BF16),
        scratch_shapes=_stream_scratch(DIFF_PAIRS_PER_STEP * 4),
        compiler_params=_cparams(("parallel", "arbitrary")),
        name="diff",
    )(lq1, lk1, lq2, lk2, sub, qb, kb, vbt, near_b)


def _outproj_kernel(x_ref, oa_ref, ob_ref, w_ref, g_ref, wr_ref, br_ref, xt_ref, route_ref, cnt_ref, carry_ref):
    @pl.when(pl.program_id(0) == 0)
    def _():
        carry_ref[...] = jnp.zeros(carry_ref.shape, F32)

    o = jnp.concatenate([oa_ref[r] for r in range(4)] + [ob_ref[r] for r in range(4)], axis=-1)
    h = x_ref[...] + _dot(o, w_ref[...])
    xt_ref[:, 0:D_MODEL] = h
    tn = _rmsnorm(h, g_ref[...]).astype(BF16)
    logits = _dot(tn, wr_ref[...]) + br_ref[...]
    lane = lax.broadcasted_iota(jnp.int32, (1, LANE), 1)
    lane_f = lane.astype(F32)
    is_grp = lane < MOE_GROUPS
    lg = jnp.where(is_grp, logits, NEG)
    mg = jnp.max(lg, axis=-1, keepdims=True)
    zg = jnp.sum(jnp.where(is_grp, jnp.exp(lg - mg), 0.0), axis=-1, keepdims=True)
    g_prob = 1.0 / zg
    g_idx = jnp.min(jnp.where(lg == mg, lane_f, 1e9), axis=-1, keepdims=True)
    lane_grp = jnp.where((lane >= MOE_GROUPS) & (lane < MOE_GROUPS + N_EXPERTS),
                         lax.shift_right_arithmetic(lane - MOE_GROUPS, 3), -1).astype(F32)
    le = jnp.where(lane_grp == g_idx, logits, NEG)
    m1 = jnp.max(le, axis=-1, keepdims=True)
    e1 = jnp.min(jnp.where(le == m1, lane_f, 1e9), axis=-1, keepdims=True)
    le2 = jnp.where(lane_f == e1, NEG, le)
    m2 = jnp.max(le2, axis=-1, keepdims=True)
    e2 = jnp.min(jnp.where(le2 == m2, lane_f, 1e9), axis=-1, keepdims=True)
    ratio = jnp.exp(m2 - m1)
    w1 = g_prob / (1.0 + ratio)
    w2 = w1 * ratio
    xt_ref[:, D_MODEL:D_MODEL + LANE] = jnp.where(lane_f == e1, w1, 0.0) + jnp.where(lane_f == e2, w2, 0.0)

    tm = h.shape[0]
    onehot = jnp.where(lane_f == g_idx, 1.0, 0.0)
    earlier = jnp.where(lax.broadcasted_iota(jnp.int32, (tm, tm), 0) > lax.broadcasted_iota(jnp.int32, (tm, tm), 1),
                        1.0, 0.0).astype(BF16)
    prefix = _dot(earlier, onehot.astype(BF16)) + carry_ref[...]
    rank = jnp.sum(onehot * prefix, axis=-1, keepdims=True)
    carry_ref[...] += jnp.sum(onehot, axis=0, keepdims=True)
    route_ref[...] = jnp.where(lane == 0, g_idx, jnp.where(lane == 1, rank, 0.0))
    cnt_ref[...] = jnp.broadcast_to(carry_ref[...], cnt_ref.shape)


def _out_proj(x2, oa, ob, w, g, wr, br):
    T = x2.shape[0]
    tm = TM_PROJ
    row = lambda i: (i, 0)
    full = lambda a: pl.BlockSpec(a.shape, lambda i: (0,) * a.ndim)
    o4 = pl.BlockSpec((4, tm, LANE), lambda i: (0, i, 0))
    return pl.pallas_call(
        _outproj_kernel,
        grid=(T // tm,),
        in_specs=[pl.BlockSpec((tm, D_MODEL), row), o4, o4, full(w), full(g), full(wr), full(br)],
        out_specs=[pl.BlockSpec((tm, XT_WIDTH), row), pl.BlockSpec((tm, LANE), row),
                   pl.BlockSpec((8, LANE), lambda i: (0, 0))],
        out_shape=[jax.ShapeDtypeStruct((T, XT_WIDTH), F32), jax.ShapeDtypeStruct((T, LANE), F32),
                   jax.ShapeDtypeStruct((8, LANE), F32)],
        scratch_shapes=[pltpu.VMEM((1, LANE), F32)],
        compiler_params=_cparams(("arbitrary",)),
        name="out_proj",
    )(x2, oa, ob, w, g, wr, br)


def _moe_kernel(tg_ref, ok_ref, idx_ref, idxn_ref, xt_hbm, g_ref, wg_ref, wu_ref, wd_ref, y_ref, xbuf, sem,
                xb_ref, cmb_ref):
    i = pl.program_id(0)
    slot = i % 2
    tm = xbuf.shape[1]

    def row_copy(index_ref, r, dst_slot):
        return pltpu.make_async_copy(xt_hbm.at[pl.ds(index_ref[0, 0, r], 1), :],
                                     xbuf.at[dst_slot, pl.ds(r, 1), :], sem.at[dst_slot])

    @pl.when(i == 0)
    def _():
        def body(r, c):
            row_copy(idx_ref, r, 0).start()
            return c
        lax.fori_loop(0, tm, body, 0)

    @pl.when((i == 0) | (ok_ref[jnp.maximum(i - 1, 0)] == 1))
    def _():
        pltpu.make_async_copy(xt_hbm.at[pl.ds(0, tm), :], xbuf.at[slot], sem.at[slot]).wait()

    @pl.when(ok_ref[i] == 0)
    def _():
        y_ref[...] = jnp.zeros(y_ref.shape, F32)

    @pl.when(ok_ref[i] == 1)
    def _():
        lane = lax.broadcasted_iota(jnp.int32, (1, LANE), 1)
        xb_ref[...] = _rmsnorm(xbuf[slot, :, 0:D_MODEL], g_ref[...]).astype(BF16)
        cmb_ref[...] = xbuf[slot, :, D_MODEL:D_MODEL + LANE]
        first_lane = MOE_GROUPS + tg_ref[i] * EPG
        per_expert = tm // EPG
        y = jnp.zeros((tm, D_MODEL), F32)
        for e in range(EPG):
            x = xb_ref[...]
            a = _dot(x, wg_ref[0, e])
            b = _dot(x, wu_ref[0, e])
            ce = jnp.sum(jnp.where(lane == first_lane + e, cmb_ref[...], 0.0), axis=-1, keepdims=True)
            y = y + _dot(((a * jax.nn.sigmoid(a)) * b * ce).astype(BF16), wd_ref[0, e])
            for r in range(e * per_expert, (e + 1) * per_expert):
                row_copy(idxn_ref, r, 1 - slot).start(priority=r % 2)
        for blk in range(ROW_SLABS):
            y_ref[pl.ds(blk, tm, stride=ROW_SLABS), :] = y[:, blk * LANE:(blk + 1) * LANE]


def _moe(tile_group, tile_ok, src3, xt, g, wg, wu, wd):
    n_tiles, _, tm = src3.shape
    last = n_tiles - 1
    grid_spec = pltpu.PrefetchScalarGridSpec(
        num_scalar_prefetch=2,
        grid=(n_tiles,),
        in_specs=[pl.BlockSpec((1, 1, tm), lambda i, tg, ok: (i, 0, 0), memory_space=pltpu.SMEM),
                  pl.BlockSpec((1, 1, tm), lambda i, tg, ok: (jnp.minimum(i + 1, last), 0, 0), memory_space=pltpu.SMEM),
                  pl.BlockSpec(memory_space=pl.ANY),
                  pl.BlockSpec((1, D_MODEL), lambda i, tg, ok: (0, 0)),
                  pl.BlockSpec((1, EPG, D_MODEL, EXPERT_FF), lambda i, tg, ok: (tg[i], 0, 0, 0)),
                  pl.BlockSpec((1, EPG, D_MODEL, EXPERT_FF), lambda i, tg, ok: (tg[i], 0, 0, 0)),
                  pl.BlockSpec((1, EPG, EXPERT_FF, D_MODEL), lambda i, tg, ok: (tg[i], 0, 0, 0))],
        out_specs=pl.BlockSpec((tm * ROW_SLABS, LANE), lambda i, tg, ok: (i, 0)),
        scratch_shapes=[pltpu.VMEM((2, tm, XT_WIDTH), F32), pltpu.SemaphoreType.DMA((2,)),
                        pltpu.VMEM((tm, D_MODEL), BF16), pltpu.VMEM((tm, LANE), F32)],
    )
    return pl.pallas_call(
        _moe_kernel,
        grid_spec=grid_spec,
        out_shape=jax.ShapeDtypeStruct((n_tiles * tm * ROW_SLABS, LANE), F32),
        compiler_params=pltpu.CompilerParams(dimension_semantics=("arbitrary",), vmem_limit_bytes=VMEM_LIMIT_MOE),
        name="moe",
    )(tile_group, tile_ok, src3, src3, xt, g, wg, wu, wd)


def _final_kernel(pos_ref, posn_ref, y_hbm, h_ref, gf_ref, o_ref, ybuf, sem):
    i = pl.program_id(0)
    slot = i % 2
    tm = h_ref.shape[0]

    def issue(index_ref, dst_slot):
        group = 8

        def body(j, c):
            for k in range(group):
                r = j * group + k
                src_row = pl.multiple_of(index_ref[0, 0, r], ROW_SLABS)
                pltpu.make_async_copy(y_hbm.at[pl.ds(src_row, ROW_SLABS), :],
                                      ybuf.at[dst_slot, pl.ds(r * ROW_SLABS, ROW_SLABS), :],
                                      sem.at[dst_slot]).start(priority=k % 2)
            return c
        lax.fori_loop(0, tm // group, body, 0)

    @pl.when(i == 0)
    def _():
        issue(pos_ref, 0)

    @pl.when(i + 1 < pl.num_programs(0))
    def _():
        issue(posn_ref, 1 - slot)

    pltpu.make_async_copy(ybuf.at[slot], ybuf.at[slot], sem.at[slot]).wait()
    y = jnp.concatenate([ybuf[slot, pl.ds(blk, tm, stride=ROW_SLABS), :] for blk in range(ROW_SLABS)], axis=-1)
    h = h_ref[...] + y
    o_ref[...] = h * lax.rsqrt(jnp.mean(h * h, axis=-1, keepdims=True) + EPS) * gf_ref[...]


def _final(pos3, y_sorted, h, gf):
    n_tiles, _, tm = pos3.shape
    last = n_tiles - 1
    row = lambda i: (i, 0)
    return pl.pallas_call(
        _final_kernel,
        grid=(n_tiles,),
        in_specs=[pl.BlockSpec((1, 1, tm), lambda i: (i, 0, 0), memory_space=pltpu.SMEM),
                  pl.BlockSpec((1, 1, tm), lambda i: (jnp.minimum(i + 1, last), 0, 0), memory_space=pltpu.SMEM),
                  pl.BlockSpec(memory_space=pl.ANY),
                  pl.BlockSpec((tm, D_MODEL), row),
                  pl.BlockSpec((1, D_MODEL), lambda i: (0, 0))],
        out_specs=pl.BlockSpec((tm, D_MODEL), row),
        out_shape=jax.ShapeDtypeStruct((n_tiles * tm, D_MODEL), F32),
        scratch_shapes=[pltpu.VMEM((2, tm * ROW_SLABS, LANE), F32), pltpu.SemaphoreType.DMA((2,))],
        compiler_params=_cparams(("arbitrary",)),
        name="final",
    )(pos3, pos3, y_sorted, h, gf)


def _qa_perm():
    new = np.arange(NSA_HEADS * NSA_HD)
    r, g, d = new // LANE, (new % LANE) // NSA_HD, new % NSA_HD
    return (g * NSA_R + r) * NSA_HD + d


def _block_diag2(w):
    z = jnp.zeros_like(w)
    return jnp.concatenate([jnp.concatenate([w, z], axis=-1), jnp.concatenate([z, w], axis=-1)], axis=-2)


def kernel(x, rel_bias, ln_mix, w_in, cmp_pos_k, cmp_pos_v, cmp_k_w1, cmp_k_w2, cmp_v_w1, cmp_v_w2,
           diff_lq1, diff_lk1, diff_lq2, diff_lk2, diff_subln, w_out, ln_ffn,
           router_group_w, router_group_b, router_expert_w, router_expert_b,
           exp_w_gate, exp_w_up, exp_w_down, ln_final):
    B, S, D = x.shape
    T = B * S
    assert D == D_MODEL and S % TQ == 0 and S >= WINDOW and T % TM_MOE == 0 and T % TM_FINAL == 0
    x2 = x.reshape(T, D)
    perm = _qa_perm()

    w = w_in[0]
    c_kc, c_vc, c_ks, c_vs, c_kw, c_vw, c_gt = 512, 640, 768, 896, 1024, 1152, 1280
    c_qb = c_gt + N_GATE
    c_kb, c_vb = c_qb + 512, c_qb + 1024
    col = lambda c, n=LANE: w[:, c:c + n]
    w_tok = jnp.concatenate([col(c_kc), col(c_vc), col(c_ks), col(c_kw), col(c_kb, 512)], axis=1).astype(BF16)
    w_feat = jnp.concatenate([w[:, perm], col(c_qb, 512), col(c_vs), col(c_vw), col(c_vb, 512), col(c_gt, N_GATE),
                              jnp.zeros((D, GATE_ROWS - N_GATE), F32)], axis=1).T.astype(BF16)
    qa, kc, vc, ks, kw, qb, kb, vst, vwt, vbt, gates_t = _in_proj(x2, ln_mix[0][None, :], w_tok, w_feat)

    w1k = _block_diag2(cmp_k_w1[0].reshape(CMP_LEN, NSA_HD, CMP_HIDDEN)).astype(BF16)
    w1v = _block_diag2(cmp_v_w1[0].reshape(CMP_LEN, NSA_HD, CMP_HIDDEN)).astype(BF16)
    w2k = _block_diag2(cmp_k_w2[0]).astype(BF16)
    w2v = _block_diag2(cmp_v_w2[0]).astype(BF16)
    posk = jnp.tile(cmp_pos_k[0], (1, NSA_G))
    posv = jnp.tile(cmp_pos_v[0], (1, NSA_G))
    kcmp, vcmpt = _compress(kc, vc, posk, posv, w1k, w1v, w2k, w2v, B, S)

    near, bias_c = _bias_tiles(rel_bias.T, S)

    n_sel = S // SEL_BLOCK
    nrow = S // CMP_STRIDE
    c_start = np.arange(nrow) * CMP_STRIDE
    s_start = np.arange(n_sel) * SEL_BLOCK
    ov = ((c_start[None, :] <= s_start[:, None] + SEL_BLOCK - 1)
          & (c_start[None, :] + CMP_LEN - 1 >= s_start[:, None])
          & (np.arange(nrow)[None, :] < nrow - 1)).astype(np.float32)
    o_a = _nsa(qa, gates_t, kcmp, vcmpt, bias_c, ks, vst, kw, vwt, near[:NSA_HEADS], jnp.asarray(ov, BF16), B, S)

    sub = jnp.tile(diff_subln[0], 2)[:, None]
    o_b = _diff(diff_lq1[0][None, :], diff_lk1[0][None, :], diff_lq2[0][None, :], diff_lk2[0][None, :],
                sub, qb, kb, vbt, near[NSA_HEADS:], B, S)

    w_o = jnp.concatenate([w_out[0][:512][perm], w_out[0][512:]], axis=0).astype(BF16)
    n_r = MOE_GROUPS + N_EXPERTS
    wr = jnp.concatenate([router_group_w[0], router_expert_w[0], jnp.zeros((D, LANE - n_r), F32)], axis=1).astype(BF16)
    br = jnp.concatenate([router_group_b[0], router_expert_b[0], jnp.zeros((LANE - n_r,), F32)])[None, :]
    xt, route, counts = _out_proj(x2, o_a, o_b, w_o, ln_ffn[0][None, :], wr, br)

    tm = TM_MOE
    n_tiles = T // tm + MOE_GROUPS
    cnt = counts[0, :MOE_GROUPS].astype(jnp.int32)
    ends = jnp.cumsum((cnt + tm - 1) // tm * tm)
    starts = ends - (cnt + tm - 1) // tm * tm
    pos = starts[route[:, 0].astype(jnp.int32)] + route[:, 1].astype(jnp.int32)
    src = jnp.zeros((n_tiles * tm,), jnp.int32).at[pos].set(
        jnp.arange(T, dtype=jnp.int32), unique_indices=True, mode="promise_in_bounds")
    tile_start = jnp.arange(n_tiles, dtype=jnp.int32) * tm
    tile_group = jnp.minimum(jnp.searchsorted(ends, tile_start, side="right"), MOE_GROUPS - 1).astype(jnp.int32)
    tile_ok = (tile_start < ends[-1]).astype(jnp.int32)

    by_group = lambda a: a[0].astype(BF16).reshape((MOE_GROUPS, EPG) + a.shape[2:])
    y_sorted = _moe(tile_group, tile_ok, src.reshape(n_tiles, 1, tm), xt, ln_ffn[0][None, :],
                    by_group(exp_w_gate), by_group(exp_w_up), by_group(exp_w_down))
    out = _final((pos * ROW_SLABS).reshape(T // TM_FINAL, 1, TM_FINAL), y_sorted, xt, ln_final[None, :])
    return out.reshape(B, S, D)
```

```python
import math

import numpy as np
import jax
import jax.numpy as jnp
from jax import lax
from jax.experimental import pallas as pl
from jax.experimental.pallas import tpu as pltpu

F32 = jnp.float32
BF16 = jnp.bfloat16
NEG = -1e30
EPS = 1e-6
LOG2E = math.log2(math.e)

D_MODEL = 1024
LANE = 128
NSA_HEADS, NSA_G, NSA_R, NSA_HD = 8, 2, 4, 64
CMP_LEN, CMP_STRIDE, CMP_HIDDEN = 32, 16, 128
SEL_BLOCK, SEL_TOPN, SEL_FORCED_LOCAL, WINDOW = 64, 8, 2, 512
DIFF_HEADS, DIFF_HD = 8, 32
REL_BUCKETS, REL_MAX_EXACT, REL_MAX_DIST = 32, 16, 128
N_REL_HEADS = NSA_HEADS + DIFF_HEADS
MOE_GROUPS, EPG, N_EXPERTS, EXPERT_FF = 4, 8, 32, 256
LAMBDA_INIT = 0.8 - 0.6 * math.exp(-0.3 * 0)
N_GATE = NSA_HEADS * 3
GATE_ROWS = 32

TQ = 256
DIFF_PAIRS_PER_STEP = 4
TM_PROJ = 512
TM_MOE = 512
TM_FINAL = 512
ANCHOR_AFTER = (1, 3, 5, 6)
XT_WIDTH = D_MODEL + LANE
ROW_SLABS = D_MODEL // LANE
VMEM_LIMIT = 48 * 1024 * 1024
VMEM_LIMIT_MOE = 56 * 1024 * 1024


def _cparams(sem):
    return pltpu.CompilerParams(dimension_semantics=sem, vmem_limit_bytes=VMEM_LIMIT)


def _dot(a, b):
    return jnp.dot(a, b, preferred_element_type=F32)


def _rmsnorm(x, g):
    return x * lax.rsqrt(jnp.mean(x * x, axis=-1, keepdims=True) + EPS) * g


def _dot_nt(a, b):
    return lax.dot_general(a, b, (((1,), (1,)), ((), ())), preferred_element_type=F32)


def _bucket_thresholds():
    n = np.arange(0, REL_MAX_DIST + 1)
    nf = np.maximum(n, 1).astype(np.float32)
    large = REL_MAX_EXACT + (np.log(nf / np.float32(REL_MAX_EXACT)) / np.float32(math.log(REL_MAX_DIST / REL_MAX_EXACT))
                             * np.float32(REL_BUCKETS - REL_MAX_EXACT)).astype(np.int32)
    large = np.minimum(large, REL_BUCKETS - 1)
    bucket = np.where(n < REL_MAX_EXACT, n, large)
    return [int(np.argmax(bucket >= b)) for b in range(REL_BUCKETS)]


_THR = _bucket_thresholds()


def _inproj_kernel(x_ref, g_ref, w_ref, wt_ref, qa_ref, kc_ref, vc_ref, ks_ref, kw_ref, qb_ref, kb_ref,
                   vst_ref, vwt_ref, vbt_ref, gt_ref):
    x = x_ref[...]
    xn = (x * lax.rsqrt(jnp.mean(x * x, axis=-1, keepdims=True) + EPS) * g_ref[...]).astype(BF16)
    a = _dot(xn, w_ref[:, 0:512])
    kc_ref[...] = a[:, 0:128]
    vc_ref[...] = a[:, 128:256]
    ks_ref[...] = a[:, 256:384].astype(BF16)
    kw_ref[...] = a[:, 384:512].astype(BF16)
    a = _dot(xn, w_ref[:, 512:1024])
    for r in range(4):
        kb_ref[r] = a[:, r * LANE:(r + 1) * LANE].astype(BF16)

    ft = _dot_nt(wt_ref[...], xn)

    def feat(row0, rows=LANE):
        return ft[row0:row0 + rows, :]
    for r in range(4):
        qa_ref[r] = (feat(r * LANE) * (NSA_HD ** -0.5 * LOG2E)).astype(BF16)
        qb_ref[r] = (feat(512 + r * LANE) * (DIFF_HD ** -0.5 * LOG2E)).astype(BF16)
        vbt_ref[r] = feat(1280 + r * LANE).astype(BF16)
    vst_ref[...] = feat(1024).astype(BF16)
    vwt_ref[...] = feat(1152).astype(BF16)
    gt_ref[...] = feat(1792, GATE_ROWS)


def _in_proj(x2, g, w, wt):
    T = x2.shape[0]
    tm = TM_PROJ
    row = lambda i: (i, 0)
    o128b = jax.ShapeDtypeStruct((T, LANE), BF16)
    o128f = jax.ShapeDtypeStruct((T, LANE), F32)
    o4 = jax.ShapeDtypeStruct((4, T, LANE), BF16)
    ot = jax.ShapeDtypeStruct((LANE, T), BF16)
    o4t = jax.ShapeDtypeStruct((4, LANE, T), BF16)
    s128 = pl.BlockSpec((tm, LANE), row)
    s4 = pl.BlockSpec((4, tm, LANE), lambda i: (0, i, 0))
    st = pl.BlockSpec((LANE, tm), lambda i: (0, i))
    s4t = pl.BlockSpec((4, LANE, tm), lambda i: (0, 0, i))
    return pl.pallas_call(
        _inproj_kernel,
        grid=(T // tm,),
        in_specs=[pl.BlockSpec((tm, D_MODEL), row),
                  pl.BlockSpec((1, D_MODEL), lambda i: (0, 0)),
                  pl.BlockSpec(w.shape, lambda i: (0, 0)),
                  pl.BlockSpec(wt.shape, lambda i: (0, 0))],
        out_specs=[s4t, s128, s128, s128, s128, s4t, s4, st, st, s4t, pl.BlockSpec((GATE_ROWS, tm), lambda i: (0, i))],
        out_shape=[o4t, o128f, o128f, o128b, o128b, o4t, o4, ot, ot, o4t, jax.ShapeDtypeStruct((GATE_ROWS, T), F32)],
        compiler_params=_cparams(("parallel",)),
        name="in_proj",
    )(x2, g, w, wt)


def _gelu_tanh(x):
    return 0.5 * x * (1.0 + jnp.tanh(math.sqrt(2.0 / math.pi) * (x + 0.044715 * (x * x * x))))


def _compress_kernel(kc_ref, vc_ref, posk_ref, posv_ref, w1k_ref, w1v_ref, w2k_ref, w2v_ref, ko_ref, vo_ref):
    nrow = kc_ref.shape[0] // CMP_STRIDE
    rid = lax.broadcasted_iota(jnp.int32, (nrow, 1), 0)
    cid = lax.broadcasted_iota(jnp.int32, (1, nrow), 1)
    for src, pos, w1, w2, out, transposed in ((kc_ref, posk_ref, w1k_ref, w2k_ref, ko_ref, False),
                                              (vc_ref, posv_ref, w1v_ref, w2v_ref, vo_ref, True)):
        hid_a = jnp.zeros((nrow, 2 * CMP_HIDDEN), F32)
        hid_b = jnp.zeros((nrow, 2 * CMP_HIDDEN), F32)
        for m in range(CMP_STRIDE):
            y = src[pl.ds(m, nrow, stride=CMP_STRIDE), :]
            hid_a = hid_a + _dot((y + pos[m:m + 1, :]).astype(BF16), w1[m])
            hid_b = hid_b + _dot((y + pos[CMP_STRIDE + m:CMP_STRIDE + m + 1, :]).astype(BF16), w1[CMP_STRIDE + m])
        hid = hid_a + pltpu.roll(hid_b, nrow - 1, 0)
        o = _dot(_gelu_tanh(hid).astype(BF16), w2[...])
        if transposed:
            out[0] = jnp.where(cid < nrow - 1, o.T, 0.0).astype(BF16)
        else:
            out[0] = jnp.where(rid < nrow - 1, o, 0.0).astype(BF16)


def _compress(kc, vc, posk, posv, w1k, w1v, w2k, w2v, B, S):
    nrow = S // CMP_STRIDE
    assert nrow == LANE
    full = lambda a: pl.BlockSpec(a.shape, lambda b: (0,) * a.ndim)
    src = pl.BlockSpec((S, LANE), lambda b: (b, 0))
    osp = pl.BlockSpec((1, nrow, LANE), lambda b: (b, 0, 0))
    osh = jax.ShapeDtypeStruct((B, nrow, LANE), BF16)
    return pl.pallas_call(
        _compress_kernel,
        grid=(B,),
        in_specs=[src, src, full(posk), full(posv), full(w1k), full(w1v), full(w2k), full(w2v)],
        out_specs=[osp, osp],
        out_shape=[osh, osh],
        compiler_params=_cparams(("parallel",)),
        name="compress",
    )(kc, vc, posk, posv, w1k, w1v, w2k, w2v)


def _bias_from_dist(dist, tab_ref, h):
    val = jnp.full(dist.shape, tab_ref[h, 0], F32)
    for b in range(1, REL_BUCKETS):
        val = jnp.where(dist >= _THR[b], tab_ref[h, b], val)
    return val


def _bias_near_kernel(tab_ref, out_ref):
    h = pl.program_id(0)
    j = lax.broadcasted_iota(jnp.int32, (TQ, TQ), 0)
    i = lax.broadcasted_iota(jnp.int32, (TQ, TQ), 1)
    for d in range(2):
        bias = (_bias_from_dist(i - j + d * TQ, tab_ref, h) - tab_ref[h, REL_BUCKETS - 1]) * LOG2E
        out_ref[0, d] = jnp.where(i - j + d * TQ >= 0, bias, NEG)


def _bias_cmp_kernel(tab_ref, out_ref):
    h = pl.program_id(0)
    c = lax.broadcasted_iota(jnp.int32, (LANE, TQ), 0)
    t = pl.program_id(1) * TQ + lax.broadcasted_iota(jnp.int32, (LANE, TQ), 1)
    out_ref[0] = _bias_from_dist(t - (c * CMP_STRIDE + CMP_LEN - 1), tab_ref, h) * LOG2E


def _bias_tiles(tab_t, S):
    smem = pl.BlockSpec(memory_space=pltpu.SMEM)
    near = pl.pallas_call(
        _bias_near_kernel,
        grid=(N_REL_HEADS,),
        in_specs=[smem],
        out_specs=pl.BlockSpec((1, 2, TQ, TQ), lambda h: (h, 0, 0, 0)),
        out_shape=jax.ShapeDtypeStruct((N_REL_HEADS, 2, TQ, TQ), F32),
        compiler_params=_cparams(("parallel",)),
        name="bias_near",
    )(tab_t)
    cmp_bias = pl.pallas_call(
        _bias_cmp_kernel,
        grid=(NSA_HEADS, S // TQ),
        in_specs=[smem],
        out_specs=pl.BlockSpec((1, LANE, TQ), lambda h, q: (h, 0, q)),
        out_shape=jax.ShapeDtypeStruct((NSA_HEADS, LANE, S), F32),
        compiler_params=_cparams(("parallel", "parallel")),
        name="bias_cmp",
    )(tab_t)
    return near, cmp_bias


ACC_ROWS = NSA_HD + 16


def _with_ones(vt):
    return jnp.concatenate([vt, jnp.ones((ACC_ROWS - vt.shape[0], vt.shape[1]), BF16)], axis=0)


def _probs(s, m):
    return jnp.exp2((s - m).astype(BF16))


def _flash_first(ss, vts):
    ms = [jnp.max(s, axis=0, keepdims=True) for s in ss]
    ps = [_probs(s, m) for s, m in zip(ss, ms)]
    return tuple((m, _dot(vt, p)) for m, p, vt in zip(ms, ps, vts))


def _flash_update(ss, vts, sts):
    ms = [jnp.maximum(st[0], jnp.max(s, axis=0, keepdims=True)) for s, st in zip(ss, sts)]
    alphas = [jnp.exp2(st[0] - m) for m, st in zip(ms, sts)]
    ps = [_probs(s, m) for s, m in zip(ss, ms)]
    return tuple((m, a * st[1] + _dot(vt, p)) for m, a, st, p, vt in zip(ms, alphas, sts, ps, vts))


def _flash_out(st):
    acc = st[1]
    return acc[0:NSA_HD, :] * (1.0 / acc[NSA_HD:NSA_HD + 1, :])


def _stream_scratch(n):
    return ([pltpu.VMEM((8, TQ), F32), pltpu.VMEM((ACC_ROWS, TQ), F32)] * n
            + [pltpu.VMEM((TQ, TQ), F32)] * (2 * n) + [pltpu.VMEM((TQ, TQ), BF16)] * (2 * n))


def _stream_refs(refs, n):
    st_refs = tuple((refs[2 * c], refs[2 * c + 1]) for c in range(n))
    s, p = refs[2 * n:4 * n], refs[4 * n:6 * n]
    return st_refs, tuple(s[:n]), tuple(s[n:]), tuple(p[:n]), tuple(p[n:])


def _causal_stream(qi, scores, vtile, st_refs, s_a, s_b, p_a, p_b, diag_ready=False):
    n_far = jnp.maximum(qi - 1, 0)
    top = qi - 2

    chains = range(len(st_refs))

    def write(s_refs, kt, near=2):
        for c in chains:
            s_refs[c][...] = scores(jnp.maximum(kt, 0), near, c)

    def pending(p_refs, kt):
        return [_dot(vtile(jnp.maximum(kt, 0), c), p_refs[c][...]) for c in chains]

    def step(s_cur, p_cur, s_next, kt_next, p_prev, kt_prev, anchor=None):
        rescale = []
        for c in chains:
            nxt = scores(jnp.maximum(kt_next, 0), 2, c)
            s_next[c][...] = nxt
            pv = _dot(vtile(jnp.maximum(kt_prev, 0), c), p_prev[c][...])
            st = st_refs[c][0]
            s = s_cur[c][...]
            m_old = st[0:1, :]
            m = jnp.maximum(m_old, jnp.max(s, axis=0, keepdims=True))
            if anchor is not None:
                m = m + anchor
            anchor = nxt[0:1, :] * 0.0
            p_cur[c][...] = _probs(s, m)
            st[0:1, :] = m
            rescale.append((jnp.exp2(m_old - m), pv))
        for c in chains:
            acc = st_refs[c][1]
            acc[...] = rescale[c][0] * (acc[...] + rescale[c][1])
        return anchor

    if not diag_ready:
        write(s_a, qi, 0)
    for c in chains:
        s_b[c][...] = scores(jnp.maximum(qi - 1, 0), 1, c)
        st, acc = st_refs[c]
        s = s_a[c][...]
        m = jnp.max(s, axis=0, keepdims=True)
        p_a[c][...] = _probs(s, m)
        st[0:1, :] = m
        acc[...] = jnp.zeros(acc.shape, F32)
    step(s_b, p_b, s_a, top, p_a, qi)

    def pair(j, c):
        kt = top - 2 * j
        anchor = step(s_a, p_a, s_b, kt - 1, p_b, kt + 1)
        step(s_b, p_b, s_a, kt - 2, p_a, kt, anchor)
        return c
    lax.fori_loop(0, n_far // 2, pair, 0)

    @pl.when(n_far % 2 == 1)
    def _():
        for c, pv in zip(chains, pending(p_b, 1)):
            st, acc = st_refs[c]
            s = s_a[c][...]
            m_old = st[0:1, :]
            m = jnp.maximum(m_old, jnp.max(s, axis=0, keepdims=True))
            st[0:1, :] = m
            acc[...] = jnp.exp2(m_old - m) * (acc[...] + pv) + _dot(vtile(0, c), _probs(s, m))

    @pl.when(n_far % 2 == 0)
    def _():
        last = jnp.where(n_far == 0, qi - 1, 0)
        for c, pv in zip(chains, pending(p_b, last)):
            acc = st_refs[c][1]
            acc[...] = acc[...] + pv
    return tuple((st[0:1, :], acc[...]) for st, acc in st_refs)


def _ktile(ref, kt):
    return ref[pl.ds(pl.multiple_of(kt * TQ, TQ), TQ), :]


def _nsa_kernel(qa_ref, gt_ref, kcmp_ref, vcmpt_ref, bc_ref, ks_ref, vst_ref, kw_ref, vwt_ref, dn_ref,
                ov_ref, o_ref, psum_ref, sel_ref, oacc_ref, sig_ref, *scratch):
    st_refs, s_a, s_b, p_a, p_b = _stream_refs(scratch, NSA_HEADS)
    qi = pl.program_id(1)
    t0 = qi * TQ
    sub_grp = lax.shift_right_arithmetic(lax.broadcasted_iota(jnp.int32, (LANE, 1), 0), 6)
    sig_ref[...] = jax.nn.sigmoid(gt_ref[...])

    def gate_row(c):
        return sig_ref[pl.ds(c, 1), :]

    def masked_q(r, g):
        return jnp.where(sub_grp == g, qa_ref[r].astype(F32), 0.0).astype(BF16)

    n_cmp = kcmp_ref.shape[1] - 1
    crow = lax.broadcasted_iota(jnp.int32, (LANE, 1), 0)
    cmp_end = jnp.where(crow < n_cmp, crow * CMP_STRIDE + CMP_LEN - 1, 1 << 30)
    mask_c = (t0 + lax.broadcasted_iota(jnp.int32, (1, TQ), 1)) >= cmp_end

    heads = [(r, g) for r in range(NSA_R) for g in range(NSA_G)]
    scores_c = [_dot(kcmp_ref[0], masked_q(r, g)) for r, g in heads]
    probs_c = []
    for (r, g), s in zip(heads, scores_c):
        s = jnp.where(mask_c, s + bc_ref[g * NSA_R + r], NEG)
        p = jnp.where(mask_c, jnp.exp2(s - jnp.max(s, axis=0, keepdims=True)), 0.0)
        l = jnp.sum(p, axis=0, keepdims=True)
        probs_c.append(p * jnp.where(l > 0.0, 1.0 / l, 0.0))
    for g in range(NSA_G):
        psum_ref[g] = sum(p for (r, gg), p in zip(heads, probs_c) if gg == g)
    outs_c = [_dot(vcmpt_ref[0], p.astype(BF16)) for p in probs_c]
    for r in range(NSA_R):
        o0, o1 = outs_c[r * NSA_G], outs_c[r * NSA_G + 1]
        oacc_ref[r] = jnp.where(sub_grp == 0, gate_row(r * 3) * o0, gate_row((NSA_R + r) * 3) * o1)

    has_prev = jnp.where(qi >= 1, 0.0, NEG)
    has_wfar = jnp.where(qi >= WINDOW // TQ, 0.0, NEG)
    kt_prev = jnp.maximum(qi - 1, 0)
    kt_wfar = jnp.maximum(qi - WINDOW // TQ, 0)
    wfar_mask = lax.broadcasted_iota(jnp.int32, (TQ, TQ), 0) > lax.broadcasted_iota(jnp.int32, (TQ, TQ), 1)

    def attend_all_heads():
        chains = [(r, g) for r in range(NSA_R) for g in range(NSA_G)]
        qms = [masked_q(r, g) for r, g in chains]
        hids = [g * NSA_R + r for r, g in chains]

        per_tile = TQ // SEL_BLOCK
        key_blk = lax.shift_right_arithmetic(lax.broadcasted_iota(jnp.int32, (TQ, LANE), 0), 6)
        lane_id = lax.broadcasted_iota(jnp.int32, (TQ, LANE), 1)

        def keys_with_block_onehot(kt, g):
            other = (1 - g) * NSA_HD
            onehot = jnp.where(lane_id - other == key_blk, 1.0, 0.0).astype(BF16)
            in_other = (lane_id >= other) & (lane_id < other + NSA_HD)
            return jnp.where(in_other, onehot, _ktile(ks_ref, kt))

        def query_with_mask_rows(kt, c):
            r, g = chains[c]
            own = qa_ref[r, g * NSA_HD:(g + 1) * NSA_HD, :]
            pair = sel_ref[g, pl.ds(pl.multiple_of((kt // 2) * 2 * per_tile, 2 * per_tile), 2 * per_tile), :]
            rows = jnp.where(kt % 2 == 0, pair[0:per_tile], pair[per_tile:2 * per_tile]).astype(BF16)
            other = jnp.concatenate([rows, jnp.zeros((NSA_HD - per_tile, TQ), BF16)], axis=0)
            return jnp.concatenate([own, other] if g == 0 else [other, own], axis=0)

        def near_bias(s, near, c):
            if near == 0:
                return s + dn_ref[hids[c], 0]
            if near == 1:
                return s + dn_ref[hids[c], 1] + has_prev
            return s

        def group_values(ref, kt, c):
            g = chains[c][1]
            return _with_ones(ref[g * NSA_HD:(g + 1) * NSA_HD, pl.ds(pl.multiple_of(kt * TQ, TQ), TQ)])

        def finish(sts, branch):
            for c, (r, g) in enumerate(chains):
                oacc_ref[r, g * NSA_HD:(g + 1) * NSA_HD, :] += gate_row(hids[c] * 3 + branch) * _flash_out(sts[c])

        def slc_scores(kt, near, c):
            return near_bias(_dot(keys_with_block_onehot(kt, chains[c][1]), query_with_mask_rows(kt, c)), near, c)

        def slc_vtile(kt, c):
            return group_values(vst_ref, kt, c)

        win_tiles = ((qi, 0), (kt_prev, 1), (kt_wfar, 2))

        def win_scores(c):
            out = []
            for kt, near in win_tiles:
                s = near_bias(_dot(_ktile(kw_ref, kt), qms[c]), near, c)
                out.append(jnp.where(wfar_mask, s + has_wfar, NEG) if near == 2 else s)
            return out

        sts = []
        ahead = win_scores(0)
        for c in range(len(chains)):
            tiles, ahead = ahead, (win_scores(c + 1) if c + 1 < len(chains) else None)
            st = _flash_first([tiles[0]], [group_values(vwt_ref, qi, c)])
            st = _flash_update([tiles[1]], [group_values(vwt_ref, kt_prev, c)], st)
            sts.append(_flash_update([tiles[2]], [group_values(vwt_ref, kt_wfar, c)], st)[0])
        finish(sts, 2)

        n_sel = ov_ref.shape[0]
        jj = lax.broadcasted_iota(jnp.int32, (n_sel, TQ), 0)
        cur = lax.shift_right_arithmetic(t0 + lax.broadcasted_iota(jnp.int32, (n_sel, TQ), 1), 6)
        valid = jj <= cur
        forced = (jj == 0) | (cur - jj < SEL_FORCED_LOCAL)
        for g in range(NSA_G):
            ps = psum_ref[g]
            hi = ps.astype(BF16)
            rem = ps - hi.astype(F32)
            mid = rem.astype(BF16)
            lo = (rem - mid.astype(F32)).astype(BF16)
            ov = ov_ref[...]
            imp = _dot(ov, hi) + _dot(ov, mid) + _dot(ov, lo)
            score = jnp.where(valid, jnp.where(forced, 1e9, imp), -1e9)
            cnt = jnp.zeros((n_sel, TQ), F32)
            for j2 in range(n_sel):
                row = score[j2:j2 + 1, :]
                tie = jnp.where(jj > j2, 1.0, 0.0)
                cnt = cnt + jnp.where(row > score, 1.0, jnp.where(row == score, tie, 0.0))
            sel_ref[g] = jnp.where(cnt < float(min(SEL_TOPN, n_sel)), jnp.where(score > -1e8, 0.0, NEG), NEG)

        finish(_causal_stream(qi, slc_scores, slc_vtile, st_refs, s_a, s_b, p_a, p_b), 1)
    attend_all_heads()

    for r in range(NSA_R):
        o_ref[r] = oacc_ref[r].T.astype(BF16)


def _nsa(qa, gates_t, kcmp, vcmpt, bias_c, ks, vst, kw, vwt, near_a, ov, B, S):
    nq = S // TQ
    T = B * S
    k_spec = pl.BlockSpec((S, LANE), lambda b, q: (b, 0))
    v_spec = pl.BlockSpec((LANE, S), lambda b, q: (0, b))
    full = lambda a: pl.BlockSpec(a.shape, lambda b, q: (0,) * a.ndim)
    cmp_spec = pl.BlockSpec((1, LANE, LANE), lambda b, q: (b, 0, 0))
    qo = pl.BlockSpec((4, TQ, LANE), lambda b, q: (0, b * nq + q, 0))
    return pl.pallas_call(
        _nsa_kernel,
        grid=(B, nq),
        in_specs=[pl.BlockSpec((4, LANE, TQ), lambda b, q: (0, 0, b * nq + q)),
                  pl.BlockSpec((GATE_ROWS, TQ), lambda b, q: (0, b * nq + q)),
                  cmp_spec, cmp_spec,
                  pl.BlockSpec((NSA_HEADS, LANE, TQ), lambda b, q: (0, 0, q)),
                  k_spec, v_spec, k_spec, v_spec, full(near_a), full(ov)],
        out_specs=qo,
        out_shape=jax.ShapeDtypeStruct((4, T, LANE), BF16),
        scratch_shapes=[pltpu.VMEM((NSA_G, LANE, TQ), F32), pltpu.VMEM((NSA_G, S // SEL_BLOCK, TQ), F32),
                        pltpu.VMEM((NSA_R, LANE, TQ), F32), pltpu.VMEM((GATE_ROWS, TQ), F32)]
                       + _stream_scratch(NSA_HEADS),
        compiler_params=_cparams(("parallel", "arbitrary")),
        name="nsa",
    )(qa, gates_t, kcmp, vcmpt, bias_c, ks, vst, kw, vwt, near_a, ov)


def _diff_kernel(lq1_ref, lk1_ref, lq2_ref, lk2_ref, sub_ref, qb_ref, kb_ref, vbt_ref, dn_ref, o_ref, *scratch):
    n = DIFF_PAIRS_PER_STEP * 4
    st_refs, s_a, s_b, p_a, p_b = _stream_refs(scratch, n)
    qi = pl.program_id(1)
    chain_of_row = lax.shift_right_arithmetic(lax.broadcasted_iota(jnp.int32, (LANE, 1), 0), 5)
    lam = (jnp.exp(jnp.sum(lq1_ref[...] * lk1_ref[...], axis=-1, keepdims=True))
           - jnp.exp(jnp.sum(lq2_ref[...] * lk2_ref[...], axis=-1, keepdims=True)) + LAMBDA_INIT)
    has_prev = jnp.where(qi >= 1, 0.0, NEG)
    n_steps = DIFF_HEADS // 2 // DIFF_PAIRS_PER_STEP

    def score_fn(step):
        pairs = [step * DIFF_PAIRS_PER_STEP + c // 4 for c in range(n)]
        qs = [qb_ref[step * DIFF_PAIRS_PER_STEP + i].astype(F32) for i in range(DIFF_PAIRS_PER_STEP)]
        qms = [jnp.where(chain_of_row == c % 4, qs[c // 4], 0.0).astype(BF16) for c in range(n)]

        def scores(kt, near, c):
            s = _dot(kb_ref[pairs[c], pl.ds(pl.multiple_of(kt * TQ, TQ), TQ), :], qms[c])
            if near == 2:
                return s
            bias = dn_ref[2 * pairs[c] + (c % 4) // 2, near]
            return s + bias if near == 0 else s + bias + has_prev
        return scores

    def start_diag(step):
        scores = score_fn(step)
        for c in range(n):
            s_a[c][...] = scores(qi, 0, c)

    start_diag(0)

    def step_body(step, carry):
        def vtile(kt, c):
            hh = (c % 4) // 2
            return _with_ones(vbt_ref[step * DIFF_PAIRS_PER_STEP + c // 4, hh * 2 * DIFF_HD:(hh + 1) * 2 * DIFF_HD,
                                      pl.ds(pl.multiple_of(kt * TQ, TQ), TQ)])

        sts = _causal_stream(qi, score_fn(step), vtile, st_refs, s_a, s_b, p_a, p_b, diag_ready=True)
        start_diag(jnp.minimum(step + 1, n_steps - 1))
        for i in range(DIFF_PAIRS_PER_STEP):
            outs = []
            for hh in range(2):
                c = 4 * i + 2 * hh
                o = _flash_out(sts[c]) - lam * _flash_out(sts[c + 1])
                outs.append(o * lax.rsqrt(jnp.mean(o * o, axis=0, keepdims=True) + EPS))
            out = jnp.concatenate(outs, axis=0) * sub_ref[...] * (1.0 - LAMBDA_INIT)
            o_ref[step * DIFF_PAIRS_PER_STEP + i] = out.T.astype(BF16)
        return carry
    lax.fori_loop(0, n_steps, step_body, 0)


def _diff(lq1, lk1, lq2, lk2, sub, qb, kb, vbt, near_b, B, S):
    nq = S // TQ
    T = B * S
    full = lambda a: pl.BlockSpec(a.shape, lambda b, q: (0,) * a.ndim)
    k_spec = pl.BlockSpec((4, S, LANE), lambda b, q: (0, b, 0))
    v_spec = pl.BlockSpec((4, LANE, S), lambda b, q: (0, 0, b))
    qo = pl.BlockSpec((4, TQ, LANE), lambda b, q: (0, b * nq + q, 0))
    return pl.pallas_call(
        _diff_kernel,
        grid=(B, nq),
        in_specs=[full(lq1), full(lk1), full(lq2), full(lk2), full(sub),
                  pl.BlockSpec((4, LANE, TQ), lambda b, q: (0, 0, b * nq + q)), k_spec, v_spec, full(near_b)],
        out_specs=qo,
        out_shape=jax.ShapeDtypeStruct((4, T, LANE), BF16),
        scratch_shapes=_stream_scratch(DIFF_PAIRS_PER_STEP * 4),
        compiler_params=_cparams(("parallel", "arbitrary")),
        name="diff",
    )(lq1, lk1, lq2, lk2, sub, qb, kb, vbt, near_b)


def _outproj_kernel(x_ref, oa_ref, ob_ref, w_ref, g_ref, wr_ref, br_ref, xt_ref, route_ref, cnt_ref, carry_ref):
    @pl.when(pl.program_id(0) == 0)
    def _():
        carry_ref[...] = jnp.zeros(carry_ref.shape, F32)

    o = jnp.concatenate([oa_ref[r] for r in range(4)] + [ob_ref[r] for r in range(4)], axis=-1)
    h = x_ref[...] + _dot(o, w_ref[...])
    xt_ref[:, 0:D_MODEL] = h
    tn = _rmsnorm(h, g_ref[...]).astype(BF16)
    logits = _dot(tn, wr_ref[...]) + br_ref[...]
    lane = lax.broadcasted_iota(jnp.int32, (1, LANE), 1)
    lane_f = lane.astype(F32)
    is_grp = lane < MOE_GROUPS
    lg = jnp.where(is_grp, logits, NEG)
    mg = jnp.max(lg, axis=-1, keepdims=True)
    zg = jnp.sum(jnp.where(is_grp, jnp.exp(lg - mg), 0.0), axis=-1, keepdims=True)
    g_prob = 1.0 / zg
    g_idx = jnp.min(jnp.where(lg == mg, lane_f, 1e9), axis=-1, keepdims=True)
    lane_grp = jnp.where((lane >= MOE_GROUPS) & (lane < MOE_GROUPS + N_EXPERTS),
                         lax.shift_right_arithmetic(lane - MOE_GROUPS, 3), -1).astype(F32)
    le = jnp.where(lane_grp == g_idx, logits, NEG)
    m1 = jnp.max(le, axis=-1, keepdims=True)
    e1 = jnp.min(jnp.where(le == m1, lane_f, 1e9), axis=-1, keepdims=True)
    le2 = jnp.where(lane_f == e1, NEG, le)
    m2 = jnp.max(le2, axis=-1, keepdims=True)
    e2 = jnp.min(jnp.where(le2 == m2, lane_f, 1e9), axis=-1, keepdims=True)
    ratio = jnp.exp(m2 - m1)
    w1 = g_prob / (1.0 + ratio)
    w2 = w1 * ratio
    xt_ref[:, D_MODEL:D_MODEL + LANE] = jnp.where(lane_f == e1, w1, 0.0) + jnp.where(lane_f == e2, w2, 0.0)

    tm = h.shape[0]
    onehot = jnp.where(lane_f == g_idx, 1.0, 0.0)
    earlier = jnp.where(lax.broadcasted_iota(jnp.int32, (tm, tm), 0) > lax.broadcasted_iota(jnp.int32, (tm, tm), 1),
                        1.0, 0.0).astype(BF16)
    prefix = _dot(earlier, onehot.astype(BF16)) + carry_ref[...]
    rank = jnp.sum(onehot * prefix, axis=-1, keepdims=True)
    carry_ref[...] += jnp.sum(onehot, axis=0, keepdims=True)
    route_ref[...] = jnp.where(lane == 0, g_idx, jnp.where(lane == 1, rank, 0.0))
    cnt_ref[...] = jnp.broadcast_to(carry_ref[...], cnt_ref.shape)


def _out_proj(x2, oa, ob, w, g, wr, br):
    T = x2.shape[0]
    tm = TM_PROJ
    row = lambda i: (i, 0)
    full = lambda a: pl.BlockSpec(a.shape, lambda i: (0,) * a.ndim)
    o4 = pl.BlockSpec((4, tm, LANE), lambda i: (0, i, 0))
    return pl.pallas_call(
        _outproj_kernel,
        grid=(T // tm,),
        in_specs=[pl.BlockSpec((tm, D_MODEL), row), o4, o4, full(w), full(g), full(wr), full(br)],
        out_specs=[pl.BlockSpec((tm, XT_WIDTH), row), pl.BlockSpec((tm, LANE), row),
                   pl.BlockSpec((8, LANE), lambda i: (0, 0))],
        out_shape=[jax.ShapeDtypeStruct((T, XT_WIDTH), F32), jax.ShapeDtypeStruct((T, LANE), F32),
                   jax.ShapeDtypeStruct((8, LANE), F32)],
        scratch_shapes=[pltpu.VMEM((1, LANE), F32)],
        compiler_params=_cparams(("arbitrary",)),
        name="out_proj",
    )(x2, oa, ob, w, g, wr, br)


def _moe_kernel(tg_ref, ok_ref, idx_ref, idxn_ref, xt_hbm, g_ref, wg_ref, wu_ref, wd_ref, y_ref, xbuf, sem,
                xb_ref, cmb_ref):
    i = pl.program_id(0)
    slot = i % 2
    tm = xbuf.shape[1]

    def row_copy(index_ref, r, dst_slot):
        return pltpu.make_async_copy(xt_hbm.at[pl.ds(index_ref[0, 0, r], 1), :],
                                     xbuf.at[dst_slot, pl.ds(r, 1), :], sem.at[dst_slot])

    @pl.when(i == 0)
    def _():
        def body(r, c):
            row_copy(idx_ref, r, 0).start()
            return c
        lax.fori_loop(0, tm, body, 0)

    @pl.when((i == 0) | (ok_ref[jnp.maximum(i - 1, 0)] == 1))
    def _():
        pltpu.make_async_copy(xt_hbm.at[pl.ds(0, tm), :], xbuf.at[slot], sem.at[slot]).wait()

    @pl.when(ok_ref[i] == 0)
    def _():
        y_ref[...] = jnp.zeros(y_ref.shape, F32)

    @pl.when(ok_ref[i] == 1)
    def _():
        lane = lax.broadcasted_iota(jnp.int32, (1, LANE), 1)
        xb_ref[...] = _rmsnorm(xbuf[slot, :, 0:D_MODEL], g_ref[...]).astype(BF16)
        cmb_ref[...] = xbuf[slot, :, D_MODEL:D_MODEL + LANE]
        first_lane = MOE_GROUPS + tg_ref[i] * EPG
        per_expert = -(-tm // (EPG - 1))
        y = jnp.zeros((tm, D_MODEL), F32)
        issued = jnp.int32(0)
        for e in range(EPG):
            x = xb_ref[...]
            a = _dot(x, wg_ref[0, e])
            b = _dot(x, wu_ref[0, e])
            ce = jnp.sum(jnp.where(lane == first_lane + e + issued, cmb_ref[...], 0.0), axis=-1, keepdims=True)
            y = y + _dot(((a * jax.nn.sigmoid(a)) * b * ce).astype(BF16), wd_ref[0, e])
            for r in range(e * per_expert, min((e + 1) * per_expert, tm)):
                row_copy(idxn_ref, r, 1 - slot).start(priority=r % 2)
            if e in ANCHOR_AFTER:
                issued = jnp.where((pltpu.semaphore_read(sem.at[1 - slot]) | 1) == 0, 1, 0)
        for blk in range(ROW_SLABS):
            y_ref[pl.ds(blk, tm, stride=ROW_SLABS), :] = y[:, blk * LANE:(blk + 1) * LANE]


def _moe(tile_group, tile_ok, src3, xt, g, wg, wu, wd):
    n_tiles, _, tm = src3.shape
    last = n_tiles - 1
    grid_spec = pltpu.PrefetchScalarGridSpec(
        num_scalar_prefetch=2,
        grid=(n_tiles,),
        in_specs=[pl.BlockSpec((1, 1, tm), lambda i, tg, ok: (i, 0, 0), memory_space=pltpu.SMEM),
                  pl.BlockSpec((1, 1, tm), lambda i, tg, ok: (jnp.minimum(i + 1, last), 0, 0), memory_space=pltpu.SMEM),
                  pl.BlockSpec(memory_space=pl.ANY),
                  pl.BlockSpec((1, D_MODEL), lambda i, tg, ok: (0, 0)),
                  pl.BlockSpec((1, EPG, D_MODEL, EXPERT_FF), lambda i, tg, ok: (tg[i], 0, 0, 0)),
                  pl.BlockSpec((1, EPG, D_MODEL, EXPERT_FF), lambda i, tg, ok: (tg[i], 0, 0, 0)),
                  pl.BlockSpec((1, EPG, EXPERT_FF, D_MODEL), lambda i, tg, ok: (tg[i], 0, 0, 0))],
        out_specs=pl.BlockSpec((tm * ROW_SLABS, LANE), lambda i, tg, ok: (i, 0)),
        scratch_shapes=[pltpu.VMEM((2, tm, XT_WIDTH), F32), pltpu.SemaphoreType.DMA((2,)),
                        pltpu.VMEM((tm, D_MODEL), BF16), pltpu.VMEM((tm, LANE), F32)],
    )
    return pl.pallas_call(
        _moe_kernel,
        grid_spec=grid_spec,
        out_shape=jax.ShapeDtypeStruct((n_tiles * tm * ROW_SLABS, LANE), F32),
        compiler_params=pltpu.CompilerParams(dimension_semantics=("arbitrary",), vmem_limit_bytes=VMEM_LIMIT_MOE),
        name="moe",
    )(tile_group, tile_ok, src3, src3, xt, g, wg, wu, wd)


def _final_kernel(pos_ref, posn_ref, y_hbm, h_ref, gf_ref, o_ref, ybuf, sem):
    i = pl.program_id(0)
    slot = i % 2
    tm = h_ref.shape[0]

    def issue(index_ref, dst_slot):
        group = 8

        def body(j, c):
            for k in range(group):
                r = j * group + k
                src_row = pl.multiple_of(index_ref[0, 0, r], ROW_SLABS)
                pltpu.make_async_copy(y_hbm.at[pl.ds(src_row, ROW_SLABS), :],
                                      ybuf.at[dst_slot, pl.ds(r * ROW_SLABS, ROW_SLABS), :],
                                      sem.at[dst_slot]).start(priority=k % 2)
            return c
        lax.fori_loop(0, tm // group, body, 0)

    @pl.when(i == 0)
    def _():
        issue(pos_ref, 0)

    @pl.when(i + 1 < pl.num_programs(0))
    def _():
        issue(posn_ref, 1 - slot)

    pltpu.make_async_copy(ybuf.at[slot], ybuf.at[slot], sem.at[slot]).wait()
    y = jnp.concatenate([ybuf[slot, pl.ds(blk, tm, stride=ROW_SLABS), :] for blk in range(ROW_SLABS)], axis=-1)
    h = h_ref[...] + y
    o_ref[...] = h * lax.rsqrt(jnp.mean(h * h, axis=-1, keepdims=True) + EPS) * gf_ref[...]


def _final(pos3, y_sorted, h, gf):
    n_tiles, _, tm = pos3.shape
    last = n_tiles - 1
    row = lambda i: (i, 0)
    return pl.pallas_call(
        _final_kernel,
        grid=(n_tiles,),
        in_specs=[pl.BlockSpec((1, 1, tm), lambda i: (i, 0, 0), memory_space=pltpu.SMEM),
                  pl.BlockSpec((1, 1, tm), lambda i: (jnp.minimum(i + 1, last), 0, 0), memory_space=pltpu.SMEM),
                  pl.BlockSpec(memory_space=pl.ANY),
                  pl.BlockSpec((tm, D_MODEL), row),
                  pl.BlockSpec((1, D_MODEL), lambda i: (0, 0))],
        out_specs=pl.BlockSpec((tm, D_MODEL), row),
        out_shape=jax.ShapeDtypeStruct((n_tiles * tm, D_MODEL), F32),
        scratch_shapes=[pltpu.VMEM((2, tm * ROW_SLABS, LANE), F32), pltpu.SemaphoreType.DMA((2,))],
        compiler_params=_cparams(("arbitrary",)),
        name="final",
    )(pos3, pos3, y_sorted, h, gf)


def _qa_perm():
    new = np.arange(NSA_HEADS * NSA_HD)
    r, g, d = new // LANE, (new % LANE) // NSA_HD, new % NSA_HD
    return (g * NSA_R + r) * NSA_HD + d


def _block_diag2(w):
    z = jnp.zeros_like(w)
    return jnp.concatenate([jnp.concatenate([w, z], axis=-1), jnp.concatenate([z, w], axis=-1)], axis=-2)


def kernel(x, rel_bias, ln_mix, w_in, cmp_pos_k, cmp_pos_v, cmp_k_w1, cmp_k_w2, cmp_v_w1, cmp_v_w2,
           diff_lq1, diff_lk1, diff_lq2, diff_lk2, diff_subln, w_out, ln_ffn,
           router_group_w, router_group_b, router_expert_w, router_expert_b,
           exp_w_gate, exp_w_up, exp_w_down, ln_final):
    B, S, D = x.shape
    T = B * S
    assert D == D_MODEL and S % TQ == 0 and S >= WINDOW and T % TM_MOE == 0 and T % TM_FINAL == 0
    x2 = x.reshape(T, D)
    perm = _qa_perm()

    w = w_in[0]
    c_kc, c_vc, c_ks, c_vs, c_kw, c_vw, c_gt = 512, 640, 768, 896, 1024, 1152, 1280
    c_qb = c_gt + N_GATE
    c_kb, c_vb = c_qb + 512, c_qb + 1024
    col = lambda c, n=LANE: w[:, c:c + n]
    w_tok = jnp.concatenate([col(c_kc), col(c_vc), col(c_ks), col(c_kw), col(c_kb, 512)], axis=1).astype(BF16)
    w_feat = jnp.concatenate([w[:, perm], col(c_qb, 512), col(c_vs), col(c_vw), col(c_vb, 512), col(c_gt, N_GATE),
                              jnp.zeros((D, GATE_ROWS - N_GATE), F32)], axis=1).T.astype(BF16)
    qa, kc, vc, ks, kw, qb, kb, vst, vwt, vbt, gates_t = _in_proj(x2, ln_mix[0][None, :], w_tok, w_feat)

    w1k = _block_diag2(cmp_k_w1[0].reshape(CMP_LEN, NSA_HD, CMP_HIDDEN)).astype(BF16)
    w1v = _block_diag2(cmp_v_w1[0].reshape(CMP_LEN, NSA_HD, CMP_HIDDEN)).astype(BF16)
    w2k = _block_diag2(cmp_k_w2[0]).astype(BF16)
    w2v = _block_diag2(cmp_v_w2[0]).astype(BF16)
    posk = jnp.tile(cmp_pos_k[0], (1, NSA_G))
    posv = jnp.tile(cmp_pos_v[0], (1, NSA_G))
    kcmp, vcmpt = _compress(kc, vc, posk, posv, w1k, w1v, w2k, w2v, B, S)

    near, bias_c = _bias_tiles(rel_bias.T, S)

    n_sel = S // SEL_BLOCK
    nrow = S // CMP_STRIDE
    c_start = np.arange(nrow) * CMP_STRIDE
    s_start = np.arange(n_sel) * SEL_BLOCK
    ov = ((c_start[None, :] <= s_start[:, None] + SEL_BLOCK - 1)
          & (c_start[None, :] + CMP_LEN - 1 >= s_start[:, None])
          & (np.arange(nrow)[None, :] < nrow - 1)).astype(np.float32)
    o_a = _nsa(qa, gates_t, kcmp, vcmpt, bias_c, ks, vst, kw, vwt, near[:NSA_HEADS], jnp.asarray(ov, BF16), B, S)

    sub = jnp.tile(diff_subln[0], 2)[:, None]
    o_b = _diff(diff_lq1[0][None, :], diff_lk1[0][None, :], diff_lq2[0][None, :], diff_lk2[0][None, :],
                sub, qb, kb, vbt, near[NSA_HEADS:], B, S)

    w_o = jnp.concatenate([w_out[0][:512][perm], w_out[0][512:]], axis=0).astype(BF16)
    n_r = MOE_GROUPS + N_EXPERTS
    wr = jnp.concatenate([router_group_w[0], router_expert_w[0], jnp.zeros((D, LANE - n_r), F32)], axis=1).astype(BF16)
    br = jnp.concatenate([router_group_b[0], router_expert_b[0], jnp.zeros((LANE - n_r,), F32)])[None, :]
    xt, route, counts = _out_proj(x2, o_a, o_b, w_o, ln_ffn[0][None, :], wr, br)

    tm = TM_MOE
    n_tiles = T // tm + MOE_GROUPS
    cnt = counts[0, :MOE_GROUPS].astype(jnp.int32)
    ends = jnp.cumsum((cnt + tm - 1) // tm * tm)
    starts = ends - (cnt + tm - 1) // tm * tm
    pos = starts[route[:, 0].astype(jnp.int32)] + route[:, 1].astype(jnp.int32)
    src = jnp.zeros((n_tiles * tm,), jnp.int32).at[pos].set(
        jnp.arange(T, dtype=jnp.int32), unique_indices=True, mode="promise_in_bounds")
    tile_start = jnp.arange(n_tiles, dtype=jnp.int32) * tm
    tile_group = jnp.minimum(jnp.searchsorted(ends, tile_start, side="right"), MOE_GROUPS - 1).astype(jnp.int32)
    tile_ok = (tile_start < ends[-1]).astype(jnp.int32)

    by_group = lambda a: a[0].astype(BF16).reshape((MOE_GROUPS, EPG) + a.shape[2:])
    y_sorted = _moe(tile_group, tile_ok, src.reshape(n_tiles, 1, tm), xt, ln_ffn[0][None, :],
                    by_group(exp_w_gate), by_group(exp_w_up), by_group(exp_w_down))
    out = _final((pos * ROW_SLABS).reshape(T // TM_FINAL, 1, TM_FINAL), y_sorted, xt, ln_final[None, :])
    return out.reshape(B, S, D)
```

```python
import math

import numpy as np
import jax
import jax.numpy as jnp
from jax import lax
from jax.experimental import pallas as pl
from jax.experimental.pallas import tpu as pltpu

F32 = jnp.float32
BF16 = jnp.bfloat16
NEG = -1e30
EPS = 1e-6
LOG2E = math.log2(math.e)

D_MODEL = 1024
LANE = 128
NSA_HEADS, NSA_G, NSA_R, NSA_HD = 8, 2, 4, 64
CMP_LEN, CMP_STRIDE, CMP_HIDDEN = 32, 16, 128
SEL_BLOCK, SEL_TOPN, SEL_FORCED_LOCAL, WINDOW = 64, 8, 2, 512
DIFF_HEADS, DIFF_HD = 8, 32
REL_BUCKETS, REL_MAX_EXACT, REL_MAX_DIST = 32, 16, 128
N_REL_HEADS = NSA_HEADS + DIFF_HEADS
MOE_GROUPS, EPG, N_EXPERTS, EXPERT_FF = 4, 8, 32, 256
LAMBDA_INIT = 0.8 - 0.6 * math.exp(-0.3 * 0)
N_GATE = NSA_HEADS * 3
GATE_ROWS = 32

TQ = 256
DIFF_PAIRS_PER_STEP = 4
TM_PROJ = 512
TM_MOE = 512
TM_FINAL = 512
ANCHOR_AFTER = (1, 3, 5, 6)
XT_WIDTH = D_MODEL + LANE
ROW_SLABS = D_MODEL // LANE
VMEM_LIMIT = 48 * 1024 * 1024
VMEM_LIMIT_MOE = 56 * 1024 * 1024


def _cparams(sem):
    return pltpu.CompilerParams(dimension_semantics=sem, vmem_limit_bytes=VMEM_LIMIT)


def _dot(a, b):
    return jnp.dot(a, b, preferred_element_type=F32)


def _rmsnorm(x, g):
    return x * lax.rsqrt(jnp.mean(x * x, axis=-1, keepdims=True) + EPS) * g


def _dot_nt(a, b):
    return lax.dot_general(a, b, (((1,), (1,)), ((), ())), preferred_element_type=F32)


def _bucket_thresholds():
    n = np.arange(0, REL_MAX_DIST + 1)
    nf = np.maximum(n, 1).astype(np.float32)
    large = REL_MAX_EXACT + (np.log(nf / np.float32(REL_MAX_EXACT)) / np.float32(math.log(REL_MAX_DIST / REL_MAX_EXACT))
                             * np.float32(REL_BUCKETS - REL_MAX_EXACT)).astype(np.int32)
    large = np.minimum(large, REL_BUCKETS - 1)
    bucket = np.where(n < REL_MAX_EXACT, n, large)
    return [int(np.argmax(bucket >= b)) for b in range(REL_BUCKETS)]


_THR = _bucket_thresholds()


def _inproj_kernel(x_ref, g_ref, w_ref, wt_ref, qa_ref, kc_ref, vc_ref, ks_ref, kw_ref, qb_ref, kb_ref,
                   vst_ref, vwt_ref, vbt_ref, gt_ref):
    x = x_ref[...]
    xn = (x * lax.rsqrt(jnp.mean(x * x, axis=-1, keepdims=True) + EPS) * g_ref[...]).astype(BF16)
    a = _dot(xn, w_ref[:, 0:512])
    kc_ref[...] = a[:, 0:128]
    vc_ref[...] = a[:, 128:256]
    ks_ref[...] = a[:, 256:384].astype(BF16)
    kw_ref[...] = a[:, 384:512].astype(BF16)
    a = _dot(xn, w_ref[:, 512:1024])
    for r in range(4):
        kb_ref[r] = a[:, r * LANE:(r + 1) * LANE].astype(BF16)

    ft = _dot_nt(wt_ref[...], xn)

    def feat(row0, rows=LANE):
        return ft[row0:row0 + rows, :]
    for r in range(4):
        qa_ref[r] = (feat(r * LANE) * (NSA_HD ** -0.5 * LOG2E)).astype(BF16)
        qb_ref[r] = (feat(512 + r * LANE) * (DIFF_HD ** -0.5 * LOG2E)).astype(BF16)
        vbt_ref[r] = feat(1280 + r * LANE).astype(BF16)
    vst_ref[...] = feat(1024).astype(BF16)
    vwt_ref[...] = feat(1152).astype(BF16)
    gt_ref[...] = feat(1792, GATE_ROWS)


def _in_proj(x2, g, w, wt):
    T = x2.shape[0]
    tm = TM_PROJ
    row = lambda i: (i, 0)
    o128b = jax.ShapeDtypeStruct((T, LANE), BF16)
    o128f = jax.ShapeDtypeStruct((T, LANE), F32)
    o4 = jax.ShapeDtypeStruct((4, T, LANE), BF16)
    ot = jax.ShapeDtypeStruct((LANE, T), BF16)
    o4t = jax.ShapeDtypeStruct((4, LANE, T), BF16)
    s128 = pl.BlockSpec((tm, LANE), row)
    s4 = pl.BlockSpec((4, tm, LANE), lambda i: (0, i, 0))
    st = pl.BlockSpec((LANE, tm), lambda i: (0, i))
    s4t = pl.BlockSpec((4, LANE, tm), lambda i: (0, 0, i))
    return pl.pallas_call(
        _inproj_kernel,
        grid=(T // tm,),
        in_specs=[pl.BlockSpec((tm, D_MODEL), row),
                  pl.BlockSpec((1, D_MODEL), lambda i: (0, 0)),
                  pl.BlockSpec(w.shape, lambda i: (0, 0)),
                  pl.BlockSpec(wt.shape, lambda i: (0, 0))],
        out_specs=[s4t, s128, s128, s128, s128, s4t, s4, st, st, s4t, pl.BlockSpec((GATE_ROWS, tm), lambda i: (0, i))],
        out_shape=[o4t, o128f, o128f, o128b, o128b, o4t, o4, ot, ot, o4t, jax.ShapeDtypeStruct((GATE_ROWS, T), F32)],
        compiler_params=_cparams(("parallel",)),
        name="in_proj",
    )(x2, g, w, wt)


def _gelu_tanh(x):
    return 0.5 * x * (1.0 + jnp.tanh(math.sqrt(2.0 / math.pi) * (x + 0.044715 * (x * x * x))))


def _compress_kernel(kc_ref, vc_ref, posk_ref, posv_ref, w1k_ref, w1v_ref, w2k_ref, w2v_ref, ko_ref, vo_ref):
    nrow = kc_ref.shape[0] // CMP_STRIDE
    rid = lax.broadcasted_iota(jnp.int32, (nrow, 1), 0)
    cid = lax.broadcasted_iota(jnp.int32, (1, nrow), 1)
    for src, pos, w1, w2, out, transposed in ((kc_ref, posk_ref, w1k_ref, w2k_ref, ko_ref, False),
                                              (vc_ref, posv_ref, w1v_ref, w2v_ref, vo_ref, True)):
        hid_a = jnp.zeros((nrow, 2 * CMP_HIDDEN), F32)
        hid_b = jnp.zeros((nrow, 2 * CMP_HIDDEN), F32)
        for m in range(CMP_STRIDE):
            y = src[pl.ds(m, nrow, stride=CMP_STRIDE), :]
            hid_a = hid_a + _dot((y + pos[m:m + 1, :]).astype(BF16), w1[m])
            hid_b = hid_b + _dot((y + pos[CMP_STRIDE + m:CMP_STRIDE + m + 1, :]).astype(BF16), w1[CMP_STRIDE + m])
        hid = hid_a + pltpu.roll(hid_b, nrow - 1, 0)
        o = _dot(_gelu_tanh(hid).astype(BF16), w2[...])
        if transposed:
            out[0] = jnp.where(cid < nrow - 1, o.T, 0.0).astype(BF16)
        else:
            out[0] = jnp.where(rid < nrow - 1, o, 0.0).astype(BF16)


def _compress(kc, vc, posk, posv, w1k, w1v, w2k, w2v, B, S):
    nrow = S // CMP_STRIDE
    assert nrow == LANE
    full = lambda a: pl.BlockSpec(a.shape, lambda b: (0,) * a.ndim)
    src = pl.BlockSpec((S, LANE), lambda b: (b, 0))
    osp = pl.BlockSpec((1, nrow, LANE), lambda b: (b, 0, 0))
    osh = jax.ShapeDtypeStruct((B, nrow, LANE), BF16)
    return pl.pallas_call(
        _compress_kernel,
        grid=(B,),
        in_specs=[src, src, full(posk), full(posv), full(w1k), full(w1v), full(w2k), full(w2v)],
        out_specs=[osp, osp],
        out_shape=[osh, osh],
        compiler_params=_cparams(("parallel",)),
        name="compress",
    )(kc, vc, posk, posv, w1k, w1v, w2k, w2v)


def _bias_from_dist(dist, tab_ref, h):
    val = jnp.full(dist.shape, tab_ref[h, 0], F32)
    for b in range(1, REL_BUCKETS):
        val = jnp.where(dist >= _THR[b], tab_ref[h, b], val)
    return val


def _bias_near_kernel(tab_ref, out_ref):
    h = pl.program_id(0)
    j = lax.broadcasted_iota(jnp.int32, (TQ, TQ), 0)
    i = lax.broadcasted_iota(jnp.int32, (TQ, TQ), 1)
    for d in range(2):
        bias = (_bias_from_dist(i - j + d * TQ, tab_ref, h) - tab_ref[h, REL_BUCKETS - 1]) * LOG2E
        out_ref[0, d] = jnp.where(i - j + d * TQ >= 0, bias, NEG)


def _bias_cmp_kernel(tab_ref, out_ref):
    h = pl.program_id(0)
    c = lax.broadcasted_iota(jnp.int32, (LANE, TQ), 0)
    t = pl.program_id(1) * TQ + lax.broadcasted_iota(jnp.int32, (LANE, TQ), 1)
    out_ref[0] = _bias_from_dist(t - (c * CMP_STRIDE + CMP_LEN - 1), tab_ref, h) * LOG2E


def _bias_tiles(tab_t, S):
    smem = pl.BlockSpec(memory_space=pltpu.SMEM)
    near = pl.pallas_call(
        _bias_near_kernel,
        grid=(N_REL_HEADS,),
        in_specs=[smem],
        out_specs=pl.BlockSpec((1, 2, TQ, TQ), lambda h: (h, 0, 0, 0)),
        out_shape=jax.ShapeDtypeStruct((N_REL_HEADS, 2, TQ, TQ), F32),
        compiler_params=_cparams(("parallel",)),
        name="bias_near",
    )(tab_t)
    cmp_bias = pl.pallas_call(
        _bias_cmp_kernel,
        grid=(NSA_HEADS, S // TQ),
        in_specs=[smem],
        out_specs=pl.BlockSpec((1, LANE, TQ), lambda h, q: (h, 0, q)),
        out_shape=jax.ShapeDtypeStruct((NSA_HEADS, LANE, S), F32),
        compiler_params=_cparams(("parallel", "parallel")),
        name="bias_cmp",
    )(tab_t)
    return near, cmp_bias


ACC_ROWS = NSA_HD + 16


def _with_ones(vt):
    return jnp.concatenate([vt, jnp.ones((ACC_ROWS - vt.shape[0], vt.shape[1]), BF16)], axis=0)


def _probs(s, m):
    return jnp.exp2((s - m).astype(BF16))


def _flash_first(ss, vts):
    ms = [jnp.max(s, axis=0, keepdims=True) for s in ss]
    ps = [_probs(s, m) for s, m in zip(ss, ms)]
    return tuple((m, _dot(vt, p)) for m, p, vt in zip(ms, ps, vts))


def _flash_update(ss, vts, sts):
    ms = [jnp.maximum(st[0], jnp.max(s, axis=0, keepdims=True)) for s, st in zip(ss, sts)]
    alphas = [jnp.exp2(st[0] - m) for m, st in zip(ms, sts)]
    ps = [_probs(s, m) for s, m in zip(ss, ms)]
    return tuple((m, a * st[1] + _dot(vt, p)) for m, a, st, p, vt in zip(ms, alphas, sts, ps, vts))


def _flash_out(st):
    acc = st[1]
    return acc[0:NSA_HD, :] * (1.0 / acc[NSA_HD:NSA_HD + 1, :])


def _stream_scratch(n):
    return ([pltpu.VMEM((8, TQ), F32), pltpu.VMEM((ACC_ROWS, TQ), F32)] * n
            + [pltpu.VMEM((TQ, TQ), F32)] * (2 * n) + [pltpu.VMEM((TQ, TQ), BF16)] * (2 * n))


def _stream_refs(refs, n):
    st_refs = tuple((refs[2 * c], refs[2 * c + 1]) for c in range(n))
    s, p = refs[2 * n:4 * n], refs[4 * n:6 * n]
    return st_refs, tuple(s[:n]), tuple(s[n:]), tuple(p[:n]), tuple(p[n:])


def _causal_stream(qi, scores, vtile, st_refs, s_a, s_b, p_a, p_b, diag_ready=False):
    n_far = jnp.maximum(qi - 1, 0)
    top = qi - 2

    chains = range(len(st_refs))

    def write(s_refs, kt, near=2):
        for c in chains:
            s_refs[c][...] = scores(jnp.maximum(kt, 0), near, c)

    def pending(p_refs, kt):
        return [_dot(vtile(jnp.maximum(kt, 0), c), p_refs[c][...]) for c in chains]

    def step(s_cur, p_cur, s_next, kt_next, p_prev, kt_prev, anchor=None):
        rescale = []
        for c in chains:
            nxt = scores(jnp.maximum(kt_next, 0), 2, c)
            s_next[c][...] = nxt
            pv = _dot(vtile(jnp.maximum(kt_prev, 0), c), p_prev[c][...])
            st = st_refs[c][0]
            s = s_cur[c][...]
            m_old = st[0:1, :]
            m = jnp.maximum(m_old, jnp.max(s, axis=0, keepdims=True))
            if anchor is not None:
                m = m + anchor
            anchor = nxt[0:1, :] * 0.0
            p_cur[c][...] = _probs(s, m)
            st[0:1, :] = m
            rescale.append((jnp.exp2(m_old - m), pv))
        for c in chains:
            acc = st_refs[c][1]
            acc[...] = rescale[c][0] * (acc[...] + rescale[c][1])
        return anchor

    if not diag_ready:
        write(s_a, qi, 0)
    for c in chains:
        s_b[c][...] = scores(jnp.maximum(qi - 1, 0), 1, c)
        st, acc = st_refs[c]
        s = s_a[c][...]
        m = jnp.max(s, axis=0, keepdims=True)
        p_a[c][...] = _probs(s, m)
        st[0:1, :] = m
        acc[...] = jnp.zeros(acc.shape, F32)
    step(s_b, p_b, s_a, top, p_a, qi)

    def pair(j, c):
        kt = top - 2 * j
        anchor = step(s_a, p_a, s_b, kt - 1, p_b, kt + 1)
        step(s_b, p_b, s_a, kt - 2, p_a, kt, anchor)
        return c
    lax.fori_loop(0, n_far // 2, pair, 0)

    @pl.when(n_far % 2 == 1)
    def _():
        for c, pv in zip(chains, pending(p_b, 1)):
            st, acc = st_refs[c]
            s = s_a[c][...]
            m_old = st[0:1, :]
            m = jnp.maximum(m_old, jnp.max(s, axis=0, keepdims=True))
            st[0:1, :] = m
            acc[...] = jnp.exp2(m_old - m) * (acc[...] + pv) + _dot(vtile(0, c), _probs(s, m))

    @pl.when(n_far % 2 == 0)
    def _():
        last = jnp.where(n_far == 0, qi - 1, 0)
        for c, pv in zip(chains, pending(p_b, last)):
            acc = st_refs[c][1]
            acc[...] = acc[...] + pv
    return tuple((st[0:1, :], acc[...]) for st, acc in st_refs)


def _ktile(ref, kt):
    return ref[pl.ds(pl.multiple_of(kt * TQ, TQ), TQ), :]


def _nsa_kernel(qa_ref, gt_ref, kcmp_ref, vcmpt_ref, bc_ref, ks_ref, vst_ref, kw_ref, vwt_ref, dn_ref,
                ov_ref, o_ref, psum_ref, sel_ref, oacc_ref, sig_ref, *scratch):
    st_refs, s_a, s_b, p_a, p_b = _stream_refs(scratch, NSA_HEADS)
    qi = pl.program_id(1)
    t0 = qi * TQ
    sub_grp = lax.shift_right_arithmetic(lax.broadcasted_iota(jnp.int32, (LANE, 1), 0), 6)
    sig_ref[...] = jax.nn.sigmoid(gt_ref[...])

    def gate_row(c):
        return sig_ref[pl.ds(c, 1), :]

    def masked_q(r, g):
        return jnp.where(sub_grp == g, qa_ref[r].astype(F32), 0.0).astype(BF16)

    n_cmp = kcmp_ref.shape[1] - 1
    crow = lax.broadcasted_iota(jnp.int32, (LANE, 1), 0)
    cmp_end = jnp.where(crow < n_cmp, crow * CMP_STRIDE + CMP_LEN - 1, 1 << 30)
    mask_c = (t0 + lax.broadcasted_iota(jnp.int32, (1, TQ), 1)) >= cmp_end

    heads = [(r, g) for r in range(NSA_R) for g in range(NSA_G)]
    scores_c = [_dot(kcmp_ref[0], masked_q(r, g)) for r, g in heads]
    probs_c = []
    for (r, g), s in zip(heads, scores_c):
        s = jnp.where(mask_c, s + bc_ref[g * NSA_R + r], NEG)
        p = jnp.where(mask_c, jnp.exp2(s - jnp.max(s, axis=0, keepdims=True)), 0.0)
        l = jnp.sum(p, axis=0, keepdims=True)
        probs_c.append(p * jnp.where(l > 0.0, 1.0 / l, 0.0))
    for g in range(NSA_G):
        psum_ref[g] = sum(p for (r, gg), p in zip(heads, probs_c) if gg == g)
    outs_c = [_dot(vcmpt_ref[0], p.astype(BF16)) for p in probs_c]
    for r in range(NSA_R):
        o0, o1 = outs_c[r * NSA_G], outs_c[r * NSA_G + 1]
        oacc_ref[r] = jnp.where(sub_grp == 0, gate_row(r * 3) * o0, gate_row((NSA_R + r) * 3) * o1)

    has_prev = jnp.where(qi >= 1, 0.0, NEG)
    has_wfar = jnp.where(qi >= WINDOW // TQ, 0.0, NEG)
    kt_prev = jnp.maximum(qi - 1, 0)
    kt_wfar = jnp.maximum(qi - WINDOW // TQ, 0)
    wfar_mask = lax.broadcasted_iota(jnp.int32, (TQ, TQ), 0) > lax.broadcasted_iota(jnp.int32, (TQ, TQ), 1)

    def attend_all_heads():
        chains = [(r, g) for r in range(NSA_R) for g in range(NSA_G)]
        qms = [masked_q(r, g) for r, g in chains]
        hids = [g * NSA_R + r for r, g in chains]

        per_tile = TQ // SEL_BLOCK
        key_blk = lax.shift_right_arithmetic(lax.broadcasted_iota(jnp.int32, (TQ, LANE), 0), 6)
        lane_id = lax.broadcasted_iota(jnp.int32, (TQ, LANE), 1)

        def keys_with_block_onehot(kt, g):
            other = (1 - g) * NSA_HD
            onehot = jnp.where(lane_id - other == key_blk, 1.0, 0.0).astype(BF16)
            in_other = (lane_id >= other) & (lane_id < other + NSA_HD)
            return jnp.where(in_other, onehot, _ktile(ks_ref, kt))

        def query_with_mask_rows(kt, c):
            r, g = chains[c]
            own = qa_ref[r, g * NSA_HD:(g + 1) * NSA_HD, :]
            pair = sel_ref[g, pl.ds(pl.multiple_of((kt // 2) * 2 * per_tile, 2 * per_tile), 2 * per_tile), :]
            rows = jnp.where(kt % 2 == 0, pair[0:per_tile], pair[per_tile:2 * per_tile]).astype(BF16)
            other = jnp.concatenate([rows, jnp.zeros((NSA_HD - per_tile, TQ), BF16)], axis=0)
            return jnp.concatenate([own, other] if g == 0 else [other, own], axis=0)

        def near_bias(s, near, c):
            if near == 0:
                return s + dn_ref[hids[c], 0]
            if near == 1:
                return s + dn_ref[hids[c], 1] + has_prev
            return s

        def group_values(ref, kt, c):
            g = chains[c][1]
            return _with_ones(ref[g * NSA_HD:(g + 1) * NSA_HD, pl.ds(pl.multiple_of(kt * TQ, TQ), TQ)])

        def finish(sts, branch):
            for c, (r, g) in enumerate(chains):
                oacc_ref[r, g * NSA_HD:(g + 1) * NSA_HD, :] += gate_row(hids[c] * 3 + branch) * _flash_out(sts[c])

        def slc_scores(kt, near, c):
            return near_bias(_dot(keys_with_block_onehot(kt, chains[c][1]), query_with_mask_rows(kt, c)), near, c)

        def slc_vtile(kt, c):
            return group_values(vst_ref, kt, c)

        win_tiles = ((qi, 0), (kt_prev, 1), (kt_wfar, 2))

        def win_scores(c):
            out = []
            for kt, near in win_tiles:
                s = near_bias(_dot(_ktile(kw_ref, kt), qms[c]), near, c)
                out.append(jnp.where(wfar_mask, s + has_wfar, NEG) if near == 2 else s)
            return out

        sts = []
        ahead = win_scores(0)
        for c in range(len(chains)):
            tiles, ahead = ahead, (win_scores(c + 1) if c + 1 < len(chains) else None)
            st = _flash_first([tiles[0]], [group_values(vwt_ref, qi, c)])
            st = _flash_update([tiles[1]], [group_values(vwt_ref, kt_prev, c)], st)
            sts.append(_flash_update([tiles[2]], [group_values(vwt_ref, kt_wfar, c)], st)[0])
        finish(sts, 2)

        n_sel = ov_ref.shape[0]
        jj = lax.broadcasted_iota(jnp.int32, (n_sel, TQ), 0)
        cur = lax.shift_right_arithmetic(t0 + lax.broadcasted_iota(jnp.int32, (n_sel, TQ), 1), 6)
        valid = jj <= cur
        forced = (jj == 0) | (cur - jj < SEL_FORCED_LOCAL)
        for g in range(NSA_G):
            ps = psum_ref[g]
            hi = ps.astype(BF16)
            rem = ps - hi.astype(F32)
            mid = rem.astype(BF16)
            lo = (rem - mid.astype(F32)).astype(BF16)
            ov = ov_ref[...]
            imp = _dot(ov, hi) + _dot(ov, mid) + _dot(ov, lo)
            score = jnp.where(valid, jnp.where(forced, 1e9, imp), -1e9)
            cnt = jnp.zeros((n_sel, TQ), F32)
            for j2 in range(n_sel):
                row = score[j2:j2 + 1, :]
                tie = jnp.where(jj > j2, 1.0, 0.0)
                cnt = cnt + jnp.where(row > score, 1.0, jnp.where(row == score, tie, 0.0))
            sel_ref[g] = jnp.where(cnt < float(min(SEL_TOPN, n_sel)), jnp.where(score > -1e8, 0.0, NEG), NEG)

        finish(_causal_stream(qi, slc_scores, slc_vtile, st_refs, s_a, s_b, p_a, p_b), 1)
    attend_all_heads()

    for r in range(NSA_R):
        o_ref[r] = oacc_ref[r].T.astype(BF16)


def _nsa(qa, gates_t, kcmp, vcmpt, bias_c, ks, vst, kw, vwt, near_a, ov, B, S):
    nq = S // TQ
    T = B * S
    k_spec = pl.BlockSpec((S, LANE), lambda b, q: (b, 0))
    v_spec = pl.BlockSpec((LANE, S), lambda b, q: (0, b))
    full = lambda a: pl.BlockSpec(a.shape, lambda b, q: (0,) * a.ndim)
    cmp_spec = pl.BlockSpec((1, LANE, LANE), lambda b, q: (b, 0, 0))
    qo = pl.BlockSpec((4, TQ, LANE), lambda b, q: (0, b * nq + q, 0))
    return pl.pallas_call(
        _nsa_kernel,
        grid=(B, nq),
        in_specs=[pl.BlockSpec((4, LANE, TQ), lambda b, q: (0, 0, b * nq + q)),
                  pl.BlockSpec((GATE_ROWS, TQ), lambda b, q: (0, b * nq + q)),
                  cmp_spec, cmp_spec,
                  pl.BlockSpec((NSA_HEADS, LANE, TQ), lambda b, q: (0, 0, q)),
                  k_spec, v_spec, k_spec, v_spec, full(near_a), full(ov)],
        out_specs=qo,
        out_shape=jax.ShapeDtypeStruct((4, T, LANE), BF16),
        scratch_shapes=[pltpu.VMEM((NSA_G, LANE, TQ), F32), pltpu.VMEM((NSA_G, S // SEL_BLOCK, TQ), F32),
                        pltpu.VMEM((NSA_R, LANE, TQ), F32), pltpu.VMEM((GATE_ROWS, TQ), F32)]
                       + _stream_scratch(NSA_HEADS),
        compiler_params=_cparams(("parallel", "arbitrary")),
        name="nsa",
    )(qa, gates_t, kcmp, vcmpt, bias_c, ks, vst, kw, vwt, near_a, ov)


def _diff_kernel(lq1_ref, lk1_ref, lq2_ref, lk2_ref, sub_ref, qb_ref, kb_ref, vbt_ref, dn_ref, o_ref, *scratch):
    n = DIFF_PAIRS_PER_STEP * 4
    st_refs, s_a, s_b, p_a, p_b = _stream_refs(scratch, n)
    qi = pl.program_id(1)
    chain_of_row = lax.shift_right_arithmetic(lax.broadcasted_iota(jnp.int32, (LANE, 1), 0), 5)
    lam = (jnp.exp(jnp.sum(lq1_ref[...] * lk1_ref[...], axis=-1, keepdims=True))
           - jnp.exp(jnp.sum(lq2_ref[...] * lk2_ref[...], axis=-1, keepdims=True)) + LAMBDA_INIT)
    has_prev = jnp.where(qi >= 1, 0.0, NEG)
    n_steps = DIFF_HEADS // 2 // DIFF_PAIRS_PER_STEP

    def score_fn(step):
        pairs = [step * DIFF_PAIRS_PER_STEP + c // 4 for c in range(n)]
        qs = [qb_ref[step * DIFF_PAIRS_PER_STEP + i].astype(F32) for i in range(DIFF_PAIRS_PER_STEP)]
        qms = [jnp.where(chain_of_row == c % 4, qs[c // 4], 0.0).astype(BF16) for c in range(n)]

        def scores(kt, near, c):
            s = _dot(kb_ref[pairs[c], pl.ds(pl.multiple_of(kt * TQ, TQ), TQ), :], qms[c])
            if near == 2:
                return s
            bias = dn_ref[2 * pairs[c] + (c % 4) // 2, near]
            return s + bias if near == 0 else s + bias + has_prev
        return scores

    def start_diag(step):
        scores = score_fn(step)
        for c in range(n):
            s_a[c][...] = scores(qi, 0, c)

    start_diag(0)

    def step_body(step, carry):
        def vtile(kt, c):
            hh = (c % 4) // 2
            return _with_ones(vbt_ref[step * DIFF_PAIRS_PER_STEP + c // 4, hh * 2 * DIFF_HD:(hh + 1) * 2 * DIFF_HD,
                                      pl.ds(pl.multiple_of(kt * TQ, TQ), TQ)])

        sts = _causal_stream(qi, score_fn(step), vtile, st_refs, s_a, s_b, p_a, p_b, diag_ready=True)
        start_diag(jnp.minimum(step + 1, n_steps - 1))
        for i in range(DIFF_PAIRS_PER_STEP):
            outs = []
            for hh in range(2):
                c = 4 * i + 2 * hh
                o = _flash_out(sts[c]) - lam * _flash_out(sts[c + 1])
                outs.append(o * lax.rsqrt(jnp.mean(o * o, axis=0, keepdims=True) + EPS))
            out = jnp.concatenate(outs, axis=0) * sub_ref[...] * (1.0 - LAMBDA_INIT)
            o_ref[step * DIFF_PAIRS_PER_STEP + i] = out.T.astype(BF16)
        return carry
    lax.fori_loop(0, n_steps, step_body, 0)


def _diff(lq1, lk1, lq2, lk2, sub, qb, kb, vbt, near_b, B, S):
    nq = S // TQ
    T = B * S
    full = lambda a: pl.BlockSpec(a.shape, lambda b, q: (0,) * a.ndim)
    k_spec = pl.BlockSpec((4, S, LANE), lambda b, q: (0, b, 0))
    v_spec = pl.BlockSpec((4, LANE, S), lambda b, q: (0, 0, b))
    qo = pl.BlockSpec((4, TQ, LANE), lambda b, q: (0, b * nq + q, 0))
    return pl.pallas_call(
        _diff_kernel,
        grid=(B, nq),
        in_specs=[full(lq1), full(lk1), full(lq2), full(lk2), full(sub),
                  pl.BlockSpec((4, LANE, TQ), lambda b, q: (0, 0, b * nq + q)), k_spec, v_spec, full(near_b)],
        out_specs=qo,
        out_shape=jax.ShapeDtypeStruct((4, T, LANE), BF16),
        scratch_shapes=_stream_scratch(DIFF_PAIRS_PER_STEP * 4),
        compiler_params=_cparams(("parallel", "arbitrary")),
        name="diff",
    )(lq1, lk1, lq2, lk2, sub, qb, kb, vbt, near_b)


def _outproj_kernel(x_ref, oa_ref, ob_ref, w_ref, g_ref, wr_ref, br_ref, xt_ref, route_ref, cnt_ref, carry_ref):
    @pl.when(pl.program_id(0) == 0)
    def _():
        carry_ref[...] = jnp.zeros(carry_ref.shape, F32)

    o = jnp.concatenate([oa_ref[r] for r in range(4)] + [ob_ref[r] for r in range(4)], axis=-1)
    h = x_ref[...] + _dot(o, w_ref[...])
    xt_ref[:, 0:D_MODEL] = h
    tn = _rmsnorm(h, g_ref[...]).astype(BF16)
    logits = _dot(tn, wr_ref[...]) + br_ref[...]
    lane = lax.broadcasted_iota(jnp.int32, (1, LANE), 1)
    lane_f = lane.astype(F32)
    is_grp = lane < MOE_GROUPS
    lg = jnp.where(is_grp, logits, NEG)
    mg = jnp.max(lg, axis=-1, keepdims=True)
    zg = jnp.sum(jnp.where(is_grp, jnp.exp(lg - mg), 0.0), axis=-1, keepdims=True)
    g_prob = 1.0 / zg
    g_idx = jnp.min(jnp.where(lg == mg, lane_f, 1e9), axis=-1, keepdims=True)
    lane_grp = jnp.where((lane >= MOE_GROUPS) & (lane < MOE_GROUPS + N_EXPERTS),
                         lax.shift_right_arithmetic(lane - MOE_GROUPS, 3), -1).astype(F32)
    le = jnp.where(lane_grp == g_idx, logits, NEG)
    m1 = jnp.max(le, axis=-1, keepdims=True)
    e1 = jnp.min(jnp.where(le == m1, lane_f, 1e9), axis=-1, keepdims=True)
    le2 = jnp.where(lane_f == e1, NEG, le)
    m2 = jnp.max(le2, axis=-1, keepdims=True)
    e2 = jnp.min(jnp.where(le2 == m2, lane_f, 1e9), axis=-1, keepdims=True)
    ratio = jnp.exp(m2 - m1)
    w1 = g_prob / (1.0 + ratio)
    w2 = w1 * ratio
    xt_ref[:, D_MODEL:D_MODEL + LANE] = jnp.where(lane_f == e1, w1, 0.0) + jnp.where(lane_f == e2, w2, 0.0)

    tm = h.shape[0]
    onehot = jnp.where(lane_f == g_idx, 1.0, 0.0)
    earlier = jnp.where(lax.broadcasted_iota(jnp.int32, (tm, tm), 0) > lax.broadcasted_iota(jnp.int32, (tm, tm), 1),
                        1.0, 0.0).astype(BF16)
    prefix = _dot(earlier, onehot.astype(BF16)) + carry_ref[...]
    rank = jnp.sum(onehot * prefix, axis=-1, keepdims=True)
    carry_ref[...] += jnp.sum(onehot, axis=0, keepdims=True)
    route_ref[...] = jnp.where(lane == 0, g_idx, jnp.where(lane == 1, rank, 0.0))
    cnt_ref[...] = jnp.broadcast_to(carry_ref[...], cnt_ref.shape)


def _out_proj(x2, oa, ob, w, g, wr, br):
    T = x2.shape[0]
    tm = TM_PROJ
    row = lambda i: (i, 0)
    full = lambda a: pl.BlockSpec(a.shape, lambda i: (0,) * a.ndim)
    o4 = pl.BlockSpec((4, tm, LANE), lambda i: (0, i, 0))
    return pl.pallas_call(
        _outproj_kernel,
        grid=(T // tm,),
        in_specs=[pl.BlockSpec((tm, D_MODEL), row), o4, o4, full(w), full(g), full(wr), full(br)],
        out_specs=[pl.BlockSpec((tm, XT_WIDTH), row), pl.BlockSpec((tm, LANE), row),
                   pl.BlockSpec((8, LANE), lambda i: (0, 0))],
        out_shape=[jax.ShapeDtypeStruct((T, XT_WIDTH), F32), jax.ShapeDtypeStruct((T, LANE), F32),
                   jax.ShapeDtypeStruct((8, LANE), F32)],
        scratch_shapes=[pltpu.VMEM((1, LANE), F32)],
        compiler_params=_cparams(("arbitrary",)),
        name="out_proj",
    )(x2, oa, ob, w, g, wr, br)


def _moe_kernel(tg_ref, ok_ref, idx_ref, idxn_ref, xt_hbm, g_ref, wg_ref, wu_ref, wd_ref, y_ref, xbuf, sem,
                xb_ref, cmb_ref):
    i = pl.program_id(0)
    slot = i % 2
    tm = xbuf.shape[1]

    def row_copy(index_ref, r, dst_slot):
        return pltpu.make_async_copy(xt_hbm.at[pl.ds(index_ref[0, 0, r], 1), :],
                                     xbuf.at[dst_slot, pl.ds(r, 1), :], sem.at[dst_slot])

    @pl.when(i == 0)
    def _():
        def body(r, c):
            row_copy(idx_ref, r, 0).start()
            return c
        lax.fori_loop(0, tm, body, 0)

    @pl.when((i == 0) | (ok_ref[jnp.maximum(i - 1, 0)] == 1))
    def _():
        pltpu.make_async_copy(xt_hbm.at[pl.ds(0, tm), :], xbuf.at[slot], sem.at[slot]).wait()

    @pl.when(ok_ref[i] == 0)
    def _():
        y_ref[...] = jnp.zeros(y_ref.shape, F32)

    @pl.when(ok_ref[i] == 1)
    def _():
        lane = lax.broadcasted_iota(jnp.int32, (1, LANE), 1)
        xb_ref[...] = _rmsnorm(xbuf[slot, :, 0:D_MODEL], g_ref[...]).astype(BF16)
        cmb_ref[...] = xbuf[slot, :, D_MODEL:D_MODEL + LANE]
        first_lane = MOE_GROUPS + tg_ref[i] * EPG
        per_expert = -(-tm // (EPG - 1))
        y = jnp.zeros((tm, D_MODEL), F32)
        issued = jnp.int32(0)
        for e in range(EPG):
            x = xb_ref[...]
            a = _dot(x, wg_ref[0, e])
            b = _dot(x, wu_ref[0, e])
            ce = jnp.sum(jnp.where(lane == first_lane + e + issued, cmb_ref[...], 0.0), axis=-1, keepdims=True)
            y = y + _dot(((a * jax.nn.sigmoid(a)) * b * ce).astype(BF16), wd_ref[0, e])
            for r in range(e * per_expert, min((e + 1) * per_expert, tm)):
                row_copy(idxn_ref, r, 1 - slot).start(priority=r % 2)
            if e in ANCHOR_AFTER:
                issued = jnp.where((pl.semaphore_read(sem.at[1 - slot]) | 1) == 0, 1, 0)
        for blk in range(ROW_SLABS):
            y_ref[pl.ds(blk, tm, stride=ROW_SLABS), :] = y[:, blk * LANE:(blk + 1) * LANE]


def _moe(tile_group, tile_ok, src3, xt, g, wg, wu, wd):
    n_tiles, _, tm = src3.shape
    last = n_tiles - 1
    grid_spec = pltpu.PrefetchScalarGridSpec(
        num_scalar_prefetch=2,
        grid=(n_tiles,),
        in_specs=[pl.BlockSpec((1, 1, tm), lambda i, tg, ok: (i, 0, 0), memory_space=pltpu.SMEM),
                  pl.BlockSpec((1, 1, tm), lambda i, tg, ok: (jnp.minimum(i + 1, last), 0, 0), memory_space=pltpu.SMEM),
                  pl.BlockSpec(memory_space=pl.ANY),
                  pl.BlockSpec((1, D_MODEL), lambda i, tg, ok: (0, 0)),
                  pl.BlockSpec((1, EPG, D_MODEL, EXPERT_FF), lambda i, tg, ok: (tg[i], 0, 0, 0)),
                  pl.BlockSpec((1, EPG, D_MODEL, EXPERT_FF), lambda i, tg, ok: (tg[i], 0, 0, 0)),
                  pl.BlockSpec((1, EPG, EXPERT_FF, D_MODEL), lambda i, tg, ok: (tg[i], 0, 0, 0))],
        out_specs=pl.BlockSpec((tm * ROW_SLABS, LANE), lambda i, tg, ok: (i, 0)),
        scratch_shapes=[pltpu.VMEM((2, tm, XT_WIDTH), F32), pltpu.SemaphoreType.DMA((2,)),
                        pltpu.VMEM((tm, D_MODEL), BF16), pltpu.VMEM((tm, LANE), F32)],
    )
    return pl.pallas_call(
        _moe_kernel,
        grid_spec=grid_spec,
        out_shape=jax.ShapeDtypeStruct((n_tiles * tm * ROW_SLABS, LANE), F32),
        compiler_params=pltpu.CompilerParams(dimension_semantics=("arbitrary",), vmem_limit_bytes=VMEM_LIMIT_MOE),
        name="moe",
    )(tile_group, tile_ok, src3, src3, xt, g, wg, wu, wd)


def _final_kernel(pos_ref, posn_ref, y_hbm, h_ref, gf_ref, o_ref, ybuf, sem):
    i = pl.program_id(0)
    slot = i % 2
    tm = h_ref.shape[0]

    def issue(index_ref, dst_slot):
        group = 8

        def body(j, c):
            for k in range(group):
                r = j * group + k
                src_row = pl.multiple_of(index_ref[0, 0, r], ROW_SLABS)
                pltpu.make_async_copy(y_hbm.at[pl.ds(src_row, ROW_SLABS), :],
                                      ybuf.at[dst_slot, pl.ds(r * ROW_SLABS, ROW_SLABS), :],
                                      sem.at[dst_slot]).start(priority=k % 2)
            return c
        lax.fori_loop(0, tm // group, body, 0)

    @pl.when(i == 0)
    def _():
        issue(pos_ref, 0)

    @pl.when(i + 1 < pl.num_programs(0))
    def _():
        issue(posn_ref, 1 - slot)

    pltpu.make_async_copy(ybuf.at[slot], ybuf.at[slot], sem.at[slot]).wait()
    y = jnp.concatenate([ybuf[slot, pl.ds(blk, tm, stride=ROW_SLABS), :] for blk in range(ROW_SLABS)], axis=-1)
    h = h_ref[...] + y
    o_ref[...] = h * lax.rsqrt(jnp.mean(h * h, axis=-1, keepdims=True) + EPS) * gf_ref[...]


def _final(pos3, y_sorted, h, gf):
    n_tiles, _, tm = pos3.shape
    last = n_tiles - 1
    row = lambda i: (i, 0)
    return pl.pallas_call(
        _final_kernel,
        grid=(n_tiles,),
        in_specs=[pl.BlockSpec((1, 1, tm), lambda i: (i, 0, 0), memory_space=pltpu.SMEM),
                  pl.BlockSpec((1, 1, tm), lambda i: (jnp.minimum(i + 1, last), 0, 0), memory_space=pltpu.SMEM),
                  pl.BlockSpec(memory_space=pl.ANY),
                  pl.BlockSpec((tm, D_MODEL), row),
                  pl.BlockSpec((1, D_MODEL), lambda i: (0, 0))],
        out_specs=pl.BlockSpec((tm, D_MODEL), row),
        out_shape=jax.ShapeDtypeStruct((n_tiles * tm, D_MODEL), F32),
        scratch_shapes=[pltpu.VMEM((2, tm * ROW_SLABS, LANE), F32), pltpu.SemaphoreType.DMA((2,))],
        compiler_params=_cparams(("arbitrary",)),
        name="final",
    )(pos3, pos3, y_sorted, h, gf)


def _qa_perm():
    new = np.arange(NSA_HEADS * NSA_HD)
    r, g, d = new // LANE, (new % LANE) // NSA_HD, new % NSA_HD
    return (g * NSA_R + r) * NSA_HD + d


def _block_diag2(w):
    z = jnp.zeros_like(w)
    return jnp.concatenate([jnp.concatenate([w, z], axis=-1), jnp.concatenate([z, w], axis=-1)], axis=-2)


def kernel(x, rel_bias, ln_mix, w_in, cmp_pos_k, cmp_pos_v, cmp_k_w1, cmp_k_w2, cmp_v_w1, cmp_v_w2,
           diff_lq1, diff_lk1, diff_lq2, diff_lk2, diff_subln, w_out, ln_ffn,
           router_group_w, router_group_b, router_expert_w, router_expert_b,
           exp_w_gate, exp_w_up, exp_w_down, ln_final):
    B, S, D = x.shape
    T = B * S
    assert D == D_MODEL and S % TQ == 0 and S >= WINDOW and T % TM_MOE == 0 and T % TM_FINAL == 0
    x2 = x.reshape(T, D)
    perm = _qa_perm()

    w = w_in[0]
    c_kc, c_vc, c_ks, c_vs, c_kw, c_vw, c_gt = 512, 640, 768, 896, 1024, 1152, 1280
    c_qb = c_gt + N_GATE
    c_kb, c_vb = c_qb + 512, c_qb + 1024
    col = lambda c, n=LANE: w[:, c:c + n]
    w_tok = jnp.concatenate([col(c_kc), col(c_vc), col(c_ks), col(c_kw), col(c_kb, 512)], axis=1).astype(BF16)
    w_feat = jnp.concatenate([w[:, perm], col(c_qb, 512), col(c_vs), col(c_vw), col(c_vb, 512), col(c_gt, N_GATE),
                              jnp.zeros((D, GATE_ROWS - N_GATE), F32)], axis=1).T.astype(BF16)
    qa, kc, vc, ks, kw, qb, kb, vst, vwt, vbt, gates_t = _in_proj(x2, ln_mix[0][None, :], w_tok, w_feat)

    w1k = _block_diag2(cmp_k_w1[0].reshape(CMP_LEN, NSA_HD, CMP_HIDDEN)).astype(BF16)
    w1v = _block_diag2(cmp_v_w1[0].reshape(CMP_LEN, NSA_HD, CMP_HIDDEN)).astype(BF16)
    w2k = _block_diag2(cmp_k_w2[0]).astype(BF16)
    w2v = _block_diag2(cmp_v_w2[0]).astype(BF16)
    posk = jnp.tile(cmp_pos_k[0], (1, NSA_G))
    posv = jnp.tile(cmp_pos_v[0], (1, NSA_G))
    kcmp, vcmpt = _compress(kc, vc, posk, posv, w1k, w1v, w2k, w2v, B, S)

    near, bias_c = _bias_tiles(rel_bias.T, S)

    n_sel = S // SEL_BLOCK
    nrow = S // CMP_STRIDE
    c_start = np.arange(nrow) * CMP_STRIDE
    s_start = np.arange(n_sel) * SEL_BLOCK
    ov = ((c_start[None, :] <= s_start[:, None] + SEL_BLOCK - 1)
          & (c_start[None, :] + CMP_LEN - 1 >= s_start[:, None])
          & (np.arange(nrow)[None, :] < nrow - 1)).astype(np.float32)
    o_a = _nsa(qa, gates_t, kcmp, vcmpt, bias_c, ks, vst, kw, vwt, near[:NSA_HEADS], jnp.asarray(ov, BF16), B, S)

    sub = jnp.tile(diff_subln[0], 2)[:, None]
    o_b = _diff(diff_lq1[0][None, :], diff_lk1[0][None, :], diff_lq2[0][None, :], diff_lk2[0][None, :],
                sub, qb, kb, vbt, near[NSA_HEADS:], B, S)

    w_o = jnp.concatenate([w_out[0][:512][perm], w_out[0][512:]], axis=0).astype(BF16)
    n_r = MOE_GROUPS + N_EXPERTS
    wr = jnp.concatenate([router_group_w[0], router_expert_w[0], jnp.zeros((D, LANE - n_r), F32)], axis=1).astype(BF16)
    br = jnp.concatenate([router_group_b[0], router_expert_b[0], jnp.zeros((LANE - n_r,), F32)])[None, :]
    xt, route, counts = _out_proj(x2, o_a, o_b, w_o, ln_ffn[0][None, :], wr, br)

    tm = TM_MOE
    n_tiles = T // tm + MOE_GROUPS
    cnt = counts[0, :MOE_GROUPS].astype(jnp.int32)
    ends = jnp.cumsum((cnt + tm - 1) // tm * tm)
    starts = ends - (cnt + tm - 1) // tm * tm
    pos = starts[route[:, 0].astype(jnp.int32)] + route[:, 1].astype(jnp.int32)
    src = jnp.zeros((n_tiles * tm,), jnp.int32).at[pos].set(
        jnp.arange(T, dtype=jnp.int32), unique_indices=True, mode="promise_in_bounds")
    tile_start = jnp.arange(n_tiles, dtype=jnp.int32) * tm
    tile_group = jnp.minimum(jnp.searchsorted(ends, tile_start, side="right"), MOE_GROUPS - 1).astype(jnp.int32)
    tile_ok = (tile_start < ends[-1]).astype(jnp.int32)

    by_group = lambda a: a[0].astype(BF16).reshape((MOE_GROUPS, EPG) + a.shape[2:])
    y_sorted = _moe(tile_group, tile_ok, src.reshape(n_tiles, 1, tm), xt, ln_ffn[0][None, :],
                    by_group(exp_w_gate), by_group(exp_w_up), by_group(exp_w_down))
    out = _final((pos * ROW_SLABS).reshape(T // TM_FINAL, 1, TM_FINAL), y_sorted, xt, ln_final[None, :])
    return out.reshape(B, S, D)
```

```python
import math

import numpy as np
import jax
import jax.numpy as jnp
from jax import lax
from jax.experimental import pallas as pl
from jax.experimental.pallas import tpu as pltpu

F32 = jnp.float32
BF16 = jnp.bfloat16
NEG = -1e30
EPS = 1e-6
LOG2E = math.log2(math.e)

D_MODEL = 1024
LANE = 128
NSA_HEADS, NSA_G, NSA_R, NSA_HD = 8, 2, 4, 64
CMP_LEN, CMP_STRIDE, CMP_HIDDEN = 32, 16, 128
SEL_BLOCK, SEL_TOPN, SEL_FORCED_LOCAL, WINDOW = 64, 8, 2, 512
DIFF_HEADS, DIFF_HD = 8, 32
REL_BUCKETS, REL_MAX_EXACT, REL_MAX_DIST = 32, 16, 128
N_REL_HEADS = NSA_HEADS + DIFF_HEADS
MOE_GROUPS, EPG, N_EXPERTS, EXPERT_FF = 4, 8, 32, 256
LAMBDA_INIT = 0.8 - 0.6 * math.exp(-0.3 * 0)
N_GATE = NSA_HEADS * 3
GATE_ROWS = 32

TQ = 256
DIFF_PAIRS_PER_STEP = 4
TM_PROJ = 512
TM_MOE = 512
TM_FINAL = 512
ANCHOR_AFTER = (1, 3, 5, 6)
XT_WIDTH = D_MODEL + LANE
ROW_SLABS = D_MODEL // LANE
VMEM_LIMIT = 48 * 1024 * 1024
VMEM_LIMIT_MOE = 56 * 1024 * 1024


def _cparams(sem):
    return pltpu.CompilerParams(dimension_semantics=sem, vmem_limit_bytes=VMEM_LIMIT)


def _dot(a, b):
    return jnp.dot(a, b, preferred_element_type=F32)


def _rmsnorm(x, g):
    return x * lax.rsqrt(jnp.mean(x * x, axis=-1, keepdims=True) + EPS) * g


def _dot_nt(a, b):
    return lax.dot_general(a, b, (((1,), (1,)), ((), ())), preferred_element_type=F32)


def _bucket_thresholds():
    n = np.arange(0, REL_MAX_DIST + 1)
    nf = np.maximum(n, 1).astype(np.float32)
    large = REL_MAX_EXACT + (np.log(nf / np.float32(REL_MAX_EXACT)) / np.float32(math.log(REL_MAX_DIST / REL_MAX_EXACT))
                             * np.float32(REL_BUCKETS - REL_MAX_EXACT)).astype(np.int32)
    large = np.minimum(large, REL_BUCKETS - 1)
    bucket = np.where(n < REL_MAX_EXACT, n, large)
    return [int(np.argmax(bucket >= b)) for b in range(REL_BUCKETS)]


_THR = _bucket_thresholds()


def _inproj_kernel(x_ref, g_ref, w_ref, wt_ref, qa_ref, kc_ref, vc_ref, ks_ref, kw_ref, qb_ref, kb_ref,
                   vst_ref, vwt_ref, vbt_ref, gt_ref):
    x = x_ref[...]
    xn = (x * lax.rsqrt(jnp.mean(x * x, axis=-1, keepdims=True) + EPS) * g_ref[...]).astype(BF16)
    a = _dot(xn, w_ref[:, 0:512])
    kc_ref[...] = a[:, 0:128]
    vc_ref[...] = a[:, 128:256]
    ks_ref[...] = a[:, 256:384].astype(BF16)
    kw_ref[...] = a[:, 384:512].astype(BF16)
    a = _dot(xn, w_ref[:, 512:1024])
    for r in range(4):
        kb_ref[r] = a[:, r * LANE:(r + 1) * LANE].astype(BF16)

    ft = _dot_nt(wt_ref[...], xn)

    def feat(row0, rows=LANE):
        return ft[row0:row0 + rows, :]
    for r in range(4):
        qa_ref[r] = (feat(r * LANE) * (NSA_HD ** -0.5 * LOG2E)).astype(BF16)
        qb_ref[r] = (feat(512 + r * LANE) * (DIFF_HD ** -0.5 * LOG2E)).astype(BF16)
        vbt_ref[r] = feat(1280 + r * LANE).astype(BF16)
    vst_ref[...] = feat(1024).astype(BF16)
    vwt_ref[...] = feat(1152).astype(BF16)
    gt_ref[...] = feat(1792, GATE_ROWS)


def _in_proj(x2, g, w, wt):
    T = x2.shape[0]
    tm = TM_PROJ
    row = lambda i: (i, 0)
    o128b = jax.ShapeDtypeStruct((T, LANE), BF16)
    o128f = jax.ShapeDtypeStruct((T, LANE), F32)
    o4 = jax.ShapeDtypeStruct((4, T, LANE), BF16)
    ot = jax.ShapeDtypeStruct((LANE, T), BF16)
    o4t = jax.ShapeDtypeStruct((4, LANE, T), BF16)
    s128 = pl.BlockSpec((tm, LANE), row)
    s4 = pl.BlockSpec((4, tm, LANE), lambda i: (0, i, 0))
    st = pl.BlockSpec((LANE, tm), lambda i: (0, i))
    s4t = pl.BlockSpec((4, LANE, tm), lambda i: (0, 0, i))
    return pl.pallas_call(
        _inproj_kernel,
        grid=(T // tm,),
        in_specs=[pl.BlockSpec((tm, D_MODEL), row),
                  pl.BlockSpec((1, D_MODEL), lambda i: (0, 0)),
                  pl.BlockSpec(w.shape, lambda i: (0, 0)),
                  pl.BlockSpec(wt.shape, lambda i: (0, 0))],
        out_specs=[s4t, s128, s128, s128, s128, s4t, s4, st, st, s4t, pl.BlockSpec((GATE_ROWS, tm), lambda i: (0, i))],
        out_shape=[o4t, o128f, o128f, o128b, o128b, o4t, o4, ot, ot, o4t, jax.ShapeDtypeStruct((GATE_ROWS, T), F32)],
        compiler_params=_cparams(("parallel",)),
        name="in_proj",
    )(x2, g, w, wt)


def _gelu_tanh(x):
    return 0.5 * x * (1.0 + jnp.tanh(math.sqrt(2.0 / math.pi) * (x + 0.044715 * (x * x * x))))


def _compress_kernel(kc_ref, vc_ref, posk_ref, posv_ref, w1k_ref, w1v_ref, w2k_ref, w2v_ref, ko_ref, vo_ref):
    nrow = kc_ref.shape[0] // CMP_STRIDE
    rid = lax.broadcasted_iota(jnp.int32, (nrow, 1), 0)
    cid = lax.broadcasted_iota(jnp.int32, (1, nrow), 1)
    for src, pos, w1, w2, out, transposed in ((kc_ref, posk_ref, w1k_ref, w2k_ref, ko_ref, False),
                                              (vc_ref, posv_ref, w1v_ref, w2v_ref, vo_ref, True)):
        hid_a = jnp.zeros((nrow, 2 * CMP_HIDDEN), F32)
        hid_b = jnp.zeros((nrow, 2 * CMP_HIDDEN), F32)
        for m in range(CMP_STRIDE):
            y = src[pl.ds(m, nrow, stride=CMP_STRIDE), :]
            hid_a = hid_a + _dot((y + pos[m:m + 1, :]).astype(BF16), w1[m])
            hid_b = hid_b + _dot((y + pos[CMP_STRIDE + m:CMP_STRIDE + m + 1, :]).astype(BF16), w1[CMP_STRIDE + m])
        hid = hid_a + pltpu.roll(hid_b, nrow - 1, 0)
        o = _dot(_gelu_tanh(hid).astype(BF16), w2[...])
        if transposed:
            out[0] = jnp.where(cid < nrow - 1, o.T, 0.0).astype(BF16)
        else:
            out[0] = jnp.where(rid < nrow - 1, o, 0.0).astype(BF16)


def _compress(kc, vc, posk, posv, w1k, w1v, w2k, w2v, B, S):
    nrow = S // CMP_STRIDE
    assert nrow == LANE
    full = lambda a: pl.BlockSpec(a.shape, lambda b: (0,) * a.ndim)
    src = pl.BlockSpec((S, LANE), lambda b: (b, 0))
    osp = pl.BlockSpec((1, nrow, LANE), lambda b: (b, 0, 0))
    osh = jax.ShapeDtypeStruct((B, nrow, LANE), BF16)
    return pl.pallas_call(
        _compress_kernel,
        grid=(B,),
        in_specs=[src, src, full(posk), full(posv), full(w1k), full(w1v), full(w2k), full(w2v)],
        out_specs=[osp, osp],
        out_shape=[osh, osh],
        compiler_params=_cparams(("parallel",)),
        name="compress",
    )(kc, vc, posk, posv, w1k, w1v, w2k, w2v)


def _bias_from_dist(dist, tab_ref, h):
    val = jnp.full(dist.shape, tab_ref[h, 0], F32)
    for b in range(1, REL_BUCKETS):
        val = jnp.where(dist >= _THR[b], tab_ref[h, b], val)
    return val


def _bias_near_kernel(tab_ref, out_ref):
    h = pl.program_id(0)
    j = lax.broadcasted_iota(jnp.int32, (TQ, TQ), 0)
    i = lax.broadcasted_iota(jnp.int32, (TQ, TQ), 1)
    for d in range(2):
        bias = (_bias_from_dist(i - j + d * TQ, tab_ref, h) - tab_ref[h, REL_BUCKETS - 1]) * LOG2E
        out_ref[0, d] = jnp.where(i - j + d * TQ >= 0, bias, NEG)


def _bias_cmp_kernel(tab_ref, out_ref):
    h = pl.program_id(0)
    c = lax.broadcasted_iota(jnp.int32, (LANE, TQ), 0)
    t = pl.program_id(1) * TQ + lax.broadcasted_iota(jnp.int32, (LANE, TQ), 1)
    out_ref[0] = _bias_from_dist(t - (c * CMP_STRIDE + CMP_LEN - 1), tab_ref, h) * LOG2E


def _bias_tiles(tab_t, S):
    smem = pl.BlockSpec(memory_space=pltpu.SMEM)
    near = pl.pallas_call(
        _bias_near_kernel,
        grid=(N_REL_HEADS,),
        in_specs=[smem],
        out_specs=pl.BlockSpec((1, 2, TQ, TQ), lambda h: (h, 0, 0, 0)),
        out_shape=jax.ShapeDtypeStruct((N_REL_HEADS, 2, TQ, TQ), F32),
        compiler_params=_cparams(("parallel",)),
        name="bias_near",
    )(tab_t)
    cmp_bias = pl.pallas_call(
        _bias_cmp_kernel,
        grid=(NSA_HEADS, S // TQ),
        in_specs=[smem],
        out_specs=pl.BlockSpec((1, LANE, TQ), lambda h, q: (h, 0, q)),
        out_shape=jax.ShapeDtypeStruct((NSA_HEADS, LANE, S), F32),
        compiler_params=_cparams(("parallel", "parallel")),
        name="bias_cmp",
    )(tab_t)
    return near, cmp_bias


ACC_ROWS = NSA_HD + 16


def _with_ones(vt):
    return jnp.concatenate([vt, jnp.ones((ACC_ROWS - vt.shape[0], vt.shape[1]), BF16)], axis=0)


def _probs(s, m):
    return jnp.exp2((s - m).astype(BF16))


def _flash_first(ss, vts):
    ms = [jnp.max(s, axis=0, keepdims=True) for s in ss]
    ps = [_probs(s, m) for s, m in zip(ss, ms)]
    return tuple((m, _dot(vt, p)) for m, p, vt in zip(ms, ps, vts))


def _flash_update(ss, vts, sts):
    ms = [jnp.maximum(st[0], jnp.max(s, axis=0, keepdims=True)) for s, st in zip(ss, sts)]
    alphas = [jnp.exp2(st[0] - m) for m, st in zip(ms, sts)]
    ps = [_probs(s, m) for s, m in zip(ss, ms)]
    return tuple((m, a * st[1] + _dot(vt, p)) for m, a, st, p, vt in zip(ms, alphas, sts, ps, vts))


def _flash_out(st):
    acc = st[1]
    return acc[0:NSA_HD, :] * (1.0 / acc[NSA_HD:NSA_HD + 1, :])


def _stream_scratch(n):
    return ([pltpu.VMEM((8, TQ), F32), pltpu.VMEM((ACC_ROWS, TQ), F32)] * n
            + [pltpu.VMEM((TQ, TQ), F32)] * (2 * n) + [pltpu.VMEM((TQ, TQ), BF16)] * (2 * n))


def _stream_refs(refs, n):
    st_refs = tuple((refs[2 * c], refs[2 * c + 1]) for c in range(n))
    s, p = refs[2 * n:4 * n], refs[4 * n:6 * n]
    return st_refs, tuple(s[:n]), tuple(s[n:]), tuple(p[:n]), tuple(p[n:])


def _causal_stream(qi, scores, vtile, st_refs, s_a, s_b, p_a, p_b, diag_ready=False):
    n_far = jnp.maximum(qi - 1, 0)
    top = qi - 2

    chains = range(len(st_refs))

    def write(s_refs, kt, near=2):
        for c in chains:
            s_refs[c][...] = scores(jnp.maximum(kt, 0), near, c)

    def pending(p_refs, kt):
        return [_dot(vtile(jnp.maximum(kt, 0), c), p_refs[c][...]) for c in chains]

    def step(s_cur, p_cur, s_next, kt_next, p_prev, kt_prev, anchor=None):
        rescale = []
        for c in chains:
            nxt = scores(jnp.maximum(kt_next, 0), 2, c)
            s_next[c][...] = nxt
            pv = _dot(vtile(jnp.maximum(kt_prev, 0), c), p_prev[c][...])
            st = st_refs[c][0]
            s = s_cur[c][...]
            m_old = st[0:1, :]
            m = jnp.maximum(m_old, jnp.max(s, axis=0, keepdims=True))
            if anchor is not None:
                m = m + anchor
            anchor = nxt[0:1, :] * 0.0
            p_cur[c][...] = _probs(s, m)
            st[0:1, :] = m
            rescale.append((jnp.exp2(m_old - m), pv))
        for c in chains:
            acc = st_refs[c][1]
            acc[...] = rescale[c][0] * (acc[...] + rescale[c][1])
        return anchor

    if not diag_ready:
        write(s_a, qi, 0)
    for c in chains:
        s_b[c][...] = scores(jnp.maximum(qi - 1, 0), 1, c)
        st, acc = st_refs[c]
        s = s_a[c][...]
        m = jnp.max(s, axis=0, keepdims=True)
        p_a[c][...] = _probs(s, m)
        st[0:1, :] = m
        acc[...] = jnp.zeros(acc.shape, F32)
    step(s_b, p_b, s_a, top, p_a, qi)

    def pair(j, c):
        kt = top - 2 * j
        anchor = step(s_a, p_a, s_b, kt - 1, p_b, kt + 1)
        step(s_b, p_b, s_a, kt - 2, p_a, kt, anchor)
        return c
    lax.fori_loop(0, n_far // 2, pair, 0)

    @pl.when(n_far % 2 == 1)
    def _():
        for c, pv in zip(chains, pending(p_b, 1)):
            st, acc = st_refs[c]
            s = s_a[c][...]
            m_old = st[0:1, :]
            m = jnp.maximum(m_old, jnp.max(s, axis=0, keepdims=True))
            st[0:1, :] = m
            acc[...] = jnp.exp2(m_old - m) * (acc[...] + pv) + _dot(vtile(0, c), _probs(s, m))

    @pl.when(n_far % 2 == 0)
    def _():
        last = jnp.where(n_far == 0, qi - 1, 0)
        for c, pv in zip(chains, pending(p_b, last)):
            acc = st_refs[c][1]
            acc[...] = acc[...] + pv
    return tuple((st[0:1, :], acc[...]) for st, acc in st_refs)


def _ktile(ref, kt):
    return ref[pl.ds(pl.multiple_of(kt * TQ, TQ), TQ), :]


def _nsa_kernel(qa_ref, gt_ref, kcmp_ref, vcmpt_ref, bc_ref, ks_ref, vst_ref, kw_ref, vwt_ref, dn_ref,
                ov_ref, o_ref, psum_ref, sel_ref, oacc_ref, sig_ref, *scratch):
    st_refs, s_a, s_b, p_a, p_b = _stream_refs(scratch, NSA_HEADS)
    qi = pl.program_id(1)
    t0 = qi * TQ
    sub_grp = lax.shift_right_arithmetic(lax.broadcasted_iota(jnp.int32, (LANE, 1), 0), 6)
    sig_ref[...] = jax.nn.sigmoid(gt_ref[...])

    def gate_row(c):
        return sig_ref[pl.ds(c, 1), :]

    def masked_q(r, g):
        return jnp.where(sub_grp == g, qa_ref[r].astype(F32), 0.0).astype(BF16)

    n_cmp = kcmp_ref.shape[1] - 1
    crow = lax.broadcasted_iota(jnp.int32, (LANE, 1), 0)
    cmp_end = jnp.where(crow < n_cmp, crow * CMP_STRIDE + CMP_LEN - 1, 1 << 30)
    mask_c = (t0 + lax.broadcasted_iota(jnp.int32, (1, TQ), 1)) >= cmp_end

    heads = [(r, g) for r in range(NSA_R) for g in range(NSA_G)]
    scores_c = [_dot(kcmp_ref[0], masked_q(r, g)) for r, g in heads]
    probs_c = []
    for (r, g), s in zip(heads, scores_c):
        s = jnp.where(mask_c, s + bc_ref[g * NSA_R + r], NEG)
        p = jnp.where(mask_c, jnp.exp2(s - jnp.max(s, axis=0, keepdims=True)), 0.0)
        l = jnp.sum(p, axis=0, keepdims=True)
        probs_c.append(p * jnp.where(l > 0.0, 1.0 / l, 0.0))
    for g in range(NSA_G):
        psum_ref[g] = sum(p for (r, gg), p in zip(heads, probs_c) if gg == g)
    outs_c = [_dot(vcmpt_ref[0], p.astype(BF16)) for p in probs_c]
    for r in range(NSA_R):
        o0, o1 = outs_c[r * NSA_G], outs_c[r * NSA_G + 1]
        oacc_ref[r] = jnp.where(sub_grp == 0, gate_row(r * 3) * o0, gate_row((NSA_R + r) * 3) * o1)

    has_prev = jnp.where(qi >= 1, 0.0, NEG)
    has_wfar = jnp.where(qi >= WINDOW // TQ, 0.0, NEG)
    kt_prev = jnp.maximum(qi - 1, 0)
    kt_wfar = jnp.maximum(qi - WINDOW // TQ, 0)
    wfar_mask = lax.broadcasted_iota(jnp.int32, (TQ, TQ), 0) > lax.broadcasted_iota(jnp.int32, (TQ, TQ), 1)

    def attend_all_heads():
        chains = [(r, g) for r in range(NSA_R) for g in range(NSA_G)]
        qms = [masked_q(r, g) for r, g in chains]
        hids = [g * NSA_R + r for r, g in chains]

        per_tile = TQ // SEL_BLOCK
        key_blk = lax.shift_right_arithmetic(lax.broadcasted_iota(jnp.int32, (TQ, LANE), 0), 6)
        lane_id = lax.broadcasted_iota(jnp.int32, (TQ, LANE), 1)

        def keys_with_block_onehot(kt, g):
            other = (1 - g) * NSA_HD
            onehot = jnp.where(lane_id - other == key_blk, 1.0, 0.0).astype(BF16)
            in_other = (lane_id >= other) & (lane_id < other + NSA_HD)
            return jnp.where(in_other, onehot, _ktile(ks_ref, kt))

        def query_with_mask_rows(kt, c):
            r, g = chains[c]
            own = qa_ref[r, g * NSA_HD:(g + 1) * NSA_HD, :]
            pair = sel_ref[g, pl.ds(pl.multiple_of((kt // 2) * 2 * per_tile, 2 * per_tile), 2 * per_tile), :]
            rows = jnp.where(kt % 2 == 0, pair[0:per_tile], pair[per_tile:2 * per_tile]).astype(BF16)
            other = jnp.concatenate([rows, jnp.zeros((NSA_HD - per_tile, TQ), BF16)], axis=0)
            return jnp.concatenate([own, other] if g == 0 else [other, own], axis=0)

        def near_bias(s, near, c):
            if near == 0:
                return s + dn_ref[hids[c], 0]
            if near == 1:
                return s + dn_ref[hids[c], 1] + has_prev
            return s

        def group_values(ref, kt, c):
            g = chains[c][1]
            return _with_ones(ref[g * NSA_HD:(g + 1) * NSA_HD, pl.ds(pl.multiple_of(kt * TQ, TQ), TQ)])

        def finish(sts, branch):
            for c, (r, g) in enumerate(chains):
                oacc_ref[r, g * NSA_HD:(g + 1) * NSA_HD, :] += gate_row(hids[c] * 3 + branch) * _flash_out(sts[c])

        def slc_scores(kt, near, c):
            return near_bias(_dot(keys_with_block_onehot(kt, chains[c][1]), query_with_mask_rows(kt, c)), near, c)

        def slc_vtile(kt, c):
            return group_values(vst_ref, kt, c)

        win_tiles = ((qi, 0), (kt_prev, 1), (kt_wfar, 2))

        def win_scores(c):
            out = []
            for kt, near in win_tiles:
                s = near_bias(_dot(_ktile(kw_ref, kt), qms[c]), near, c)
                out.append(jnp.where(wfar_mask, s + has_wfar, NEG) if near == 2 else s)
            return out

        sts = []
        ahead = win_scores(0)
        for c in range(len(chains)):
            tiles, ahead = ahead, (win_scores(c + 1) if c + 1 < len(chains) else None)
            st = _flash_first([tiles[0]], [group_values(vwt_ref, qi, c)])
            st = _flash_update([tiles[1]], [group_values(vwt_ref, kt_prev, c)], st)
            sts.append(_flash_update([tiles[2]], [group_values(vwt_ref, kt_wfar, c)], st)[0])
        finish(sts, 2)

        n_sel = ov_ref.shape[0]
        jj = lax.broadcasted_iota(jnp.int32, (n_sel, TQ), 0)
        cur = lax.shift_right_arithmetic(t0 + lax.broadcasted_iota(jnp.int32, (n_sel, TQ), 1), 6)
        valid = jj <= cur
        forced = (jj == 0) | (cur - jj < SEL_FORCED_LOCAL)
        for g in range(NSA_G):
            ps = psum_ref[g]
            hi = ps.astype(BF16)
            rem = ps - hi.astype(F32)
            mid = rem.astype(BF16)
            lo = (rem - mid.astype(F32)).astype(BF16)
            ov = ov_ref[...]
            imp = _dot(ov, hi) + _dot(ov, mid) + _dot(ov, lo)
            score = jnp.where(valid, jnp.where(forced, 1e9, imp), -1e9)
            cnt = jnp.zeros((n_sel, TQ), F32)
            for j2 in range(n_sel):
                row = score[j2:j2 + 1, :]
                tie = jnp.where(jj > j2, 1.0, 0.0)
                cnt = cnt + jnp.where(row > score, 1.0, jnp.where(row == score, tie, 0.0))
            sel_ref[g] = jnp.where(cnt < float(min(SEL_TOPN, n_sel)), jnp.where(score > -1e8, 0.0, NEG), NEG)

        finish(_causal_stream(qi, slc_scores, slc_vtile, st_refs, s_a, s_b, p_a, p_b), 1)
    attend_all_heads()

    for r in range(NSA_R):
        o_ref[r] = oacc_ref[r].T.astype(BF16)


def _nsa(qa, gates_t, kcmp, vcmpt, bias_c, ks, vst, kw, vwt, near_a, ov, B, S):
    nq = S // TQ
    T = B * S
    k_spec = pl.BlockSpec((S, LANE), lambda b, q: (b, 0))
    v_spec = pl.BlockSpec((LANE, S), lambda b, q: (0, b))
    full = lambda a: pl.BlockSpec(a.shape, lambda b, q: (0,) * a.ndim)
    cmp_spec = pl.BlockSpec((1, LANE, LANE), lambda b, q: (b, 0, 0))
    qo = pl.BlockSpec((4, TQ, LANE), lambda b, q: (0, b * nq + q, 0))
    return pl.pallas_call(
        _nsa_kernel,
        grid=(B, nq),
        in_specs=[pl.BlockSpec((4, LANE, TQ), lambda b, q: (0, 0, b * nq + q)),
                  pl.BlockSpec((GATE_ROWS, TQ), lambda b, q: (0, b * nq + q)),
                  cmp_spec, cmp_spec,
                  pl.BlockSpec((NSA_HEADS, LANE, TQ), lambda b, q: (0, 0, q)),
                  k_spec, v_spec, k_spec, v_spec, full(near_a), full(ov)],
        out_specs=qo,
        out_shape=jax.ShapeDtypeStruct((4, T, LANE), BF16),
        scratch_shapes=[pltpu.VMEM((NSA_G, LANE, TQ), F32), pltpu.VMEM((NSA_G, S // SEL_BLOCK, TQ), F32),
                        pltpu.VMEM((NSA_R, LANE, TQ), F32), pltpu.VMEM((GATE_ROWS, TQ), F32)]
                       + _stream_scratch(NSA_HEADS),
        compiler_params=_cparams(("parallel", "arbitrary")),
        name="nsa",
    )(qa, gates_t, kcmp, vcmpt, bias_c, ks, vst, kw, vwt, near_a, ov)


def _diff_kernel(lq1_ref, lk1_ref, lq2_ref, lk2_ref, sub_ref, qb_ref, kb_ref, vbt_ref, dn_ref, o_ref, *scratch):
    n = DIFF_PAIRS_PER_STEP * 4
    st_refs, s_a, s_b, p_a, p_b = _stream_refs(scratch, n)
    qi = pl.program_id(1)
    chain_of_row = lax.shift_right_arithmetic(lax.broadcasted_iota(jnp.int32, (LANE, 1), 0), 5)
    lam = (jnp.exp(jnp.sum(lq1_ref[...] * lk1_ref[...], axis=-1, keepdims=True))
           - jnp.exp(jnp.sum(lq2_ref[...] * lk2_ref[...], axis=-1, keepdims=True)) + LAMBDA_INIT)
    has_prev = jnp.where(qi >= 1, 0.0, NEG)
    n_steps = DIFF_HEADS // 2 // DIFF_PAIRS_PER_STEP

    def score_fn(step):
        pairs = [step * DIFF_PAIRS_PER_STEP + c // 4 for c in range(n)]
        qs = [qb_ref[step * DIFF_PAIRS_PER_STEP + i].astype(F32) for i in range(DIFF_PAIRS_PER_STEP)]
        qms = [jnp.where(chain_of_row == c % 4, qs[c // 4], 0.0).astype(BF16) for c in range(n)]

        def scores(kt, near, c):
            s = _dot(kb_ref[pairs[c], pl.ds(pl.multiple_of(kt * TQ, TQ), TQ), :], qms[c])
            if near == 2:
                return s
            bias = dn_ref[2 * pairs[c] + (c % 4) // 2, near]
            return s + bias if near == 0 else s + bias + has_prev
        return scores

    def start_diag(step):
        scores = score_fn(step)
        for c in range(n):
            s_a[c][...] = scores(qi, 0, c)

    start_diag(0)

    def step_body(step, carry):
        def vtile(kt, c):
            hh = (c % 4) // 2
            return _with_ones(vbt_ref[step * DIFF_PAIRS_PER_STEP + c // 4, hh * 2 * DIFF_HD:(hh + 1) * 2 * DIFF_HD,
                                      pl.ds(pl.multiple_of(kt * TQ, TQ), TQ)])

        sts = _causal_stream(qi, score_fn(step), vtile, st_refs, s_a, s_b, p_a, p_b, diag_ready=True)
        start_diag(jnp.minimum(step + 1, n_steps - 1))
        for i in range(DIFF_PAIRS_PER_STEP):
            outs = []
            for hh in range(2):
                c = 4 * i + 2 * hh
                o = _flash_out(sts[c]) - lam * _flash_out(sts[c + 1])
                outs.append(o * lax.rsqrt(jnp.mean(o * o, axis=0, keepdims=True) + EPS))
            out = jnp.concatenate(outs, axis=0) * sub_ref[...] * (1.0 - LAMBDA_INIT)
            o_ref[step * DIFF_PAIRS_PER_STEP + i] = out.T.astype(BF16)
        return carry
    lax.fori_loop(0, n_steps, step_body, 0)


def _diff(lq1, lk1, lq2, lk2, sub, qb, kb, vbt, near_b, B, S):
    nq = S // TQ
    T = B * S
    full = lambda a: pl.BlockSpec(a.shape, lambda b, q: (0,) * a.ndim)
    k_spec = pl.BlockSpec((4, S, LANE), lambda b, q: (0, b, 0))
    v_spec = pl.BlockSpec((4, LANE, S), lambda b, q: (0, 0, b))
    qo = pl.BlockSpec((4, TQ, LANE), lambda b, q: (0, b * nq + q, 0))
    return pl.pallas_call(
        _diff_kernel,
        grid=(B, nq),
        in_specs=[full(lq1), full(lk1), full(lq2), full(lk2), full(sub),
                  pl.BlockSpec((4, LANE, TQ), lambda b, q: (0, 0, b * nq + q)), k_spec, v_spec, full(near_b)],
        out_specs=qo,
        out_shape=jax.ShapeDtypeStruct((4, T, LANE), BF16),
        scratch_shapes=_stream_scratch(DIFF_PAIRS_PER_STEP * 4),
        compiler_params=_cparams(("parallel", "arbitrary")),
        name="diff",
    )(lq1, lk1, lq2, lk2, sub, qb, kb, vbt, near_b)


def _outproj_kernel(x_ref, oa_ref, ob_ref, w_ref, g_ref, wr_ref, br_ref, xt_ref, route_ref, cnt_ref, carry_ref):
    @pl.when(pl.program_id(0) == 0)
    def _():
        carry_ref[...] = jnp.zeros(carry_ref.shape, F32)

    o = jnp.concatenate([oa_ref[r] for r in range(4)] + [ob_ref[r] for r in range(4)], axis=-1)
    h = x_ref[...] + _dot(o, w_ref[...])
    xt_ref[:, 0:D_MODEL] = h
    tn = _rmsnorm(h, g_ref[...]).astype(BF16)
    logits = _dot(tn, wr_ref[...]) + br_ref[...]
    lane = lax.broadcasted_iota(jnp.int32, (1, LANE), 1)
    lane_f = lane.astype(F32)
    is_grp = lane < MOE_GROUPS
    lg = jnp.where(is_grp, logits, NEG)
    mg = jnp.max(lg, axis=-1, keepdims=True)
    zg = jnp.sum(jnp.where(is_grp, jnp.exp(lg - mg), 0.0), axis=-1, keepdims=True)
    g_prob = 1.0 / zg
    g_idx = jnp.min(jnp.where(lg == mg, lane_f, 1e9), axis=-1, keepdims=True)
    lane_grp = jnp.where((lane >= MOE_GROUPS) & (lane < MOE_GROUPS + N_EXPERTS),
                         lax.shift_right_arithmetic(lane - MOE_GROUPS, 3), -1).astype(F32)
    le = jnp.where(lane_grp == g_idx, logits, NEG)
    m1 = jnp.max(le, axis=-1, keepdims=True)
    e1 = jnp.min(jnp.where(le == m1, lane_f, 1e9), axis=-1, keepdims=True)
    le2 = jnp.where(lane_f == e1, NEG, le)
    m2 = jnp.max(le2, axis=-1, keepdims=True)
    e2 = jnp.min(jnp.where(le2 == m2, lane_f, 1e9), axis=-1, keepdims=True)
    ratio = jnp.exp(m2 - m1)
    w1 = g_prob / (1.0 + ratio)
    w2 = w1 * ratio
    xt_ref[:, D_MODEL:D_MODEL + LANE] = jnp.where(lane_f == e1, w1, 0.0) + jnp.where(lane_f == e2, w2, 0.0)

    tm = h.shape[0]
    onehot = jnp.where(lane_f == g_idx, 1.0, 0.0)
    earlier = jnp.where(lax.broadcasted_iota(jnp.int32, (tm, tm), 0) > lax.broadcasted_iota(jnp.int32, (tm, tm), 1),
                        1.0, 0.0).astype(BF16)
    prefix = _dot(earlier, onehot.astype(BF16)) + carry_ref[...]
    rank = jnp.sum(onehot * prefix, axis=-1, keepdims=True)
    carry_ref[...] += jnp.sum(onehot, axis=0, keepdims=True)
    route_ref[...] = jnp.where(lane == 0, g_idx, jnp.where(lane == 1, rank, 0.0))
    cnt_ref[...] = jnp.broadcast_to(carry_ref[...], cnt_ref.shape)


def _out_proj(x2, oa, ob, w, g, wr, br):
    T = x2.shape[0]
    tm = TM_PROJ
    row = lambda i: (i, 0)
    full = lambda a: pl.BlockSpec(a.shape, lambda i: (0,) * a.ndim)
    o4 = pl.BlockSpec((4, tm, LANE), lambda i: (0, i, 0))
    return pl.pallas_call(
        _outproj_kernel,
        grid=(T // tm,),
        in_specs=[pl.BlockSpec((tm, D_MODEL), row), o4, o4, full(w), full(g), full(wr), full(br)],
        out_specs=[pl.BlockSpec((tm, XT_WIDTH), row), pl.BlockSpec((tm, LANE), row),
                   pl.BlockSpec((8, LANE), lambda i: (0, 0))],
        out_shape=[jax.ShapeDtypeStruct((T, XT_WIDTH), F32), jax.ShapeDtypeStruct((T, LANE), F32),
                   jax.ShapeDtypeStruct((8, LANE), F32)],
        scratch_shapes=[pltpu.VMEM((1, LANE), F32)],
        compiler_params=_cparams(("arbitrary",)),
        name="out_proj",
    )(x2, oa, ob, w, g, wr, br)


def _moe_kernel(tg_ref, ok_ref, idx_ref, idxn_ref, xt_hbm, g_ref, wg_ref, wu_ref, wd_ref, y_ref, xbuf, sem,
                xb_ref, cmb_ref, fence_sem):
    i = pl.program_id(0)
    slot = i % 2
    tm = xbuf.shape[1]

    def row_copy(index_ref, r, dst_slot):
        return pltpu.make_async_copy(xt_hbm.at[pl.ds(index_ref[0, 0, r], 1), :],
                                     xbuf.at[dst_slot, pl.ds(r, 1), :], sem.at[dst_slot])

    @pl.when(i == 0)
    def _():
        def body(r, c):
            row_copy(idx_ref, r, 0).start()
            return c
        lax.fori_loop(0, tm, body, 0)

    @pl.when((i == 0) | (ok_ref[jnp.maximum(i - 1, 0)] == 1))
    def _():
        pltpu.make_async_copy(xt_hbm.at[pl.ds(0, tm), :], xbuf.at[slot], sem.at[slot]).wait()

    @pl.when(ok_ref[i] == 0)
    def _():
        y_ref[...] = jnp.zeros(y_ref.shape, F32)

    @pl.when(ok_ref[i] == 1)
    def _():
        lane = lax.broadcasted_iota(jnp.int32, (1, LANE), 1)
        xb_ref[...] = _rmsnorm(xbuf[slot, :, 0:D_MODEL], g_ref[...]).astype(BF16)
        cmb_ref[...] = xbuf[slot, :, D_MODEL:D_MODEL + LANE]
        first_lane = MOE_GROUPS + tg_ref[i] * EPG
        per_expert = -(-tm // (EPG - 1))
        y = jnp.zeros((tm, D_MODEL), F32)
        issued = jnp.int32(0)
        for e in range(EPG):
            x = xb_ref[...]
            a = _dot(x, wg_ref[0, e])
            b = _dot(x, wu_ref[0, e])
            ce = jnp.sum(jnp.where(lane == first_lane + e + issued, cmb_ref[...], 0.0), axis=-1, keepdims=True)
            y = y + _dot(((a * jax.nn.sigmoid(a)) * b * ce).astype(BF16), wd_ref[0, e])
            for r in range(e * per_expert, min((e + 1) * per_expert, tm)):
                row_copy(idxn_ref, r, 1 - slot).start(priority=r % 2)
            if e in ANCHOR_AFTER:
                issued = pl.semaphore_read(fence_sem)
        for blk in range(ROW_SLABS):
            y_ref[pl.ds(blk, tm, stride=ROW_SLABS), :] = y[:, blk * LANE:(blk + 1) * LANE]


def _moe(tile_group, tile_ok, src3, xt, g, wg, wu, wd):
    n_tiles, _, tm = src3.shape
    last = n_tiles - 1
    grid_spec = pltpu.PrefetchScalarGridSpec(
        num_scalar_prefetch=2,
        grid=(n_tiles,),
        in_specs=[pl.BlockSpec((1, 1, tm), lambda i, tg, ok: (i, 0, 0), memory_space=pltpu.SMEM),
                  pl.BlockSpec((1, 1, tm), lambda i, tg, ok: (jnp.minimum(i + 1, last), 0, 0), memory_space=pltpu.SMEM),
                  pl.BlockSpec(memory_space=pl.ANY),
                  pl.BlockSpec((1, D_MODEL), lambda i, tg, ok: (0, 0)),
                  pl.BlockSpec((1, EPG, D_MODEL, EXPERT_FF), lambda i, tg, ok: (tg[i], 0, 0, 0)),
                  pl.BlockSpec((1, EPG, D_MODEL, EXPERT_FF), lambda i, tg, ok: (tg[i], 0, 0, 0)),
                  pl.BlockSpec((1, EPG, EXPERT_FF, D_MODEL), lambda i, tg, ok: (tg[i], 0, 0, 0))],
        out_specs=pl.BlockSpec((tm * ROW_SLABS, LANE), lambda i, tg, ok: (i, 0)),
        scratch_shapes=[pltpu.VMEM((2, tm, XT_WIDTH), F32), pltpu.SemaphoreType.DMA((2,)),
                        pltpu.VMEM((tm, D_MODEL), BF16), pltpu.VMEM((tm, LANE), F32), pltpu.SemaphoreType.REGULAR],
    )
    return pl.pallas_call(
        _moe_kernel,
        grid_spec=grid_spec,
        out_shape=jax.ShapeDtypeStruct((n_tiles * tm * ROW_SLABS, LANE), F32),
        compiler_params=pltpu.CompilerParams(dimension_semantics=("arbitrary",), vmem_limit_bytes=VMEM_LIMIT_MOE),
        name="moe",
    )(tile_group, tile_ok, src3, src3, xt, g, wg, wu, wd)


def _final_kernel(pos_ref, posn_ref, y_hbm, h_ref, gf_ref, o_ref, ybuf, sem):
    i = pl.program_id(0)
    slot = i % 2
    tm = h_ref.shape[0]

    def issue(index_ref, dst_slot):
        group = 8

        def body(j, c):
            for k in range(group):
                r = j * group + k
                src_row = pl.multiple_of(index_ref[0, 0, r], ROW_SLABS)
                pltpu.make_async_copy(y_hbm.at[pl.ds(src_row, ROW_SLABS), :],
                                      ybuf.at[dst_slot, pl.ds(r * ROW_SLABS, ROW_SLABS), :],
                                      sem.at[dst_slot]).start(priority=k % 2)
            return c
        lax.fori_loop(0, tm // group, body, 0)

    @pl.when(i == 0)
    def _():
        issue(pos_ref, 0)

    @pl.when(i + 1 < pl.num_programs(0))
    def _():
        issue(posn_ref, 1 - slot)

    pltpu.make_async_copy(ybuf.at[slot], ybuf.at[slot], sem.at[slot]).wait()
    y = jnp.concatenate([ybuf[slot, pl.ds(blk, tm, stride=ROW_SLABS), :] for blk in range(ROW_SLABS)], axis=-1)
    h = h_ref[...] + y
    o_ref[...] = h * lax.rsqrt(jnp.mean(h * h, axis=-1, keepdims=True) + EPS) * gf_ref[...]


def _final(pos3, y_sorted, h, gf):
    n_tiles, _, tm = pos3.shape
    last = n_tiles - 1
    row = lambda i: (i, 0)
    return pl.pallas_call(
        _final_kernel,
        grid=(n_tiles,),
        in_specs=[pl.BlockSpec((1, 1, tm), lambda i: (i, 0, 0), memory_space=pltpu.SMEM),
                  pl.BlockSpec((1, 1, tm), lambda i: (jnp.minimum(i + 1, last), 0, 0), memory_space=pltpu.SMEM),
                  pl.BlockSpec(memory_space=pl.ANY),
                  pl.BlockSpec((tm, D_MODEL), row),
                  pl.BlockSpec((1, D_MODEL), lambda i: (0, 0))],
        out_specs=pl.BlockSpec((tm, D_MODEL), row),
        out_shape=jax.ShapeDtypeStruct((n_tiles * tm, D_MODEL), F32),
        scratch_shapes=[pltpu.VMEM((2, tm * ROW_SLABS, LANE), F32), pltpu.SemaphoreType.DMA((2,))],
        compiler_params=_cparams(("arbitrary",)),
        name="final",
    )(pos3, pos3, y_sorted, h, gf)


def _qa_perm():
    new = np.arange(NSA_HEADS * NSA_HD)
    r, g, d = new // LANE, (new % LANE) // NSA_HD, new % NSA_HD
    return (g * NSA_R + r) * NSA_HD + d


def _block_diag2(w):
    z = jnp.zeros_like(w)
    return jnp.concatenate([jnp.concatenate([w, z], axis=-1), jnp.concatenate([z, w], axis=-1)], axis=-2)


def kernel(x, rel_bias, ln_mix, w_in, cmp_pos_k, cmp_pos_v, cmp_k_w1, cmp_k_w2, cmp_v_w1, cmp_v_w2,
           diff_lq1, diff_lk1, diff_lq2, diff_lk2, diff_subln, w_out, ln_ffn,
           router_group_w, router_group_b, router_expert_w, router_expert_b,
           exp_w_gate, exp_w_up, exp_w_down, ln_final):
    B, S, D = x.shape
    T = B * S
    assert D == D_MODEL and S % TQ == 0 and S >= WINDOW and T % TM_MOE == 0 and T % TM_FINAL == 0
    x2 = x.reshape(T, D)
    perm = _qa_perm()

    w = w_in[0]
    c_kc, c_vc, c_ks, c_vs, c_kw, c_vw, c_gt = 512, 640, 768, 896, 1024, 1152, 1280
    c_qb = c_gt + N_GATE
    c_kb, c_vb = c_qb + 512, c_qb + 1024
    col = lambda c, n=LANE: w[:, c:c + n]
    w_tok = jnp.concatenate([col(c_kc), col(c_vc), col(c_ks), col(c_kw), col(c_kb, 512)], axis=1).astype(BF16)
    w_feat = jnp.concatenate([w[:, perm], col(c_qb, 512), col(c_vs), col(c_vw), col(c_vb, 512), col(c_gt, N_GATE),
                              jnp.zeros((D, GATE_ROWS - N_GATE), F32)], axis=1).T.astype(BF16)
    qa, kc, vc, ks, kw, qb, kb, vst, vwt, vbt, gates_t = _in_proj(x2, ln_mix[0][None, :], w_tok, w_feat)

    w1k = _block_diag2(cmp_k_w1[0].reshape(CMP_LEN, NSA_HD, CMP_HIDDEN)).astype(BF16)
    w1v = _block_diag2(cmp_v_w1[0].reshape(CMP_LEN, NSA_HD, CMP_HIDDEN)).astype(BF16)
    w2k = _block_diag2(cmp_k_w2[0]).astype(BF16)
    w2v = _block_diag2(cmp_v_w2[0]).astype(BF16)
    posk = jnp.tile(cmp_pos_k[0], (1, NSA_G))
    posv = jnp.tile(cmp_pos_v[0], (1, NSA_G))
    kcmp, vcmpt = _compress(kc, vc, posk, posv, w1k, w1v, w2k, w2v, B, S)

    near, bias_c = _bias_tiles(rel_bias.T, S)

    n_sel = S // SEL_BLOCK
    nrow = S // CMP_STRIDE
    c_start = np.arange(nrow) * CMP_STRIDE
    s_start = np.arange(n_sel) * SEL_BLOCK
    ov = ((c_start[None, :] <= s_start[:, None] + SEL_BLOCK - 1)
          & (c_start[None, :] + CMP_LEN - 1 >= s_start[:, None])
          & (np.arange(nrow)[None, :] < nrow - 1)).astype(np.float32)
    o_a = _nsa(qa, gates_t, kcmp, vcmpt, bias_c, ks, vst, kw, vwt, near[:NSA_HEADS], jnp.asarray(ov, BF16), B, S)

    sub = jnp.tile(diff_subln[0], 2)[:, None]
    o_b = _diff(diff_lq1[0][None, :], diff_lk1[0][None, :], diff_lq2[0][None, :], diff_lk2[0][None, :],
                sub, qb, kb, vbt, near[NSA_HEADS:], B, S)

    w_o = jnp.concatenate([w_out[0][:512][perm], w_out[0][512:]], axis=0).astype(BF16)
    n_r = MOE_GROUPS + N_EXPERTS
    wr = jnp.concatenate([router_group_w[0], router_expert_w[0], jnp.zeros((D, LANE - n_r), F32)], axis=1).astype(BF16)
    br = jnp.concatenate([router_group_b[0], router_expert_b[0], jnp.zeros((LANE - n_r,), F32)])[None, :]
    xt, route, counts = _out_proj(x2, o_a, o_b, w_o, ln_ffn[0][None, :], wr, br)

    tm = TM_MOE
    n_tiles = T // tm + MOE_GROUPS
    cnt = counts[0, :MOE_GROUPS].astype(jnp.int32)
    ends = jnp.cumsum((cnt + tm - 1) // tm * tm)
    starts = ends - (cnt + tm - 1) // tm * tm
    pos = starts[route[:, 0].astype(jnp.int32)] + route[:, 1].astype(jnp.int32)
    src = jnp.zeros((n_tiles * tm,), jnp.int32).at[pos].set(
        jnp.arange(T, dtype=jnp.int32), unique_indices=True, mode="promise_in_bounds")
    tile_start = jnp.arange(n_tiles, dtype=jnp.int32) * tm
    tile_group = jnp.minimum(jnp.searchsorted(ends, tile_start, side="right"), MOE_GROUPS - 1).astype(jnp.int32)
    tile_ok = (tile_start < ends[-1]).astype(jnp.int32)

    by_group = lambda a: a[0].astype(BF16).reshape((MOE_GROUPS, EPG) + a.shape[2:])
    y_sorted = _moe(tile_group, tile_ok, src.reshape(n_tiles, 1, tm), xt, ln_ffn[0][None, :],
                    by_group(exp_w_gate), by_group(exp_w_up), by_group(exp_w_down))
    out = _final((pos * ROW_SLABS).reshape(T // TM_FINAL, 1, TM_FINAL), y_sorted, xt, ln_final[None, :])
    return out.reshape(B, S, D)
```

```python
import math

import numpy as np
import jax
import jax.numpy as jnp
from jax import lax
from jax.experimental import pallas as pl
from jax.experimental.pallas import tpu as pltpu

F32 = jnp.float32
BF16 = jnp.bfloat16
NEG = -1e30
EPS = 1e-6
LOG2E = math.log2(math.e)

D_MODEL = 1024
LANE = 128
NSA_HEADS, NSA_G, NSA_R, NSA_HD = 8, 2, 4, 64
CMP_LEN, CMP_STRIDE, CMP_HIDDEN = 32, 16, 128
SEL_BLOCK, SEL_TOPN, SEL_FORCED_LOCAL, WINDOW = 64, 8, 2, 512
DIFF_HEADS, DIFF_HD = 8, 32
REL_BUCKETS, REL_MAX_EXACT, REL_MAX_DIST = 32, 16, 128
N_REL_HEADS = NSA_HEADS + DIFF_HEADS
MOE_GROUPS, EPG, N_EXPERTS, EXPERT_FF = 4, 8, 32, 256
LAMBDA_INIT = 0.8 - 0.6 * math.exp(-0.3 * 0)
N_GATE = NSA_HEADS * 3
GATE_ROWS = 32

TQ = 256
DIFF_PAIRS_PER_STEP = 4
TM_PROJ = 512
TM_MOE = 512
TM_FINAL = 512
ANCHOR_AFTER = (1, 3, 5, 6)
XT_WIDTH = D_MODEL + LANE
ROW_SLABS = D_MODEL // LANE
VMEM_LIMIT = 48 * 1024 * 1024
VMEM_LIMIT_MOE = 56 * 1024 * 1024


def _cparams(sem):
    return pltpu.CompilerParams(dimension_semantics=sem, vmem_limit_bytes=VMEM_LIMIT)


def _dot(a, b):
    return jnp.dot(a, b, preferred_element_type=F32)


def _rmsnorm(x, g):
    return x * lax.rsqrt(jnp.mean(x * x, axis=-1, keepdims=True) + EPS) * g


def _dot_nt(a, b):
    return lax.dot_general(a, b, (((1,), (1,)), ((), ())), preferred_element_type=F32)


def _bucket_thresholds():
    n = np.arange(0, REL_MAX_DIST + 1)
    nf = np.maximum(n, 1).astype(np.float32)
    large = REL_MAX_EXACT + (np.log(nf / np.float32(REL_MAX_EXACT)) / np.float32(math.log(REL_MAX_DIST / REL_MAX_EXACT))
                             * np.float32(REL_BUCKETS - REL_MAX_EXACT)).astype(np.int32)
    large = np.minimum(large, REL_BUCKETS - 1)
    bucket = np.where(n < REL_MAX_EXACT, n, large)
    return [int(np.argmax(bucket >= b)) for b in range(REL_BUCKETS)]


_THR = _bucket_thresholds()


def _inproj_kernel(x_ref, g_ref, w_ref, wt_ref, qa_ref, kc_ref, vc_ref, ks_ref, kw_ref, qb_ref, kb_ref,
                   vst_ref, vwt_ref, vbt_ref, gt_ref):
    x = x_ref[...]
    xn = (x * lax.rsqrt(jnp.mean(x * x, axis=-1, keepdims=True) + EPS) * g_ref[...]).astype(BF16)
    a = _dot(xn, w_ref[:, 0:512])
    kc_ref[...] = a[:, 0:128]
    vc_ref[...] = a[:, 128:256]
    ks_ref[...] = a[:, 256:384].astype(BF16)
    kw_ref[...] = a[:, 384:512].astype(BF16)
    a = _dot(xn, w_ref[:, 512:1024])
    for r in range(4):
        kb_ref[r] = a[:, r * LANE:(r + 1) * LANE].astype(BF16)

    ft = _dot_nt(wt_ref[...], xn)

    def feat(row0, rows=LANE):
        return ft[row0:row0 + rows, :]
    for r in range(4):
        qa_ref[r] = (feat(r * LANE) * (NSA_HD ** -0.5 * LOG2E)).astype(BF16)
        qb_ref[r] = (feat(512 + r * LANE) * (DIFF_HD ** -0.5 * LOG2E)).astype(BF16)
        vbt_ref[r] = feat(1280 + r * LANE).astype(BF16)
    vst_ref[...] = feat(1024).astype(BF16)
    vwt_ref[...] = feat(1152).astype(BF16)
    gt_ref[...] = feat(1792, GATE_ROWS)


def _in_proj(x2, g, w, wt):
    T = x2.shape[0]
    tm = TM_PROJ
    row = lambda i: (i, 0)
    o128b = jax.ShapeDtypeStruct((T, LANE), BF16)
    o128f = jax.ShapeDtypeStruct((T, LANE), F32)
    o4 = jax.ShapeDtypeStruct((4, T, LANE), BF16)
    ot = jax.ShapeDtypeStruct((LANE, T), BF16)
    o4t = jax.ShapeDtypeStruct((4, LANE, T), BF16)
    s128 = pl.BlockSpec((tm, LANE), row)
    s4 = pl.BlockSpec((4, tm, LANE), lambda i: (0, i, 0))
    st = pl.BlockSpec((LANE, tm), lambda i: (0, i))
    s4t = pl.BlockSpec((4, LANE, tm), lambda i: (0, 0, i))
    return pl.pallas_call(
        _inproj_kernel,
        grid=(T // tm,),
        in_specs=[pl.BlockSpec((tm, D_MODEL), row),
                  pl.BlockSpec((1, D_MODEL), lambda i: (0, 0)),
                  pl.BlockSpec(w.shape, lambda i: (0, 0)),
                  pl.BlockSpec(wt.shape, lambda i: (0, 0))],
        out_specs=[s4t, s128, s128, s128, s128, s4t, s4, st, st, s4t, pl.BlockSpec((GATE_ROWS, tm), lambda i: (0, i))],
        out_shape=[o4t, o128f, o128f, o128b, o128b, o4t, o4, ot, ot, o4t, jax.ShapeDtypeStruct((GATE_ROWS, T), F32)],
        compiler_params=_cparams(("parallel",)),
        name="in_proj",
    )(x2, g, w, wt)


def _gelu_tanh(x):
    return 0.5 * x * (1.0 + jnp.tanh(math.sqrt(2.0 / math.pi) * (x + 0.044715 * (x * x * x))))


def _compress_kernel(kc_ref, vc_ref, posk_ref, posv_ref, w1k_ref, w1v_ref, w2k_ref, w2v_ref, ko_ref, vo_ref):
    nrow = kc_ref.shape[0] // CMP_STRIDE
    rid = lax.broadcasted_iota(jnp.int32, (nrow, 1), 0)
    cid = lax.broadcasted_iota(jnp.int32, (1, nrow), 1)
    for src, pos, w1, w2, out, transposed in ((kc_ref, posk_ref, w1k_ref, w2k_ref, ko_ref, False),
                                              (vc_ref, posv_ref, w1v_ref, w2v_ref, vo_ref, True)):
        hid_a = jnp.zeros((nrow, 2 * CMP_HIDDEN), F32)
        hid_b = jnp.zeros((nrow, 2 * CMP_HIDDEN), F32)
        for m in range(CMP_STRIDE):
            y = src[pl.ds(m, nrow, stride=CMP_STRIDE), :]
            hid_a = hid_a + _dot((y + pos[m:m + 1, :]).astype(BF16), w1[m])
            hid_b = hid_b + _dot((y + pos[CMP_STRIDE + m:CMP_STRIDE + m + 1, :]).astype(BF16), w1[CMP_STRIDE + m])
        hid = hid_a + pltpu.roll(hid_b, nrow - 1, 0)
        o = _dot(_gelu_tanh(hid).astype(BF16), w2[...])
        if transposed:
            out[0] = jnp.where(cid < nrow - 1, o.T, 0.0).astype(BF16)
        else:
            out[0] = jnp.where(rid < nrow - 1, o, 0.0).astype(BF16)


def _compress(kc, vc, posk, posv, w1k, w1v, w2k, w2v, B, S):
    nrow = S // CMP_STRIDE
    assert nrow == LANE
    full = lambda a: pl.BlockSpec(a.shape, lambda b: (0,) * a.ndim)
    src = pl.BlockSpec((S, LANE), lambda b: (b, 0))
    osp = pl.BlockSpec((1, nrow, LANE), lambda b: (b, 0, 0))
    osh = jax.ShapeDtypeStruct((B, nrow, LANE), BF16)
    return pl.pallas_call(
        _compress_kernel,
        grid=(B,),
        in_specs=[src, src, full(posk), full(posv), full(w1k), full(w1v), full(w2k), full(w2v)],
        out_specs=[osp, osp],
        out_shape=[osh, osh],
        compiler_params=_cparams(("parallel",)),
        name="compress",
    )(kc, vc, posk, posv, w1k, w1v, w2k, w2v)


def _bias_from_dist(dist, tab_ref, h):
    val = jnp.full(dist.shape, tab_ref[h, 0], F32)
    for b in range(1, REL_BUCKETS):
        val = jnp.where(dist >= _THR[b], tab_ref[h, b], val)
    return val


def _bias_near_kernel(tab_ref, out_ref):
    h = pl.program_id(0)
    j = lax.broadcasted_iota(jnp.int32, (TQ, TQ), 0)
    i = lax.broadcasted_iota(jnp.int32, (TQ, TQ), 1)
    for d in range(2):
        bias = (_bias_from_dist(i - j + d * TQ, tab_ref, h) - tab_ref[h, REL_BUCKETS - 1]) * LOG2E
        out_ref[0, d] = jnp.where(i - j + d * TQ >= 0, bias, NEG)


def _bias_cmp_kernel(tab_ref, out_ref):
    h = pl.program_id(0)
    c = lax.broadcasted_iota(jnp.int32, (LANE, TQ), 0)
    t = pl.program_id(1) * TQ + lax.broadcasted_iota(jnp.int32, (LANE, TQ), 1)
    out_ref[0] = _bias_from_dist(t - (c * CMP_STRIDE + CMP_LEN - 1), tab_ref, h) * LOG2E


def _bias_tiles(tab_t, S):
    smem = pl.BlockSpec(memory_space=pltpu.SMEM)
    near = pl.pallas_call(
        _bias_near_kernel,
        grid=(N_REL_HEADS,),
        in_specs=[smem],
        out_specs=pl.BlockSpec((1, 2, TQ, TQ), lambda h: (h, 0, 0, 0)),
        out_shape=jax.ShapeDtypeStruct((N_REL_HEADS, 2, TQ, TQ), F32),
        compiler_params=_cparams(("parallel",)),
        name="bias_near",
    )(tab_t)
    cmp_bias = pl.pallas_call(
        _bias_cmp_kernel,
        grid=(NSA_HEADS, S // TQ),
        in_specs=[smem],
        out_specs=pl.BlockSpec((1, LANE, TQ), lambda h, q: (h, 0, q)),
        out_shape=jax.ShapeDtypeStruct((NSA_HEADS, LANE, S), F32),
        compiler_params=_cparams(("parallel", "parallel")),
        name="bias_cmp",
    )(tab_t)
    return near, cmp_bias


ACC_ROWS = NSA_HD + 16


def _with_ones(vt):
    return jnp.concatenate([vt, jnp.ones((ACC_ROWS - vt.shape[0], vt.shape[1]), BF16)], axis=0)


def _probs(s, m):
    return jnp.exp2((s - m).astype(BF16))


def _flash_first(ss, vts):
    ms = [jnp.max(s, axis=0, keepdims=True) for s in ss]
    ps = [_probs(s, m) for s, m in zip(ss, ms)]
    return tuple((m, _dot(vt, p)) for m, p, vt in zip(ms, ps, vts))


def _flash_update(ss, vts, sts):
    ms = [jnp.maximum(st[0], jnp.max(s, axis=0, keepdims=True)) for s, st in zip(ss, sts)]
    alphas = [jnp.exp2(st[0] - m) for m, st in zip(ms, sts)]
    ps = [_probs(s, m) for s, m in zip(ss, ms)]
    return tuple((m, a * st[1] + _dot(vt, p)) for m, a, st, p, vt in zip(ms, alphas, sts, ps, vts))


def _flash_out(st):
    acc = st[1]
    return acc[0:NSA_HD, :] * (1.0 / acc[NSA_HD:NSA_HD + 1, :])


def _stream_scratch(n):
    return ([pltpu.VMEM((8, TQ), F32), pltpu.VMEM((ACC_ROWS, TQ), F32)] * n
            + [pltpu.VMEM((TQ, TQ), F32)] * (2 * n) + [pltpu.VMEM((TQ, TQ), BF16)] * (2 * n))


def _stream_refs(refs, n):
    st_refs = tuple((refs[2 * c], refs[2 * c + 1]) for c in range(n))
    s, p = refs[2 * n:4 * n], refs[4 * n:6 * n]
    return st_refs, tuple(s[:n]), tuple(s[n:]), tuple(p[:n]), tuple(p[n:])


def _causal_stream(qi, scores, vtile, st_refs, s_a, s_b, p_a, p_b, diag_ready=False):
    n_far = jnp.maximum(qi - 1, 0)
    top = qi - 2

    chains = range(len(st_refs))

    def write(s_refs, kt, near=2):
        for c in chains:
            s_refs[c][...] = scores(jnp.maximum(kt, 0), near, c)

    def pending(p_refs, kt):
        return [_dot(vtile(jnp.maximum(kt, 0), c), p_refs[c][...]) for c in chains]

    def step(s_cur, p_cur, s_next, kt_next, p_prev, kt_prev, anchor=None):
        rescale = []
        for c in chains:
            nxt = scores(jnp.maximum(kt_next, 0), 2, c)
            s_next[c][...] = nxt
            pv = _dot(vtile(jnp.maximum(kt_prev, 0), c), p_prev[c][...])
            st = st_refs[c][0]
            s = s_cur[c][...]
            m_old = st[0:1, :]
            m = jnp.maximum(m_old, jnp.max(s, axis=0, keepdims=True))
            if anchor is not None:
                m = m + anchor
            anchor = nxt[0:1, :] * 0.0
            p_cur[c][...] = _probs(s, m)
            st[0:1, :] = m
            rescale.append((jnp.exp2(m_old - m), pv))
        for c in chains:
            acc = st_refs[c][1]
            acc[...] = rescale[c][0] * (acc[...] + rescale[c][1])
        return anchor

    if not diag_ready:
        write(s_a, qi, 0)
    for c in chains:
        s_b[c][...] = scores(jnp.maximum(qi - 1, 0), 1, c)
        st, acc = st_refs[c]
        s = s_a[c][...]
        m = jnp.max(s, axis=0, keepdims=True)
        p_a[c][...] = _probs(s, m)
        st[0:1, :] = m
        acc[...] = jnp.zeros(acc.shape, F32)
    step(s_b, p_b, s_a, top, p_a, qi)

    def pair(j, c):
        kt = top - 2 * j
        anchor = step(s_a, p_a, s_b, kt - 1, p_b, kt + 1)
        step(s_b, p_b, s_a, kt - 2, p_a, kt, anchor)
        return c
    lax.fori_loop(0, n_far // 2, pair, 0)

    @pl.when(n_far % 2 == 1)
    def _():
        for c, pv in zip(chains, pending(p_b, 1)):
            st, acc = st_refs[c]
            s = s_a[c][...]
            m_old = st[0:1, :]
            m = jnp.maximum(m_old, jnp.max(s, axis=0, keepdims=True))
            st[0:1, :] = m
            acc[...] = jnp.exp2(m_old - m) * (acc[...] + pv) + _dot(vtile(0, c), _probs(s, m))

    @pl.when(n_far % 2 == 0)
    def _():
        last = jnp.where(n_far == 0, qi - 1, 0)
        for c, pv in zip(chains, pending(p_b, last)):
            acc = st_refs[c][1]
            acc[...] = acc[...] + pv
    return tuple((st[0:1, :], acc[...]) for st, acc in st_refs)


def _ktile(ref, kt):
    return ref[pl.ds(pl.multiple_of(kt * TQ, TQ), TQ), :]


def _nsa_kernel(qa_ref, gt_ref, kcmp_ref, vcmpt_ref, bc_ref, ks_ref, vst_ref, kw_ref, vwt_ref, dn_ref,
                ov_ref, o_ref, psum_ref, sel_ref, oacc_ref, sig_ref, *scratch):
    st_refs, s_a, s_b, p_a, p_b = _stream_refs(scratch, NSA_HEADS)
    qi = pl.program_id(1)
    t0 = qi * TQ
    sub_grp = lax.shift_right_arithmetic(lax.broadcasted_iota(jnp.int32, (LANE, 1), 0), 6)
    sig_ref[...] = jax.nn.sigmoid(gt_ref[...])

    def gate_row(c):
        return sig_ref[pl.ds(c, 1), :]

    def masked_q(r, g):
        return jnp.where(sub_grp == g, qa_ref[r].astype(F32), 0.0).astype(BF16)

    n_cmp = kcmp_ref.shape[1] - 1
    crow = lax.broadcasted_iota(jnp.int32, (LANE, 1), 0)
    cmp_end = jnp.where(crow < n_cmp, crow * CMP_STRIDE + CMP_LEN - 1, 1 << 30)
    mask_c = (t0 + lax.broadcasted_iota(jnp.int32, (1, TQ), 1)) >= cmp_end

    heads = [(r, g) for r in range(NSA_R) for g in range(NSA_G)]
    scores_c = [_dot(kcmp_ref[0], masked_q(r, g)) for r, g in heads]
    probs_c = []
    for (r, g), s in zip(heads, scores_c):
        s = jnp.where(mask_c, s + bc_ref[g * NSA_R + r], NEG)
        p = jnp.where(mask_c, jnp.exp2(s - jnp.max(s, axis=0, keepdims=True)), 0.0)
        l = jnp.sum(p, axis=0, keepdims=True)
        probs_c.append(p * jnp.where(l > 0.0, 1.0 / l, 0.0))
    for g in range(NSA_G):
        psum_ref[g] = sum(p for (r, gg), p in zip(heads, probs_c) if gg == g)
    outs_c = [_dot(vcmpt_ref[0], p.astype(BF16)) for p in probs_c]
    for r in range(NSA_R):
        o0, o1 = outs_c[r * NSA_G], outs_c[r * NSA_G + 1]
        oacc_ref[r] = jnp.where(sub_grp == 0, gate_row(r * 3) * o0, gate_row((NSA_R + r) * 3) * o1)

    has_prev = jnp.where(qi >= 1, 0.0, NEG)
    has_wfar = jnp.where(qi >= WINDOW // TQ, 0.0, NEG)
    kt_prev = jnp.maximum(qi - 1, 0)
    kt_wfar = jnp.maximum(qi - WINDOW // TQ, 0)
    wfar_mask = lax.broadcasted_iota(jnp.int32, (TQ, TQ), 0) > lax.broadcasted_iota(jnp.int32, (TQ, TQ), 1)

    def attend_all_heads():
        chains = [(r, g) for r in range(NSA_R) for g in range(NSA_G)]
        qms = [masked_q(r, g) for r, g in chains]
        hids = [g * NSA_R + r for r, g in chains]

        per_tile = TQ // SEL_BLOCK
        key_blk = lax.shift_right_arithmetic(lax.broadcasted_iota(jnp.int32, (TQ, LANE), 0), 6)
        lane_id = lax.broadcasted_iota(jnp.int32, (TQ, LANE), 1)

        def keys_with_block_onehot(kt, g):
            other = (1 - g) * NSA_HD
            onehot = jnp.where(lane_id - other == key_blk, 1.0, 0.0).astype(BF16)
            in_other = (lane_id >= other) & (lane_id < other + NSA_HD)
            return jnp.where(in_other, onehot, _ktile(ks_ref, kt))

        def query_with_mask_rows(kt, c):
            r, g = chains[c]
            own = qa_ref[r, g * NSA_HD:(g + 1) * NSA_HD, :]
            pair = sel_ref[g, pl.ds(pl.multiple_of((kt // 2) * 2 * per_tile, 2 * per_tile), 2 * per_tile), :]
            rows = jnp.where(kt % 2 == 0, pair[0:per_tile], pair[per_tile:2 * per_tile]).astype(BF16)
            other = jnp.concatenate([rows, jnp.zeros((NSA_HD - per_tile, TQ), BF16)], axis=0)
            return jnp.concatenate([own, other] if g == 0 else [other, own], axis=0)

        def near_bias(s, near, c):
            if near == 0:
                return s + dn_ref[hids[c], 0]
            if near == 1:
                return s + dn_ref[hids[c], 1] + has_prev
            return s

        def group_values(ref, kt, c):
            g = chains[c][1]
            return _with_ones(ref[g * NSA_HD:(g + 1) * NSA_HD, pl.ds(pl.multiple_of(kt * TQ, TQ), TQ)])

        def finish(sts, branch):
            for c, (r, g) in enumerate(chains):
                oacc_ref[r, g * NSA_HD:(g + 1) * NSA_HD, :] += gate_row(hids[c] * 3 + branch) * _flash_out(sts[c])

        def slc_scores(kt, near, c):
            return near_bias(_dot(keys_with_block_onehot(kt, chains[c][1]), query_with_mask_rows(kt, c)), near, c)

        def slc_vtile(kt, c):
            return group_values(vst_ref, kt, c)

        win_tiles = ((qi, 0), (kt_prev, 1), (kt_wfar, 2))

        def win_scores(c):
            out = []
            for kt, near in win_tiles:
                s = near_bias(_dot(_ktile(kw_ref, kt), qms[c]), near, c)
                out.append(jnp.where(wfar_mask, s + has_wfar, NEG) if near == 2 else s)
            return out

        sts = []
        ahead = win_scores(0)
        for c in range(len(chains)):
            tiles, ahead = ahead, (win_scores(c + 1) if c + 1 < len(chains) else None)
            st = _flash_first([tiles[0]], [group_values(vwt_ref, qi, c)])
            st = _flash_update([tiles[1]], [group_values(vwt_ref, kt_prev, c)], st)
            sts.append(_flash_update([tiles[2]], [group_values(vwt_ref, kt_wfar, c)], st)[0])
        finish(sts, 2)

        n_sel = ov_ref.shape[0]
        jj = lax.broadcasted_iota(jnp.int32, (n_sel, TQ), 0)
        cur = lax.shift_right_arithmetic(t0 + lax.broadcasted_iota(jnp.int32, (n_sel, TQ), 1), 6)
        valid = jj <= cur
        forced = (jj == 0) | (cur - jj < SEL_FORCED_LOCAL)
        for g in range(NSA_G):
            ps = psum_ref[g]
            hi = ps.astype(BF16)
            rem = ps - hi.astype(F32)
            mid = rem.astype(BF16)
            lo = (rem - mid.astype(F32)).astype(BF16)
            ov = ov_ref[...]
            imp = _dot(ov, hi) + _dot(ov, mid) + _dot(ov, lo)
            score = jnp.where(valid, jnp.where(forced, 1e9, imp), -1e9)
            cnt = jnp.zeros((n_sel, TQ), F32)
            for j2 in range(n_sel):
                row = score[j2:j2 + 1, :]
                tie = jnp.where(jj > j2, 1.0, 0.0)
                cnt = cnt + jnp.where(row > score, 1.0, jnp.where(row == score, tie, 0.0))
            sel_ref[g] = jnp.where(cnt < float(min(SEL_TOPN, n_sel)), jnp.where(score > -1e8, 0.0, NEG), NEG)

        finish(_causal_stream(qi, slc_scores, slc_vtile, st_refs, s_a, s_b, p_a, p_b), 1)
    attend_all_heads()

    for r in range(NSA_R):
        o_ref[r] = oacc_ref[r].T.astype(BF16)


def _nsa(qa, gates_t, kcmp, vcmpt, bias_c, ks, vst, kw, vwt, near_a, ov, B, S):
    nq = S // TQ
    T = B * S
    k_spec = pl.BlockSpec((S, LANE), lambda b, q: (b, 0))
    v_spec = pl.BlockSpec((LANE, S), lambda b, q: (0, b))
    full = lambda a: pl.BlockSpec(a.shape, lambda b, q: (0,) * a.ndim)
    cmp_spec = pl.BlockSpec((1, LANE, LANE), lambda b, q: (b, 0, 0))
    qo = pl.BlockSpec((4, TQ, LANE), lambda b, q: (0, b * nq + q, 0))
    return pl.pallas_call(
        _nsa_kernel,
        grid=(B, nq),
        in_specs=[pl.BlockSpec((4, LANE, TQ), lambda b, q: (0, 0, b * nq + q)),
                  pl.BlockSpec((GATE_ROWS, TQ), lambda b, q: (0, b * nq + q)),
                  cmp_spec, cmp_spec,
                  pl.BlockSpec((NSA_HEADS, LANE, TQ), lambda b, q: (0, 0, q)),
                  k_spec, v_spec, k_spec, v_spec, full(near_a), full(ov)],
        out_specs=qo,
        out_shape=jax.ShapeDtypeStruct((4, T, LANE), BF16),
        scratch_shapes=[pltpu.VMEM((NSA_G, LANE, TQ), F32), pltpu.VMEM((NSA_G, S // SEL_BLOCK, TQ), F32),
                        pltpu.VMEM((NSA_R, LANE, TQ), F32), pltpu.VMEM((GATE_ROWS, TQ), F32)]
                       + _stream_scratch(NSA_HEADS),
        compiler_params=_cparams(("parallel", "arbitrary")),
        name="nsa",
    )(qa, gates_t, kcmp, vcmpt, bias_c, ks, vst, kw, vwt, near_a, ov)


def _diff_kernel(lq1_ref, lk1_ref, lq2_ref, lk2_ref, sub_ref, qb_ref, kb_ref, vbt_ref, dn_ref, o_ref, *scratch):
    n = DIFF_PAIRS_PER_STEP * 4
    st_refs, s_a, s_b, p_a, p_b = _stream_refs(scratch, n)
    qi = pl.program_id(1)
    chain_of_row = lax.shift_right_arithmetic(lax.broadcasted_iota(jnp.int32, (LANE, 1), 0), 5)
    lam = (jnp.exp(jnp.sum(lq1_ref[...] * lk1_ref[...], axis=-1, keepdims=True))
           - jnp.exp(jnp.sum(lq2_ref[...] * lk2_ref[...], axis=-1, keepdims=True)) + LAMBDA_INIT)
    has_prev = jnp.where(qi >= 1, 0.0, NEG)
    n_steps = DIFF_HEADS // 2 // DIFF_PAIRS_PER_STEP

    def score_fn(step):
        pairs = [step * DIFF_PAIRS_PER_STEP + c // 4 for c in range(n)]
        qs = [qb_ref[step * DIFF_PAIRS_PER_STEP + i].astype(F32) for i in range(DIFF_PAIRS_PER_STEP)]
        qms = [jnp.where(chain_of_row == c % 4, qs[c // 4], 0.0).astype(BF16) for c in range(n)]

        def scores(kt, near, c):
            s = _dot(kb_ref[pairs[c], pl.ds(pl.multiple_of(kt * TQ, TQ), TQ), :], qms[c])
            if near == 2:
                return s
            bias = dn_ref[2 * pairs[c] + (c % 4) // 2, near]
            return s + bias if near == 0 else s + bias + has_prev
        return scores

    def start_diag(step):
        scores = score_fn(step)
        for c in range(n):
            s_a[c][...] = scores(qi, 0, c)

    start_diag(0)

    def step_body(step, carry):
        def vtile(kt, c):
            hh = (c % 4) // 2
            return _with_ones(vbt_ref[step * DIFF_PAIRS_PER_STEP + c // 4, hh * 2 * DIFF_HD:(hh + 1) * 2 * DIFF_HD,
                                      pl.ds(pl.multiple_of(kt * TQ, TQ), TQ)])

        sts = _causal_stream(qi, score_fn(step), vtile, st_refs, s_a, s_b, p_a, p_b, diag_ready=True)
        start_diag(jnp.minimum(step + 1, n_steps - 1))
        for i in range(DIFF_PAIRS_PER_STEP):
            outs = []
            for hh in range(2):
                c = 4 * i + 2 * hh
                o = _flash_out(sts[c]) - lam * _flash_out(sts[c + 1])
                outs.append(o * lax.rsqrt(jnp.mean(o * o, axis=0, keepdims=True) + EPS))
            out = jnp.concatenate(outs, axis=0) * sub_ref[...] * (1.0 - LAMBDA_INIT)
            o_ref[step * DIFF_PAIRS_PER_STEP + i] = out.T.astype(BF16)
        return carry
    lax.fori_loop(0, n_steps, step_body, 0)


def _diff(lq1, lk1, lq2, lk2, sub, qb, kb, vbt, near_b, B, S):
    nq = S // TQ
    T = B * S
    full = lambda a: pl.BlockSpec(a.shape, lambda b, q: (0,) * a.ndim)
    k_spec = pl.BlockSpec((4, S, LANE), lambda b, q: (0, b, 0))
    v_spec = pl.BlockSpec((4, LANE, S), lambda b, q: (0, 0, b))
    qo = pl.BlockSpec((4, TQ, LANE), lambda b, q: (0, b * nq + q, 0))
    return pl.pallas_call(
        _diff_kernel,
        grid=(B, nq),
        in_specs=[full(lq1), full(lk1), full(lq2), full(lk2), full(sub),
                  pl.BlockSpec((4, LANE, TQ), lambda b, q: (0, 0, b * nq + q)), k_spec, v_spec, full(near_b)],
        out_specs=qo,
        out_shape=jax.ShapeDtypeStruct((4, T, LANE), BF16),
        scratch_shapes=_stream_scratch(DIFF_PAIRS_PER_STEP * 4),
        compiler_params=_cparams(("parallel", "arbitrary")),
        name="diff",
    )(lq1, lk1, lq2, lk2, sub, qb, kb, vbt, near_b)


def _outproj_kernel(x_ref, oa_ref, ob_ref, w_ref, g_ref, wr_ref, br_ref, xt_ref, route_ref, cnt_ref, carry_ref):
    @pl.when(pl.program_id(0) == 0)
    def _():
        carry_ref[...] = jnp.zeros(carry_ref.shape, F32)

    o = jnp.concatenate([oa_ref[r] for r in range(4)] + [ob_ref[r] for r in range(4)], axis=-1)
    h = x_ref[...] + _dot(o, w_ref[...])
    xt_ref[:, 0:D_MODEL] = h
    tn = _rmsnorm(h, g_ref[...]).astype(BF16)
    logits = _dot(tn, wr_ref[...]) + br_ref[...]
    lane = lax.broadcasted_iota(jnp.int32, (1, LANE), 1)
    lane_f = lane.astype(F32)
    is_grp = lane < MOE_GROUPS
    lg = jnp.where(is_grp, logits, NEG)
    mg = jnp.max(lg, axis=-1, keepdims=True)
    zg = jnp.sum(jnp.where(is_grp, jnp.exp(lg - mg), 0.0), axis=-1, keepdims=True)
    g_prob = 1.0 / zg
    g_idx = jnp.min(jnp.where(lg == mg, lane_f, 1e9), axis=-1, keepdims=True)
    lane_grp = jnp.where((lane >= MOE_GROUPS) & (lane < MOE_GROUPS + N_EXPERTS),
                         lax.shift_right_arithmetic(lane - MOE_GROUPS, 3), -1).astype(F32)
    le = jnp.where(lane_grp == g_idx, logits, NEG)
    m1 = jnp.max(le, axis=-1, keepdims=True)
    e1 = jnp.min(jnp.where(le == m1, lane_f, 1e9), axis=-1, keepdims=True)
    le2 = jnp.where(lane_f == e1, NEG, le)
    m2 = jnp.max(le2, axis=-1, keepdims=True)
    e2 = jnp.min(jnp.where(le2 == m2, lane_f, 1e9), axis=-1, keepdims=True)
    ratio = jnp.exp(m2 - m1)
    w1 = g_prob / (1.0 + ratio)
    w2 = w1 * ratio
    xt_ref[:, D_MODEL:D_MODEL + LANE] = jnp.where(lane_f == e1, w1, 0.0) + jnp.where(lane_f == e2, w2, 0.0)

    tm = h.shape[0]
    onehot = jnp.where(lane_f == g_idx, 1.0, 0.0)
    earlier = jnp.where(lax.broadcasted_iota(jnp.int32, (tm, tm), 0) > lax.broadcasted_iota(jnp.int32, (tm, tm), 1),
                        1.0, 0.0).astype(BF16)
    prefix = _dot(earlier, onehot.astype(BF16)) + carry_ref[...]
    rank = jnp.sum(onehot * prefix, axis=-1, keepdims=True)
    carry_ref[...] += jnp.sum(onehot, axis=0, keepdims=True)
    route_ref[...] = jnp.where(lane == 0, g_idx, jnp.where(lane == 1, rank, 0.0))
    cnt_ref[...] = jnp.broadcast_to(carry_ref[...], cnt_ref.shape)


def _out_proj(x2, oa, ob, w, g, wr, br):
    T = x2.shape[0]
    tm = TM_PROJ
    row = lambda i: (i, 0)
    full = lambda a: pl.BlockSpec(a.shape, lambda i: (0,) * a.ndim)
    o4 = pl.BlockSpec((4, tm, LANE), lambda i: (0, i, 0))
    return pl.pallas_call(
        _outproj_kernel,
        grid=(T // tm,),
        in_specs=[pl.BlockSpec((tm, D_MODEL), row), o4, o4, full(w), full(g), full(wr), full(br)],
        out_specs=[pl.BlockSpec((tm, XT_WIDTH), row), pl.BlockSpec((tm, LANE), row),
                   pl.BlockSpec((8, LANE), lambda i: (0, 0))],
        out_shape=[jax.ShapeDtypeStruct((T, XT_WIDTH), F32), jax.ShapeDtypeStruct((T, LANE), F32),
                   jax.ShapeDtypeStruct((8, LANE), F32)],
        scratch_shapes=[pltpu.VMEM((1, LANE), F32)],
        compiler_params=_cparams(("arbitrary",)),
        name="out_proj",
    )(x2, oa, ob, w, g, wr, br)


def _moe_kernel(tg_ref, ok_ref, idx_ref, idxn_ref, xt_hbm, g_ref, wg_ref, wu_ref, wd_ref, y_ref, xbuf, sem,
                xb_ref, cmb_ref, fence_sem):
    i = pl.program_id(0)
    slot = i % 2
    tm = xbuf.shape[1]

    def row_copy(index_ref, r, dst_slot):
        return pltpu.make_async_copy(xt_hbm.at[pl.ds(index_ref[0, 0, r], 1), :],
                                     xbuf.at[dst_slot, pl.ds(r, 1), :], sem.at[dst_slot])

    @pl.when(i == 0)
    def _():
        def body(r, c):
            row_copy(idx_ref, r, 0).start()
            return c
        lax.fori_loop(0, tm, body, 0)

    @pl.when((i == 0) | (ok_ref[jnp.maximum(i - 1, 0)] == 1))
    def _():
        pltpu.make_async_copy(xt_hbm.at[pl.ds(0, tm), :], xbuf.at[slot], sem.at[slot]).wait()

    @pl.when(ok_ref[i] == 0)
    def _():
        y_ref[...] = jnp.zeros(y_ref.shape, F32)

    @pl.when(ok_ref[i] == 1)
    def _():
        lane = lax.broadcasted_iota(jnp.int32, (1, LANE), 1)
        xb_ref[...] = _rmsnorm(xbuf[slot, :, 0:D_MODEL], g_ref[...]).astype(BF16)
        cmb_ref[...] = xbuf[slot, :, D_MODEL:D_MODEL + LANE]
        first_lane = MOE_GROUPS + tg_ref[i] * EPG
        per_expert = -(-tm // (EPG - 1))
        y = jnp.zeros((tm, D_MODEL), F32)
        issued = jnp.int32(0)
        for e in range(EPG):
            x = xb_ref[...]
            a = _dot(x, wg_ref[0, e])
            b = _dot(x, wu_ref[0, e])
            ce = jnp.sum(jnp.where(lane == first_lane + e + issued, cmb_ref[...], 0.0), axis=-1, keepdims=True)
            y = y + _dot(((a * jax.nn.sigmoid(a)) * b * ce).astype(BF16), wd_ref[0, e])
            for r in range(e * per_expert, min((e + 1) * per_expert, tm)):
                row_copy(idxn_ref, r, 1 - slot).start(priority=r % 2)
            if e in ANCHOR_AFTER:
                issued = pl.semaphore_read(fence_sem)
        for blk in range(ROW_SLABS):
            y_ref[pl.ds(blk, tm, stride=ROW_SLABS), :] = y[:, blk * LANE:(blk + 1) * LANE]


def _moe(tile_group, tile_ok, src3, xt, g, wg, wu, wd):
    n_tiles, _, tm = src3.shape
    last = n_tiles - 1
    grid_spec = pltpu.PrefetchScalarGridSpec(
        num_scalar_prefetch=2,
        grid=(n_tiles,),
        in_specs=[pl.BlockSpec((1, 1, tm), lambda i, tg, ok: (i, 0, 0), memory_space=pltpu.SMEM),
                  pl.BlockSpec((1, 1, tm), lambda i, tg, ok: (jnp.minimum(i + 1, last), 0, 0), memory_space=pltpu.SMEM),
                  pl.BlockSpec(memory_space=pl.ANY),
                  pl.BlockSpec((1, D_MODEL), lambda i, tg, ok: (0, 0)),
                  pl.BlockSpec((1, EPG, D_MODEL, EXPERT_FF), lambda i, tg, ok: (tg[i], 0, 0, 0)),
                  pl.BlockSpec((1, EPG, D_MODEL, EXPERT_FF), lambda i, tg, ok: (tg[i], 0, 0, 0)),
                  pl.BlockSpec((1, EPG, EXPERT_FF, D_MODEL), lambda i, tg, ok: (tg[i], 0, 0, 0))],
        out_specs=pl.BlockSpec((tm * ROW_SLABS, LANE), lambda i, tg, ok: (i, 0)),
        scratch_shapes=[pltpu.VMEM((2, tm, XT_WIDTH), F32), pltpu.SemaphoreType.DMA((2,)),
                        pltpu.VMEM((tm, D_MODEL), BF16), pltpu.VMEM((tm, LANE), F32), pltpu.SemaphoreType.REGULAR],
    )
    return pl.pallas_call(
        _moe_kernel,
        grid_spec=grid_spec,
        out_shape=jax.ShapeDtypeStruct((n_tiles * tm * ROW_SLABS, LANE), F32),
        compiler_params=pltpu.CompilerParams(dimension_semantics=("arbitrary",), vmem_limit_bytes=VMEM_LIMIT_MOE),
        name="moe",
    )(tile_group, tile_ok, src3, src3, xt, g, wg, wu, wd)


def _final_kernel(pos_ref, posn_ref, y_hbm, h_ref, gf_ref, o_ref, ybuf, sem):
    i = pl.program_id(0)
    slot = i % 2
    tm = h_ref.shape[0]

    def issue(index_ref, dst_slot):
        group = 8

        def body(j, c):
            for k in range(group):
                r = j * group + k
                src_row = pl.multiple_of(index_ref[0, 0, r], ROW_SLABS)
                pltpu.make_async_copy(y_hbm.at[pl.ds(src_row, ROW_SLABS), :],
                                      ybuf.at[dst_slot, pl.ds(r * ROW_SLABS, ROW_SLABS), :],
                                      sem.at[dst_slot]).start(priority=k % 2)
            return c
        lax.fori_loop(0, tm // group, body, 0)

    @pl.when(i == 0)
    def _():
        issue(pos_ref, 0)

    @pl.when(i + 1 < pl.num_programs(0))
    def _():
        issue(posn_ref, 1 - slot)

    pltpu.make_async_copy(ybuf.at[slot], ybuf.at[slot], sem.at[slot]).wait()
    y = jnp.concatenate([ybuf[slot, pl.ds(blk, tm, stride=ROW_SLABS), :] for blk in range(ROW_SLABS)], axis=-1)
    h = h_ref[...] + y
    o_ref[...] = h * lax.rsqrt(jnp.mean(h * h, axis=-1, keepdims=True) + EPS) * gf_ref[...]


def _final(pos3, y_sorted, h, gf):
    n_tiles, _, tm = pos3.shape
    last = n_tiles - 1
    row = lambda i: (i, 0)
    return pl.pallas_call(
        _final_kernel,
        grid=(n_tiles,),
        in_specs=[pl.BlockSpec((1, 1, tm), lambda i: (i, 0, 0), memory_space=pltpu.SMEM),
                  pl.BlockSpec((1, 1, tm), lambda i: (jnp.minimum(i + 1, last), 0, 0), memory_space=pltpu.SMEM),
                  pl.BlockSpec(memory_space=pl.ANY),
                  pl.BlockSpec((tm, D_MODEL), row),
                  pl.BlockSpec((1, D_MODEL), lambda i: (0, 0))],
        out_specs=pl.BlockSpec((tm, D_MODEL), row),
        out_shape=jax.ShapeDtypeStruct((n_tiles * tm, D_MODEL), F32),
        scratch_shapes=[pltpu.VMEM((2, tm * ROW_SLABS, LANE), F32), pltpu.SemaphoreType.DMA((2,))],
        compiler_params=_cparams(("arbitrary",)),
        name="final",
    )(pos3, pos3, y_sorted, h, gf)


def _qa_perm():
    new = np.arange(NSA_HEADS * NSA_HD)
    r, g, d = new // LANE, (new % LANE) // NSA_HD, new % NSA_HD
    return (g * NSA_R + r) * NSA_HD + d


def _block_diag2(w):
    z = jnp.zeros_like(w)
    return jnp.concatenate([jnp.concatenate([w, z], axis=-1), jnp.concatenate([z, w], axis=-1)], axis=-2)


def kernel(x, rel_bias, ln_mix, w_in, cmp_pos_k, cmp_pos_v, cmp_k_w1, cmp_k_w2, cmp_v_w1, cmp_v_w2,
           diff_lq1, diff_lk1, diff_lq2, diff_lk2, diff_subln, w_out, ln_ffn,
           router_group_w, router_group_b, router_expert_w, router_expert_b,
           exp_w_gate, exp_w_up, exp_w_down, ln_final):
    B, S, D = x.shape
    T = B * S
    assert D == D_MODEL and S % TQ == 0 and S >= WINDOW and T % TM_MOE == 0 and T % TM_FINAL == 0
    x2 = x.reshape(T, D)
    perm = _qa_perm()

    w = w_in[0]
    c_kc, c_vc, c_ks, c_vs, c_kw, c_vw, c_gt = 512, 640, 768, 896, 1024, 1152, 1280
    c_qb = c_gt + N_GATE
    c_kb, c_vb = c_qb + 512, c_qb + 1024
    col = lambda c, n=LANE: w[:, c:c + n]
    w_tok = jnp.concatenate([col(c_kc), col(c_vc), col(c_ks), col(c_kw), col(c_kb, 512)], axis=1).astype(BF16)
    w_feat = jnp.concatenate([w[:, perm], col(c_qb, 512), col(c_vs), col(c_vw), col(c_vb, 512), col(c_gt, N_GATE),
                              jnp.zeros((D, GATE_ROWS - N_GATE), F32)], axis=1).T.astype(BF16)
    qa, kc, vc, ks, kw, qb, kb, vst, vwt, vbt, gates_t = _in_proj(x2, ln_mix[0][None, :], w_tok, w_feat)

    w1k = _block_diag2(cmp_k_w1[0].reshape(CMP_LEN, NSA_HD, CMP_HIDDEN)).astype(BF16)
    w1v = _block_diag2(cmp_v_w1[0].reshape(CMP_LEN, NSA_HD, CMP_HIDDEN)).astype(BF16)
    w2k = _block_diag2(cmp_k_w2[0]).astype(BF16)
    w2v = _block_diag2(cmp_v_w2[0]).astype(BF16)
    posk = jnp.tile(cmp_pos_k[0], (1, NSA_G))
    posv = jnp.tile(cmp_pos_v[0], (1, NSA_G))
    kcmp, vcmpt = _compress(kc, vc, posk, posv, w1k, w1v, w2k, w2v, B, S)

    near, bias_c = _bias_tiles(rel_bias.T, S)

    n_sel = S // SEL_BLOCK
    nrow = S // CMP_STRIDE
    c_start = np.arange(nrow) * CMP_STRIDE
    s_start = np.arange(n_sel) * SEL_BLOCK
    ov = ((c_start[None, :] <= s_start[:, None] + SEL_BLOCK - 1)
          & (c_start[None, :] + CMP_LEN - 1 >= s_start[:, None])
          & (np.arange(nrow)[None, :] < nrow - 1)).astype(np.float32)
    o_a = _nsa(qa, gates_t, kcmp, vcmpt, bias_c, ks, vst, kw, vwt, near[:NSA_HEADS], jnp.asarray(ov, BF16), B, S)

    sub = jnp.tile(diff_subln[0], 2)[:, None]
    o_b = _diff(diff_lq1[0][None, :], diff_lk1[0][None, :], diff_lq2[0][None, :], diff_lk2[0][None, :],
                sub, qb, kb, vbt, near[NSA_HEADS:], B, S)

    w_o = jnp.concatenate([w_out[0][:512][perm], w_out[0][512:]], axis=0).astype(BF16)
    n_r = MOE_GROUPS + N_EXPERTS
    wr = jnp.concatenate([router_group_w[0], router_expert_w[0], jnp.zeros((D, LANE - n_r), F32)], axis=1).astype(BF16)
    br = jnp.concatenate([router_group_b[0], router_expert_b[0], jnp.zeros((LANE - n_r,), F32)])[None, :]
    xt, route, counts = _out_proj(x2, o_a, o_b, w_o, ln_ffn[0][None, :], wr, br)

    tm = TM_MOE
    n_tiles = T // tm + MOE_GROUPS
    cnt = counts[0, :MOE_GROUPS].astype(jnp.int32)
    ends = jnp.cumsum((cnt + tm - 1) // tm * tm)
    starts = ends - (cnt + tm - 1) // tm * tm
    pos = starts[route[:, 0].astype(jnp.int32)] + route[:, 1].astype(jnp.int32)
    order = jnp.argsort(pos).astype(jnp.int32)
    row = jnp.arange(n_tiles * tm, dtype=jnp.int32)
    row_group = jnp.minimum(jnp.searchsorted(ends, row, side="right"), MOE_GROUPS - 1)
    local = row - starts[row_group]
    compact = (jnp.cumsum(cnt) - cnt)[row_group] + local
    src = jnp.where(local < cnt[row_group], order[jnp.clip(compact, 0, T - 1)], 0)
    tile_start = jnp.arange(n_tiles, dtype=jnp.int32) * tm
    tile_group = jnp.minimum(jnp.searchsorted(ends, tile_start, side="right"), MOE_GROUPS - 1).astype(jnp.int32)
    tile_ok = (tile_start < ends[-1]).astype(jnp.int32)

    by_group = lambda a: a[0].astype(BF16).reshape((MOE_GROUPS, EPG) + a.shape[2:])
    y_sorted = _moe(tile_group, tile_ok, src.reshape(n_tiles, 1, tm), xt, ln_ffn[0][None, :],
                    by_group(exp_w_gate), by_group(exp_w_up), by_group(exp_w_down))
    out = _final((pos * ROW_SLABS).reshape(T // TM_FINAL, 1, TM_FINAL), y_sorted, xt, ln_final[None, :])
    return out.reshape(B, S, D)
```

```python
import math

import numpy as np
import jax
import jax.numpy as jnp
from jax import lax
from jax.experimental import pallas as pl
from jax.experimental.pallas import tpu as pltpu

F32 = jnp.float32
BF16 = jnp.bfloat16
NEG = -1e30
EPS = 1e-6
LOG2E = math.log2(math.e)

D_MODEL = 1024
LANE = 128
NSA_HEADS, NSA_G, NSA_R, NSA_HD = 8, 2, 4, 64
CMP_LEN, CMP_STRIDE, CMP_HIDDEN = 32, 16, 128
SEL_BLOCK, SEL_TOPN, SEL_FORCED_LOCAL, WINDOW = 64, 8, 2, 512
DIFF_HEADS, DIFF_HD = 8, 32
REL_BUCKETS, REL_MAX_EXACT, REL_MAX_DIST = 32, 16, 128
N_REL_HEADS = NSA_HEADS + DIFF_HEADS
MOE_GROUPS, EPG, N_EXPERTS, EXPERT_FF = 4, 8, 32, 256
LAMBDA_INIT = 0.8 - 0.6 * math.exp(-0.3 * 0)
N_GATE = NSA_HEADS * 3
GATE_ROWS = 32

TQ = 256
DIFF_PAIRS_PER_STEP = 4
TM_PROJ = 512
TM_MOE = 512
TM_FINAL = 512
ANCHOR_AFTER = (1, 3, 5, 6)
XT_WIDTH = D_MODEL + LANE
ROW_SLABS = D_MODEL // LANE
VMEM_LIMIT = 48 * 1024 * 1024
VMEM_LIMIT_MOE = 56 * 1024 * 1024


def _cparams(sem):
    return pltpu.CompilerParams(dimension_semantics=sem, vmem_limit_bytes=VMEM_LIMIT)


def _dot(a, b):
    return jnp.dot(a, b, preferred_element_type=F32)


def _rmsnorm(x, g):
    return x * lax.rsqrt(jnp.mean(x * x, axis=-1, keepdims=True) + EPS) * g


def _dot_nt(a, b):
    return lax.dot_general(a, b, (((1,), (1,)), ((), ())), preferred_element_type=F32)


def _bucket_thresholds():
    n = np.arange(0, REL_MAX_DIST + 1)
    nf = np.maximum(n, 1).astype(np.float32)
    large = REL_MAX_EXACT + (np.log(nf / np.float32(REL_MAX_EXACT)) / np.float32(math.log(REL_MAX_DIST / REL_MAX_EXACT))
                             * np.float32(REL_BUCKETS - REL_MAX_EXACT)).astype(np.int32)
    large = np.minimum(large, REL_BUCKETS - 1)
    bucket = np.where(n < REL_MAX_EXACT, n, large)
    return [int(np.argmax(bucket >= b)) for b in range(REL_BUCKETS)]


_THR = _bucket_thresholds()


def _inproj_kernel(x_ref, g_ref, w_ref, wt_ref, qa_ref, kc_ref, vc_ref, ks_ref, kw_ref, qb_ref, kb_ref,
                   vst_ref, vwt_ref, vbt_ref, gt_ref):
    x = x_ref[...]
    xn = (x * lax.rsqrt(jnp.mean(x * x, axis=-1, keepdims=True) + EPS) * g_ref[...]).astype(BF16)
    a = _dot(xn, w_ref[:, 0:512])
    kc_ref[...] = a[:, 0:128]
    vc_ref[...] = a[:, 128:256]
    ks_ref[...] = a[:, 256:384].astype(BF16)
    kw_ref[...] = a[:, 384:512].astype(BF16)
    a = _dot(xn, w_ref[:, 512:1024])
    for r in range(4):
        kb_ref[r] = a[:, r * LANE:(r + 1) * LANE].astype(BF16)

    ft = _dot_nt(wt_ref[...], xn)

    def feat(row0, rows=LANE):
        return ft[row0:row0 + rows, :]
    for r in range(4):
        qa_ref[r] = (feat(r * LANE) * (NSA_HD ** -0.5 * LOG2E)).astype(BF16)
        qb_ref[r] = (feat(512 + r * LANE) * (DIFF_HD ** -0.5 * LOG2E)).astype(BF16)
        vbt_ref[r] = feat(1280 + r * LANE).astype(BF16)
    vst_ref[...] = feat(1024).astype(BF16)
    vwt_ref[...] = feat(1152).astype(BF16)
    gt_ref[...] = feat(1792, GATE_ROWS)


def _in_proj(x2, g, w, wt):
    T = x2.shape[0]
    tm = TM_PROJ
    row = lambda i: (i, 0)
    o128b = jax.ShapeDtypeStruct((T, LANE), BF16)
    o128f = jax.ShapeDtypeStruct((T, LANE), F32)
    o4 = jax.ShapeDtypeStruct((4, T, LANE), BF16)
    ot = jax.ShapeDtypeStruct((LANE, T), BF16)
    o4t = jax.ShapeDtypeStruct((4, LANE, T), BF16)
    s128 = pl.BlockSpec((tm, LANE), row)
    s4 = pl.BlockSpec((4, tm, LANE), lambda i: (0, i, 0))
    st = pl.BlockSpec((LANE, tm), lambda i: (0, i))
    s4t = pl.BlockSpec((4, LANE, tm), lambda i: (0, 0, i))
    return pl.pallas_call(
        _inproj_kernel,
        grid=(T // tm,),
        in_specs=[pl.BlockSpec((tm, D_MODEL), row),
                  pl.BlockSpec((1, D_MODEL), lambda i: (0, 0)),
                  pl.BlockSpec(w.shape, lambda i: (0, 0)),
                  pl.BlockSpec(wt.shape, lambda i: (0, 0))],
        out_specs=[s4t, s128, s128, s128, s128, s4t, s4, st, st, s4t, pl.BlockSpec((GATE_ROWS, tm), lambda i: (0, i))],
        out_shape=[o4t, o128f, o128f, o128b, o128b, o4t, o4, ot, ot, o4t, jax.ShapeDtypeStruct((GATE_ROWS, T), F32)],
        compiler_params=_cparams(("parallel",)),
        name="in_proj",
    )(x2, g, w, wt)


def _gelu_tanh(x):
    return 0.5 * x * (1.0 + jnp.tanh(math.sqrt(2.0 / math.pi) * (x + 0.044715 * (x * x * x))))


def _compress_kernel(kc_ref, vc_ref, posk_ref, posv_ref, w1k_ref, w1v_ref, w2k_ref, w2v_ref, ko_ref, vo_ref):
    nrow = kc_ref.shape[0] // CMP_STRIDE
    rid = lax.broadcasted_iota(jnp.int32, (nrow, 1), 0)
    cid = lax.broadcasted_iota(jnp.int32, (1, nrow), 1)
    for src, pos, w1, w2, out, transposed in ((kc_ref, posk_ref, w1k_ref, w2k_ref, ko_ref, False),
                                              (vc_ref, posv_ref, w1v_ref, w2v_ref, vo_ref, True)):
        hid_a = jnp.zeros((nrow, 2 * CMP_HIDDEN), F32)
        hid_b = jnp.zeros((nrow, 2 * CMP_HIDDEN), F32)
        for m in range(CMP_STRIDE):
            y = src[pl.ds(m, nrow, stride=CMP_STRIDE), :]
            hid_a = hid_a + _dot((y + pos[m:m + 1, :]).astype(BF16), w1[m])
            hid_b = hid_b + _dot((y + pos[CMP_STRIDE + m:CMP_STRIDE + m + 1, :]).astype(BF16), w1[CMP_STRIDE + m])
        hid = hid_a + pltpu.roll(hid_b, nrow - 1, 0)
        o = _dot(_gelu_tanh(hid).astype(BF16), w2[...])
        if transposed:
            out[0] = jnp.where(cid < nrow - 1, o.T, 0.0).astype(BF16)
        else:
            out[0] = jnp.where(rid < nrow - 1, o, 0.0).astype(BF16)


def _compress(kc, vc, posk, posv, w1k, w1v, w2k, w2v, B, S):
    nrow = S // CMP_STRIDE
    assert nrow == LANE
    full = lambda a: pl.BlockSpec(a.shape, lambda b: (0,) * a.ndim)
    src = pl.BlockSpec((S, LANE), lambda b: (b, 0))
    osp = pl.BlockSpec((1, nrow, LANE), lambda b: (b, 0, 0))
    osh = jax.ShapeDtypeStruct((B, nrow, LANE), BF16)
    return pl.pallas_call(
        _compress_kernel,
        grid=(B,),
        in_specs=[src, src, full(posk), full(posv), full(w1k), full(w1v), full(w2k), full(w2v)],
        out_specs=[osp, osp],
        out_shape=[osh, osh],
        compiler_params=_cparams(("parallel",)),
        name="compress",
    )(kc, vc, posk, posv, w1k, w1v, w2k, w2v)


def _bias_from_dist(dist, tab_ref, h):
    val = jnp.full(dist.shape, tab_ref[h, 0], F32)
    for b in range(1, REL_BUCKETS):
        val = jnp.where(dist >= _THR[b], tab_ref[h, b], val)
    return val


def _bias_near_kernel(tab_ref, out_ref):
    h = pl.program_id(0)
    j = lax.broadcasted_iota(jnp.int32, (TQ, TQ), 0)
    i = lax.broadcasted_iota(jnp.int32, (TQ, TQ), 1)
    for d in range(2):
        bias = (_bias_from_dist(i - j + d * TQ, tab_ref, h) - tab_ref[h, REL_BUCKETS - 1]) * LOG2E
        out_ref[0, d] = jnp.where(i - j + d * TQ >= 0, bias, NEG)


def _bias_cmp_kernel(tab_ref, out_ref):
    h = pl.program_id(0)
    c = lax.broadcasted_iota(jnp.int32, (LANE, TQ), 0)
    t = pl.program_id(1) * TQ + lax.broadcasted_iota(jnp.int32, (LANE, TQ), 1)
    out_ref[0] = _bias_from_dist(t - (c * CMP_STRIDE + CMP_LEN - 1), tab_ref, h) * LOG2E


def _bias_tiles(tab_t, S):
    smem = pl.BlockSpec(memory_space=pltpu.SMEM)
    near = pl.pallas_call(
        _bias_near_kernel,
        grid=(N_REL_HEADS,),
        in_specs=[smem],
        out_specs=pl.BlockSpec((1, 2, TQ, TQ), lambda h: (h, 0, 0, 0)),
        out_shape=jax.ShapeDtypeStruct((N_REL_HEADS, 2, TQ, TQ), F32),
        compiler_params=_cparams(("parallel",)),
        name="bias_near",
    )(tab_t)
    cmp_bias = pl.pallas_call(
        _bias_cmp_kernel,
        grid=(NSA_HEADS, S // TQ),
        in_specs=[smem],
        out_specs=pl.BlockSpec((1, LANE, TQ), lambda h, q: (h, 0, q)),
        out_shape=jax.ShapeDtypeStruct((NSA_HEADS, LANE, S), F32),
        compiler_params=_cparams(("parallel", "parallel")),
        name="bias_cmp",
    )(tab_t)
    return near, cmp_bias


ACC_ROWS = NSA_HD + 16


def _with_ones(vt):
    return jnp.concatenate([vt, jnp.ones((ACC_ROWS - vt.shape[0], vt.shape[1]), BF16)], axis=0)


def _probs(s, m):
    return jnp.exp2((s - m).astype(BF16))


def _flash_first(ss, vts):
    ms = [jnp.max(s, axis=0, keepdims=True) for s in ss]
    ps = [_probs(s, m) for s, m in zip(ss, ms)]
    return tuple((m, _dot(vt, p)) for m, p, vt in zip(ms, ps, vts))


def _flash_update(ss, vts, sts):
    ms = [jnp.maximum(st[0], jnp.max(s, axis=0, keepdims=True)) for s, st in zip(ss, sts)]
    alphas = [jnp.exp2(st[0] - m) for m, st in zip(ms, sts)]
    ps = [_probs(s, m) for s, m in zip(ss, ms)]
    return tuple((m, a * st[1] + _dot(vt, p)) for m, a, st, p, vt in zip(ms, alphas, sts, ps, vts))


def _flash_out(st):
    acc = st[1]
    return acc[0:NSA_HD, :] * (1.0 / acc[NSA_HD:NSA_HD + 1, :])


def _stream_scratch(n):
    return ([pltpu.VMEM((8, TQ), F32), pltpu.VMEM((ACC_ROWS, TQ), F32)] * n
            + [pltpu.VMEM((TQ, TQ), F32)] * (2 * n) + [pltpu.VMEM((TQ, TQ), BF16)] * (2 * n))


def _stream_refs(refs, n):
    st_refs = tuple((refs[2 * c], refs[2 * c + 1]) for c in range(n))
    s, p = refs[2 * n:4 * n], refs[4 * n:6 * n]
    return st_refs, tuple(s[:n]), tuple(s[n:]), tuple(p[:n]), tuple(p[n:])


def _causal_stream(qi, scores, vtile, st_refs, s_a, s_b, p_a, p_b, diag_ready=False):
    n_far = jnp.maximum(qi - 1, 0)
    top = qi - 2

    chains = range(len(st_refs))

    def write(s_refs, kt, near=2):
        for c in chains:
            s_refs[c][...] = scores(jnp.maximum(kt, 0), near, c)

    def pending(p_refs, kt):
        return [_dot(vtile(jnp.maximum(kt, 0), c), p_refs[c][...]) for c in chains]

    def step(s_cur, p_cur, s_next, kt_next, p_prev, kt_prev, anchor=None):
        rescale = []
        for c in chains:
            nxt = scores(jnp.maximum(kt_next, 0), 2, c)
            s_next[c][...] = nxt
            pv = _dot(vtile(jnp.maximum(kt_prev, 0), c), p_prev[c][...])
            st = st_refs[c][0]
            s = s_cur[c][...]
            m_old = st[0:1, :]
            m = jnp.maximum(m_old, jnp.max(s, axis=0, keepdims=True))
            if anchor is not None:
                m = m + anchor
            anchor = nxt[0:1, :] * 0.0
            p_cur[c][...] = _probs(s, m)
            st[0:1, :] = m
            rescale.append((jnp.exp2(m_old - m), pv))
        for c in chains:
            acc = st_refs[c][1]
            acc[...] = rescale[c][0] * (acc[...] + rescale[c][1])
        return anchor

    if not diag_ready:
        write(s_a, qi, 0)
    for c in chains:
        s_b[c][...] = scores(jnp.maximum(qi - 1, 0), 1, c)
        st, acc = st_refs[c]
        s = s_a[c][...]
        m = jnp.max(s, axis=0, keepdims=True)
        p_a[c][...] = _probs(s, m)
        st[0:1, :] = m
        acc[...] = jnp.zeros(acc.shape, F32)
    step(s_b, p_b, s_a, top, p_a, qi)

    def pair(j, c):
        kt = top - 2 * j
        anchor = step(s_a, p_a, s_b, kt - 1, p_b, kt + 1)
        step(s_b, p_b, s_a, kt - 2, p_a, kt, anchor)
        return c
    lax.fori_loop(0, n_far // 2, pair, 0)

    @pl.when(n_far % 2 == 1)
    def _():
        for c, pv in zip(chains, pending(p_b, 1)):
            st, acc = st_refs[c]
            s = s_a[c][...]
            m_old = st[0:1, :]
            m = jnp.maximum(m_old, jnp.max(s, axis=0, keepdims=True))
            st[0:1, :] = m
            acc[...] = jnp.exp2(m_old - m) * (acc[...] + pv) + _dot(vtile(0, c), _probs(s, m))

    @pl.when(n_far % 2 == 0)
    def _():
        last = jnp.where(n_far == 0, qi - 1, 0)
        for c, pv in zip(chains, pending(p_b, last)):
            acc = st_refs[c][1]
            acc[...] = acc[...] + pv
    return tuple((st[0:1, :], acc[...]) for st, acc in st_refs)


def _ktile(ref, kt):
    return ref[pl.ds(pl.multiple_of(kt * TQ, TQ), TQ), :]


def _nsa_kernel(qa_ref, gt_ref, kcmp_ref, vcmpt_ref, bc_ref, ks_ref, vst_ref, kw_ref, vwt_ref, dn_ref,
                ov_ref, o_ref, psum_ref, sel_ref, oacc_ref, sig_ref, *scratch):
    st_refs, s_a, s_b, p_a, p_b = _stream_refs(scratch, NSA_HEADS)
    qi = pl.program_id(1)
    t0 = qi * TQ
    sub_grp = lax.shift_right_arithmetic(lax.broadcasted_iota(jnp.int32, (LANE, 1), 0), 6)
    sig_ref[...] = jax.nn.sigmoid(gt_ref[...])

    def gate_row(c):
        return sig_ref[pl.ds(c, 1), :]

    def masked_q(r, g):
        return jnp.where(sub_grp == g, qa_ref[r].astype(F32), 0.0).astype(BF16)

    n_cmp = kcmp_ref.shape[1] - 1
    crow = lax.broadcasted_iota(jnp.int32, (LANE, 1), 0)
    cmp_end = jnp.where(crow < n_cmp, crow * CMP_STRIDE + CMP_LEN - 1, 1 << 30)
    mask_c = (t0 + lax.broadcasted_iota(jnp.int32, (1, TQ), 1)) >= cmp_end

    heads = [(r, g) for r in range(NSA_R) for g in range(NSA_G)]
    scores_c = [_dot(kcmp_ref[0], masked_q(r, g)) for r, g in heads]
    probs_c = []
    for (r, g), s in zip(heads, scores_c):
        s = jnp.where(mask_c, s + bc_ref[g * NSA_R + r], NEG)
        p = jnp.where(mask_c, jnp.exp2(s - jnp.max(s, axis=0, keepdims=True)), 0.0)
        l = jnp.sum(p, axis=0, keepdims=True)
        probs_c.append(p * jnp.where(l > 0.0, 1.0 / l, 0.0))
    for g in range(NSA_G):
        psum_ref[g] = sum(p for (r, gg), p in zip(heads, probs_c) if gg == g)
    outs_c = [_dot(vcmpt_ref[0], p.astype(BF16)) for p in probs_c]
    for r in range(NSA_R):
        o0, o1 = outs_c[r * NSA_G], outs_c[r * NSA_G + 1]
        oacc_ref[r] = jnp.where(sub_grp == 0, gate_row(r * 3) * o0, gate_row((NSA_R + r) * 3) * o1)

    has_prev = jnp.where(qi >= 1, 0.0, NEG)
    has_wfar = jnp.where(qi >= WINDOW // TQ, 0.0, NEG)
    kt_prev = jnp.maximum(qi - 1, 0)
    kt_wfar = jnp.maximum(qi - WINDOW // TQ, 0)
    wfar_mask = lax.broadcasted_iota(jnp.int32, (TQ, TQ), 0) > lax.broadcasted_iota(jnp.int32, (TQ, TQ), 1)

    def attend_all_heads():
        chains = [(r, g) for r in range(NSA_R) for g in range(NSA_G)]
        qms = [masked_q(r, g) for r, g in chains]
        hids = [g * NSA_R + r for r, g in chains]

        per_tile = TQ // SEL_BLOCK
        key_blk = lax.shift_right_arithmetic(lax.broadcasted_iota(jnp.int32, (TQ, LANE), 0), 6)
        lane_id = lax.broadcasted_iota(jnp.int32, (TQ, LANE), 1)

        def keys_with_block_onehot(kt, g):
            other = (1 - g) * NSA_HD
            onehot = jnp.where(lane_id - other == key_blk, 1.0, 0.0).astype(BF16)
            in_other = (lane_id >= other) & (lane_id < other + NSA_HD)
            return jnp.where(in_other, onehot, _ktile(ks_ref, kt))

        def query_with_mask_rows(kt, c):
            r, g = chains[c]
            own = qa_ref[r, g * NSA_HD:(g + 1) * NSA_HD, :]
            pair = sel_ref[g, pl.ds(pl.multiple_of((kt // 2) * 2 * per_tile, 2 * per_tile), 2 * per_tile), :]
            rows = jnp.where(kt % 2 == 0, pair[0:per_tile], pair[per_tile:2 * per_tile]).astype(BF16)
            other = jnp.concatenate([rows, jnp.zeros((NSA_HD - per_tile, TQ), BF16)], axis=0)
            return jnp.concatenate([own, other] if g == 0 else [other, own], axis=0)

        def near_bias(s, near, c):
            if near == 0:
                return s + dn_ref[hids[c], 0]
            if near == 1:
                return s + dn_ref[hids[c], 1] + has_prev
            return s

        def group_values(ref, kt, c):
            g = chains[c][1]
            return _with_ones(ref[g * NSA_HD:(g + 1) * NSA_HD, pl.ds(pl.multiple_of(kt * TQ, TQ), TQ)])

        def finish(sts, branch):
            for c, (r, g) in enumerate(chains):
                oacc_ref[r, g * NSA_HD:(g + 1) * NSA_HD, :] += gate_row(hids[c] * 3 + branch) * _flash_out(sts[c])

        def slc_scores(kt, near, c):
            return near_bias(_dot(keys_with_block_onehot(kt, chains[c][1]), query_with_mask_rows(kt, c)), near, c)

        def slc_vtile(kt, c):
            return group_values(vst_ref, kt, c)

        win_tiles = ((qi, 0), (kt_prev, 1), (kt_wfar, 2))

        def win_scores(c):
            out = []
            for kt, near in win_tiles:
                s = near_bias(_dot(_ktile(kw_ref, kt), qms[c]), near, c)
                out.append(jnp.where(wfar_mask, s + has_wfar, NEG) if near == 2 else s)
            return out

        sts = []
        ahead = win_scores(0)
        for c in range(len(chains)):
            tiles, ahead = ahead, (win_scores(c + 1) if c + 1 < len(chains) else None)
            st = _flash_first([tiles[0]], [group_values(vwt_ref, qi, c)])
            st = _flash_update([tiles[1]], [group_values(vwt_ref, kt_prev, c)], st)
            sts.append(_flash_update([tiles[2]], [group_values(vwt_ref, kt_wfar, c)], st)[0])
        finish(sts, 2)

        n_sel = ov_ref.shape[0]
        jj = lax.broadcasted_iota(jnp.int32, (n_sel, TQ), 0)
        cur = lax.shift_right_arithmetic(t0 + lax.broadcasted_iota(jnp.int32, (n_sel, TQ), 1), 6)
        valid = jj <= cur
        forced = (jj == 0) | (cur - jj < SEL_FORCED_LOCAL)
        for g in range(NSA_G):
            ps = psum_ref[g]
            hi = ps.astype(BF16)
            rem = ps - hi.astype(F32)
            mid = rem.astype(BF16)
            lo = (rem - mid.astype(F32)).astype(BF16)
            ov = ov_ref[...]
            imp = _dot(ov, hi) + _dot(ov, mid) + _dot(ov, lo)
            score = jnp.where(valid, jnp.where(forced, 1e9, imp), -1e9)
            cnt = jnp.zeros((n_sel, TQ), F32)
            for j2 in range(n_sel):
                row = score[j2:j2 + 1, :]
                tie = jnp.where(jj > j2, 1.0, 0.0)
                cnt = cnt + jnp.where(row > score, 1.0, jnp.where(row == score, tie, 0.0))
            sel_ref[g] = jnp.where(cnt < float(min(SEL_TOPN, n_sel)), jnp.where(score > -1e8, 0.0, NEG), NEG)

        finish(_causal_stream(qi, slc_scores, slc_vtile, st_refs, s_a, s_b, p_a, p_b), 1)
    attend_all_heads()

    for r in range(NSA_R):
        o_ref[r] = oacc_ref[r].T.astype(BF16)


def _nsa(qa, gates_t, kcmp, vcmpt, bias_c, ks, vst, kw, vwt, near_a, ov, B, S):
    nq = S // TQ
    T = B * S
    k_spec = pl.BlockSpec((S, LANE), lambda b, q: (b, 0))
    v_spec = pl.BlockSpec((LANE, S), lambda b, q: (0, b))
    full = lambda a: pl.BlockSpec(a.shape, lambda b, q: (0,) * a.ndim)
    cmp_spec = pl.BlockSpec((1, LANE, LANE), lambda b, q: (b, 0, 0))
    qo = pl.BlockSpec((4, TQ, LANE), lambda b, q: (0, b * nq + q, 0))
    return pl.pallas_call(
        _nsa_kernel,
        grid=(B, nq),
        in_specs=[pl.BlockSpec((4, LANE, TQ), lambda b, q: (0, 0, b * nq + q)),
                  pl.BlockSpec((GATE_ROWS, TQ), lambda b, q: (0, b * nq + q)),
                  cmp_spec, cmp_spec,
                  pl.BlockSpec((NSA_HEADS, LANE, TQ), lambda b, q: (0, 0, q)),
                  k_spec, v_spec, k_spec, v_spec, full(near_a), full(ov)],
        out_specs=qo,
        out_shape=jax.ShapeDtypeStruct((4, T, LANE), BF16),
        scratch_shapes=[pltpu.VMEM((NSA_G, LANE, TQ), F32), pltpu.VMEM((NSA_G, S // SEL_BLOCK, TQ), F32),
                        pltpu.VMEM((NSA_R, LANE, TQ), F32), pltpu.VMEM((GATE_ROWS, TQ), F32)]
                       + _stream_scratch(NSA_HEADS),
        compiler_params=_cparams(("parallel", "arbitrary")),
        name="nsa",
    )(qa, gates_t, kcmp, vcmpt, bias_c, ks, vst, kw, vwt, near_a, ov)


def _diff_kernel(lq1_ref, lk1_ref, lq2_ref, lk2_ref, sub_ref, qb_ref, kb_ref, vbt_ref, dn_ref, o_ref, *scratch):
    n = DIFF_PAIRS_PER_STEP * 4
    st_refs, s_a, s_b, p_a, p_b = _stream_refs(scratch, n)
    qi = pl.program_id(1)
    chain_of_row = lax.shift_right_arithmetic(lax.broadcasted_iota(jnp.int32, (LANE, 1), 0), 5)
    lam = (jnp.exp(jnp.sum(lq1_ref[...] * lk1_ref[...], axis=-1, keepdims=True))
           - jnp.exp(jnp.sum(lq2_ref[...] * lk2_ref[...], axis=-1, keepdims=True)) + LAMBDA_INIT)
    has_prev = jnp.where(qi >= 1, 0.0, NEG)
    n_steps = DIFF_HEADS // 2 // DIFF_PAIRS_PER_STEP

    def score_fn(step):
        pairs = [step * DIFF_PAIRS_PER_STEP + c // 4 for c in range(n)]
        qs = [qb_ref[step * DIFF_PAIRS_PER_STEP + i].astype(F32) for i in range(DIFF_PAIRS_PER_STEP)]
        qms = [jnp.where(chain_of_row == c % 4, qs[c // 4], 0.0).astype(BF16) for c in range(n)]

        def scores(kt, near, c):
            s = _dot(kb_ref[pairs[c], pl.ds(pl.multiple_of(kt * TQ, TQ), TQ), :], qms[c])
            if near == 2:
                return s
            bias = dn_ref[2 * pairs[c] + (c % 4) // 2, near]
            return s + bias if near == 0 else s + bias + has_prev
        return scores

    def start_diag(step):
        scores = score_fn(step)
        for c in range(n):
            s_a[c][...] = scores(qi, 0, c)

    start_diag(0)

    def step_body(step, carry):
        def vtile(kt, c):
            hh = (c % 4) // 2
            return _with_ones(vbt_ref[step * DIFF_PAIRS_PER_STEP + c // 4, hh * 2 * DIFF_HD:(hh + 1) * 2 * DIFF_HD,
                                      pl.ds(pl.multiple_of(kt * TQ, TQ), TQ)])

        sts = _causal_stream(qi, score_fn(step), vtile, st_refs, s_a, s_b, p_a, p_b, diag_ready=True)
        start_diag(jnp.minimum(step + 1, n_steps - 1))
        for i in range(DIFF_PAIRS_PER_STEP):
            outs = []
            for hh in range(2):
                c = 4 * i + 2 * hh
                o = _flash_out(sts[c]) - lam * _flash_out(sts[c + 1])
                outs.append(o * lax.rsqrt(jnp.mean(o * o, axis=0, keepdims=True) + EPS))
            out = jnp.concatenate(outs, axis=0) * sub_ref[...] * (1.0 - LAMBDA_INIT)
            o_ref[step * DIFF_PAIRS_PER_STEP + i] = out.T.astype(BF16)
        return carry
    lax.fori_loop(0, n_steps, step_body, 0)


def _diff(lq1, lk1, lq2, lk2, sub, qb, kb, vbt, near_b, B, S):
    nq = S // TQ
    T = B * S
    full = lambda a: pl.BlockSpec(a.shape, lambda b, q: (0,) * a.ndim)
    k_spec = pl.BlockSpec((4, S, LANE), lambda b, q: (0, b, 0))
    v_spec = pl.BlockSpec((4, LANE, S), lambda b, q: (0, 0, b))
    qo = pl.BlockSpec((4, TQ, LANE), lambda b, q: (0, b * nq + q, 0))
    return pl.pallas_call(
        _diff_kernel,
        grid=(B, nq),
        in_specs=[full(lq1), full(lk1), full(lq2), full(lk2), full(sub),
                  pl.BlockSpec((4, LANE, TQ), lambda b, q: (0, 0, b * nq + q)), k_spec, v_spec, full(near_b)],
        out_specs=qo,
        out_shape=jax.ShapeDtypeStruct((4, T, LANE), BF16),
        scratch_shapes=_stream_scratch(DIFF_PAIRS_PER_STEP * 4),
        compiler_params=_cparams(("parallel", "arbitrary")),
        name="diff",
    )(lq1, lk1, lq2, lk2, sub, qb, kb, vbt, near_b)


def _outproj_kernel(x_ref, oa_ref, ob_ref, w_ref, g_ref, wr_ref, br_ref, xt_ref, route_ref, cnt_ref, carry_ref):
    @pl.when(pl.program_id(0) == 0)
    def _():
        carry_ref[...] = jnp.zeros(carry_ref.shape, F32)

    o = jnp.concatenate([oa_ref[r] for r in range(4)] + [ob_ref[r] for r in range(4)], axis=-1)
    h = x_ref[...] + _dot(o, w_ref[...])
    xt_ref[:, 0:D_MODEL] = h
    tn = _rmsnorm(h, g_ref[...]).astype(BF16)
    logits = _dot(tn, wr_ref[...]) + br_ref[...]
    lane = lax.broadcasted_iota(jnp.int32, (1, LANE), 1)
    lane_f = lane.astype(F32)
    is_grp = lane < MOE_GROUPS
    lg = jnp.where(is_grp, logits, NEG)
    mg = jnp.max(lg, axis=-1, keepdims=True)
    zg = jnp.sum(jnp.where(is_grp, jnp.exp(lg - mg), 0.0), axis=-1, keepdims=True)
    g_prob = 1.0 / zg
    g_idx = jnp.min(jnp.where(lg == mg, lane_f, 1e9), axis=-1, keepdims=True)
    lane_grp = jnp.where((lane >= MOE_GROUPS) & (lane < MOE_GROUPS + N_EXPERTS),
                         lax.shift_right_arithmetic(lane - MOE_GROUPS, 3), -1).astype(F32)
    le = jnp.where(lane_grp == g_idx, logits, NEG)
    m1 = jnp.max(le, axis=-1, keepdims=True)
    e1 = jnp.min(jnp.where(le == m1, lane_f, 1e9), axis=-1, keepdims=True)
    le2 = jnp.where(lane_f == e1, NEG, le)
    m2 = jnp.max(le2, axis=-1, keepdims=True)
    e2 = jnp.min(jnp.where(le2 == m2, lane_f, 1e9), axis=-1, keepdims=True)
    ratio = jnp.exp(m2 - m1)
    w1 = g_prob / (1.0 + ratio)
    w2 = w1 * ratio
    xt_ref[:, D_MODEL:D_MODEL + LANE] = jnp.where(lane_f == e1, w1, 0.0) + jnp.where(lane_f == e2, w2, 0.0)

    tm = h.shape[0]
    onehot = jnp.where(lane_f == g_idx, 1.0, 0.0)
    earlier = jnp.where(lax.broadcasted_iota(jnp.int32, (tm, tm), 0) > lax.broadcasted_iota(jnp.int32, (tm, tm), 1),
                        1.0, 0.0).astype(BF16)
    prefix = _dot(earlier, onehot.astype(BF16)) + carry_ref[...]
    rank = jnp.sum(onehot * prefix, axis=-1, keepdims=True)
    carry_ref[...] += jnp.sum(onehot, axis=0, keepdims=True)
    route_ref[...] = jnp.where(lane == 0, g_idx, jnp.where(lane == 1, rank, 0.0))
    cnt_ref[...] = jnp.broadcast_to(carry_ref[...], cnt_ref.shape)


def _out_proj(x2, oa, ob, w, g, wr, br):
    T = x2.shape[0]
    tm = TM_PROJ
    row = lambda i: (i, 0)
    full = lambda a: pl.BlockSpec(a.shape, lambda i: (0,) * a.ndim)
    o4 = pl.BlockSpec((4, tm, LANE), lambda i: (0, i, 0))
    return pl.pallas_call(
        _outproj_kernel,
        grid=(T // tm,),
        in_specs=[pl.BlockSpec((tm, D_MODEL), row), o4, o4, full(w), full(g), full(wr), full(br)],
        out_specs=[pl.BlockSpec((tm, XT_WIDTH), row), pl.BlockSpec((tm, LANE), row),
                   pl.BlockSpec((8, LANE), lambda i: (0, 0))],
        out_shape=[jax.ShapeDtypeStruct((T, XT_WIDTH), F32), jax.ShapeDtypeStruct((T, LANE), F32),
                   jax.ShapeDtypeStruct((8, LANE), F32)],
        scratch_shapes=[pltpu.VMEM((1, LANE), F32)],
        compiler_params=_cparams(("arbitrary",)),
        name="out_proj",
    )(x2, oa, ob, w, g, wr, br)


def _moe_kernel(tg_ref, ok_ref, idx_ref, idxn_ref, xt_hbm, g_ref, wg_ref, wu_ref, wd_ref, y_ref, xbuf, sem,
                xb_ref, cmb_ref, fence_sem):
    i = pl.program_id(0)
    slot = i % 2
    tm = xbuf.shape[1]

    def row_copy(index_ref, r, dst_slot):
        return pltpu.make_async_copy(xt_hbm.at[pl.ds(index_ref[0, 0, r], 1), :],
                                     xbuf.at[dst_slot, pl.ds(r, 1), :], sem.at[dst_slot])

    @pl.when(i == 0)
    def _():
        def body(r, c):
            row_copy(idx_ref, r, 0).start()
            return c
        lax.fori_loop(0, tm, body, 0)

    @pl.when((i == 0) | (ok_ref[jnp.maximum(i - 1, 0)] == 1))
    def _():
        pltpu.make_async_copy(xt_hbm.at[pl.ds(0, tm), :], xbuf.at[slot], sem.at[slot]).wait()

    @pl.when(ok_ref[i] == 0)
    def _():
        y_ref[...] = jnp.zeros(y_ref.shape, F32)

    @pl.when(ok_ref[i] == 1)
    def _():
        lane = lax.broadcasted_iota(jnp.int32, (1, LANE), 1)
        xb_ref[...] = _rmsnorm(xbuf[slot, :, 0:D_MODEL], g_ref[...]).astype(BF16)
        cmb_ref[...] = xbuf[slot, :, D_MODEL:D_MODEL + LANE]
        first_lane = MOE_GROUPS + tg_ref[i] * EPG
        per_expert = -(-tm // (EPG - 1))
        y = jnp.zeros((tm, D_MODEL), F32)
        issued = jnp.int32(0)
        for e in range(EPG):
            x = xb_ref[...]
            a = _dot(x, wg_ref[0, e])
            b = _dot(x, wu_ref[0, e])
            ce = jnp.sum(jnp.where(lane == first_lane + e + issued, cmb_ref[...], 0.0), axis=-1, keepdims=True)
            y = y + _dot(((a * jax.nn.sigmoid(a)) * b * ce).astype(BF16), wd_ref[0, e])
            for r in range(e * per_expert, min((e + 1) * per_expert, tm)):
                row_copy(idxn_ref, r, 1 - slot).start(priority=r % 2)
            if e in ANCHOR_AFTER:
                issued = pl.semaphore_read(fence_sem)
        for blk in range(ROW_SLABS):
            y_ref[pl.ds(blk, tm, stride=ROW_SLABS), :] = y[:, blk * LANE:(blk + 1) * LANE]


def _moe(tile_group, tile_ok, src3, xt, g, wg, wu, wd):
    n_tiles, _, tm = src3.shape
    last = n_tiles - 1
    grid_spec = pltpu.PrefetchScalarGridSpec(
        num_scalar_prefetch=2,
        grid=(n_tiles,),
        in_specs=[pl.BlockSpec((1, 1, tm), lambda i, tg, ok: (i, 0, 0), memory_space=pltpu.SMEM),
                  pl.BlockSpec((1, 1, tm), lambda i, tg, ok: (jnp.minimum(i + 1, last), 0, 0), memory_space=pltpu.SMEM),
                  pl.BlockSpec(memory_space=pl.ANY),
                  pl.BlockSpec((1, D_MODEL), lambda i, tg, ok: (0, 0)),
                  pl.BlockSpec((1, EPG, D_MODEL, EXPERT_FF), lambda i, tg, ok: (tg[i], 0, 0, 0)),
                  pl.BlockSpec((1, EPG, D_MODEL, EXPERT_FF), lambda i, tg, ok: (tg[i], 0, 0, 0)),
                  pl.BlockSpec((1, EPG, EXPERT_FF, D_MODEL), lambda i, tg, ok: (tg[i], 0, 0, 0))],
        out_specs=pl.BlockSpec((tm * ROW_SLABS, LANE), lambda i, tg, ok: (i, 0)),
        scratch_shapes=[pltpu.VMEM((2, tm, XT_WIDTH), F32), pltpu.SemaphoreType.DMA((2,)),
                        pltpu.VMEM((tm, D_MODEL), BF16), pltpu.VMEM((tm, LANE), F32), pltpu.SemaphoreType.REGULAR],
    )
    return pl.pallas_call(
        _moe_kernel,
        grid_spec=grid_spec,
        out_shape=jax.ShapeDtypeStruct((n_tiles * tm * ROW_SLABS, LANE), F32),
        compiler_params=pltpu.CompilerParams(dimension_semantics=("arbitrary",), vmem_limit_bytes=VMEM_LIMIT_MOE),
        name="moe",
    )(tile_group, tile_ok, src3, src3, xt, g, wg, wu, wd)


def _final_kernel(pos_ref, posn_ref, y_hbm, h_ref, gf_ref, o_ref, ybuf, sem):
    i = pl.program_id(0)
    slot = i % 2
    tm = h_ref.shape[0]

    def issue(index_ref, dst_slot):
        group = 8

        def body(j, c):
            for k in range(group):
                r = j * group + k
                src_row = pl.multiple_of(index_ref[0, 0, r], ROW_SLABS)
                pltpu.make_async_copy(y_hbm.at[pl.ds(src_row, ROW_SLABS), :],
                                      ybuf.at[dst_slot, pl.ds(r * ROW_SLABS, ROW_SLABS), :],
                                      sem.at[dst_slot]).start(priority=k % 2)
            return c
        lax.fori_loop(0, tm // group, body, 0)

    @pl.when(i == 0)
    def _():
        issue(pos_ref, 0)

    @pl.when(i + 1 < pl.num_programs(0))
    def _():
        issue(posn_ref, 1 - slot)

    pltpu.make_async_copy(ybuf.at[slot], ybuf.at[slot], sem.at[slot]).wait()
    y = jnp.concatenate([ybuf[slot, pl.ds(blk, tm, stride=ROW_SLABS), :] for blk in range(ROW_SLABS)], axis=-1)
    h = h_ref[...] + y
    o_ref[...] = h * lax.rsqrt(jnp.mean(h * h, axis=-1, keepdims=True) + EPS) * gf_ref[...]


def _final(pos3, y_sorted, h, gf):
    n_tiles, _, tm = pos3.shape
    last = n_tiles - 1
    row = lambda i: (i, 0)
    return pl.pallas_call(
        _final_kernel,
        grid=(n_tiles,),
        in_specs=[pl.BlockSpec((1, 1, tm), lambda i: (i, 0, 0), memory_space=pltpu.SMEM),
                  pl.BlockSpec((1, 1, tm), lambda i: (jnp.minimum(i + 1, last), 0, 0), memory_space=pltpu.SMEM),
                  pl.BlockSpec(memory_space=pl.ANY),
                  pl.BlockSpec((tm, D_MODEL), row),
                  pl.BlockSpec((1, D_MODEL), lambda i: (0, 0))],
        out_specs=pl.BlockSpec((tm, D_MODEL), row),
        out_shape=jax.ShapeDtypeStruct((n_tiles * tm, D_MODEL), F32),
        scratch_shapes=[pltpu.VMEM((2, tm * ROW_SLABS, LANE), F32), pltpu.SemaphoreType.DMA((2,))],
        compiler_params=_cparams(("arbitrary",)),
        name="final",
    )(pos3, pos3, y_sorted, h, gf)


def _qa_perm():
    new = np.arange(NSA_HEADS * NSA_HD)
    r, g, d = new // LANE, (new % LANE) // NSA_HD, new % NSA_HD
    return (g * NSA_R + r) * NSA_HD + d


def _block_diag2(w):
    z = jnp.zeros_like(w)
    return jnp.concatenate([jnp.concatenate([w, z], axis=-1), jnp.concatenate([z, w], axis=-1)], axis=-2)


def kernel(x, rel_bias, ln_mix, w_in, cmp_pos_k, cmp_pos_v, cmp_k_w1, cmp_k_w2, cmp_v_w1, cmp_v_w2,
           diff_lq1, diff_lk1, diff_lq2, diff_lk2, diff_subln, w_out, ln_ffn,
           router_group_w, router_group_b, router_expert_w, router_expert_b,
           exp_w_gate, exp_w_up, exp_w_down, ln_final):
    B, S, D = x.shape
    T = B * S
    assert D == D_MODEL and S % TQ == 0 and S >= WINDOW and T % TM_MOE == 0 and T % TM_FINAL == 0
    x2 = x.reshape(T, D)
    perm = _qa_perm()

    w = w_in[0]
    c_kc, c_vc, c_ks, c_vs, c_kw, c_vw, c_gt = 512, 640, 768, 896, 1024, 1152, 1280
    c_qb = c_gt + N_GATE
    c_kb, c_vb = c_qb + 512, c_qb + 1024
    col = lambda c, n=LANE: w[:, c:c + n]
    w_tok = jnp.concatenate([col(c_kc), col(c_vc), col(c_ks), col(c_kw), col(c_kb, 512)], axis=1).astype(BF16)
    w_feat = jnp.concatenate([w[:, perm], col(c_qb, 512), col(c_vs), col(c_vw), col(c_vb, 512), col(c_gt, N_GATE),
                              jnp.zeros((D, GATE_ROWS - N_GATE), F32)], axis=1).T.astype(BF16)
    qa, kc, vc, ks, kw, qb, kb, vst, vwt, vbt, gates_t = _in_proj(x2, ln_mix[0][None, :], w_tok, w_feat)

    w1k = _block_diag2(cmp_k_w1[0].reshape(CMP_LEN, NSA_HD, CMP_HIDDEN)).astype(BF16)
    w1v = _block_diag2(cmp_v_w1[0].reshape(CMP_LEN, NSA_HD, CMP_HIDDEN)).astype(BF16)
    w2k = _block_diag2(cmp_k_w2[0]).astype(BF16)
    w2v = _block_diag2(cmp_v_w2[0]).astype(BF16)
    posk = jnp.tile(cmp_pos_k[0], (1, NSA_G))
    posv = jnp.tile(cmp_pos_v[0], (1, NSA_G))
    kcmp, vcmpt = _compress(kc, vc, posk, posv, w1k, w1v, w2k, w2v, B, S)

    near, bias_c = _bias_tiles(rel_bias.T, S)

    n_sel = S // SEL_BLOCK
    nrow = S // CMP_STRIDE
    c_start = np.arange(nrow) * CMP_STRIDE
    s_start = np.arange(n_sel) * SEL_BLOCK
    ov = ((c_start[None, :] <= s_start[:, None] + SEL_BLOCK - 1)
          & (c_start[None, :] + CMP_LEN - 1 >= s_start[:, None])
          & (np.arange(nrow)[None, :] < nrow - 1)).astype(np.float32)
    o_a = _nsa(qa, gates_t, kcmp, vcmpt, bias_c, ks, vst, kw, vwt, near[:NSA_HEADS], jnp.asarray(ov, BF16), B, S)

    sub = jnp.tile(diff_subln[0], 2)[:, None]
    o_b = _diff(diff_lq1[0][None, :], diff_lk1[0][None, :], diff_lq2[0][None, :], diff_lk2[0][None, :],
                sub, qb, kb, vbt, near[NSA_HEADS:], B, S)

    w_o = jnp.concatenate([w_out[0][:512][perm], w_out[0][512:]], axis=0).astype(BF16)
    n_r = MOE_GROUPS + N_EXPERTS
    wr = jnp.concatenate([router_group_w[0], router_expert_w[0], jnp.zeros((D, LANE - n_r), F32)], axis=1).astype(BF16)
    br = jnp.concatenate([router_group_b[0], router_expert_b[0], jnp.zeros((LANE - n_r,), F32)])[None, :]
    xt, route, counts = _out_proj(x2, o_a, o_b, w_o, ln_ffn[0][None, :], wr, br)

    tm = TM_MOE
    n_tiles = T // tm + MOE_GROUPS
    cnt = counts[0, :MOE_GROUPS].astype(jnp.int32)
    ends = jnp.cumsum((cnt + tm - 1) // tm * tm)
    starts = ends - (cnt + tm - 1) // tm * tm
    pos = starts[route[:, 0].astype(jnp.int32)] + route[:, 1].astype(jnp.int32)
    order = jnp.argsort(pos).astype(jnp.int32)
    row = jnp.arange(n_tiles * tm, dtype=jnp.int32)
    group_of = lambda r: sum((r >= ends[g]).astype(jnp.int32) for g in range(MOE_GROUPS - 1))
    row_group = group_of(row)
    local = row - starts[row_group]
    compact = (jnp.cumsum(cnt) - cnt)[row_group] + local
    src = jnp.where(local < cnt[row_group], order[jnp.clip(compact, 0, T - 1)], 0)
    tile_start = jnp.arange(n_tiles, dtype=jnp.int32) * tm
    tile_group = group_of(tile_start)
    tile_ok = (tile_start < ends[-1]).astype(jnp.int32)

    by_group = lambda a: a[0].astype(BF16).reshape((MOE_GROUPS, EPG) + a.shape[2:])
    y_sorted = _moe(tile_group, tile_ok, src.reshape(n_tiles, 1, tm), xt, ln_ffn[0][None, :],
                    by_group(exp_w_gate), by_group(exp_w_up), by_group(exp_w_down))
    out = _final((pos * ROW_SLABS).reshape(T // TM_FINAL, 1, TM_FINAL), y_sorted, xt, ln_final[None, :])
    return out.reshape(B, S, D)
```

```python
import math

import numpy as np
import jax
import jax.numpy as jnp
from jax import lax
from jax.experimental import pallas as pl
from jax.experimental.pallas import tpu as pltpu

F32 = jnp.float32
BF16 = jnp.bfloat16
NEG = -1e30
EPS = 1e-6
LOG2E = math.log2(math.e)

D_MODEL = 1024
LANE = 128
NSA_HEADS, NSA_G, NSA_R, NSA_HD = 8, 2, 4, 64
CMP_LEN, CMP_STRIDE, CMP_HIDDEN = 32, 16, 128
SEL_BLOCK, SEL_TOPN, SEL_FORCED_LOCAL, WINDOW = 64, 8, 2, 512
DIFF_HEADS, DIFF_HD = 8, 32
REL_BUCKETS, REL_MAX_EXACT, REL_MAX_DIST = 32, 16, 128
N_REL_HEADS = NSA_HEADS + DIFF_HEADS
MOE_GROUPS, EPG, N_EXPERTS, EXPERT_FF = 4, 8, 32, 256
LAMBDA_INIT = 0.8 - 0.6 * math.exp(-0.3 * 0)
N_GATE = NSA_HEADS * 3
GATE_ROWS = 32

TQ = 256
DIFF_PAIRS_PER_STEP = 4
TM_PROJ = 512
TM_MOE = 512
TM_FINAL = 512
ANCHOR_AFTER = (1, 3, 5, 6)
XT_WIDTH = D_MODEL + LANE
ROW_SLABS = D_MODEL // LANE
VMEM_LIMIT = 48 * 1024 * 1024
VMEM_LIMIT_MOE = 56 * 1024 * 1024


def _cparams(sem):
    return pltpu.CompilerParams(dimension_semantics=sem, vmem_limit_bytes=VMEM_LIMIT)


def _dot(a, b):
    return jnp.dot(a, b, preferred_element_type=F32)


def _rmsnorm(x, g):
    return x * lax.rsqrt(jnp.mean(x * x, axis=-1, keepdims=True) + EPS) * g


def _dot_nt(a, b):
    return lax.dot_general(a, b, (((1,), (1,)), ((), ())), preferred_element_type=F32)


def _bucket_thresholds():
    n = np.arange(0, REL_MAX_DIST + 1)
    nf = np.maximum(n, 1).astype(np.float32)
    large = REL_MAX_EXACT + (np.log(nf / np.float32(REL_MAX_EXACT)) / np.float32(math.log(REL_MAX_DIST / REL_MAX_EXACT))
                             * np.float32(REL_BUCKETS - REL_MAX_EXACT)).astype(np.int32)
    large = np.minimum(large, REL_BUCKETS - 1)
    bucket = np.where(n < REL_MAX_EXACT, n, large)
    return [int(np.argmax(bucket >= b)) for b in range(REL_BUCKETS)]


_THR = _bucket_thresholds()


def _inproj_kernel(x_ref, g_ref, w_ref, wt_ref, qa_ref, kc_ref, vc_ref, ks_ref, kw_ref, qb_ref, kb_ref,
                   vst_ref, vwt_ref, vbt_ref, gt_ref):
    x = x_ref[...]
    xn = (x * lax.rsqrt(jnp.mean(x * x, axis=-1, keepdims=True) + EPS) * g_ref[...]).astype(BF16)
    a = _dot(xn, w_ref[:, 0:512])
    kc_ref[...] = a[:, 0:128]
    vc_ref[...] = a[:, 128:256]
    ks_ref[...] = a[:, 256:384].astype(BF16)
    kw_ref[...] = a[:, 384:512].astype(BF16)
    a = _dot(xn, w_ref[:, 512:1024])
    for r in range(4):
        kb_ref[r] = a[:, r * LANE:(r + 1) * LANE].astype(BF16)

    ft = _dot_nt(wt_ref[...], xn)

    def feat(row0, rows=LANE):
        return ft[row0:row0 + rows, :]
    for r in range(4):
        qa_ref[r] = (feat(r * LANE) * (NSA_HD ** -0.5 * LOG2E)).astype(BF16)
        qb_ref[r] = (feat(512 + r * LANE) * (DIFF_HD ** -0.5 * LOG2E)).astype(BF16)
        vbt_ref[r] = feat(1280 + r * LANE).astype(BF16)
    vst_ref[...] = feat(1024).astype(BF16)
    vwt_ref[...] = feat(1152).astype(BF16)
    gt_ref[...] = feat(1792, GATE_ROWS)


def _in_proj(x2, g, w, wt):
    T = x2.shape[0]
    tm = TM_PROJ
    row = lambda i: (i, 0)
    o128b = jax.ShapeDtypeStruct((T, LANE), BF16)
    o128f = jax.ShapeDtypeStruct((T, LANE), F32)
    o4 = jax.ShapeDtypeStruct((4, T, LANE), BF16)
    ot = jax.ShapeDtypeStruct((LANE, T), BF16)
    o4t = jax.ShapeDtypeStruct((4, LANE, T), BF16)
    s128 = pl.BlockSpec((tm, LANE), row)
    s4 = pl.BlockSpec((4, tm, LANE), lambda i: (0, i, 0))
    st = pl.BlockSpec((LANE, tm), lambda i: (0, i))
    s4t = pl.BlockSpec((4, LANE, tm), lambda i: (0, 0, i))
    return pl.pallas_call(
        _inproj_kernel,
        grid=(T // tm,),
        in_specs=[pl.BlockSpec((tm, D_MODEL), row),
                  pl.BlockSpec((1, D_MODEL), lambda i: (0, 0)),
                  pl.BlockSpec(w.shape, lambda i: (0, 0)),
                  pl.BlockSpec(wt.shape, lambda i: (0, 0))],
        out_specs=[s4t, s128, s128, s128, s128, s4t, s4, st, st, s4t, pl.BlockSpec((GATE_ROWS, tm), lambda i: (0, i))],
        out_shape=[o4t, o128f, o128f, o128b, o128b, o4t, o4, ot, ot, o4t, jax.ShapeDtypeStruct((GATE_ROWS, T), F32)],
        compiler_params=_cparams(("parallel",)),
        name="in_proj",
    )(x2, g, w, wt)


def _gelu_tanh(x):
    return 0.5 * x * (1.0 + jnp.tanh(math.sqrt(2.0 / math.pi) * (x + 0.044715 * (x * x * x))))


def _compress_kernel(kc_ref, vc_ref, posk_ref, posv_ref, w1k_ref, w1v_ref, w2k_ref, w2v_ref, ko_ref, vo_ref):
    nrow = kc_ref.shape[0] // CMP_STRIDE
    rid = lax.broadcasted_iota(jnp.int32, (nrow, 1), 0)
    cid = lax.broadcasted_iota(jnp.int32, (1, nrow), 1)
    for src, pos, w1, w2, out, transposed in ((kc_ref, posk_ref, w1k_ref, w2k_ref, ko_ref, False),
                                              (vc_ref, posv_ref, w1v_ref, w2v_ref, vo_ref, True)):
        hid_a = jnp.zeros((nrow, 2 * CMP_HIDDEN), F32)
        hid_b = jnp.zeros((nrow, 2 * CMP_HIDDEN), F32)
        for m in range(CMP_STRIDE):
            y = src[pl.ds(m, nrow, stride=CMP_STRIDE), :]
            hid_a = hid_a + _dot((y + pos[m:m + 1, :]).astype(BF16), w1[m])
            hid_b = hid_b + _dot((y + pos[CMP_STRIDE + m:CMP_STRIDE + m + 1, :]).astype(BF16), w1[CMP_STRIDE + m])
        hid = hid_a + pltpu.roll(hid_b, nrow - 1, 0)
        o = _dot(_gelu_tanh(hid).astype(BF16), w2[...])
        if transposed:
            out[0] = jnp.where(cid < nrow - 1, o.T, 0.0).astype(BF16)
        else:
            out[0] = jnp.where(rid < nrow - 1, o, 0.0).astype(BF16)


def _compress(kc, vc, posk, posv, w1k, w1v, w2k, w2v, B, S):
    nrow = S // CMP_STRIDE
    assert nrow == LANE
    full = lambda a: pl.BlockSpec(a.shape, lambda b: (0,) * a.ndim)
    src = pl.BlockSpec((S, LANE), lambda b: (b, 0))
    osp = pl.BlockSpec((1, nrow, LANE), lambda b: (b, 0, 0))
    osh = jax.ShapeDtypeStruct((B, nrow, LANE), BF16)
    return pl.pallas_call(
        _compress_kernel,
        grid=(B,),
        in_specs=[src, src, full(posk), full(posv), full(w1k), full(w1v), full(w2k), full(w2v)],
        out_specs=[osp, osp],
        out_shape=[osh, osh],
        compiler_params=_cparams(("parallel",)),
        name="compress",
    )(kc, vc, posk, posv, w1k, w1v, w2k, w2v)


def _bias_from_dist(dist, tab_ref, h):
    val = jnp.full(dist.shape, tab_ref[h, 0], F32)
    for b in range(1, REL_BUCKETS):
        val = jnp.where(dist >= _THR[b], tab_ref[h, b], val)
    return val


def _bias_near_kernel(tab_ref, out_ref):
    h = pl.program_id(0)
    j = lax.broadcasted_iota(jnp.int32, (TQ, TQ), 0)
    i = lax.broadcasted_iota(jnp.int32, (TQ, TQ), 1)
    for d in range(2):
        bias = (_bias_from_dist(i - j + d * TQ, tab_ref, h) - tab_ref[h, REL_BUCKETS - 1]) * LOG2E
        out_ref[0, d] = jnp.where(i - j + d * TQ >= 0, bias, NEG)


def _bias_cmp_kernel(tab_ref, out_ref):
    h = pl.program_id(0)
    qi = pl.program_id(1)
    band = TQ // CMP_STRIDE
    out_ref[0] = jnp.full((LANE, TQ), tab_ref[h, REL_BUCKETS - 1] * LOG2E, F32)
    c0 = pl.multiple_of(jnp.maximum(qi - 1, 0) * band, band)
    c = c0 + lax.broadcasted_iota(jnp.int32, (2 * band, TQ), 0)
    t = qi * TQ + lax.broadcasted_iota(jnp.int32, (2 * band, TQ), 1)
    out_ref[0, pl.ds(c0, 2 * band), :] = _bias_from_dist(t - (c * CMP_STRIDE + CMP_LEN - 1), tab_ref, h) * LOG2E


def _bias_tiles(tab_t, S):
    smem = pl.BlockSpec(memory_space=pltpu.SMEM)
    near = pl.pallas_call(
        _bias_near_kernel,
        grid=(N_REL_HEADS,),
        in_specs=[smem],
        out_specs=pl.BlockSpec((1, 2, TQ, TQ), lambda h: (h, 0, 0, 0)),
        out_shape=jax.ShapeDtypeStruct((N_REL_HEADS, 2, TQ, TQ), F32),
        compiler_params=_cparams(("parallel",)),
        name="bias_near",
    )(tab_t)
    cmp_bias = pl.pallas_call(
        _bias_cmp_kernel,
        grid=(NSA_HEADS, S // TQ),
        in_specs=[smem],
        out_specs=pl.BlockSpec((1, LANE, TQ), lambda h, q: (h, 0, q)),
        out_shape=jax.ShapeDtypeStruct((NSA_HEADS, LANE, S), F32),
        compiler_params=_cparams(("parallel", "parallel")),
        name="bias_cmp",
    )(tab_t)
    return near, cmp_bias


ACC_ROWS = NSA_HD + 16


def _with_ones(vt):
    return jnp.concatenate([vt, jnp.ones((ACC_ROWS - vt.shape[0], vt.shape[1]), BF16)], axis=0)


def _probs(s, m):
    return jnp.exp2((s - m).astype(BF16))


def _flash_first(ss, vts):
    ms = [jnp.max(s, axis=0, keepdims=True) for s in ss]
    ps = [_probs(s, m) for s, m in zip(ss, ms)]
    return tuple((m, _dot(vt, p)) for m, p, vt in zip(ms, ps, vts))


def _flash_update(ss, vts, sts):
    ms = [jnp.maximum(st[0], jnp.max(s, axis=0, keepdims=True)) for s, st in zip(ss, sts)]
    alphas = [jnp.exp2(st[0] - m) for m, st in zip(ms, sts)]
    ps = [_probs(s, m) for s, m in zip(ss, ms)]
    return tuple((m, a * st[1] + _dot(vt, p)) for m, a, st, p, vt in zip(ms, alphas, sts, ps, vts))


def _flash_out(st):
    acc = st[1]
    return acc[0:NSA_HD, :] * (1.0 / acc[NSA_HD:NSA_HD + 1, :])


def _stream_scratch(n):
    return ([pltpu.VMEM((8, TQ), F32), pltpu.VMEM((ACC_ROWS, TQ), F32)] * n
            + [pltpu.VMEM((TQ, TQ), F32)] * (2 * n) + [pltpu.VMEM((TQ, TQ), BF16)] * (2 * n))


def _stream_refs(refs, n):
    st_refs = tuple((refs[2 * c], refs[2 * c + 1]) for c in range(n))
    s, p = refs[2 * n:4 * n], refs[4 * n:6 * n]
    return st_refs, tuple(s[:n]), tuple(s[n:]), tuple(p[:n]), tuple(p[n:])


def _causal_stream(qi, scores, vtile, st_refs, s_a, s_b, p_a, p_b, diag_ready=False):
    n_far = jnp.maximum(qi - 1, 0)
    top = qi - 2

    chains = range(len(st_refs))

    def write(s_refs, kt, near=2):
        for c in chains:
            s_refs[c][...] = scores(jnp.maximum(kt, 0), near, c)

    def pending(p_refs, kt):
        return [_dot(vtile(jnp.maximum(kt, 0), c), p_refs[c][...]) for c in chains]

    def step(s_cur, p_cur, s_next, kt_next, p_prev, kt_prev, anchor=None):
        rescale = []
        for c in chains:
            nxt = scores(jnp.maximum(kt_next, 0), 2, c)
            s_next[c][...] = nxt
            pv = _dot(vtile(jnp.maximum(kt_prev, 0), c), p_prev[c][...])
            st = st_refs[c][0]
            s = s_cur[c][...]
            m_old = st[0:1, :]
            m = jnp.maximum(m_old, jnp.max(s, axis=0, keepdims=True))
            if anchor is not None:
                m = m + anchor
            anchor = nxt[0:1, :] * 0.0
            p_cur[c][...] = _probs(s, m)
            st[0:1, :] = m
            rescale.append((jnp.exp2(m_old - m), pv))
        for c in chains:
            acc = st_refs[c][1]
            acc[...] = rescale[c][0] * (acc[...] + rescale[c][1])
        return anchor

    if not diag_ready:
        write(s_a, qi, 0)
    for c in chains:
        s_b[c][...] = scores(jnp.maximum(qi - 1, 0), 1, c)
        st, acc = st_refs[c]
        s = s_a[c][...]
        m = jnp.max(s, axis=0, keepdims=True)
        p_a[c][...] = _probs(s, m)
        st[0:1, :] = m
        acc[...] = jnp.zeros(acc.shape, F32)
    step(s_b, p_b, s_a, top, p_a, qi)

    def pair(j, c):
        kt = top - 2 * j
        anchor = step(s_a, p_a, s_b, kt - 1, p_b, kt + 1)
        step(s_b, p_b, s_a, kt - 2, p_a, kt, anchor)
        return c
    lax.fori_loop(0, n_far // 2, pair, 0)

    @pl.when(n_far % 2 == 1)
    def _():
        for c, pv in zip(chains, pending(p_b, 1)):
            st, acc = st_refs[c]
            s = s_a[c][...]
            m_old = st[0:1, :]
            m = jnp.maximum(m_old, jnp.max(s, axis=0, keepdims=True))
            st[0:1, :] = m
            acc[...] = jnp.exp2(m_old - m) * (acc[...] + pv) + _dot(vtile(0, c), _probs(s, m))

    @pl.when(n_far % 2 == 0)
    def _():
        last = jnp.where(n_far == 0, qi - 1, 0)
        for c, pv in zip(chains, pending(p_b, last)):
            acc = st_refs[c][1]
            acc[...] = acc[...] + pv
    return tuple((st[0:1, :], acc[...]) for st, acc in st_refs)


def _ktile(ref, kt):
    return ref[pl.ds(pl.multiple_of(kt * TQ, TQ), TQ), :]


def _nsa_kernel(qa_ref, gt_ref, kcmp_ref, vcmpt_ref, bc_ref, ks_ref, vst_ref, kw_ref, vwt_ref, dn_ref,
                ov_ref, o_ref, psum_ref, sel_ref, oacc_ref, sig_ref, *scratch):
    st_refs, s_a, s_b, p_a, p_b = _stream_refs(scratch, NSA_HEADS)
    qi = pl.program_id(1)
    t0 = qi * TQ
    sub_grp = lax.shift_right_arithmetic(lax.broadcasted_iota(jnp.int32, (LANE, 1), 0), 6)
    sig_ref[...] = jax.nn.sigmoid(gt_ref[...])

    def gate_row(c):
        return sig_ref[pl.ds(c, 1), :]

    def masked_q(r, g):
        return jnp.where(sub_grp == g, qa_ref[r].astype(F32), 0.0).astype(BF16)

    n_cmp = kcmp_ref.shape[1] - 1
    crow = lax.broadcasted_iota(jnp.int32, (LANE, 1), 0)
    cmp_end = jnp.where(crow < n_cmp, crow * CMP_STRIDE + CMP_LEN - 1, 1 << 30)
    mask_c = (t0 + lax.broadcasted_iota(jnp.int32, (1, TQ), 1)) >= cmp_end

    heads = [(r, g) for r in range(NSA_R) for g in range(NSA_G)]
    scores_c = [_dot(kcmp_ref[0], masked_q(r, g)) for r, g in heads]
    probs_c = []
    for (r, g), s in zip(heads, scores_c):
        s = jnp.where(mask_c, s + bc_ref[g * NSA_R + r], NEG)
        p = jnp.where(mask_c, jnp.exp2(s - jnp.max(s, axis=0, keepdims=True)), 0.0)
        l = jnp.sum(p, axis=0, keepdims=True)
        probs_c.append(p * jnp.where(l > 0.0, 1.0 / l, 0.0))
    for g in range(NSA_G):
        psum_ref[g] = sum(p for (r, gg), p in zip(heads, probs_c) if gg == g)
    outs_c = [_dot(vcmpt_ref[0], p.astype(BF16)) for p in probs_c]
    for r in range(NSA_R):
        o0, o1 = outs_c[r * NSA_G], outs_c[r * NSA_G + 1]
        oacc_ref[r] = jnp.where(sub_grp == 0, gate_row(r * 3) * o0, gate_row((NSA_R + r) * 3) * o1)

    has_prev = jnp.where(qi >= 1, 0.0, NEG)
    has_wfar = jnp.where(qi >= WINDOW // TQ, 0.0, NEG)
    kt_prev = jnp.maximum(qi - 1, 0)
    kt_wfar = jnp.maximum(qi - WINDOW // TQ, 0)
    wfar_mask = lax.broadcasted_iota(jnp.int32, (TQ, TQ), 0) > lax.broadcasted_iota(jnp.int32, (TQ, TQ), 1)

    def attend_all_heads():
        chains = [(r, g) for r in range(NSA_R) for g in range(NSA_G)]
        qms = [masked_q(r, g) for r, g in chains]
        hids = [g * NSA_R + r for r, g in chains]

        per_tile = TQ // SEL_BLOCK
        key_blk = lax.shift_right_arithmetic(lax.broadcasted_iota(jnp.int32, (TQ, LANE), 0), 6)
        lane_id = lax.broadcasted_iota(jnp.int32, (TQ, LANE), 1)

        def keys_with_block_onehot(kt, g):
            other = (1 - g) * NSA_HD
            onehot = jnp.where(lane_id - other == key_blk, 1.0, 0.0).astype(BF16)
            in_other = (lane_id >= other) & (lane_id < other + NSA_HD)
            return jnp.where(in_other, onehot, _ktile(ks_ref, kt))

        def query_with_mask_rows(kt, c):
            r, g = chains[c]
            own = qa_ref[r, g * NSA_HD:(g + 1) * NSA_HD, :]
            pair = sel_ref[g, pl.ds(pl.multiple_of((kt // 2) * 2 * per_tile, 2 * per_tile), 2 * per_tile), :]
            rows = jnp.where(kt % 2 == 0, pair[0:per_tile], pair[per_tile:2 * per_tile]).astype(BF16)
            other = jnp.concatenate([rows, jnp.zeros((NSA_HD - per_tile, TQ), BF16)], axis=0)
            return jnp.concatenate([own, other] if g == 0 else [other, own], axis=0)

        def near_bias(s, near, c):
            if near == 0:
                return s + dn_ref[hids[c], 0]
            if near == 1:
                return s + dn_ref[hids[c], 1] + has_prev
            return s

        def group_values(ref, kt, c):
            g = chains[c][1]
            return _with_ones(ref[g * NSA_HD:(g + 1) * NSA_HD, pl.ds(pl.multiple_of(kt * TQ, TQ), TQ)])

        def finish(sts, branch):
            for c, (r, g) in enumerate(chains):
                oacc_ref[r, g * NSA_HD:(g + 1) * NSA_HD, :] += gate_row(hids[c] * 3 + branch) * _flash_out(sts[c])

        def slc_scores(kt, near, c):
            return near_bias(_dot(keys_with_block_onehot(kt, chains[c][1]), query_with_mask_rows(kt, c)), near, c)

        def slc_vtile(kt, c):
            return group_values(vst_ref, kt, c)

        win_tiles = ((qi, 0), (kt_prev, 1), (kt_wfar, 2))

        def win_scores(c):
            out = []
            for kt, near in win_tiles:
                s = near_bias(_dot(_ktile(kw_ref, kt), qms[c]), near, c)
                out.append(jnp.where(wfar_mask, s + has_wfar, NEG) if near == 2 else s)
            return out

        sts = []
        ahead = win_scores(0)
        for c in range(len(chains)):
            tiles, ahead = ahead, (win_scores(c + 1) if c + 1 < len(chains) else None)
            st = _flash_first([tiles[0]], [group_values(vwt_ref, qi, c)])
            st = _flash_update([tiles[1]], [group_values(vwt_ref, kt_prev, c)], st)
            sts.append(_flash_update([tiles[2]], [group_values(vwt_ref, kt_wfar, c)], st)[0])
        finish(sts, 2)

        n_sel = ov_ref.shape[0]
        jj = lax.broadcasted_iota(jnp.int32, (n_sel, TQ), 0)
        cur = lax.shift_right_arithmetic(t0 + lax.broadcasted_iota(jnp.int32, (n_sel, TQ), 1), 6)
        valid = jj <= cur
        forced = (jj == 0) | (cur - jj < SEL_FORCED_LOCAL)
        for g in range(NSA_G):
            ps = psum_ref[g]
            hi = ps.astype(BF16)
            rem = ps - hi.astype(F32)
            mid = rem.astype(BF16)
            lo = (rem - mid.astype(F32)).astype(BF16)
            ov = ov_ref[...]
            imp = _dot(ov, hi) + _dot(ov, mid) + _dot(ov, lo)
            score = jnp.where(valid, jnp.where(forced, 1e9, imp), -1e9)
            jf = jj.astype(F32)
            rest = score
            chosen = jnp.zeros((n_sel, TQ), F32)
            for _ in range(min(SEL_TOPN, n_sel)):
                best = jnp.max(rest, axis=0, keepdims=True)
                first = jnp.min(jnp.where(rest == best, jf, float(n_sel)), axis=0, keepdims=True)
                hit = jf == first
                chosen = jnp.where(hit, 1.0, chosen)
                rest = jnp.where(hit, -3e38, rest)
            sel_ref[g] = jnp.where(chosen > 0.5, jnp.where(score > -1e8, 0.0, NEG), NEG)

        finish(_causal_stream(qi, slc_scores, slc_vtile, st_refs, s_a, s_b, p_a, p_b), 1)
    attend_all_heads()

    for r in range(NSA_R):
        o_ref[r] = oacc_ref[r].T.astype(BF16)


def _nsa(qa, gates_t, kcmp, vcmpt, bias_c, ks, vst, kw, vwt, near_a, ov, B, S):
    nq = S // TQ
    T = B * S
    k_spec = pl.BlockSpec((S, LANE), lambda b, q: (b, 0))
    v_spec = pl.BlockSpec((LANE, S), lambda b, q: (0, b))
    full = lambda a: pl.BlockSpec(a.shape, lambda b, q: (0,) * a.ndim)
    cmp_spec = pl.BlockSpec((1, LANE, LANE), lambda b, q: (b, 0, 0))
    qo = pl.BlockSpec((4, TQ, LANE), lambda b, q: (0, b * nq + q, 0))
    return pl.pallas_call(
        _nsa_kernel,
        grid=(B, nq),
        in_specs=[pl.BlockSpec((4, LANE, TQ), lambda b, q: (0, 0, b * nq + q)),
                  pl.BlockSpec((GATE_ROWS, TQ), lambda b, q: (0, b * nq + q)),
                  cmp_spec, cmp_spec,
                  pl.BlockSpec((NSA_HEADS, LANE, TQ), lambda b, q: (0, 0, q)),
                  k_spec, v_spec, k_spec, v_spec, full(near_a), full(ov)],
        out_specs=qo,
        out_shape=jax.ShapeDtypeStruct((4, T, LANE), BF16),
        scratch_shapes=[pltpu.VMEM((NSA_G, LANE, TQ), F32), pltpu.VMEM((NSA_G, S // SEL_BLOCK, TQ), F32),
                        pltpu.VMEM((NSA_R, LANE, TQ), F32), pltpu.VMEM((GATE_ROWS, TQ), F32)]
                       + _stream_scratch(NSA_HEADS),
        compiler_params=_cparams(("parallel", "arbitrary")),
        name="nsa",
    )(qa, gates_t, kcmp, vcmpt, bias_c, ks, vst, kw, vwt, near_a, ov)


def _diff_kernel(lq1_ref, lk1_ref, lq2_ref, lk2_ref, sub_ref, qb_ref, kb_ref, vbt_ref, dn_ref, o_ref, *scratch):
    n = DIFF_PAIRS_PER_STEP * 4
    st_refs, s_a, s_b, p_a, p_b = _stream_refs(scratch, n)
    qi = pl.program_id(1)
    chain_of_row = lax.shift_right_arithmetic(lax.broadcasted_iota(jnp.int32, (LANE, 1), 0), 5)
    lam = (jnp.exp(jnp.sum(lq1_ref[...] * lk1_ref[...], axis=-1, keepdims=True))
           - jnp.exp(jnp.sum(lq2_ref[...] * lk2_ref[...], axis=-1, keepdims=True)) + LAMBDA_INIT)
    has_prev = jnp.where(qi >= 1, 0.0, NEG)
    n_steps = DIFF_HEADS // 2 // DIFF_PAIRS_PER_STEP

    def score_fn(step):
        pairs = [step * DIFF_PAIRS_PER_STEP + c // 4 for c in range(n)]
        qs = [qb_ref[step * DIFF_PAIRS_PER_STEP + i].astype(F32) for i in range(DIFF_PAIRS_PER_STEP)]
        qms = [jnp.where(chain_of_row == c % 4, qs[c // 4], 0.0).astype(BF16) for c in range(n)]

        def scores(kt, near, c):
            s = _dot(kb_ref[pairs[c], pl.ds(pl.multiple_of(kt * TQ, TQ), TQ), :], qms[c])
            if near == 2:
                return s
            bias = dn_ref[2 * pairs[c] + (c % 4) // 2, near]
            return s + bias if near == 0 else s + bias + has_prev
        return scores

    def start_diag(step):
        scores = score_fn(step)
        for c in range(n):
            s_a[c][...] = scores(qi, 0, c)

    start_diag(0)

    def step_body(step, carry):
        def vtile(kt, c):
            hh = (c % 4) // 2
            return _with_ones(vbt_ref[step * DIFF_PAIRS_PER_STEP + c // 4, hh * 2 * DIFF_HD:(hh + 1) * 2 * DIFF_HD,
                                      pl.ds(pl.multiple_of(kt * TQ, TQ), TQ)])

        sts = _causal_stream(qi, score_fn(step), vtile, st_refs, s_a, s_b, p_a, p_b, diag_ready=True)
        start_diag(jnp.minimum(step + 1, n_steps - 1))
        for i in range(DIFF_PAIRS_PER_STEP):
            outs = []
            for hh in range(2):
                c = 4 * i + 2 * hh
                o = _flash_out(sts[c]) - lam * _flash_out(sts[c + 1])
                outs.append(o * lax.rsqrt(jnp.mean(o * o, axis=0, keepdims=True) + EPS))
            out = jnp.concatenate(outs, axis=0) * sub_ref[...] * (1.0 - LAMBDA_INIT)
            o_ref[step * DIFF_PAIRS_PER_STEP + i] = out.T.astype(BF16)
        return carry
    lax.fori_loop(0, n_steps, step_body, 0)


def _diff(lq1, lk1, lq2, lk2, sub, qb, kb, vbt, near_b, B, S):
    nq = S // TQ
    T = B * S
    full = lambda a: pl.BlockSpec(a.shape, lambda b, q: (0,) * a.ndim)
    k_spec = pl.BlockSpec((4, S, LANE), lambda b, q: (0, b, 0))
    v_spec = pl.BlockSpec((4, LANE, S), lambda b, q: (0, 0, b))
    qo = pl.BlockSpec((4, TQ, LANE), lambda b, q: (0, b * nq + q, 0))
    return pl.pallas_call(
        _diff_kernel,
        grid=(B, nq),
        in_specs=[full(lq1), full(lk1), full(lq2), full(lk2), full(sub),
                  pl.BlockSpec((4, LANE, TQ), lambda b, q: (0, 0, b * nq + q)), k_spec, v_spec, full(near_b)],
        out_specs=qo,
        out_shape=jax.ShapeDtypeStruct((4, T, LANE), BF16),
        scratch_shapes=_stream_scratch(DIFF_PAIRS_PER_STEP * 4),
        compiler_params=_cparams(("parallel", "arbitrary")),
        name="diff",
    )(lq1, lk1, lq2, lk2, sub, qb, kb, vbt, near_b)


def _outproj_kernel(x_ref, oa_ref, ob_ref, w_ref, g_ref, wr_ref, br_ref, xt_ref, route_ref, cnt_ref, carry_ref):
    @pl.when(pl.program_id(0) == 0)
    def _():
        carry_ref[...] = jnp.zeros(carry_ref.shape, F32)

    o = jnp.concatenate([oa_ref[r] for r in range(4)] + [ob_ref[r] for r in range(4)], axis=-1)
    h = x_ref[...] + _dot(o, w_ref[...])
    xt_ref[:, 0:D_MODEL] = h
    tn = _rmsnorm(h, g_ref[...]).astype(BF16)
    logits = _dot(tn, wr_ref[...]) + br_ref[...]
    lane = lax.broadcasted_iota(jnp.int32, (1, LANE), 1)
    lane_f = lane.astype(F32)
    is_grp = lane < MOE_GROUPS
    lg = jnp.where(is_grp, logits, NEG)
    mg = jnp.max(lg, axis=-1, keepdims=True)
    zg = jnp.sum(jnp.where(is_grp, jnp.exp(lg - mg), 0.0), axis=-1, keepdims=True)
    g_prob = 1.0 / zg
    g_idx = jnp.min(jnp.where(lg == mg, lane_f, 1e9), axis=-1, keepdims=True)
    lane_grp = jnp.where((lane >= MOE_GROUPS) & (lane < MOE_GROUPS + N_EXPERTS),
                         lax.shift_right_arithmetic(lane - MOE_GROUPS, 3), -1).astype(F32)
    le = jnp.where(lane_grp == g_idx, logits, NEG)
    m1 = jnp.max(le, axis=-1, keepdims=True)
    e1 = jnp.min(jnp.where(le == m1, lane_f, 1e9), axis=-1, keepdims=True)
    le2 = jnp.where(lane_f == e1, NEG, le)
    m2 = jnp.max(le2, axis=-1, keepdims=True)
    e2 = jnp.min(jnp.where(le2 == m2, lane_f, 1e9), axis=-1, keepdims=True)
    ratio = jnp.exp(m2 - m1)
    w1 = g_prob / (1.0 + ratio)
    w2 = w1 * ratio
    xt_ref[:, D_MODEL:D_MODEL + LANE] = jnp.where(lane_f == e1, w1, 0.0) + jnp.where(lane_f == e2, w2, 0.0)

    tm = h.shape[0]
    onehot = jnp.where(lane_f == g_idx, 1.0, 0.0)
    earlier = jnp.where(lax.broadcasted_iota(jnp.int32, (tm, tm), 0) > lax.broadcasted_iota(jnp.int32, (tm, tm), 1),
                        1.0, 0.0).astype(BF16)
    prefix = _dot(earlier, onehot.astype(BF16)) + carry_ref[...]
    rank = jnp.sum(onehot * prefix, axis=-1, keepdims=True)
    carry_ref[...] += jnp.sum(onehot, axis=0, keepdims=True)
    route_ref[...] = jnp.where(lane == 0, g_idx, jnp.where(lane == 1, rank, 0.0))
    cnt_ref[...] = jnp.broadcast_to(carry_ref[...], cnt_ref.shape)


def _out_proj(x2, oa, ob, w, g, wr, br):
    T = x2.shape[0]
    tm = TM_PROJ
    row = lambda i: (i, 0)
    full = lambda a: pl.BlockSpec(a.shape, lambda i: (0,) * a.ndim)
    o4 = pl.BlockSpec((4, tm, LANE), lambda i: (0, i, 0))
    return pl.pallas_call(
        _outproj_kernel,
        grid=(T // tm,),
        in_specs=[pl.BlockSpec((tm, D_MODEL), row), o4, o4, full(w), full(g), full(wr), full(br)],
        out_specs=[pl.BlockSpec((tm, XT_WIDTH), row), pl.BlockSpec((tm, LANE), row),
                   pl.BlockSpec((8, LANE), lambda i: (0, 0))],
        out_shape=[jax.ShapeDtypeStruct((T, XT_WIDTH), F32), jax.ShapeDtypeStruct((T, LANE), F32),
                   jax.ShapeDtypeStruct((8, LANE), F32)],
        scratch_shapes=[pltpu.VMEM((1, LANE), F32)],
        compiler_params=_cparams(("arbitrary",)),
        name="out_proj",
    )(x2, oa, ob, w, g, wr, br)


def _moe_kernel(tg_ref, ok_ref, idx_ref, idxn_ref, xt_hbm, g_ref, wg_ref, wu_ref, wd_ref, y_ref, xbuf, sem,
                xb_ref, cmb_ref, fence_sem):
    i = pl.program_id(0)
    slot = i % 2
    tm = xbuf.shape[1]

    def row_copy(index_ref, r, dst_slot):
        return pltpu.make_async_copy(xt_hbm.at[pl.ds(index_ref[0, 0, r], 1), :],
                                     xbuf.at[dst_slot, pl.ds(r, 1), :], sem.at[dst_slot])

    @pl.when(i == 0)
    def _():
        def body(r, c):
            row_copy(idx_ref, r, 0).start()
            return c
        lax.fori_loop(0, tm, body, 0)

    @pl.when((i == 0) | (ok_ref[jnp.maximum(i - 1, 0)] == 1))
    def _():
        pltpu.make_async_copy(xt_hbm.at[pl.ds(0, tm), :], xbuf.at[slot], sem.at[slot]).wait()

    @pl.when(ok_ref[i] == 0)
    def _():
        y_ref[...] = jnp.zeros(y_ref.shape, F32)

    @pl.when(ok_ref[i] == 1)
    def _():
        lane = lax.broadcasted_iota(jnp.int32, (1, LANE), 1)
        xb_ref[...] = _rmsnorm(xbuf[slot, :, 0:D_MODEL], g_ref[...]).astype(BF16)
        cmb_ref[...] = xbuf[slot, :, D_MODEL:D_MODEL + LANE]
        first_lane = MOE_GROUPS + tg_ref[i] * EPG
        per_expert = -(-tm // (EPG - 1))
        y = jnp.zeros((tm, D_MODEL), F32)
        issued = jnp.int32(0)
        for e in range(EPG):
            x = xb_ref[...]
            a = _dot(x, wg_ref[0, e])
            b = _dot(x, wu_ref[0, e])
            ce = jnp.sum(jnp.where(lane == first_lane + e + issued, cmb_ref[...], 0.0), axis=-1, keepdims=True)
            y = y + _dot(((a * jax.nn.sigmoid(a)) * b * ce).astype(BF16), wd_ref[0, e])
            for r in range(e * per_expert, min((e + 1) * per_expert, tm)):
                row_copy(idxn_ref, r, 1 - slot).start(priority=r % 2)
            if e in ANCHOR_AFTER:
                issued = pl.semaphore_read(fence_sem)
        for blk in range(ROW_SLABS):
            y_ref[pl.ds(blk, tm, stride=ROW_SLABS), :] = y[:, blk * LANE:(blk + 1) * LANE]


def _moe(tile_group, tile_ok, src3, xt, g, wg, wu, wd):
    n_tiles, _, tm = src3.shape
    last = n_tiles - 1
    grid_spec = pltpu.PrefetchScalarGridSpec(
        num_scalar_prefetch=2,
        grid=(n_tiles,),
        in_specs=[pl.BlockSpec((1, 1, tm), lambda i, tg, ok: (i, 0, 0), memory_space=pltpu.SMEM),
                  pl.BlockSpec((1, 1, tm), lambda i, tg, ok: (jnp.minimum(i + 1, last), 0, 0), memory_space=pltpu.SMEM),
                  pl.BlockSpec(memory_space=pl.ANY),
                  pl.BlockSpec((1, D_MODEL), lambda i, tg, ok: (0, 0)),
                  pl.BlockSpec((1, EPG, D_MODEL, EXPERT_FF), lambda i, tg, ok: (tg[i], 0, 0, 0)),
                  pl.BlockSpec((1, EPG, D_MODEL, EXPERT_FF), lambda i, tg, ok: (tg[i], 0, 0, 0)),
                  pl.BlockSpec((1, EPG, EXPERT_FF, D_MODEL), lambda i, tg, ok: (tg[i], 0, 0, 0))],
        out_specs=pl.BlockSpec((tm * ROW_SLABS, LANE), lambda i, tg, ok: (i, 0)),
        scratch_shapes=[pltpu.VMEM((2, tm, XT_WIDTH), F32), pltpu.SemaphoreType.DMA((2,)),
                        pltpu.VMEM((tm, D_MODEL), BF16), pltpu.VMEM((tm, LANE), F32), pltpu.SemaphoreType.REGULAR],
    )
    return pl.pallas_call(
        _moe_kernel,
        grid_spec=grid_spec,
        out_shape=jax.ShapeDtypeStruct((n_tiles * tm * ROW_SLABS, LANE), F32),
        compiler_params=pltpu.CompilerParams(dimension_semantics=("arbitrary",), vmem_limit_bytes=VMEM_LIMIT_MOE),
        name="moe",
    )(tile_group, tile_ok, src3, src3, xt, g, wg, wu, wd)


def _final_kernel(pos_ref, posn_ref, y_hbm, h_ref, gf_ref, o_ref, ybuf, sem):
    i = pl.program_id(0)
    slot = i % 2
    tm = h_ref.shape[0]

    def issue(index_ref, dst_slot):
        group = 8

        def body(j, c):
            for k in range(group):
                r = j * group + k
                src_row = pl.multiple_of(index_ref[0, 0, r], ROW_SLABS)
                pltpu.make_async_copy(y_hbm.at[pl.ds(src_row, ROW_SLABS), :],
                                      ybuf.at[dst_slot, pl.ds(r * ROW_SLABS, ROW_SLABS), :],
                                      sem.at[dst_slot]).start(priority=k % 2)
            return c
        lax.fori_loop(0, tm // group, body, 0)

    @pl.when(i == 0)
    def _():
        issue(pos_ref, 0)

    @pl.when(i + 1 < pl.num_programs(0))
    def _():
        issue(posn_ref, 1 - slot)

    pltpu.make_async_copy(ybuf.at[slot], ybuf.at[slot], sem.at[slot]).wait()
    y = jnp.concatenate([ybuf[slot, pl.ds(blk, tm, stride=ROW_SLABS), :] for blk in range(ROW_SLABS)], axis=-1)
    h = h_ref[...] + y
    o_ref[...] = h * lax.rsqrt(jnp.mean(h * h, axis=-1, keepdims=True) + EPS) * gf_ref[...]


def _final(pos3, y_sorted, h, gf):
    n_tiles, _, tm = pos3.shape
    last = n_tiles - 1
    row = lambda i: (i, 0)
    return pl.pallas_call(
        _final_kernel,
        grid=(n_tiles,),
        in_specs=[pl.BlockSpec((1, 1, tm), lambda i: (i, 0, 0), memory_space=pltpu.SMEM),
                  pl.BlockSpec((1, 1, tm), lambda i: (jnp.minimum(i + 1, last), 0, 0), memory_space=pltpu.SMEM),
                  pl.BlockSpec(memory_space=pl.ANY),
                  pl.BlockSpec((tm, D_MODEL), row),
                  pl.BlockSpec((1, D_MODEL), lambda i: (0, 0))],
        out_specs=pl.BlockSpec((tm, D_MODEL), row),
        out_shape=jax.ShapeDtypeStruct((n_tiles * tm, D_MODEL), F32),
        scratch_shapes=[pltpu.VMEM((2, tm * ROW_SLABS, LANE), F32), pltpu.SemaphoreType.DMA((2,))],
        compiler_params=_cparams(("arbitrary",)),
        name="final",
    )(pos3, pos3, y_sorted, h, gf)


def _qa_perm():
    new = np.arange(NSA_HEADS * NSA_HD)
    r, g, d = new // LANE, (new % LANE) // NSA_HD, new % NSA_HD
    return (g * NSA_R + r) * NSA_HD + d


def _block_diag2(w):
    z = jnp.zeros_like(w)
    return jnp.concatenate([jnp.concatenate([w, z], axis=-1), jnp.concatenate([z, w], axis=-1)], axis=-2)


def kernel(x, rel_bias, ln_mix, w_in, cmp_pos_k, cmp_pos_v, cmp_k_w1, cmp_k_w2, cmp_v_w1, cmp_v_w2,
           diff_lq1, diff_lk1, diff_lq2, diff_lk2, diff_subln, w_out, ln_ffn,
           router_group_w, router_group_b, router_expert_w, router_expert_b,
           exp_w_gate, exp_w_up, exp_w_down, ln_final):
    B, S, D = x.shape
    T = B * S
    assert D == D_MODEL and S % TQ == 0 and S >= WINDOW and T % TM_MOE == 0 and T % TM_FINAL == 0
    x2 = x.reshape(T, D)
    perm = _qa_perm()

    w = w_in[0]
    c_kc, c_vc, c_ks, c_vs, c_kw, c_vw, c_gt = 512, 640, 768, 896, 1024, 1152, 1280
    c_qb = c_gt + N_GATE
    c_kb, c_vb = c_qb + 512, c_qb + 1024
    col = lambda c, n=LANE: w[:, c:c + n]
    w_tok = jnp.concatenate([col(c_kc), col(c_vc), col(c_ks), col(c_kw), col(c_kb, 512)], axis=1).astype(BF16)
    w_feat = jnp.concatenate([w[:, perm], col(c_qb, 512), col(c_vs), col(c_vw), col(c_vb, 512), col(c_gt, N_GATE),
                              jnp.zeros((D, GATE_ROWS - N_GATE), F32)], axis=1).T.astype(BF16)
    qa, kc, vc, ks, kw, qb, kb, vst, vwt, vbt, gates_t = _in_proj(x2, ln_mix[0][None, :], w_tok, w_feat)

    w1k = _block_diag2(cmp_k_w1[0].reshape(CMP_LEN, NSA_HD, CMP_HIDDEN)).astype(BF16)
    w1v = _block_diag2(cmp_v_w1[0].reshape(CMP_LEN, NSA_HD, CMP_HIDDEN)).astype(BF16)
    w2k = _block_diag2(cmp_k_w2[0]).astype(BF16)
    w2v = _block_diag2(cmp_v_w2[0]).astype(BF16)
    posk = jnp.tile(cmp_pos_k[0], (1, NSA_G))
    posv = jnp.tile(cmp_pos_v[0], (1, NSA_G))
    kcmp, vcmpt = _compress(kc, vc, posk, posv, w1k, w1v, w2k, w2v, B, S)

    near, bias_c = _bias_tiles(rel_bias.T, S)

    n_sel = S // SEL_BLOCK
    nrow = S // CMP_STRIDE
    c_start = np.arange(nrow) * CMP_STRIDE
    s_start = np.arange(n_sel) * SEL_BLOCK
    ov = ((c_start[None, :] <= s_start[:, None] + SEL_BLOCK - 1)
          & (c_start[None, :] + CMP_LEN - 1 >= s_start[:, None])
          & (np.arange(nrow)[None, :] < nrow - 1)).astype(np.float32)
    o_a = _nsa(qa, gates_t, kcmp, vcmpt, bias_c, ks, vst, kw, vwt, near[:NSA_HEADS], jnp.asarray(ov, BF16), B, S)

    sub = jnp.tile(diff_subln[0], 2)[:, None]
    o_b = _diff(diff_lq1[0][None, :], diff_lk1[0][None, :], diff_lq2[0][None, :], diff_lk2[0][None, :],
                sub, qb, kb, vbt, near[NSA_HEADS:], B, S)

    w_o = jnp.concatenate([w_out[0][:512][perm], w_out[0][512:]], axis=0).astype(BF16)
    n_r = MOE_GROUPS + N_EXPERTS
    wr = jnp.concatenate([router_group_w[0], router_expert_w[0], jnp.zeros((D, LANE - n_r), F32)], axis=1).astype(BF16)
    br = jnp.concatenate([router_group_b[0], router_expert_b[0], jnp.zeros((LANE - n_r,), F32)])[None, :]
    xt, route, counts = _out_proj(x2, o_a, o_b, w_o, ln_ffn[0][None, :], wr, br)

    tm = TM_MOE
    n_tiles = T // tm + MOE_GROUPS
    cnt = counts[0, :MOE_GROUPS].astype(jnp.int32)
    ends = jnp.cumsum((cnt + tm - 1) // tm * tm)
    starts = ends - (cnt + tm - 1) // tm * tm
    pos = starts[route[:, 0].astype(jnp.int32)] + route[:, 1].astype(jnp.int32)
    order = jnp.argsort(pos).astype(jnp.int32)
    row = jnp.arange(n_tiles * tm, dtype=jnp.int32)
    group_of = lambda r: sum((r >= ends[g]).astype(jnp.int32) for g in range(MOE_GROUPS - 1))
    row_group = group_of(row)
    local = row - starts[row_group]
    compact = (jnp.cumsum(cnt) - cnt)[row_group] + local
    src = jnp.where(local < cnt[row_group], order[jnp.clip(compact, 0, T - 1)], 0)
    tile_start = jnp.arange(n_tiles, dtype=jnp.int32) * tm
    tile_group = group_of(tile_start)
    tile_ok = (tile_start < ends[-1]).astype(jnp.int32)

    by_group = lambda a: a[0].astype(BF16).reshape((MOE_GROUPS, EPG) + a.shape[2:])
    y_sorted = _moe(tile_group, tile_ok, src.reshape(n_tiles, 1, tm), xt, ln_ffn[0][None, :],
                    by_group(exp_w_gate), by_group(exp_w_up), by_group(exp_w_down))
    out = _final((pos * ROW_SLABS).reshape(T // TM_FINAL, 1, TM_FINAL), y_sorted, xt, ln_final[None, :])
    return out.reshape(B, S, D)
```

```python
import math

import numpy as np
import jax
import jax.numpy as jnp
from jax import lax
from jax.experimental import pallas as pl
from jax.experimental.pallas import tpu as pltpu

F32 = jnp.float32
BF16 = jnp.bfloat16
NEG = -1e30
EPS = 1e-6
LOG2E = math.log2(math.e)

D_MODEL = 1024
LANE = 128
NSA_HEADS, NSA_G, NSA_R, NSA_HD = 8, 2, 4, 64
CMP_LEN, CMP_STRIDE, CMP_HIDDEN = 32, 16, 128
SEL_BLOCK, SEL_TOPN, SEL_FORCED_LOCAL, WINDOW = 64, 8, 2, 512
DIFF_HEADS, DIFF_HD = 8, 32
REL_BUCKETS, REL_MAX_EXACT, REL_MAX_DIST = 32, 16, 128
N_REL_HEADS = NSA_HEADS + DIFF_HEADS
MOE_GROUPS, EPG, N_EXPERTS, EXPERT_FF = 4, 8, 32, 256
LAMBDA_INIT = 0.8 - 0.6 * math.exp(-0.3 * 0)
N_GATE = NSA_HEADS * 3
GATE_ROWS = 32

TQ = 256
DIFF_PAIRS_PER_STEP = 4
TM_PROJ = 512
TM_MOE = 512
TM_FINAL = 512
ANCHOR_AFTER = (1, 3, 5, 6)
XT_WIDTH = D_MODEL + LANE
ROW_SLABS = D_MODEL // LANE
VMEM_LIMIT = 48 * 1024 * 1024
VMEM_LIMIT_MOE = 56 * 1024 * 1024


def _cparams(sem):
    return pltpu.CompilerParams(dimension_semantics=sem, vmem_limit_bytes=VMEM_LIMIT)


def _dot(a, b):
    return jnp.dot(a, b, preferred_element_type=F32)


def _rmsnorm(x, g):
    return x * lax.rsqrt(jnp.mean(x * x, axis=-1, keepdims=True) + EPS) * g


def _dot_nt(a, b):
    return lax.dot_general(a, b, (((1,), (1,)), ((), ())), preferred_element_type=F32)


def _bucket_thresholds():
    n = np.arange(0, REL_MAX_DIST + 1)
    nf = np.maximum(n, 1).astype(np.float32)
    large = REL_MAX_EXACT + (np.log(nf / np.float32(REL_MAX_EXACT)) / np.float32(math.log(REL_MAX_DIST / REL_MAX_EXACT))
                             * np.float32(REL_BUCKETS - REL_MAX_EXACT)).astype(np.int32)
    large = np.minimum(large, REL_BUCKETS - 1)
    bucket = np.where(n < REL_MAX_EXACT, n, large)
    return [int(np.argmax(bucket >= b)) for b in range(REL_BUCKETS)]


_THR = _bucket_thresholds()


def _inproj_kernel(x_ref, g_ref, w_ref, wt_ref, qa_ref, kc_ref, vc_ref, ks_ref, kw_ref, qb_ref, kb_ref,
                   vst_ref, vwt_ref, vbt_ref, gt_ref):
    x = x_ref[...]
    xn = (x * lax.rsqrt(jnp.mean(x * x, axis=-1, keepdims=True) + EPS) * g_ref[...]).astype(BF16)
    a = _dot(xn, w_ref[:, 0:512])
    kc_ref[...] = a[:, 0:128]
    vc_ref[...] = a[:, 128:256]
    ks_ref[...] = a[:, 256:384].astype(BF16)
    kw_ref[...] = a[:, 384:512].astype(BF16)
    a = _dot(xn, w_ref[:, 512:1024])
    for r in range(4):
        kb_ref[r] = a[:, r * LANE:(r + 1) * LANE].astype(BF16)

    ft = _dot_nt(wt_ref[...], xn)

    def feat(row0, rows=LANE):
        return ft[row0:row0 + rows, :]
    for r in range(4):
        qa_ref[r] = (feat(r * LANE) * (NSA_HD ** -0.5 * LOG2E)).astype(BF16)
        qb_ref[r] = (feat(512 + r * LANE) * (DIFF_HD ** -0.5 * LOG2E)).astype(BF16)
        vbt_ref[r] = feat(1280 + r * LANE).astype(BF16)
    vst_ref[...] = feat(1024).astype(BF16)
    vwt_ref[...] = feat(1152).astype(BF16)
    gt_ref[...] = feat(1792, GATE_ROWS)


def _in_proj(x2, g, w, wt):
    T = x2.shape[0]
    tm = TM_PROJ
    row = lambda i: (i, 0)
    o128b = jax.ShapeDtypeStruct((T, LANE), BF16)
    o128f = jax.ShapeDtypeStruct((T, LANE), F32)
    o4 = jax.ShapeDtypeStruct((4, T, LANE), BF16)
    ot = jax.ShapeDtypeStruct((LANE, T), BF16)
    o4t = jax.ShapeDtypeStruct((4, LANE, T), BF16)
    s128 = pl.BlockSpec((tm, LANE), row)
    s4 = pl.BlockSpec((4, tm, LANE), lambda i: (0, i, 0))
    st = pl.BlockSpec((LANE, tm), lambda i: (0, i))
    s4t = pl.BlockSpec((4, LANE, tm), lambda i: (0, 0, i))
    return pl.pallas_call(
        _inproj_kernel,
        grid=(T // tm,),
        in_specs=[pl.BlockSpec((tm, D_MODEL), row),
                  pl.BlockSpec((1, D_MODEL), lambda i: (0, 0)),
                  pl.BlockSpec(w.shape, lambda i: (0, 0)),
                  pl.BlockSpec(wt.shape, lambda i: (0, 0))],
        out_specs=[s4t, s128, s128, s128, s128, s4t, s4, st, st, s4t, pl.BlockSpec((GATE_ROWS, tm), lambda i: (0, i))],
        out_shape=[o4t, o128f, o128f, o128b, o128b, o4t, o4, ot, ot, o4t, jax.ShapeDtypeStruct((GATE_ROWS, T), F32)],
        compiler_params=_cparams(("parallel",)),
        name="in_proj",
    )(x2, g, w, wt)


def _gelu_tanh(x):
    return 0.5 * x * (1.0 + jnp.tanh(math.sqrt(2.0 / math.pi) * (x + 0.044715 * (x * x * x))))


def _compress_kernel(kc_ref, vc_ref, posk_ref, posv_ref, w1k_ref, w1v_ref, w2k_ref, w2v_ref, ko_ref, vo_ref):
    nrow = kc_ref.shape[0] // CMP_STRIDE
    rid = lax.broadcasted_iota(jnp.int32, (nrow, 1), 0)
    cid = lax.broadcasted_iota(jnp.int32, (1, nrow), 1)
    for src, pos, w1, w2, out, transposed in ((kc_ref, posk_ref, w1k_ref, w2k_ref, ko_ref, False),
                                              (vc_ref, posv_ref, w1v_ref, w2v_ref, vo_ref, True)):
        hid_a = jnp.zeros((nrow, 2 * CMP_HIDDEN), F32)
        hid_b = jnp.zeros((nrow, 2 * CMP_HIDDEN), F32)
        for m in range(CMP_STRIDE):
            y = src[pl.ds(m, nrow, stride=CMP_STRIDE), :]
            hid_a = hid_a + _dot((y + pos[m:m + 1, :]).astype(BF16), w1[m])
            hid_b = hid_b + _dot((y + pos[CMP_STRIDE + m:CMP_STRIDE + m + 1, :]).astype(BF16), w1[CMP_STRIDE + m])
        hid = hid_a + pltpu.roll(hid_b, nrow - 1, 0)
        o = _dot(_gelu_tanh(hid).astype(BF16), w2[...])
        if transposed:
            out[0] = jnp.where(cid < nrow - 1, o.T, 0.0).astype(BF16)
        else:
            out[0] = jnp.where(rid < nrow - 1, o, 0.0).astype(BF16)


def _compress(kc, vc, posk, posv, w1k, w1v, w2k, w2v, B, S):
    nrow = S // CMP_STRIDE
    assert nrow == LANE
    full = lambda a: pl.BlockSpec(a.shape, lambda b: (0,) * a.ndim)
    src = pl.BlockSpec((S, LANE), lambda b: (b, 0))
    osp = pl.BlockSpec((1, nrow, LANE), lambda b: (b, 0, 0))
    osh = jax.ShapeDtypeStruct((B, nrow, LANE), BF16)
    return pl.pallas_call(
        _compress_kernel,
        grid=(B,),
        in_specs=[src, src, full(posk), full(posv), full(w1k), full(w1v), full(w2k), full(w2v)],
        out_specs=[osp, osp],
        out_shape=[osh, osh],
        compiler_params=_cparams(("parallel",)),
        name="compress",
    )(kc, vc, posk, posv, w1k, w1v, w2k, w2v)


def _bias_from_dist(dist, tab_ref, h):
    val = jnp.full(dist.shape, tab_ref[h, 0], F32)
    for b in range(1, REL_BUCKETS):
        val = jnp.where(dist >= _THR[b], tab_ref[h, b], val)
    return val


def _bias_near_kernel(tab_ref, out_ref):
    h = pl.program_id(0)
    j = lax.broadcasted_iota(jnp.int32, (TQ, TQ), 0)
    i = lax.broadcasted_iota(jnp.int32, (TQ, TQ), 1)
    for d in range(2):
        bias = (_bias_from_dist(i - j + d * TQ, tab_ref, h) - tab_ref[h, REL_BUCKETS - 1]) * LOG2E
        out_ref[0, d] = jnp.where(i - j + d * TQ >= 0, bias, NEG)


def _bias_cmp_kernel(tab_ref, out_ref):
    h = pl.program_id(0)
    band = TQ // CMP_STRIDE
    out_ref[0] = jnp.full(out_ref.shape[1:], tab_ref[h, REL_BUCKETS - 1] * LOG2E, F32)
    for qi in range(out_ref.shape[2] // TQ):
        c0 = max(qi - 1, 0) * band
        c = c0 + lax.broadcasted_iota(jnp.int32, (2 * band, TQ), 0)
        t = qi * TQ + lax.broadcasted_iota(jnp.int32, (2 * band, TQ), 1)
        out_ref[0, c0:c0 + 2 * band, qi * TQ:(qi + 1) * TQ] = (
            _bias_from_dist(t - (c * CMP_STRIDE + CMP_LEN - 1), tab_ref, h) * LOG2E)


def _bias_tiles(tab_t, S):
    smem = pl.BlockSpec(memory_space=pltpu.SMEM)
    near = pl.pallas_call(
        _bias_near_kernel,
        grid=(N_REL_HEADS,),
        in_specs=[smem],
        out_specs=pl.BlockSpec((1, 2, TQ, TQ), lambda h: (h, 0, 0, 0)),
        out_shape=jax.ShapeDtypeStruct((N_REL_HEADS, 2, TQ, TQ), F32),
        compiler_params=_cparams(("parallel",)),
        name="bias_near",
    )(tab_t)
    cmp_bias = pl.pallas_call(
        _bias_cmp_kernel,
        grid=(NSA_HEADS,),
        in_specs=[smem],
        out_specs=pl.BlockSpec((1, LANE, S), lambda h: (h, 0, 0)),
        out_shape=jax.ShapeDtypeStruct((NSA_HEADS, LANE, S), F32),
        compiler_params=_cparams(("parallel",)),
        name="bias_cmp",
    )(tab_t)
    return near, cmp_bias


ACC_ROWS = NSA_HD + 16


def _with_ones(vt):
    return jnp.concatenate([vt, jnp.ones((ACC_ROWS - vt.shape[0], vt.shape[1]), BF16)], axis=0)


def _probs(s, m):
    return jnp.exp2((s - m).astype(BF16))


def _flash_first(ss, vts):
    ms = [jnp.max(s, axis=0, keepdims=True) for s in ss]
    ps = [_probs(s, m) for s, m in zip(ss, ms)]
    return tuple((m, _dot(vt, p)) for m, p, vt in zip(ms, ps, vts))


def _flash_update(ss, vts, sts):
    ms = [jnp.maximum(st[0], jnp.max(s, axis=0, keepdims=True)) for s, st in zip(ss, sts)]
    alphas = [jnp.exp2(st[0] - m) for m, st in zip(ms, sts)]
    ps = [_probs(s, m) for s, m in zip(ss, ms)]
    return tuple((m, a * st[1] + _dot(vt, p)) for m, a, st, p, vt in zip(ms, alphas, sts, ps, vts))


def _flash_out(st):
    acc = st[1]
    return acc[0:NSA_HD, :] * (1.0 / acc[NSA_HD:NSA_HD + 1, :])


def _stream_scratch(n):
    return ([pltpu.VMEM((8, TQ), F32), pltpu.VMEM((ACC_ROWS, TQ), F32)] * n
            + [pltpu.VMEM((TQ, TQ), F32)] * (2 * n) + [pltpu.VMEM((TQ, TQ), BF16)] * (2 * n))


def _stream_refs(refs, n):
    st_refs = tuple((refs[2 * c], refs[2 * c + 1]) for c in range(n))
    s, p = refs[2 * n:4 * n], refs[4 * n:6 * n]
    return st_refs, tuple(s[:n]), tuple(s[n:]), tuple(p[:n]), tuple(p[n:])


def _causal_stream(qi, scores, vtile, st_refs, s_a, s_b, p_a, p_b, diag_ready=False):
    n_far = jnp.maximum(qi - 1, 0)
    top = qi - 2

    chains = range(len(st_refs))

    def write(s_refs, kt, near=2):
        for c in chains:
            s_refs[c][...] = scores(jnp.maximum(kt, 0), near, c)

    def pending(p_refs, kt):
        return [_dot(vtile(jnp.maximum(kt, 0), c), p_refs[c][...]) for c in chains]

    def step(s_cur, p_cur, s_next, kt_next, p_prev, kt_prev, anchor=None):
        rescale = []
        for c in chains:
            nxt = scores(jnp.maximum(kt_next, 0), 2, c)
            s_next[c][...] = nxt
            pv = _dot(vtile(jnp.maximum(kt_prev, 0), c), p_prev[c][...])
            st = st_refs[c][0]
            s = s_cur[c][...]
            m_old = st[0:1, :]
            m = jnp.maximum(m_old, jnp.max(s, axis=0, keepdims=True))
            if anchor is not None:
                m = m + anchor
            anchor = nxt[0:1, :] * 0.0
            p_cur[c][...] = _probs(s, m)
            st[0:1, :] = m
            rescale.append((jnp.exp2(m_old - m), pv))
        for c in chains:
            acc = st_refs[c][1]
            acc[...] = rescale[c][0] * (acc[...] + rescale[c][1])
        return anchor

    if not diag_ready:
        write(s_a, qi, 0)
    for c in chains:
        s_b[c][...] = scores(jnp.maximum(qi - 1, 0), 1, c)
        st, acc = st_refs[c]
        s = s_a[c][...]
        m = jnp.max(s, axis=0, keepdims=True)
        p_a[c][...] = _probs(s, m)
        st[0:1, :] = m
        acc[...] = jnp.zeros(acc.shape, F32)
    step(s_b, p_b, s_a, top, p_a, qi)

    def pair(j, c):
        kt = top - 2 * j
        anchor = step(s_a, p_a, s_b, kt - 1, p_b, kt + 1)
        step(s_b, p_b, s_a, kt - 2, p_a, kt, anchor)
        return c
    lax.fori_loop(0, n_far // 2, pair, 0)

    @pl.when(n_far % 2 == 1)
    def _():
        for c, pv in zip(chains, pending(p_b, 1)):
            st, acc = st_refs[c]
            s = s_a[c][...]
            m_old = st[0:1, :]
            m = jnp.maximum(m_old, jnp.max(s, axis=0, keepdims=True))
            st[0:1, :] = m
            acc[...] = jnp.exp2(m_old - m) * (acc[...] + pv) + _dot(vtile(0, c), _probs(s, m))

    @pl.when(n_far % 2 == 0)
    def _():
        last = jnp.where(n_far == 0, qi - 1, 0)
        for c, pv in zip(chains, pending(p_b, last)):
            acc = st_refs[c][1]
            acc[...] = acc[...] + pv
    return tuple((st[0:1, :], acc[...]) for st, acc in st_refs)


def _ktile(ref, kt):
    return ref[pl.ds(pl.multiple_of(kt * TQ, TQ), TQ), :]


def _nsa_kernel(qa_ref, gt_ref, kcmp_ref, vcmpt_ref, bc_ref, ks_ref, vst_ref, kw_ref, vwt_ref, dn_ref,
                ov_ref, o_ref, psum_ref, sel_ref, oacc_ref, sig_ref, *scratch):
    st_refs, s_a, s_b, p_a, p_b = _stream_refs(scratch, NSA_HEADS)
    qi = pl.program_id(1)
    t0 = qi * TQ
    sub_grp = lax.shift_right_arithmetic(lax.broadcasted_iota(jnp.int32, (LANE, 1), 0), 6)
    sig_ref[...] = jax.nn.sigmoid(gt_ref[...])

    def gate_row(c):
        return sig_ref[pl.ds(c, 1), :]

    def masked_q(r, g):
        return jnp.where(sub_grp == g, qa_ref[r].astype(F32), 0.0).astype(BF16)

    n_cmp = kcmp_ref.shape[1] - 1
    crow = lax.broadcasted_iota(jnp.int32, (LANE, 1), 0)
    cmp_end = jnp.where(crow < n_cmp, crow * CMP_STRIDE + CMP_LEN - 1, 1 << 30)
    mask_c = (t0 + lax.broadcasted_iota(jnp.int32, (1, TQ), 1)) >= cmp_end

    heads = [(r, g) for r in range(NSA_R) for g in range(NSA_G)]
    scores_c = [_dot(kcmp_ref[0], masked_q(r, g)) for r, g in heads]
    probs_c = []
    for (r, g), s in zip(heads, scores_c):
        s = jnp.where(mask_c, s + bc_ref[g * NSA_R + r], NEG)
        p = jnp.where(mask_c, jnp.exp2(s - jnp.max(s, axis=0, keepdims=True)), 0.0)
        l = jnp.sum(p, axis=0, keepdims=True)
        probs_c.append(p * jnp.where(l > 0.0, 1.0 / l, 0.0))
    for g in range(NSA_G):
        psum_ref[g] = sum(p for (r, gg), p in zip(heads, probs_c) if gg == g)
    outs_c = [_dot(vcmpt_ref[0], p.astype(BF16)) for p in probs_c]
    for r in range(NSA_R):
        o0, o1 = outs_c[r * NSA_G], outs_c[r * NSA_G + 1]
        oacc_ref[r] = jnp.where(sub_grp == 0, gate_row(r * 3) * o0, gate_row((NSA_R + r) * 3) * o1)

    has_prev = jnp.where(qi >= 1, 0.0, NEG)
    has_wfar = jnp.where(qi >= WINDOW // TQ, 0.0, NEG)
    kt_prev = jnp.maximum(qi - 1, 0)
    kt_wfar = jnp.maximum(qi - WINDOW // TQ, 0)
    wfar_mask = lax.broadcasted_iota(jnp.int32, (TQ, TQ), 0) > lax.broadcasted_iota(jnp.int32, (TQ, TQ), 1)

    def attend_all_heads():
        chains = [(r, g) for r in range(NSA_R) for g in range(NSA_G)]
        qms = [masked_q(r, g) for r, g in chains]
        hids = [g * NSA_R + r for r, g in chains]

        per_tile = TQ // SEL_BLOCK
        key_blk = lax.shift_right_arithmetic(lax.broadcasted_iota(jnp.int32, (TQ, LANE), 0), 6)
        lane_id = lax.broadcasted_iota(jnp.int32, (TQ, LANE), 1)

        def keys_with_block_onehot(kt, g):
            other = (1 - g) * NSA_HD
            onehot = jnp.where(lane_id - other == key_blk, 1.0, 0.0).astype(BF16)
            in_other = (lane_id >= other) & (lane_id < other + NSA_HD)
            return jnp.where(in_other, onehot, _ktile(ks_ref, kt))

        def query_with_mask_rows(kt, c):
            r, g = chains[c]
            own = qa_ref[r, g * NSA_HD:(g + 1) * NSA_HD, :]
            pair = sel_ref[g, pl.ds(pl.multiple_of((kt // 2) * 2 * per_tile, 2 * per_tile), 2 * per_tile), :]
            rows = jnp.where(kt % 2 == 0, pair[0:per_tile], pair[per_tile:2 * per_tile]).astype(BF16)
            other = jnp.concatenate([rows, jnp.zeros((NSA_HD - per_tile, TQ), BF16)], axis=0)
            return jnp.concatenate([own, other] if g == 0 else [other, own], axis=0)

        def near_bias(s, near, c):
            if near == 0:
                return s + dn_ref[hids[c], 0]
            if near == 1:
                return s + dn_ref[hids[c], 1] + has_prev
            return s

        def group_values(ref, kt, c):
            g = chains[c][1]
            return _with_ones(ref[g * NSA_HD:(g + 1) * NSA_HD, pl.ds(pl.multiple_of(kt * TQ, TQ), TQ)])

        def finish(sts, branch):
            for c, (r, g) in enumerate(chains):
                oacc_ref[r, g * NSA_HD:(g + 1) * NSA_HD, :] += gate_row(hids[c] * 3 + branch) * _flash_out(sts[c])

        def slc_scores(kt, near, c):
            return near_bias(_dot(keys_with_block_onehot(kt, chains[c][1]), query_with_mask_rows(kt, c)), near, c)

        def slc_vtile(kt, c):
            return group_values(vst_ref, kt, c)

        win_tiles = ((qi, 0), (kt_prev, 1), (kt_wfar, 2))

        def win_scores(c):
            out = []
            for kt, near in win_tiles:
                s = near_bias(_dot(_ktile(kw_ref, kt), qms[c]), near, c)
                out.append(jnp.where(wfar_mask, s + has_wfar, NEG) if near == 2 else s)
            return out

        sts = []
        ahead = win_scores(0)
        for c in range(len(chains)):
            tiles, ahead = ahead, (win_scores(c + 1) if c + 1 < len(chains) else None)
            st = _flash_first([tiles[0]], [group_values(vwt_ref, qi, c)])
            st = _flash_update([tiles[1]], [group_values(vwt_ref, kt_prev, c)], st)
            sts.append(_flash_update([tiles[2]], [group_values(vwt_ref, kt_wfar, c)], st)[0])
        finish(sts, 2)

        n_sel = ov_ref.shape[0]
        jj = lax.broadcasted_iota(jnp.int32, (n_sel, TQ), 0)
        cur = lax.shift_right_arithmetic(t0 + lax.broadcasted_iota(jnp.int32, (n_sel, TQ), 1), 6)
        valid = jj <= cur
        forced = (jj == 0) | (cur - jj < SEL_FORCED_LOCAL)
        for g in range(NSA_G):
            ps = psum_ref[g]
            hi = ps.astype(BF16)
            rem = ps - hi.astype(F32)
            mid = rem.astype(BF16)
            lo = (rem - mid.astype(F32)).astype(BF16)
            ov = ov_ref[...]
            imp = _dot(ov, hi) + _dot(ov, mid) + _dot(ov, lo)
            score = jnp.where(valid, jnp.where(forced, 1e9, imp), -1e9)
            jf = jj.astype(F32)
            rest = score
            chosen = jnp.zeros((n_sel, TQ), F32)
            for _ in range(min(SEL_TOPN, n_sel)):
                best = jnp.max(rest, axis=0, keepdims=True)
                first = jnp.min(jnp.where(rest == best, jf, float(n_sel)), axis=0, keepdims=True)
                hit = jf == first
                chosen = jnp.where(hit, 1.0, chosen)
                rest = jnp.where(hit, -3e38, rest)
            sel_ref[g] = jnp.where(chosen > 0.5, jnp.where(score > -1e8, 0.0, NEG), NEG)

        finish(_causal_stream(qi, slc_scores, slc_vtile, st_refs, s_a, s_b, p_a, p_b), 1)
    attend_all_heads()

    for r in range(NSA_R):
        o_ref[r] = oacc_ref[r].T.astype(BF16)


def _nsa(qa, gates_t, kcmp, vcmpt, bias_c, ks, vst, kw, vwt, near_a, ov, B, S):
    nq = S // TQ
    T = B * S
    k_spec = pl.BlockSpec((S, LANE), lambda b, q: (b, 0))
    v_spec = pl.BlockSpec((LANE, S), lambda b, q: (0, b))
    full = lambda a: pl.BlockSpec(a.shape, lambda b, q: (0,) * a.ndim)
    cmp_spec = pl.BlockSpec((1, LANE, LANE), lambda b, q: (b, 0, 0))
    qo = pl.BlockSpec((4, TQ, LANE), lambda b, q: (0, b * nq + q, 0))
    return pl.pallas_call(
        _nsa_kernel,
        grid=(B, nq),
        in_specs=[pl.BlockSpec((4, LANE, TQ), lambda b, q: (0, 0, b * nq + q)),
                  pl.BlockSpec((GATE_ROWS, TQ), lambda b, q: (0, b * nq + q)),
                  cmp_spec, cmp_spec,
                  pl.BlockSpec((NSA_HEADS, LANE, TQ), lambda b, q: (0, 0, q)),
                  k_spec, v_spec, k_spec, v_spec, full(near_a), full(ov)],
        out_specs=qo,
        out_shape=jax.ShapeDtypeStruct((4, T, LANE), BF16),
        scratch_shapes=[pltpu.VMEM((NSA_G, LANE, TQ), F32), pltpu.VMEM((NSA_G, S // SEL_BLOCK, TQ), F32),
                        pltpu.VMEM((NSA_R, LANE, TQ), F32), pltpu.VMEM((GATE_ROWS, TQ), F32)]
                       + _stream_scratch(NSA_HEADS),
        compiler_params=_cparams(("parallel", "arbitrary")),
        name="nsa",
    )(qa, gates_t, kcmp, vcmpt, bias_c, ks, vst, kw, vwt, near_a, ov)


def _diff_kernel(lq1_ref, lk1_ref, lq2_ref, lk2_ref, sub_ref, qb_ref, kb_ref, vbt_ref, dn_ref, o_ref, *scratch):
    n = DIFF_PAIRS_PER_STEP * 4
    st_refs, s_a, s_b, p_a, p_b = _stream_refs(scratch, n)
    qi = pl.program_id(1)
    chain_of_row = lax.shift_right_arithmetic(lax.broadcasted_iota(jnp.int32, (LANE, 1), 0), 5)
    lam = (jnp.exp(jnp.sum(lq1_ref[...] * lk1_ref[...], axis=-1, keepdims=True))
           - jnp.exp(jnp.sum(lq2_ref[...] * lk2_ref[...], axis=-1, keepdims=True)) + LAMBDA_INIT)
    has_prev = jnp.where(qi >= 1, 0.0, NEG)
    n_steps = DIFF_HEADS // 2 // DIFF_PAIRS_PER_STEP

    def score_fn(step):
        pairs = [step * DIFF_PAIRS_PER_STEP + c // 4 for c in range(n)]
        qs = [qb_ref[step * DIFF_PAIRS_PER_STEP + i].astype(F32) for i in range(DIFF_PAIRS_PER_STEP)]
        qms = [jnp.where(chain_of_row == c % 4, qs[c // 4], 0.0).astype(BF16) for c in range(n)]

        def scores(kt, near, c):
            s = _dot(kb_ref[pairs[c], pl.ds(pl.multiple_of(kt * TQ, TQ), TQ), :], qms[c])
            if near == 2:
                return s
            bias = dn_ref[2 * pairs[c] + (c % 4) // 2, near]
            return s + bias if near == 0 else s + bias + has_prev
        return scores

    def start_diag(step):
        scores = score_fn(step)
        for c in range(n):
            s_a[c][...] = scores(qi, 0, c)

    start_diag(0)

    def step_body(step, carry):
        def vtile(kt, c):
            hh = (c % 4) // 2
            return _with_ones(vbt_ref[step * DIFF_PAIRS_PER_STEP + c // 4, hh * 2 * DIFF_HD:(hh + 1) * 2 * DIFF_HD,
                                      pl.ds(pl.multiple_of(kt * TQ, TQ), TQ)])

        sts = _causal_stream(qi, score_fn(step), vtile, st_refs, s_a, s_b, p_a, p_b, diag_ready=True)
        start_diag(jnp.minimum(step + 1, n_steps - 1))
        for i in range(DIFF_PAIRS_PER_STEP):
            outs = []
            for hh in range(2):
                c = 4 * i + 2 * hh
                o = _flash_out(sts[c]) - lam * _flash_out(sts[c + 1])
                outs.append(o * lax.rsqrt(jnp.mean(o * o, axis=0, keepdims=True) + EPS))
            out = jnp.concatenate(outs, axis=0) * sub_ref[...] * (1.0 - LAMBDA_INIT)
            o_ref[step * DIFF_PAIRS_PER_STEP + i] = out.T.astype(BF16)
        return carry
    lax.fori_loop(0, n_steps, step_body, 0)


def _diff(lq1, lk1, lq2, lk2, sub, qb, kb, vbt, near_b, B, S):
    nq = S // TQ
    T = B * S
    full = lambda a: pl.BlockSpec(a.shape, lambda b, q: (0,) * a.ndim)
    k_spec = pl.BlockSpec((4, S, LANE), lambda b, q: (0, b, 0))
    v_spec = pl.BlockSpec((4, LANE, S), lambda b, q: (0, 0, b))
    qo = pl.BlockSpec((4, TQ, LANE), lambda b, q: (0, b * nq + q, 0))
    return pl.pallas_call(
        _diff_kernel,
        grid=(B, nq),
        in_specs=[full(lq1), full(lk1), full(lq2), full(lk2), full(sub),
                  pl.BlockSpec((4, LANE, TQ), lambda b, q: (0, 0, b * nq + q)), k_spec, v_spec, full(near_b)],
        out_specs=qo,
        out_shape=jax.ShapeDtypeStruct((4, T, LANE), BF16),
        scratch_shapes=_stream_scratch(DIFF_PAIRS_PER_STEP * 4),
        compiler_params=_cparams(("parallel", "arbitrary")),
        name="diff",
    )(lq1, lk1, lq2, lk2, sub, qb, kb, vbt, near_b)


def _outproj_kernel(x_ref, oa_ref, ob_ref, w_ref, g_ref, wr_ref, br_ref, xt_ref, route_ref, cnt_ref, carry_ref):
    @pl.when(pl.program_id(0) == 0)
    def _():
        carry_ref[...] = jnp.zeros(carry_ref.shape, F32)

    o = jnp.concatenate([oa_ref[r] for r in range(4)] + [ob_ref[r] for r in range(4)], axis=-1)
    h = x_ref[...] + _dot(o, w_ref[...])
    xt_ref[:, 0:D_MODEL] = h
    tn = _rmsnorm(h, g_ref[...]).astype(BF16)
    logits = _dot(tn, wr_ref[...]) + br_ref[...]
    lane = lax.broadcasted_iota(jnp.int32, (1, LANE), 1)
    lane_f = lane.astype(F32)
    is_grp = lane < MOE_GROUPS
    lg = jnp.where(is_grp, logits, NEG)
    mg = jnp.max(lg, axis=-1, keepdims=True)
    zg = jnp.sum(jnp.where(is_grp, jnp.exp(lg - mg), 0.0), axis=-1, keepdims=True)
    g_prob = 1.0 / zg
    g_idx = jnp.min(jnp.where(lg == mg, lane_f, 1e9), axis=-1, keepdims=True)
    lane_grp = jnp.where((lane >= MOE_GROUPS) & (lane < MOE_GROUPS + N_EXPERTS),
                         lax.shift_right_arithmetic(lane - MOE_GROUPS, 3), -1).astype(F32)
    le = jnp.where(lane_grp == g_idx, logits, NEG)
    m1 = jnp.max(le, axis=-1, keepdims=True)
    e1 = jnp.min(jnp.where(le == m1, lane_f, 1e9), axis=-1, keepdims=True)
    le2 = jnp.where(lane_f == e1, NEG, le)
    m2 = jnp.max(le2, axis=-1, keepdims=True)
    e2 = jnp.min(jnp.where(le2 == m2, lane_f, 1e9), axis=-1, keepdims=True)
    ratio = jnp.exp(m2 - m1)
    w1 = g_prob / (1.0 + ratio)
    w2 = w1 * ratio
    xt_ref[:, D_MODEL:D_MODEL + LANE] = jnp.where(lane_f == e1, w1, 0.0) + jnp.where(lane_f == e2, w2, 0.0)

    tm = h.shape[0]
    onehot = jnp.where(lane_f == g_idx, 1.0, 0.0)
    earlier = jnp.where(lax.broadcasted_iota(jnp.int32, (tm, tm), 0) > lax.broadcasted_iota(jnp.int32, (tm, tm), 1),
                        1.0, 0.0).astype(BF16)
    prefix = _dot(earlier, onehot.astype(BF16)) + carry_ref[...]
    rank = jnp.sum(onehot * prefix, axis=-1, keepdims=True)
    carry_ref[...] += jnp.sum(onehot, axis=0, keepdims=True)
    route_ref[...] = jnp.where(lane == 0, g_idx, jnp.where(lane == 1, rank, 0.0))
    cnt_ref[...] = jnp.broadcast_to(carry_ref[...], cnt_ref.shape)


def _out_proj(x2, oa, ob, w, g, wr, br):
    T = x2.shape[0]
    tm = TM_PROJ
    row = lambda i: (i, 0)
    full = lambda a: pl.BlockSpec(a.shape, lambda i: (0,) * a.ndim)
    o4 = pl.BlockSpec((4, tm, LANE), lambda i: (0, i, 0))
    return pl.pallas_call(
        _outproj_kernel,
        grid=(T // tm,),
        in_specs=[pl.BlockSpec((tm, D_MODEL), row), o4, o4, full(w), full(g), full(wr), full(br)],
        out_specs=[pl.BlockSpec((tm, XT_WIDTH), row), pl.BlockSpec((tm, LANE), row),
                   pl.BlockSpec((8, LANE), lambda i: (0, 0))],
        out_shape=[jax.ShapeDtypeStruct((T, XT_WIDTH), F32), jax.ShapeDtypeStruct((T, LANE), F32),
                   jax.ShapeDtypeStruct((8, LANE), F32)],
        scratch_shapes=[pltpu.VMEM((1, LANE), F32)],
        compiler_params=_cparams(("arbitrary",)),
        name="out_proj",
    )(x2, oa, ob, w, g, wr, br)


def _moe_kernel(tg_ref, ok_ref, idx_ref, idxn_ref, xt_hbm, g_ref, wg_ref, wu_ref, wd_ref, y_ref, xbuf, sem,
                xb_ref, cmb_ref, fence_sem):
    i = pl.program_id(0)
    slot = i % 2
    tm = xbuf.shape[1]

    def row_copy(index_ref, r, dst_slot):
        return pltpu.make_async_copy(xt_hbm.at[pl.ds(index_ref[0, 0, r], 1), :],
                                     xbuf.at[dst_slot, pl.ds(r, 1), :], sem.at[dst_slot])

    @pl.when(i == 0)
    def _():
        def body(r, c):
            row_copy(idx_ref, r, 0).start()
            return c
        lax.fori_loop(0, tm, body, 0)

    @pl.when((i == 0) | (ok_ref[jnp.maximum(i - 1, 0)] == 1))
    def _():
        pltpu.make_async_copy(xt_hbm.at[pl.ds(0, tm), :], xbuf.at[slot], sem.at[slot]).wait()

    @pl.when(ok_ref[i] == 0)
    def _():
        y_ref[...] = jnp.zeros(y_ref.shape, F32)

    @pl.when(ok_ref[i] == 1)
    def _():
        lane = lax.broadcasted_iota(jnp.int32, (1, LANE), 1)
        xb_ref[...] = _rmsnorm(xbuf[slot, :, 0:D_MODEL], g_ref[...]).astype(BF16)
        cmb_ref[...] = xbuf[slot, :, D_MODEL:D_MODEL + LANE]
        first_lane = MOE_GROUPS + tg_ref[i] * EPG
        per_expert = -(-tm // (EPG - 1))
        y = jnp.zeros((tm, D_MODEL), F32)
        issued = jnp.int32(0)
        for e in range(EPG):
            x = xb_ref[...]
            a = _dot(x, wg_ref[0, e])
            b = _dot(x, wu_ref[0, e])
            ce = jnp.sum(jnp.where(lane == first_lane + e + issued, cmb_ref[...], 0.0), axis=-1, keepdims=True)
            y = y + _dot(((a * jax.nn.sigmoid(a)) * b * ce).astype(BF16), wd_ref[0, e])
            for r in range(e * per_expert, min((e + 1) * per_expert, tm)):
                row_copy(idxn_ref, r, 1 - slot).start(priority=r % 2)
            if e in ANCHOR_AFTER:
                issued = pl.semaphore_read(fence_sem)
        for blk in range(ROW_SLABS):
            y_ref[pl.ds(blk, tm, stride=ROW_SLABS), :] = y[:, blk * LANE:(blk + 1) * LANE]


def _moe(tile_group, tile_ok, src3, xt, g, wg, wu, wd):
    n_tiles, _, tm = src3.shape
    last = n_tiles - 1
    grid_spec = pltpu.PrefetchScalarGridSpec(
        num_scalar_prefetch=2,
        grid=(n_tiles,),
        in_specs=[pl.BlockSpec((1, 1, tm), lambda i, tg, ok: (i, 0, 0), memory_space=pltpu.SMEM),
                  pl.BlockSpec((1, 1, tm), lambda i, tg, ok: (jnp.minimum(i + 1, last), 0, 0), memory_space=pltpu.SMEM),
                  pl.BlockSpec(memory_space=pl.ANY),
                  pl.BlockSpec((1, D_MODEL), lambda i, tg, ok: (0, 0)),
                  pl.BlockSpec((1, EPG, D_MODEL, EXPERT_FF), lambda i, tg, ok: (tg[i], 0, 0, 0)),
                  pl.BlockSpec((1, EPG, D_MODEL, EXPERT_FF), lambda i, tg, ok: (tg[i], 0, 0, 0)),
                  pl.BlockSpec((1, EPG, EXPERT_FF, D_MODEL), lambda i, tg, ok: (tg[i], 0, 0, 0))],
        out_specs=pl.BlockSpec((tm * ROW_SLABS, LANE), lambda i, tg, ok: (i, 0)),
        scratch_shapes=[pltpu.VMEM((2, tm, XT_WIDTH), F32), pltpu.SemaphoreType.DMA((2,)),
                        pltpu.VMEM((tm, D_MODEL), BF16), pltpu.VMEM((tm, LANE), F32), pltpu.SemaphoreType.REGULAR],
    )
    return pl.pallas_call(
        _moe_kernel,
        grid_spec=grid_spec,
        out_shape=jax.ShapeDtypeStruct((n_tiles * tm * ROW_SLABS, LANE), F32),
        compiler_params=pltpu.CompilerParams(dimension_semantics=("arbitrary",), vmem_limit_bytes=VMEM_LIMIT_MOE),
        name="moe",
    )(tile_group, tile_ok, src3, src3, xt, g, wg, wu, wd)


def _final_kernel(pos_ref, posn_ref, y_hbm, h_ref, gf_ref, o_ref, ybuf, sem):
    i = pl.program_id(0)
    slot = i % 2
    tm = h_ref.shape[0]

    def issue(index_ref, dst_slot):
        group = 8

        def body(j, c):
            for k in range(group):
                r = j * group + k
                src_row = pl.multiple_of(index_ref[0, 0, r], ROW_SLABS)
                pltpu.make_async_copy(y_hbm.at[pl.ds(src_row, ROW_SLABS), :],
                                      ybuf.at[dst_slot, pl.ds(r * ROW_SLABS, ROW_SLABS), :],
                                      sem.at[dst_slot]).start(priority=k % 2)
            return c
        lax.fori_loop(0, tm // group, body, 0)

    @pl.when(i == 0)
    def _():
        issue(pos_ref, 0)

    @pl.when(i + 1 < pl.num_programs(0))
    def _():
        issue(posn_ref, 1 - slot)

    pltpu.make_async_copy(ybuf.at[slot], ybuf.at[slot], sem.at[slot]).wait()
    y = jnp.concatenate([ybuf[slot, pl.ds(blk, tm, stride=ROW_SLABS), :] for blk in range(ROW_SLABS)], axis=-1)
    h = h_ref[...] + y
    o_ref[...] = h * lax.rsqrt(jnp.mean(h * h, axis=-1, keepdims=True) + EPS) * gf_ref[...]


def _final(pos3, y_sorted, h, gf):
    n_tiles, _, tm = pos3.shape
    last = n_tiles - 1
    row = lambda i: (i, 0)
    return pl.pallas_call(
        _final_kernel,
        grid=(n_tiles,),
        in_specs=[pl.BlockSpec((1, 1, tm), lambda i: (i, 0, 0), memory_space=pltpu.SMEM),
                  pl.BlockSpec((1, 1, tm), lambda i: (jnp.minimum(i + 1, last), 0, 0), memory_space=pltpu.SMEM),
                  pl.BlockSpec(memory_space=pl.ANY),
                  pl.BlockSpec((tm, D_MODEL), row),
                  pl.BlockSpec((1, D_MODEL), lambda i: (0, 0))],
        out_specs=pl.BlockSpec((tm, D_MODEL), row),
        out_shape=jax.ShapeDtypeStruct((n_tiles * tm, D_MODEL), F32),
        scratch_shapes=[pltpu.VMEM((2, tm * ROW_SLABS, LANE), F32), pltpu.SemaphoreType.DMA((2,))],
        compiler_params=_cparams(("arbitrary",)),
        name="final",
    )(pos3, pos3, y_sorted, h, gf)


def _qa_perm():
    new = np.arange(NSA_HEADS * NSA_HD)
    r, g, d = new // LANE, (new % LANE) // NSA_HD, new % NSA_HD
    return (g * NSA_R + r) * NSA_HD + d


def _block_diag2(w):
    z = jnp.zeros_like(w)
    return jnp.concatenate([jnp.concatenate([w, z], axis=-1), jnp.concatenate([z, w], axis=-1)], axis=-2)


def kernel(x, rel_bias, ln_mix, w_in, cmp_pos_k, cmp_pos_v, cmp_k_w1, cmp_k_w2, cmp_v_w1, cmp_v_w2,
           diff_lq1, diff_lk1, diff_lq2, diff_lk2, diff_subln, w_out, ln_ffn,
           router_group_w, router_group_b, router_expert_w, router_expert_b,
           exp_w_gate, exp_w_up, exp_w_down, ln_final):
    B, S, D = x.shape
    T = B * S
    assert D == D_MODEL and S % TQ == 0 and S >= WINDOW and T % TM_MOE == 0 and T % TM_FINAL == 0
    x2 = x.reshape(T, D)
    perm = _qa_perm()

    w = w_in[0]
    c_kc, c_vc, c_ks, c_vs, c_kw, c_vw, c_gt = 512, 640, 768, 896, 1024, 1152, 1280
    c_qb = c_gt + N_GATE
    c_kb, c_vb = c_qb + 512, c_qb + 1024
    col = lambda c, n=LANE: w[:, c:c + n]
    w_tok = jnp.concatenate([col(c_kc), col(c_vc), col(c_ks), col(c_kw), col(c_kb, 512)], axis=1).astype(BF16)
    w_feat = jnp.concatenate([w[:, perm], col(c_qb, 512), col(c_vs), col(c_vw), col(c_vb, 512), col(c_gt, N_GATE),
                              jnp.zeros((D, GATE_ROWS - N_GATE), F32)], axis=1).T.astype(BF16)
    qa, kc, vc, ks, kw, qb, kb, vst, vwt, vbt, gates_t = _in_proj(x2, ln_mix[0][None, :], w_tok, w_feat)

    w1k = _block_diag2(cmp_k_w1[0].reshape(CMP_LEN, NSA_HD, CMP_HIDDEN)).astype(BF16)
    w1v = _block_diag2(cmp_v_w1[0].reshape(CMP_LEN, NSA_HD, CMP_HIDDEN)).astype(BF16)
    w2k = _block_diag2(cmp_k_w2[0]).astype(BF16)
    w2v = _block_diag2(cmp_v_w2[0]).astype(BF16)
    posk = jnp.tile(cmp_pos_k[0], (1, NSA_G))
    posv = jnp.tile(cmp_pos_v[0], (1, NSA_G))
    kcmp, vcmpt = _compress(kc, vc, posk, posv, w1k, w1v, w2k, w2v, B, S)

    near, bias_c = _bias_tiles(rel_bias.T, S)

    n_sel = S // SEL_BLOCK
    nrow = S // CMP_STRIDE
    c_start = np.arange(nrow) * CMP_STRIDE
    s_start = np.arange(n_sel) * SEL_BLOCK
    ov = ((c_start[None, :] <= s_start[:, None] + SEL_BLOCK - 1)
          & (c_start[None, :] + CMP_LEN - 1 >= s_start[:, None])
          & (np.arange(nrow)[None, :] < nrow - 1)).astype(np.float32)
    o_a = _nsa(qa, gates_t, kcmp, vcmpt, bias_c, ks, vst, kw, vwt, near[:NSA_HEADS], jnp.asarray(ov, BF16), B, S)

    sub = jnp.tile(diff_subln[0], 2)[:, None]
    o_b = _diff(diff_lq1[0][None, :], diff_lk1[0][None, :], diff_lq2[0][None, :], diff_lk2[0][None, :],
                sub, qb, kb, vbt, near[NSA_HEADS:], B, S)

    w_o = jnp.concatenate([w_out[0][:512][perm], w_out[0][512:]], axis=0).astype(BF16)
    n_r = MOE_GROUPS + N_EXPERTS
    wr = jnp.concatenate([router_group_w[0], router_expert_w[0], jnp.zeros((D, LANE - n_r), F32)], axis=1).astype(BF16)
    br = jnp.concatenate([router_group_b[0], router_expert_b[0], jnp.zeros((LANE - n_r,), F32)])[None, :]
    xt, route, counts = _out_proj(x2, o_a, o_b, w_o, ln_ffn[0][None, :], wr, br)

    tm = TM_MOE
    n_tiles = T // tm + MOE_GROUPS
    cnt = counts[0, :MOE_GROUPS].astype(jnp.int32)
    ends = jnp.cumsum((cnt + tm - 1) // tm * tm)
    starts = ends - (cnt + tm - 1) // tm * tm
    pos = starts[route[:, 0].astype(jnp.int32)] + route[:, 1].astype(jnp.int32)
    order = jnp.argsort(pos).astype(jnp.int32)
    row = jnp.arange(n_tiles * tm, dtype=jnp.int32)
    group_of = lambda r: sum((r >= ends[g]).astype(jnp.int32) for g in range(MOE_GROUPS - 1))
    row_group = group_of(row)
    local = row - starts[row_group]
    compact = (jnp.cumsum(cnt) - cnt)[row_group] + local
    src = jnp.where(local < cnt[row_group], order[jnp.clip(compact, 0, T - 1)], 0)
    tile_start = jnp.arange(n_tiles, dtype=jnp.int32) * tm
    tile_group = group_of(tile_start)
    tile_ok = (tile_start < ends[-1]).astype(jnp.int32)

    by_group = lambda a: a[0].astype(BF16).reshape((MOE_GROUPS, EPG) + a.shape[2:])
    y_sorted = _moe(tile_group, tile_ok, src.reshape(n_tiles, 1, tm), xt, ln_ffn[0][None, :],
                    by_group(exp_w_gate), by_group(exp_w_up), by_group(exp_w_down))
    out = _final((pos * ROW_SLABS).reshape(T // TM_FINAL, 1, TM_FINAL), y_sorted, xt, ln_final[None, :])
    return out.reshape(B, S, D)
```

```python
import math

import numpy as np
import jax
import jax.numpy as jnp
from jax import lax
from jax.experimental import pallas as pl
from jax.experimental.pallas import tpu as pltpu

F32 = jnp.float32
BF16 = jnp.bfloat16
NEG = -1e30
EPS = 1e-6
LOG2E = math.log2(math.e)

D_MODEL = 1024
LANE = 128
NSA_HEADS, NSA_G, NSA_R, NSA_HD = 8, 2, 4, 64
CMP_LEN, CMP_STRIDE, CMP_HIDDEN = 32, 16, 128
SEL_BLOCK, SEL_TOPN, SEL_FORCED_LOCAL, WINDOW = 64, 8, 2, 512
DIFF_HEADS, DIFF_HD = 8, 32
REL_BUCKETS, REL_MAX_EXACT, REL_MAX_DIST = 32, 16, 128
N_REL_HEADS = NSA_HEADS + DIFF_HEADS
MOE_GROUPS, EPG, N_EXPERTS, EXPERT_FF = 4, 8, 32, 256
LAMBDA_INIT = 0.8 - 0.6 * math.exp(-0.3 * 0)
N_GATE = NSA_HEADS * 3
GATE_ROWS = 32

TQ = 256
DIFF_PAIRS_PER_STEP = 4
TM_PROJ = 512
TM_MOE = 512
TM_FINAL = 512
ANCHOR_AFTER = (1, 3, 5, 6)
XT_WIDTH = D_MODEL + LANE
ROW_SLABS = D_MODEL // LANE
VMEM_LIMIT = 48 * 1024 * 1024
VMEM_LIMIT_MOE = 56 * 1024 * 1024


def _cparams(sem):
    return pltpu.CompilerParams(dimension_semantics=sem, vmem_limit_bytes=VMEM_LIMIT)


def _dot(a, b):
    return jnp.dot(a, b, preferred_element_type=F32)


def _rmsnorm(x, g):
    return x * lax.rsqrt(jnp.mean(x * x, axis=-1, keepdims=True) + EPS) * g


def _dot_nt(a, b):
    return lax.dot_general(a, b, (((1,), (1,)), ((), ())), preferred_element_type=F32)


def _bucket_thresholds():
    n = np.arange(0, REL_MAX_DIST + 1)
    nf = np.maximum(n, 1).astype(np.float32)
    large = REL_MAX_EXACT + (np.log(nf / np.float32(REL_MAX_EXACT)) / np.float32(math.log(REL_MAX_DIST / REL_MAX_EXACT))
                             * np.float32(REL_BUCKETS - REL_MAX_EXACT)).astype(np.int32)
    large = np.minimum(large, REL_BUCKETS - 1)
    bucket = np.where(n < REL_MAX_EXACT, n, large)
    return [int(np.argmax(bucket >= b)) for b in range(REL_BUCKETS)]


_THR = _bucket_thresholds()


def _inproj_kernel(x_ref, g_ref, w_ref, wt_ref, qa_ref, kc_ref, vc_ref, ks_ref, kw_ref, qb_ref, kb_ref,
                   vst_ref, vwt_ref, vbt_ref, gt_ref):
    x = x_ref[...]
    xn = (x * lax.rsqrt(jnp.mean(x * x, axis=-1, keepdims=True) + EPS) * g_ref[...]).astype(BF16)
    a = _dot(xn, w_ref[:, 0:512])
    kc_ref[...] = a[:, 0:128]
    vc_ref[...] = a[:, 128:256]
    ks_ref[...] = a[:, 256:384].astype(BF16)
    kw_ref[...] = a[:, 384:512].astype(BF16)
    a = _dot(xn, w_ref[:, 512:1024])
    for r in range(4):
        kb_ref[r] = a[:, r * LANE:(r + 1) * LANE].astype(BF16)

    ft = _dot_nt(wt_ref[...], xn)

    def feat(row0, rows=LANE):
        return ft[row0:row0 + rows, :]
    for r in range(4):
        qa_ref[r] = (feat(r * LANE) * (NSA_HD ** -0.5 * LOG2E)).astype(BF16)
        qb_ref[r] = (feat(512 + r * LANE) * (DIFF_HD ** -0.5 * LOG2E)).astype(BF16)
        vbt_ref[r] = feat(1280 + r * LANE).astype(BF16)
    vst_ref[...] = feat(1024).astype(BF16)
    vwt_ref[...] = feat(1152).astype(BF16)
    gt_ref[...] = feat(1792, GATE_ROWS)


def _in_proj(x2, g, w, wt):
    T = x2.shape[0]
    tm = TM_PROJ
    row = lambda i: (i, 0)
    o128b = jax.ShapeDtypeStruct((T, LANE), BF16)
    o128f = jax.ShapeDtypeStruct((T, LANE), F32)
    o4 = jax.ShapeDtypeStruct((4, T, LANE), BF16)
    ot = jax.ShapeDtypeStruct((LANE, T), BF16)
    o4t = jax.ShapeDtypeStruct((4, LANE, T), BF16)
    s128 = pl.BlockSpec((tm, LANE), row)
    s4 = pl.BlockSpec((4, tm, LANE), lambda i: (0, i, 0))
    st = pl.BlockSpec((LANE, tm), lambda i: (0, i))
    s4t = pl.BlockSpec((4, LANE, tm), lambda i: (0, 0, i))
    return pl.pallas_call(
        _inproj_kernel,
        grid=(T // tm,),
        in_specs=[pl.BlockSpec((tm, D_MODEL), row),
                  pl.BlockSpec((1, D_MODEL), lambda i: (0, 0)),
                  pl.BlockSpec(w.shape, lambda i: (0, 0)),
                  pl.BlockSpec(wt.shape, lambda i: (0, 0))],
        out_specs=[s4t, s128, s128, s128, s128, s4t, s4, st, st, s4t, pl.BlockSpec((GATE_ROWS, tm), lambda i: (0, i))],
        out_shape=[o4t, o128f, o128f, o128b, o128b, o4t, o4, ot, ot, o4t, jax.ShapeDtypeStruct((GATE_ROWS, T), F32)],
        compiler_params=_cparams(("parallel",)),
        name="in_proj",
    )(x2, g, w, wt)


def _gelu_tanh(x):
    return 0.5 * x * (1.0 + jnp.tanh(math.sqrt(2.0 / math.pi) * (x + 0.044715 * (x * x * x))))


def _compress_kernel(kc_ref, vc_ref, posk_ref, posv_ref, w1k_ref, w1v_ref, w2k_ref, w2v_ref, ko_ref, vo_ref):
    nrow = kc_ref.shape[0] // CMP_STRIDE
    rid = lax.broadcasted_iota(jnp.int32, (nrow, 1), 0)
    cid = lax.broadcasted_iota(jnp.int32, (1, nrow), 1)
    for src, pos, w1, w2, out, transposed in ((kc_ref, posk_ref, w1k_ref, w2k_ref, ko_ref, False),
                                              (vc_ref, posv_ref, w1v_ref, w2v_ref, vo_ref, True)):
        hid_a = jnp.zeros((nrow, 2 * CMP_HIDDEN), F32)
        hid_b = jnp.zeros((nrow, 2 * CMP_HIDDEN), F32)
        for m in range(CMP_STRIDE):
            y = src[pl.ds(m, nrow, stride=CMP_STRIDE), :]
            hid_a = hid_a + _dot((y + pos[m:m + 1, :]).astype(BF16), w1[m])
            hid_b = hid_b + _dot((y + pos[CMP_STRIDE + m:CMP_STRIDE + m + 1, :]).astype(BF16), w1[CMP_STRIDE + m])
        hid = hid_a + pltpu.roll(hid_b, nrow - 1, 0)
        o = _dot(_gelu_tanh(hid).astype(BF16), w2[...])
        if transposed:
            out[0] = jnp.where(cid < nrow - 1, o.T, 0.0).astype(BF16)
        else:
            out[0] = jnp.where(rid < nrow - 1, o, 0.0).astype(BF16)


def _compress(kc, vc, posk, posv, w1k, w1v, w2k, w2v, B, S):
    nrow = S // CMP_STRIDE
    assert nrow == LANE
    full = lambda a: pl.BlockSpec(a.shape, lambda b: (0,) * a.ndim)
    src = pl.BlockSpec((S, LANE), lambda b: (b, 0))
    osp = pl.BlockSpec((1, nrow, LANE), lambda b: (b, 0, 0))
    osh = jax.ShapeDtypeStruct((B, nrow, LANE), BF16)
    return pl.pallas_call(
        _compress_kernel,
        grid=(B,),
        in_specs=[src, src, full(posk), full(posv), full(w1k), full(w1v), full(w2k), full(w2v)],
        out_specs=[osp, osp],
        out_shape=[osh, osh],
        compiler_params=_cparams(("parallel",)),
        name="compress",
    )(kc, vc, posk, posv, w1k, w1v, w2k, w2v)


def _bias_from_dist(dist, tab_ref, h):
    val = jnp.full(dist.shape, tab_ref[h, 0], F32)
    for b in range(1, REL_BUCKETS):
        val = jnp.where(dist >= _THR[b], tab_ref[h, b], val)
    return val


def _bias_near_kernel(tab_ref, out_ref):
    h = pl.program_id(0)
    j = lax.broadcasted_iota(jnp.int32, (TQ, TQ), 0)
    i = lax.broadcasted_iota(jnp.int32, (TQ, TQ), 1)
    for d in range(2):
        bias = (_bias_from_dist(i - j + d * TQ, tab_ref, h) - tab_ref[h, REL_BUCKETS - 1]) * LOG2E
        out_ref[0, d] = jnp.where(i - j + d * TQ >= 0, bias, NEG)


def _bias_cmp_kernel(tab_ref, out_ref):
    h = pl.program_id(0)
    band = TQ // CMP_STRIDE
    out_ref[0] = jnp.full(out_ref.shape[1:], tab_ref[h, REL_BUCKETS - 1] * LOG2E, F32)
    for qi in range(out_ref.shape[2] // TQ):
        c0 = max(qi - 1, 0) * band
        c = c0 + lax.broadcasted_iota(jnp.int32, (2 * band, TQ), 0)
        t = qi * TQ + lax.broadcasted_iota(jnp.int32, (2 * band, TQ), 1)
        out_ref[0, c0:c0 + 2 * band, qi * TQ:(qi + 1) * TQ] = (
            _bias_from_dist(t - (c * CMP_STRIDE + CMP_LEN - 1), tab_ref, h) * LOG2E)


def _bias_tiles(tab_t, S):
    smem = pl.BlockSpec(memory_space=pltpu.SMEM)
    near = pl.pallas_call(
        _bias_near_kernel,
        grid=(N_REL_HEADS,),
        in_specs=[smem],
        out_specs=pl.BlockSpec((1, 2, TQ, TQ), lambda h: (h, 0, 0, 0)),
        out_shape=jax.ShapeDtypeStruct((N_REL_HEADS, 2, TQ, TQ), F32),
        compiler_params=_cparams(("parallel",)),
        name="bias_near",
    )(tab_t)
    cmp_bias = pl.pallas_call(
        _bias_cmp_kernel,
        grid=(NSA_HEADS,),
        in_specs=[smem],
        out_specs=pl.BlockSpec((1, LANE, S), lambda h: (h, 0, 0)),
        out_shape=jax.ShapeDtypeStruct((NSA_HEADS, LANE, S), F32),
        compiler_params=_cparams(("parallel",)),
        name="bias_cmp",
    )(tab_t)
    return near, cmp_bias


ACC_ROWS = NSA_HD + 16


def _with_ones(vt):
    return jnp.concatenate([vt, jnp.ones((ACC_ROWS - vt.shape[0], vt.shape[1]), BF16)], axis=0)


def _probs(s, m):
    return jnp.exp2((s - m).astype(BF16))


def _flash_first(ss, vts):
    ms = [jnp.max(s, axis=0, keepdims=True) for s in ss]
    ps = [_probs(s, m) for s, m in zip(ss, ms)]
    return tuple((m, _dot(vt, p)) for m, p, vt in zip(ms, ps, vts))


def _flash_update(ss, vts, sts):
    ms = [jnp.maximum(st[0], jnp.max(s, axis=0, keepdims=True)) for s, st in zip(ss, sts)]
    alphas = [jnp.exp2(st[0] - m) for m, st in zip(ms, sts)]
    ps = [_probs(s, m) for s, m in zip(ss, ms)]
    return tuple((m, a * st[1] + _dot(vt, p)) for m, a, st, p, vt in zip(ms, alphas, sts, ps, vts))


def _flash_out(st):
    acc = st[1]
    return acc[0:NSA_HD, :] * (1.0 / acc[NSA_HD:NSA_HD + 1, :])


def _stream_scratch(n):
    return ([pltpu.VMEM((8, TQ), F32), pltpu.VMEM((ACC_ROWS, TQ), F32)] * n
            + [pltpu.VMEM((TQ, TQ), F32)] * (2 * n) + [pltpu.VMEM((TQ, TQ), BF16)] * (2 * n))


def _stream_refs(refs, n):
    st_refs = tuple((refs[2 * c], refs[2 * c + 1]) for c in range(n))
    s, p = refs[2 * n:4 * n], refs[4 * n:6 * n]
    return st_refs, tuple(s[:n]), tuple(s[n:]), tuple(p[:n]), tuple(p[n:])


def _causal_stream(qi, scores, vtile, st_refs, s_a, s_b, p_a, p_b, diag_ready=False):
    n_far = jnp.maximum(qi - 1, 0)
    top = qi - 2

    chains = range(len(st_refs))

    def write(s_refs, kt, near=2):
        for c in chains:
            s_refs[c][...] = scores(jnp.maximum(kt, 0), near, c)

    def pending(p_refs, kt):
        return [_dot(vtile(jnp.maximum(kt, 0), c), p_refs[c][...]) for c in chains]

    def step(s_cur, p_cur, s_next, kt_next, p_prev, kt_prev, anchor=None):
        rescale = []
        for c in chains:
            nxt = scores(jnp.maximum(kt_next, 0), 2, c)
            s_next[c][...] = nxt
            pv = _dot(vtile(jnp.maximum(kt_prev, 0), c), p_prev[c][...])
            st = st_refs[c][0]
            s = s_cur[c][...]
            m_old = st[0:1, :]
            m = jnp.maximum(m_old, jnp.max(s, axis=0, keepdims=True))
            if anchor is not None:
                m = m + anchor
            anchor = nxt[0:1, :] * 0.0
            p_cur[c][...] = _probs(s, m)
            st[0:1, :] = m
            rescale.append((jnp.exp2(m_old - m), pv))
        for c in chains:
            acc = st_refs[c][1]
            acc[...] = rescale[c][0] * (acc[...] + rescale[c][1])
        return anchor

    if not diag_ready:
        write(s_a, qi, 0)
    for c in chains:
        s_b[c][...] = scores(jnp.maximum(qi - 1, 0), 1, c)
        st, acc = st_refs[c]
        s = s_a[c][...]
        m = jnp.max(s, axis=0, keepdims=True)
        p_a[c][...] = _probs(s, m)
        st[0:1, :] = m
        acc[...] = jnp.zeros(acc.shape, F32)
    step(s_b, p_b, s_a, top, p_a, qi)

    def pair(j, c):
        kt = top - 2 * j
        anchor = step(s_a, p_a, s_b, kt - 1, p_b, kt + 1)
        step(s_b, p_b, s_a, kt - 2, p_a, kt, anchor)
        return c
    lax.fori_loop(0, n_far // 2, pair, 0)

    @pl.when(n_far % 2 == 1)
    def _():
        for c, pv in zip(chains, pending(p_b, 1)):
            st, acc = st_refs[c]
            s = s_a[c][...]
            m_old = st[0:1, :]
            m = jnp.maximum(m_old, jnp.max(s, axis=0, keepdims=True))
            st[0:1, :] = m
            acc[...] = jnp.exp2(m_old - m) * (acc[...] + pv) + _dot(vtile(0, c), _probs(s, m))

    @pl.when(n_far % 2 == 0)
    def _():
        last = jnp.where(n_far == 0, qi - 1, 0)
        for c, pv in zip(chains, pending(p_b, last)):
            acc = st_refs[c][1]
            acc[...] = acc[...] + pv
    return tuple((st[0:1, :], acc[...]) for st, acc in st_refs)


def _ktile(ref, kt):
    return ref[pl.ds(pl.multiple_of(kt * TQ, TQ), TQ), :]


def _nsa_kernel(qa_ref, gt_ref, kcmp_ref, vcmpt_ref, bc_ref, ks_ref, vst_ref, kw_ref, vwt_ref, dn_ref,
                ov_ref, o_ref, psum_ref, sel_ref, oacc_ref, sig_ref, *scratch):
    st_refs, s_a, s_b, p_a, p_b = _stream_refs(scratch, NSA_HEADS)
    qi = pl.program_id(1)
    t0 = qi * TQ
    sub_grp = lax.shift_right_arithmetic(lax.broadcasted_iota(jnp.int32, (LANE, 1), 0), 6)
    sig_ref[...] = jax.nn.sigmoid(gt_ref[...])

    def gate_row(c):
        return sig_ref[pl.ds(c, 1), :]

    def masked_q(r, g):
        return jnp.where(sub_grp == g, qa_ref[r].astype(F32), 0.0).astype(BF16)

    n_cmp = kcmp_ref.shape[1] - 1
    crow = lax.broadcasted_iota(jnp.int32, (LANE, 1), 0)
    cmp_end = jnp.where(crow < n_cmp, crow * CMP_STRIDE + CMP_LEN - 1, 1 << 30)
    mask_c = (t0 + lax.broadcasted_iota(jnp.int32, (1, TQ), 1)) >= cmp_end

    heads = [(r, g) for r in range(NSA_R) for g in range(NSA_G)]
    scores_c = [_dot(kcmp_ref[0], masked_q(r, g)) for r, g in heads]
    probs_c = []
    for (r, g), s in zip(heads, scores_c):
        s = jnp.where(mask_c, s + bc_ref[g * NSA_R + r], NEG)
        p = jnp.where(mask_c, jnp.exp2(s - jnp.max(s, axis=0, keepdims=True)), 0.0)
        l = jnp.sum(p, axis=0, keepdims=True)
        probs_c.append(p * jnp.where(l > 0.0, 1.0 / l, 0.0))
    for g in range(NSA_G):
        psum_ref[g] = sum(p for (r, gg), p in zip(heads, probs_c) if gg == g)
    outs_c = [_dot(vcmpt_ref[0], p.astype(BF16)) for p in probs_c]
    for r in range(NSA_R):
        o0, o1 = outs_c[r * NSA_G], outs_c[r * NSA_G + 1]
        oacc_ref[r] = jnp.where(sub_grp == 0, gate_row(r * 3) * o0, gate_row((NSA_R + r) * 3) * o1)

    has_prev = jnp.where(qi >= 1, 0.0, NEG)
    has_wfar = jnp.where(qi >= WINDOW // TQ, 0.0, NEG)
    kt_prev = jnp.maximum(qi - 1, 0)
    kt_wfar = jnp.maximum(qi - WINDOW // TQ, 0)
    wfar_mask = lax.broadcasted_iota(jnp.int32, (TQ, TQ), 0) > lax.broadcasted_iota(jnp.int32, (TQ, TQ), 1)

    def attend_all_heads():
        chains = [(r, g) for r in range(NSA_R) for g in range(NSA_G)]
        qms = [masked_q(r, g) for r, g in chains]
        hids = [g * NSA_R + r for r, g in chains]

        per_tile = TQ // SEL_BLOCK
        key_blk = lax.shift_right_arithmetic(lax.broadcasted_iota(jnp.int32, (TQ, LANE), 0), 6)
        lane_id = lax.broadcasted_iota(jnp.int32, (TQ, LANE), 1)

        def keys_with_block_onehot(kt, g):
            other = (1 - g) * NSA_HD
            onehot = jnp.where(lane_id - other == key_blk, 1.0, 0.0).astype(BF16)
            in_other = (lane_id >= other) & (lane_id < other + NSA_HD)
            return jnp.where(in_other, onehot, _ktile(ks_ref, kt))

        def query_with_mask_rows(kt, c):
            r, g = chains[c]
            own = qa_ref[r, g * NSA_HD:(g + 1) * NSA_HD, :]
            pair = sel_ref[g, pl.ds(pl.multiple_of((kt // 2) * 2 * per_tile, 2 * per_tile), 2 * per_tile), :]
            rows = jnp.where(kt % 2 == 0, pair[0:per_tile], pair[per_tile:2 * per_tile]).astype(BF16)
            other = jnp.concatenate([rows, jnp.zeros((NSA_HD - per_tile, TQ), BF16)], axis=0)
            return jnp.concatenate([own, other] if g == 0 else [other, own], axis=0)

        def near_bias(s, near, c):
            if near == 0:
                return s + dn_ref[hids[c], 0]
            if near == 1:
                return s + dn_ref[hids[c], 1] + has_prev
            return s

        def group_values(ref, kt, c):
            g = chains[c][1]
            return _with_ones(ref[g * NSA_HD:(g + 1) * NSA_HD, pl.ds(pl.multiple_of(kt * TQ, TQ), TQ)])

        def finish(sts, branch):
            for c, (r, g) in enumerate(chains):
                oacc_ref[r, g * NSA_HD:(g + 1) * NSA_HD, :] += gate_row(hids[c] * 3 + branch) * _flash_out(sts[c])

        def slc_scores(kt, near, c):
            return near_bias(_dot(keys_with_block_onehot(kt, chains[c][1]), query_with_mask_rows(kt, c)), near, c)

        def slc_vtile(kt, c):
            return group_values(vst_ref, kt, c)

        win_tiles = ((qi, 0), (kt_prev, 1), (kt_wfar, 2))

        def win_scores(c):
            out = []
            for kt, near in win_tiles:
                s = near_bias(_dot(_ktile(kw_ref, kt), qms[c]), near, c)
                out.append(jnp.where(wfar_mask, s + has_wfar, NEG) if near == 2 else s)
            return out

        sts = []
        ahead = win_scores(0)
        for c in range(len(chains)):
            tiles, ahead = ahead, (win_scores(c + 1) if c + 1 < len(chains) else None)
            st = _flash_first([tiles[0]], [group_values(vwt_ref, qi, c)])
            st = _flash_update([tiles[1]], [group_values(vwt_ref, kt_prev, c)], st)
            sts.append(_flash_update([tiles[2]], [group_values(vwt_ref, kt_wfar, c)], st)[0])
        finish(sts, 2)

        n_sel = ov_ref.shape[0]
        jj = lax.broadcasted_iota(jnp.int32, (n_sel, TQ), 0)
        cur = lax.shift_right_arithmetic(t0 + lax.broadcasted_iota(jnp.int32, (n_sel, TQ), 1), 6)
        valid = jj <= cur
        forced = (jj == 0) | (cur - jj < SEL_FORCED_LOCAL)
        for g in range(NSA_G):
            ps = psum_ref[g]
            hi = ps.astype(BF16)
            rem = ps - hi.astype(F32)
            mid = rem.astype(BF16)
            lo = (rem - mid.astype(F32)).astype(BF16)
            ov = ov_ref[...]
            imp = _dot(ov, hi) + _dot(ov, mid) + _dot(ov, lo)
            score = jnp.where(valid, jnp.where(forced, 1e9, imp), -1e9)
            jf = jj.astype(F32)
            rest = score
            chosen = jnp.zeros((n_sel, TQ), F32)
            for _ in range(min(SEL_TOPN, n_sel)):
                best = jnp.max(rest, axis=0, keepdims=True)
                first = jnp.min(jnp.where(rest == best, jf, float(n_sel)), axis=0, keepdims=True)
                hit = jf == first
                chosen = jnp.where(hit, 1.0, chosen)
                rest = jnp.where(hit, -3e38, rest)
            sel_ref[g] = jnp.where(chosen > 0.5, jnp.where(score > -1e8, 0.0, NEG), NEG)

        finish(_causal_stream(qi, slc_scores, slc_vtile, st_refs, s_a, s_b, p_a, p_b), 1)
    attend_all_heads()

    for r in range(NSA_R):
        o_ref[r] = oacc_ref[r].T.astype(BF16)


def _nsa(qa, gates_t, kcmp, vcmpt, bias_c, ks, vst, kw, vwt, near_a, ov, B, S):
    nq = S // TQ
    T = B * S
    k_spec = pl.BlockSpec((S, LANE), lambda b, q: (b, 0))
    v_spec = pl.BlockSpec((LANE, S), lambda b, q: (0, b))
    full = lambda a: pl.BlockSpec(a.shape, lambda b, q: (0,) * a.ndim)
    cmp_spec = pl.BlockSpec((1, LANE, LANE), lambda b, q: (b, 0, 0))
    qo = pl.BlockSpec((4, TQ, LANE), lambda b, q: (0, b * nq + q, 0))
    return pl.pallas_call(
        _nsa_kernel,
        grid=(B, nq),
        in_specs=[pl.BlockSpec((4, LANE, TQ), lambda b, q: (0, 0, b * nq + q)),
                  pl.BlockSpec((GATE_ROWS, TQ), lambda b, q: (0, b * nq + q)),
                  cmp_spec, cmp_spec,
                  pl.BlockSpec((NSA_HEADS, LANE, TQ), lambda b, q: (0, 0, q)),
                  k_spec, v_spec, k_spec, v_spec,
                  pl.BlockSpec((NSA_HEADS, 2, TQ, TQ), lambda b, q: (0, 0, 0, 0)),
                  full(ov)],
        out_specs=qo,
        out_shape=jax.ShapeDtypeStruct((4, T, LANE), BF16),
        scratch_shapes=[pltpu.VMEM((NSA_G, LANE, TQ), F32), pltpu.VMEM((NSA_G, S // SEL_BLOCK, TQ), F32),
                        pltpu.VMEM((NSA_R, LANE, TQ), F32), pltpu.VMEM((GATE_ROWS, TQ), F32)]
                       + _stream_scratch(NSA_HEADS),
        compiler_params=_cparams(("parallel", "arbitrary")),
        name="nsa",
    )(qa, gates_t, kcmp, vcmpt, bias_c, ks, vst, kw, vwt, near_a, ov)


def _diff_kernel(lq1_ref, lk1_ref, lq2_ref, lk2_ref, sub_ref, qb_ref, kb_ref, vbt_ref, dn_ref, o_ref, *scratch):
    n = DIFF_PAIRS_PER_STEP * 4
    st_refs, s_a, s_b, p_a, p_b = _stream_refs(scratch, n)
    qi = pl.program_id(1)
    chain_of_row = lax.shift_right_arithmetic(lax.broadcasted_iota(jnp.int32, (LANE, 1), 0), 5)
    lam = (jnp.exp(jnp.sum(lq1_ref[...] * lk1_ref[...], axis=-1, keepdims=True))
           - jnp.exp(jnp.sum(lq2_ref[...] * lk2_ref[...], axis=-1, keepdims=True)) + LAMBDA_INIT)
    has_prev = jnp.where(qi >= 1, 0.0, NEG)
    n_steps = DIFF_HEADS // 2 // DIFF_PAIRS_PER_STEP

    def score_fn(step):
        pairs = [step * DIFF_PAIRS_PER_STEP + c // 4 for c in range(n)]
        qs = [qb_ref[step * DIFF_PAIRS_PER_STEP + i].astype(F32) for i in range(DIFF_PAIRS_PER_STEP)]
        qms = [jnp.where(chain_of_row == c % 4, qs[c // 4], 0.0).astype(BF16) for c in range(n)]

        def scores(kt, near, c):
            s = _dot(kb_ref[pairs[c], pl.ds(pl.multiple_of(kt * TQ, TQ), TQ), :], qms[c])
            if near == 2:
                return s
            bias = dn_ref[2 * pairs[c] + (c % 4) // 2, near]
            return s + bias if near == 0 else s + bias + has_prev
        return scores

    def start_diag(step):
        scores = score_fn(step)
        for c in range(n):
            s_a[c][...] = scores(qi, 0, c)

    start_diag(0)

    def step_body(step, carry):
        def vtile(kt, c):
            hh = (c % 4) // 2
            return _with_ones(vbt_ref[step * DIFF_PAIRS_PER_STEP + c // 4, hh * 2 * DIFF_HD:(hh + 1) * 2 * DIFF_HD,
                                      pl.ds(pl.multiple_of(kt * TQ, TQ), TQ)])

        sts = _causal_stream(qi, score_fn(step), vtile, st_refs, s_a, s_b, p_a, p_b, diag_ready=True)
        start_diag(jnp.minimum(step + 1, n_steps - 1))
        for i in range(DIFF_PAIRS_PER_STEP):
            outs = []
            for hh in range(2):
                c = 4 * i + 2 * hh
                o = _flash_out(sts[c]) - lam * _flash_out(sts[c + 1])
                outs.append(o * lax.rsqrt(jnp.mean(o * o, axis=0, keepdims=True) + EPS))
            out = jnp.concatenate(outs, axis=0) * sub_ref[...] * (1.0 - LAMBDA_INIT)
            o_ref[step * DIFF_PAIRS_PER_STEP + i] = out.T.astype(BF16)
        return carry
    lax.fori_loop(0, n_steps, step_body, 0)


def _diff(lq1, lk1, lq2, lk2, sub, qb, kb, vbt, near_b, B, S):
    nq = S // TQ
    T = B * S
    full = lambda a: pl.BlockSpec(a.shape, lambda b, q: (0,) * a.ndim)
    k_spec = pl.BlockSpec((4, S, LANE), lambda b, q: (0, b, 0))
    v_spec = pl.BlockSpec((4, LANE, S), lambda b, q: (0, 0, b))
    qo = pl.BlockSpec((4, TQ, LANE), lambda b, q: (0, b * nq + q, 0))
    return pl.pallas_call(
        _diff_kernel,
        grid=(B, nq),
        in_specs=[full(lq1), full(lk1), full(lq2), full(lk2), full(sub),
                  pl.BlockSpec((4, LANE, TQ), lambda b, q: (0, 0, b * nq + q)), k_spec, v_spec,
                  pl.BlockSpec((DIFF_HEADS, 2, TQ, TQ), lambda b, q: (NSA_HEADS // DIFF_HEADS, 0, 0, 0))],
        out_specs=qo,
        out_shape=jax.ShapeDtypeStruct((4, T, LANE), BF16),
        scratch_shapes=_stream_scratch(DIFF_PAIRS_PER_STEP * 4),
        compiler_params=_cparams(("parallel", "arbitrary")),
        name="diff",
    )(lq1, lk1, lq2, lk2, sub, qb, kb, vbt, near_b)


def _outproj_kernel(x_ref, oa_ref, ob_ref, w_ref, g_ref, wr_ref, br_ref, xt_ref, route_ref, cnt_ref, carry_ref):
    @pl.when(pl.program_id(0) == 0)
    def _():
        carry_ref[...] = jnp.zeros(carry_ref.shape, F32)

    o = jnp.concatenate([oa_ref[r] for r in range(4)] + [ob_ref[r] for r in range(4)], axis=-1)
    h = x_ref[...] + _dot(o, w_ref[...])
    xt_ref[:, 0:D_MODEL] = h
    tn = _rmsnorm(h, g_ref[...]).astype(BF16)
    logits = _dot(tn, wr_ref[...]) + br_ref[...]
    lane = lax.broadcasted_iota(jnp.int32, (1, LANE), 1)
    lane_f = lane.astype(F32)
    is_grp = lane < MOE_GROUPS
    lg = jnp.where(is_grp, logits, NEG)
    mg = jnp.max(lg, axis=-1, keepdims=True)
    zg = jnp.sum(jnp.where(is_grp, jnp.exp(lg - mg), 0.0), axis=-1, keepdims=True)
    g_prob = 1.0 / zg
    g_idx = jnp.min(jnp.where(lg == mg, lane_f, 1e9), axis=-1, keepdims=True)
    lane_grp = jnp.where((lane >= MOE_GROUPS) & (lane < MOE_GROUPS + N_EXPERTS),
                         lax.shift_right_arithmetic(lane - MOE_GROUPS, 3), -1).astype(F32)
    le = jnp.where(lane_grp == g_idx, logits, NEG)
    m1 = jnp.max(le, axis=-1, keepdims=True)
    e1 = jnp.min(jnp.where(le == m1, lane_f, 1e9), axis=-1, keepdims=True)
    le2 = jnp.where(lane_f == e1, NEG, le)
    m2 = jnp.max(le2, axis=-1, keepdims=True)
    e2 = jnp.min(jnp.where(le2 == m2, lane_f, 1e9), axis=-1, keepdims=True)
    ratio = jnp.exp(m2 - m1)
    w1 = g_prob / (1.0 + ratio)
    w2 = w1 * ratio
    xt_ref[:, D_MODEL:D_MODEL + LANE] = jnp.where(lane_f == e1, w1, 0.0) + jnp.where(lane_f == e2, w2, 0.0)

    tm = h.shape[0]
    onehot = jnp.where(lane_f == g_idx, 1.0, 0.0)
    earlier = jnp.where(lax.broadcasted_iota(jnp.int32, (tm, tm), 0) > lax.broadcasted_iota(jnp.int32, (tm, tm), 1),
                        1.0, 0.0).astype(BF16)
    prefix = _dot(earlier, onehot.astype(BF16)) + carry_ref[...]
    rank = jnp.sum(onehot * prefix, axis=-1, keepdims=True)
    carry_ref[...] += jnp.sum(onehot, axis=0, keepdims=True)
    route_ref[...] = jnp.where(lane == 0, g_idx, jnp.where(lane == 1, rank, 0.0))
    cnt_ref[...] = jnp.broadcast_to(carry_ref[...], cnt_ref.shape)


def _out_proj(x2, oa, ob, w, g, wr, br):
    T = x2.shape[0]
    tm = TM_PROJ
    row = lambda i: (i, 0)
    full = lambda a: pl.BlockSpec(a.shape, lambda i: (0,) * a.ndim)
    o4 = pl.BlockSpec((4, tm, LANE), lambda i: (0, i, 0))
    return pl.pallas_call(
        _outproj_kernel,
        grid=(T // tm,),
        in_specs=[pl.BlockSpec((tm, D_MODEL), row), o4, o4, full(w), full(g), full(wr), full(br)],
        out_specs=[pl.BlockSpec((tm, XT_WIDTH), row), pl.BlockSpec((tm, LANE), row),
                   pl.BlockSpec((8, LANE), lambda i: (0, 0))],
        out_shape=[jax.ShapeDtypeStruct((T, XT_WIDTH), F32), jax.ShapeDtypeStruct((T, LANE), F32),
                   jax.ShapeDtypeStruct((8, LANE), F32)],
        scratch_shapes=[pltpu.VMEM((1, LANE), F32)],
        compiler_params=_cparams(("arbitrary",)),
        name="out_proj",
    )(x2, oa, ob, w, g, wr, br)


def _moe_kernel(tg_ref, ok_ref, idx_ref, idxn_ref, xt_hbm, g_ref, wg_ref, wu_ref, wd_ref, y_ref, xbuf, sem,
                xb_ref, cmb_ref, fence_sem):
    i = pl.program_id(0)
    slot = i % 2
    tm = xbuf.shape[1]

    def row_copy(index_ref, r, dst_slot):
        return pltpu.make_async_copy(xt_hbm.at[pl.ds(index_ref[0, 0, r], 1), :],
                                     xbuf.at[dst_slot, pl.ds(r, 1), :], sem.at[dst_slot])

    @pl.when(i == 0)
    def _():
        def body(r, c):
            row_copy(idx_ref, r, 0).start()
            return c
        lax.fori_loop(0, tm, body, 0)

    @pl.when((i == 0) | (ok_ref[jnp.maximum(i - 1, 0)] == 1))
    def _():
        pltpu.make_async_copy(xt_hbm.at[pl.ds(0, tm), :], xbuf.at[slot], sem.at[slot]).wait()

    @pl.when(ok_ref[i] == 0)
    def _():
        y_ref[...] = jnp.zeros(y_ref.shape, F32)

    @pl.when(ok_ref[i] == 1)
    def _():
        lane = lax.broadcasted_iota(jnp.int32, (1, LANE), 1)
        xb_ref[...] = _rmsnorm(xbuf[slot, :, 0:D_MODEL], g_ref[...]).astype(BF16)
        cmb_ref[...] = xbuf[slot, :, D_MODEL:D_MODEL + LANE]
        first_lane = MOE_GROUPS + tg_ref[i] * EPG
        per_expert = -(-tm // (EPG - 1))
        y = jnp.zeros((tm, D_MODEL), F32)
        issued = jnp.int32(0)
        for e in range(EPG):
            x = xb_ref[...]
            a = _dot(x, wg_ref[0, e])
            b = _dot(x, wu_ref[0, e])
            ce = jnp.sum(jnp.where(lane == first_lane + e + issued, cmb_ref[...], 0.0), axis=-1, keepdims=True)
            y = y + _dot(((a * jax.nn.sigmoid(a)) * b * ce).astype(BF16), wd_ref[0, e])
            for r in range(e * per_expert, min((e + 1) * per_expert, tm)):
                row_copy(idxn_ref, r, 1 - slot).start(priority=r % 2)
            if e in ANCHOR_AFTER:
                issued = pl.semaphore_read(fence_sem)
        for blk in range(ROW_SLABS):
            y_ref[pl.ds(blk, tm, stride=ROW_SLABS), :] = y[:, blk * LANE:(blk + 1) * LANE]


def _moe(tile_group, tile_ok, src3, xt, g, wg, wu, wd):
    n_tiles, _, tm = src3.shape
    last = n_tiles - 1
    grid_spec = pltpu.PrefetchScalarGridSpec(
        num_scalar_prefetch=2,
        grid=(n_tiles,),
        in_specs=[pl.BlockSpec((1, 1, tm), lambda i, tg, ok: (i, 0, 0), memory_space=pltpu.SMEM),
                  pl.BlockSpec((1, 1, tm), lambda i, tg, ok: (jnp.minimum(i + 1, last), 0, 0), memory_space=pltpu.SMEM),
                  pl.BlockSpec(memory_space=pl.ANY),
                  pl.BlockSpec((1, D_MODEL), lambda i, tg, ok: (0, 0)),
                  pl.BlockSpec((1, EPG, D_MODEL, EXPERT_FF), lambda i, tg, ok: (tg[i], 0, 0, 0)),
                  pl.BlockSpec((1, EPG, D_MODEL, EXPERT_FF), lambda i, tg, ok: (tg[i], 0, 0, 0)),
                  pl.BlockSpec((1, EPG, EXPERT_FF, D_MODEL), lambda i, tg, ok: (tg[i], 0, 0, 0))],
        out_specs=pl.BlockSpec((tm * ROW_SLABS, LANE), lambda i, tg, ok: (i, 0)),
        scratch_shapes=[pltpu.VMEM((2, tm, XT_WIDTH), F32), pltpu.SemaphoreType.DMA((2,)),
                        pltpu.VMEM((tm, D_MODEL), BF16), pltpu.VMEM((tm, LANE), F32), pltpu.SemaphoreType.REGULAR],
    )
    return pl.pallas_call(
        _moe_kernel,
        grid_spec=grid_spec,
        out_shape=jax.ShapeDtypeStruct((n_tiles * tm * ROW_SLABS, LANE), F32),
        compiler_params=pltpu.CompilerParams(dimension_semantics=("arbitrary",), vmem_limit_bytes=VMEM_LIMIT_MOE),
        name="moe",
    )(tile_group, tile_ok, src3, src3, xt, g, wg, wu, wd)


def _final_kernel(pos_ref, posn_ref, y_hbm, h_ref, gf_ref, o_ref, ybuf, sem):
    i = pl.program_id(0)
    slot = i % 2
    tm = h_ref.shape[0]

    def issue(index_ref, dst_slot):
        group = 8

        def body(j, c):
            for k in range(group):
                r = j * group + k
                src_row = pl.multiple_of(index_ref[0, 0, r], ROW_SLABS)
                pltpu.make_async_copy(y_hbm.at[pl.ds(src_row, ROW_SLABS), :],
                                      ybuf.at[dst_slot, pl.ds(r * ROW_SLABS, ROW_SLABS), :],
                                      sem.at[dst_slot]).start(priority=k % 2)
            return c
        lax.fori_loop(0, tm // group, body, 0)

    @pl.when(i == 0)
    def _():
        issue(pos_ref, 0)

    @pl.when(i + 1 < pl.num_programs(0))
    def _():
        issue(posn_ref, 1 - slot)

    pltpu.make_async_copy(ybuf.at[slot], ybuf.at[slot], sem.at[slot]).wait()
    y = jnp.concatenate([ybuf[slot, pl.ds(blk, tm, stride=ROW_SLABS), :] for blk in range(ROW_SLABS)], axis=-1)
    h = h_ref[...] + y
    o_ref[...] = h * lax.rsqrt(jnp.mean(h * h, axis=-1, keepdims=True) + EPS) * gf_ref[...]


def _final(pos3, y_sorted, h, gf):
    n_tiles, _, tm = pos3.shape
    last = n_tiles - 1
    row = lambda i: (i, 0)
    return pl.pallas_call(
        _final_kernel,
        grid=(n_tiles,),
        in_specs=[pl.BlockSpec((1, 1, tm), lambda i: (i, 0, 0), memory_space=pltpu.SMEM),
                  pl.BlockSpec((1, 1, tm), lambda i: (jnp.minimum(i + 1, last), 0, 0), memory_space=pltpu.SMEM),
                  pl.BlockSpec(memory_space=pl.ANY),
                  pl.BlockSpec((tm, D_MODEL), row),
                  pl.BlockSpec((1, D_MODEL), lambda i: (0, 0))],
        out_specs=pl.BlockSpec((tm, D_MODEL), row),
        out_shape=jax.ShapeDtypeStruct((n_tiles * tm, D_MODEL), F32),
        scratch_shapes=[pltpu.VMEM((2, tm * ROW_SLABS, LANE), F32), pltpu.SemaphoreType.DMA((2,))],
        compiler_params=_cparams(("arbitrary",)),
        name="final",
    )(pos3, pos3, y_sorted, h, gf)


def _qa_perm():
    new = np.arange(NSA_HEADS * NSA_HD)
    r, g, d = new // LANE, (new % LANE) // NSA_HD, new % NSA_HD
    return (g * NSA_R + r) * NSA_HD + d


def _block_diag2(w):
    z = jnp.zeros_like(w)
    return jnp.concatenate([jnp.concatenate([w, z], axis=-1), jnp.concatenate([z, w], axis=-1)], axis=-2)


def kernel(x, rel_bias, ln_mix, w_in, cmp_pos_k, cmp_pos_v, cmp_k_w1, cmp_k_w2, cmp_v_w1, cmp_v_w2,
           diff_lq1, diff_lk1, diff_lq2, diff_lk2, diff_subln, w_out, ln_ffn,
           router_group_w, router_group_b, router_expert_w, router_expert_b,
           exp_w_gate, exp_w_up, exp_w_down, ln_final):
    B, S, D = x.shape
    T = B * S
    assert D == D_MODEL and S % TQ == 0 and S >= WINDOW and T % TM_MOE == 0 and T % TM_FINAL == 0
    x2 = x.reshape(T, D)
    perm = _qa_perm()

    w = w_in[0]
    c_kc, c_vc, c_ks, c_vs, c_kw, c_vw, c_gt = 512, 640, 768, 896, 1024, 1152, 1280
    c_qb = c_gt + N_GATE
    c_kb, c_vb = c_qb + 512, c_qb + 1024
    col = lambda c, n=LANE: w[:, c:c + n]
    w_tok = jnp.concatenate([col(c_kc), col(c_vc), col(c_ks), col(c_kw), col(c_kb, 512)], axis=1).astype(BF16)
    w_feat = jnp.concatenate([w[:, perm], col(c_qb, 512), col(c_vs), col(c_vw), col(c_vb, 512), col(c_gt, N_GATE),
                              jnp.zeros((D, GATE_ROWS - N_GATE), F32)], axis=1).T.astype(BF16)
    qa, kc, vc, ks, kw, qb, kb, vst, vwt, vbt, gates_t = _in_proj(x2, ln_mix[0][None, :], w_tok, w_feat)

    w1k = _block_diag2(cmp_k_w1[0].reshape(CMP_LEN, NSA_HD, CMP_HIDDEN)).astype(BF16)
    w1v = _block_diag2(cmp_v_w1[0].reshape(CMP_LEN, NSA_HD, CMP_HIDDEN)).astype(BF16)
    w2k = _block_diag2(cmp_k_w2[0]).astype(BF16)
    w2v = _block_diag2(cmp_v_w2[0]).astype(BF16)
    posk = jnp.tile(cmp_pos_k[0], (1, NSA_G))
    posv = jnp.tile(cmp_pos_v[0], (1, NSA_G))
    kcmp, vcmpt = _compress(kc, vc, posk, posv, w1k, w1v, w2k, w2v, B, S)

    near, bias_c = _bias_tiles(rel_bias.T, S)

    n_sel = S // SEL_BLOCK
    nrow = S // CMP_STRIDE
    c_start = np.arange(nrow) * CMP_STRIDE
    s_start = np.arange(n_sel) * SEL_BLOCK
    ov = ((c_start[None, :] <= s_start[:, None] + SEL_BLOCK - 1)
          & (c_start[None, :] + CMP_LEN - 1 >= s_start[:, None])
          & (np.arange(nrow)[None, :] < nrow - 1)).astype(np.float32)
    o_a = _nsa(qa, gates_t, kcmp, vcmpt, bias_c, ks, vst, kw, vwt, near, jnp.asarray(ov, BF16), B, S)

    sub = jnp.tile(diff_subln[0], 2)[:, None]
    o_b = _diff(diff_lq1[0][None, :], diff_lk1[0][None, :], diff_lq2[0][None, :], diff_lk2[0][None, :],
                sub, qb, kb, vbt, near, B, S)

    w_o = jnp.concatenate([w_out[0][:512][perm], w_out[0][512:]], axis=0).astype(BF16)
    n_r = MOE_GROUPS + N_EXPERTS
    wr = jnp.concatenate([router_group_w[0], router_expert_w[0], jnp.zeros((D, LANE - n_r), F32)], axis=1).astype(BF16)
    br = jnp.concatenate([router_group_b[0], router_expert_b[0], jnp.zeros((LANE - n_r,), F32)])[None, :]
    xt, route, counts = _out_proj(x2, o_a, o_b, w_o, ln_ffn[0][None, :], wr, br)

    tm = TM_MOE
    n_tiles = T // tm + MOE_GROUPS
    cnt = counts[0, :MOE_GROUPS].astype(jnp.int32)
    ends = jnp.cumsum((cnt + tm - 1) // tm * tm)
    starts = ends - (cnt + tm - 1) // tm * tm
    pos = starts[route[:, 0].astype(jnp.int32)] + route[:, 1].astype(jnp.int32)
    order = jnp.argsort(pos).astype(jnp.int32)
    row = jnp.arange(n_tiles * tm, dtype=jnp.int32)
    group_of = lambda r: sum((r >= ends[g]).astype(jnp.int32) for g in range(MOE_GROUPS - 1))
    row_group = group_of(row)
    local = row - starts[row_group]
    compact = (jnp.cumsum(cnt) - cnt)[row_group] + local
    src = jnp.where(local < cnt[row_group], order[jnp.clip(compact, 0, T - 1)], 0)
    tile_start = jnp.arange(n_tiles, dtype=jnp.int32) * tm
    tile_group = group_of(tile_start)
    tile_ok = (tile_start < ends[-1]).astype(jnp.int32)

    by_group = lambda a: a[0].astype(BF16).reshape((MOE_GROUPS, EPG) + a.shape[2:])
    y_sorted = _moe(tile_group, tile_ok, src.reshape(n_tiles, 1, tm), xt, ln_ffn[0][None, :],
                    by_group(exp_w_gate), by_group(exp_w_up), by_group(exp_w_down))
    out = _final((pos * ROW_SLABS).reshape(T // TM_FINAL, 1, TM_FINAL), y_sorted, xt, ln_final[None, :])
    return out.reshape(B, S, D)
```

```python
import math

import numpy as np
import jax
import jax.numpy as jnp
from jax import lax
from jax.experimental import pallas as pl
from jax.experimental.pallas import tpu as pltpu

F32 = jnp.float32
BF16 = jnp.bfloat16
NEG = -1e30
EPS = 1e-6
LOG2E = math.log2(math.e)

D_MODEL = 1024
LANE = 128
NSA_HEADS, NSA_G, NSA_R, NSA_HD = 8, 2, 4, 64
CMP_LEN, CMP_STRIDE, CMP_HIDDEN = 32, 16, 128
SEL_BLOCK, SEL_TOPN, SEL_FORCED_LOCAL, WINDOW = 64, 8, 2, 512
DIFF_HEADS, DIFF_HD = 8, 32
REL_BUCKETS, REL_MAX_EXACT, REL_MAX_DIST = 32, 16, 128
N_REL_HEADS = NSA_HEADS + DIFF_HEADS
MOE_GROUPS, EPG, N_EXPERTS, EXPERT_FF = 4, 8, 32, 256
LAMBDA_INIT = 0.8 - 0.6 * math.exp(-0.3 * 0)
N_GATE = NSA_HEADS * 3
GATE_ROWS = 32

TQ = 256
DIFF_PAIRS_PER_STEP = 4
TM_PROJ = 512
TM_MOE = 512
TM_FINAL = 512
ANCHOR_AFTER = (1, 3, 5, 6)
XT_WIDTH = D_MODEL + LANE
ROW_SLABS = D_MODEL // LANE
VMEM_LIMIT = 48 * 1024 * 1024
VMEM_LIMIT_MOE = 56 * 1024 * 1024


def _cparams(sem):
    return pltpu.CompilerParams(dimension_semantics=sem, vmem_limit_bytes=VMEM_LIMIT)


def _dot(a, b):
    return jnp.dot(a, b, preferred_element_type=F32)


def _rmsnorm(x, g):
    return x * lax.rsqrt(jnp.mean(x * x, axis=-1, keepdims=True) + EPS) * g


def _dot_nt(a, b):
    return lax.dot_general(a, b, (((1,), (1,)), ((), ())), preferred_element_type=F32)


def _bucket_thresholds():
    n = np.arange(0, REL_MAX_DIST + 1)
    nf = np.maximum(n, 1).astype(np.float32)
    large = REL_MAX_EXACT + (np.log(nf / np.float32(REL_MAX_EXACT)) / np.float32(math.log(REL_MAX_DIST / REL_MAX_EXACT))
                             * np.float32(REL_BUCKETS - REL_MAX_EXACT)).astype(np.int32)
    large = np.minimum(large, REL_BUCKETS - 1)
    bucket = np.where(n < REL_MAX_EXACT, n, large)
    return [int(np.argmax(bucket >= b)) for b in range(REL_BUCKETS)]


_THR = _bucket_thresholds()


def _inproj_kernel(x_ref, g_ref, w_ref, wt_ref, qa_ref, kc_ref, vc_ref, ks_ref, kw_ref, qb_ref, kb_ref,
                   vst_ref, vwt_ref, vbt_ref, gt_ref):
    x = x_ref[...]
    xn = (x * lax.rsqrt(jnp.mean(x * x, axis=-1, keepdims=True) + EPS) * g_ref[...]).astype(BF16)
    a = _dot(xn, w_ref[:, 0:512])
    kc_ref[...] = a[:, 0:128]
    vc_ref[...] = a[:, 128:256]
    ks_ref[...] = a[:, 256:384].astype(BF16)
    kw_ref[...] = a[:, 384:512].astype(BF16)
    a = _dot(xn, w_ref[:, 512:1024])
    for r in range(4):
        kb_ref[r] = a[:, r * LANE:(r + 1) * LANE].astype(BF16)

    ft = _dot_nt(wt_ref[...], xn)

    def feat(row0, rows=LANE):
        return ft[row0:row0 + rows, :]
    for r in range(4):
        qa_ref[r] = (feat(r * LANE) * (NSA_HD ** -0.5 * LOG2E)).astype(BF16)
        qb_ref[r] = (feat(512 + r * LANE) * (DIFF_HD ** -0.5 * LOG2E)).astype(BF16)
        vbt_ref[r] = feat(1280 + r * LANE).astype(BF16)
    vst_ref[...] = feat(1024).astype(BF16)
    vwt_ref[...] = feat(1152).astype(BF16)
    gt_ref[...] = feat(1792, GATE_ROWS)


def _in_proj(x2, g, w, wt):
    T = x2.shape[0]
    tm = TM_PROJ
    row = lambda i: (i, 0)
    o128b = jax.ShapeDtypeStruct((T, LANE), BF16)
    o128f = jax.ShapeDtypeStruct((T, LANE), F32)
    o4 = jax.ShapeDtypeStruct((4, T, LANE), BF16)
    ot = jax.ShapeDtypeStruct((LANE, T), BF16)
    o4t = jax.ShapeDtypeStruct((4, LANE, T), BF16)
    s128 = pl.BlockSpec((tm, LANE), row)
    s4 = pl.BlockSpec((4, tm, LANE), lambda i: (0, i, 0))
    st = pl.BlockSpec((LANE, tm), lambda i: (0, i))
    s4t = pl.BlockSpec((4, LANE, tm), lambda i: (0, 0, i))
    return pl.pallas_call(
        _inproj_kernel,
        grid=(T // tm,),
        in_specs=[pl.BlockSpec((tm, D_MODEL), row),
                  pl.BlockSpec((1, D_MODEL), lambda i: (0, 0)),
                  pl.BlockSpec(w.shape, lambda i: (0, 0)),
                  pl.BlockSpec(wt.shape, lambda i: (0, 0))],
        out_specs=[s4t, s128, s128, s128, s128, s4t, s4, st, st, s4t, pl.BlockSpec((GATE_ROWS, tm), lambda i: (0, i))],
        out_shape=[o4t, o128f, o128f, o128b, o128b, o4t, o4, ot, ot, o4t, jax.ShapeDtypeStruct((GATE_ROWS, T), F32)],
        compiler_params=_cparams(("parallel",)),
        name="in_proj",
    )(x2, g, w, wt)


def _gelu_tanh(x):
    return 0.5 * x * (1.0 + jnp.tanh(math.sqrt(2.0 / math.pi) * (x + 0.044715 * (x * x * x))))


def _compress_kernel(kc_ref, vc_ref, posk_ref, posv_ref, w1k_ref, w1v_ref, w2k_ref, w2v_ref, ko_ref, vo_ref):
    nrow = kc_ref.shape[0] // CMP_STRIDE
    rid = lax.broadcasted_iota(jnp.int32, (nrow, 1), 0)
    cid = lax.broadcasted_iota(jnp.int32, (1, nrow), 1)
    for src, pos, w1, w2, out, transposed in ((kc_ref, posk_ref, w1k_ref, w2k_ref, ko_ref, False),
                                              (vc_ref, posv_ref, w1v_ref, w2v_ref, vo_ref, True)):
        hid_a = jnp.zeros((nrow, 2 * CMP_HIDDEN), F32)
        hid_b = jnp.zeros((nrow, 2 * CMP_HIDDEN), F32)
        for m in range(CMP_STRIDE):
            y = src[pl.ds(m, nrow, stride=CMP_STRIDE), :]
            hid_a = hid_a + _dot((y + pos[m:m + 1, :]).astype(BF16), w1[m])
            hid_b = hid_b + _dot((y + pos[CMP_STRIDE + m:CMP_STRIDE + m + 1, :]).astype(BF16), w1[CMP_STRIDE + m])
        hid = hid_a + pltpu.roll(hid_b, nrow - 1, 0)
        o = _dot(_gelu_tanh(hid).astype(BF16), w2[...])
        if transposed:
            out[0] = jnp.where(cid < nrow - 1, o.T, 0.0).astype(BF16)
        else:
            out[0] = jnp.where(rid < nrow - 1, o, 0.0).astype(BF16)


def _compress(kc, vc, posk, posv, w1k, w1v, w2k, w2v, B, S):
    nrow = S // CMP_STRIDE
    assert nrow == LANE
    full = lambda a: pl.BlockSpec(a.shape, lambda b: (0,) * a.ndim)
    src = pl.BlockSpec((S, LANE), lambda b: (b, 0))
    osp = pl.BlockSpec((1, nrow, LANE), lambda b: (b, 0, 0))
    osh = jax.ShapeDtypeStruct((B, nrow, LANE), BF16)
    return pl.pallas_call(
        _compress_kernel,
        grid=(B,),
        in_specs=[src, src, full(posk), full(posv), full(w1k), full(w1v), full(w2k), full(w2v)],
        out_specs=[osp, osp],
        out_shape=[osh, osh],
        compiler_params=_cparams(("parallel",)),
        name="compress",
    )(kc, vc, posk, posv, w1k, w1v, w2k, w2v)


def _bias_from_dist(dist, tab_ref, h):
    val = jnp.full(dist.shape, tab_ref[h, 0], F32)
    for b in range(1, REL_BUCKETS):
        val = jnp.where(dist >= _THR[b], tab_ref[h, b], val)
    return val


def _bias_near_kernel(tab_ref, out_ref):
    h = pl.program_id(0)
    j = lax.broadcasted_iota(jnp.int32, (TQ, TQ), 0)
    i = lax.broadcasted_iota(jnp.int32, (TQ, TQ), 1)
    for d in range(2):
        bias = (_bias_from_dist(i - j + d * TQ, tab_ref, h) - tab_ref[h, REL_BUCKETS - 1]) * LOG2E
        out_ref[0, d] = jnp.where(i - j + d * TQ >= 0, bias, NEG)


def _bias_cmp_kernel(tab_ref, out_ref):
    h = pl.program_id(0)
    band = TQ // CMP_STRIDE
    out_ref[0] = jnp.full(out_ref.shape[1:], tab_ref[h, REL_BUCKETS - 1] * LOG2E, F32)
    for qi in range(out_ref.shape[2] // TQ):
        c0 = max(qi - 1, 0) * band
        c = c0 + lax.broadcasted_iota(jnp.int32, (2 * band, TQ), 0)
        t = qi * TQ + lax.broadcasted_iota(jnp.int32, (2 * band, TQ), 1)
        out_ref[0, c0:c0 + 2 * band, qi * TQ:(qi + 1) * TQ] = (
            _bias_from_dist(t - (c * CMP_STRIDE + CMP_LEN - 1), tab_ref, h) * LOG2E)


def _bias_tiles(tab_t, S):
    smem = pl.BlockSpec(memory_space=pltpu.SMEM)
    near = pl.pallas_call(
        _bias_near_kernel,
        grid=(N_REL_HEADS,),
        in_specs=[smem],
        out_specs=pl.BlockSpec((1, 2, TQ, TQ), lambda h: (h, 0, 0, 0)),
        out_shape=jax.ShapeDtypeStruct((N_REL_HEADS, 2, TQ, TQ), F32),
        compiler_params=_cparams(("parallel",)),
        name="bias_near",
    )(tab_t)
    cmp_bias = pl.pallas_call(
        _bias_cmp_kernel,
        grid=(NSA_HEADS,),
        in_specs=[smem],
        out_specs=pl.BlockSpec((1, LANE, S), lambda h: (h, 0, 0)),
        out_shape=jax.ShapeDtypeStruct((NSA_HEADS, LANE, S), F32),
        compiler_params=_cparams(("parallel",)),
        name="bias_cmp",
    )(tab_t)
    return near, cmp_bias


ACC_ROWS = NSA_HD + 16


def _with_ones(vt):
    return jnp.concatenate([vt, jnp.ones((ACC_ROWS - vt.shape[0], vt.shape[1]), BF16)], axis=0)


def _probs(s, m):
    return jnp.exp2((s - m).astype(BF16))


def _flash_first(ss, vts):
    ms = [jnp.max(s, axis=0, keepdims=True) for s in ss]
    ps = [_probs(s, m) for s, m in zip(ss, ms)]
    return tuple((m, _dot(vt, p)) for m, p, vt in zip(ms, ps, vts))


def _flash_update(ss, vts, sts):
    ms = [jnp.maximum(st[0], jnp.max(s, axis=0, keepdims=True)) for s, st in zip(ss, sts)]
    alphas = [jnp.exp2(st[0] - m) for m, st in zip(ms, sts)]
    ps = [_probs(s, m) for s, m in zip(ss, ms)]
    return tuple((m, a * st[1] + _dot(vt, p)) for m, a, st, p, vt in zip(ms, alphas, sts, ps, vts))


def _flash_out(st):
    acc = st[1]
    return acc[0:NSA_HD, :] * (1.0 / acc[NSA_HD:NSA_HD + 1, :])


def _stream_scratch(n):
    return ([pltpu.VMEM((8, TQ), F32), pltpu.VMEM((ACC_ROWS, TQ), F32)] * n
            + [pltpu.VMEM((TQ, TQ), F32)] * (2 * n) + [pltpu.VMEM((TQ, TQ), BF16)] * (2 * n))


def _stream_refs(refs, n):
    st_refs = tuple((refs[2 * c], refs[2 * c + 1]) for c in range(n))
    s, p = refs[2 * n:4 * n], refs[4 * n:6 * n]
    return st_refs, tuple(s[:n]), tuple(s[n:]), tuple(p[:n]), tuple(p[n:])


def _causal_stream(qi, scores, vtile, st_refs, s_a, s_b, p_a, p_b, diag_ready=False):
    n_far = jnp.maximum(qi - 1, 0)
    top = qi - 2

    chains = range(len(st_refs))

    def write(s_refs, kt, near=2):
        for c in chains:
            s_refs[c][...] = scores(jnp.maximum(kt, 0), near, c)

    def pending(p_refs, kt):
        return [_dot(vtile(jnp.maximum(kt, 0), c), p_refs[c][...]) for c in chains]

    def step(s_cur, p_cur, s_next, kt_next, p_prev, kt_prev):
        rescale = []
        for c in chains:
            nxt = scores(jnp.maximum(kt_next, 0), 2, c)
            s_next[c][...] = nxt
            pv = _dot(vtile(jnp.maximum(kt_prev, 0), c), p_prev[c][...])
            st = st_refs[c][0]
            s = s_cur[c][...]
            m_old = st[0:1, :]
            m = jnp.maximum(m_old, jnp.max(s, axis=0, keepdims=True))
            m = m + nxt[0:1, :] * 0.0
            p_cur[c][...] = _probs(s, m)
            st[0:1, :] = m
            rescale.append((jnp.exp2(m_old - m), pv))
        for c in chains:
            acc = st_refs[c][1]
            acc[...] = rescale[c][0] * (acc[...] + rescale[c][1])

    if not diag_ready:
        write(s_a, qi, 0)
    for c in chains:
        s_b[c][...] = scores(jnp.maximum(qi - 1, 0), 1, c)
        st, acc = st_refs[c]
        s = s_a[c][...]
        m = jnp.max(s, axis=0, keepdims=True)
        p_a[c][...] = _probs(s, m)
        st[0:1, :] = m
        acc[...] = jnp.zeros(acc.shape, F32)
    step(s_b, p_b, s_a, top, p_a, qi)

    def pair(j, c):
        kt = top - 2 * j
        step(s_a, p_a, s_b, kt - 1, p_b, kt + 1)
        step(s_b, p_b, s_a, kt - 2, p_a, kt)
        return c
    lax.fori_loop(0, n_far // 2, pair, 0)

    @pl.when(n_far % 2 == 1)
    def _():
        for c, pv in zip(chains, pending(p_b, 1)):
            st, acc = st_refs[c]
            s = s_a[c][...]
            m_old = st[0:1, :]
            m = jnp.maximum(m_old, jnp.max(s, axis=0, keepdims=True))
            st[0:1, :] = m
            acc[...] = jnp.exp2(m_old - m) * (acc[...] + pv) + _dot(vtile(0, c), _probs(s, m))

    @pl.when(n_far % 2 == 0)
    def _():
        last = jnp.where(n_far == 0, qi - 1, 0)
        for c, pv in zip(chains, pending(p_b, last)):
            acc = st_refs[c][1]
            acc[...] = acc[...] + pv
    return tuple((st[0:1, :], acc[...]) for st, acc in st_refs)


def _ktile(ref, kt):
    return ref[pl.ds(pl.multiple_of(kt * TQ, TQ), TQ), :]


def _nsa_kernel(qa_ref, gt_ref, kcmp_ref, vcmpt_ref, bc_ref, ks_ref, vst_ref, kw_ref, vwt_ref, dn_ref,
                ov_ref, o_ref, psum_ref, sel_ref, oacc_ref, sig_ref, *scratch):
    st_refs, s_a, s_b, p_a, p_b = _stream_refs(scratch, NSA_HEADS)
    qi = pl.program_id(1)
    t0 = qi * TQ
    sub_grp = lax.shift_right_arithmetic(lax.broadcasted_iota(jnp.int32, (LANE, 1), 0), 6)
    sig_ref[...] = jax.nn.sigmoid(gt_ref[...])

    def gate_row(c):
        return sig_ref[pl.ds(c, 1), :]

    def masked_q(r, g):
        return jnp.where(sub_grp == g, qa_ref[r].astype(F32), 0.0).astype(BF16)

    n_cmp = kcmp_ref.shape[1] - 1
    crow = lax.broadcasted_iota(jnp.int32, (LANE, 1), 0)
    cmp_end = jnp.where(crow < n_cmp, crow * CMP_STRIDE + CMP_LEN - 1, 1 << 30)
    mask_c = (t0 + lax.broadcasted_iota(jnp.int32, (1, TQ), 1)) >= cmp_end

    heads = [(r, g) for r in range(NSA_R) for g in range(NSA_G)]
    scores_c = [_dot(kcmp_ref[0], masked_q(r, g)) for r, g in heads]
    probs_c = []
    for (r, g), s in zip(heads, scores_c):
        s = jnp.where(mask_c, s + bc_ref[g * NSA_R + r], NEG)
        p = jnp.where(mask_c, jnp.exp2(s - jnp.max(s, axis=0, keepdims=True)), 0.0)
        l = jnp.sum(p, axis=0, keepdims=True)
        probs_c.append(p * jnp.where(l > 0.0, 1.0 / l, 0.0))
    for g in range(NSA_G):
        psum_ref[g] = sum(p for (r, gg), p in zip(heads, probs_c) if gg == g)
    outs_c = [_dot(vcmpt_ref[0], p.astype(BF16)) for p in probs_c]
    for r in range(NSA_R):
        o0, o1 = outs_c[r * NSA_G], outs_c[r * NSA_G + 1]
        oacc_ref[r] = jnp.where(sub_grp == 0, gate_row(r * 3) * o0, gate_row((NSA_R + r) * 3) * o1)

    has_prev = jnp.where(qi >= 1, 0.0, NEG)
    has_wfar = jnp.where(qi >= WINDOW // TQ, 0.0, NEG)
    kt_prev = jnp.maximum(qi - 1, 0)
    kt_wfar = jnp.maximum(qi - WINDOW // TQ, 0)
    wfar_mask = lax.broadcasted_iota(jnp.int32, (TQ, TQ), 0) > lax.broadcasted_iota(jnp.int32, (TQ, TQ), 1)

    def attend_all_heads():
        chains = [(r, g) for r in range(NSA_R) for g in range(NSA_G)]
        qms = [masked_q(r, g) for r, g in chains]
        hids = [g * NSA_R + r for r, g in chains]

        per_tile = TQ // SEL_BLOCK
        key_blk = lax.shift_right_arithmetic(lax.broadcasted_iota(jnp.int32, (TQ, LANE), 0), 6)
        lane_id = lax.broadcasted_iota(jnp.int32, (TQ, LANE), 1)

        def keys_with_block_onehot(kt, g):
            other = (1 - g) * NSA_HD
            onehot = jnp.where(lane_id - other == key_blk, 1.0, 0.0).astype(BF16)
            in_other = (lane_id >= other) & (lane_id < other + NSA_HD)
            return jnp.where(in_other, onehot, _ktile(ks_ref, kt))

        def query_with_mask_rows(kt, c):
            r, g = chains[c]
            own = qa_ref[r, g * NSA_HD:(g + 1) * NSA_HD, :]
            pair = sel_ref[g, pl.ds(pl.multiple_of((kt // 2) * 2 * per_tile, 2 * per_tile), 2 * per_tile), :]
            rows = jnp.where(kt % 2 == 0, pair[0:per_tile], pair[per_tile:2 * per_tile]).astype(BF16)
            other = jnp.concatenate([rows, jnp.zeros((NSA_HD - per_tile, TQ), BF16)], axis=0)
            return jnp.concatenate([own, other] if g == 0 else [other, own], axis=0)

        def near_bias(s, near, c):
            if near == 0:
                return s + dn_ref[hids[c], 0]
            if near == 1:
                return s + dn_ref[hids[c], 1] + has_prev
            return s

        def group_values(ref, kt, c):
            g = chains[c][1]
            return _with_ones(ref[g * NSA_HD:(g + 1) * NSA_HD, pl.ds(pl.multiple_of(kt * TQ, TQ), TQ)])

        def finish(sts, branch):
            for c, (r, g) in enumerate(chains):
                oacc_ref[r, g * NSA_HD:(g + 1) * NSA_HD, :] += gate_row(hids[c] * 3 + branch) * _flash_out(sts[c])

        def slc_scores(kt, near, c):
            return near_bias(_dot(keys_with_block_onehot(kt, chains[c][1]), query_with_mask_rows(kt, c)), near, c)

        def slc_vtile(kt, c):
            return group_values(vst_ref, kt, c)

        win_tiles = ((qi, 0), (kt_prev, 1), (kt_wfar, 2))

        def win_scores(c):
            out = []
            for kt, near in win_tiles:
                s = near_bias(_dot(_ktile(kw_ref, kt), qms[c]), near, c)
                out.append(jnp.where(wfar_mask, s + has_wfar, NEG) if near == 2 else s)
            return out

        sts = []
        ahead = win_scores(0)
        for c in range(len(chains)):
            tiles, ahead = ahead, (win_scores(c + 1) if c + 1 < len(chains) else None)
            st = _flash_first([tiles[0]], [group_values(vwt_ref, qi, c)])
            st = _flash_update([tiles[1]], [group_values(vwt_ref, kt_prev, c)], st)
            sts.append(_flash_update([tiles[2]], [group_values(vwt_ref, kt_wfar, c)], st)[0])
        finish(sts, 2)

        n_sel = ov_ref.shape[0]
        jj = lax.broadcasted_iota(jnp.int32, (n_sel, TQ), 0)
        cur = lax.shift_right_arithmetic(t0 + lax.broadcasted_iota(jnp.int32, (n_sel, TQ), 1), 6)
        valid = jj <= cur
        forced = (jj == 0) | (cur - jj < SEL_FORCED_LOCAL)
        for g in range(NSA_G):
            ps = psum_ref[g]
            hi = ps.astype(BF16)
            rem = ps - hi.astype(F32)
            mid = rem.astype(BF16)
            lo = (rem - mid.astype(F32)).astype(BF16)
            ov = ov_ref[...]
            imp = _dot(ov, hi) + _dot(ov, mid) + _dot(ov, lo)
            score = jnp.where(valid, jnp.where(forced, 1e9, imp), -1e9)
            jf = jj.astype(F32)
            rest = score
            chosen = jnp.zeros((n_sel, TQ), F32)
            for _ in range(min(SEL_TOPN, n_sel)):
                best = jnp.max(rest, axis=0, keepdims=True)
                first = jnp.min(jnp.where(rest == best, jf, float(n_sel)), axis=0, keepdims=True)
                hit = jf == first
                chosen = jnp.where(hit, 1.0, chosen)
                rest = jnp.where(hit, -3e38, rest)
            sel_ref[g] = jnp.where(chosen > 0.5, jnp.where(score > -1e8, 0.0, NEG), NEG)

        finish(_causal_stream(qi, slc_scores, slc_vtile, st_refs, s_a, s_b, p_a, p_b), 1)
    attend_all_heads()

    for r in range(NSA_R):
        o_ref[r] = oacc_ref[r].T.astype(BF16)


def _nsa(qa, gates_t, kcmp, vcmpt, bias_c, ks, vst, kw, vwt, near_a, ov, B, S):
    nq = S // TQ
    T = B * S
    k_spec = pl.BlockSpec((S, LANE), lambda b, q: (b, 0))
    v_spec = pl.BlockSpec((LANE, S), lambda b, q: (0, b))
    full = lambda a: pl.BlockSpec(a.shape, lambda b, q: (0,) * a.ndim)
    cmp_spec = pl.BlockSpec((1, LANE, LANE), lambda b, q: (b, 0, 0))
    qo = pl.BlockSpec((4, TQ, LANE), lambda b, q: (0, b * nq + q, 0))
    return pl.pallas_call(
        _nsa_kernel,
        grid=(B, nq),
        in_specs=[pl.BlockSpec((4, LANE, TQ), lambda b, q: (0, 0, b * nq + q)),
                  pl.BlockSpec((GATE_ROWS, TQ), lambda b, q: (0, b * nq + q)),
                  cmp_spec, cmp_spec,
                  pl.BlockSpec((NSA_HEADS, LANE, TQ), lambda b, q: (0, 0, q)),
                  k_spec, v_spec, k_spec, v_spec,
                  pl.BlockSpec((NSA_HEADS, 2, TQ, TQ), lambda b, q: (0, 0, 0, 0)),
                  full(ov)],
        out_specs=qo,
        out_shape=jax.ShapeDtypeStruct((4, T, LANE), BF16),
        scratch_shapes=[pltpu.VMEM((NSA_G, LANE, TQ), F32), pltpu.VMEM((NSA_G, S // SEL_BLOCK, TQ), F32),
                        pltpu.VMEM((NSA_R, LANE, TQ), F32), pltpu.VMEM((GATE_ROWS, TQ), F32)]
                       + _stream_scratch(NSA_HEADS),
        compiler_params=_cparams(("parallel", "arbitrary")),
        name="nsa",
    )(qa, gates_t, kcmp, vcmpt, bias_c, ks, vst, kw, vwt, near_a, ov)


def _diff_kernel(lq1_ref, lk1_ref, lq2_ref, lk2_ref, sub_ref, qb_ref, kb_ref, vbt_ref, dn_ref, o_ref, *scratch):
    n = DIFF_PAIRS_PER_STEP * 4
    st_refs, s_a, s_b, p_a, p_b = _stream_refs(scratch, n)
    qi = pl.program_id(1)
    chain_of_row = lax.shift_right_arithmetic(lax.broadcasted_iota(jnp.int32, (LANE, 1), 0), 5)
    lam = (jnp.exp(jnp.sum(lq1_ref[...] * lk1_ref[...], axis=-1, keepdims=True))
           - jnp.exp(jnp.sum(lq2_ref[...] * lk2_ref[...], axis=-1, keepdims=True)) + LAMBDA_INIT)
    has_prev = jnp.where(qi >= 1, 0.0, NEG)
    n_steps = DIFF_HEADS // 2 // DIFF_PAIRS_PER_STEP

    def score_fn(step):
        pairs = [step * DIFF_PAIRS_PER_STEP + c // 4 for c in range(n)]
        qs = [qb_ref[step * DIFF_PAIRS_PER_STEP + i].astype(F32) for i in range(DIFF_PAIRS_PER_STEP)]
        qms = [jnp.where(chain_of_row == c % 4, qs[c // 4], 0.0).astype(BF16) for c in range(n)]

        def scores(kt, near, c):
            s = _dot(kb_ref[pairs[c], pl.ds(pl.multiple_of(kt * TQ, TQ), TQ), :], qms[c])
            if near == 2:
                return s
            bias = dn_ref[2 * pairs[c] + (c % 4) // 2, near]
            return s + bias if near == 0 else s + bias + has_prev
        return scores

    def start_diag(step):
        scores = score_fn(step)
        for c in range(n):
            s_a[c][...] = scores(qi, 0, c)

    start_diag(0)

    def step_body(step, carry):
        def vtile(kt, c):
            hh = (c % 4) // 2
            return _with_ones(vbt_ref[step * DIFF_PAIRS_PER_STEP + c // 4, hh * 2 * DIFF_HD:(hh + 1) * 2 * DIFF_HD,
                                      pl.ds(pl.multiple_of(kt * TQ, TQ), TQ)])

        sts = _causal_stream(qi, score_fn(step), vtile, st_refs, s_a, s_b, p_a, p_b, diag_ready=True)
        start_diag(jnp.minimum(step + 1, n_steps - 1))
        for i in range(DIFF_PAIRS_PER_STEP):
            outs = []
            for hh in range(2):
                c = 4 * i + 2 * hh
                o = _flash_out(sts[c]) - lam * _flash_out(sts[c + 1])
                outs.append(o * lax.rsqrt(jnp.mean(o * o, axis=0, keepdims=True) + EPS))
            out = jnp.concatenate(outs, axis=0) * sub_ref[...] * (1.0 - LAMBDA_INIT)
            o_ref[step * DIFF_PAIRS_PER_STEP + i] = out.T.astype(BF16)
        return carry
    lax.fori_loop(0, n_steps, step_body, 0)


def _diff(lq1, lk1, lq2, lk2, sub, qb, kb, vbt, near_b, B, S):
    nq = S // TQ
    T = B * S
    full = lambda a: pl.BlockSpec(a.shape, lambda b, q: (0,) * a.ndim)
    k_spec = pl.BlockSpec((4, S, LANE), lambda b, q: (0, b, 0))
    v_spec = pl.BlockSpec((4, LANE, S), lambda b, q: (0, 0, b))
    qo = pl.BlockSpec((4, TQ, LANE), lambda b, q: (0, b * nq + q, 0))
    return pl.pallas_call(
        _diff_kernel,
        grid=(B, nq),
        in_specs=[full(lq1), full(lk1), full(lq2), full(lk2), full(sub),
                  pl.BlockSpec((4, LANE, TQ), lambda b, q: (0, 0, b * nq + q)), k_spec, v_spec,
                  pl.BlockSpec((DIFF_HEADS, 2, TQ, TQ), lambda b, q: (NSA_HEADS // DIFF_HEADS, 0, 0, 0))],
        out_specs=qo,
        out_shape=jax.ShapeDtypeStruct((4, T, LANE), BF16),
        scratch_shapes=_stream_scratch(DIFF_PAIRS_PER_STEP * 4),
        compiler_params=_cparams(("parallel", "arbitrary")),
        name="diff",
    )(lq1, lk1, lq2, lk2, sub, qb, kb, vbt, near_b)


def _outproj_kernel(x_ref, oa_ref, ob_ref, w_ref, g_ref, wr_ref, br_ref, xt_ref, route_ref, cnt_ref, carry_ref):
    @pl.when(pl.program_id(0) == 0)
    def _():
        carry_ref[...] = jnp.zeros(carry_ref.shape, F32)

    o = jnp.concatenate([oa_ref[r] for r in range(4)] + [ob_ref[r] for r in range(4)], axis=-1)
    h = x_ref[...] + _dot(o, w_ref[...])
    xt_ref[:, 0:D_MODEL] = h
    tn = _rmsnorm(h, g_ref[...]).astype(BF16)
    logits = _dot(tn, wr_ref[...]) + br_ref[...]
    lane = lax.broadcasted_iota(jnp.int32, (1, LANE), 1)
    lane_f = lane.astype(F32)
    is_grp = lane < MOE_GROUPS
    lg = jnp.where(is_grp, logits, NEG)
    mg = jnp.max(lg, axis=-1, keepdims=True)
    zg = jnp.sum(jnp.where(is_grp, jnp.exp(lg - mg), 0.0), axis=-1, keepdims=True)
    g_prob = 1.0 / zg
    g_idx = jnp.min(jnp.where(lg == mg, lane_f, 1e9), axis=-1, keepdims=True)
    lane_grp = jnp.where((lane >= MOE_GROUPS) & (lane < MOE_GROUPS + N_EXPERTS),
                         lax.shift_right_arithmetic(lane - MOE_GROUPS, 3), -1).astype(F32)
    le = jnp.where(lane_grp == g_idx, logits, NEG)
    m1 = jnp.max(le, axis=-1, keepdims=True)
    e1 = jnp.min(jnp.where(le == m1, lane_f, 1e9), axis=-1, keepdims=True)
    le2 = jnp.where(lane_f == e1, NEG, le)
    m2 = jnp.max(le2, axis=-1, keepdims=True)
    e2 = jnp.min(jnp.where(le2 == m2, lane_f, 1e9), axis=-1, keepdims=True)
    ratio = jnp.exp(m2 - m1)
    w1 = g_prob / (1.0 + ratio)
    w2 = w1 * ratio
    xt_ref[:, D_MODEL:D_MODEL + LANE] = jnp.where(lane_f == e1, w1, 0.0) + jnp.where(lane_f == e2, w2, 0.0)

    tm = h.shape[0]
    onehot = jnp.where(lane_f == g_idx, 1.0, 0.0)
    earlier = jnp.where(lax.broadcasted_iota(jnp.int32, (tm, tm), 0) > lax.broadcasted_iota(jnp.int32, (tm, tm), 1),
                        1.0, 0.0).astype(BF16)
    prefix = _dot(earlier, onehot.astype(BF16)) + carry_ref[...]
    rank = jnp.sum(onehot * prefix, axis=-1, keepdims=True)
    carry_ref[...] += jnp.sum(onehot, axis=0, keepdims=True)
    route_ref[...] = jnp.where(lane == 0, g_idx, jnp.where(lane == 1, rank, 0.0))
    cnt_ref[...] = jnp.broadcast_to(carry_ref[...], cnt_ref.shape)


def _out_proj(x2, oa, ob, w, g, wr, br):
    T = x2.shape[0]
    tm = TM_PROJ
    row = lambda i: (i, 0)
    full = lambda a: pl.BlockSpec(a.shape, lambda i: (0,) * a.ndim)
    o4 = pl.BlockSpec((4, tm, LANE), lambda i: (0, i, 0))
    return pl.pallas_call(
        _outproj_kernel,
        grid=(T // tm,),
        in_specs=[pl.BlockSpec((tm, D_MODEL), row), o4, o4, full(w), full(g), full(wr), full(br)],
        out_specs=[pl.BlockSpec((tm, XT_WIDTH), row), pl.BlockSpec((tm, LANE), row),
                   pl.BlockSpec((8, LANE), lambda i: (0, 0))],
        out_shape=[jax.ShapeDtypeStruct((T, XT_WIDTH), F32), jax.ShapeDtypeStruct((T, LANE), F32),
                   jax.ShapeDtypeStruct((8, LANE), F32)],
        scratch_shapes=[pltpu.VMEM((1, LANE), F32)],
        compiler_params=_cparams(("arbitrary",)),
        name="out_proj",
    )(x2, oa, ob, w, g, wr, br)


def _moe_kernel(tg_ref, ok_ref, idx_ref, idxn_ref, xt_hbm, g_ref, wg_ref, wu_ref, wd_ref, y_ref, xbuf, sem,
                xb_ref, cmb_ref, fence_sem):
    i = pl.program_id(0)
    slot = i % 2
    tm = xbuf.shape[1]

    def row_copy(index_ref, r, dst_slot):
        return pltpu.make_async_copy(xt_hbm.at[pl.ds(index_ref[0, 0, r], 1), :],
                                     xbuf.at[dst_slot, pl.ds(r, 1), :], sem.at[dst_slot])

    @pl.when(i == 0)
    def _():
        def body(r, c):
            row_copy(idx_ref, r, 0).start()
            return c
        lax.fori_loop(0, tm, body, 0)

    @pl.when((i == 0) | (ok_ref[jnp.maximum(i - 1, 0)] == 1))
    def _():
        pltpu.make_async_copy(xt_hbm.at[pl.ds(0, tm), :], xbuf.at[slot], sem.at[slot]).wait()

    @pl.when(ok_ref[i] == 0)
    def _():
        y_ref[...] = jnp.zeros(y_ref.shape, F32)

    @pl.when(ok_ref[i] == 1)
    def _():
        lane = lax.broadcasted_iota(jnp.int32, (1, LANE), 1)
        xb_ref[...] = _rmsnorm(xbuf[slot, :, 0:D_MODEL], g_ref[...]).astype(BF16)
        cmb_ref[...] = xbuf[slot, :, D_MODEL:D_MODEL + LANE]
        first_lane = MOE_GROUPS + tg_ref[i] * EPG
        per_expert = -(-tm // (EPG - 1))
        y = jnp.zeros((tm, D_MODEL), F32)
        issued = jnp.int32(0)
        for e in range(EPG):
            x = xb_ref[...]
            a = _dot(x, wg_ref[0, e])
            b = _dot(x, wu_ref[0, e])
            ce = jnp.sum(jnp.where(lane == first_lane + e + issued, cmb_ref[...], 0.0), axis=-1, keepdims=True)
            y = y + _dot(((a * jax.nn.sigmoid(a)) * b * ce).astype(BF16), wd_ref[0, e])
            for r in range(e * per_expert, min((e + 1) * per_expert, tm)):
                row_copy(idxn_ref, r, 1 - slot).start(priority=r % 2)
            if e in ANCHOR_AFTER:
                issued = pl.semaphore_read(fence_sem)
        for blk in range(ROW_SLABS):
            y_ref[pl.ds(blk, tm, stride=ROW_SLABS), :] = y[:, blk * LANE:(blk + 1) * LANE]


def _moe(tile_group, tile_ok, src3, xt, g, wg, wu, wd):
    n_tiles, _, tm = src3.shape
    last = n_tiles - 1
    grid_spec = pltpu.PrefetchScalarGridSpec(
        num_scalar_prefetch=2,
        grid=(n_tiles,),
        in_specs=[pl.BlockSpec((1, 1, tm), lambda i, tg, ok: (i, 0, 0), memory_space=pltpu.SMEM),
                  pl.BlockSpec((1, 1, tm), lambda i, tg, ok: (jnp.minimum(i + 1, last), 0, 0), memory_space=pltpu.SMEM),
                  pl.BlockSpec(memory_space=pl.ANY),
                  pl.BlockSpec((1, D_MODEL), lambda i, tg, ok: (0, 0)),
                  pl.BlockSpec((1, EPG, D_MODEL, EXPERT_FF), lambda i, tg, ok: (tg[i], 0, 0, 0)),
                  pl.BlockSpec((1, EPG, D_MODEL, EXPERT_FF), lambda i, tg, ok: (tg[i], 0, 0, 0)),
                  pl.BlockSpec((1, EPG, EXPERT_FF, D_MODEL), lambda i, tg, ok: (tg[i], 0, 0, 0))],
        out_specs=pl.BlockSpec((tm * ROW_SLABS, LANE), lambda i, tg, ok: (i, 0)),
        scratch_shapes=[pltpu.VMEM((2, tm, XT_WIDTH), F32), pltpu.SemaphoreType.DMA((2,)),
                        pltpu.VMEM((tm, D_MODEL), BF16), pltpu.VMEM((tm, LANE), F32), pltpu.SemaphoreType.REGULAR],
    )
    return pl.pallas_call(
        _moe_kernel,
        grid_spec=grid_spec,
        out_shape=jax.ShapeDtypeStruct((n_tiles * tm * ROW_SLABS, LANE), F32),
        compiler_params=pltpu.CompilerParams(dimension_semantics=("arbitrary",), vmem_limit_bytes=VMEM_LIMIT_MOE),
        name="moe",
    )(tile_group, tile_ok, src3, src3, xt, g, wg, wu, wd)


def _final_kernel(pos_ref, posn_ref, y_hbm, h_ref, gf_ref, o_ref, ybuf, sem):
    i = pl.program_id(0)
    slot = i % 2
    tm = h_ref.shape[0]

    def issue(index_ref, dst_slot):
        group = 8

        def body(j, c):
            for k in range(group):
                r = j * group + k
                src_row = pl.multiple_of(index_ref[0, 0, r], ROW_SLABS)
                pltpu.make_async_copy(y_hbm.at[pl.ds(src_row, ROW_SLABS), :],
                                      ybuf.at[dst_slot, pl.ds(r * ROW_SLABS, ROW_SLABS), :],
                                      sem.at[dst_slot]).start(priority=k % 2)
            return c
        lax.fori_loop(0, tm // group, body, 0)

    @pl.when(i == 0)
    def _():
        issue(pos_ref, 0)

    @pl.when(i + 1 < pl.num_programs(0))
    def _():
        issue(posn_ref, 1 - slot)

    pltpu.make_async_copy(ybuf.at[slot], ybuf.at[slot], sem.at[slot]).wait()
    y = jnp.concatenate([ybuf[slot, pl.ds(blk, tm, stride=ROW_SLABS), :] for blk in range(ROW_SLABS)], axis=-1)
    h = h_ref[...] + y
    o_ref[...] = h * lax.rsqrt(jnp.mean(h * h, axis=-1, keepdims=True) + EPS) * gf_ref[...]


def _final(pos3, y_sorted, h, gf):
    n_tiles, _, tm = pos3.shape
    last = n_tiles - 1
    row = lambda i: (i, 0)
    return pl.pallas_call(
        _final_kernel,
        grid=(n_tiles,),
        in_specs=[pl.BlockSpec((1, 1, tm), lambda i: (i, 0, 0), memory_space=pltpu.SMEM),
                  pl.BlockSpec((1, 1, tm), lambda i: (jnp.minimum(i + 1, last), 0, 0), memory_space=pltpu.SMEM),
                  pl.BlockSpec(memory_space=pl.ANY),
                  pl.BlockSpec((tm, D_MODEL), row),
                  pl.BlockSpec((1, D_MODEL), lambda i: (0, 0))],
        out_specs=pl.BlockSpec((tm, D_MODEL), row),
        out_shape=jax.ShapeDtypeStruct((n_tiles * tm, D_MODEL), F32),
        scratch_shapes=[pltpu.VMEM((2, tm * ROW_SLABS, LANE), F32), pltpu.SemaphoreType.DMA((2,))],
        compiler_params=_cparams(("arbitrary",)),
        name="final",
    )(pos3, pos3, y_sorted, h, gf)


def _qa_perm():
    new = np.arange(NSA_HEADS * NSA_HD)
    r, g, d = new // LANE, (new % LANE) // NSA_HD, new % NSA_HD
    return (g * NSA_R + r) * NSA_HD + d


def _block_diag2(w):
    z = jnp.zeros_like(w)
    return jnp.concatenate([jnp.concatenate([w, z], axis=-1), jnp.concatenate([z, w], axis=-1)], axis=-2)


def kernel(x, rel_bias, ln_mix, w_in, cmp_pos_k, cmp_pos_v, cmp_k_w1, cmp_k_w2, cmp_v_w1, cmp_v_w2,
           diff_lq1, diff_lk1, diff_lq2, diff_lk2, diff_subln, w_out, ln_ffn,
           router_group_w, router_group_b, router_expert_w, router_expert_b,
           exp_w_gate, exp_w_up, exp_w_down, ln_final):
    B, S, D = x.shape
    T = B * S
    assert D == D_MODEL and S % TQ == 0 and S >= WINDOW and T % TM_MOE == 0 and T % TM_FINAL == 0
    x2 = x.reshape(T, D)
    perm = _qa_perm()

    w = w_in[0]
    c_kc, c_vc, c_ks, c_vs, c_kw, c_vw, c_gt = 512, 640, 768, 896, 1024, 1152, 1280
    c_qb = c_gt + N_GATE
    c_kb, c_vb = c_qb + 512, c_qb + 1024
    col = lambda c, n=LANE: w[:, c:c + n]
    w_tok = jnp.concatenate([col(c_kc), col(c_vc), col(c_ks), col(c_kw), col(c_kb, 512)], axis=1).astype(BF16)
    w_feat = jnp.concatenate([w[:, perm], col(c_qb, 512), col(c_vs), col(c_vw), col(c_vb, 512), col(c_gt, N_GATE),
                              jnp.zeros((D, GATE_ROWS - N_GATE), F32)], axis=1).T.astype(BF16)
    qa, kc, vc, ks, kw, qb, kb, vst, vwt, vbt, gates_t = _in_proj(x2, ln_mix[0][None, :], w_tok, w_feat)

    w1k = _block_diag2(cmp_k_w1[0].reshape(CMP_LEN, NSA_HD, CMP_HIDDEN)).astype(BF16)
    w1v = _block_diag2(cmp_v_w1[0].reshape(CMP_LEN, NSA_HD, CMP_HIDDEN)).astype(BF16)
    w2k = _block_diag2(cmp_k_w2[0]).astype(BF16)
    w2v = _block_diag2(cmp_v_w2[0]).astype(BF16)
    posk = jnp.tile(cmp_pos_k[0], (1, NSA_G))
    posv = jnp.tile(cmp_pos_v[0], (1, NSA_G))
    kcmp, vcmpt = _compress(kc, vc, posk, posv, w1k, w1v, w2k, w2v, B, S)

    near, bias_c = _bias_tiles(rel_bias.T, S)

    n_sel = S // SEL_BLOCK
    nrow = S // CMP_STRIDE
    c_start = np.arange(nrow) * CMP_STRIDE
    s_start = np.arange(n_sel) * SEL_BLOCK
    ov = ((c_start[None, :] <= s_start[:, None] + SEL_BLOCK - 1)
          & (c_start[None, :] + CMP_LEN - 1 >= s_start[:, None])
          & (np.arange(nrow)[None, :] < nrow - 1)).astype(np.float32)
    o_a = _nsa(qa, gates_t, kcmp, vcmpt, bias_c, ks, vst, kw, vwt, near, jnp.asarray(ov, BF16), B, S)

    sub = jnp.tile(diff_subln[0], 2)[:, None]
    o_b = _diff(diff_lq1[0][None, :], diff_lk1[0][None, :], diff_lq2[0][None, :], diff_lk2[0][None, :],
                sub, qb, kb, vbt, near, B, S)

    w_o = jnp.concatenate([w_out[0][:512][perm], w_out[0][512:]], axis=0).astype(BF16)
    n_r = MOE_GROUPS + N_EXPERTS
    wr = jnp.concatenate([router_group_w[0], router_expert_w[0], jnp.zeros((D, LANE - n_r), F32)], axis=1).astype(BF16)
    br = jnp.concatenate([router_group_b[0], router_expert_b[0], jnp.zeros((LANE - n_r,), F32)])[None, :]
    xt, route, counts = _out_proj(x2, o_a, o_b, w_o, ln_ffn[0][None, :], wr, br)

    tm = TM_MOE
    n_tiles = T // tm + MOE_GROUPS
    cnt = counts[0, :MOE_GROUPS].astype(jnp.int32)
    ends = jnp.cumsum((cnt + tm - 1) // tm * tm)
    starts = ends - (cnt + tm - 1) // tm * tm
    pos = starts[route[:, 0].astype(jnp.int32)] + route[:, 1].astype(jnp.int32)
    order = jnp.argsort(pos).astype(jnp.int32)
    row = jnp.arange(n_tiles * tm, dtype=jnp.int32)
    group_of = lambda r: sum((r >= ends[g]).astype(jnp.int32) for g in range(MOE_GROUPS - 1))
    row_group = group_of(row)
    local = row - starts[row_group]
    compact = (jnp.cumsum(cnt) - cnt)[row_group] + local
    src = jnp.where(local < cnt[row_group], order[jnp.clip(compact, 0, T - 1)], 0)
    tile_start = jnp.arange(n_tiles, dtype=jnp.int32) * tm
    tile_group = group_of(tile_start)
    tile_ok = (tile_start < ends[-1]).astype(jnp.int32)

    by_group = lambda a: a[0].astype(BF16).reshape((MOE_GROUPS, EPG) + a.shape[2:])
    y_sorted = _moe(tile_group, tile_ok, src.reshape(n_tiles, 1, tm), xt, ln_ffn[0][None, :],
                    by_group(exp_w_gate), by_group(exp_w_up), by_group(exp_w_down))
    out = _final((pos * ROW_SLABS).reshape(T // TM_FINAL, 1, TM_FINAL), y_sorted, xt, ln_final[None, :])
    return out.reshape(B, S, D)
```

```python
import math

import numpy as np
import jax
import jax.numpy as jnp
from jax import lax
from jax.experimental import pallas as pl
from jax.experimental.pallas import tpu as pltpu

F32 = jnp.float32
BF16 = jnp.bfloat16
NEG = -1e30
EPS = 1e-6
LOG2E = math.log2(math.e)

D_MODEL = 1024
LANE = 128
NSA_HEADS, NSA_G, NSA_R, NSA_HD = 8, 2, 4, 64
CMP_LEN, CMP_STRIDE, CMP_HIDDEN = 32, 16, 128
SEL_BLOCK, SEL_TOPN, SEL_FORCED_LOCAL, WINDOW = 64, 8, 2, 512
DIFF_HEADS, DIFF_HD = 8, 32
REL_BUCKETS, REL_MAX_EXACT, REL_MAX_DIST = 32, 16, 128
N_REL_HEADS = NSA_HEADS + DIFF_HEADS
MOE_GROUPS, EPG, N_EXPERTS, EXPERT_FF = 4, 8, 32, 256
LAMBDA_INIT = 0.8 - 0.6 * math.exp(-0.3 * 0)
N_GATE = NSA_HEADS * 3
GATE_ROWS = 32

TQ = 256
DIFF_PAIRS_PER_STEP = 4
TM_PROJ = 512
TM_MOE = 512
TM_FINAL = 512
ANCHOR_AFTER = (1, 3, 5, 6)
XT_WIDTH = D_MODEL + LANE
ROW_SLABS = D_MODEL // LANE
OUT_SUB = 256
X_RING = 3
VMEM_LIMIT = 48 * 1024 * 1024
VMEM_LIMIT_MOE = 56 * 1024 * 1024


def _cparams(sem):
    return pltpu.CompilerParams(dimension_semantics=sem, vmem_limit_bytes=VMEM_LIMIT)


def _dot(a, b):
    return jnp.dot(a, b, preferred_element_type=F32)


def _rmsnorm(x, g):
    return x * lax.rsqrt(jnp.mean(x * x, axis=-1, keepdims=True) + EPS) * g


def _dot_nt(a, b):
    return lax.dot_general(a, b, (((1,), (1,)), ((), ())), preferred_element_type=F32)


def _bucket_thresholds():
    n = np.arange(0, REL_MAX_DIST + 1)
    nf = np.maximum(n, 1).astype(np.float32)
    large = REL_MAX_EXACT + (np.log(nf / np.float32(REL_MAX_EXACT)) / np.float32(math.log(REL_MAX_DIST / REL_MAX_EXACT))
                             * np.float32(REL_BUCKETS - REL_MAX_EXACT)).astype(np.int32)
    large = np.minimum(large, REL_BUCKETS - 1)
    bucket = np.where(n < REL_MAX_EXACT, n, large)
    return [int(np.argmax(bucket >= b)) for b in range(REL_BUCKETS)]


_THR = _bucket_thresholds()


def _inproj_kernel(x_ref, g_ref, w_ref, wt_ref, qa_ref, kc_ref, vc_ref, ks_ref, kw_ref, qb_ref, kb_ref,
                   vst_ref, vwt_ref, vbt_ref, gt_ref):
    x = x_ref[...]
    xn = (x * lax.rsqrt(jnp.mean(x * x, axis=-1, keepdims=True) + EPS) * g_ref[...]).astype(BF16)
    a = _dot(xn, w_ref[:, 0:512])
    kc_ref[...] = a[:, 0:128]
    vc_ref[...] = a[:, 128:256]
    ks_ref[...] = a[:, 256:384].astype(BF16)
    kw_ref[...] = a[:, 384:512].astype(BF16)
    a = _dot(xn, w_ref[:, 512:1024])
    for r in range(4):
        kb_ref[r] = a[:, r * LANE:(r + 1) * LANE].astype(BF16)

    ft = _dot_nt(wt_ref[...], xn)

    def feat(row0, rows=LANE):
        return ft[row0:row0 + rows, :]
    for r in range(4):
        qa_ref[r] = (feat(r * LANE) * (NSA_HD ** -0.5 * LOG2E)).astype(BF16)
        qb_ref[r] = (feat(512 + r * LANE) * (DIFF_HD ** -0.5 * LOG2E)).astype(BF16)
        vbt_ref[r] = feat(1280 + r * LANE).astype(BF16)
    vst_ref[...] = feat(1024).astype(BF16)
    vwt_ref[...] = feat(1152).astype(BF16)
    gt_ref[...] = feat(1792, GATE_ROWS)


def _in_proj(x2, g, w, wt):
    T = x2.shape[0]
    tm = TM_PROJ
    row = lambda i: (i, 0)
    o128b = jax.ShapeDtypeStruct((T, LANE), BF16)
    o128f = jax.ShapeDtypeStruct((T, LANE), F32)
    o4 = jax.ShapeDtypeStruct((4, T, LANE), BF16)
    ot = jax.ShapeDtypeStruct((LANE, T), BF16)
    o4t = jax.ShapeDtypeStruct((4, LANE, T), BF16)
    s128 = pl.BlockSpec((tm, LANE), row)
    s4 = pl.BlockSpec((4, tm, LANE), lambda i: (0, i, 0))
    st = pl.BlockSpec((LANE, tm), lambda i: (0, i))
    s4t = pl.BlockSpec((4, LANE, tm), lambda i: (0, 0, i))
    return pl.pallas_call(
        _inproj_kernel,
        grid=(T // tm,),
        in_specs=[pl.BlockSpec((tm, D_MODEL), row),
                  pl.BlockSpec((1, D_MODEL), lambda i: (0, 0)),
                  pl.BlockSpec(w.shape, lambda i: (0, 0)),
                  pl.BlockSpec(wt.shape, lambda i: (0, 0))],
        out_specs=[s4t, s128, s128, s128, s128, s4t, s4, st, st, s4t, pl.BlockSpec((GATE_ROWS, tm), lambda i: (0, i))],
        out_shape=[o4t, o128f, o128f, o128b, o128b, o4t, o4, ot, ot, o4t, jax.ShapeDtypeStruct((GATE_ROWS, T), F32)],
        compiler_params=_cparams(("parallel",)),
        name="in_proj",
    )(x2, g, w, wt)


def _gelu_tanh(x):
    return 0.5 * x * (1.0 + jnp.tanh(math.sqrt(2.0 / math.pi) * (x + 0.044715 * (x * x * x))))


def _compress_kernel(kc_ref, vc_ref, posk_ref, posv_ref, w1k_ref, w1v_ref, w2k_ref, w2v_ref, ko_ref, vo_ref):
    nrow = kc_ref.shape[0] // CMP_STRIDE
    rid = lax.broadcasted_iota(jnp.int32, (nrow, 1), 0)
    cid = lax.broadcasted_iota(jnp.int32, (1, nrow), 1)
    for src, pos, w1, w2, out, transposed in ((kc_ref, posk_ref, w1k_ref, w2k_ref, ko_ref, False),
                                              (vc_ref, posv_ref, w1v_ref, w2v_ref, vo_ref, True)):
        hid_a = jnp.zeros((nrow, 2 * CMP_HIDDEN), F32)
        hid_b = jnp.zeros((nrow, 2 * CMP_HIDDEN), F32)
        for m in range(CMP_STRIDE):
            y = src[pl.ds(m, nrow, stride=CMP_STRIDE), :]
            hid_a = hid_a + _dot((y + pos[m:m + 1, :]).astype(BF16), w1[m])
            hid_b = hid_b + _dot((y + pos[CMP_STRIDE + m:CMP_STRIDE + m + 1, :]).astype(BF16), w1[CMP_STRIDE + m])
        hid = hid_a + pltpu.roll(hid_b, nrow - 1, 0)
        o = _dot(_gelu_tanh(hid).astype(BF16), w2[...])
        if transposed:
            out[0] = jnp.where(cid < nrow - 1, o.T, 0.0).astype(BF16)
        else:
            out[0] = jnp.where(rid < nrow - 1, o, 0.0).astype(BF16)


def _compress(kc, vc, posk, posv, w1k, w1v, w2k, w2v, B, S):
    nrow = S // CMP_STRIDE
    assert nrow == LANE
    full = lambda a: pl.BlockSpec(a.shape, lambda b: (0,) * a.ndim)
    src = pl.BlockSpec((S, LANE), lambda b: (b, 0))
    osp = pl.BlockSpec((1, nrow, LANE), lambda b: (b, 0, 0))
    osh = jax.ShapeDtypeStruct((B, nrow, LANE), BF16)
    return pl.pallas_call(
        _compress_kernel,
        grid=(B,),
        in_specs=[src, src, full(posk), full(posv), full(w1k), full(w1v), full(w2k), full(w2v)],
        out_specs=[osp, osp],
        out_shape=[osh, osh],
        compiler_params=_cparams(("parallel",)),
        name="compress",
    )(kc, vc, posk, posv, w1k, w1v, w2k, w2v)


def _bias_from_dist(dist, tab_ref, h):
    val = jnp.full(dist.shape, tab_ref[h, 0], F32)
    for b in range(1, REL_BUCKETS):
        val = jnp.where(dist >= _THR[b], tab_ref[h, b], val)
    return val


def _bias_near_kernel(tab_ref, out_ref):
    h = pl.program_id(0)
    j = lax.broadcasted_iota(jnp.int32, (TQ, TQ), 0)
    i = lax.broadcasted_iota(jnp.int32, (TQ, TQ), 1)
    for d in range(2):
        bias = (_bias_from_dist(i - j + d * TQ, tab_ref, h) - tab_ref[h, REL_BUCKETS - 1]) * LOG2E
        out_ref[0, d] = jnp.where(i - j + d * TQ >= 0, bias, NEG)


def _bias_cmp_kernel(tab_ref, out_ref):
    h = pl.program_id(0)
    band = TQ // CMP_STRIDE
    out_ref[0] = jnp.full(out_ref.shape[1:], tab_ref[h, REL_BUCKETS - 1] * LOG2E, F32)
    for qi in range(out_ref.shape[2] // TQ):
        c0 = max(qi - 1, 0) * band
        c = c0 + lax.broadcasted_iota(jnp.int32, (2 * band, TQ), 0)
        t = qi * TQ + lax.broadcasted_iota(jnp.int32, (2 * band, TQ), 1)
        out_ref[0, c0:c0 + 2 * band, qi * TQ:(qi + 1) * TQ] = (
            _bias_from_dist(t - (c * CMP_STRIDE + CMP_LEN - 1), tab_ref, h) * LOG2E)


def _bias_tiles(tab_t, S):
    smem = pl.BlockSpec(memory_space=pltpu.SMEM)
    near = pl.pallas_call(
        _bias_near_kernel,
        grid=(N_REL_HEADS,),
        in_specs=[smem],
        out_specs=pl.BlockSpec((1, 2, TQ, TQ), lambda h: (h, 0, 0, 0)),
        out_shape=jax.ShapeDtypeStruct((N_REL_HEADS, 2, TQ, TQ), F32),
        compiler_params=_cparams(("parallel",)),
        name="bias_near",
    )(tab_t)
    cmp_bias = pl.pallas_call(
        _bias_cmp_kernel,
        grid=(NSA_HEADS,),
        in_specs=[smem],
        out_specs=pl.BlockSpec((1, LANE, S), lambda h: (h, 0, 0)),
        out_shape=jax.ShapeDtypeStruct((NSA_HEADS, LANE, S), F32),
        compiler_params=_cparams(("parallel",)),
        name="bias_cmp",
    )(tab_t)
    return near, cmp_bias


ACC_ROWS = NSA_HD + 16


def _with_ones(vt):
    return jnp.concatenate([vt, jnp.ones((ACC_ROWS - vt.shape[0], vt.shape[1]), BF16)], axis=0)


def _probs(s, m):
    return jnp.exp2((s - m).astype(BF16))


def _flash_first(ss, vts):
    ms = [jnp.max(s, axis=0, keepdims=True) for s in ss]
    ps = [_probs(s, m) for s, m in zip(ss, ms)]
    return tuple((m, _dot(vt, p)) for m, p, vt in zip(ms, ps, vts))


def _flash_update(ss, vts, sts):
    ms = [jnp.maximum(st[0], jnp.max(s, axis=0, keepdims=True)) for s, st in zip(ss, sts)]
    alphas = [jnp.exp2(st[0] - m) for m, st in zip(ms, sts)]
    ps = [_probs(s, m) for s, m in zip(ss, ms)]
    return tuple((m, a * st[1] + _dot(vt, p)) for m, a, st, p, vt in zip(ms, alphas, sts, ps, vts))


def _flash_out(st):
    acc = st[1]
    return acc[0:NSA_HD, :] * (1.0 / acc[NSA_HD:NSA_HD + 1, :])


def _stream_scratch(n):
    return ([pltpu.VMEM((8, TQ), F32), pltpu.VMEM((ACC_ROWS, TQ), F32)] * n
            + [pltpu.VMEM((TQ, TQ), F32)] * (2 * n) + [pltpu.VMEM((TQ, TQ), BF16)] * (2 * n))


def _stream_refs(refs, n):
    st_refs = tuple((refs[2 * c], refs[2 * c + 1]) for c in range(n))
    s, p = refs[2 * n:4 * n], refs[4 * n:6 * n]
    return st_refs, tuple(s[:n]), tuple(s[n:]), tuple(p[:n]), tuple(p[n:])


def _causal_stream(qi, scores, vtile, st_refs, s_a, s_b, p_a, p_b, diag_ready=False):
    n_far = jnp.maximum(qi - 1, 0)
    top = qi - 2

    chains = range(len(st_refs))

    def write(s_refs, kt, near=2):
        for c in chains:
            s_refs[c][...] = scores(jnp.maximum(kt, 0), near, c)

    def pending(p_refs, kt):
        return [_dot(vtile(jnp.maximum(kt, 0), c), p_refs[c][...]) for c in chains]

    def step(s_cur, p_cur, s_next, kt_next, p_prev, kt_prev):
        rescale = []
        for c in chains:
            nxt = scores(jnp.maximum(kt_next, 0), 2, c)
            s_next[c][...] = nxt
            pv = _dot(vtile(jnp.maximum(kt_prev, 0), c), p_prev[c][...])
            st = st_refs[c][0]
            s = s_cur[c][...]
            m_old = st[0:1, :]
            m = jnp.maximum(m_old, jnp.max(s, axis=0, keepdims=True))
            m = m + nxt[0:1, :] * 0.0
            p_cur[c][...] = _probs(s, m)
            st[0:1, :] = m
            rescale.append((jnp.exp2(m_old - m), pv))
        for c in chains:
            acc = st_refs[c][1]
            acc[...] = rescale[c][0] * (acc[...] + rescale[c][1])

    if not diag_ready:
        write(s_a, qi, 0)
    for c in chains:
        s_b[c][...] = scores(jnp.maximum(qi - 1, 0), 1, c)
        st, acc = st_refs[c]
        s = s_a[c][...]
        m = jnp.max(s, axis=0, keepdims=True)
        p_a[c][...] = _probs(s, m)
        st[0:1, :] = m
        acc[...] = jnp.zeros(acc.shape, F32)
    step(s_b, p_b, s_a, top, p_a, qi)

    def pair(j, c):
        kt = top - 2 * j
        step(s_a, p_a, s_b, kt - 1, p_b, kt + 1)
        step(s_b, p_b, s_a, kt - 2, p_a, kt)
        return c
    lax.fori_loop(0, n_far // 2, pair, 0)

    @pl.when(n_far % 2 == 1)
    def _():
        for c, pv in zip(chains, pending(p_b, 1)):
            st, acc = st_refs[c]
            s = s_a[c][...]
            m_old = st[0:1, :]
            m = jnp.maximum(m_old, jnp.max(s, axis=0, keepdims=True))
            st[0:1, :] = m
            acc[...] = jnp.exp2(m_old - m) * (acc[...] + pv) + _dot(vtile(0, c), _probs(s, m))

    @pl.when(n_far % 2 == 0)
    def _():
        last = jnp.where(n_far == 0, qi - 1, 0)
        for c, pv in zip(chains, pending(p_b, last)):
            acc = st_refs[c][1]
            acc[...] = acc[...] + pv
    return tuple((st[0:1, :], acc[...]) for st, acc in st_refs)


def _ktile(ref, kt):
    return ref[pl.ds(pl.multiple_of(kt * TQ, TQ), TQ), :]


def _nsa_kernel(qa_ref, gt_ref, kcmp_ref, vcmpt_ref, bc_ref, ks_ref, vst_ref, kw_ref, vwt_ref, dn_ref,
                ov_ref, o_ref, psum_ref, sel_ref, oacc_ref, sig_ref, *scratch):
    st_refs, s_a, s_b, p_a, p_b = _stream_refs(scratch, NSA_HEADS)
    qi = pl.program_id(1)
    t0 = qi * TQ
    sub_grp = lax.shift_right_arithmetic(lax.broadcasted_iota(jnp.int32, (LANE, 1), 0), 6)
    sig_ref[...] = jax.nn.sigmoid(gt_ref[...])

    def gate_row(c):
        return sig_ref[pl.ds(c, 1), :]

    def masked_q(r, g):
        return jnp.where(sub_grp == g, qa_ref[r].astype(F32), 0.0).astype(BF16)

    n_cmp = kcmp_ref.shape[1] - 1
    crow = lax.broadcasted_iota(jnp.int32, (LANE, 1), 0)
    cmp_end = jnp.where(crow < n_cmp, crow * CMP_STRIDE + CMP_LEN - 1, 1 << 30)
    mask_c = (t0 + lax.broadcasted_iota(jnp.int32, (1, TQ), 1)) >= cmp_end

    heads = [(r, g) for r in range(NSA_R) for g in range(NSA_G)]
    scores_c = [_dot(kcmp_ref[0], masked_q(r, g)) for r, g in heads]
    probs_c = []
    for (r, g), s in zip(heads, scores_c):
        s = jnp.where(mask_c, s + bc_ref[g * NSA_R + r], NEG)
        p = jnp.where(mask_c, jnp.exp2(s - jnp.max(s, axis=0, keepdims=True)), 0.0)
        l = jnp.sum(p, axis=0, keepdims=True)
        probs_c.append(p * jnp.where(l > 0.0, 1.0 / l, 0.0))
    for g in range(NSA_G):
        psum_ref[g] = sum(p for (r, gg), p in zip(heads, probs_c) if gg == g)
    outs_c = [_dot(vcmpt_ref[0], p.astype(BF16)) for p in probs_c]
    for r in range(NSA_R):
        o0, o1 = outs_c[r * NSA_G], outs_c[r * NSA_G + 1]
        oacc_ref[r] = jnp.where(sub_grp == 0, gate_row(r * 3) * o0, gate_row((NSA_R + r) * 3) * o1)

    has_prev = jnp.where(qi >= 1, 0.0, NEG)
    has_wfar = jnp.where(qi >= WINDOW // TQ, 0.0, NEG)
    kt_prev = jnp.maximum(qi - 1, 0)
    kt_wfar = jnp.maximum(qi - WINDOW // TQ, 0)
    wfar_mask = lax.broadcasted_iota(jnp.int32, (TQ, TQ), 0) > lax.broadcasted_iota(jnp.int32, (TQ, TQ), 1)

    def attend_all_heads():
        chains = [(r, g) for r in range(NSA_R) for g in range(NSA_G)]
        qms = [masked_q(r, g) for r, g in chains]
        hids = [g * NSA_R + r for r, g in chains]

        per_tile = TQ // SEL_BLOCK
        key_blk = lax.shift_right_arithmetic(lax.broadcasted_iota(jnp.int32, (TQ, LANE), 0), 6)
        lane_id = lax.broadcasted_iota(jnp.int32, (TQ, LANE), 1)

        def keys_with_block_onehot(kt, g):
            other = (1 - g) * NSA_HD
            onehot = jnp.where(lane_id - other == key_blk, 1.0, 0.0).astype(BF16)
            in_other = (lane_id >= other) & (lane_id < other + NSA_HD)
            return jnp.where(in_other, onehot, _ktile(ks_ref, kt))

        def query_with_mask_rows(kt, c):
            r, g = chains[c]
            own = qa_ref[r, g * NSA_HD:(g + 1) * NSA_HD, :]
            pair = sel_ref[g, pl.ds(pl.multiple_of((kt // 2) * 2 * per_tile, 2 * per_tile), 2 * per_tile), :]
            rows = jnp.where(kt % 2 == 0, pair[0:per_tile], pair[per_tile:2 * per_tile]).astype(BF16)
            other = jnp.concatenate([rows, jnp.zeros((NSA_HD - per_tile, TQ), BF16)], axis=0)
            return jnp.concatenate([own, other] if g == 0 else [other, own], axis=0)

        def near_bias(s, near, c):
            if near == 0:
                return s + dn_ref[hids[c], 0]
            if near == 1:
                return s + dn_ref[hids[c], 1] + has_prev
            return s

        def group_values(ref, kt, c):
            g = chains[c][1]
            return _with_ones(ref[g * NSA_HD:(g + 1) * NSA_HD, pl.ds(pl.multiple_of(kt * TQ, TQ), TQ)])

        def finish(sts, branch):
            for c, (r, g) in enumerate(chains):
                oacc_ref[r, g * NSA_HD:(g + 1) * NSA_HD, :] += gate_row(hids[c] * 3 + branch) * _flash_out(sts[c])

        def slc_scores(kt, near, c):
            return near_bias(_dot(keys_with_block_onehot(kt, chains[c][1]), query_with_mask_rows(kt, c)), near, c)

        def slc_vtile(kt, c):
            return group_values(vst_ref, kt, c)

        win_tiles = ((qi, 0), (kt_prev, 1), (kt_wfar, 2))

        def win_scores(c):
            out = []
            for kt, near in win_tiles:
                s = near_bias(_dot(_ktile(kw_ref, kt), qms[c]), near, c)
                out.append(jnp.where(wfar_mask, s + has_wfar, NEG) if near == 2 else s)
            return out

        sts = []
        ahead = win_scores(0)
        for c in range(len(chains)):
            tiles, ahead = ahead, (win_scores(c + 1) if c + 1 < len(chains) else None)
            st = _flash_first([tiles[0]], [group_values(vwt_ref, qi, c)])
            st = _flash_update([tiles[1]], [group_values(vwt_ref, kt_prev, c)], st)
            sts.append(_flash_update([tiles[2]], [group_values(vwt_ref, kt_wfar, c)], st)[0])
        finish(sts, 2)

        n_sel = ov_ref.shape[0]
        jj = lax.broadcasted_iota(jnp.int32, (n_sel, TQ), 0)
        cur = lax.shift_right_arithmetic(t0 + lax.broadcasted_iota(jnp.int32, (n_sel, TQ), 1), 6)
        valid = jj <= cur
        forced = (jj == 0) | (cur - jj < SEL_FORCED_LOCAL)
        for g in range(NSA_G):
            ps = psum_ref[g]
            hi = ps.astype(BF16)
            rem = ps - hi.astype(F32)
            mid = rem.astype(BF16)
            lo = (rem - mid.astype(F32)).astype(BF16)
            ov = ov_ref[...]
            imp = _dot(ov, hi) + _dot(ov, mid) + _dot(ov, lo)
            score = jnp.where(valid, jnp.where(forced, 1e9, imp), -1e9)
            jf = jj.astype(F32)
            rest = score
            chosen = jnp.zeros((n_sel, TQ), F32)
            for _ in range(min(SEL_TOPN, n_sel)):
                best = jnp.max(rest, axis=0, keepdims=True)
                first = jnp.min(jnp.where(rest == best, jf, float(n_sel)), axis=0, keepdims=True)
                hit = jf == first
                chosen = jnp.where(hit, 1.0, chosen)
                rest = jnp.where(hit, -3e38, rest)
            sel_ref[g] = jnp.where(chosen > 0.5, jnp.where(score > -1e8, 0.0, NEG), NEG)

        finish(_causal_stream(qi, slc_scores, slc_vtile, st_refs, s_a, s_b, p_a, p_b), 1)
    attend_all_heads()

    for r in range(NSA_R):
        o_ref[r] = oacc_ref[r].T.astype(BF16)


def _nsa(qa, gates_t, kcmp, vcmpt, bias_c, ks, vst, kw, vwt, near_a, ov, B, S):
    nq = S // TQ
    T = B * S
    k_spec = pl.BlockSpec((S, LANE), lambda b, q: (b, 0))
    v_spec = pl.BlockSpec((LANE, S), lambda b, q: (0, b))
    full = lambda a: pl.BlockSpec(a.shape, lambda b, q: (0,) * a.ndim)
    cmp_spec = pl.BlockSpec((1, LANE, LANE), lambda b, q: (b, 0, 0))
    qo = pl.BlockSpec((4, TQ, LANE), lambda b, q: (0, b * nq + q, 0))
    return pl.pallas_call(
        _nsa_kernel,
        grid=(B, nq),
        in_specs=[pl.BlockSpec((4, LANE, TQ), lambda b, q: (0, 0, b * nq + q)),
                  pl.BlockSpec((GATE_ROWS, TQ), lambda b, q: (0, b * nq + q)),
                  cmp_spec, cmp_spec,
                  pl.BlockSpec((NSA_HEADS, LANE, TQ), lambda b, q: (0, 0, q)),
                  k_spec, v_spec, k_spec, v_spec,
                  pl.BlockSpec((NSA_HEADS, 2, TQ, TQ), lambda b, q: (0, 0, 0, 0)),
                  full(ov)],
        out_specs=qo,
        out_shape=jax.ShapeDtypeStruct((4, T, LANE), BF16),
        scratch_shapes=[pltpu.VMEM((NSA_G, LANE, TQ), F32), pltpu.VMEM((NSA_G, S // SEL_BLOCK, TQ), F32),
                        pltpu.VMEM((NSA_R, LANE, TQ), F32), pltpu.VMEM((GATE_ROWS, TQ), F32)]
                       + _stream_scratch(NSA_HEADS),
        compiler_params=_cparams(("parallel", "arbitrary")),
        name="nsa",
    )(qa, gates_t, kcmp, vcmpt, bias_c, ks, vst, kw, vwt, near_a, ov)


def _diff_kernel(lq1_ref, lk1_ref, lq2_ref, lk2_ref, sub_ref, qb_ref, kb_ref, vbt_ref, dn_ref, o_ref, *scratch):
    n = DIFF_PAIRS_PER_STEP * 4
    st_refs, s_a, s_b, p_a, p_b = _stream_refs(scratch, n)
    qi = pl.program_id(1)
    chain_of_row = lax.shift_right_arithmetic(lax.broadcasted_iota(jnp.int32, (LANE, 1), 0), 5)
    lam = (jnp.exp(jnp.sum(lq1_ref[...] * lk1_ref[...], axis=-1, keepdims=True))
           - jnp.exp(jnp.sum(lq2_ref[...] * lk2_ref[...], axis=-1, keepdims=True)) + LAMBDA_INIT)
    has_prev = jnp.where(qi >= 1, 0.0, NEG)
    n_steps = DIFF_HEADS // 2 // DIFF_PAIRS_PER_STEP

    def score_fn(step):
        pairs = [step * DIFF_PAIRS_PER_STEP + c // 4 for c in range(n)]
        qs = [qb_ref[step * DIFF_PAIRS_PER_STEP + i].astype(F32) for i in range(DIFF_PAIRS_PER_STEP)]
        qms = [jnp.where(chain_of_row == c % 4, qs[c // 4], 0.0).astype(BF16) for c in range(n)]

        def scores(kt, near, c):
            s = _dot(kb_ref[pairs[c], pl.ds(pl.multiple_of(kt * TQ, TQ), TQ), :], qms[c])
            if near == 2:
                return s
            bias = dn_ref[2 * pairs[c] + (c % 4) // 2, near]
            return s + bias if near == 0 else s + bias + has_prev
        return scores

    def start_diag(step):
        scores = score_fn(step)
        for c in range(n):
            s_a[c][...] = scores(qi, 0, c)

    start_diag(0)

    def step_body(step, carry):
        def vtile(kt, c):
            hh = (c % 4) // 2
            return _with_ones(vbt_ref[step * DIFF_PAIRS_PER_STEP + c // 4, hh * 2 * DIFF_HD:(hh + 1) * 2 * DIFF_HD,
                                      pl.ds(pl.multiple_of(kt * TQ, TQ), TQ)])

        sts = _causal_stream(qi, score_fn(step), vtile, st_refs, s_a, s_b, p_a, p_b, diag_ready=True)
        start_diag(jnp.minimum(step + 1, n_steps - 1))
        for i in range(DIFF_PAIRS_PER_STEP):
            outs = []
            for hh in range(2):
                c = 4 * i + 2 * hh
                o = _flash_out(sts[c]) - lam * _flash_out(sts[c + 1])
                outs.append(o * lax.rsqrt(jnp.mean(o * o, axis=0, keepdims=True) + EPS))
            out = jnp.concatenate(outs, axis=0) * sub_ref[...] * (1.0 - LAMBDA_INIT)
            o_ref[step * DIFF_PAIRS_PER_STEP + i] = out.T.astype(BF16)
        return carry
    lax.fori_loop(0, n_steps, step_body, 0)


def _diff(lq1, lk1, lq2, lk2, sub, qb, kb, vbt, near_b, B, S):
    nq = S // TQ
    T = B * S
    full = lambda a: pl.BlockSpec(a.shape, lambda b, q: (0,) * a.ndim)
    k_spec = pl.BlockSpec((4, S, LANE), lambda b, q: (0, b, 0))
    v_spec = pl.BlockSpec((4, LANE, S), lambda b, q: (0, 0, b))
    qo = pl.BlockSpec((4, TQ, LANE), lambda b, q: (0, b * nq + q, 0))
    return pl.pallas_call(
        _diff_kernel,
        grid=(B, nq),
        in_specs=[full(lq1), full(lk1), full(lq2), full(lk2), full(sub),
                  pl.BlockSpec((4, LANE, TQ), lambda b, q: (0, 0, b * nq + q)), k_spec, v_spec,
                  pl.BlockSpec((DIFF_HEADS, 2, TQ, TQ), lambda b, q: (NSA_HEADS // DIFF_HEADS, 0, 0, 0))],
        out_specs=qo,
        out_shape=jax.ShapeDtypeStruct((4, T, LANE), BF16),
        scratch_shapes=_stream_scratch(DIFF_PAIRS_PER_STEP * 4),
        compiler_params=_cparams(("parallel", "arbitrary")),
        name="diff",
    )(lq1, lk1, lq2, lk2, sub, qb, kb, vbt, near_b)


def _outproj_kernel(x_hbm, oa_ref, ob_ref, w_ref, g_ref, wr_ref, br_ref, xt_ref, route_ref, cnt_ref, carry_ref,
                    xbuf, xsem):
    i = pl.program_id(0)
    tm = xbuf.shape[1]

    def x_copy(step):
        slot = step % X_RING
        return pltpu.make_async_copy(x_hbm.at[pl.ds(step * tm, tm), :], xbuf.at[slot], xsem.at[slot])

    @pl.when(i == 0)
    def _():
        carry_ref[...] = jnp.zeros(carry_ref.shape, F32)
        for s in range(X_RING - 1):
            x_copy(s).start()

    @pl.when(i + (X_RING - 1) < pl.num_programs(0))
    def _():
        x_copy(i + (X_RING - 1)).start()

    x_copy(i).wait()
    carry = carry_ref[...]
    for r0 in range(0, tm, OUT_SUB):
        rows = pl.ds(r0, OUT_SUB)
        o = jnp.concatenate([oa_ref[r, rows, :] for r in range(4)] + [ob_ref[r, rows, :] for r in range(4)], axis=-1)
        h = xbuf[i % X_RING, rows, :] + _dot(o, w_ref[...])
        xt_ref[rows, 0:D_MODEL] = h
        tn = _rmsnorm(h, g_ref[...]).astype(BF16)
        logits = _dot(tn, wr_ref[...]) + br_ref[...]
        weights, route, carry = _route_rows(logits, carry)
        xt_ref[rows, D_MODEL:D_MODEL + LANE] = weights
        route_ref[rows, :] = route
    carry_ref[...] = carry
    cnt_ref[...] = jnp.broadcast_to(carry, cnt_ref.shape)


def _route_rows(logits, carry):
    lane = lax.broadcasted_iota(jnp.int32, (1, LANE), 1)
    lane_f = lane.astype(F32)
    is_grp = lane < MOE_GROUPS
    lg = jnp.where(is_grp, logits, NEG)
    mg = jnp.max(lg, axis=-1, keepdims=True)
    zg = jnp.sum(jnp.where(is_grp, jnp.exp(lg - mg), 0.0), axis=-1, keepdims=True)
    g_prob = 1.0 / zg
    g_idx = jnp.min(jnp.where(lg == mg, lane_f, 1e9), axis=-1, keepdims=True)
    lane_grp = jnp.where((lane >= MOE_GROUPS) & (lane < MOE_GROUPS + N_EXPERTS),
                         lax.shift_right_arithmetic(lane - MOE_GROUPS, 3), -1).astype(F32)
    le = jnp.where(lane_grp == g_idx, logits, NEG)
    m1 = jnp.max(le, axis=-1, keepdims=True)
    e1 = jnp.min(jnp.where(le == m1, lane_f, 1e9), axis=-1, keepdims=True)
    le2 = jnp.where(lane_f == e1, NEG, le)
    m2 = jnp.max(le2, axis=-1, keepdims=True)
    e2 = jnp.min(jnp.where(le2 == m2, lane_f, 1e9), axis=-1, keepdims=True)
    ratio = jnp.exp(m2 - m1)
    w1 = g_prob / (1.0 + ratio)
    w2 = w1 * ratio
    weights = jnp.where(lane_f == e1, w1, 0.0) + jnp.where(lane_f == e2, w2, 0.0)

    tm = logits.shape[0]
    onehot = jnp.where(lane_f == g_idx, 1.0, 0.0)
    earlier = jnp.where(lax.broadcasted_iota(jnp.int32, (tm, tm), 0) > lax.broadcasted_iota(jnp.int32, (tm, tm), 1),
                        1.0, 0.0).astype(BF16)
    prefix = _dot(earlier, onehot.astype(BF16)) + carry
    rank = jnp.sum(onehot * prefix, axis=-1, keepdims=True)
    route = jnp.where(lane == 0, g_idx, jnp.where(lane == 1, rank, 0.0))
    return weights, route, carry + jnp.sum(onehot, axis=0, keepdims=True)


def _out_proj(x2, oa, ob, w, g, wr, br):
    T = x2.shape[0]
    tm = TM_PROJ
    row = lambda i: (i, 0)
    full = lambda a: pl.BlockSpec(a.shape, lambda i: (0,) * a.ndim)
    o4 = pl.BlockSpec((4, tm, LANE), lambda i: (0, i, 0))
    assert T // tm >= X_RING - 1
    return pl.pallas_call(
        _outproj_kernel,
        grid=(T // tm,),
        in_specs=[pl.BlockSpec(memory_space=pl.ANY), o4, o4, full(w), full(g), full(wr), full(br)],
        out_specs=[pl.BlockSpec((tm, XT_WIDTH), row), pl.BlockSpec((tm, LANE), row),
                   pl.BlockSpec((8, LANE), lambda i: (0, 0))],
        out_shape=[jax.ShapeDtypeStruct((T, XT_WIDTH), F32), jax.ShapeDtypeStruct((T, LANE), F32),
                   jax.ShapeDtypeStruct((8, LANE), F32)],
        scratch_shapes=[pltpu.VMEM((1, LANE), F32), pltpu.VMEM((X_RING, tm, D_MODEL), x2.dtype),
                        pltpu.SemaphoreType.DMA((X_RING,))],
        compiler_params=_cparams(("arbitrary",)),
        name="out_proj",
    )(x2, oa, ob, w, g, wr, br)


def _moe_kernel(tg_ref, ok_ref, idx_ref, idxn_ref, xt_hbm, g_ref, wg_ref, wu_ref, wd_ref, y_ref, xbuf, sem,
                xb_ref, cmb_ref, fence_sem):
    i = pl.program_id(0)
    slot = i % 2
    tm = xbuf.shape[1]

    def row_copy(index_ref, r, dst_slot):
        return pltpu.make_async_copy(xt_hbm.at[pl.ds(index_ref[0, 0, r], 1), :],
                                     xbuf.at[dst_slot, pl.ds(r, 1), :], sem.at[dst_slot])

    @pl.when(i == 0)
    def _():
        def body(r, c):
            row_copy(idx_ref, r, 0).start()
            return c
        lax.fori_loop(0, tm, body, 0)

    @pl.when((i == 0) | (ok_ref[jnp.maximum(i - 1, 0)] == 1))
    def _():
        pltpu.make_async_copy(xt_hbm.at[pl.ds(0, tm), :], xbuf.at[slot], sem.at[slot]).wait()

    @pl.when(ok_ref[i] == 0)
    def _():
        y_ref[...] = jnp.zeros(y_ref.shape, F32)

    @pl.when(ok_ref[i] == 1)
    def _():
        lane = lax.broadcasted_iota(jnp.int32, (1, LANE), 1)
        xb_ref[...] = _rmsnorm(xbuf[slot, :, 0:D_MODEL], g_ref[...]).astype(BF16)
        cmb_ref[...] = xbuf[slot, :, D_MODEL:D_MODEL + LANE]
        first_lane = MOE_GROUPS + tg_ref[i] * EPG
        per_expert = -(-tm // (EPG - 1))
        y = jnp.zeros((tm, D_MODEL), F32)
        issued = jnp.int32(0)
        for e in range(EPG):
            x = xb_ref[...]
            a = _dot(x, wg_ref[0, e])
            b = _dot(x, wu_ref[0, e])
            ce = jnp.sum(jnp.where(lane == first_lane + e + issued, cmb_ref[...], 0.0), axis=-1, keepdims=True)
            y = y + _dot(((a * jax.nn.sigmoid(a)) * b * ce).astype(BF16), wd_ref[0, e])
            for r in range(e * per_expert, min((e + 1) * per_expert, tm)):
                row_copy(idxn_ref, r, 1 - slot).start(priority=r % 2)
            if e in ANCHOR_AFTER:
                issued = pl.semaphore_read(fence_sem)
        for blk in range(ROW_SLABS):
            y_ref[pl.ds(blk, tm, stride=ROW_SLABS), :] = y[:, blk * LANE:(blk + 1) * LANE]


def _moe(tile_group, tile_ok, src3, xt, g, wg, wu, wd):
    n_tiles, _, tm = src3.shape
    last = n_tiles - 1
    grid_spec = pltpu.PrefetchScalarGridSpec(
        num_scalar_prefetch=2,
        grid=(n_tiles,),
        in_specs=[pl.BlockSpec((1, 1, tm), lambda i, tg, ok: (i, 0, 0), memory_space=pltpu.SMEM),
                  pl.BlockSpec((1, 1, tm), lambda i, tg, ok: (jnp.minimum(i + 1, last), 0, 0), memory_space=pltpu.SMEM),
                  pl.BlockSpec(memory_space=pl.ANY),
                  pl.BlockSpec((1, D_MODEL), lambda i, tg, ok: (0, 0)),
                  pl.BlockSpec((1, EPG, D_MODEL, EXPERT_FF), lambda i, tg, ok: (tg[i], 0, 0, 0)),
                  pl.BlockSpec((1, EPG, D_MODEL, EXPERT_FF), lambda i, tg, ok: (tg[i], 0, 0, 0)),
                  pl.BlockSpec((1, EPG, EXPERT_FF, D_MODEL), lambda i, tg, ok: (tg[i], 0, 0, 0))],
        out_specs=pl.BlockSpec((tm * ROW_SLABS, LANE), lambda i, tg, ok: (i, 0)),
        scratch_shapes=[pltpu.VMEM((2, tm, XT_WIDTH), F32), pltpu.SemaphoreType.DMA((2,)),
                        pltpu.VMEM((tm, D_MODEL), BF16), pltpu.VMEM((tm, LANE), F32), pltpu.SemaphoreType.REGULAR],
    )
    return pl.pallas_call(
        _moe_kernel,
        grid_spec=grid_spec,
        out_shape=jax.ShapeDtypeStruct((n_tiles * tm * ROW_SLABS, LANE), F32),
        compiler_params=pltpu.CompilerParams(dimension_semantics=("arbitrary",), vmem_limit_bytes=VMEM_LIMIT_MOE),
        name="moe",
    )(tile_group, tile_ok, src3, src3, xt, g, wg, wu, wd)


def _final_kernel(pos_ref, posn_ref, y_hbm, h_ref, gf_ref, o_ref, ybuf, sem):
    i = pl.program_id(0)
    slot = i % 2
    tm = h_ref.shape[0]

    def issue(index_ref, dst_slot):
        group = 8

        def body(j, c):
            for k in range(group):
                r = j * group + k
                src_row = pl.multiple_of(index_ref[0, 0, r], ROW_SLABS)
                pltpu.make_async_copy(y_hbm.at[pl.ds(src_row, ROW_SLABS), :],
                                      ybuf.at[dst_slot, pl.ds(r * ROW_SLABS, ROW_SLABS), :],
                                      sem.at[dst_slot]).start(priority=k % 2)
            return c
        lax.fori_loop(0, tm // group, body, 0)

    @pl.when(i == 0)
    def _():
        issue(pos_ref, 0)

    @pl.when(i + 1 < pl.num_programs(0))
    def _():
        issue(posn_ref, 1 - slot)

    pltpu.make_async_copy(ybuf.at[slot], ybuf.at[slot], sem.at[slot]).wait()
    y = jnp.concatenate([ybuf[slot, pl.ds(blk, tm, stride=ROW_SLABS), :] for blk in range(ROW_SLABS)], axis=-1)
    h = h_ref[...] + y
    o_ref[...] = h * lax.rsqrt(jnp.mean(h * h, axis=-1, keepdims=True) + EPS) * gf_ref[...]


def _final(pos3, y_sorted, h, gf):
    n_tiles, _, tm = pos3.shape
    last = n_tiles - 1
    row = lambda i: (i, 0)
    return pl.pallas_call(
        _final_kernel,
        grid=(n_tiles,),
        in_specs=[pl.BlockSpec((1, 1, tm), lambda i: (i, 0, 0), memory_space=pltpu.SMEM),
                  pl.BlockSpec((1, 1, tm), lambda i: (jnp.minimum(i + 1, last), 0, 0), memory_space=pltpu.SMEM),
                  pl.BlockSpec(memory_space=pl.ANY),
                  pl.BlockSpec((tm, D_MODEL), row),
                  pl.BlockSpec((1, D_MODEL), lambda i: (0, 0))],
        out_specs=pl.BlockSpec((tm, D_MODEL), row),
        out_shape=jax.ShapeDtypeStruct((n_tiles * tm, D_MODEL), F32),
        scratch_shapes=[pltpu.VMEM((2, tm * ROW_SLABS, LANE), F32), pltpu.SemaphoreType.DMA((2,))],
        compiler_params=_cparams(("arbitrary",)),
        name="final",
    )(pos3, pos3, y_sorted, h, gf)


def _qa_perm():
    new = np.arange(NSA_HEADS * NSA_HD)
    r, g, d = new // LANE, (new % LANE) // NSA_HD, new % NSA_HD
    return (g * NSA_R + r) * NSA_HD + d


def _block_diag2(w):
    z = jnp.zeros_like(w)
    return jnp.concatenate([jnp.concatenate([w, z], axis=-1), jnp.concatenate([z, w], axis=-1)], axis=-2)


def kernel(x, rel_bias, ln_mix, w_in, cmp_pos_k, cmp_pos_v, cmp_k_w1, cmp_k_w2, cmp_v_w1, cmp_v_w2,
           diff_lq1, diff_lk1, diff_lq2, diff_lk2, diff_subln, w_out, ln_ffn,
           router_group_w, router_group_b, router_expert_w, router_expert_b,
           exp_w_gate, exp_w_up, exp_w_down, ln_final):
    B, S, D = x.shape
    T = B * S
    assert D == D_MODEL and S % TQ == 0 and S >= WINDOW and T % TM_MOE == 0 and T % TM_FINAL == 0
    x2 = x.reshape(T, D)
    perm = _qa_perm()

    w = w_in[0]
    c_kc, c_vc, c_ks, c_vs, c_kw, c_vw, c_gt = 512, 640, 768, 896, 1024, 1152, 1280
    c_qb = c_gt + N_GATE
    c_kb, c_vb = c_qb + 512, c_qb + 1024
    col = lambda c, n=LANE: w[:, c:c + n]
    w_tok = jnp.concatenate([col(c_kc), col(c_vc), col(c_ks), col(c_kw), col(c_kb, 512)], axis=1).astype(BF16)
    w_feat = jnp.concatenate([w[:, perm], col(c_qb, 512), col(c_vs), col(c_vw), col(c_vb, 512), col(c_gt, N_GATE),
                              jnp.zeros((D, GATE_ROWS - N_GATE), F32)], axis=1).T.astype(BF16)
    qa, kc, vc, ks, kw, qb, kb, vst, vwt, vbt, gates_t = _in_proj(x2, ln_mix[0][None, :], w_tok, w_feat)

    w1k = _block_diag2(cmp_k_w1[0].reshape(CMP_LEN, NSA_HD, CMP_HIDDEN)).astype(BF16)
    w1v = _block_diag2(cmp_v_w1[0].reshape(CMP_LEN, NSA_HD, CMP_HIDDEN)).astype(BF16)
    w2k = _block_diag2(cmp_k_w2[0]).astype(BF16)
    w2v = _block_diag2(cmp_v_w2[0]).astype(BF16)
    posk = jnp.tile(cmp_pos_k[0], (1, NSA_G))
    posv = jnp.tile(cmp_pos_v[0], (1, NSA_G))
    kcmp, vcmpt = _compress(kc, vc, posk, posv, w1k, w1v, w2k, w2v, B, S)

    near, bias_c = _bias_tiles(rel_bias.T, S)

    n_sel = S // SEL_BLOCK
    nrow = S // CMP_STRIDE
    c_start = np.arange(nrow) * CMP_STRIDE
    s_start = np.arange(n_sel) * SEL_BLOCK
    ov = ((c_start[None, :] <= s_start[:, None] + SEL_BLOCK - 1)
          & (c_start[None, :] + CMP_LEN - 1 >= s_start[:, None])
          & (np.arange(nrow)[None, :] < nrow - 1)).astype(np.float32)
    o_a = _nsa(qa, gates_t, kcmp, vcmpt, bias_c, ks, vst, kw, vwt, near, jnp.asarray(ov, BF16), B, S)

    sub = jnp.tile(diff_subln[0], 2)[:, None]
    o_b = _diff(diff_lq1[0][None, :], diff_lk1[0][None, :], diff_lq2[0][None, :], diff_lk2[0][None, :],
                sub, qb, kb, vbt, near, B, S)

    w_o = jnp.concatenate([w_out[0][:512][perm], w_out[0][512:]], axis=0).astype(BF16)
    n_r = MOE_GROUPS + N_EXPERTS
    wr = jnp.concatenate([router_group_w[0], router_expert_w[0], jnp.zeros((D, LANE - n_r), F32)], axis=1).astype(BF16)
    br = jnp.concatenate([router_group_b[0], router_expert_b[0], jnp.zeros((LANE - n_r,), F32)])[None, :]
    xt, route, counts = _out_proj(x2, o_a, o_b, w_o, ln_ffn[0][None, :], wr, br)

    tm = TM_MOE
    n_tiles = T // tm + MOE_GROUPS
    cnt = counts[0, :MOE_GROUPS].astype(jnp.int32)
    ends = jnp.cumsum((cnt + tm - 1) // tm * tm)
    starts = ends - (cnt + tm - 1) // tm * tm
    pos = starts[route[:, 0].astype(jnp.int32)] + route[:, 1].astype(jnp.int32)
    order = jnp.argsort(pos).astype(jnp.int32)
    row = jnp.arange(n_tiles * tm, dtype=jnp.int32)
    group_of = lambda r: sum((r >= ends[g]).astype(jnp.int32) for g in range(MOE_GROUPS - 1))
    row_group = group_of(row)
    local = row - starts[row_group]
    compact = (jnp.cumsum(cnt) - cnt)[row_group] + local
    src = jnp.where(local < cnt[row_group], order[jnp.clip(compact, 0, T - 1)], 0)
    tile_start = jnp.arange(n_tiles, dtype=jnp.int32) * tm
    tile_group = group_of(tile_start)
    tile_ok = (tile_start < ends[-1]).astype(jnp.int32)

    by_group = lambda a: a[0].astype(BF16).reshape((MOE_GROUPS, EPG) + a.shape[2:])
    y_sorted = _moe(tile_group, tile_ok, src.reshape(n_tiles, 1, tm), xt, ln_ffn[0][None, :],
                    by_group(exp_w_gate), by_group(exp_w_up), by_group(exp_w_down))
    out = _final((pos * ROW_SLABS).reshape(T // TM_FINAL, 1, TM_FINAL), y_sorted, xt, ln_final[None, :])
    return out.reshape(B, S, D)
```
